```python
import math
import jax, jax.numpy as jnp
from jax import lax
import numpy as np

D_MODEL = 1024
BATCH = 16
SEQ = 256
DEPTH = 1
DEC_BATCH = 2
DEC_SEQ = 1024
PAST_LEN = 512

GRID_W = 64
ML_HEADS = 4
ML_HEAD_DIM = D_MODEL // ML_HEADS
ML_WIDTH = ML_HEADS * ML_HEAD_DIM
DA_HEADS = 8
DA_HEAD_DIM = D_MODEL // (2 * DA_HEADS)
DA_V_DIM = 2 * DA_HEAD_DIM
DA_QK_WIDTH = 2 * DA_HEADS * DA_HEAD_DIM
DA_WIDTH = DA_HEADS * DA_V_DIM
D_FF = 2816
N_MOD = 9
N_GATE_COLS = 4 * ML_HEADS
CHUNK = 128
Q_BLOCK = 128
ROPE_BASE = 10000.0
EPS = 1e-6
IN_SIZES = (ML_WIDTH, ML_WIDTH, ML_WIDTH, ML_WIDTH, N_GATE_COLS,
            DA_QK_WIDTH, DA_QK_WIDTH, DA_WIDTH, D_MODEL, D_MODEL)
D_IN = 4 * ML_WIDTH + N_GATE_COLS + 2 * DA_QK_WIDTH + DA_WIDTH + 2 * D_MODEL

kernel_name = 'diffusion_hybrid_mlstm_diffattn_step'


def rmsnorm(x, g):
    xf = x.astype(jnp.float32)
    y = xf * lax.rsqrt(jnp.mean(xf * xf, axis=-1, keepdims=True) + EPS)
    return (y * g.astype(jnp.float32)).astype(x.dtype)


def adaln(cvec, w, b):
    m = jax.nn.silu(cvec) @ w + b
    return m.reshape(cvec.shape[0], N_MOD, D_MODEL)


def swiglu(h, w1, w3, w2):
    return (jax.nn.silu(h @ w1) * (h @ w3)) @ w2


def split_cols(proj):
    idx = []
    acc = 0
    for s in IN_SIZES[:-1]:
        acc += s
        idx.append(acc)
    return jnp.split(proj, idx, axis=-1)


def _rotate(xs, pos):
    n = xs.shape[-1] // 2
    freqs = jnp.power(ROPE_BASE, -jnp.arange(n, dtype=jnp.float32) / n)
    ang = pos.astype(jnp.float32)[:, None] * freqs[None, :]
    cos = jnp.cos(ang)[None, :, None, None, :]
    sin = jnp.sin(ang)[None, :, None, None, :]
    x1, x2 = xs[..., :n], xs[..., n:]
    return jnp.concatenate([x1 * cos - x2 * sin, x2 * cos + x1 * sin], axis=-1)


def rope_2d(x):
    T = x.shape[1]
    rows = T // GRID_W
    row = jnp.repeat(jnp.arange(rows), GRID_W)
    col = jnp.tile(jnp.arange(GRID_W), rows)
    half = DA_HEAD_DIM // 2
    xf = x.astype(jnp.float32)
    out = jnp.concatenate([_rotate(xf[..., :half], row), _rotate(xf[..., half:], col)], axis=-1)
    return out.astype(x.dtype)


def diff_attention(q, k, v, lam):
    B, Tq = q.shape[0], q.shape[1]
    nb = Tq // Q_BLOCK
    kf = k.astype(jnp.float32)
    vf = v.astype(jnp.float32)
    scale = DA_HEAD_DIM ** -0.5
    qb = jnp.moveaxis(q.reshape(B, nb, Q_BLOCK, DA_HEADS, 2, DA_HEAD_DIM), 1, 0)

    def block(qi):
        s = jnp.einsum('bqhcd,bkhcd->bhcqk', qi.astype(jnp.float32), kf) * scale
        pr = jax.nn.softmax(s, axis=-1)
        w = pr[:, :, 0] - lam * pr[:, :, 1]
        return jnp.einsum('bhqk,bkhv->bqhv', w, vf)

    out = lax.map(block, qb)
    return jnp.moveaxis(out, 0, 1).reshape(B, Tq, DA_HEADS, DA_V_DIM)


def mlstm_chunkwise(q, k, v, log_i, log_f, state):
    B, H, T, d = q.shape
    nc = T // CHUNK
    chunk = lambda z: jnp.moveaxis(z.reshape(z.shape[:2] + (nc, CHUNK) + z.shape[3:]), 2, 0)
    tril = jnp.tril(jnp.ones((CHUNK, CHUNK), dtype=bool))
    C0, n0, m0 = (s.astype(jnp.float32) for s in state)

    def step(carry, xs):
        C, n, m = carry
        qc, kc, vc, ic, fc = xs
        b = jnp.cumsum(fc, axis=-1)
        log_d = jnp.where(tril, b[..., :, None] - b[..., None, :] + ic[..., None, :], -jnp.inf)
        a = b + m[..., None]
        m_t = jnp.maximum(a, jnp.max(log_d, axis=-1))
        dmat = jnp.exp(log_d - m_t[..., None])
        inter = jnp.exp(a - m_t)
        s = jnp.einsum('bhtd,bhsd->bhts', qc, kc) * dmat
        num = jnp.einsum('bhts,bhsv->bhtv', s, vc) + inter[..., None] * jnp.einsum('bhtd,bhdv->bhtv', qc, C)
        den = jnp.sum(s, axis=-1) + inter * jnp.einsum('bhtd,bhd->bht', qc, n)
        h = num / jnp.maximum(jnp.abs(den), jnp.exp(-m_t))[..., None]
        b_last = b[..., -1]
        g = b_last[..., None] - b + ic
        m_new = jnp.maximum(b_last + m, jnp.max(g, axis=-1))
        w = jnp.exp(g - m_new[..., None])
        decay = jnp.exp(b_last + m - m_new)
        C_new = decay[..., None, None] * C + jnp.einsum('bhs,bhsd,bhsv->bhdv', w, kc, vc)
        n_new = decay[..., None] * n + jnp.einsum('bhs,bhsd->bhd', w, kc)
        return (C_new, n_new, m_new), h

    final, hs = lax.scan(step, (C0, n0, m0), (chunk(q), chunk(k), chunk(v), chunk(log_i), chunk(log_f)))
    h = jnp.moveaxis(hs, 0, 2).reshape(B, H, T, d)
    return h, final


def token_mixer(h, p, l, ctx):
    B, T, _ = h.shape
    lam_init = 0.8 - 0.6 * math.exp(-0.3 * l)
    mq, mk, mv, mo, mg, dq, dk, dv, gm, gd = split_cols(h @ p['w_in'][l])
    dq = rmsnorm(dq.reshape(B, T, DA_HEADS, 2, DA_HEAD_DIM), p['g_qn'][l])
    dk = rmsnorm(dk.reshape(B, T, DA_HEADS, 2, DA_HEAD_DIM), p['g_kn'][l])
    dv = dv.reshape(B, T, DA_HEADS, DA_V_DIM)
    if ctx is None:
        keys, vals = dk, dv
        zero = (jnp.zeros((B, ML_HEADS, ML_HEAD_DIM, ML_HEAD_DIM), jnp.float32),
                jnp.zeros((B, ML_HEADS, ML_HEAD_DIM), jnp.float32),
                jnp.zeros((B, ML_HEADS), jnp.float32))
        st0_f, st0_b = zero, zero
        q_att = dq
    else:
        k_ctx, v_ctx, st0_f, st0_b = ctx
        q_att = rope_2d(dq)
        keys = jnp.concatenate([rope_2d(dk), k_ctx.astype(dk.dtype)], axis=1)
        vals = jnp.concatenate([dv, v_ctx.astype(dv.dtype)], axis=1)
    f32 = lambda z: z.astype(jnp.float32)
    lam = (jnp.exp(jnp.sum(f32(p['lam_q1'][l]) * f32(p['lam_k1'][l])))
           - jnp.exp(jnp.sum(f32(p['lam_q2'][l]) * f32(p['lam_k2'][l]))) + lam_init)
    att = diff_attention(q_att, keys, vals, lam)
    att = (rmsnorm(att, p['g_sub'][l]) * (1.0 - lam_init)).reshape(B, T, DA_WIDTH).astype(h.dtype)
    heads = lambda z: z.reshape(B, T, ML_HEADS, ML_HEAD_DIM).transpose(0, 2, 1, 3).astype(jnp.float32)
    q = heads(mq) * (ML_HEAD_DIM ** -0.5)
    k = heads(mk)
    v = heads(mv)
    gates = (mg + p['b_gate'][l]).astype(jnp.float32).reshape(B, T, 2, 2, ML_HEADS).transpose(2, 3, 0, 4, 1)
    log_i = gates[:, 0]
    log_f = jax.nn.log_sigmoid(gates[:, 1])
    h_f, st_f = mlstm_chunkwise(q, k, v, log_i[0], log_f[0], st0_f)
    rev = lambda z: jnp.flip(z, axis=2)
    h_b, st_b = mlstm_chunkwise(rev(q), rev(k), rev(v), jnp.flip(log_i[1], -1), jnp.flip(log_f[1], -1), st0_b)
    hm = (h_f + rev(h_b)).transpose(0, 2, 1, 3)
    hm = (rmsnorm(hm, p['g_mh'][l]).reshape(B, T, ML_WIDTH) * jax.nn.sigmoid(mo.astype(jnp.float32))).astype(h.dtype)
    y = jax.nn.sigmoid(gm) * (hm @ p['w_br_m'][l]) + jax.nn.sigmoid(gd) * (att @ p['w_br_d'][l])
    out = y @ p['w_out'][l]
    ctx_out = (dk, dv, st_f, st_b) if ctx is None else None
    return out, ctx_out


def layer(x, mods, l, p, ctx):
    m = [mods[:, j][:, None, :] for j in range(N_MOD)]
    g = p['g_norm'][l]
    hh = rmsnorm(x, g[0]) * (1 + m[1]) + m[0]
    x = x + 0.5 * m[2] * swiglu(hh, p['ffn1_w1'][l], p['ffn1_w3'][l], p['ffn1_w2'][l])
    hh = rmsnorm(x, g[1]) * (1 + m[4]) + m[3]
    y, ctx_out = token_mixer(hh, p, l, ctx)
    x = x + m[5] * y
    hh = rmsnorm(x, g[2]) * (1 + m[7]) + m[6]
    x = x + 0.5 * m[8] * swiglu(hh, p['ffn2_w1'][l], p['ffn2_w3'][l], p['ffn2_w2'][l])
    return x, ctx_out


def setup_inputs(seed: int = 0) -> dict:
    key = jax.random.key(seed)
    ks = iter(jax.random.split(key, 48))
    nrm = lambda shape, s: jax.random.normal(next(ks), shape, jnp.float32) * s
    D, F = D_MODEL, D_FF
    i_b = nrm((DEPTH, 2, 1, ML_HEADS), 0.1)
    f_b = jnp.linspace(3.0, 6.0, ML_HEADS) + nrm((DEPTH, 2, 1, ML_HEADS), 0.1)
    return {
        'x_prompt': nrm((BATCH, SEQ, D), 1.0),
        'x_sample': nrm((DEC_BATCH, DEC_SEQ, D), 1.0),
        'c': nrm((DEC_BATCH, D), 1.0),
        'cache_k': nrm((DEC_BATCH, DEPTH, PAST_LEN, DA_HEADS, 2, DA_HEAD_DIM), 1.0),
        'cache_v': nrm((DEC_BATCH, DEPTH, PAST_LEN, DA_HEADS, DA_V_DIM), 1.0),
        'state_C': nrm((DEC_BATCH, DEPTH, 2, ML_HEADS, ML_HEAD_DIM, ML_HEAD_DIM), 0.05),
        'state_n': nrm((DEC_BATCH, DEPTH, 2, ML_HEADS, ML_HEAD_DIM), 0.1),
        'state_m': nrm((DEC_BATCH, DEPTH, 2, ML_HEADS), 1.0),
        'c_ctx': nrm((D,), 1.0),
        'w_ada': nrm((DEPTH, D, N_MOD * D), 0.5 * D ** -0.5),
        'b_ada': nrm((DEPTH, N_MOD * D), 0.02),
        'g_norm': 1.0 + nrm((DEPTH, 3, D), 0.05),
        'ffn1_w1': nrm((DEPTH, D, F), D ** -0.5),
        'ffn1_w3': nrm((DEPTH, D, F), D ** -0.5),
        'ffn1_w2': nrm((DEPTH, F, D), F ** -0.5),
        'ffn2_w1': nrm((DEPTH, D, F), D ** -0.5),
        'ffn2_w3': nrm((DEPTH, D, F), D ** -0.5),
        'ffn2_w2': nrm((DEPTH, F, D), F ** -0.5),
        'w_in': nrm((DEPTH, D, D_IN), D ** -0.5),
        'b_gate': jnp.concatenate([i_b, f_b], axis=2).reshape(DEPTH, N_GATE_COLS),
        'g_qn': 1.0 + nrm((DEPTH, DA_HEAD_DIM), 0.05),
        'g_kn': 1.0 + nrm((DEPTH, DA_HEAD_DIM), 0.05),
        'lam_q1': nrm((DEPTH, DA_HEAD_DIM), 0.1),
        'lam_k1': nrm((DEPTH, DA_HEAD_DIM), 0.1),
        'lam_q2': nrm((DEPTH, DA_HEAD_DIM), 0.1),
        'lam_k2': nrm((DEPTH, DA_HEAD_DIM), 0.1),
        'g_sub': 1.0 + nrm((DEPTH, DA_V_DIM), 0.05),
        'g_mh': 1.0 + nrm((DEPTH, ML_HEADS, ML_HEAD_DIM), 0.05),
        'w_br_m': nrm((DEPTH, ML_WIDTH, D), ML_WIDTH ** -0.5),
        'w_br_d': nrm((DEPTH, DA_WIDTH, D), DA_WIDTH ** -0.5),
        'w_out': nrm((DEPTH, D, D), D ** -0.5),
    }


def reference(x_prompt, x_sample, c, cache_k, cache_v, state_C, state_n, state_m, c_ctx,
              w_ada, b_ada, g_norm, ffn1_w1, ffn1_w3, ffn1_w2, ffn2_w1, ffn2_w3, ffn2_w2,
              w_in, b_gate, g_qn, g_kn, lam_q1, lam_k1, lam_q2, lam_k2, g_sub, g_mh,
              w_br_m, w_br_d, w_out):
    p = dict(g_norm=g_norm, ffn1_w1=ffn1_w1, ffn1_w3=ffn1_w3, ffn1_w2=ffn1_w2,
             ffn2_w1=ffn2_w1, ffn2_w3=ffn2_w3, ffn2_w2=ffn2_w2, w_in=w_in, b_gate=b_gate,
             g_qn=g_qn, g_kn=g_kn, lam_q1=lam_q1, lam_k1=lam_k1, lam_q2=lam_q2, lam_k2=lam_k2,
             g_sub=g_sub, g_mh=g_mh, w_br_m=w_br_m, w_br_d=w_br_d, w_out=w_out)
    xp, xs = x_prompt, x_sample
    ks_, vs_, Cs, ns, ms = [], [], [], [], []
    for l in range(DEPTH):
        mods_ctx = adaln(c_ctx[None, :], w_ada[l], b_ada[l])
        xp, (k_l, v_l, st_f, st_b) = layer(xp, mods_ctx, l, p, None)
        ks_.append(k_l)
        vs_.append(v_l)
        Cs.append(jnp.stack([st_f[0], st_b[0]], axis=1))
        ns.append(jnp.stack([st_f[1], st_b[1]], axis=1))
        ms.append(jnp.stack([st_f[2], st_b[2]], axis=1))
        mods_lat = adaln(c, w_ada[l], b_ada[l])
        ctx = (cache_k[:, l], cache_v[:, l],
               (state_C[:, l, 0], state_n[:, l, 0], state_m[:, l, 0]),
               (state_C[:, l, 1], state_n[:, l, 1], state_m[:, l, 1]))
        xs, _ = layer(xs, mods_lat, l, p, ctx)
    new_k = jnp.stack(ks_, axis=1)
    new_v = jnp.stack(vs_, axis=1)
    new_C = jnp.stack(Cs, axis=1)
    new_n = jnp.stack(ns, axis=1)
    new_m = jnp.stack(ms, axis=1)
    return (xp, xs, new_k, new_v, new_C, new_n, new_m)
```

```python
import functools
import math

import jax
import jax.numpy as jnp
from jax import lax
from jax.experimental import pallas as pl
from jax.experimental.pallas import tpu as pltpu

F32 = jnp.float32
BF16 = jnp.bfloat16

D_MODEL = 1024
D_FF = 2816
N_MOD = 9
GRID_W = 64
ML_HEADS = 4
ML_HEAD_DIM = 256
DA_HEADS = 8
DA_HEAD_DIM = 64
DA_V_DIM = 128
N_GATE_COLS = 16
CHUNK = 128
ROPE_BASE = 10000.0
EPS = 1e-6

LANES = 128
GATE_PAD = LANES
MXU_DIM = 256
VMEM_LIMIT = 56 * 1024 * 1024

FFN_TM = 512
FFN_TF = 256
PROJ_TM = 256
MERGE_TM = 512
ATT_TQ = 256


def _params(n_axes):
    return pltpu.CompilerParams(dimension_semantics=("arbitrary",) * n_axes,
                                vmem_limit_bytes=VMEM_LIMIT)


def _dot(a, b):
    return jnp.dot(a.astype(BF16), b.astype(BF16), preferred_element_type=F32)


def _dot_nt(a, b):
    return lax.dot_general(a.astype(BF16), b.astype(BF16), (((1,), (1,)), ((), ())),
                           preferred_element_type=F32)


def _dot_tn(a, b):
    return lax.dot_general(a.astype(BF16), b.astype(BF16), (((0,), (0,)), ((), ())),
                           preferred_element_type=F32)


def _sigmoid(x):
    return 1.0 / (1.0 + jnp.exp(-x))


def _log_sigmoid(x):
    return jnp.minimum(x, 0.0) - jnp.log1p(jnp.exp(-jnp.abs(x)))


def _modulated_norm(x, g, shift, scale):
    y = x * lax.rsqrt(jnp.mean(x * x, axis=-1, keepdims=True) + EPS) * g
    return y * (1.0 + scale) + shift


def _resident(shape):
    return pl.BlockSpec(shape, lambda *_: (0,) * len(shape), pipeline_mode=pl.Buffered(1))


def _mods_kernel(c_ref, w_ref, b_ref, o_ref):
    c = c_ref[...]
    o_ref[...] = _dot(c * _sigmoid(c), w_ref[...]) + b_ref[...]


def _mods(cvecs, w_ada, b_ada):
    n = N_MOD * D_MODEL
    tn = D_MODEL
    return pl.pallas_call(
        _mods_kernel,
        grid=(n // tn,),
        in_specs=[pl.BlockSpec((8, D_MODEL), lambda j: (0, 0)),
                  pl.BlockSpec((D_MODEL, tn), lambda j: (0, j)),
                  pl.BlockSpec((1, tn), lambda j: (0, j))],
        out_specs=pl.BlockSpec((8, tn), lambda j: (0, j)),
        out_shape=jax.ShapeDtypeStruct((8, n), F32),
        compiler_params=_params(1),
        name="adaln_mods",
    )(cvecs, w_ada, b_ada.reshape(1, n))


def _ffn_kernel(x_ref, m_ref, g_ref, w1_ref, w3_ref, w2_ref, o_ref, *, base):
    x = x_ref[...]
    hh = _modulated_norm(x, g_ref[...], m_ref[base:base + 1, :], m_ref[base + 1:base + 2, :]).astype(BF16)
    acc = jnp.zeros(x.shape, F32)
    for f in range(D_FF // FFN_TF):
        cols = slice(f * FFN_TF, (f + 1) * FFN_TF)
        a = _dot(hh, w1_ref[:, cols])
        b = _dot(hh, w3_ref[:, cols])
        acc = acc + _dot(a * _sigmoid(a) * b, w2_ref[cols, :])
    o_ref[...] = x + 0.5 * m_ref[base + 2:base + 3, :] * acc


def _ffn(x, mods, mod_index, g, w1, w3, w2, base):
    rows = x.shape[0]
    row = lambda i: (i, 0)
    return pl.pallas_call(
        functools.partial(_ffn_kernel, base=base),
        grid=(rows // FFN_TM,),
        in_specs=[pl.BlockSpec((FFN_TM, D_MODEL), row),
                  pl.BlockSpec((None, N_MOD, D_MODEL), lambda i: (mod_index(i * FFN_TM), 0, 0)),
                  _resident((1, D_MODEL)),
                  _resident((D_MODEL, D_FF)),
                  _resident((D_MODEL, D_FF)),
                  _resident((D_FF, D_MODEL))],
        out_specs=pl.BlockSpec((FFN_TM, D_MODEL), row),
        out_shape=jax.ShapeDtypeStruct((rows, D_MODEL), F32),
        compiler_params=_params(1),
        name="ffn",
    )(x, mods, g.reshape(1, D_MODEL), w1, w3, w2)


def _group_norm64(x, bd, g):
    ss = _dot(x * x, bd)
    return x * lax.rsqrt(ss * (1.0 / DA_HEAD_DIM) + EPS) * g


def _rope(x, cos, sin_signed):
    first = (lax.broadcasted_iota(jnp.int32, x.shape, 1) % 32) < 16
    partner = jnp.where(first, pltpu.roll(x, LANES - 16, 1), pltpu.roll(x, 16, 1))
    return x * cos + partner * sin_signed


def _proj_kernel(*refs, rope):
    (x_ref, m_ref, g_ref, w_ref, wg_ref, bg_ref, bd_ref, gq_ref, gk_ref), refs = refs[:9], refs[9:]
    if rope:
        (cos_ref, sin_ref), refs = refs[:2], refs[2:]
    mq_ref, mk_ref, mv_ref, so_ref, dq_ref, dk_ref, dv_ref, sgm_ref, sgd_ref, gates_ref = refs

    hh = _modulated_norm(x_ref[...], g_ref[...], m_ref[3:4, :], m_ref[4:5, :]).astype(BF16)
    seg = lambda s: _dot(hh, w_ref[:, s * D_MODEL:(s + 1) * D_MODEL])

    mq_ref[...] = (seg(0) * (ML_HEAD_DIM ** -0.5)).astype(mq_ref.dtype)
    mk_ref[...] = seg(1).astype(mk_ref.dtype)
    mv_ref[...] = seg(2).astype(mv_ref.dtype)
    so_ref[...] = _sigmoid(seg(3))
    gates_ref[...] = _dot(hh, wg_ref[...]) + bg_ref[...]

    bd = bd_ref[...]
    for s, g_head_ref, o_ref in ((4, gq_ref, dq_ref), (5, gk_ref, dk_ref)):
        y = seg(s)
        for c in range(D_MODEL // MXU_DIM):
            cols = slice(c * MXU_DIM, (c + 1) * MXU_DIM)
            z = _group_norm64(y[:, cols], bd, g_head_ref[...])
            if rope:
                z = jnp.concatenate(
                    [_rope(z[:, k * LANES:(k + 1) * LANES], cos_ref[...], sin_ref[...])
                     for k in range(MXU_DIM // LANES)], axis=1)
            o_ref[:, cols] = z.astype(o_ref.dtype)

    dv_ref[...] = seg(6).astype(dv_ref.dtype)
    sgm_ref[...] = _sigmoid(seg(7))
    sgd_ref[...] = _sigmoid(seg(8))


def _proj(x, mods, mod_index, g, w_main, w_gate, b_gate, bd, gq, gk, rope_tabs, seq_len):
    rows = x.shape[0]
    rope = rope_tabs is not None
    row = lambda i: (i, 0)
    tile = pl.BlockSpec((PROJ_TM, D_MODEL), row)
    in_specs = [tile,
                pl.BlockSpec((None, N_MOD, D_MODEL), lambda i: (mod_index(i * PROJ_TM), 0, 0)),
                _resident((1, D_MODEL)),
                _resident(w_main.shape),
                _resident(w_gate.shape),
                _resident((1, GATE_PAD)),
                _resident((MXU_DIM, MXU_DIM)),
                _resident((1, MXU_DIM)),
                _resident((1, MXU_DIM))]
    args = [x, mods, g.reshape(1, D_MODEL), w_main, w_gate, b_gate, bd, gq, gk]
    if rope:
        tiles_per_seq = seq_len // PROJ_TM
        tab = pl.BlockSpec((PROJ_TM, LANES), lambda i: (i % tiles_per_seq, 0))
        in_specs += [tab, tab]
        args += list(rope_tabs)
    kv_dtype = BF16 if rope else F32
    dtypes = [BF16, BF16, BF16, F32, BF16, kv_dtype, kv_dtype, F32, F32]
    out_shape = [jax.ShapeDtypeStruct((rows, D_MODEL), dt) for dt in dtypes]
    out_shape.append(jax.ShapeDtypeStruct((rows, GATE_PAD), F32))
    out_specs = [tile] * 9 + [pl.BlockSpec((PROJ_TM, GATE_PAD), row)]
    return pl.pallas_call(
        functools.partial(_proj_kernel, rope=rope),
        grid=(rows // PROJ_TM,),
        in_specs=in_specs,
        out_specs=out_specs,
        out_shape=out_shape,
        compiler_params=_params(1),
        name="mixer_in_proj",
    )(*args)


def _lambda(lam_ref, lam_init):
    lam = lam_ref[...]
    s1 = jnp.sum(lam[0:1, :] * lam[1:2, :], axis=1, keepdims=True)
    s2 = jnp.sum(lam[2:3, :] * lam[3:4, :], axis=1, keepdims=True)
    return jnp.exp(s1) - jnp.exp(s2) + lam_init


def _attn_kernel(*refs, cached, lam_init):
    if cached:
        q_ref, k_ref, v_ref, ck_ref, cv_ref, lam_ref, gs_ref, o_ref = refs
    else:
        q_ref, k_ref, v_ref, lam_ref, gs_ref, o_ref = refs
    lam = _lambda(lam_ref, lam_init)
    scale = DA_HEAD_DIM ** -0.5
    lane = lax.broadcasted_iota(jnp.int32, (1, DA_V_DIM), 1)
    comp_masks = [lane < DA_HEAD_DIM, lane >= DA_HEAD_DIM]
    for h in range(DA_HEADS):
        cols = slice(h * DA_V_DIM, (h + 1) * DA_V_DIM)
        q = q_ref[:, cols].astype(BF16)
        key_sets = [(k_ref[:, cols].astype(BF16), v_ref[:, cols].astype(BF16))]
        if cached:
            key_sets.append((ck_ref[:, cols].astype(BF16), cv_ref[:, cols].astype(BF16)))
        probs = []
        for mask in comp_masks:
            qc = jnp.where(mask, q, jnp.zeros_like(q))
            s = [_dot_nt(qc, k) * scale for k, _ in key_sets]
            mx = functools.reduce(jnp.maximum, [jnp.max(si, axis=1, keepdims=True) for si in s])
            e = [jnp.exp(si - mx) for si in s]
            den = functools.reduce(jnp.add, [jnp.sum(ei, axis=1, keepdims=True) for ei in e])
            inv = 1.0 / den
            probs.append([ei * inv for ei in e])
        out = functools.reduce(
            jnp.add, [_dot(p0 - lam * p1, v) for p0, p1, (_, v) in zip(probs[0], probs[1], key_sets)])
        out = out * lax.rsqrt(jnp.mean(out * out, axis=1, keepdims=True) + EPS) * gs_ref[...]
        o_ref[:, cols] = (out * (1.0 - lam_init)).astype(o_ref.dtype)


def _attn(q, k, v, cache, lam_vecs, g_sub, batch, seq_len, lam_init):
    rows = q.shape[0]
    nq = seq_len // ATT_TQ
    q_spec = pl.BlockSpec((ATT_TQ, D_MODEL), lambda b, i: (b * nq + i, 0))
    kv_spec = pl.BlockSpec((seq_len, D_MODEL), lambda b, i: (b, 0))
    in_specs = [q_spec, kv_spec, kv_spec]
    args = [q, k, v]
    if cache is not None:
        past = cache[0].shape[1]
        c_spec = pl.BlockSpec((None, past, D_MODEL), lambda b, i: (b, 0, 0))
        in_specs += [c_spec, c_spec]
        args += list(cache)
    in_specs += [_resident((4, DA_HEAD_DIM)), _resident((1, DA_V_DIM))]
    args += [lam_vecs, g_sub.reshape(1, DA_V_DIM)]
    return pl.pallas_call(
        functools.partial(_attn_kernel, cached=cache is not None, lam_init=lam_init),
        grid=(batch, nq),
        in_specs=in_specs,
        out_specs=q_spec,
        out_shape=jax.ShapeDtypeStruct((rows, D_MODEL), BF16),
        compiler_params=_params(2),
        name="diff_attention",
    )(*args)


def _mlstm_chunk(q, k, v, ic_row, fc_row, ic_col, fc_col, C, n_row, m, reverse):
    L = q.shape[0]
    t = lax.broadcasted_iota(jnp.int32, (L, L), 0)
    s = lax.broadcasted_iota(jnp.int32, (L, L), 1)
    seen = (s >= t) if reverse else (s <= t)
    seen_t = (t >= s) if reverse else (t <= s)
    b_col = jnp.sum(jnp.where(seen, fc_row, 0.0), axis=1, keepdims=True)
    b_row = jnp.sum(jnp.where(seen_t, fc_col, 0.0), axis=0, keepdims=True)
    b_last = jnp.sum(fc_col, axis=0, keepdims=True)

    log_d = jnp.where(seen, b_col - b_row + ic_row, -jnp.inf)
    a = b_col + m
    m_t = jnp.maximum(a, jnp.max(log_d, axis=1, keepdims=True))
    dmat = jnp.exp(log_d - m_t)
    inter = jnp.exp(a - m_t)
    sc = _dot_nt(q, k) * dmat
    num = _dot(sc, v) + inter * _dot(q, C)
    den = jnp.sum(sc, axis=1, keepdims=True) + inter * jnp.sum(q * n_row, axis=1, keepdims=True)
    h = num * (1.0 / jnp.maximum(jnp.abs(den), jnp.exp(-m_t)))

    g = b_last - b_col + ic_col
    m_new = jnp.maximum(b_last + m, jnp.max(g, axis=0, keepdims=True))
    w = jnp.exp(g - m_new)
    decay = jnp.exp(b_last + m - m_new)
    wk = w * k
    C_new = decay * C + _dot_tn(wk, v)
    n_new = decay * n_row + jnp.sum(wk, axis=0, keepdims=True)
    return h, C_new, n_new, m_new


def _mlstm_kernel(*refs, seq_len, has_state, emit_state):
    q_ref, k_ref, v_ref, so_ref, gc_ref, gr_ref, gmh_ref = refs[:7]
    refs = refs[7:]
    if has_state:
        (c0_ref, n0_ref, m0_ref), refs = refs[:3], refs[3:]
    hm_ref, refs = refs[0], refs[1:]
    if emit_state:
        (c_out_ref, n_out_ref, m_out_ref), refs = refs[:3], refs[3:]
    c_sc, n_sc, m_sc, h_sc = refs

    nc = seq_len // CHUNK
    for d in range(2):
        if has_state:
            c_sc[d] = c0_ref[d]
            n_sc[d] = n0_ref[d]
            m_sc[d] = m0_ref[d][:, 0:1]
        else:
            c_sc[d] = jnp.zeros((ML_HEAD_DIM, ML_HEAD_DIM), F32)
            n_sc[d] = jnp.zeros((1, ML_HEAD_DIM), F32)
            m_sc[d] = jnp.zeros((1, 1), F32)

    for step in range(nc):
        for d in range(2):
            c = step if d == 0 else nc - 1 - step
            rows = slice(c * CHUNK, (c + 1) * CHUNK)
            q = q_ref[rows, :].astype(F32)
            k = k_ref[rows, :].astype(F32)
            v = v_ref[rows, :]
            ic_row = gr_ref[2 * d:2 * d + 1, rows]
            fc_row = _log_sigmoid(gr_ref[2 * d + 1:2 * d + 2, rows])
            ic_col = gc_ref[rows, 2 * d:2 * d + 1]
            fc_col = _log_sigmoid(gc_ref[rows, 2 * d + 1:2 * d + 2])
            h, C_new, n_new, m_new = _mlstm_chunk(q, k, v, ic_row, fc_row, ic_col, fc_col,
                                                  c_sc[d], n_sc[d], m_sc[d], reverse=(d == 1))
            c_sc[d] = C_new
            n_sc[d] = n_new
            m_sc[d] = m_new
            h_sc[d, rows, :] = h

    hsum = h_sc[0] + h_sc[1]
    hn = hsum * lax.rsqrt(jnp.mean(hsum * hsum, axis=1, keepdims=True) + EPS) * gmh_ref[...]
    hm_ref[...] = (hn * so_ref[...]).astype(hm_ref.dtype)
    if emit_state:
        for d in range(2):
            c_out_ref[d] = c_sc[d]
            n_out_ref[d] = n_sc[d]
            m_out_ref[d] = jnp.broadcast_to(m_sc[d], (1, LANES))


def _mlstm(q, k, v, so, gates_col, gates_row, g_mh, state, batch, seq_len, emit_state):
    rows = q.shape[0]
    d = ML_HEAD_DIM
    tile = pl.BlockSpec((seq_len, d), lambda b, h: (b, h))
    in_specs = [tile, tile, tile, tile,
                pl.BlockSpec((None, None, seq_len, 4), lambda b, h: (b, h, 0, 0)),
                pl.BlockSpec((None, None, 4, seq_len), lambda b, h: (b, h, 0, 0)),
                pl.BlockSpec((None, 1, d), lambda b, h: (h, 0, 0))]
    args = [q, k, v, so, gates_col, gates_row, g_mh.reshape(ML_HEADS, 1, d)]
    c_spec = pl.BlockSpec((None, 2, None, d, d), lambda b, h: (b, 0, h, 0, 0))
    n_spec = pl.BlockSpec((None, 2, None, 1, d), lambda b, h: (b, 0, h, 0, 0))
    m_spec = pl.BlockSpec((None, 2, None, 1, LANES), lambda b, h: (b, 0, h, 0, 0))
    if state is not None:
        in_specs += [c_spec, n_spec, m_spec]
        args += list(state)
    out_specs = [tile]
    out_shape = [jax.ShapeDtypeStruct((rows, D_MODEL), BF16)]
    if emit_state:
        out_specs += [c_spec, n_spec, m_spec]
        out_shape += [jax.ShapeDtypeStruct((batch, 2, ML_HEADS, d, d), F32),
                      jax.ShapeDtypeStruct((batch, 2, ML_HEADS, 1, d), F32),
                      jax.ShapeDtypeStruct((batch, 2, ML_HEADS, 1, LANES), F32)]
    return pl.pallas_call(
        functools.partial(_mlstm_kernel, seq_len=seq_len, has_state=state is not None, emit_state=emit_state),
        grid=(batch, ML_HEADS),
        in_specs=in_specs,
        out_specs=out_specs,
        out_shape=out_shape,
        scratch_shapes=[pltpu.VMEM((2, d, d), F32), pltpu.VMEM((2, 1, d), F32),
                        pltpu.VMEM((2, 1, 1), F32), pltpu.VMEM((2, seq_len, d), F32)],
        compiler_params=_params(2),
        name="mlstm",
    )(*args)


def _merge_kernel(x_ref, m_ref, hm_ref, att_ref, sgm_ref, sgd_ref, wm_ref, wd_ref, wo_ref, o_ref):
    y = sgm_ref[...] * _dot(hm_ref[...], wm_ref[...]) + sgd_ref[...] * _dot(att_ref[...], wd_ref[...])
    o_ref[...] = x_ref[...] + m_ref[5:6, :] * _dot(y, wo_ref[...])


def _merge(x, mods, mod_index, hm, att, sgm, sgd, wm, wd, wo):
    rows = x.shape[0]
    tile = pl.BlockSpec((MERGE_TM, D_MODEL), lambda i: (i, 0))
    w_spec = _resident((D_MODEL, D_MODEL))
    return pl.pallas_call(
        _merge_kernel,
        grid=(rows // MERGE_TM,),
        in_specs=[tile, pl.BlockSpec((None, N_MOD, D_MODEL), lambda i: (mod_index(i * MERGE_TM), 0, 0)),
                  tile, tile, tile, tile, w_spec, w_spec, w_spec],
        out_specs=tile,
        out_shape=jax.ShapeDtypeStruct((rows, D_MODEL), F32),
        compiler_params=_params(1),
        name="branch_merge",
    )(x, mods, hm, att, sgm, sgd, wm, wd, wo)


def _rope_tables(seq_len):
    lane = jnp.arange(LANES)
    r = lane % 32
    freqs = jnp.power(ROPE_BASE, -(r % 16).astype(F32) / 16.0)
    tok = jnp.arange(seq_len)
    pos = jnp.where((lane % 64 < 32)[None, :], (tok // GRID_W)[:, None], (tok % GRID_W)[:, None]).astype(F32)
    ang = pos * freqs[None, :]
    sign = jnp.where(r < 16, -1.0, 1.0).astype(F32)
    return jnp.cos(ang), jnp.sin(ang) * sign[None, :]


def _gate_layouts(gates, batch, seq_len):
    g = gates[:, :N_GATE_COLS].reshape(batch, seq_len, 2, 2, ML_HEADS)
    col = g.transpose(0, 4, 1, 2, 3).reshape(batch, ML_HEADS, seq_len, 4)
    return col, col.transpose(0, 1, 3, 2)


def _layer(x, mods, mod_index, w, batch, seq_len, ctx, lam_init):
    x = _ffn(x, mods, mod_index, w["g_norm"][0], w["ffn1_w1"], w["ffn1_w3"], w["ffn1_w2"], base=0)
    rope_tabs = None if ctx is None else _rope_tables(seq_len)
    mq, mk, mv, so, dq, dk, dv, sgm, sgd, gates = _proj(
        x, mods, mod_index, w["g_norm"][1], w["w_main"], w["w_gate"], w["b_gate"], w["bd"], w["g_qn"], w["g_kn"],
        rope_tabs, seq_len)
    cache = None if ctx is None else (ctx[0], ctx[1])
    att = _attn(dq, dk, dv, cache, w["lam"], w["g_sub"], batch, seq_len, lam_init)
    gates_col, gates_row = _gate_layouts(gates, batch, seq_len)
    state = None if ctx is None else ctx[2]
    res = _mlstm(mq, mk, mv, so, gates_col, gates_row, w["g_mh"], state, batch, seq_len, emit_state=ctx is None)
    hm = res[0]
    x = _merge(x, mods, mod_index, hm, att, sgm, sgd, w["w_br_m"], w["w_br_d"], w["w_out"])
    x = _ffn(x, mods, mod_index, w["g_norm"][2], w["ffn2_w1"], w["ffn2_w3"], w["ffn2_w2"], base=6)
    return x, dk, dv, res[1:]


def kernel(x_prompt, x_sample, c, cache_k, cache_v, state_C, state_n, state_m, c_ctx, w_ada, b_ada, g_norm, ffn1_w1, ffn1_w3, ffn1_w2, ffn2_w1, ffn2_w3, ffn2_w2, w_in, b_gate, g_qn, g_kn, lam_q1, lam_k1, lam_q2, lam_k2, g_sub, g_mh, w_br_m, w_br_d, w_out):
    depth = w_ada.shape[0]
    assert depth == 1
    l = 0
    bp, tp, _ = x_prompt.shape
    bs, ts, _ = x_sample.shape
    past = cache_k.shape[2]
    lam_init = 0.8 - 0.6 * math.exp(-0.3 * l)

    cvecs = jnp.concatenate([c_ctx[None, :], c, jnp.zeros((8 - 1 - bs, D_MODEL), F32)], axis=0)
    mods = _mods(cvecs, w_ada[l], b_ada[l]).reshape(8, N_MOD, D_MODEL)

    gate_lo = 4 * D_MODEL
    w_in_l = w_in[l]
    group = jnp.arange(MXU_DIM) // DA_HEAD_DIM
    w = dict(
        g_norm=g_norm[l],
        ffn1_w1=ffn1_w1[l].astype(BF16), ffn1_w3=ffn1_w3[l].astype(BF16), ffn1_w2=ffn1_w2[l].astype(BF16),
        ffn2_w1=ffn2_w1[l].astype(BF16), ffn2_w3=ffn2_w3[l].astype(BF16), ffn2_w2=ffn2_w2[l].astype(BF16),
        w_main=jnp.concatenate([w_in_l[:, :gate_lo], w_in_l[:, gate_lo + N_GATE_COLS:]], axis=1).astype(BF16),
        w_gate=jnp.pad(w_in_l[:, gate_lo:gate_lo + N_GATE_COLS], ((0, 0), (0, GATE_PAD - N_GATE_COLS))).astype(BF16),
        b_gate=jnp.pad(b_gate[l], (0, GATE_PAD - N_GATE_COLS)).reshape(1, GATE_PAD),
        bd=(group[:, None] == group[None, :]).astype(BF16),
        g_qn=jnp.tile(g_qn[l], MXU_DIM // DA_HEAD_DIM).reshape(1, MXU_DIM),
        g_kn=jnp.tile(g_kn[l], MXU_DIM // DA_HEAD_DIM).reshape(1, MXU_DIM),
        lam=jnp.stack([lam_q1[l], lam_k1[l], lam_q2[l], lam_k2[l]]),
        g_sub=g_sub[l], g_mh=g_mh[l],
        w_br_m=w_br_m[l].astype(BF16), w_br_d=w_br_d[l].astype(BF16), w_out=w_out[l].astype(BF16),
    )

    xp, new_k, new_v, (new_c, new_n, new_m) = _layer(
        x_prompt.reshape(bp * tp, D_MODEL), mods, lambda r: 0, w, bp, tp, None, lam_init)

    ctx = (cache_k[:, l].reshape(bs, past, D_MODEL), cache_v[:, l].reshape(bs, past, D_MODEL),
           (state_C[:, l], state_n[:, l].reshape(bs, 2, ML_HEADS, 1, ML_HEAD_DIM),
            jnp.broadcast_to(state_m[:, l][..., None, None], (bs, 2, ML_HEADS, 1, LANES))))
    xs, _, _, _ = _layer(x_sample.reshape(bs * ts, D_MODEL), mods, lambda r: 1 + r // ts, w, bs, ts, ctx, lam_init)

    return (xp.reshape(bp, tp, D_MODEL), xs.reshape(bs, ts, D_MODEL),
            new_k.reshape(bp, 1, tp, DA_HEADS, 2, DA_HEAD_DIM),
            new_v.reshape(bp, 1, tp, DA_HEADS, DA_V_DIM),
            new_c[:, None], new_n.reshape(bp, 1, 2, ML_HEADS, ML_HEAD_DIM),
            new_m[..., 0, 0][:, None])
```

```python
import functools
import math

import jax
import jax.numpy as jnp
from jax import lax
from jax.experimental import pallas as pl
from jax.experimental.pallas import tpu as pltpu

F32 = jnp.float32
BF16 = jnp.bfloat16

D_MODEL = 1024
D_FF = 2816
N_MOD = 9
GRID_W = 64
ML_HEADS = 4
ML_HEAD_DIM = 256
DA_HEADS = 8
DA_HEAD_DIM = 64
DA_V_DIM = 128
N_GATE_COLS = 16
CHUNK = 128
ROPE_BASE = 10000.0
EPS = 1e-6

LANES = 128
GATE_PAD = LANES
MXU_DIM = 256
VMEM_LIMIT = 56 * 1024 * 1024

FFN_TM = 512
FFN_TF = 256
PROJ_TM = 256
MERGE_TM = 512
ATT_TQ = 256


def _params(n_axes):
    return pltpu.CompilerParams(dimension_semantics=("arbitrary",) * n_axes,
                                vmem_limit_bytes=VMEM_LIMIT)


def _dot(a, b):
    return jnp.dot(a.astype(BF16), b.astype(BF16), preferred_element_type=F32)


def _dot_nt(a, b):
    return lax.dot_general(a.astype(BF16), b.astype(BF16), (((1,), (1,)), ((), ())),
                           preferred_element_type=F32)


def _dot_tn(a, b):
    return lax.dot_general(a.astype(BF16), b.astype(BF16), (((0,), (0,)), ((), ())),
                           preferred_element_type=F32)


def _sigmoid(x):
    return 1.0 / (1.0 + jnp.exp(-x))


def _log_sigmoid(x):
    return jnp.minimum(x, 0.0) - jnp.log1p(jnp.exp(-jnp.abs(x)))


def _modulated_norm(x, g, shift, scale):
    y = x * lax.rsqrt(jnp.mean(x * x, axis=-1, keepdims=True) + EPS) * g
    return y * (1.0 + scale) + shift


def _resident(shape):
    return pl.BlockSpec(shape, lambda *_: (0,) * len(shape), pipeline_mode=pl.Buffered(1))


def _mods_kernel(c_ref, w_ref, b_ref, o_ref):
    c = c_ref[...]
    o_ref[...] = _dot(c * _sigmoid(c), w_ref[...]) + b_ref[...]


def _mods(cvecs, w_ada, b_ada):
    n = N_MOD * D_MODEL
    tn = D_MODEL
    return pl.pallas_call(
        _mods_kernel,
        grid=(n // tn,),
        in_specs=[pl.BlockSpec((8, D_MODEL), lambda j: (0, 0)),
                  pl.BlockSpec((D_MODEL, tn), lambda j: (0, j)),
                  pl.BlockSpec((1, tn), lambda j: (0, j))],
        out_specs=pl.BlockSpec((8, tn), lambda j: (0, j)),
        out_shape=jax.ShapeDtypeStruct((8, n), F32),
        compiler_params=_params(1),
        name="adaln_mods",
    )(cvecs, w_ada, b_ada.reshape(1, n))


def _ffn_kernel(x_ref, m_ref, g_ref, w1_ref, w3_ref, w2_ref, o_ref, *, base):
    x = x_ref[...]
    hh = _modulated_norm(x, g_ref[...], m_ref[base:base + 1, :], m_ref[base + 1:base + 2, :]).astype(BF16)
    acc = jnp.zeros(x.shape, F32)
    for f in range(D_FF // FFN_TF):
        cols = slice(f * FFN_TF, (f + 1) * FFN_TF)
        a = _dot(hh, w1_ref[:, cols])
        b = _dot(hh, w3_ref[:, cols])
        acc = acc + _dot(a * _sigmoid(a) * b, w2_ref[cols, :])
    o_ref[...] = x + 0.5 * m_ref[base + 2:base + 3, :] * acc


def _ffn(x, mods, mod_index, g, w1, w3, w2, base):
    rows = x.shape[0]
    row = lambda i: (i, 0)
    return pl.pallas_call(
        functools.partial(_ffn_kernel, base=base),
        grid=(rows // FFN_TM,),
        in_specs=[pl.BlockSpec((FFN_TM, D_MODEL), row),
                  pl.BlockSpec((None, N_MOD, D_MODEL), lambda i: (mod_index(i * FFN_TM), 0, 0)),
                  _resident((1, D_MODEL)),
                  _resident((D_MODEL, D_FF)),
                  _resident((D_MODEL, D_FF)),
                  _resident((D_FF, D_MODEL))],
        out_specs=pl.BlockSpec((FFN_TM, D_MODEL), row),
        out_shape=jax.ShapeDtypeStruct((rows, D_MODEL), F32),
        compiler_params=_params(1),
        name="ffn",
    )(x, mods, g.reshape(1, D_MODEL), w1, w3, w2)


def _group_norm64(x, bd, g):
    ss = _dot(x * x, bd)
    return x * lax.rsqrt(ss * (1.0 / DA_HEAD_DIM) + EPS) * g


def _rope(x, cos, sin_signed):
    first = (lax.broadcasted_iota(jnp.int32, x.shape, 1) % 32) < 16
    partner = jnp.where(first, pltpu.roll(x, LANES - 16, 1), pltpu.roll(x, 16, 1))
    return x * cos + partner * sin_signed


def _proj_kernel(*refs, rope):
    (x_ref, m_ref, g_ref, w_ref, wg_ref, bg_ref, bd_ref, gq_ref, gk_ref), refs = refs[:9], refs[9:]
    if rope:
        (cos_ref, sin_ref), refs = refs[:2], refs[2:]
        mq_ref, mk_ref, mv_ref, so_ref, dq_ref, dk_ref, dv_ref, sgm_ref, sgd_ref, gates_ref = refs
        newk_ref = None
    else:
        mq_ref, mk_ref, mv_ref, so_ref, dq_ref, dk_ref, dv_ref, sgm_ref, sgd_ref, gates_ref, newk_ref = refs

    hh = _modulated_norm(x_ref[...], g_ref[...], m_ref[3:4, :], m_ref[4:5, :]).astype(BF16)
    seg = lambda s: _dot(hh, w_ref[:, s * D_MODEL:(s + 1) * D_MODEL])

    mq_ref[...] = (seg(0) * (ML_HEAD_DIM ** -0.5)).astype(mq_ref.dtype)
    mk_ref[...] = seg(1).astype(mk_ref.dtype)
    mv_ref[...] = seg(2).astype(mv_ref.dtype)
    so_ref[...] = _sigmoid(seg(3))

    gates = _dot(hh, wg_ref[...]) + bg_ref[...]
    col = lax.broadcasted_iota(jnp.int32, gates.shape, 1)
    gates_ref[...] = jnp.where((col // ML_HEADS) % 2 == 1, _log_sigmoid(gates), gates)

    bd = bd_ref[...]
    for s, g_head_ref, o_ref in ((4, gq_ref, dq_ref), (5, gk_ref, dk_ref)):
        y = seg(s)
        for c in range(D_MODEL // MXU_DIM):
            cols = slice(c * MXU_DIM, (c + 1) * MXU_DIM)
            z = _group_norm64(y[:, cols], bd, g_head_ref[...])
            if rope:
                z = jnp.concatenate(
                    [_rope(z[:, k * LANES:(k + 1) * LANES], cos_ref[...], sin_ref[...])
                     for k in range(MXU_DIM // LANES)], axis=1)
            o_ref[:, cols] = z.astype(o_ref.dtype)
            if newk_ref is not None and o_ref is dk_ref:
                groups = MXU_DIM // DA_HEAD_DIM
                newk_ref[:, c * groups:(c + 1) * groups, :] = z.reshape(z.shape[0], groups, DA_HEAD_DIM)

    dv_ref[...] = seg(6).astype(dv_ref.dtype)
    sgm_ref[...] = _sigmoid(seg(7))
    sgd_ref[...] = _sigmoid(seg(8))


def _proj(x, mods, mod_index, g, w_main, w_gate, b_gate, bd, gq, gk, rope_tabs, seq_len):
    rows = x.shape[0]
    rope = rope_tabs is not None
    row = lambda i: (i, 0)
    tile = pl.BlockSpec((PROJ_TM, D_MODEL), row)
    in_specs = [tile,
                pl.BlockSpec((None, N_MOD, D_MODEL), lambda i: (mod_index(i * PROJ_TM), 0, 0)),
                _resident((1, D_MODEL)),
                _resident(w_main.shape),
                _resident(w_gate.shape),
                _resident((1, GATE_PAD)),
                _resident((MXU_DIM, MXU_DIM)),
                _resident((1, MXU_DIM)),
                _resident((1, MXU_DIM))]
    args = [x, mods, g.reshape(1, D_MODEL), w_main, w_gate, b_gate, bd, gq, gk]
    if rope:
        tiles_per_seq = seq_len // PROJ_TM
        tab = pl.BlockSpec((PROJ_TM, LANES), lambda i: (i % tiles_per_seq, 0))
        in_specs += [tab, tab]
        args += list(rope_tabs)
    dtypes = [BF16, BF16, BF16, F32, BF16, BF16, BF16 if rope else F32, F32, F32]
    out_shape = [jax.ShapeDtypeStruct((rows, D_MODEL), dt) for dt in dtypes]
    out_shape.append(jax.ShapeDtypeStruct((rows, GATE_PAD), F32))
    out_specs = [tile] * 9 + [pl.BlockSpec((PROJ_TM, GATE_PAD), row)]
    if not rope:
        n_groups = 2 * DA_HEADS
        out_shape.append(jax.ShapeDtypeStruct((rows, n_groups, DA_HEAD_DIM), F32))
        out_specs.append(pl.BlockSpec((PROJ_TM, n_groups, DA_HEAD_DIM), lambda i: (i, 0, 0)))
    return pl.pallas_call(
        functools.partial(_proj_kernel, rope=rope),
        grid=(rows // PROJ_TM,),
        in_specs=in_specs,
        out_specs=out_specs,
        out_shape=out_shape,
        compiler_params=_params(1),
        name="mixer_in_proj",
    )(*args)


def _lambda(lam_ref, lam_init):
    lam = lam_ref[...]
    s1 = jnp.sum(lam[0:1, :] * lam[1:2, :], axis=1, keepdims=True)
    s2 = jnp.sum(lam[2:3, :] * lam[3:4, :], axis=1, keepdims=True)
    return jnp.exp(s1) - jnp.exp(s2) + lam_init


def _attn_kernel(*refs, cached, lam_init):
    if cached:
        q_ref, k_ref, v_ref, ck_ref, cv_ref, lam_ref, gs_ref, o_ref = refs
    else:
        q_ref, k_ref, v_ref, lam_ref, gs_ref, o_ref = refs
    lam = _lambda(lam_ref, lam_init)
    scale = DA_HEAD_DIM ** -0.5
    lane = lax.broadcasted_iota(jnp.int32, (1, DA_V_DIM), 1)
    comp_masks = [lane < DA_HEAD_DIM, lane >= DA_HEAD_DIM]
    for h in range(DA_HEADS):
        cols = slice(h * DA_V_DIM, (h + 1) * DA_V_DIM)
        q = q_ref[:, cols].astype(BF16) * scale
        key_sets = [(k_ref[:, cols].astype(BF16), v_ref[:, cols].astype(BF16))]
        if cached:
            key_sets.append((ck_ref[:, cols].astype(BF16), cv_ref[:, cols].astype(BF16)))
        branch = []
        for mask in comp_masks:
            qc = jnp.where(mask, q, jnp.zeros_like(q))
            s = [_dot_nt(qc, k) for k, _ in key_sets]
            mx = functools.reduce(jnp.maximum, [jnp.max(si, axis=1, keepdims=True) for si in s])
            e = [jnp.exp(si - mx) for si in s]
            den = functools.reduce(jnp.add, [jnp.sum(ei, axis=1, keepdims=True) for ei in e])
            pv = functools.reduce(jnp.add, [_dot(ei, v) for ei, (_, v) in zip(e, key_sets)])
            branch.append(pv * (1.0 / den))
        out = branch[0] - lam * branch[1]
        out = out * lax.rsqrt(jnp.mean(out * out, axis=1, keepdims=True) + EPS) * gs_ref[...]
        o_ref[:, cols] = (out * (1.0 - lam_init)).astype(o_ref.dtype)


def _attn(q, k, v, cache, lam_vecs, g_sub, batch, seq_len, lam_init):
    rows = q.shape[0]
    nq = seq_len // ATT_TQ
    q_spec = pl.BlockSpec((ATT_TQ, D_MODEL), lambda b, i: (b * nq + i, 0))
    kv_spec = pl.BlockSpec((seq_len, D_MODEL), lambda b, i: (b, 0))
    in_specs = [q_spec, kv_spec, kv_spec]
    args = [q, k, v]
    if cache is not None:
        past = cache[0].shape[1]
        c_spec = pl.BlockSpec((None, past, D_MODEL), lambda b, i: (b, 0, 0))
        in_specs += [c_spec, c_spec]
        args += list(cache)
    in_specs += [_resident((4, DA_HEAD_DIM)), _resident((1, DA_V_DIM))]
    args += [lam_vecs, g_sub.reshape(1, DA_V_DIM)]
    return pl.pallas_call(
        functools.partial(_attn_kernel, cached=cache is not None, lam_init=lam_init),
        grid=(batch, nq),
        in_specs=in_specs,
        out_specs=q_spec,
        out_shape=jax.ShapeDtypeStruct((rows, D_MODEL), BF16),
        compiler_params=_params(2),
        name="diff_attention",
    )(*args)


def _per_chain(fn, a, b):
    return jnp.stack([fn(a[i], b[i]) for i in range(a.shape[0])])


def _mlstm_step(q, k, v, ic_row, fc_row, ic_col, fc_col, C, n_row, m, seen, seen_t):
    b_col = jnp.sum(jnp.where(seen, fc_row, 0.0), axis=2, keepdims=True)
    b_row = jnp.sum(jnp.where(seen_t, fc_col, 0.0), axis=1, keepdims=True)
    b_last = jnp.sum(fc_col, axis=1, keepdims=True)

    log_d = jnp.where(seen, b_col - b_row + ic_row, -jnp.inf)
    a = b_col + m
    m_t = jnp.maximum(a, jnp.max(log_d, axis=2, keepdims=True))
    dmat = jnp.exp(log_d - m_t)
    inter = jnp.exp(a - m_t)
    sc = _per_chain(_dot_nt, q, k) * dmat
    num = _per_chain(_dot, sc, v) + inter * _per_chain(_dot, q, C)
    den = jnp.sum(sc, axis=2, keepdims=True) + inter * jnp.sum(q * n_row, axis=2, keepdims=True)
    h = num * (1.0 / jnp.maximum(jnp.abs(den), jnp.exp(-m_t)))

    g = b_last - b_col + ic_col
    m_new = jnp.maximum(b_last + m, jnp.max(g, axis=1, keepdims=True))
    w = jnp.exp(g - m_new)
    decay = jnp.exp(b_last + m - m_new)
    wk = w * k
    C_new = decay * C + _per_chain(_dot_tn, wk, v)
    n_new = decay * n_row + jnp.sum(wk, axis=1, keepdims=True)
    return h, C_new, n_new, m_new


def _mlstm_kernel(*refs, seq_len, has_state, emit_state):
    q_ref, k_ref, v_ref, so_ref, g_ref, gmh_ref = refs[:6]
    refs = refs[6:]
    if has_state:
        (c0_ref, n0_ref, m0_ref), refs = refs[:3], refs[3:]
    hm_ref, refs = refs[0], refs[1:]
    if emit_state:
        (c_out_ref, n_out_ref, m_out_ref), refs = refs[:3], refs[3:]
    c_sc, n_sc, m_sc, gr_sc, h_sc = refs

    nc = seq_len // CHUNK
    d_head = ML_HEAD_DIM
    chains = [(d, h) for d in range(2) for h in range(ML_HEADS)]
    for i, (d, h) in enumerate(chains):
        if has_state:
            c_sc[i] = c0_ref[d, h]
            n_sc[i] = n0_ref[d, h]
            m_sc[i] = m0_ref[d, h][:, 0:1]
        else:
            c_sc[i] = jnp.zeros((d_head, d_head), F32)
            n_sc[i] = jnp.zeros((1, d_head), F32)
            m_sc[i] = jnp.zeros((1, 1), F32)
    for c in range(nc):
        gr_sc[c] = g_ref[c * CHUNK:(c + 1) * CHUNK, :].T

    t = lax.broadcasted_iota(jnp.int32, (CHUNK, CHUNK), 0)
    s = lax.broadcasted_iota(jnp.int32, (CHUNK, CHUNK), 1)
    seen = jnp.stack([s <= t] * ML_HEADS + [s >= t] * ML_HEADS)
    seen_t = jnp.stack([t <= s] * ML_HEADS + [t >= s] * ML_HEADS)

    def step(c_fwd, c_bwd, rows_of):
        chunk_of = (c_fwd, c_bwd)
        g_cols = [g_ref[rows_of(c), :] for c in chunk_of]
        g_rows = [gr_sc[c] for c in chunk_of]
        ji = [d * 2 * ML_HEADS + h for d, h in chains]
        jf = [j + ML_HEADS for j in ji]
        head_cols = [slice(h * d_head, (h + 1) * d_head) for _, h in chains]
        stack = lambda pick: jnp.stack([pick(i, d) for i, (d, _) in enumerate(chains)])
        hh, C_new, n_new, m_new = _mlstm_step(
            stack(lambda i, d: q_ref[rows_of(chunk_of[d]), head_cols[i]].astype(F32)),
            stack(lambda i, d: k_ref[rows_of(chunk_of[d]), head_cols[i]].astype(F32)),
            stack(lambda i, d: v_ref[rows_of(chunk_of[d]), head_cols[i]]),
            stack(lambda i, d: g_rows[d][ji[i]:ji[i] + 1, :]),
            stack(lambda i, d: g_rows[d][jf[i]:jf[i] + 1, :]),
            stack(lambda i, d: g_cols[d][:, ji[i]:ji[i] + 1]),
            stack(lambda i, d: g_cols[d][:, jf[i]:jf[i] + 1]),
            c_sc[...], n_sc[...], m_sc[...], seen, seen_t)
        c_sc[...] = C_new
        n_sc[...] = n_new
        m_sc[...] = m_new
        for i, (d, _) in enumerate(chains):
            h_sc[d, rows_of(chunk_of[d]), head_cols[i]] = hh[i]

    if nc <= 2:
        for c in range(nc):
            step(c, nc - 1 - c, lambda cc: slice(cc * CHUNK, (cc + 1) * CHUNK))
    else:
        def body(c, carry):
            step(c, nc - 1 - c, lambda cc: pl.ds(pl.multiple_of(cc * CHUNK, CHUNK), CHUNK))
            return carry
        lax.fori_loop(0, nc, body, 0)

    for c in range(nc):
        rows = slice(c * CHUNK, (c + 1) * CHUNK)
        for h in range(ML_HEADS):
            hcols = slice(h * d_head, (h + 1) * d_head)
            hsum = h_sc[0, rows, hcols] + h_sc[1, rows, hcols]
            hn = hsum * lax.rsqrt(jnp.mean(hsum * hsum, axis=1, keepdims=True) + EPS) * gmh_ref[h]
            hm_ref[rows, hcols] = (hn * so_ref[rows, hcols]).astype(hm_ref.dtype)
    if emit_state:
        for i, (d, h) in enumerate(chains):
            c_out_ref[d, h] = c_sc[i]
            n_out_ref[d, h] = n_sc[i]
            m_out_ref[d, h] = jnp.broadcast_to(m_sc[i], (1, LANES))


def _mlstm(q, k, v, so, gates, g_mh, state, batch, seq_len, emit_state):
    rows = q.shape[0]
    d = ML_HEAD_DIM
    tile = pl.BlockSpec((seq_len, D_MODEL), lambda b: (b, 0))
    in_specs = [tile, tile, tile, tile,
                pl.BlockSpec((seq_len, GATE_PAD), lambda b: (b, 0)),
                _resident((ML_HEADS, 1, d))]
    args = [q, k, v, so, gates, g_mh.reshape(ML_HEADS, 1, d)]
    c_spec = pl.BlockSpec((None, 2, ML_HEADS, d, d), lambda b: (b, 0, 0, 0, 0))
    n_spec = pl.BlockSpec((None, 2, ML_HEADS, 1, d), lambda b: (b, 0, 0, 0, 0))
    m_spec = pl.BlockSpec((None, 2, ML_HEADS, 1, LANES), lambda b: (b, 0, 0, 0, 0))
    if state is not None:
        in_specs += [c_spec, n_spec, m_spec]
        args += list(state)
    out_specs = [tile]
    out_shape = [jax.ShapeDtypeStruct((rows, D_MODEL), BF16)]
    if emit_state:
        out_specs += [c_spec, n_spec, m_spec]
        out_shape += [jax.ShapeDtypeStruct((batch, 2, ML_HEADS, d, d), F32),
                      jax.ShapeDtypeStruct((batch, 2, ML_HEADS, 1, d), F32),
                      jax.ShapeDtypeStruct((batch, 2, ML_HEADS, 1, LANES), F32)]
    n_state = 2 * ML_HEADS
    return pl.pallas_call(
        functools.partial(_mlstm_kernel, seq_len=seq_len, has_state=state is not None, emit_state=emit_state),
        grid=(batch,),
        in_specs=in_specs,
        out_specs=out_specs,
        out_shape=out_shape,
        scratch_shapes=[pltpu.VMEM((n_state, d, d), F32), pltpu.VMEM((n_state, 1, d), F32),
                        pltpu.VMEM((n_state, 1, 1), F32),
                        pltpu.VMEM((seq_len // CHUNK, GATE_PAD, CHUNK), F32),
                        pltpu.VMEM((2, seq_len, D_MODEL), F32)],
        compiler_params=_params(1),
        name="mlstm",
    )(*args)


def _merge_kernel(x_ref, m_ref, hm_ref, att_ref, sgm_ref, sgd_ref, wm_ref, wd_ref, wo_ref, o_ref):
    y = sgm_ref[...] * _dot(hm_ref[...], wm_ref[...]) + sgd_ref[...] * _dot(att_ref[...], wd_ref[...])
    o_ref[...] = x_ref[...] + m_ref[5:6, :] * _dot(y, wo_ref[...])


def _merge(x, mods, mod_index, hm, att, sgm, sgd, wm, wd, wo):
    rows = x.shape[0]
    tile = pl.BlockSpec((MERGE_TM, D_MODEL), lambda i: (i, 0))
    w_spec = _resident((D_MODEL, D_MODEL))
    return pl.pallas_call(
        _merge_kernel,
        grid=(rows // MERGE_TM,),
        in_specs=[tile, pl.BlockSpec((None, N_MOD, D_MODEL), lambda i: (mod_index(i * MERGE_TM), 0, 0)),
                  tile, tile, tile, tile, w_spec, w_spec, w_spec],
        out_specs=tile,
        out_shape=jax.ShapeDtypeStruct((rows, D_MODEL), F32),
        compiler_params=_params(1),
        name="branch_merge",
    )(x, mods, hm, att, sgm, sgd, wm, wd, wo)


def _rope_tables(seq_len):
    lane = jnp.arange(LANES)
    r = lane % 32
    freqs = jnp.power(ROPE_BASE, -(r % 16).astype(F32) / 16.0)
    tok = jnp.arange(seq_len)
    pos = jnp.where((lane % 64 < 32)[None, :], (tok // GRID_W)[:, None], (tok % GRID_W)[:, None]).astype(F32)
    ang = pos * freqs[None, :]
    sign = jnp.where(r < 16, -1.0, 1.0).astype(F32)
    return jnp.cos(ang), jnp.sin(ang) * sign[None, :]


def _layer(x, mods, mod_index, w, batch, seq_len, ctx, lam_init):
    x = _ffn(x, mods, mod_index, w["g_norm"][0], w["ffn1_w1"], w["ffn1_w3"], w["ffn1_w2"], base=0)
    rope_tabs = None if ctx is None else _rope_tables(seq_len)
    mq, mk, mv, so, dq, dk, dv, sgm, sgd, gates, *new_k = _proj(
        x, mods, mod_index, w["g_norm"][1], w["w_main"], w["w_gate"], w["b_gate"], w["bd"], w["g_qn"], w["g_kn"],
        rope_tabs, seq_len)
    cache = None if ctx is None else (ctx[0], ctx[1])
    att = _attn(dq, dk, dv, cache, w["lam"], w["g_sub"], batch, seq_len, lam_init)
    state = None if ctx is None else ctx[2]
    res = _mlstm(mq, mk, mv, so, gates, w["g_mh"], state, batch, seq_len, emit_state=ctx is None)
    x = _merge(x, mods, mod_index, res[0], att, sgm, sgd, w["w_br_m"], w["w_br_d"], w["w_out"])
    x = _ffn(x, mods, mod_index, w["g_norm"][2], w["ffn2_w1"], w["ffn2_w3"], w["ffn2_w2"], base=6)
    return x, new_k, dv, res[1:]


def kernel(x_prompt, x_sample, c, cache_k, cache_v, state_C, state_n, state_m, c_ctx, w_ada, b_ada, g_norm, ffn1_w1, ffn1_w3, ffn1_w2, ffn2_w1, ffn2_w3, ffn2_w2, w_in, b_gate, g_qn, g_kn, lam_q1, lam_k1, lam_q2, lam_k2, g_sub, g_mh, w_br_m, w_br_d, w_out):
    depth = w_ada.shape[0]
    assert depth == 1
    l = 0
    bp, tp, _ = x_prompt.shape
    bs, ts, _ = x_sample.shape
    past = cache_k.shape[2]
    lam_init = 0.8 - 0.6 * math.exp(-0.3 * l)

    cvecs = jnp.concatenate([c_ctx[None, :], c, jnp.zeros((8 - 1 - bs, D_MODEL), F32)], axis=0)
    mods = _mods(cvecs, w_ada[l], b_ada[l]).reshape(8, N_MOD, D_MODEL)

    gate_lo = 4 * D_MODEL
    w_in_l = w_in[l]
    group = jnp.arange(MXU_DIM) // DA_HEAD_DIM
    w = dict(
        g_norm=g_norm[l],
        ffn1_w1=ffn1_w1[l].astype(BF16), ffn1_w3=ffn1_w3[l].astype(BF16), ffn1_w2=ffn1_w2[l].astype(BF16),
        ffn2_w1=ffn2_w1[l].astype(BF16), ffn2_w3=ffn2_w3[l].astype(BF16), ffn2_w2=ffn2_w2[l].astype(BF16),
        w_main=jnp.concatenate([w_in_l[:, :gate_lo], w_in_l[:, gate_lo + N_GATE_COLS:]], axis=1).astype(BF16),
        w_gate=jnp.pad(w_in_l[:, gate_lo:gate_lo + N_GATE_COLS], ((0, 0), (0, GATE_PAD - N_GATE_COLS))).astype(BF16),
        b_gate=jnp.pad(b_gate[l], (0, GATE_PAD - N_GATE_COLS)).reshape(1, GATE_PAD),
        bd=(group[:, None] == group[None, :]).astype(BF16),
        g_qn=jnp.tile(g_qn[l], MXU_DIM // DA_HEAD_DIM).reshape(1, MXU_DIM),
        g_kn=jnp.tile(g_kn[l], MXU_DIM // DA_HEAD_DIM).reshape(1, MXU_DIM),
        lam=jnp.stack([lam_q1[l], lam_k1[l], lam_q2[l], lam_k2[l]]),
        g_sub=g_sub[l], g_mh=g_mh[l],
        w_br_m=w_br_m[l].astype(BF16), w_br_d=w_br_d[l].astype(BF16), w_out=w_out[l].astype(BF16),
    )

    xp, (new_k,), new_v, (new_c, new_n, new_m) = _layer(
        x_prompt.reshape(bp * tp, D_MODEL), mods, lambda r: 0, w, bp, tp, None, lam_init)

    ctx = (cache_k[:, l].reshape(bs, past, D_MODEL), cache_v[:, l].reshape(bs, past, D_MODEL),
           (state_C[:, l], state_n[:, l].reshape(bs, 2, ML_HEADS, 1, ML_HEAD_DIM),
            jnp.broadcast_to(state_m[:, l][..., None, None], (bs, 2, ML_HEADS, 1, LANES))))
    xs, _, _, _ = _layer(x_sample.reshape(bs * ts, D_MODEL), mods, lambda r: 1 + r // ts, w, bs, ts, ctx, lam_init)

    return (xp.reshape(bp, tp, D_MODEL), xs.reshape(bs, ts, D_MODEL),
            new_k.reshape(bp, 1, tp, DA_HEADS, 2, DA_HEAD_DIM),
            new_v.reshape(bp, 1, tp, DA_HEADS, DA_V_DIM),
            new_c[:, None], new_n.reshape(bp, 1, 2, ML_HEADS, ML_HEAD_DIM),
            new_m[..., 0, 0][:, None])
```

```python
import functools
import math

import jax
import jax.numpy as jnp
from jax import lax
from jax.experimental import pallas as pl
from jax.experimental.pallas import tpu as pltpu

F32 = jnp.float32
BF16 = jnp.bfloat16

D_MODEL = 1024
D_FF = 2816
N_MOD = 9
GRID_W = 64
ML_HEADS = 4
ML_HEAD_DIM = 256
DA_HEADS = 8
DA_HEAD_DIM = 64
DA_V_DIM = 128
N_GATE_COLS = 16
CHUNK = 128
ROPE_BASE = 10000.0
EPS = 1e-6

LANES = 128
GATE_PAD = LANES
MXU_DIM = 256
VMEM_LIMIT = 56 * 1024 * 1024

FFN_TM = 512
FFN_TF = 256
PROJ_TM = 256
MERGE_TM = 512
ATT_TQ = 256


def _params(n_axes):
    return pltpu.CompilerParams(dimension_semantics=("arbitrary",) * n_axes,
                                vmem_limit_bytes=VMEM_LIMIT)


def _dot(a, b):
    return jnp.dot(a.astype(BF16), b.astype(BF16), preferred_element_type=F32)


def _dot_nt(a, b):
    return lax.dot_general(a.astype(BF16), b.astype(BF16), (((1,), (1,)), ((), ())),
                           preferred_element_type=F32)


def _dot_tn(a, b):
    return lax.dot_general(a.astype(BF16), b.astype(BF16), (((0,), (0,)), ((), ())),
                           preferred_element_type=F32)


def _sigmoid(x):
    return 1.0 / (1.0 + jnp.exp(-x))


def _log_sigmoid(x):
    return jnp.minimum(x, 0.0) - jnp.log1p(jnp.exp(-jnp.abs(x)))


def _modulated_norm(x, g, shift, scale):
    y = x * lax.rsqrt(jnp.mean(x * x, axis=-1, keepdims=True) + EPS) * g
    return y * (1.0 + scale) + shift


def _resident(shape):
    return pl.BlockSpec(shape, lambda *_: (0,) * len(shape), pipeline_mode=pl.Buffered(1))


def _mods_kernel(c_ref, w_ref, b_ref, o_ref):
    c = c_ref[...]
    o_ref[...] = _dot(c * _sigmoid(c), w_ref[...]) + b_ref[...]


def _mods(cvecs, w_ada, b_ada):
    n = N_MOD * D_MODEL
    tn = D_MODEL
    return pl.pallas_call(
        _mods_kernel,
        grid=(n // tn,),
        in_specs=[pl.BlockSpec((8, D_MODEL), lambda j: (0, 0)),
                  pl.BlockSpec((D_MODEL, tn), lambda j: (0, j)),
                  pl.BlockSpec((1, tn), lambda j: (0, j))],
        out_specs=pl.BlockSpec((8, tn), lambda j: (0, j)),
        out_shape=jax.ShapeDtypeStruct((8, n), F32),
        compiler_params=_params(1),
        name="adaln_mods",
    )(cvecs, w_ada, b_ada.reshape(1, n))


def _ffn_kernel(x_ref, m_ref, g_ref, w1_ref, w3_ref, w2_ref, o_ref, w1_sc, w3_sc, w2_sc, hh_sc, acc_sc, *, base):
    g = pl.program_id(0)
    nf = D_FF // FFN_TF
    norm = lambda x: _modulated_norm(x, g_ref[...], m_ref[base:base + 1, :], m_ref[base + 1:base + 2, :]).astype(BF16)
    finish = lambda x, acc: x + 0.5 * m_ref[base + 2:base + 3, :] * acc

    def tile(hh, f):
        a = _dot(hh, w1_sc[f])
        b = _dot(hh, w3_sc[f])
        return _dot(a * _sigmoid(a) * b, w2_sc[f])

    @pl.when(g < nf)
    def _():
        w1_sc[g] = w1_ref[...].astype(BF16)
        w3_sc[g] = w3_ref[...].astype(BF16)
        w2_sc[g] = w2_ref[...].astype(BF16)

        @pl.when(g == 0)
        def _():
            hh_sc[...] = norm(x_ref[...])
            acc_sc[...] = jnp.zeros(acc_sc.shape, F32)

        acc_sc[...] += tile(hh_sc[...], g)

        @pl.when(g == nf - 1)
        def _():
            o_ref[...] = finish(x_ref[...], acc_sc[...])

    @pl.when(g >= nf)
    def _():
        x = x_ref[...]
        hh = norm(x)
        acc = jnp.zeros(x.shape, F32)
        for f in range(nf):
            acc = acc + tile(hh, f)
        o_ref[...] = finish(x, acc)


def _ffn(x, mods, mod_index, g, w1, w3, w2, base):
    rows = x.shape[0]
    nf = D_FF // FFN_TF
    row_tile = lambda s: jnp.maximum(s - (nf - 1), 0)
    f_tile = lambda s: jnp.minimum(s, nf - 1)
    row = lambda s: (row_tile(s), 0)
    return pl.pallas_call(
        functools.partial(_ffn_kernel, base=base),
        grid=(nf - 1 + rows // FFN_TM,),
        in_specs=[pl.BlockSpec((FFN_TM, D_MODEL), row),
                  pl.BlockSpec((None, N_MOD, D_MODEL), lambda s: (mod_index(row_tile(s) * FFN_TM), 0, 0)),
                  _resident((1, D_MODEL)),
                  pl.BlockSpec((D_MODEL, FFN_TF), lambda s: (0, f_tile(s))),
                  pl.BlockSpec((D_MODEL, FFN_TF), lambda s: (0, f_tile(s))),
                  pl.BlockSpec((FFN_TF, D_MODEL), lambda s: (f_tile(s), 0))],
        out_specs=pl.BlockSpec((FFN_TM, D_MODEL), row),
        out_shape=jax.ShapeDtypeStruct((rows, D_MODEL), F32),
        scratch_shapes=[pltpu.VMEM((nf, D_MODEL, FFN_TF), BF16), pltpu.VMEM((nf, D_MODEL, FFN_TF), BF16),
                        pltpu.VMEM((nf, FFN_TF, D_MODEL), BF16),
                        pltpu.VMEM((FFN_TM, D_MODEL), BF16), pltpu.VMEM((FFN_TM, D_MODEL), F32)],
        compiler_params=_params(1),
        name="ffn",
    )(x, mods, g.reshape(1, D_MODEL), w1, w3, w2)


def _group_norm64(x, bd, g):
    ss = _dot(x * x, bd)
    return x * lax.rsqrt(ss * (1.0 / DA_HEAD_DIM) + EPS) * g


def _rope(x, cos, sin_signed):
    first = (lax.broadcasted_iota(jnp.int32, x.shape, 1) % 32) < 16
    partner = jnp.where(first, pltpu.roll(x, LANES - 16, 1), pltpu.roll(x, 16, 1))
    return x * cos + partner * sin_signed


N_SEG = 9
GATE_LO = 4 * D_MODEL


def _seg_start(s):
    return s * D_MODEL + N_GATE_COLS * (s >= 4)


def _proj_kernel(*refs, rope):
    (x_ref, m_ref, g_ref, wt_ref, wgt_ref, bg_ref, bd_ref, gq_ref, gk_ref), refs = refs[:9], refs[9:]
    if rope:
        (cos_ref, sin_ref), refs = refs[:2], refs[2:]
        newk_ref = None
    else:
        newk_ref, refs = refs[10], refs[:10] + refs[11:]
    (mq_ref, mk_ref, mv_ref, so_ref, dq_ref, dk_ref, dv_ref, sgm_ref, sgd_ref, gates_ref,
     w_sc, wg_sc, hh_sc) = refs
    step = pl.program_id(0)
    norm = lambda: _modulated_norm(x_ref[...], g_ref[...], m_ref[3:4, :], m_ref[4:5, :]).astype(BF16)

    def gates(hh):
        y = _dot_nt(hh, wg_sc[...]) + bg_ref[...]
        col = lax.broadcasted_iota(jnp.int32, y.shape, 1)
        gates_ref[...] = jnp.where((col // ML_HEADS) % 2 == 1, _log_sigmoid(y), y)

    def qk_segment(y, g_head_ref, o_ref, cache_ref):
        bd = bd_ref[...]
        for c in range(D_MODEL // MXU_DIM):
            cols = slice(c * MXU_DIM, (c + 1) * MXU_DIM)
            z = _group_norm64(y[:, cols], bd, g_head_ref[...])
            if rope:
                z = jnp.concatenate(
                    [_rope(z[:, k * LANES:(k + 1) * LANES], cos_ref[...], sin_ref[...])
                     for k in range(MXU_DIM // LANES)], axis=1)
            o_ref[:, cols] = z.astype(o_ref.dtype)
            if cache_ref is not None:
                groups = MXU_DIM // DA_HEAD_DIM
                cache_ref[:, c * groups:(c + 1) * groups, :] = z.reshape(z.shape[0], groups, DA_HEAD_DIM)

    def segment(s, hh):
        y = _dot_nt(hh, w_sc[s])
        if s == 0:
            mq_ref[...] = (y * (ML_HEAD_DIM ** -0.5)).astype(mq_ref.dtype)
        elif s == 4:
            qk_segment(y, gq_ref, dq_ref, None)
        elif s == 5:
            qk_segment(y, gk_ref, dk_ref, newk_ref)
        elif s in (3, 7, 8):
            {3: so_ref, 7: sgm_ref, 8: sgd_ref}[s][...] = _sigmoid(y)
        else:
            o_ref = {1: mk_ref, 2: mv_ref, 6: dv_ref}[s]
            o_ref[...] = y.astype(o_ref.dtype)

    for s in range(N_SEG):
        @pl.when(step == s)
        def _(s=s):
            w_sc[s] = wt_ref[...].astype(BF16)
            if s == 0:
                wg_sc[...] = jnp.zeros(wg_sc.shape, BF16)
                wg_sc[0:N_GATE_COLS, :] = wgt_ref[...].astype(BF16)
                hh_sc[...] = norm()
                gates(hh_sc[...])
            segment(s, hh_sc[...])

    @pl.when(step >= N_SEG)
    def _():
        hh = norm()
        gates(hh)
        for s in range(N_SEG):
            segment(s, hh)


def _proj(x, mods, mod_index, g, w_in_t, b_gate, bd, gq, gk, rope_tabs, seq_len):
    rows = x.shape[0]
    rope = rope_tabs is not None
    row_tile = lambda s: jnp.maximum(s - (N_SEG - 1), 0)
    row = lambda s: (row_tile(s), 0)
    tile = pl.BlockSpec((PROJ_TM, D_MODEL), row)
    elem = lambda n: (pl.Element(n), pl.Element(D_MODEL))
    seg_row = lambda s: pl.multiple_of(_seg_start(jnp.minimum(s, N_SEG - 1)), N_GATE_COLS)
    in_specs = [tile,
                pl.BlockSpec((None, N_MOD, D_MODEL), lambda s: (mod_index(row_tile(s) * PROJ_TM), 0, 0)),
                _resident((1, D_MODEL)),
                pl.BlockSpec(elem(D_MODEL), lambda s: (seg_row(s), 0)),
                pl.BlockSpec(elem(N_GATE_COLS), lambda s: (GATE_LO, 0)),
                _resident((1, GATE_PAD)),
                _resident((MXU_DIM, MXU_DIM)),
                _resident((1, MXU_DIM)),
                _resident((1, MXU_DIM))]
    args = [x, mods, g.reshape(1, D_MODEL), w_in_t, w_in_t, b_gate, bd, gq, gk]
    if rope:
        tiles_per_seq = seq_len // PROJ_TM
        tab = pl.BlockSpec((PROJ_TM, LANES), lambda s: (row_tile(s) % tiles_per_seq, 0))
        in_specs += [tab, tab]
        args += list(rope_tabs)
    dtypes = [BF16, BF16, BF16, F32, BF16, BF16, BF16 if rope else F32, F32, F32]
    out_shape = [jax.ShapeDtypeStruct((rows, D_MODEL), dt) for dt in dtypes]
    out_shape.append(jax.ShapeDtypeStruct((rows, GATE_PAD), F32))
    out_specs = [tile] * 9 + [pl.BlockSpec((PROJ_TM, GATE_PAD), row)]
    if not rope:
        n_groups = 2 * DA_HEADS
        out_shape.append(jax.ShapeDtypeStruct((rows, n_groups, DA_HEAD_DIM), F32))
        out_specs.append(pl.BlockSpec((PROJ_TM, n_groups, DA_HEAD_DIM), lambda s: (row_tile(s), 0, 0)))
    return pl.pallas_call(
        functools.partial(_proj_kernel, rope=rope),
        grid=(N_SEG - 1 + rows // PROJ_TM,),
        in_specs=in_specs,
        out_specs=out_specs,
        out_shape=out_shape,
        scratch_shapes=[pltpu.VMEM((N_SEG, D_MODEL, D_MODEL), BF16), pltpu.VMEM((GATE_PAD, D_MODEL), BF16),
                        pltpu.VMEM((PROJ_TM, D_MODEL), BF16)],
        compiler_params=_params(1),
        name="mixer_in_proj",
    )(*args)


def _lambda(lam_ref, lam_init):
    lam = lam_ref[...]
    s1 = jnp.sum(lam[0:1, :] * lam[1:2, :], axis=1, keepdims=True)
    s2 = jnp.sum(lam[2:3, :] * lam[3:4, :], axis=1, keepdims=True)
    return jnp.exp(s1) - jnp.exp(s2) + lam_init


def _attn_kernel(*refs, cached, lam_init):
    if cached:
        q_ref, k_ref, v_ref, ck_ref, cv_ref, lam_ref, gs_ref, o_ref = refs
    else:
        q_ref, k_ref, v_ref, lam_ref, gs_ref, o_ref = refs
    lam = _lambda(lam_ref, lam_init)
    scale = DA_HEAD_DIM ** -0.5
    lane = lax.broadcasted_iota(jnp.int32, (1, DA_V_DIM), 1)
    comp_masks = [lane < DA_HEAD_DIM, lane >= DA_HEAD_DIM]
    for h in range(DA_HEADS):
        cols = slice(h * DA_V_DIM, (h + 1) * DA_V_DIM)
        q = q_ref[:, cols].astype(BF16) * scale
        key_sets = [(k_ref[:, cols].astype(BF16), v_ref[:, cols].astype(BF16))]
        if cached:
            key_sets.append((ck_ref[:, cols].astype(BF16), cv_ref[:, cols].astype(BF16)))
        branch = []
        for mask in comp_masks:
            qc = jnp.where(mask, q, jnp.zeros_like(q))
            s = [_dot_nt(qc, k) for k, _ in key_sets]
            mx = functools.reduce(jnp.maximum, [jnp.max(si, axis=1, keepdims=True) for si in s])
            e = [jnp.exp(si - mx) for si in s]
            den = functools.reduce(jnp.add, [jnp.sum(ei, axis=1, keepdims=True) for ei in e])
            pv = functools.reduce(jnp.add, [_dot(ei, v) for ei, (_, v) in zip(e, key_sets)])
            branch.append(pv * (1.0 / den))
        out = branch[0] - lam * branch[1]
        out = out * lax.rsqrt(jnp.mean(out * out, axis=1, keepdims=True) + EPS) * gs_ref[...]
        o_ref[:, cols] = (out * (1.0 - lam_init)).astype(o_ref.dtype)


def _attn(q, k, v, cache, lam_vecs, g_sub, batch, seq_len, lam_init):
    rows = q.shape[0]
    nq = seq_len // ATT_TQ
    q_spec = pl.BlockSpec((ATT_TQ, D_MODEL), lambda b, i: (b * nq + i, 0))
    kv_spec = pl.BlockSpec((seq_len, D_MODEL), lambda b, i: (b, 0))
    in_specs = [q_spec, kv_spec, kv_spec]
    args = [q, k, v]
    if cache is not None:
        past = cache[0].shape[1]
        c_spec = pl.BlockSpec((None, past, D_MODEL), lambda b, i: (b, 0, 0))
        in_specs += [c_spec, c_spec]
        args += list(cache)
    in_specs += [_resident((4, DA_HEAD_DIM)), _resident((1, DA_V_DIM))]
    args += [lam_vecs, g_sub.reshape(1, DA_V_DIM)]
    return pl.pallas_call(
        functools.partial(_attn_kernel, cached=cache is not None, lam_init=lam_init),
        grid=(batch, nq),
        in_specs=in_specs,
        out_specs=q_spec,
        out_shape=jax.ShapeDtypeStruct((rows, D_MODEL), BF16),
        compiler_params=_params(2),
        name="diff_attention",
    )(*args)


def _per_chain(fn, a, b):
    return jnp.stack([fn(a[i], b[i]) for i in range(a.shape[0])])


def _mlstm_step(q, k, v, ic_row, fc_row, ic_col, fc_col, C, n_row, m, seen, seen_t):
    b_col = jnp.sum(jnp.where(seen, fc_row, 0.0), axis=2, keepdims=True)
    b_row = jnp.sum(jnp.where(seen_t, fc_col, 0.0), axis=1, keepdims=True)
    b_last = jnp.sum(fc_col, axis=1, keepdims=True)

    log_d = jnp.where(seen, b_col - b_row + ic_row, -jnp.inf)
    a = b_col + m
    m_t = jnp.maximum(a, jnp.max(log_d, axis=2, keepdims=True))
    dmat = jnp.exp(log_d - m_t)
    inter = jnp.exp(a - m_t)
    sc = _per_chain(_dot_nt, q, k) * dmat
    num = _per_chain(_dot, sc, v) + inter * _per_chain(_dot, q, C)
    den = jnp.sum(sc, axis=2, keepdims=True) + inter * jnp.sum(q * n_row, axis=2, keepdims=True)
    h = num * (1.0 / jnp.maximum(jnp.abs(den), jnp.exp(-m_t)))

    g = b_last - b_col + ic_col
    m_new = jnp.maximum(b_last + m, jnp.max(g, axis=1, keepdims=True))
    w = jnp.exp(g - m_new)
    decay = jnp.exp(b_last + m - m_new)
    wk = w * k
    C_new = decay * C + _per_chain(_dot_tn, wk, v)
    n_new = decay * n_row + jnp.sum(wk, axis=1, keepdims=True)
    return h, C_new, n_new, m_new


def _mlstm_kernel(*refs, seq_len, has_state, emit_state):
    q_ref, k_ref, v_ref, so_ref, g_ref, gmh_ref = refs[:6]
    refs = refs[6:]
    if has_state:
        (c0_ref, n0_ref, m0_ref), refs = refs[:3], refs[3:]
    hm_ref, refs = refs[0], refs[1:]
    if emit_state:
        (c_out_ref, n_out_ref, m_out_ref), refs = refs[:3], refs[3:]
    c_sc, n_sc, m_sc, gr_sc, h_sc = refs

    nc = seq_len // CHUNK
    d_head = ML_HEAD_DIM
    chains = [(d, h) for d in range(2) for h in range(ML_HEADS)]
    for i, (d, h) in enumerate(chains):
        if has_state:
            c_sc[i] = c0_ref[d, h]
            n_sc[i] = n0_ref[d, h]
            m_sc[i] = m0_ref[d, h][:, 0:1]
        else:
            c_sc[i] = jnp.zeros((d_head, d_head), F32)
            n_sc[i] = jnp.zeros((1, d_head), F32)
            m_sc[i] = jnp.zeros((1, 1), F32)
    for c in range(nc):
        gr_sc[c] = g_ref[c * CHUNK:(c + 1) * CHUNK, :].T

    t = lax.broadcasted_iota(jnp.int32, (CHUNK, CHUNK), 0)
    s = lax.broadcasted_iota(jnp.int32, (CHUNK, CHUNK), 1)
    seen = jnp.stack([s <= t] * ML_HEADS + [s >= t] * ML_HEADS)
    seen_t = jnp.stack([t <= s] * ML_HEADS + [t >= s] * ML_HEADS)

    def step(c_fwd, c_bwd, rows_of):
        chunk_of = (c_fwd, c_bwd)
        g_cols = [g_ref[rows_of(c), :] for c in chunk_of]
        g_rows = [gr_sc[c] for c in chunk_of]
        ji = [d * 2 * ML_HEADS + h for d, h in chains]
        jf = [j + ML_HEADS for j in ji]
        head_cols = [slice(h * d_head, (h + 1) * d_head) for _, h in chains]
        stack = lambda pick: jnp.stack([pick(i, d) for i, (d, _) in enumerate(chains)])
        hh, C_new, n_new, m_new = _mlstm_step(
            stack(lambda i, d: q_ref[rows_of(chunk_of[d]), head_cols[i]].astype(F32)),
            stack(lambda i, d: k_ref[rows_of(chunk_of[d]), head_cols[i]].astype(F32)),
            stack(lambda i, d: v_ref[rows_of(chunk_of[d]), head_cols[i]]),
            stack(lambda i, d: g_rows[d][ji[i]:ji[i] + 1, :]),
            stack(lambda i, d: g_rows[d][jf[i]:jf[i] + 1, :]),
            stack(lambda i, d: g_cols[d][:, ji[i]:ji[i] + 1]),
            stack(lambda i, d: g_cols[d][:, jf[i]:jf[i] + 1]),
            c_sc[...], n_sc[...], m_sc[...], seen, seen_t)
        c_sc[...] = C_new
        n_sc[...] = n_new
        m_sc[...] = m_new
        for i, (d, _) in enumerate(chains):
            h_sc[d, rows_of(chunk_of[d]), head_cols[i]] = hh[i]

    if nc <= 2:
        for c in range(nc):
            step(c, nc - 1 - c, lambda cc: slice(cc * CHUNK, (cc + 1) * CHUNK))
    else:
        def body(c, carry):
            step(c, nc - 1 - c, lambda cc: pl.ds(pl.multiple_of(cc * CHUNK, CHUNK), CHUNK))
            return carry
        lax.fori_loop(0, nc, body, 0)

    for c in range(nc):
        rows = slice(c * CHUNK, (c + 1) * CHUNK)
        for h in range(ML_HEADS):
            hcols = slice(h * d_head, (h + 1) * d_head)
            hsum = h_sc[0, rows, hcols] + h_sc[1, rows, hcols]
            hn = hsum * lax.rsqrt(jnp.mean(hsum * hsum, axis=1, keepdims=True) + EPS) * gmh_ref[h]
            hm_ref[rows, hcols] = (hn * so_ref[rows, hcols]).astype(hm_ref.dtype)
    if emit_state:
        for i, (d, h) in enumerate(chains):
            c_out_ref[d, h] = c_sc[i]
            n_out_ref[d, h] = n_sc[i]
            m_out_ref[d, h] = jnp.broadcast_to(m_sc[i], (1, LANES))


def _mlstm(q, k, v, so, gates, g_mh, state, batch, seq_len, emit_state):
    rows = q.shape[0]
    d = ML_HEAD_DIM
    tile = pl.BlockSpec((seq_len, D_MODEL), lambda b: (b, 0))
    in_specs = [tile, tile, tile, tile,
                pl.BlockSpec((seq_len, GATE_PAD), lambda b: (b, 0)),
                _resident((ML_HEADS, 1, d))]
    args = [q, k, v, so, gates, g_mh.reshape(ML_HEADS, 1, d)]
    c_spec = pl.BlockSpec((None, 2, ML_HEADS, d, d), lambda b: (b, 0, 0, 0, 0))
    n_spec = pl.BlockSpec((None, 2, ML_HEADS, 1, d), lambda b: (b, 0, 0, 0, 0))
    m_spec = pl.BlockSpec((None, 2, ML_HEADS, 1, LANES), lambda b: (b, 0, 0, 0, 0))
    if state is not None:
        in_specs += [c_spec, n_spec, m_spec]
        args += list(state)
    out_specs = [tile]
    out_shape = [jax.ShapeDtypeStruct((rows, D_MODEL), BF16)]
    if emit_state:
        out_specs += [c_spec, n_spec, m_spec]
        out_shape += [jax.ShapeDtypeStruct((batch, 2, ML_HEADS, d, d), F32),
                      jax.ShapeDtypeStruct((batch, 2, ML_HEADS, 1, d), F32),
                      jax.ShapeDtypeStruct((batch, 2, ML_HEADS, 1, LANES), F32)]
    n_state = 2 * ML_HEADS
    return pl.pallas_call(
        functools.partial(_mlstm_kernel, seq_len=seq_len, has_state=state is not None, emit_state=emit_state),
        grid=(batch,),
        in_specs=in_specs,
        out_specs=out_specs,
        out_shape=out_shape,
        scratch_shapes=[pltpu.VMEM((n_state, d, d), F32), pltpu.VMEM((n_state, 1, d), F32),
                        pltpu.VMEM((n_state, 1, 1), F32),
                        pltpu.VMEM((seq_len // CHUNK, GATE_PAD, CHUNK), F32),
                        pltpu.VMEM((2, seq_len, D_MODEL), F32)],
        compiler_params=_params(1),
        name="mlstm",
    )(*args)


def _merge_kernel(x_ref, m_ref, hm_ref, att_ref, sgm_ref, sgd_ref, wm_ref, wd_ref, wo_ref, o_ref, w_sc):
    @pl.when(pl.program_id(0) == 0)
    def _():
        for j, w_ref in enumerate((wm_ref, wd_ref, wo_ref)):
            w_sc[j] = w_ref[...].astype(BF16)

    y = sgm_ref[...] * _dot(hm_ref[...], w_sc[0]) + sgd_ref[...] * _dot(att_ref[...], w_sc[1])
    o_ref[...] = x_ref[...] + m_ref[5:6, :] * _dot(y, w_sc[2])


def _merge(x, mods, mod_index, hm, att, sgm, sgd, wm, wd, wo):
    rows = x.shape[0]
    tile = pl.BlockSpec((MERGE_TM, D_MODEL), lambda i: (i, 0))
    w_spec = _resident((D_MODEL, D_MODEL))
    return pl.pallas_call(
        _merge_kernel,
        grid=(rows // MERGE_TM,),
        in_specs=[tile, pl.BlockSpec((None, N_MOD, D_MODEL), lambda i: (mod_index(i * MERGE_TM), 0, 0)),
                  tile, tile, tile, tile, w_spec, w_spec, w_spec],
        out_specs=tile,
        out_shape=jax.ShapeDtypeStruct((rows, D_MODEL), F32),
        scratch_shapes=[pltpu.VMEM((3, D_MODEL, D_MODEL), BF16)],
        compiler_params=_params(1),
        name="branch_merge",
    )(x, mods, hm, att, sgm, sgd, wm, wd, wo)


def _rope_tables(seq_len):
    lane = jnp.arange(LANES)
    r = lane % 32
    freqs = jnp.power(ROPE_BASE, -(r % 16).astype(F32) / 16.0)
    tok = jnp.arange(seq_len)
    pos = jnp.where((lane % 64 < 32)[None, :], (tok // GRID_W)[:, None], (tok % GRID_W)[:, None]).astype(F32)
    ang = pos * freqs[None, :]
    sign = jnp.where(r < 16, -1.0, 1.0).astype(F32)
    return jnp.cos(ang), jnp.sin(ang) * sign[None, :]


def _layer(x, mods, mod_index, w, batch, seq_len, ctx, lam_init):
    x = _ffn(x, mods, mod_index, w["g_norm"][0], w["ffn1_w1"], w["ffn1_w3"], w["ffn1_w2"], base=0)
    rope_tabs = None if ctx is None else _rope_tables(seq_len)
    mq, mk, mv, so, dq, dk, dv, sgm, sgd, gates, *new_k = _proj(
        x, mods, mod_index, w["g_norm"][1], w["w_in_t"], w["b_gate"], w["bd"], w["g_qn"], w["g_kn"],
        rope_tabs, seq_len)
    cache = None if ctx is None else (ctx[0], ctx[1])
    att = _attn(dq, dk, dv, cache, w["lam"], w["g_sub"], batch, seq_len, lam_init)
    state = None if ctx is None else ctx[2]
    res = _mlstm(mq, mk, mv, so, gates, w["g_mh"], state, batch, seq_len, emit_state=ctx is None)
    x = _merge(x, mods, mod_index, res[0], att, sgm, sgd, w["w_br_m"], w["w_br_d"], w["w_out"])
    x = _ffn(x, mods, mod_index, w["g_norm"][2], w["ffn2_w1"], w["ffn2_w3"], w["ffn2_w2"], base=6)
    return x, new_k, dv, res[1:]


def kernel(x_prompt, x_sample, c, cache_k, cache_v, state_C, state_n, state_m, c_ctx, w_ada, b_ada, g_norm, ffn1_w1, ffn1_w3, ffn1_w2, ffn2_w1, ffn2_w3, ffn2_w2, w_in, b_gate, g_qn, g_kn, lam_q1, lam_k1, lam_q2, lam_k2, g_sub, g_mh, w_br_m, w_br_d, w_out):
    depth = w_ada.shape[0]
    assert depth == 1
    l = 0
    bp, tp, _ = x_prompt.shape
    bs, ts, _ = x_sample.shape
    past = cache_k.shape[2]
    lam_init = 0.8 - 0.6 * math.exp(-0.3 * l)

    cvecs = jnp.concatenate([c_ctx[None, :], c, jnp.zeros((8 - 1 - bs, D_MODEL), F32)], axis=0)
    mods = _mods(cvecs, w_ada[l], b_ada[l]).reshape(8, N_MOD, D_MODEL)

    group = jnp.arange(MXU_DIM) // DA_HEAD_DIM
    w = dict(
        g_norm=g_norm[l],
        ffn1_w1=ffn1_w1[l], ffn1_w3=ffn1_w3[l], ffn1_w2=ffn1_w2[l],
        ffn2_w1=ffn2_w1[l], ffn2_w3=ffn2_w3[l], ffn2_w2=ffn2_w2[l],
        w_in_t=w_in[l].T,
        b_gate=jnp.pad(b_gate[l], (0, GATE_PAD - N_GATE_COLS)).reshape(1, GATE_PAD),
        bd=(group[:, None] == group[None, :]).astype(BF16),
        g_qn=jnp.tile(g_qn[l], MXU_DIM // DA_HEAD_DIM).reshape(1, MXU_DIM),
        g_kn=jnp.tile(g_kn[l], MXU_DIM // DA_HEAD_DIM).reshape(1, MXU_DIM),
        lam=jnp.stack([lam_q1[l], lam_k1[l], lam_q2[l], lam_k2[l]]),
        g_sub=g_sub[l], g_mh=g_mh[l],
        w_br_m=w_br_m[l], w_br_d=w_br_d[l], w_out=w_out[l],
    )

    xp, (new_k,), new_v, (new_c, new_n, new_m) = _layer(
        x_prompt.reshape(bp * tp, D_MODEL), mods, lambda r: 0, w, bp, tp, None, lam_init)

    ctx = (cache_k[:, l].reshape(bs, past, D_MODEL), cache_v[:, l].reshape(bs, past, D_MODEL),
           (state_C[:, l], state_n[:, l].reshape(bs, 2, ML_HEADS, 1, ML_HEAD_DIM),
            jnp.broadcast_to(state_m[:, l][..., None, None], (bs, 2, ML_HEADS, 1, LANES))))
    xs, _, _, _ = _layer(x_sample.reshape(bs * ts, D_MODEL), mods, lambda r: 1 + r // ts, w, bs, ts, ctx, lam_init)

    return (xp.reshape(bp, tp, D_MODEL), xs.reshape(bs, ts, D_MODEL),
            new_k.reshape(bp, 1, tp, DA_HEADS, 2, DA_HEAD_DIM),
            new_v.reshape(bp, 1, tp, DA_HEADS, DA_V_DIM),
            new_c[:, None], new_n.reshape(bp, 1, 2, ML_HEADS, ML_HEAD_DIM),
            new_m[..., 0, 0][:, None])
```

```python
import functools
import math

import jax
import jax.numpy as jnp
from jax import lax
from jax.experimental import pallas as pl
from jax.experimental.pallas import tpu as pltpu

F32 = jnp.float32
BF16 = jnp.bfloat16

D_MODEL = 1024
D_FF = 2816
N_MOD = 9
GRID_W = 64
ML_HEADS = 4
ML_HEAD_DIM = 256
DA_HEADS = 8
DA_HEAD_DIM = 64
DA_V_DIM = 128
N_GATE_COLS = 16
CHUNK = 128
ROPE_BASE = 10000.0
EPS = 1e-6

LANES = 128
GATE_PAD = LANES
MXU_DIM = 256
VMEM_LIMIT = 56 * 1024 * 1024

FFN_TM = 512
FFN_TF = 256
PROJ_TM = 256
MERGE_TM = 512
ATT_TQ = 256


def _params(n_axes):
    return pltpu.CompilerParams(dimension_semantics=("arbitrary",) * n_axes,
                                vmem_limit_bytes=VMEM_LIMIT)


def _dot(a, b):
    return jnp.dot(a.astype(BF16), b.astype(BF16), preferred_element_type=F32)


def _dot_nt(a, b):
    return lax.dot_general(a.astype(BF16), b.astype(BF16), (((1,), (1,)), ((), ())),
                           preferred_element_type=F32)


def _dot_tn(a, b):
    return lax.dot_general(a.astype(BF16), b.astype(BF16), (((0,), (0,)), ((), ())),
                           preferred_element_type=F32)


def _sigmoid(x):
    return 1.0 / (1.0 + jnp.exp(-x))


def _log_sigmoid(x):
    return jnp.minimum(x, 0.0) - jnp.log1p(jnp.exp(-jnp.abs(x)))


def _modulated_norm(x, g, shift, scale):
    y = x * lax.rsqrt(jnp.mean(x * x, axis=-1, keepdims=True) + EPS) * g
    return y * (1.0 + scale) + shift


def _resident(shape):
    return pl.BlockSpec(shape, lambda *_: (0,) * len(shape), pipeline_mode=pl.Buffered(1))


def _mods_kernel(c_ref, w_ref, b_ref, o_ref):
    c = c_ref[...]
    o_ref[...] = _dot(c * _sigmoid(c), w_ref[...]) + b_ref[...]


def _mods(cvecs, w_ada, b_ada):
    n = N_MOD * D_MODEL
    tn = D_MODEL
    return pl.pallas_call(
        _mods_kernel,
        grid=(n // tn,),
        in_specs=[pl.BlockSpec((8, D_MODEL), lambda j: (0, 0)),
                  pl.BlockSpec((D_MODEL, tn), lambda j: (0, j)),
                  pl.BlockSpec((1, tn), lambda j: (0, j))],
        out_specs=pl.BlockSpec((8, tn), lambda j: (0, j)),
        out_shape=jax.ShapeDtypeStruct((8, n), F32),
        compiler_params=_params(1),
        name="adaln_mods",
    )(cvecs, w_ada, b_ada.reshape(1, n))


def _ffn_kernel(x_ref, m_ref, g_ref, w1_ref, w3_ref, w2_ref, o_ref, w1_sc, w3_sc, w2_sc, hh_sc, acc_sc, *, base):
    g = pl.program_id(0)
    nf = D_FF // FFN_TF
    norm = lambda x: _modulated_norm(x, g_ref[...], m_ref[base:base + 1, :], m_ref[base + 1:base + 2, :]).astype(BF16)
    finish = lambda x, acc: x + 0.5 * m_ref[base + 2:base + 3, :] * acc

    def tile(hh, f):
        a = _dot(hh, w1_sc[f])
        b = _dot(hh, w3_sc[f])
        return _dot(a * _sigmoid(a) * b, w2_sc[f])

    @pl.when(g < nf)
    def _():
        w1_sc[g] = w1_ref[...].astype(BF16)
        w3_sc[g] = w3_ref[...].astype(BF16)
        w2_sc[g] = w2_ref[...].astype(BF16)

        @pl.when(g == 0)
        def _():
            hh_sc[...] = norm(x_ref[...])
            acc_sc[...] = jnp.zeros(acc_sc.shape, F32)

        acc_sc[...] += tile(hh_sc[...], g)

        @pl.when(g == nf - 1)
        def _():
            o_ref[...] = finish(x_ref[...], acc_sc[...])

    @pl.when(g >= nf)
    def _():
        x = x_ref[...]
        hh = norm(x)
        acc = jnp.zeros(x.shape, F32)
        for f in range(nf):
            acc = acc + tile(hh, f)
        o_ref[...] = finish(x, acc)


def _ffn(x, mods, mod_index, g, w1, w3, w2, base):
    rows = x.shape[0]
    nf = D_FF // FFN_TF
    row_tile = lambda s: jnp.maximum(s - (nf - 1), 0)
    f_tile = lambda s: jnp.minimum(s, nf - 1)
    row = lambda s: (row_tile(s), 0)
    return pl.pallas_call(
        functools.partial(_ffn_kernel, base=base),
        grid=(nf - 1 + rows // FFN_TM,),
        in_specs=[pl.BlockSpec((FFN_TM, D_MODEL), row),
                  pl.BlockSpec((None, N_MOD, D_MODEL), lambda s: (mod_index(row_tile(s) * FFN_TM), 0, 0)),
                  _resident((1, D_MODEL)),
                  pl.BlockSpec((D_MODEL, FFN_TF), lambda s: (0, f_tile(s))),
                  pl.BlockSpec((D_MODEL, FFN_TF), lambda s: (0, f_tile(s))),
                  pl.BlockSpec((FFN_TF, D_MODEL), lambda s: (f_tile(s), 0))],
        out_specs=pl.BlockSpec((FFN_TM, D_MODEL), row),
        out_shape=jax.ShapeDtypeStruct((rows, D_MODEL), F32),
        scratch_shapes=[pltpu.VMEM((nf, D_MODEL, FFN_TF), BF16), pltpu.VMEM((nf, D_MODEL, FFN_TF), BF16),
                        pltpu.VMEM((nf, FFN_TF, D_MODEL), BF16),
                        pltpu.VMEM((FFN_TM, D_MODEL), BF16), pltpu.VMEM((FFN_TM, D_MODEL), F32)],
        compiler_params=_params(1),
        name="ffn",
    )(x, mods, g.reshape(1, D_MODEL), w1, w3, w2)


def _group_norm64(x, bd, g):
    ss = _dot(x * x, bd)
    return x * lax.rsqrt(ss * (1.0 / DA_HEAD_DIM) + EPS) * g


def _rope(x, cos, sin_signed):
    first = (lax.broadcasted_iota(jnp.int32, x.shape, 1) % 32) < 16
    partner = jnp.where(first, pltpu.roll(x, LANES - 16, 1), pltpu.roll(x, 16, 1))
    return x * cos + partner * sin_signed


N_SEG = 9
GATE_LO = 4 * D_MODEL


def _seg_start(s):
    return s * D_MODEL + N_GATE_COLS * (s >= 4)


def _proj_kernel(*refs, rope):
    (x_ref, m_ref, g_ref, wt_ref, wgt_ref, bg_ref, bd_ref, gq_ref, gk_ref), refs = refs[:9], refs[9:]
    if rope:
        (cos_ref, sin_ref), refs = refs[:2], refs[2:]
        newk_ref = None
    else:
        newk_ref, refs = refs[10], refs[:10] + refs[11:]
    (mq_ref, mk_ref, mv_ref, so_ref, dq_ref, dk_ref, dv_ref, sgm_ref, sgd_ref, gates_ref,
     w_sc, wg_sc, hh_sc) = refs
    step = pl.program_id(0)
    norm = lambda: _modulated_norm(x_ref[...], g_ref[...], m_ref[3:4, :], m_ref[4:5, :]).astype(BF16)

    def gates(hh):
        y = _dot_nt(hh, wg_sc[...]) + bg_ref[...]
        col = lax.broadcasted_iota(jnp.int32, y.shape, 1)
        gates_ref[...] = jnp.where((col // ML_HEADS) % 2 == 1, _log_sigmoid(y), y)

    def qk_segment(y, g_head_ref, o_ref, cache_ref):
        bd = bd_ref[...]
        for c in range(D_MODEL // MXU_DIM):
            cols = slice(c * MXU_DIM, (c + 1) * MXU_DIM)
            z = _group_norm64(y[:, cols], bd, g_head_ref[...])
            if rope:
                z = jnp.concatenate(
                    [_rope(z[:, k * LANES:(k + 1) * LANES], cos_ref[...], sin_ref[...])
                     for k in range(MXU_DIM // LANES)], axis=1)
            o_ref[:, cols] = z.astype(o_ref.dtype)
            if cache_ref is not None:
                groups = MXU_DIM // DA_HEAD_DIM
                cache_ref[:, c * groups:(c + 1) * groups, :] = z.reshape(z.shape[0], groups, DA_HEAD_DIM)

    def segment(s, hh):
        if s in (0, 2):
            o_ref = {0: mq_ref, 2: mv_ref}[s]
            y_t = _dot_nt(w_sc[s], hh) * ((ML_HEAD_DIM ** -0.5) if s == 0 else 1.0)
            for j in range(PROJ_TM // CHUNK):
                o_ref[j] = y_t[:, j * CHUNK:(j + 1) * CHUNK].astype(o_ref.dtype)
            return
        y = _dot_nt(hh, w_sc[s])
        if s == 4:
            qk_segment(y, gq_ref, dq_ref, None)
        elif s == 5:
            qk_segment(y, gk_ref, dk_ref, newk_ref)
        elif s in (3, 7, 8):
            {3: so_ref, 7: sgm_ref, 8: sgd_ref}[s][...] = _sigmoid(y)
        else:
            o_ref = {1: mk_ref, 6: dv_ref}[s]
            o_ref[...] = y.astype(o_ref.dtype)

    for s in range(N_SEG):
        @pl.when(step == s)
        def _(s=s):
            w_sc[s] = wt_ref[...].astype(BF16)
            if s == 0:
                wg_sc[...] = jnp.zeros(wg_sc.shape, BF16)
                wg_sc[0:N_GATE_COLS, :] = wgt_ref[...].astype(BF16)
                hh_sc[...] = norm()
                gates(hh_sc[...])
            segment(s, hh_sc[...])

    @pl.when(step >= N_SEG)
    def _():
        hh = norm()
        gates(hh)
        for s in range(N_SEG):
            segment(s, hh)


def _proj(x, mods, mod_index, g, w_in_t, b_gate, bd, gq, gk, rope_tabs, seq_len):
    rows = x.shape[0]
    rope = rope_tabs is not None
    row_tile = lambda s: jnp.maximum(s - (N_SEG - 1), 0)
    row = lambda s: (row_tile(s), 0)
    tile = pl.BlockSpec((PROJ_TM, D_MODEL), row)
    elem = lambda n: (pl.Element(n), pl.Element(D_MODEL))
    seg_row = lambda s: pl.multiple_of(_seg_start(jnp.minimum(s, N_SEG - 1)), N_GATE_COLS)
    in_specs = [tile,
                pl.BlockSpec((None, N_MOD, D_MODEL), lambda s: (mod_index(row_tile(s) * PROJ_TM), 0, 0)),
                _resident((1, D_MODEL)),
                pl.BlockSpec(elem(D_MODEL), lambda s: (seg_row(s), 0)),
                pl.BlockSpec(elem(N_GATE_COLS), lambda s: (GATE_LO, 0)),
                _resident((1, GATE_PAD)),
                _resident((MXU_DIM, MXU_DIM)),
                _resident((1, MXU_DIM)),
                _resident((1, MXU_DIM))]
    args = [x, mods, g.reshape(1, D_MODEL), w_in_t, w_in_t, b_gate, bd, gq, gk]
    if rope:
        tiles_per_seq = seq_len // PROJ_TM
        tab = pl.BlockSpec((PROJ_TM, LANES), lambda s: (row_tile(s) % tiles_per_seq, 0))
        in_specs += [tab, tab]
        args += list(rope_tabs)
    dtypes = [BF16, BF16, BF16, F32, BF16, BF16, BF16 if rope else F32, F32, F32]
    out_shape = [jax.ShapeDtypeStruct((rows, D_MODEL), dt) for dt in dtypes]
    out_shape.append(jax.ShapeDtypeStruct((rows, GATE_PAD), F32))
    out_specs = [tile] * 9 + [pl.BlockSpec((PROJ_TM, GATE_PAD), row)]
    slabs = PROJ_TM // CHUNK
    for j in (0, 2):
        out_shape[j] = jax.ShapeDtypeStruct((rows // CHUNK, D_MODEL, CHUNK), BF16)
        out_specs[j] = pl.BlockSpec((slabs, D_MODEL, CHUNK), lambda s: (row_tile(s), 0, 0))
    if not rope:
        n_groups = 2 * DA_HEADS
        out_shape.append(jax.ShapeDtypeStruct((rows, n_groups, DA_HEAD_DIM), F32))
        out_specs.append(pl.BlockSpec((PROJ_TM, n_groups, DA_HEAD_DIM), lambda s: (row_tile(s), 0, 0)))
    return pl.pallas_call(
        functools.partial(_proj_kernel, rope=rope),
        grid=(N_SEG - 1 + rows // PROJ_TM,),
        in_specs=in_specs,
        out_specs=out_specs,
        out_shape=out_shape,
        scratch_shapes=[pltpu.VMEM((N_SEG, D_MODEL, D_MODEL), BF16), pltpu.VMEM((GATE_PAD, D_MODEL), BF16),
                        pltpu.VMEM((PROJ_TM, D_MODEL), BF16)],
        compiler_params=_params(1),
        name="mixer_in_proj",
    )(*args)


def _lambda(lam_ref, lam_init):
    lam = lam_ref[...]
    s1 = jnp.sum(lam[0:1, :] * lam[1:2, :], axis=1, keepdims=True)
    s2 = jnp.sum(lam[2:3, :] * lam[3:4, :], axis=1, keepdims=True)
    return jnp.exp(s1) - jnp.exp(s2) + lam_init


def _attn_kernel(*refs, cached, lam_init):
    if cached:
        q_ref, k_ref, v_ref, ck_ref, cv_ref, lam_ref, gs_ref, o_ref = refs
    else:
        q_ref, k_ref, v_ref, lam_ref, gs_ref, o_ref = refs
    lam = _lambda(lam_ref, lam_init)
    scale = DA_HEAD_DIM ** -0.5
    lane = lax.broadcasted_iota(jnp.int32, (1, DA_V_DIM), 1)
    comp_masks = [lane < DA_HEAD_DIM, lane >= DA_HEAD_DIM]
    for h in range(DA_HEADS):
        cols = slice(h * DA_V_DIM, (h + 1) * DA_V_DIM)
        q = q_ref[:, cols].astype(BF16) * scale
        key_sets = [(k_ref[:, cols].astype(BF16), v_ref[:, cols].astype(BF16))]
        if cached:
            key_sets.append((ck_ref[:, cols].astype(BF16), cv_ref[:, cols].astype(BF16)))
        branch = []
        for mask in comp_masks:
            qc = jnp.where(mask, q, jnp.zeros_like(q))
            s = [_dot_nt(qc, k) for k, _ in key_sets]
            mx = functools.reduce(jnp.maximum, [jnp.max(si, axis=1, keepdims=True) for si in s])
            e = [jnp.exp(si - mx) for si in s]
            den = functools.reduce(jnp.add, [jnp.sum(ei, axis=1, keepdims=True) for ei in e])
            pv = functools.reduce(jnp.add, [_dot(ei, v) for ei, (_, v) in zip(e, key_sets)])
            branch.append(pv * (1.0 / den))
        out = branch[0] - lam * branch[1]
        out = out * lax.rsqrt(jnp.mean(out * out, axis=1, keepdims=True) + EPS) * gs_ref[...]
        o_ref[:, cols] = (out * (1.0 - lam_init)).astype(o_ref.dtype)


def _attn(q, k, v, cache, lam_vecs, g_sub, batch, seq_len, lam_init):
    rows = q.shape[0]
    nq = seq_len // ATT_TQ
    q_spec = pl.BlockSpec((ATT_TQ, D_MODEL), lambda b, i: (b * nq + i, 0))
    kv_spec = pl.BlockSpec((seq_len, D_MODEL), lambda b, i: (b, 0))
    in_specs = [q_spec, kv_spec, kv_spec]
    args = [q, k, v]
    if cache is not None:
        past = cache[0].shape[1]
        c_spec = pl.BlockSpec((None, past, D_MODEL), lambda b, i: (b, 0, 0))
        in_specs += [c_spec, c_spec]
        args += list(cache)
    in_specs += [_resident((4, DA_HEAD_DIM)), _resident((1, DA_V_DIM))]
    args += [lam_vecs, g_sub.reshape(1, DA_V_DIM)]
    return pl.pallas_call(
        functools.partial(_attn_kernel, cached=cache is not None, lam_init=lam_init),
        grid=(batch, nq),
        in_specs=in_specs,
        out_specs=q_spec,
        out_shape=jax.ShapeDtypeStruct((rows, D_MODEL), BF16),
        compiler_params=_params(2),
        name="diff_attention",
    )(*args)


def _per_chain(fn, a, b):
    return jnp.stack([fn(a[i], b[i]) for i in range(a.shape[0])])


ML_EXT = 8


def _rows(x):
    return jnp.stack([x[i:i + 1, :] for i in range(x.shape[0])])


def _chunk_scan(x, reverse_rows):
    n = x.shape[1]
    lane = lax.broadcasted_iota(jnp.int32, x.shape, 1)
    k = 1
    while k < n:
        fwd = jnp.where(lane >= k, pltpu.roll(x, k, 1), 0.0)
        bwd = jnp.where(lane < n - k, pltpu.roll(x, n - k, 1), 0.0)
        x = x + jnp.where(reverse_rows, bwd, fwd)
        k *= 2
    return x


def _mlstm_step(k, q_t, v_t, ic, fc, CT, m, seen_t, reverse_rows):
    B, L = ic.shape
    d = k.shape[2]
    b2 = _chunk_scan(fc, reverse_rows)
    u2 = ic - b2
    u_t = jnp.concatenate([u2, jnp.zeros((L - B, L), F32)], axis=0).T
    u_col = jnp.stack([jnp.broadcast_to(u_t[:, i:i + 1], (L, L)) for i in range(B)])
    b, i_g, f_g = _rows(b2), _rows(ic), _rows(fc)
    b_last = jnp.sum(f_g, axis=2, keepdims=True)

    log_d = jnp.where(seen_t, b + u_col, -jnp.inf)
    a = b + m
    m_t = jnp.maximum(a, jnp.max(log_d, axis=1, keepdims=True))
    dmat = jnp.exp(log_d - m_t)
    inter = jnp.exp(a - m_t)
    sc = _per_chain(_dot, k, q_t) * dmat
    cq = _per_chain(_dot, CT, q_t)
    num = _per_chain(_dot, v_t, sc) + inter * cq[:, :d, :]
    den = jnp.sum(sc, axis=1, keepdims=True) + inter * cq[:, d:d + 1, :]
    h_t = num * (1.0 / jnp.maximum(jnp.abs(den), jnp.exp(-m_t)))

    g = b_last - b + i_g
    m_new = jnp.maximum(b_last + m, jnp.max(g, axis=2, keepdims=True))
    w = jnp.exp(g - m_new)
    decay = jnp.exp(b_last + m - m_new)
    vw = jnp.concatenate([v_t.astype(F32), jnp.ones((B, ML_EXT, L), F32)], axis=1) * w
    CT_new = decay * CT + _per_chain(_dot, vw, k)
    return h_t, CT_new, m_new


def _mlstm_kernel(*refs, seq_len, has_state, emit_state):
    q_ref, k_ref, v_ref, so_ref, g_ref, gmh_ref = refs[:6]
    refs = refs[6:]
    if has_state:
        (c0_ref, n0_ref, m0_ref), refs = refs[:3], refs[3:]
    hm_ref, refs = refs[0], refs[1:]
    if emit_state:
        (c_out_ref, n_out_ref, m_out_ref), refs = refs[:3], refs[3:]
    ct_sc, m_sc, gr_sc, h_sc = refs

    nc = seq_len // CHUNK
    d_head = ML_HEAD_DIM
    chains = [(d, h) for d in range(2) for h in range(ML_HEADS)]
    n_chain = len(chains)
    for i, (d, h) in enumerate(chains):
        if has_state:
            ct_sc[i, 0:d_head, :] = c0_ref[d, h].T
            ct_sc[i, d_head:d_head + ML_EXT, :] = jnp.broadcast_to(n0_ref[d, h], (ML_EXT, d_head))
            m_sc[i] = m0_ref[d, h][:, 0:1]
        else:
            ct_sc[i] = jnp.zeros((d_head + ML_EXT, d_head), F32)
            m_sc[i] = jnp.zeros((1, 1), F32)
    for c in range(nc):
        gr_sc[c] = g_ref[c * CHUNK:(c + 1) * CHUNK, :].T

    s_idx = lax.broadcasted_iota(jnp.int32, (CHUNK, CHUNK), 0)
    t_idx = lax.broadcasted_iota(jnp.int32, (CHUNK, CHUNK), 1)
    seen_t = jnp.stack([s_idx <= t_idx] * ML_HEADS + [s_idx >= t_idx] * ML_HEADS)
    reverse_rows = lax.broadcasted_iota(jnp.int32, (n_chain, CHUNK), 0) >= ML_HEADS

    def step(c_fwd, c_bwd, rows_of):
        chunk_of = (c_fwd, c_bwd)
        g_fwd, g_bwd = gr_sc[c_fwd], gr_sc[c_bwd]
        lo = 2 * ML_HEADS
        ic = jnp.concatenate([g_fwd[0:ML_HEADS], g_bwd[lo:lo + ML_HEADS]], axis=0)
        fc = jnp.concatenate([g_fwd[ML_HEADS:lo], g_bwd[lo + ML_HEADS:2 * lo]], axis=0)
        head = [slice(h * d_head, (h + 1) * d_head) for _, h in chains]
        stack = lambda pick: jnp.stack([pick(i, chunk_of[d]) for i, (d, _) in enumerate(chains)])
        h_t, CT_new, m_new = _mlstm_step(
            stack(lambda i, c: k_ref[rows_of(c), head[i]]),
            stack(lambda i, c: q_ref[c, head[i], :]),
            stack(lambda i, c: v_ref[c, head[i], :]),
            ic, fc, ct_sc[...], m_sc[...], seen_t, reverse_rows)
        ct_sc[...] = CT_new
        m_sc[...] = m_new
        for i, (d, _) in enumerate(chains):
            h_sc[d, rows_of(chunk_of[d]), head[i]] = h_t[i].T

    if nc <= 2:
        for c in range(nc):
            step(c, nc - 1 - c, lambda cc: slice(cc * CHUNK, (cc + 1) * CHUNK))
    else:
        def body(c, carry):
            step(c, nc - 1 - c, lambda cc: pl.ds(pl.multiple_of(cc * CHUNK, CHUNK), CHUNK))
            return carry
        lax.fori_loop(0, nc, body, 0)

    for c in range(nc):
        rows = slice(c * CHUNK, (c + 1) * CHUNK)
        for h in range(ML_HEADS):
            hcols = slice(h * d_head, (h + 1) * d_head)
            hsum = h_sc[0, rows, hcols] + h_sc[1, rows, hcols]
            hn = hsum * lax.rsqrt(jnp.mean(hsum * hsum, axis=1, keepdims=True) + EPS) * gmh_ref[h]
            hm_ref[rows, hcols] = (hn * so_ref[rows, hcols]).astype(hm_ref.dtype)
    if emit_state:
        for i, (d, h) in enumerate(chains):
            c_out_ref[d, h] = ct_sc[i, 0:d_head, :].T
            n_out_ref[d, h] = ct_sc[i, d_head:d_head + 1, :]
            m_out_ref[d, h] = jnp.broadcast_to(m_sc[i], (1, LANES))


def _mlstm(q_t, k, v_t, so, gates, g_mh, state, batch, seq_len, emit_state):
    rows = k.shape[0]
    d = ML_HEAD_DIM
    nc = seq_len // CHUNK
    tile = pl.BlockSpec((seq_len, D_MODEL), lambda b: (b, 0))
    tile_t = pl.BlockSpec((nc, D_MODEL, CHUNK), lambda b: (b, 0, 0))
    in_specs = [tile_t, tile, tile_t, tile,
                pl.BlockSpec((seq_len, GATE_PAD), lambda b: (b, 0)),
                _resident((ML_HEADS, 1, d))]
    args = [q_t, k, v_t, so, gates, g_mh.reshape(ML_HEADS, 1, d)]
    c_spec = pl.BlockSpec((None, 2, ML_HEADS, d, d), lambda b: (b, 0, 0, 0, 0))
    n_spec = pl.BlockSpec((None, 2, ML_HEADS, 1, d), lambda b: (b, 0, 0, 0, 0))
    m_spec = pl.BlockSpec((None, 2, ML_HEADS, 1, LANES), lambda b: (b, 0, 0, 0, 0))
    if state is not None:
        in_specs += [c_spec, n_spec, m_spec]
        args += list(state)
    out_specs = [tile]
    out_shape = [jax.ShapeDtypeStruct((rows, D_MODEL), BF16)]
    if emit_state:
        out_specs += [c_spec, n_spec, m_spec]
        out_shape += [jax.ShapeDtypeStruct((batch, 2, ML_HEADS, d, d), F32),
                      jax.ShapeDtypeStruct((batch, 2, ML_HEADS, 1, d), F32),
                      jax.ShapeDtypeStruct((batch, 2, ML_HEADS, 1, LANES), F32)]
    n_state = 2 * ML_HEADS
    return pl.pallas_call(
        functools.partial(_mlstm_kernel, seq_len=seq_len, has_state=state is not None, emit_state=emit_state),
        grid=(batch,),
        in_specs=in_specs,
        out_specs=out_specs,
        out_shape=out_shape,
        scratch_shapes=[pltpu.VMEM((n_state, d + ML_EXT, d), F32), pltpu.VMEM((n_state, 1, 1), F32),
                        pltpu.VMEM((nc, GATE_PAD, CHUNK), F32),
                        pltpu.VMEM((2, seq_len, D_MODEL), F32)],
        compiler_params=_params(1),
        name="mlstm",
    )(*args)


def _merge_kernel(x_ref, m_ref, hm_ref, att_ref, sgm_ref, sgd_ref, wm_ref, wd_ref, wo_ref, o_ref, w_sc):
    @pl.when(pl.program_id(0) == 0)
    def _():
        for j, w_ref in enumerate((wm_ref, wd_ref, wo_ref)):
            w_sc[j] = w_ref[...].astype(BF16)

    y = sgm_ref[...] * _dot(hm_ref[...], w_sc[0]) + sgd_ref[...] * _dot(att_ref[...], w_sc[1])
    o_ref[...] = x_ref[...] + m_ref[5:6, :] * _dot(y, w_sc[2])


def _merge(x, mods, mod_index, hm, att, sgm, sgd, wm, wd, wo):
    rows = x.shape[0]
    tile = pl.BlockSpec((MERGE_TM, D_MODEL), lambda i: (i, 0))
    w_spec = _resident((D_MODEL, D_MODEL))
    return pl.pallas_call(
        _merge_kernel,
        grid=(rows // MERGE_TM,),
        in_specs=[tile, pl.BlockSpec((None, N_MOD, D_MODEL), lambda i: (mod_index(i * MERGE_TM), 0, 0)),
                  tile, tile, tile, tile, w_spec, w_spec, w_spec],
        out_specs=tile,
        out_shape=jax.ShapeDtypeStruct((rows, D_MODEL), F32),
        scratch_shapes=[pltpu.VMEM((3, D_MODEL, D_MODEL), BF16)],
        compiler_params=_params(1),
        name="branch_merge",
    )(x, mods, hm, att, sgm, sgd, wm, wd, wo)


def _rope_tables(seq_len):
    lane = jnp.arange(LANES)
    r = lane % 32
    freqs = jnp.power(ROPE_BASE, -(r % 16).astype(F32) / 16.0)
    tok = jnp.arange(seq_len)
    pos = jnp.where((lane % 64 < 32)[None, :], (tok // GRID_W)[:, None], (tok % GRID_W)[:, None]).astype(F32)
    ang = pos * freqs[None, :]
    sign = jnp.where(r < 16, -1.0, 1.0).astype(F32)
    return jnp.cos(ang), jnp.sin(ang) * sign[None, :]


def _layer(x, mods, mod_index, w, batch, seq_len, ctx, lam_init):
    x = _ffn(x, mods, mod_index, w["g_norm"][0], w["ffn1_w1"], w["ffn1_w3"], w["ffn1_w2"], base=0)
    rope_tabs = None if ctx is None else _rope_tables(seq_len)
    mq, mk, mv, so, dq, dk, dv, sgm, sgd, gates, *new_k = _proj(
        x, mods, mod_index, w["g_norm"][1], w["w_in_t"], w["b_gate"], w["bd"], w["g_qn"], w["g_kn"],
        rope_tabs, seq_len)
    cache = None if ctx is None else (ctx[0], ctx[1])
    att = _attn(dq, dk, dv, cache, w["lam"], w["g_sub"], batch, seq_len, lam_init)
    state = None if ctx is None else ctx[2]
    res = _mlstm(mq, mk, mv, so, gates, w["g_mh"], state, batch, seq_len, emit_state=ctx is None)
    x = _merge(x, mods, mod_index, res[0], att, sgm, sgd, w["w_br_m"], w["w_br_d"], w["w_out"])
    x = _ffn(x, mods, mod_index, w["g_norm"][2], w["ffn2_w1"], w["ffn2_w3"], w["ffn2_w2"], base=6)
    return x, new_k, dv, res[1:]


def kernel(x_prompt, x_sample, c, cache_k, cache_v, state_C, state_n, state_m, c_ctx, w_ada, b_ada, g_norm, ffn1_w1, ffn1_w3, ffn1_w2, ffn2_w1, ffn2_w3, ffn2_w2, w_in, b_gate, g_qn, g_kn, lam_q1, lam_k1, lam_q2, lam_k2, g_sub, g_mh, w_br_m, w_br_d, w_out):
    depth = w_ada.shape[0]
    assert depth == 1
    l = 0
    bp, tp, _ = x_prompt.shape
    bs, ts, _ = x_sample.shape
    past = cache_k.shape[2]
    lam_init = 0.8 - 0.6 * math.exp(-0.3 * l)

    cvecs = jnp.concatenate([c_ctx[None, :], c, jnp.zeros((8 - 1 - bs, D_MODEL), F32)], axis=0)
    mods = _mods(cvecs, w_ada[l], b_ada[l]).reshape(8, N_MOD, D_MODEL)

    group = jnp.arange(MXU_DIM) // DA_HEAD_DIM
    w = dict(
        g_norm=g_norm[l],
        ffn1_w1=ffn1_w1[l], ffn1_w3=ffn1_w3[l], ffn1_w2=ffn1_w2[l],
        ffn2_w1=ffn2_w1[l], ffn2_w3=ffn2_w3[l], ffn2_w2=ffn2_w2[l],
        w_in_t=w_in[l].T,
        b_gate=jnp.pad(b_gate[l], (0, GATE_PAD - N_GATE_COLS)).reshape(1, GATE_PAD),
        bd=(group[:, None] == group[None, :]).astype(BF16),
        g_qn=jnp.tile(g_qn[l], MXU_DIM // DA_HEAD_DIM).reshape(1, MXU_DIM),
        g_kn=jnp.tile(g_kn[l], MXU_DIM // DA_HEAD_DIM).reshape(1, MXU_DIM),
        lam=jnp.stack([lam_q1[l], lam_k1[l], lam_q2[l], lam_k2[l]]),
        g_sub=g_sub[l], g_mh=g_mh[l],
        w_br_m=w_br_m[l], w_br_d=w_br_d[l], w_out=w_out[l],
    )

    xp, (new_k,), new_v, (new_c, new_n, new_m) = _layer(
        x_prompt.reshape(bp * tp, D_MODEL), mods, lambda r: 0, w, bp, tp, None, lam_init)

    ctx = (cache_k[:, l].reshape(bs, past, D_MODEL), cache_v[:, l].reshape(bs, past, D_MODEL),
           (state_C[:, l], state_n[:, l].reshape(bs, 2, ML_HEADS, 1, ML_HEAD_DIM),
            jnp.broadcast_to(state_m[:, l][..., None, None], (bs, 2, ML_HEADS, 1, LANES))))
    xs, _, _, _ = _layer(x_sample.reshape(bs * ts, D_MODEL), mods, lambda r: 1 + r // ts, w, bs, ts, ctx, lam_init)

    return (xp.reshape(bp, tp, D_MODEL), xs.reshape(bs, ts, D_MODEL),
            new_k.reshape(bp, 1, tp, DA_HEADS, 2, DA_HEAD_DIM),
            new_v.reshape(bp, 1, tp, DA_HEADS, DA_V_DIM),
            new_c[:, None], new_n.reshape(bp, 1, 2, ML_HEADS, ML_HEAD_DIM),
            new_m[..., 0, 0][:, None])
```

```python
import functools
import math

import jax
import jax.numpy as jnp
from jax import lax
from jax.experimental import pallas as pl
from jax.experimental.pallas import tpu as pltpu

F32 = jnp.float32
BF16 = jnp.bfloat16

D_MODEL = 1024
D_FF = 2816
N_MOD = 9
GRID_W = 64
ML_HEADS = 4
ML_HEAD_DIM = 256
DA_HEADS = 8
DA_HEAD_DIM = 64
DA_V_DIM = 128
N_GATE_COLS = 16
CHUNK = 128
ROPE_BASE = 10000.0
EPS = 1e-6

LANES = 128
GATE_PAD = LANES
MXU_DIM = 256
VMEM_LIMIT = 56 * 1024 * 1024

FFN_TM = 512
FFN_TF = 256
PROJ_TM = 256
MERGE_TM = 512
ATT_TQ = 256
ATT_TK = 256


def _params(n_axes):
    return pltpu.CompilerParams(dimension_semantics=("arbitrary",) * n_axes,
                                vmem_limit_bytes=VMEM_LIMIT)


def _dot(a, b):
    return jnp.dot(a.astype(BF16), b.astype(BF16), preferred_element_type=F32)


def _dot_nt(a, b):
    return lax.dot_general(a.astype(BF16), b.astype(BF16), (((1,), (1,)), ((), ())),
                           preferred_element_type=F32)


def _dot_tn(a, b):
    return lax.dot_general(a.astype(BF16), b.astype(BF16), (((0,), (0,)), ((), ())),
                           preferred_element_type=F32)


def _sigmoid(x):
    return 1.0 / (1.0 + jnp.exp(-x))


def _log_sigmoid(x):
    return jnp.minimum(x, 0.0) - jnp.log1p(jnp.exp(-jnp.abs(x)))


def _modulated_norm(x, g, shift, scale):
    y = x * lax.rsqrt(jnp.mean(x * x, axis=-1, keepdims=True) + EPS) * g
    return y * (1.0 + scale) + shift


def _resident(shape):
    return pl.BlockSpec(shape, lambda *_: (0,) * len(shape), pipeline_mode=pl.Buffered(1))


def _mods_kernel(c_ref, w_ref, b_ref, o_ref):
    c = c_ref[...]
    o_ref[...] = _dot(c * _sigmoid(c), w_ref[...]) + b_ref[...]


def _mods(cvecs, w_ada, b_ada):
    n = N_MOD * D_MODEL
    tn = D_MODEL
    return pl.pallas_call(
        _mods_kernel,
        grid=(n // tn,),
        in_specs=[pl.BlockSpec((8, D_MODEL), lambda j: (0, 0)),
                  pl.BlockSpec((D_MODEL, tn), lambda j: (0, j)),
                  pl.BlockSpec((1, tn), lambda j: (0, j))],
        out_specs=pl.BlockSpec((8, tn), lambda j: (0, j)),
        out_shape=jax.ShapeDtypeStruct((8, n), F32),
        compiler_params=_params(1),
        name="adaln_mods",
    )(cvecs, w_ada, b_ada.reshape(1, n))


def _ffn_kernel(x_ref, m_ref, g_ref, w1_ref, w3_ref, w2_ref, o_ref, w1_sc, w3_sc, w2_sc, hh_sc, acc_sc, *, base):
    g = pl.program_id(0)
    nf = D_FF // FFN_TF
    norm = lambda x: _modulated_norm(x, g_ref[...], m_ref[base:base + 1, :], m_ref[base + 1:base + 2, :]).astype(BF16)
    finish = lambda x, acc: x + 0.5 * m_ref[base + 2:base + 3, :] * acc

    def tile(hh, f):
        a = _dot(hh, w1_sc[f])
        b = _dot(hh, w3_sc[f])
        return _dot(a * _sigmoid(a) * b, w2_sc[f])

    @pl.when(g < nf)
    def _():
        w1_sc[g] = w1_ref[...].astype(BF16)
        w3_sc[g] = w3_ref[...].astype(BF16)
        w2_sc[g] = w2_ref[...].astype(BF16)

        @pl.when(g == 0)
        def _():
            hh_sc[...] = norm(x_ref[...])
            acc_sc[...] = jnp.zeros(acc_sc.shape, F32)

        acc_sc[...] += tile(hh_sc[...], g)

        @pl.when(g == nf - 1)
        def _():
            o_ref[...] = finish(x_ref[...], acc_sc[...])

    @pl.when(g >= nf)
    def _():
        x = x_ref[...]
        hh = norm(x)
        acc = jnp.zeros(x.shape, F32)
        for f in range(nf):
            acc = acc + tile(hh, f)
        o_ref[...] = finish(x, acc)


def _ffn(x, mods, mod_index, g, w1, w3, w2, base):
    rows = x.shape[0]
    nf = D_FF // FFN_TF
    row_tile = lambda s: jnp.maximum(s - (nf - 1), 0)
    f_tile = lambda s: jnp.minimum(s, nf - 1)
    row = lambda s: (row_tile(s), 0)
    return pl.pallas_call(
        functools.partial(_ffn_kernel, base=base),
        grid=(nf - 1 + rows // FFN_TM,),
        in_specs=[pl.BlockSpec((FFN_TM, D_MODEL), row),
                  pl.BlockSpec((None, N_MOD, D_MODEL), lambda s: (mod_index(row_tile(s) * FFN_TM), 0, 0)),
                  _resident((1, D_MODEL)),
                  pl.BlockSpec((D_MODEL, FFN_TF), lambda s: (0, f_tile(s))),
                  pl.BlockSpec((D_MODEL, FFN_TF), lambda s: (0, f_tile(s))),
                  pl.BlockSpec((FFN_TF, D_MODEL), lambda s: (f_tile(s), 0))],
        out_specs=pl.BlockSpec((FFN_TM, D_MODEL), row),
        out_shape=jax.ShapeDtypeStruct((rows, D_MODEL), F32),
        scratch_shapes=[pltpu.VMEM((nf, D_MODEL, FFN_TF), BF16), pltpu.VMEM((nf, D_MODEL, FFN_TF), BF16),
                        pltpu.VMEM((nf, FFN_TF, D_MODEL), BF16),
                        pltpu.VMEM((FFN_TM, D_MODEL), BF16), pltpu.VMEM((FFN_TM, D_MODEL), F32)],
        compiler_params=_params(1),
        name="ffn",
    )(x, mods, g.reshape(1, D_MODEL), w1, w3, w2)


def _group_norm64(x, bd, g):
    ss = _dot(x * x, bd)
    return x * lax.rsqrt(ss * (1.0 / DA_HEAD_DIM) + EPS) * g


def _rope(x, cos, sin_signed):
    first = (lax.broadcasted_iota(jnp.int32, x.shape, 1) % 32) < 16
    partner = jnp.where(first, pltpu.roll(x, LANES - 16, 1), pltpu.roll(x, 16, 1))
    return x * cos + partner * sin_signed


N_SEG = 9
GATE_LO = 4 * D_MODEL


def _seg_start(s):
    return s * D_MODEL + N_GATE_COLS * (s >= 4)


def _proj_kernel(*refs, rope):
    (x_ref, m_ref, g_ref, wt_ref, wgt_ref, bg_ref, bd_ref, gq_ref, gk_ref), refs = refs[:9], refs[9:]
    if rope:
        (cos_ref, sin_ref), refs = refs[:2], refs[2:]
        newk_ref = None
    else:
        newk_ref, refs = refs[10], refs[:10] + refs[11:]
    (mq_ref, mk_ref, mv_ref, so_ref, dq_ref, dk_ref, dv_ref, sgm_ref, sgd_ref, gates_ref,
     w_sc, wg_sc, hh_sc) = refs
    step = pl.program_id(0)
    norm = lambda: _modulated_norm(x_ref[...], g_ref[...], m_ref[3:4, :], m_ref[4:5, :]).astype(BF16)

    def gates(hh):
        y = _dot_nt(hh, wg_sc[...]) + bg_ref[...]
        col = lax.broadcasted_iota(jnp.int32, y.shape, 1)
        gates_ref[...] = jnp.where((col // ML_HEADS) % 2 == 1, _log_sigmoid(y), y)

    def qk_segment(y, g_head_ref, o_ref, cache_ref):
        bd = bd_ref[...]
        for c in range(D_MODEL // MXU_DIM):
            cols = slice(c * MXU_DIM, (c + 1) * MXU_DIM)
            z = _group_norm64(y[:, cols], bd, g_head_ref[...])
            if rope:
                z = jnp.concatenate(
                    [_rope(z[:, k * LANES:(k + 1) * LANES], cos_ref[...], sin_ref[...])
                     for k in range(MXU_DIM // LANES)], axis=1)
            o_ref[:, cols] = z.astype(o_ref.dtype)
            if cache_ref is not None:
                groups = MXU_DIM // DA_HEAD_DIM
                cache_ref[:, c * groups:(c + 1) * groups, :] = z.reshape(z.shape[0], groups, DA_HEAD_DIM)

    def segment(s, hh):
        if s in (0, 2):
            o_ref = {0: mq_ref, 2: mv_ref}[s]
            y_t = _dot_nt(w_sc[s], hh) * ((ML_HEAD_DIM ** -0.5) if s == 0 else 1.0)
            for j in range(PROJ_TM // CHUNK):
                o_ref[j] = y_t[:, j * CHUNK:(j + 1) * CHUNK].astype(o_ref.dtype)
            return
        y = _dot_nt(hh, w_sc[s])
        if s == 4:
            qk_segment(y, gq_ref, dq_ref, None)
        elif s == 5:
            qk_segment(y, gk_ref, dk_ref, newk_ref)
        elif s in (3, 7, 8):
            {3: so_ref, 7: sgm_ref, 8: sgd_ref}[s][...] = _sigmoid(y)
        else:
            o_ref = {1: mk_ref, 6: dv_ref}[s]
            o_ref[...] = y.astype(o_ref.dtype)

    for s in range(N_SEG):
        @pl.when(step == s)
        def _(s=s):
            w_sc[s] = wt_ref[...].astype(BF16)
            if s == 0:
                wg_sc[...] = jnp.zeros(wg_sc.shape, BF16)
                wg_sc[0:N_GATE_COLS, :] = wgt_ref[...].astype(BF16)
                hh_sc[...] = norm()
                gates(hh_sc[...])
            segment(s, hh_sc[...])

    @pl.when(step >= N_SEG)
    def _():
        hh = norm()
        gates(hh)
        for s in range(N_SEG):
            segment(s, hh)


def _proj(x, mods, mod_index, g, w_in_t, b_gate, bd, gq, gk, rope_tabs, seq_len):
    rows = x.shape[0]
    rope = rope_tabs is not None
    row_tile = lambda s: jnp.maximum(s - (N_SEG - 1), 0)
    row = lambda s: (row_tile(s), 0)
    tile = pl.BlockSpec((PROJ_TM, D_MODEL), row)
    elem = lambda n: (pl.Element(n), pl.Element(D_MODEL))
    seg_row = lambda s: pl.multiple_of(_seg_start(jnp.minimum(s, N_SEG - 1)), N_GATE_COLS)
    in_specs = [tile,
                pl.BlockSpec((None, N_MOD, D_MODEL), lambda s: (mod_index(row_tile(s) * PROJ_TM), 0, 0)),
                _resident((1, D_MODEL)),
                pl.BlockSpec(elem(D_MODEL), lambda s: (seg_row(s), 0)),
                pl.BlockSpec(elem(N_GATE_COLS), lambda s: (GATE_LO, 0)),
                _resident((1, GATE_PAD)),
                _resident((MXU_DIM, MXU_DIM)),
                _resident((1, MXU_DIM)),
                _resident((1, MXU_DIM))]
    args = [x, mods, g.reshape(1, D_MODEL), w_in_t, w_in_t, b_gate, bd, gq, gk]
    if rope:
        tiles_per_seq = seq_len // PROJ_TM
        tab = pl.BlockSpec((PROJ_TM, LANES), lambda s: (row_tile(s) % tiles_per_seq, 0))
        in_specs += [tab, tab]
        args += list(rope_tabs)
    dtypes = [BF16, BF16, BF16, F32, BF16, BF16, BF16 if rope else F32, F32, F32]
    out_shape = [jax.ShapeDtypeStruct((rows, D_MODEL), dt) for dt in dtypes]
    out_shape.append(jax.ShapeDtypeStruct((rows, GATE_PAD), F32))
    out_specs = [tile] * 9 + [pl.BlockSpec((PROJ_TM, GATE_PAD), row)]
    slabs = PROJ_TM // CHUNK
    for j in (0, 2):
        out_shape[j] = jax.ShapeDtypeStruct((rows // CHUNK, D_MODEL, CHUNK), BF16)
        out_specs[j] = pl.BlockSpec((slabs, D_MODEL, CHUNK), lambda s: (row_tile(s), 0, 0))
    if not rope:
        n_groups = 2 * DA_HEADS
        out_shape.append(jax.ShapeDtypeStruct((rows, n_groups, DA_HEAD_DIM), F32))
        out_specs.append(pl.BlockSpec((PROJ_TM, n_groups, DA_HEAD_DIM), lambda s: (row_tile(s), 0, 0)))
    return pl.pallas_call(
        functools.partial(_proj_kernel, rope=rope),
        grid=(N_SEG - 1 + rows // PROJ_TM,),
        in_specs=in_specs,
        out_specs=out_specs,
        out_shape=out_shape,
        scratch_shapes=[pltpu.VMEM((N_SEG, D_MODEL, D_MODEL), BF16), pltpu.VMEM((GATE_PAD, D_MODEL), BF16),
                        pltpu.VMEM((PROJ_TM, D_MODEL), BF16)],
        compiler_params=_params(1),
        name="mixer_in_proj",
    )(*args)


def _lambda(lam_ref, lam_init):
    lam = lam_ref[...]
    s1 = jnp.sum(lam[0:1, :] * lam[1:2, :], axis=1, keepdims=True)
    s2 = jnp.sum(lam[2:3, :] * lam[3:4, :], axis=1, keepdims=True)
    return jnp.exp(s1) - jnp.exp(s2) + lam_init


def _attn_kernel(*refs, cached, lam_init):
    if cached:
        q_ref, k_ref, v_ref, ck_ref, cv_ref, lam_ref, gs_ref, o_ref, kall_sc, vt_sc, s_sc = refs
    else:
        q_ref, k_ref, v_ref, lam_ref, gs_ref, o_ref, kall_sc, vt_sc, s_sc = refs
    seq = k_ref.shape[0]

    @pl.when(pl.program_id(1) == 0)
    def _():
        for h in range(DA_HEADS):
            cols = slice(h * DA_V_DIM, (h + 1) * DA_V_DIM)
            kall_sc[h, 0:seq, :] = k_ref[:, cols].astype(BF16)
            vt_sc[h, :, 0:seq] = v_ref[:, cols].astype(F32).T.astype(BF16)
            if cached:
                past = ck_ref.shape[0]
                kall_sc[h, seq:seq + past, :] = ck_ref[:, cols].astype(BF16)
                vt_sc[h, :, seq:seq + past] = cv_ref[:, cols].T.astype(BF16)

    lam = _lambda(lam_ref, lam_init)
    scale = DA_HEAD_DIM ** -0.5
    lane = lax.broadcasted_iota(jnp.int32, (1, DA_V_DIM), 1)
    comp_masks = [lane < DA_HEAD_DIM, lane >= DA_HEAD_DIM]
    tq = q_ref.shape[0]
    n_keys = kall_sc.shape[1]
    n_tiles = n_keys // ATT_TK

    def stacked_q(h):
        q = q_ref[:, h * DA_V_DIM:(h + 1) * DA_V_DIM].astype(BF16) * scale
        return jnp.concatenate([jnp.where(m, q, jnp.zeros_like(q)) for m in comp_masks], axis=0)

    def score_tile(h, j, qq, m8):
        rows = slice(j * ATT_TK, (j + 1) * ATT_TK)
        st = _dot_nt(kall_sc[h, rows, :], qq)
        s_sc[h % 2, rows, :] = st
        t8 = jnp.max(st.reshape(ATT_TK // 8, 8, 2 * tq), axis=0)
        return t8 if m8 is None else jnp.maximum(m8, t8)

    def prob_tile(h, j, mx, d8, pv):
        rows = slice(j * ATT_TK, (j + 1) * ATT_TK)
        e = jnp.exp(s_sc[h % 2, rows, :] - mx)
        s8 = jnp.sum(e.reshape(ATT_TK // 8, 8, 2 * tq), axis=0)
        p = _dot(vt_sc[h, :, rows], e)
        return (s8 if d8 is None else d8 + s8), (p if pv is None else pv + p)

    qq = stacked_q(0)
    m8 = None
    for j in range(n_tiles):
        m8 = score_tile(0, j, qq, m8)
    for h in range(DA_HEADS):
        mx = jnp.max(m8, axis=0, keepdims=True)
        if h + 1 < DA_HEADS:
            qq = stacked_q(h + 1)
        m8, d8, pv = None, None, None
        for j in range(n_tiles):
            if h + 1 < DA_HEADS:
                m8 = score_tile(h + 1, j, qq, m8)
            d8, pv = prob_tile(h, j, mx, d8, pv)
        pv = pv * (1.0 / jnp.sum(d8, axis=0, keepdims=True))
        out_t = pv[:, :tq] - lam * pv[:, tq:]
        out_t = out_t * lax.rsqrt(jnp.mean(out_t * out_t, axis=0, keepdims=True) + EPS)
        out = out_t.T * gs_ref[...]
        o_ref[:, h * DA_V_DIM:(h + 1) * DA_V_DIM] = (out * (1.0 - lam_init)).astype(o_ref.dtype)


def _attn(q, k, v, cache, lam_vecs, g_sub, batch, seq_len, lam_init):
    rows = q.shape[0]
    nq = seq_len // ATT_TQ
    q_spec = pl.BlockSpec((ATT_TQ, D_MODEL), lambda b, i: (b * nq + i, 0))
    kv_spec = pl.BlockSpec((seq_len, D_MODEL), lambda b, i: (b, 0))
    in_specs = [q_spec, kv_spec, kv_spec]
    args = [q, k, v]
    n_keys = seq_len
    if cache is not None:
        past = cache[0].shape[1]
        n_keys += past
        c_spec = pl.BlockSpec((None, past, D_MODEL), lambda b, i: (b, 0, 0))
        in_specs += [c_spec, c_spec]
        args += list(cache)
    in_specs += [_resident((4, DA_HEAD_DIM)), _resident((1, DA_V_DIM))]
    args += [lam_vecs, g_sub.reshape(1, DA_V_DIM)]
    return pl.pallas_call(
        functools.partial(_attn_kernel, cached=cache is not None, lam_init=lam_init),
        grid=(batch, nq),
        in_specs=in_specs,
        out_specs=q_spec,
        out_shape=jax.ShapeDtypeStruct((rows, D_MODEL), BF16),
        scratch_shapes=[pltpu.VMEM((DA_HEADS, n_keys, DA_V_DIM), BF16),
                        pltpu.VMEM((DA_HEADS, DA_V_DIM, n_keys), BF16),
                        pltpu.VMEM((2, n_keys, 2 * ATT_TQ), F32)],
        compiler_params=_params(2),
        name="diff_attention",
    )(*args)


def _per_chain(fn, a, b):
    return jnp.stack([fn(a[i], b[i]) for i in range(a.shape[0])])


ML_EXT = 8


def _rows(x):
    return jnp.stack([x[i:i + 1, :] for i in range(x.shape[0])])


def _chunk_scan(x, reverse_rows):
    n = x.shape[1]
    lane = lax.broadcasted_iota(jnp.int32, x.shape, 1)
    k = 1
    while k < n:
        fwd = jnp.where(lane >= k, pltpu.roll(x, k, 1), 0.0)
        bwd = jnp.where(lane < n - k, pltpu.roll(x, n - k, 1), 0.0)
        x = x + jnp.where(reverse_rows, bwd, fwd)
        k *= 2
    return x


def _mlstm_step(k, q_t, v_t, ic, fc, CT, m, seen_t, reverse_rows):
    B, L = ic.shape
    d = k.shape[2]
    b2 = _chunk_scan(fc, reverse_rows)
    u2 = ic - b2
    u_t = jnp.concatenate([u2, jnp.zeros((L - B, L), F32)], axis=0).T
    u_col = jnp.stack([jnp.broadcast_to(u_t[:, i:i + 1], (L, L)) for i in range(B)])
    b, i_g, f_g = _rows(b2), _rows(ic), _rows(fc)
    b_last = jnp.sum(f_g, axis=2, keepdims=True)

    log_d = jnp.where(seen_t, b + u_col, -jnp.inf)
    a = b + m
    m_t = jnp.maximum(a, jnp.max(log_d, axis=1, keepdims=True))
    dmat = jnp.exp(log_d - m_t)
    inter = jnp.exp(a - m_t)
    sc = _per_chain(_dot, k, q_t) * dmat
    cq = _per_chain(_dot, CT, q_t)
    num = _per_chain(_dot, v_t, sc) + inter * cq[:, :d, :]
    den = jnp.sum(sc, axis=1, keepdims=True) + inter * cq[:, d:d + 1, :]
    h_t = num * (1.0 / jnp.maximum(jnp.abs(den), jnp.exp(-m_t)))

    g = b_last - b + i_g
    m_new = jnp.maximum(b_last + m, jnp.max(g, axis=2, keepdims=True))
    w = jnp.exp(g - m_new)
    decay = jnp.exp(b_last + m - m_new)
    vw = jnp.concatenate([v_t.astype(F32), jnp.ones((B, ML_EXT, L), F32)], axis=1) * w
    CT_new = decay * CT + _per_chain(_dot, vw, k)
    return h_t, CT_new, m_new


def _mlstm_kernel(*refs, seq_len, has_state, emit_state):
    q_ref, k_ref, v_ref, so_ref, g_ref, gmh_ref = refs[:6]
    refs = refs[6:]
    if has_state:
        (c0_ref, n0_ref, m0_ref), refs = refs[:3], refs[3:]
    hm_ref, refs = refs[0], refs[1:]
    if emit_state:
        (c_out_ref, n_out_ref, m_out_ref), refs = refs[:3], refs[3:]
    ct_sc, m_sc, gr_sc, h_sc = refs

    nc = seq_len // CHUNK
    d_head = ML_HEAD_DIM
    chains = [(d, h) for d in range(2) for h in range(ML_HEADS)]
    n_chain = len(chains)
    for i, (d, h) in enumerate(chains):
        if has_state:
            ct_sc[i, 0:d_head, :] = c0_ref[d, h].T
            ct_sc[i, d_head:d_head + ML_EXT, :] = jnp.broadcast_to(n0_ref[d, h], (ML_EXT, d_head))
            m_sc[i] = m0_ref[d, h][:, 0:1]
        else:
            ct_sc[i] = jnp.zeros((d_head + ML_EXT, d_head), F32)
            m_sc[i] = jnp.zeros((1, 1), F32)
    for c in range(nc):
        gr_sc[c] = g_ref[c * CHUNK:(c + 1) * CHUNK, :].T

    s_idx = lax.broadcasted_iota(jnp.int32, (CHUNK, CHUNK), 0)
    t_idx = lax.broadcasted_iota(jnp.int32, (CHUNK, CHUNK), 1)
    seen_t = jnp.stack([s_idx <= t_idx] * ML_HEADS + [s_idx >= t_idx] * ML_HEADS)
    reverse_rows = lax.broadcasted_iota(jnp.int32, (n_chain, CHUNK), 0) >= ML_HEADS

    def step(c_fwd, c_bwd, rows_of):
        chunk_of = (c_fwd, c_bwd)
        g_fwd, g_bwd = gr_sc[c_fwd], gr_sc[c_bwd]
        lo = 2 * ML_HEADS
        ic = jnp.concatenate([g_fwd[0:ML_HEADS], g_bwd[lo:lo + ML_HEADS]], axis=0)
        fc = jnp.concatenate([g_fwd[ML_HEADS:lo], g_bwd[lo + ML_HEADS:2 * lo]], axis=0)
        head = [slice(h * d_head, (h + 1) * d_head) for _, h in chains]
        stack = lambda pick: jnp.stack([pick(i, chunk_of[d]) for i, (d, _) in enumerate(chains)])
        h_t, CT_new, m_new = _mlstm_step(
            stack(lambda i, c: k_ref[rows_of(c), head[i]]),
            stack(lambda i, c: q_ref[c, head[i], :]),
            stack(lambda i, c: v_ref[c, head[i], :]),
            ic, fc, ct_sc[...], m_sc[...], seen_t, reverse_rows)
        ct_sc[...] = CT_new
        m_sc[...] = m_new
        for i, (d, _) in enumerate(chains):
            h_sc[d, rows_of(chunk_of[d]), head[i]] = h_t[i].T

    if nc <= 2:
        for c in range(nc):
            step(c, nc - 1 - c, lambda cc: slice(cc * CHUNK, (cc + 1) * CHUNK))
    else:
        def body(c, carry):
            step(c, nc - 1 - c, lambda cc: pl.ds(pl.multiple_of(cc * CHUNK, CHUNK), CHUNK))
            return carry
        lax.fori_loop(0, nc, body, 0)

    for c in range(nc):
        rows = slice(c * CHUNK, (c + 1) * CHUNK)
        for h in range(ML_HEADS):
            hcols = slice(h * d_head, (h + 1) * d_head)
            hsum = h_sc[0, rows, hcols] + h_sc[1, rows, hcols]
            hn = hsum * lax.rsqrt(jnp.mean(hsum * hsum, axis=1, keepdims=True) + EPS) * gmh_ref[h]
            hm_ref[rows, hcols] = (hn * so_ref[rows, hcols]).astype(hm_ref.dtype)
    if emit_state:
        for i, (d, h) in enumerate(chains):
            c_out_ref[d, h] = ct_sc[i, 0:d_head, :].T
            n_out_ref[d, h] = ct_sc[i, d_head:d_head + 1, :]
            m_out_ref[d, h] = jnp.broadcast_to(m_sc[i], (1, LANES))


def _mlstm(q_t, k, v_t, so, gates, g_mh, state, batch, seq_len, emit_state):
    rows = k.shape[0]
    d = ML_HEAD_DIM
    nc = seq_len // CHUNK
    tile = pl.BlockSpec((seq_len, D_MODEL), lambda b: (b, 0))
    tile_t = pl.BlockSpec((nc, D_MODEL, CHUNK), lambda b: (b, 0, 0))
    in_specs = [tile_t, tile, tile_t, tile,
                pl.BlockSpec((seq_len, GATE_PAD), lambda b: (b, 0)),
                _resident((ML_HEADS, 1, d))]
    args = [q_t, k, v_t, so, gates, g_mh.reshape(ML_HEADS, 1, d)]
    c_spec = pl.BlockSpec((None, 2, ML_HEADS, d, d), lambda b: (b, 0, 0, 0, 0))
    n_spec = pl.BlockSpec((None, 2, ML_HEADS, 1, d), lambda b: (b, 0, 0, 0, 0))
    m_spec = pl.BlockSpec((None, 2, ML_HEADS, 1, LANES), lambda b: (b, 0, 0, 0, 0))
    if state is not None:
        in_specs += [c_spec, n_spec, m_spec]
        args += list(state)
    out_specs = [tile]
    out_shape = [jax.ShapeDtypeStruct((rows, D_MODEL), BF16)]
    if emit_state:
        out_specs += [c_spec, n_spec, m_spec]
        out_shape += [jax.ShapeDtypeStruct((batch, 2, ML_HEADS, d, d), F32),
                      jax.ShapeDtypeStruct((batch, 2, ML_HEADS, 1, d), F32),
                      jax.ShapeDtypeStruct((batch, 2, ML_HEADS, 1, LANES), F32)]
    n_state = 2 * ML_HEADS
    return pl.pallas_call(
        functools.partial(_mlstm_kernel, seq_len=seq_len, has_state=state is not None, emit_state=emit_state),
        grid=(batch,),
        in_specs=in_specs,
        out_specs=out_specs,
        out_shape=out_shape,
        scratch_shapes=[pltpu.VMEM((n_state, d + ML_EXT, d), F32), pltpu.VMEM((n_state, 1, 1), F32),
                        pltpu.VMEM((nc, GATE_PAD, CHUNK), F32),
                        pltpu.VMEM((2, seq_len, D_MODEL), F32)],
        compiler_params=_params(1),
        name="mlstm",
    )(*args)


def _merge_kernel(x_ref, m_ref, hm_ref, att_ref, sgm_ref, sgd_ref, wm_ref, wd_ref, wo_ref, o_ref, w_sc):
    @pl.when(pl.program_id(0) == 0)
    def _():
        for j, w_ref in enumerate((wm_ref, wd_ref, wo_ref)):
            w_sc[j] = w_ref[...].astype(BF16)

    y = sgm_ref[...] * _dot(hm_ref[...], w_sc[0]) + sgd_ref[...] * _dot(att_ref[...], w_sc[1])
    o_ref[...] = x_ref[...] + m_ref[5:6, :] * _dot(y, w_sc[2])


def _merge(x, mods, mod_index, hm, att, sgm, sgd, wm, wd, wo):
    rows = x.shape[0]
    tile = pl.BlockSpec((MERGE_TM, D_MODEL), lambda i: (i, 0))
    w_spec = _resident((D_MODEL, D_MODEL))
    return pl.pallas_call(
        _merge_kernel,
        grid=(rows // MERGE_TM,),
        in_specs=[tile, pl.BlockSpec((None, N_MOD, D_MODEL), lambda i: (mod_index(i * MERGE_TM), 0, 0)),
                  tile, tile, tile, tile, w_spec, w_spec, w_spec],
        out_specs=tile,
        out_shape=jax.ShapeDtypeStruct((rows, D_MODEL), F32),
        scratch_shapes=[pltpu.VMEM((3, D_MODEL, D_MODEL), BF16)],
        compiler_params=_params(1),
        name="branch_merge",
    )(x, mods, hm, att, sgm, sgd, wm, wd, wo)


def _rope_tables(seq_len):
    lane = jnp.arange(LANES)
    r = lane % 32
    freqs = jnp.power(ROPE_BASE, -(r % 16).astype(F32) / 16.0)
    tok = jnp.arange(seq_len)
    pos = jnp.where((lane % 64 < 32)[None, :], (tok // GRID_W)[:, None], (tok % GRID_W)[:, None]).astype(F32)
    ang = pos * freqs[None, :]
    sign = jnp.where(r < 16, -1.0, 1.0).astype(F32)
    return jnp.cos(ang), jnp.sin(ang) * sign[None, :]


def _layer(x, mods, mod_index, w, batch, seq_len, ctx, lam_init):
    x = _ffn(x, mods, mod_index, w["g_norm"][0], w["ffn1_w1"], w["ffn1_w3"], w["ffn1_w2"], base=0)
    rope_tabs = None if ctx is None else _rope_tables(seq_len)
    mq, mk, mv, so, dq, dk, dv, sgm, sgd, gates, *new_k = _proj(
        x, mods, mod_index, w["g_norm"][1], w["w_in_t"], w["b_gate"], w["bd"], w["g_qn"], w["g_kn"],
        rope_tabs, seq_len)
    cache = None if ctx is None else (ctx[0], ctx[1])
    att = _attn(dq, dk, dv, cache, w["lam"], w["g_sub"], batch, seq_len, lam_init)
    state = None if ctx is None else ctx[2]
    res = _mlstm(mq, mk, mv, so, gates, w["g_mh"], state, batch, seq_len, emit_state=ctx is None)
    x = _merge(x, mods, mod_index, res[0], att, sgm, sgd, w["w_br_m"], w["w_br_d"], w["w_out"])
    x = _ffn(x, mods, mod_index, w["g_norm"][2], w["ffn2_w1"], w["ffn2_w3"], w["ffn2_w2"], base=6)
    return x, new_k, dv, res[1:]


def kernel(x_prompt, x_sample, c, cache_k, cache_v, state_C, state_n, state_m, c_ctx, w_ada, b_ada, g_norm, ffn1_w1, ffn1_w3, ffn1_w2, ffn2_w1, ffn2_w3, ffn2_w2, w_in, b_gate, g_qn, g_kn, lam_q1, lam_k1, lam_q2, lam_k2, g_sub, g_mh, w_br_m, w_br_d, w_out):
    depth = w_ada.shape[0]
    assert depth == 1
    l = 0
    bp, tp, _ = x_prompt.shape
    bs, ts, _ = x_sample.shape
    past = cache_k.shape[2]
    lam_init = 0.8 - 0.6 * math.exp(-0.3 * l)

    cvecs = jnp.concatenate([c_ctx[None, :], c, jnp.zeros((8 - 1 - bs, D_MODEL), F32)], axis=0)
    mods = _mods(cvecs, w_ada[l], b_ada[l]).reshape(8, N_MOD, D_MODEL)

    group = jnp.arange(MXU_DIM) // DA_HEAD_DIM
    w = dict(
        g_norm=g_norm[l],
        ffn1_w1=ffn1_w1[l], ffn1_w3=ffn1_w3[l], ffn1_w2=ffn1_w2[l],
        ffn2_w1=ffn2_w1[l], ffn2_w3=ffn2_w3[l], ffn2_w2=ffn2_w2[l],
        w_in_t=w_in[l].T,
        b_gate=jnp.pad(b_gate[l], (0, GATE_PAD - N_GATE_COLS)).reshape(1, GATE_PAD),
        bd=(group[:, None] == group[None, :]).astype(BF16),
        g_qn=jnp.tile(g_qn[l], MXU_DIM // DA_HEAD_DIM).reshape(1, MXU_DIM),
        g_kn=jnp.tile(g_kn[l], MXU_DIM // DA_HEAD_DIM).reshape(1, MXU_DIM),
        lam=jnp.stack([lam_q1[l], lam_k1[l], lam_q2[l], lam_k2[l]]),
        g_sub=g_sub[l], g_mh=g_mh[l],
        w_br_m=w_br_m[l], w_br_d=w_br_d[l], w_out=w_out[l],
    )

    xp, (new_k,), new_v, (new_c, new_n, new_m) = _layer(
        x_prompt.reshape(bp * tp, D_MODEL), mods, lambda r: 0, w, bp, tp, None, lam_init)

    ctx = (cache_k[:, l].reshape(bs, past, D_MODEL), cache_v[:, l].reshape(bs, past, D_MODEL),
           (state_C[:, l], state_n[:, l].reshape(bs, 2, ML_HEADS, 1, ML_HEAD_DIM),
            jnp.broadcast_to(state_m[:, l][..., None, None], (bs, 2, ML_HEADS, 1, LANES))))
    xs, _, _, _ = _layer(x_sample.reshape(bs * ts, D_MODEL), mods, lambda r: 1 + r // ts, w, bs, ts, ctx, lam_init)

    return (xp.reshape(bp, tp, D_MODEL), xs.reshape(bs, ts, D_MODEL),
            new_k.reshape(bp, 1, tp, DA_HEADS, 2, DA_HEAD_DIM),
            new_v.reshape(bp, 1, tp, DA_HEADS, DA_V_DIM),
            new_c[:, None], new_n.reshape(bp, 1, 2, ML_HEADS, ML_HEAD_DIM),
            new_m[..., 0, 0][:, None])
```

```python
import functools
import math

import jax
import jax.numpy as jnp
from jax import lax
from jax.experimental import pallas as pl
from jax.experimental.pallas import tpu as pltpu

F32 = jnp.float32
BF16 = jnp.bfloat16

D_MODEL = 1024
D_FF = 2816
N_MOD = 9
GRID_W = 64
ML_HEADS = 4
ML_HEAD_DIM = 256
DA_HEADS = 8
DA_HEAD_DIM = 64
DA_V_DIM = 128
N_GATE_COLS = 16
CHUNK = 128
ROPE_BASE = 10000.0
QK_LOG2_SCALE = DA_HEAD_DIM ** -0.5 * math.log2(math.e)
EPS = 1e-6

LANES = 128
GATE_PAD = LANES
MXU_DIM = 256
VMEM_LIMIT = 56 * 1024 * 1024

FFN_TM = 512
FFN_TF = 256
PROJ_TM = 256
MERGE_TM = 512
ATT_TQ = 256
ATT_TK = 256


def _params(n_axes):
    return pltpu.CompilerParams(dimension_semantics=("arbitrary",) * n_axes,
                                vmem_limit_bytes=VMEM_LIMIT)


def _dot(a, b):
    return jnp.dot(a.astype(BF16), b.astype(BF16), preferred_element_type=F32)


def _dot_nt(a, b):
    return lax.dot_general(a.astype(BF16), b.astype(BF16), (((1,), (1,)), ((), ())),
                           preferred_element_type=F32)


def _dot_tn(a, b):
    return lax.dot_general(a.astype(BF16), b.astype(BF16), (((0,), (0,)), ((), ())),
                           preferred_element_type=F32)


def _sigmoid(x):
    return 1.0 / (1.0 + jnp.exp(-x))


def _log_sigmoid(x):
    return jnp.minimum(x, 0.0) - jnp.log1p(jnp.exp(-jnp.abs(x)))


def _modulated_norm(x, g, shift, scale):
    y = x * lax.rsqrt(jnp.mean(x * x, axis=-1, keepdims=True) + EPS) * g
    return y * (1.0 + scale) + shift


def _resident(shape):
    return pl.BlockSpec(shape, lambda *_: (0,) * len(shape), pipeline_mode=pl.Buffered(1))


def _mods_kernel(c_ref, w_ref, b_ref, o_ref):
    c = c_ref[...]
    o_ref[...] = _dot(c * _sigmoid(c), w_ref[...]) + b_ref[...]


def _mods(cvecs, w_ada, b_ada):
    n = N_MOD * D_MODEL
    tn = D_MODEL
    return pl.pallas_call(
        _mods_kernel,
        grid=(n // tn,),
        in_specs=[pl.BlockSpec((8, D_MODEL), lambda j: (0, 0)),
                  pl.BlockSpec((D_MODEL, tn), lambda j: (0, j)),
                  pl.BlockSpec((1, tn), lambda j: (0, j))],
        out_specs=pl.BlockSpec((8, tn), lambda j: (0, j)),
        out_shape=jax.ShapeDtypeStruct((8, n), F32),
        compiler_params=_params(1),
        name="adaln_mods",
    )(cvecs, w_ada, b_ada.reshape(1, n))


def _ffn_kernel(x_ref, m_ref, g_ref, w1_ref, w3_ref, w2_ref, o_ref, w1_sc, w3_sc, w2_sc, hh_sc, acc_sc, *, base):
    g = pl.program_id(0)
    nf = D_FF // FFN_TF
    norm = lambda x: _modulated_norm(x, g_ref[...], m_ref[base:base + 1, :], m_ref[base + 1:base + 2, :]).astype(BF16)
    finish = lambda x, acc: x + 0.5 * m_ref[base + 2:base + 3, :] * acc

    def tile(hh, f):
        a = _dot(hh, w1_sc[f])
        b = _dot(hh, w3_sc[f])
        return _dot(a * _sigmoid(a) * b, w2_sc[f])

    @pl.when(g < nf)
    def _():
        w1_sc[g] = w1_ref[...].astype(BF16)
        w3_sc[g] = w3_ref[...].astype(BF16)
        w2_sc[g] = w2_ref[...].astype(BF16)

        @pl.when(g == 0)
        def _():
            hh_sc[...] = norm(x_ref[...])
            acc_sc[...] = jnp.zeros(acc_sc.shape, F32)

        acc_sc[...] += tile(hh_sc[...], g)

        @pl.when(g == nf - 1)
        def _():
            o_ref[...] = finish(x_ref[...], acc_sc[...])

    @pl.when(g >= nf)
    def _():
        x = x_ref[...]
        hh = norm(x)
        acc = jnp.zeros(x.shape, F32)
        for f in range(nf):
            acc = acc + tile(hh, f)
        o_ref[...] = finish(x, acc)


def _ffn(x, mods, mod_index, g, w1, w3, w2, base):
    rows = x.shape[0]
    nf = D_FF // FFN_TF
    row_tile = lambda s: jnp.maximum(s - (nf - 1), 0)
    f_tile = lambda s: jnp.minimum(s, nf - 1)
    row = lambda s: (row_tile(s), 0)
    return pl.pallas_call(
        functools.partial(_ffn_kernel, base=base),
        grid=(nf - 1 + rows // FFN_TM,),
        in_specs=[pl.BlockSpec((FFN_TM, D_MODEL), row),
                  pl.BlockSpec((None, N_MOD, D_MODEL), lambda s: (mod_index(row_tile(s) * FFN_TM), 0, 0)),
                  _resident((1, D_MODEL)),
                  pl.BlockSpec((D_MODEL, FFN_TF), lambda s: (0, f_tile(s))),
                  pl.BlockSpec((D_MODEL, FFN_TF), lambda s: (0, f_tile(s))),
                  pl.BlockSpec((FFN_TF, D_MODEL), lambda s: (f_tile(s), 0))],
        out_specs=pl.BlockSpec((FFN_TM, D_MODEL), row),
        out_shape=jax.ShapeDtypeStruct((rows, D_MODEL), F32),
        scratch_shapes=[pltpu.VMEM((nf, D_MODEL, FFN_TF), BF16), pltpu.VMEM((nf, D_MODEL, FFN_TF), BF16),
                        pltpu.VMEM((nf, FFN_TF, D_MODEL), BF16),
                        pltpu.VMEM((FFN_TM, D_MODEL), BF16), pltpu.VMEM((FFN_TM, D_MODEL), F32)],
        compiler_params=_params(1),
        name="ffn",
    )(x, mods, g.reshape(1, D_MODEL), w1, w3, w2)


def _group_norm64(x, bd, g):
    ss = _dot(x * x, bd)
    return x * lax.rsqrt(ss * (1.0 / DA_HEAD_DIM) + EPS) * g


def _rope(x, cos, sin_signed):
    first = (lax.broadcasted_iota(jnp.int32, x.shape, 1) % 32) < 16
    partner = jnp.where(first, pltpu.roll(x, LANES - 16, 1), pltpu.roll(x, 16, 1))
    return x * cos + partner * sin_signed


N_SEG = 9
GATE_LO = 4 * D_MODEL


def _seg_start(s):
    return s * D_MODEL + N_GATE_COLS * (s >= 4)


def _proj_kernel(*refs, rope):
    (x_ref, m_ref, g_ref, wt_ref, wgt_ref, bg_ref, bd_ref, gq_ref, gk_ref), refs = refs[:9], refs[9:]
    if rope:
        (cos_ref, sin_ref), refs = refs[:2], refs[2:]
        newk_ref = None
    else:
        newk_ref, refs = refs[10], refs[:10] + refs[11:]
    (mq_ref, mk_ref, mv_ref, so_ref, dq_ref, dk_ref, dv_ref, sgm_ref, sgd_ref, gates_ref,
     w_sc, wg_sc, hh_sc) = refs
    step = pl.program_id(0)
    norm = lambda: _modulated_norm(x_ref[...], g_ref[...], m_ref[3:4, :], m_ref[4:5, :]).astype(BF16)

    def gates(hh):
        y = _dot_nt(hh, wg_sc[...]) + bg_ref[...]
        col = lax.broadcasted_iota(jnp.int32, y.shape, 1)
        gates_ref[...] = jnp.where((col // ML_HEADS) % 2 == 1, _log_sigmoid(y), y)

    def qk_segment(y, g_head_ref, o_ref, cache_ref):
        bd = bd_ref[...]
        for c in range(D_MODEL // MXU_DIM):
            cols = slice(c * MXU_DIM, (c + 1) * MXU_DIM)
            z = _group_norm64(y[:, cols], bd, g_head_ref[...])
            if rope:
                z = jnp.concatenate(
                    [_rope(z[:, k * LANES:(k + 1) * LANES], cos_ref[...], sin_ref[...])
                     for k in range(MXU_DIM // LANES)], axis=1)
            o_ref[:, cols] = z.astype(o_ref.dtype)
            if cache_ref is not None:
                groups = MXU_DIM // DA_HEAD_DIM
                cache_ref[:, c * groups:(c + 1) * groups, :] = z.reshape(z.shape[0], groups, DA_HEAD_DIM)

    def segment(s, hh):
        if s in (0, 2):
            o_ref = {0: mq_ref, 2: mv_ref}[s]
            y_t = _dot_nt(w_sc[s], hh) * ((ML_HEAD_DIM ** -0.5) if s == 0 else 1.0)
            for j in range(PROJ_TM // CHUNK):
                o_ref[j] = y_t[:, j * CHUNK:(j + 1) * CHUNK].astype(o_ref.dtype)
            return
        y = _dot_nt(hh, w_sc[s])
        if s == 4:
            qk_segment(y, gq_ref, dq_ref, None)
        elif s == 5:
            qk_segment(y, gk_ref, dk_ref, newk_ref)
        elif s in (3, 7, 8):
            {3: so_ref, 7: sgm_ref, 8: sgd_ref}[s][...] = _sigmoid(y)
        else:
            o_ref = {1: mk_ref, 6: dv_ref}[s]
            o_ref[...] = y.astype(o_ref.dtype)

    for s in range(N_SEG):
        @pl.when(step == s)
        def _(s=s):
            w_sc[s] = wt_ref[...].astype(BF16)
            if s == 0:
                wg_sc[...] = jnp.zeros(wg_sc.shape, BF16)
                wg_sc[0:N_GATE_COLS, :] = wgt_ref[...].astype(BF16)
                hh_sc[...] = norm()
                gates(hh_sc[...])
            segment(s, hh_sc[...])

    @pl.when(step >= N_SEG)
    def _():
        hh = norm()
        gates(hh)
        for s in range(N_SEG):
            segment(s, hh)


def _proj(x, mods, mod_index, g, w_in_t, b_gate, bd, gq, gk, rope_tabs, seq_len):
    rows = x.shape[0]
    rope = rope_tabs is not None
    row_tile = lambda s: jnp.maximum(s - (N_SEG - 1), 0)
    row = lambda s: (row_tile(s), 0)
    tile = pl.BlockSpec((PROJ_TM, D_MODEL), row)
    elem = lambda n: (pl.Element(n), pl.Element(D_MODEL))
    seg_row = lambda s: pl.multiple_of(_seg_start(jnp.minimum(s, N_SEG - 1)), N_GATE_COLS)
    in_specs = [tile,
                pl.BlockSpec((None, N_MOD, D_MODEL), lambda s: (mod_index(row_tile(s) * PROJ_TM), 0, 0)),
                _resident((1, D_MODEL)),
                pl.BlockSpec(elem(D_MODEL), lambda s: (seg_row(s), 0)),
                pl.BlockSpec(elem(N_GATE_COLS), lambda s: (GATE_LO, 0)),
                _resident((1, GATE_PAD)),
                _resident((MXU_DIM, MXU_DIM)),
                _resident((1, MXU_DIM)),
                _resident((1, MXU_DIM))]
    args = [x, mods, g.reshape(1, D_MODEL), w_in_t, w_in_t, b_gate, bd, gq, gk]
    if rope:
        tiles_per_seq = seq_len // PROJ_TM
        tab = pl.BlockSpec((PROJ_TM, LANES), lambda s: (row_tile(s) % tiles_per_seq, 0))
        in_specs += [tab, tab]
        args += list(rope_tabs)
    dtypes = [BF16, BF16, BF16, F32, BF16, BF16, BF16 if rope else F32, F32, F32]
    out_shape = [jax.ShapeDtypeStruct((rows, D_MODEL), dt) for dt in dtypes]
    out_shape.append(jax.ShapeDtypeStruct((rows, GATE_PAD), F32))
    out_specs = [tile] * 9 + [pl.BlockSpec((PROJ_TM, GATE_PAD), row)]
    slabs = PROJ_TM // CHUNK
    for j in (0, 2):
        out_shape[j] = jax.ShapeDtypeStruct((rows // CHUNK, D_MODEL, CHUNK), BF16)
        out_specs[j] = pl.BlockSpec((slabs, D_MODEL, CHUNK), lambda s: (row_tile(s), 0, 0))
    if not rope:
        n_groups = 2 * DA_HEADS
        out_shape.append(jax.ShapeDtypeStruct((rows, n_groups, DA_HEAD_DIM), F32))
        out_specs.append(pl.BlockSpec((PROJ_TM, n_groups, DA_HEAD_DIM), lambda s: (row_tile(s), 0, 0)))
    return pl.pallas_call(
        functools.partial(_proj_kernel, rope=rope),
        grid=(N_SEG - 1 + rows // PROJ_TM,),
        in_specs=in_specs,
        out_specs=out_specs,
        out_shape=out_shape,
        scratch_shapes=[pltpu.VMEM((N_SEG, D_MODEL, D_MODEL), BF16), pltpu.VMEM((GATE_PAD, D_MODEL), BF16),
                        pltpu.VMEM((PROJ_TM, D_MODEL), BF16)],
        compiler_params=_params(1),
        name="mixer_in_proj",
    )(*args)


def _lambda(lam_ref, lam_init):
    lam = lam_ref[...]
    s1 = jnp.sum(lam[0:1, :] * lam[1:2, :], axis=1, keepdims=True)
    s2 = jnp.sum(lam[2:3, :] * lam[3:4, :], axis=1, keepdims=True)
    return jnp.exp(s1) - jnp.exp(s2) + lam_init


def _attn_kernel(*refs, cached, lam_init):
    if cached:
        q_ref, k_ref, v_ref, ck_ref, cv_ref, lam_ref, gs_ref, o_ref, kall_sc, vt_sc, s_sc = refs
    else:
        q_ref, k_ref, v_ref, lam_ref, gs_ref, o_ref, kall_sc, vt_sc, s_sc = refs
    seq = k_ref.shape[0]

    @pl.when(pl.program_id(1) == 0)
    def _():
        for h in range(DA_HEADS):
            cols = slice(h * DA_V_DIM, (h + 1) * DA_V_DIM)
            kall_sc[h, 0:seq, :] = k_ref[:, cols].astype(BF16)
            vt_sc[h, :, 0:seq] = v_ref[:, cols].astype(F32).T.astype(BF16)
            if cached:
                past = ck_ref.shape[0]
                kall_sc[h, seq:seq + past, :] = ck_ref[:, cols].astype(BF16)
                vt_sc[h, :, seq:seq + past] = cv_ref[:, cols].T.astype(BF16)

    lam = _lambda(lam_ref, lam_init)
    lane = lax.broadcasted_iota(jnp.int32, (1, DA_V_DIM), 1)
    comp_masks = [lane < DA_HEAD_DIM, lane >= DA_HEAD_DIM]
    tq = q_ref.shape[0]
    n_keys = kall_sc.shape[1]
    n_tiles = n_keys // ATT_TK

    def stacked_q(h):
        q = q_ref[:, h * DA_V_DIM:(h + 1) * DA_V_DIM].astype(BF16)
        return jnp.concatenate([jnp.where(m, q, jnp.zeros_like(q)) for m in comp_masks], axis=0)

    def score_tile(h, j, qq, m8):
        rows = slice(j * ATT_TK, (j + 1) * ATT_TK)
        st = _dot_nt(kall_sc[h, rows, :], qq)
        s_sc[h % 2, rows, :] = st
        t8 = jnp.max(st.reshape(ATT_TK // 8, 8, 2 * tq), axis=0)
        return t8 if m8 is None else jnp.maximum(m8, t8)

    def prob_tile(h, j, mx, d8, pv):
        rows = slice(j * ATT_TK, (j + 1) * ATT_TK)
        e = jnp.exp2(s_sc[h % 2, rows, :] - mx)
        s8 = jnp.sum(e.reshape(ATT_TK // 8, 8, 2 * tq), axis=0)
        p = _dot(vt_sc[h, :, rows], e)
        return (s8 if d8 is None else d8 + s8), (p if pv is None else pv + p)

    qq = stacked_q(0)
    m8 = None
    for j in range(n_tiles):
        m8 = score_tile(0, j, qq, m8)
    for h in range(DA_HEADS):
        mx = jnp.max(m8, axis=0, keepdims=True)
        if h + 1 < DA_HEADS:
            qq = stacked_q(h + 1)
        m8, d8, pv = None, None, None
        for j in range(n_tiles):
            if h + 1 < DA_HEADS:
                m8 = score_tile(h + 1, j, qq, m8)
            d8, pv = prob_tile(h, j, mx, d8, pv)
        pv = pv * (1.0 / jnp.sum(d8, axis=0, keepdims=True))
        out_t = pv[:, :tq] - lam * pv[:, tq:]
        out_t = out_t * lax.rsqrt(jnp.mean(out_t * out_t, axis=0, keepdims=True) + EPS)
        out = out_t.T * gs_ref[...]
        o_ref[:, h * DA_V_DIM:(h + 1) * DA_V_DIM] = (out * (1.0 - lam_init)).astype(o_ref.dtype)


def _attn(q, k, v, cache, lam_vecs, g_sub, batch, seq_len, lam_init):
    rows = q.shape[0]
    nq = seq_len // ATT_TQ
    q_spec = pl.BlockSpec((ATT_TQ, D_MODEL), lambda b, i: (b * nq + i, 0))
    kv_spec = pl.BlockSpec((seq_len, D_MODEL), lambda b, i: (b, 0))
    in_specs = [q_spec, kv_spec, kv_spec]
    args = [q, k, v]
    n_keys = seq_len
    if cache is not None:
        past = cache[0].shape[1]
        n_keys += past
        c_spec = pl.BlockSpec((None, past, D_MODEL), lambda b, i: (b, 0, 0))
        in_specs += [c_spec, c_spec]
        args += list(cache)
    in_specs += [_resident((4, DA_HEAD_DIM)), _resident((1, DA_V_DIM))]
    args += [lam_vecs, g_sub.reshape(1, DA_V_DIM)]
    return pl.pallas_call(
        functools.partial(_attn_kernel, cached=cache is not None, lam_init=lam_init),
        grid=(batch, nq),
        in_specs=in_specs,
        out_specs=q_spec,
        out_shape=jax.ShapeDtypeStruct((rows, D_MODEL), BF16),
        scratch_shapes=[pltpu.VMEM((DA_HEADS, n_keys, DA_V_DIM), BF16),
                        pltpu.VMEM((DA_HEADS, DA_V_DIM, n_keys), BF16),
                        pltpu.VMEM((2, n_keys, 2 * ATT_TQ), F32)],
        compiler_params=_params(2),
        name="diff_attention",
    )(*args)


def _per_chain(fn, a, b):
    return jnp.stack([fn(a[i], b[i]) for i in range(a.shape[0])])


ML_EXT = 8


def _rows(x):
    return jnp.stack([x[i:i + 1, :] for i in range(x.shape[0])])


def _split3(x):
    hi = x.astype(BF16)
    r = x - hi.astype(F32)
    mid = r.astype(BF16)
    lo = (r - mid.astype(F32)).astype(BF16)
    return jnp.concatenate([hi, mid, lo], axis=1)


def _chunk_scan(x, reverse_rows, tri_prefix, tri_suffix):
    parts = _split3(x)
    return jnp.where(reverse_rows, _dot(parts, tri_suffix), _dot(parts, tri_prefix))


def _paired_value_matmul(v_t, sc):
    B, L, _ = sc.shape
    zero = jnp.zeros((L, L), sc.dtype)
    out = []
    for i in range(0, B, 2):
        lhs = jnp.concatenate([v_t[i], v_t[i + 1]], axis=1)
        rhs = jnp.concatenate([jnp.concatenate([sc[i], zero], axis=1),
                               jnp.concatenate([zero, sc[i + 1]], axis=1)], axis=0)
        both = _dot(lhs, rhs)
        out += [both[:, :L], both[:, L:]]
    return jnp.stack(out)


def _mlstm_step(k, q_t, v_t, ic, fc, CT, m, seen_t, reverse_rows):
    B, L = ic.shape
    d = k.shape[2]
    tri = lambda keep: jnp.concatenate([jnp.where(keep, 1.0, 0.0).astype(BF16)] * 3, axis=0)
    b2 = _chunk_scan(fc, reverse_rows, tri(seen_t[0]), tri(seen_t[B - 1]))
    u2 = ic - b2
    u_t = jnp.concatenate([u2, jnp.zeros((L - B, L), F32)], axis=0).T
    u_col = jnp.stack([jnp.broadcast_to(u_t[:, i:i + 1], (L, L)) for i in range(B)])
    b, i_g, f_g = _rows(b2), _rows(ic), _rows(fc)
    b_last = jnp.sum(f_g, axis=2, keepdims=True)

    log_d = jnp.where(seen_t, b + u_col, -jnp.inf)
    a = b + m
    m_t = jnp.maximum(a, jnp.max(log_d, axis=1, keepdims=True))
    dmat = jnp.exp(log_d - m_t)
    inter = jnp.exp(a - m_t)
    sc = _per_chain(_dot, k, q_t) * dmat
    cq = _per_chain(_dot, CT, q_t)
    num = _paired_value_matmul(v_t, sc.astype(BF16)) + inter * cq[:, :d, :]
    den = jnp.sum(sc, axis=1, keepdims=True) + inter * cq[:, d:d + 1, :]
    h_t = num * (1.0 / jnp.maximum(jnp.abs(den), jnp.exp(-m_t)))

    g = b_last - b + i_g
    m_new = jnp.maximum(b_last + m, jnp.max(g, axis=2, keepdims=True))
    w = jnp.exp(g - m_new)
    decay = jnp.exp(b_last + m - m_new)
    vw = jnp.concatenate([v_t.astype(F32), jnp.ones((B, ML_EXT, L), F32)], axis=1) * w
    CT_new = decay * CT + _per_chain(_dot, vw, k)
    return h_t, CT_new, m_new


def _mlstm_kernel(*refs, seq_len, has_state, emit_state):
    q_ref, k_ref, v_ref, so_ref, g_ref, gmh_ref = refs[:6]
    refs = refs[6:]
    if has_state:
        (c0_ref, n0_ref, m0_ref), refs = refs[:3], refs[3:]
    hm_ref, refs = refs[0], refs[1:]
    if emit_state:
        (c_out_ref, n_out_ref, m_out_ref), refs = refs[:3], refs[3:]
    ct_sc, m_sc, gr_sc, h_sc = refs

    nc = seq_len // CHUNK
    d_head = ML_HEAD_DIM
    chains = [(d, h) for d in range(2) for h in range(ML_HEADS)]
    n_chain = len(chains)
    for i, (d, h) in enumerate(chains):
        if has_state:
            ct_sc[i, 0:d_head, :] = c0_ref[d, h].T
            ct_sc[i, d_head:d_head + ML_EXT, :] = jnp.broadcast_to(n0_ref[d, h], (ML_EXT, d_head))
            m_sc[i] = m0_ref[d, h][:, 0:1]
        else:
            ct_sc[i] = jnp.zeros((d_head + ML_EXT, d_head), F32)
            m_sc[i] = jnp.zeros((1, 1), F32)
    for c in range(nc):
        gr_sc[c] = g_ref[c * CHUNK:(c + 1) * CHUNK, :].T

    s_idx = lax.broadcasted_iota(jnp.int32, (CHUNK, CHUNK), 0)
    t_idx = lax.broadcasted_iota(jnp.int32, (CHUNK, CHUNK), 1)
    seen_t = jnp.stack([s_idx <= t_idx] * ML_HEADS + [s_idx >= t_idx] * ML_HEADS)
    reverse_rows = lax.broadcasted_iota(jnp.int32, (n_chain, CHUNK), 0) >= ML_HEADS

    def step(c_fwd, c_bwd, rows_of):
        chunk_of = (c_fwd, c_bwd)
        g_fwd, g_bwd = gr_sc[c_fwd], gr_sc[c_bwd]
        lo = 2 * ML_HEADS
        ic = jnp.concatenate([g_fwd[0:ML_HEADS], g_bwd[lo:lo + ML_HEADS]], axis=0)
        fc = jnp.concatenate([g_fwd[ML_HEADS:lo], g_bwd[lo + ML_HEADS:2 * lo]], axis=0)
        head = [slice(h * d_head, (h + 1) * d_head) for _, h in chains]
        stack = lambda pick: jnp.stack([pick(i, chunk_of[d]) for i, (d, _) in enumerate(chains)])
        h_t, CT_new, m_new = _mlstm_step(
            stack(lambda i, c: k_ref[rows_of(c), head[i]]),
            stack(lambda i, c: q_ref[c, head[i], :]),
            stack(lambda i, c: v_ref[c, head[i], :]),
            ic, fc, ct_sc[...], m_sc[...], seen_t, reverse_rows)
        ct_sc[...] = CT_new
        m_sc[...] = m_new
        for i, (d, _) in enumerate(chains):
            h_sc[d, chunk_of[d], head[i], :] = h_t[i]

    if nc <= 2:
        for c in range(nc):
            step(c, nc - 1 - c, lambda cc: slice(cc * CHUNK, (cc + 1) * CHUNK))
    else:
        def body(c, carry):
            step(c, nc - 1 - c, lambda cc: pl.ds(pl.multiple_of(cc * CHUNK, CHUNK), CHUNK))
            return carry
        lax.fori_loop(0, nc, body, 0)

    for h in range(ML_HEADS):
        hcols = slice(h * d_head, (h + 1) * d_head)
        gain = jnp.broadcast_to(gmh_ref[h], (d_head, CHUNK))
        for c in range(nc):
            rows = slice(c * CHUNK, (c + 1) * CHUNK)
            hsum = h_sc[0, c, hcols, :] + h_sc[1, c, hcols, :]
            hn = hsum * lax.rsqrt(jnp.mean(hsum * hsum, axis=0, keepdims=True) + EPS) * gain
            hm_ref[rows, hcols] = (hn.T * so_ref[rows, hcols]).astype(hm_ref.dtype)
    if emit_state:
        for i, (d, h) in enumerate(chains):
            c_out_ref[d, h] = ct_sc[i, 0:d_head, :].T
            n_out_ref[d, h] = ct_sc[i, d_head:d_head + 1, :]
            m_out_ref[d, h] = jnp.broadcast_to(m_sc[i], (1, LANES))


def _mlstm(q_t, k, v_t, so, gates, g_mh, state, batch, seq_len, emit_state):
    rows = k.shape[0]
    d = ML_HEAD_DIM
    nc = seq_len // CHUNK
    tile = pl.BlockSpec((seq_len, D_MODEL), lambda b: (b, 0))
    tile_t = pl.BlockSpec((nc, D_MODEL, CHUNK), lambda b: (b, 0, 0))
    in_specs = [tile_t, tile, tile_t, tile,
                pl.BlockSpec((seq_len, GATE_PAD), lambda b: (b, 0)),
                _resident((ML_HEADS, d, 1))]
    args = [q_t, k, v_t, so, gates, g_mh.reshape(ML_HEADS, d, 1)]
    c_spec = pl.BlockSpec((None, 2, ML_HEADS, d, d), lambda b: (b, 0, 0, 0, 0))
    n_spec = pl.BlockSpec((None, 2, ML_HEADS, 1, d), lambda b: (b, 0, 0, 0, 0))
    m_spec = pl.BlockSpec((None, 2, ML_HEADS, 1, LANES), lambda b: (b, 0, 0, 0, 0))
    if state is not None:
        in_specs += [c_spec, n_spec, m_spec]
        args += list(state)
    out_specs = [tile]
    out_shape = [jax.ShapeDtypeStruct((rows, D_MODEL), BF16)]
    if emit_state:
        out_specs += [c_spec, n_spec, m_spec]
        out_shape += [jax.ShapeDtypeStruct((batch, 2, ML_HEADS, d, d), F32),
                      jax.ShapeDtypeStruct((batch, 2, ML_HEADS, 1, d), F32),
                      jax.ShapeDtypeStruct((batch, 2, ML_HEADS, 1, LANES), F32)]
    n_state = 2 * ML_HEADS
    return pl.pallas_call(
        functools.partial(_mlstm_kernel, seq_len=seq_len, has_state=state is not None, emit_state=emit_state),
        grid=(batch,),
        in_specs=in_specs,
        out_specs=out_specs,
        out_shape=out_shape,
        scratch_shapes=[pltpu.VMEM((n_state, d + ML_EXT, d), F32), pltpu.VMEM((n_state, 1, 1), F32),
                        pltpu.VMEM((nc, GATE_PAD, CHUNK), F32),
                        pltpu.VMEM((2, nc, D_MODEL, CHUNK), F32)],
        compiler_params=_params(1),
        name="mlstm",
    )(*args)


def _merge_kernel(x_ref, m_ref, hm_ref, att_ref, sgm_ref, sgd_ref, wm_ref, wd_ref, wo_ref, o_ref, w_sc):
    @pl.when(pl.program_id(0) == 0)
    def _():
        for j, w_ref in enumerate((wm_ref, wd_ref, wo_ref)):
            w_sc[j] = w_ref[...].astype(BF16)

    y = sgm_ref[...] * _dot(hm_ref[...], w_sc[0]) + sgd_ref[...] * _dot(att_ref[...], w_sc[1])
    o_ref[...] = x_ref[...] + m_ref[5:6, :] * _dot(y, w_sc[2])


def _merge(x, mods, mod_index, hm, att, sgm, sgd, wm, wd, wo):
    rows = x.shape[0]
    tile = pl.BlockSpec((MERGE_TM, D_MODEL), lambda i: (i, 0))
    w_spec = _resident((D_MODEL, D_MODEL))
    return pl.pallas_call(
        _merge_kernel,
        grid=(rows // MERGE_TM,),
        in_specs=[tile, pl.BlockSpec((None, N_MOD, D_MODEL), lambda i: (mod_index(i * MERGE_TM), 0, 0)),
                  tile, tile, tile, tile, w_spec, w_spec, w_spec],
        out_specs=tile,
        out_shape=jax.ShapeDtypeStruct((rows, D_MODEL), F32),
        scratch_shapes=[pltpu.VMEM((3, D_MODEL, D_MODEL), BF16)],
        compiler_params=_params(1),
        name="branch_merge",
    )(x, mods, hm, att, sgm, sgd, wm, wd, wo)


def _rope_tables(seq_len):
    lane = jnp.arange(LANES)
    r = lane % 32
    freqs = jnp.power(ROPE_BASE, -(r % 16).astype(F32) / 16.0)
    tok = jnp.arange(seq_len)
    pos = jnp.where((lane % 64 < 32)[None, :], (tok // GRID_W)[:, None], (tok % GRID_W)[:, None]).astype(F32)
    ang = pos * freqs[None, :]
    sign = jnp.where(r < 16, -1.0, 1.0).astype(F32)
    return jnp.cos(ang), jnp.sin(ang) * sign[None, :]


def _layer(x, mods, mod_index, w, batch, seq_len, ctx, lam_init):
    x = _ffn(x, mods, mod_index, w["g_norm"][0], w["ffn1_w1"], w["ffn1_w3"], w["ffn1_w2"], base=0)
    rope_tabs = None if ctx is None else _rope_tables(seq_len)
    mq, mk, mv, so, dq, dk, dv, sgm, sgd, gates, *new_k = _proj(
        x, mods, mod_index, w["g_norm"][1], w["w_in_t"], w["b_gate"], w["bd"], w["g_qn"], w["g_kn"],
        rope_tabs, seq_len)
    cache = None if ctx is None else (ctx[0], ctx[1])
    att = _attn(dq, dk, dv, cache, w["lam"], w["g_sub"], batch, seq_len, lam_init)
    state = None if ctx is None else ctx[2]
    res = _mlstm(mq, mk, mv, so, gates, w["g_mh"], state, batch, seq_len, emit_state=ctx is None)
    x = _merge(x, mods, mod_index, res[0], att, sgm, sgd, w["w_br_m"], w["w_br_d"], w["w_out"])
    x = _ffn(x, mods, mod_index, w["g_norm"][2], w["ffn2_w1"], w["ffn2_w3"], w["ffn2_w2"], base=6)
    return x, new_k, dv, res[1:]


def kernel(x_prompt, x_sample, c, cache_k, cache_v, state_C, state_n, state_m, c_ctx, w_ada, b_ada, g_norm, ffn1_w1, ffn1_w3, ffn1_w2, ffn2_w1, ffn2_w3, ffn2_w2, w_in, b_gate, g_qn, g_kn, lam_q1, lam_k1, lam_q2, lam_k2, g_sub, g_mh, w_br_m, w_br_d, w_out):
    depth = w_ada.shape[0]
    assert depth == 1
    l = 0
    bp, tp, _ = x_prompt.shape
    bs, ts, _ = x_sample.shape
    past = cache_k.shape[2]
    lam_init = 0.8 - 0.6 * math.exp(-0.3 * l)

    cvecs = jnp.concatenate([c_ctx[None, :], c, jnp.zeros((8 - 1 - bs, D_MODEL), F32)], axis=0)
    mods = _mods(cvecs, w_ada[l], b_ada[l]).reshape(8, N_MOD, D_MODEL)

    group = jnp.arange(MXU_DIM) // DA_HEAD_DIM
    w = dict(
        g_norm=g_norm[l],
        ffn1_w1=ffn1_w1[l], ffn1_w3=ffn1_w3[l], ffn1_w2=ffn1_w2[l],
        ffn2_w1=ffn2_w1[l], ffn2_w3=ffn2_w3[l], ffn2_w2=ffn2_w2[l],
        w_in_t=w_in[l].T,
        b_gate=jnp.pad(b_gate[l], (0, GATE_PAD - N_GATE_COLS)).reshape(1, GATE_PAD),
        bd=(group[:, None] == group[None, :]).astype(BF16),
        g_qn=jnp.tile(g_qn[l] * QK_LOG2_SCALE, MXU_DIM // DA_HEAD_DIM).reshape(1, MXU_DIM),
        g_kn=jnp.tile(g_kn[l], MXU_DIM // DA_HEAD_DIM).reshape(1, MXU_DIM),
        lam=jnp.stack([lam_q1[l], lam_k1[l], lam_q2[l], lam_k2[l]]),
        g_sub=g_sub[l], g_mh=g_mh[l],
        w_br_m=w_br_m[l], w_br_d=w_br_d[l], w_out=w_out[l],
    )

    xp, (new_k,), new_v, (new_c, new_n, new_m) = _layer(
        x_prompt.reshape(bp * tp, D_MODEL), mods, lambda r: 0, w, bp, tp, None, lam_init)

    ctx = (cache_k[:, l].reshape(bs, past, D_MODEL), cache_v[:, l].reshape(bs, past, D_MODEL),
           (state_C[:, l], state_n[:, l].reshape(bs, 2, ML_HEADS, 1, ML_HEAD_DIM),
            jnp.broadcast_to(state_m[:, l][..., None, None], (bs, 2, ML_HEADS, 1, LANES))))
    xs, _, _, _ = _layer(x_sample.reshape(bs * ts, D_MODEL), mods, lambda r: 1 + r // ts, w, bs, ts, ctx, lam_init)

    return (xp.reshape(bp, tp, D_MODEL), xs.reshape(bs, ts, D_MODEL),
            new_k.reshape(bp, 1, tp, DA_HEADS, 2, DA_HEAD_DIM),
            new_v.reshape(bp, 1, tp, DA_HEADS, DA_V_DIM),
            new_c[:, None], new_n.reshape(bp, 1, 2, ML_HEADS, ML_HEAD_DIM),
            new_m[..., 0, 0][:, None])
```

```python
import functools
import math

import jax
import jax.numpy as jnp
import numpy as np
from jax import lax
from jax.experimental import pallas as pl
from jax.experimental.pallas import tpu as pltpu

F32 = jnp.float32
BF16 = jnp.bfloat16

D_MODEL = 1024
D_FF = 2816
N_MOD = 9
GRID_W = 64
ML_HEADS = 4
ML_HEAD_DIM = 256
DA_HEADS = 8
DA_HEAD_DIM = 64
DA_V_DIM = 128
N_GATE_COLS = 16
CHUNK = 128
ROPE_BASE = 10000.0
QK_LOG2_SCALE = DA_HEAD_DIM ** -0.5 * math.log2(math.e)
EPS = 1e-6

LANES = 128
GATE_PAD = LANES
MXU_DIM = 256
VMEM_LIMIT = 56 * 1024 * 1024

FFN_TM = 512
FFN_TF = 256
PROJ_TM = 256
MERGE_TM = 512
ATT_TQ = 256
ATT_TK_MAX = 768


def _params(n_axes):
    return pltpu.CompilerParams(dimension_semantics=("arbitrary",) * n_axes,
                                vmem_limit_bytes=VMEM_LIMIT)


def _dot(a, b):
    return jnp.dot(a.astype(BF16), b.astype(BF16), preferred_element_type=F32)


def _dot_nt(a, b):
    return lax.dot_general(a.astype(BF16), b.astype(BF16), (((1,), (1,)), ((), ())),
                           preferred_element_type=F32)


def _dot_tn(a, b):
    return lax.dot_general(a.astype(BF16), b.astype(BF16), (((0,), (0,)), ((), ())),
                           preferred_element_type=F32)


def _sigmoid(x):
    return 1.0 / (1.0 + jnp.exp(-x))


def _log_sigmoid(x):
    return jnp.minimum(x, 0.0) - jnp.log1p(jnp.exp(-jnp.abs(x)))


def _modulated_norm(x, g, shift, scale):
    y = x * lax.rsqrt(jnp.mean(x * x, axis=-1, keepdims=True) + EPS) * g
    return y * (1.0 + scale) + shift


def _resident(shape):
    return pl.BlockSpec(shape, lambda *_: (0,) * len(shape), pipeline_mode=pl.Buffered(1))


def _mods_kernel(c_ref, w_ref, b_ref, o_ref):
    c = c_ref[...]
    o_ref[...] = _dot(c * _sigmoid(c), w_ref[...]) + b_ref[...]


def _mods(cvecs, w_ada, b_ada):
    n = N_MOD * D_MODEL
    tn = D_MODEL
    return pl.pallas_call(
        _mods_kernel,
        grid=(n // tn,),
        in_specs=[pl.BlockSpec((8, D_MODEL), lambda j: (0, 0)),
                  pl.BlockSpec((D_MODEL, tn), lambda j: (0, j)),
                  pl.BlockSpec((1, tn), lambda j: (0, j))],
        out_specs=pl.BlockSpec((8, tn), lambda j: (0, j)),
        out_shape=jax.ShapeDtypeStruct((8, n), F32),
        compiler_params=_params(1),
        name="adaln_mods",
    )(cvecs, w_ada, b_ada.reshape(1, n))


def _ffn_kernel(x_ref, m_ref, g_ref, w1_ref, w3_ref, w2_ref, o_ref, w1_sc, w3_sc, w2_sc, hh_sc, acc_sc, *, base):
    g = pl.program_id(0)
    nf = D_FF // FFN_TF
    norm = lambda x: _modulated_norm(x, g_ref[...], m_ref[base:base + 1, :], m_ref[base + 1:base + 2, :]).astype(BF16)
    finish = lambda x, acc: x + 0.5 * m_ref[base + 2:base + 3, :] * acc

    def tile(hh, f):
        a = _dot(hh, w1_sc[f])
        b = _dot(hh, w3_sc[f])
        return _dot(a * _sigmoid(a) * b, w2_sc[f])

    @pl.when(g < nf)
    def _():
        w1_sc[g] = w1_ref[...].astype(BF16)
        w3_sc[g] = w3_ref[...].astype(BF16)
        w2_sc[g] = w2_ref[...].astype(BF16)

        @pl.when(g == 0)
        def _():
            hh_sc[...] = norm(x_ref[...])
            acc_sc[...] = jnp.zeros(acc_sc.shape, F32)

        acc_sc[...] += tile(hh_sc[...], g)

        @pl.when(g == nf - 1)
        def _():
            o_ref[...] = finish(x_ref[...], acc_sc[...])

    @pl.when(g >= nf)
    def _():
        x = x_ref[...]
        hh = norm(x)
        acc = jnp.zeros(x.shape, F32)
        for f in range(nf):
            acc = acc + tile(hh, f)
        o_ref[...] = finish(x, acc)


def _ffn(x, mods, mod_index, g, w1, w3, w2, base):
    rows = x.shape[0]
    nf = D_FF // FFN_TF
    row_tile = lambda s: jnp.maximum(s - (nf - 1), 0)
    f_tile = lambda s: jnp.minimum(s, nf - 1)
    row = lambda s: (row_tile(s), 0)
    return pl.pallas_call(
        functools.partial(_ffn_kernel, base=base),
        grid=(nf - 1 + rows // FFN_TM,),
        in_specs=[pl.BlockSpec((FFN_TM, D_MODEL), row),
                  pl.BlockSpec((None, N_MOD, D_MODEL), lambda s: (mod_index(row_tile(s) * FFN_TM), 0, 0)),
                  _resident((1, D_MODEL)),
                  pl.BlockSpec((D_MODEL, FFN_TF), lambda s: (0, f_tile(s))),
                  pl.BlockSpec((D_MODEL, FFN_TF), lambda s: (0, f_tile(s))),
                  pl.BlockSpec((FFN_TF, D_MODEL), lambda s: (f_tile(s), 0))],
        out_specs=pl.BlockSpec((FFN_TM, D_MODEL), row),
        out_shape=jax.ShapeDtypeStruct((rows, D_MODEL), F32),
        scratch_shapes=[pltpu.VMEM((nf, D_MODEL, FFN_TF), BF16), pltpu.VMEM((nf, D_MODEL, FFN_TF), BF16),
                        pltpu.VMEM((nf, FFN_TF, D_MODEL), BF16),
                        pltpu.VMEM((FFN_TM, D_MODEL), BF16), pltpu.VMEM((FFN_TM, D_MODEL), F32)],
        compiler_params=_params(1),
        name="ffn",
    )(x, mods, g.reshape(1, D_MODEL), w1, w3, w2)


def _group_norm64(x, bd, g):
    ss = _dot(x * x, bd)
    return x * lax.rsqrt(ss * (1.0 / DA_HEAD_DIM) + EPS) * g


def _rope(x, cos, sin_signed):
    first = (lax.broadcasted_iota(jnp.int32, x.shape, 1) % 32) < 16
    partner = jnp.where(first, pltpu.roll(x, LANES - 16, 1), pltpu.roll(x, 16, 1))
    return x * cos + partner * sin_signed


N_SEG = 9
GATE_LO = 4 * D_MODEL


def _seg_start(s):
    return s * D_MODEL + N_GATE_COLS * (s >= 4)


def _proj_kernel(*refs, rope):
    (x_ref, m_ref, g_ref, wt_ref, wgt_ref, bg_ref, bd_ref, gq_ref, gk_ref), refs = refs[:9], refs[9:]
    if rope:
        (cos_ref, sin_ref), refs = refs[:2], refs[2:]
        newk_ref = None
    else:
        newk_ref, refs = refs[10], refs[:10] + refs[11:]
    (mq_ref, mk_ref, mv_ref, so_ref, dq_ref, dk_ref, dv_ref, sgm_ref, sgd_ref, gates_ref,
     w_sc, wg_sc, hh_sc) = refs
    step = pl.program_id(0)
    norm = lambda: _modulated_norm(x_ref[...], g_ref[...], m_ref[3:4, :], m_ref[4:5, :]).astype(BF16)

    def gates(hh):
        y = _dot_nt(hh, wg_sc[...]) + bg_ref[...]
        col = lax.broadcasted_iota(jnp.int32, y.shape, 1)
        gates_ref[...] = jnp.where((col // ML_HEADS) % 2 == 1, _log_sigmoid(y), y)

    def qk_segment(y, g_head_ref, o_ref, cache_ref):
        bd = bd_ref[...]
        for c in range(D_MODEL // MXU_DIM):
            cols = slice(c * MXU_DIM, (c + 1) * MXU_DIM)
            z = _group_norm64(y[:, cols], bd, g_head_ref[...])
            if rope:
                z = jnp.concatenate(
                    [_rope(z[:, k * LANES:(k + 1) * LANES], cos_ref[...], sin_ref[...])
                     for k in range(MXU_DIM // LANES)], axis=1)
            o_ref[:, cols] = z.astype(o_ref.dtype)
            if cache_ref is not None:
                groups = MXU_DIM // DA_HEAD_DIM
                cache_ref[:, c * groups:(c + 1) * groups, :] = z.reshape(z.shape[0], groups, DA_HEAD_DIM)

    def segment(s, hh):
        if s in (0, 2):
            o_ref = {0: mq_ref, 2: mv_ref}[s]
            y_t = _dot_nt(w_sc[s], hh) * ((ML_HEAD_DIM ** -0.5) if s == 0 else 1.0)
            for j in range(PROJ_TM // CHUNK):
                o_ref[j] = y_t[:, j * CHUNK:(j + 1) * CHUNK].astype(o_ref.dtype)
            return
        y = _dot_nt(hh, w_sc[s])
        if s == 4:
            qk_segment(y, gq_ref, dq_ref, None)
        elif s == 5:
            qk_segment(y, gk_ref, dk_ref, newk_ref)
        elif s in (3, 7, 8):
            {3: so_ref, 7: sgm_ref, 8: sgd_ref}[s][...] = _sigmoid(y)
        else:
            o_ref = {1: mk_ref, 6: dv_ref}[s]
            o_ref[...] = y.astype(o_ref.dtype)

    for s in range(N_SEG):
        @pl.when(step == s)
        def _(s=s):
            w_sc[s] = wt_ref[...].astype(BF16)
            if s == 0:
                wg_sc[...] = jnp.zeros(wg_sc.shape, BF16)
                wg_sc[0:N_GATE_COLS, :] = wgt_ref[...].astype(BF16)
                hh_sc[...] = norm()
                gates(hh_sc[...])
            segment(s, hh_sc[...])

    @pl.when(step >= N_SEG)
    def _():
        hh = norm()
        gates(hh)
        for s in range(N_SEG):
            segment(s, hh)


def _proj(x, mods, mod_index, g, w_in_t, b_gate, bd, gq, gk, rope_tabs, seq_len):
    rows = x.shape[0]
    rope = rope_tabs is not None
    row_tile = lambda s: jnp.maximum(s - (N_SEG - 1), 0)
    row = lambda s: (row_tile(s), 0)
    tile = pl.BlockSpec((PROJ_TM, D_MODEL), row)
    elem = lambda n: (pl.Element(n), pl.Element(D_MODEL))
    seg_row = lambda s: pl.multiple_of(_seg_start(jnp.minimum(s, N_SEG - 1)), N_GATE_COLS)
    in_specs = [tile,
                pl.BlockSpec((None, N_MOD, D_MODEL), lambda s: (mod_index(row_tile(s) * PROJ_TM), 0, 0)),
                _resident((1, D_MODEL)),
                pl.BlockSpec(elem(D_MODEL), lambda s: (seg_row(s), 0)),
                pl.BlockSpec(elem(N_GATE_COLS), lambda s: (GATE_LO, 0)),
                _resident((1, GATE_PAD)),
                _resident((MXU_DIM, MXU_DIM)),
                _resident((1, MXU_DIM)),
                _resident((1, MXU_DIM))]
    args = [x, mods, g.reshape(1, D_MODEL), w_in_t, w_in_t, b_gate, bd, gq, gk]
    if rope:
        tiles_per_seq = seq_len // PROJ_TM
        tab = pl.BlockSpec((PROJ_TM, LANES), lambda s: (row_tile(s) % tiles_per_seq, 0))
        in_specs += [tab, tab]
        args += list(rope_tabs)
    dtypes = [BF16, BF16, BF16, F32, BF16, BF16, BF16 if rope else F32, F32, F32]
    out_shape = [jax.ShapeDtypeStruct((rows, D_MODEL), dt) for dt in dtypes]
    out_shape.append(jax.ShapeDtypeStruct((rows, GATE_PAD), F32))
    out_specs = [tile] * 9 + [pl.BlockSpec((PROJ_TM, GATE_PAD), row)]
    slabs = PROJ_TM // CHUNK
    for j in (0, 2):
        out_shape[j] = jax.ShapeDtypeStruct((rows // CHUNK, D_MODEL, CHUNK), BF16)
        out_specs[j] = pl.BlockSpec((slabs, D_MODEL, CHUNK), lambda s: (row_tile(s), 0, 0))
    if not rope:
        n_groups = 2 * DA_HEADS
        out_shape.append(jax.ShapeDtypeStruct((rows, n_groups, DA_HEAD_DIM), F32))
        out_specs.append(pl.BlockSpec((PROJ_TM, n_groups, DA_HEAD_DIM), lambda s: (row_tile(s), 0, 0)))
    return pl.pallas_call(
        functools.partial(_proj_kernel, rope=rope),
        grid=(N_SEG - 1 + rows // PROJ_TM,),
        in_specs=in_specs,
        out_specs=out_specs,
        out_shape=out_shape,
        scratch_shapes=[pltpu.VMEM((N_SEG, D_MODEL, D_MODEL), BF16), pltpu.VMEM((GATE_PAD, D_MODEL), BF16),
                        pltpu.VMEM((PROJ_TM, D_MODEL), BF16)],
        compiler_params=_params(1),
        name="mixer_in_proj",
    )(*args)


def _lambda(lam_ref, lam_init):
    lam = lam_ref[...]
    s1 = jnp.sum(lam[0:1, :] * lam[1:2, :], axis=1, keepdims=True)
    s2 = jnp.sum(lam[2:3, :] * lam[3:4, :], axis=1, keepdims=True)
    return jnp.exp(s1) - jnp.exp(s2) + lam_init


def _attn_kernel(*refs, cached, lam_init):
    if cached:
        q_ref, k_ref, v_ref, ck_ref, cv_ref, lam_ref, gs_ref, o_ref, kall_sc, vt_sc, s_sc = refs
    else:
        q_ref, k_ref, v_ref, lam_ref, gs_ref, o_ref, kall_sc, vt_sc, s_sc = refs
    seq = k_ref.shape[0]

    @pl.when(pl.program_id(1) == 0)
    def _():
        for h in range(DA_HEADS):
            cols = slice(h * DA_V_DIM, (h + 1) * DA_V_DIM)
            kall_sc[h, 0:seq, :] = k_ref[:, cols].astype(BF16)
            vt_sc[h, :, 0:seq] = v_ref[:, cols].astype(F32).T.astype(BF16)
            if cached:
                past = ck_ref.shape[1]
                kall_sc[h, seq:seq + past, :] = ck_ref[cols, :].T.astype(BF16)
                vt_sc[h, :, seq:seq + past] = cv_ref[pl.ds(h, past, stride=DA_HEADS), :].T.astype(BF16)

    lam = _lambda(lam_ref, lam_init)
    lane = lax.broadcasted_iota(jnp.int32, (1, DA_V_DIM), 1)
    comp_masks = [lane < DA_HEAD_DIM, lane >= DA_HEAD_DIM]
    tq = q_ref.shape[0]
    n_keys = kall_sc.shape[1]
    n_tiles = pl.cdiv(n_keys, ATT_TK_MAX)
    tk = n_keys // n_tiles

    def stacked_q(h):
        q = q_ref[:, h * DA_V_DIM:(h + 1) * DA_V_DIM].astype(BF16)
        return jnp.concatenate([jnp.where(m, q, jnp.zeros_like(q)) for m in comp_masks], axis=0)

    def score_tile(h, j, qq, m8):
        rows = slice(j * tk, (j + 1) * tk)
        st = _dot_nt(kall_sc[h, rows, :], qq)
        s_sc[h % 2, rows, :] = st
        t8 = jnp.max(st.reshape(tk // 8, 8, 2 * tq), axis=0)
        return t8 if m8 is None else jnp.maximum(m8, t8)

    def prob_tile(h, j, mx, d8, pv):
        rows = slice(j * tk, (j + 1) * tk)
        e = jnp.exp2(s_sc[h % 2, rows, :] - mx)
        s8 = jnp.sum(e.reshape(tk // 8, 8, 2 * tq), axis=0)
        p = _dot(vt_sc[h, :, rows], e)
        return (s8 if d8 is None else d8 + s8), (p if pv is None else pv + p)

    qq = stacked_q(0)
    m8 = None
    for j in range(n_tiles):
        m8 = score_tile(0, j, qq, m8)
    for h in range(DA_HEADS):
        mx = jnp.max(m8, axis=0, keepdims=True)
        if h + 1 < DA_HEADS:
            qq = stacked_q(h + 1)
        m8, d8, pv = None, None, None
        for j in range(n_tiles):
            if h + 1 < DA_HEADS:
                m8 = score_tile(h + 1, j, qq, m8)
            d8, pv = prob_tile(h, j, mx, d8, pv)
        pv = pv * (1.0 / jnp.sum(d8, axis=0, keepdims=True))
        out_t = pv[:, :tq] - lam * pv[:, tq:]
        out_t = out_t * lax.rsqrt(jnp.mean(out_t * out_t, axis=0, keepdims=True) + EPS)
        out = out_t.T * gs_ref[...]
        o_ref[:, h * DA_V_DIM:(h + 1) * DA_V_DIM] = (out * (1.0 - lam_init)).astype(o_ref.dtype)


def _attn(q, k, v, cache, lam_vecs, g_sub, batch, seq_len, lam_init):
    rows = q.shape[0]
    nq = seq_len // ATT_TQ
    q_spec = pl.BlockSpec((ATT_TQ, D_MODEL), lambda b, i: (b * nq + i, 0))
    kv_spec = pl.BlockSpec((seq_len, D_MODEL), lambda b, i: (b, 0))
    in_specs = [q_spec, kv_spec, kv_spec]
    args = [q, k, v]
    n_keys = seq_len
    if cache is not None:
        past = cache[0].shape[2]
        n_keys += past
        in_specs += [pl.BlockSpec((None, D_MODEL, past), lambda b, i: (b, 0, 0)),
                     pl.BlockSpec((None, past * DA_HEADS, DA_V_DIM), lambda b, i: (b, 0, 0))]
        args += list(cache)
    in_specs += [_resident((4, DA_HEAD_DIM)), _resident((1, DA_V_DIM))]
    args += [lam_vecs, g_sub.reshape(1, DA_V_DIM)]
    return pl.pallas_call(
        functools.partial(_attn_kernel, cached=cache is not None, lam_init=lam_init),
        grid=(batch, nq),
        in_specs=in_specs,
        out_specs=q_spec,
        out_shape=jax.ShapeDtypeStruct((rows, D_MODEL), BF16),
        scratch_shapes=[pltpu.VMEM((DA_HEADS, n_keys, DA_V_DIM), BF16),
                        pltpu.VMEM((DA_HEADS, DA_V_DIM, n_keys), BF16),
                        pltpu.VMEM((2, n_keys, 2 * ATT_TQ), F32)],
        compiler_params=_params(2),
        name="diff_attention",
    )(*args)


def _per_chain(fn, a, b):
    return jnp.stack([fn(a[i], b[i]) for i in range(a.shape[0])])


ML_EXT = 8


def _rows(x):
    return jnp.stack([x[i:i + 1, :] for i in range(x.shape[0])])


def _split3(x):
    hi = x.astype(BF16)
    r = x - hi.astype(F32)
    mid = r.astype(BF16)
    lo = (r - mid.astype(F32)).astype(BF16)
    return jnp.concatenate([hi, mid, lo], axis=1)


def _chunk_scan(x, reverse_rows, tri_prefix, tri_suffix):
    parts = _split3(x)
    return jnp.where(reverse_rows, _dot(parts, tri_suffix), _dot(parts, tri_prefix))


def _paired_value_matmul(v_t, sc):
    B, L, _ = sc.shape
    zero = jnp.zeros((L, L), sc.dtype)
    out = []
    for i in range(0, B, 2):
        lhs = jnp.concatenate([v_t[i], v_t[i + 1]], axis=1)
        rhs = jnp.concatenate([jnp.concatenate([sc[i], zero], axis=1),
                               jnp.concatenate([zero, sc[i + 1]], axis=1)], axis=0)
        both = _dot(lhs, rhs)
        out += [both[:, :L], both[:, L:]]
    return jnp.stack(out)


def _mlstm_step(k, q_t, v_t, ic, fc, CT, m, seen_t, reverse_rows):
    B, L = ic.shape
    d = k.shape[2]
    tri = lambda keep: jnp.concatenate([jnp.where(keep, 1.0, 0.0).astype(BF16)] * 3, axis=0)
    b2 = _chunk_scan(fc, reverse_rows, tri(seen_t[0]), tri(seen_t[B - 1]))
    u2 = ic - b2
    u_t = jnp.concatenate([u2, jnp.zeros((L - B, L), F32)], axis=0).T
    u_col = jnp.stack([jnp.broadcast_to(u_t[:, i:i + 1], (L, L)) for i in range(B)])
    b, i_g, f_g = _rows(b2), _rows(ic), _rows(fc)
    b_last = jnp.sum(f_g, axis=2, keepdims=True)

    log_d = jnp.where(seen_t, b + u_col, -jnp.inf)
    a = b + m
    m_t = jnp.maximum(a, jnp.max(log_d, axis=1, keepdims=True))
    dmat = jnp.exp(log_d - m_t)
    inter = jnp.exp(a - m_t)
    sc = _per_chain(_dot, k, q_t) * dmat
    cq = _per_chain(_dot, CT, q_t)
    num = _paired_value_matmul(v_t, sc.astype(BF16)) + inter * cq[:, :d, :]
    den = jnp.sum(sc, axis=1, keepdims=True) + inter * cq[:, d:d + 1, :]
    h_t = num * (1.0 / jnp.maximum(jnp.abs(den), jnp.exp(-m_t)))

    g = b_last - b + i_g
    m_new = jnp.maximum(b_last + m, jnp.max(g, axis=2, keepdims=True))
    w = jnp.exp(g - m_new)
    decay = jnp.exp(b_last + m - m_new)
    vw = jnp.concatenate([v_t.astype(F32), jnp.ones((B, ML_EXT, L), F32)], axis=1) * w
    CT_new = decay * CT + _per_chain(_dot, vw, k)
    return h_t, CT_new, m_new


def _mlstm_kernel(*refs, seq_len, has_state, emit_state):
    q_ref, k_ref, v_ref, so_ref, g_ref, gmh_ref = refs[:6]
    refs = refs[6:]
    if has_state:
        (c0_ref, n0_ref, m0_ref), refs = refs[:3], refs[3:]
    hm_ref, refs = refs[0], refs[1:]
    if emit_state:
        (c_out_ref, n_out_ref, m_out_ref), refs = refs[:3], refs[3:]
    ct_sc, m_sc, gr_sc, h_sc = refs

    nc = seq_len // CHUNK
    d_head = ML_HEAD_DIM
    chains = [(d, h) for d in range(2) for h in range(ML_HEADS)]
    n_chain = len(chains)
    for i, (d, h) in enumerate(chains):
        if has_state:
            ct_sc[i, 0:d_head, :] = c0_ref[d, h].T
            ct_sc[i, d_head:d_head + ML_EXT, :] = jnp.broadcast_to(n0_ref[d, h], (ML_EXT, d_head))
            m_sc[i] = m0_ref[d, h][:, 0:1]
        else:
            ct_sc[i] = jnp.zeros((d_head + ML_EXT, d_head), F32)
            m_sc[i] = jnp.zeros((1, 1), F32)
    for c in range(nc):
        gr_sc[c] = g_ref[c * CHUNK:(c + 1) * CHUNK, :].T

    s_idx = lax.broadcasted_iota(jnp.int32, (CHUNK, CHUNK), 0)
    t_idx = lax.broadcasted_iota(jnp.int32, (CHUNK, CHUNK), 1)
    seen_t = jnp.stack([s_idx <= t_idx] * ML_HEADS + [s_idx >= t_idx] * ML_HEADS)
    reverse_rows = lax.broadcasted_iota(jnp.int32, (n_chain, CHUNK), 0) >= ML_HEADS

    def step(c_fwd, c_bwd, rows_of):
        chunk_of = (c_fwd, c_bwd)
        g_fwd, g_bwd = gr_sc[c_fwd], gr_sc[c_bwd]
        lo = 2 * ML_HEADS
        ic = jnp.concatenate([g_fwd[0:ML_HEADS], g_bwd[lo:lo + ML_HEADS]], axis=0)
        fc = jnp.concatenate([g_fwd[ML_HEADS:lo], g_bwd[lo + ML_HEADS:2 * lo]], axis=0)
        head = [slice(h * d_head, (h + 1) * d_head) for _, h in chains]
        stack = lambda pick: jnp.stack([pick(i, chunk_of[d]) for i, (d, _) in enumerate(chains)])
        h_t, CT_new, m_new = _mlstm_step(
            stack(lambda i, c: k_ref[rows_of(c), head[i]]),
            stack(lambda i, c: q_ref[c, head[i], :]),
            stack(lambda i, c: v_ref[c, head[i], :]),
            ic, fc, ct_sc[...], m_sc[...], seen_t, reverse_rows)
        ct_sc[...] = CT_new
        m_sc[...] = m_new
        for i, (d, _) in enumerate(chains):
            h_sc[d, chunk_of[d], head[i], :] = h_t[i]

    if nc <= 2:
        for c in range(nc):
            step(c, nc - 1 - c, lambda cc: slice(cc * CHUNK, (cc + 1) * CHUNK))
    else:
        def body(c, carry):
            step(c, nc - 1 - c, lambda cc: pl.ds(pl.multiple_of(cc * CHUNK, CHUNK), CHUNK))
            return carry
        lax.fori_loop(0, nc, body, 0)

    for h in range(ML_HEADS):
        hcols = slice(h * d_head, (h + 1) * d_head)
        gain = jnp.broadcast_to(gmh_ref[h], (d_head, CHUNK))
        for c in range(nc):
            rows = slice(c * CHUNK, (c + 1) * CHUNK)
            hsum = h_sc[0, c, hcols, :] + h_sc[1, c, hcols, :]
            hn = hsum * lax.rsqrt(jnp.mean(hsum * hsum, axis=0, keepdims=True) + EPS) * gain
            hm_ref[rows, hcols] = (hn.T * so_ref[rows, hcols]).astype(hm_ref.dtype)
    if emit_state:
        for i, (d, h) in enumerate(chains):
            c_out_ref[d, h] = ct_sc[i, 0:d_head, :].T
            n_out_ref[d, h] = ct_sc[i, d_head:d_head + 1, :]
            m_out_ref[d, h] = jnp.broadcast_to(m_sc[i], (1, LANES))


def _mlstm(q_t, k, v_t, so, gates, g_mh, state, batch, seq_len, emit_state):
    rows = k.shape[0]
    d = ML_HEAD_DIM
    nc = seq_len // CHUNK
    tile = pl.BlockSpec((seq_len, D_MODEL), lambda b: (b, 0))
    tile_t = pl.BlockSpec((nc, D_MODEL, CHUNK), lambda b: (b, 0, 0))
    in_specs = [tile_t, tile, tile_t, tile,
                pl.BlockSpec((seq_len, GATE_PAD), lambda b: (b, 0)),
                _resident((ML_HEADS, d, 1))]
    args = [q_t, k, v_t, so, gates, g_mh.reshape(ML_HEADS, d, 1)]
    c_spec = pl.BlockSpec((None, 2, ML_HEADS, d, d), lambda b: (b, 0, 0, 0, 0))
    n_spec = pl.BlockSpec((None, 2, ML_HEADS, 1, d), lambda b: (b, 0, 0, 0, 0))
    m_spec = pl.BlockSpec((None, 2, ML_HEADS, 1, LANES), lambda b: (b, 0, 0, 0, 0))
    if state is not None:
        in_specs += [c_spec, n_spec, m_spec]
        args += list(state)
    out_specs = [tile]
    out_shape = [jax.ShapeDtypeStruct((rows, D_MODEL), BF16)]
    if emit_state:
        out_specs += [c_spec, n_spec, m_spec]
        out_shape += [jax.ShapeDtypeStruct((batch, 2, ML_HEADS, d, d), F32),
                      jax.ShapeDtypeStruct((batch, 2, ML_HEADS, 1, d), F32),
                      jax.ShapeDtypeStruct((batch, 2, ML_HEADS, 1, LANES), F32)]
    n_state = 2 * ML_HEADS
    return pl.pallas_call(
        functools.partial(_mlstm_kernel, seq_len=seq_len, has_state=state is not None, emit_state=emit_state),
        grid=(batch,),
        in_specs=in_specs,
        out_specs=out_specs,
        out_shape=out_shape,
        scratch_shapes=[pltpu.VMEM((n_state, d + ML_EXT, d), F32), pltpu.VMEM((n_state, 1, 1), F32),
                        pltpu.VMEM((nc, GATE_PAD, CHUNK), F32),
                        pltpu.VMEM((2, nc, D_MODEL, CHUNK), F32)],
        compiler_params=_params(1),
        name="mlstm",
    )(*args)


def _merge_kernel(x_ref, m_ref, hm_ref, att_ref, sgm_ref, sgd_ref, wm_ref, wd_ref, wo_ref, o_ref, w_sc):
    @pl.when(pl.program_id(0) == 0)
    def _():
        for j, w_ref in enumerate((wm_ref, wd_ref, wo_ref)):
            w_sc[j] = w_ref[...].astype(BF16)

    y = sgm_ref[...] * _dot(hm_ref[...], w_sc[0]) + sgd_ref[...] * _dot(att_ref[...], w_sc[1])
    o_ref[...] = x_ref[...] + m_ref[5:6, :] * _dot(y, w_sc[2])


def _merge(x, mods, mod_index, hm, att, sgm, sgd, wm, wd, wo):
    rows = x.shape[0]
    tile = pl.BlockSpec((MERGE_TM, D_MODEL), lambda i: (i, 0))
    w_spec = _resident((D_MODEL, D_MODEL))
    return pl.pallas_call(
        _merge_kernel,
        grid=(rows // MERGE_TM,),
        in_specs=[tile, pl.BlockSpec((None, N_MOD, D_MODEL), lambda i: (mod_index(i * MERGE_TM), 0, 0)),
                  tile, tile, tile, tile, w_spec, w_spec, w_spec],
        out_specs=tile,
        out_shape=jax.ShapeDtypeStruct((rows, D_MODEL), F32),
        scratch_shapes=[pltpu.VMEM((3, D_MODEL, D_MODEL), BF16)],
        compiler_params=_params(1),
        name="branch_merge",
    )(x, mods, hm, att, sgm, sgd, wm, wd, wo)


def _rope_tables(seq_len):
    lane = np.arange(LANES)
    r = lane % 32
    freqs = np.power(np.float32(ROPE_BASE), -(r % 16).astype(np.float32) / np.float32(16.0))
    tok = np.arange(seq_len)
    pos = np.where((lane % 64 < 32)[None, :], (tok // GRID_W)[:, None], (tok % GRID_W)[:, None]).astype(np.float32)
    ang = pos * freqs[None, :]
    sign = np.where(r < 16, -1.0, 1.0).astype(np.float32)
    return jnp.asarray(np.cos(ang), F32), jnp.asarray(np.sin(ang) * sign[None, :], F32)


def _layer(x, mods, mod_index, w, batch, seq_len, ctx, lam_init):
    x = _ffn(x, mods, mod_index, w["g_norm"][0], w["ffn1_w1"], w["ffn1_w3"], w["ffn1_w2"], base=0)
    rope_tabs = None if ctx is None else _rope_tables(seq_len)
    mq, mk, mv, so, dq, dk, dv, sgm, sgd, gates, *new_k = _proj(
        x, mods, mod_index, w["g_norm"][1], w["w_in_t"], w["b_gate"], w["bd"], w["g_qn"], w["g_kn"],
        rope_tabs, seq_len)
    cache = None if ctx is None else (ctx[0], ctx[1])
    att = _attn(dq, dk, dv, cache, w["lam"], w["g_sub"], batch, seq_len, lam_init)
    state = None if ctx is None else ctx[2]
    res = _mlstm(mq, mk, mv, so, gates, w["g_mh"], state, batch, seq_len, emit_state=ctx is None)
    x = _merge(x, mods, mod_index, res[0], att, sgm, sgd, w["w_br_m"], w["w_br_d"], w["w_out"])
    x = _ffn(x, mods, mod_index, w["g_norm"][2], w["ffn2_w1"], w["ffn2_w3"], w["ffn2_w2"], base=6)
    return x, new_k, dv, res[1:]


def kernel(x_prompt, x_sample, c, cache_k, cache_v, state_C, state_n, state_m, c_ctx, w_ada, b_ada, g_norm, ffn1_w1, ffn1_w3, ffn1_w2, ffn2_w1, ffn2_w3, ffn2_w2, w_in, b_gate, g_qn, g_kn, lam_q1, lam_k1, lam_q2, lam_k2, g_sub, g_mh, w_br_m, w_br_d, w_out):
    depth = w_ada.shape[0]
    assert depth == 1
    l = 0
    bp, tp, _ = x_prompt.shape
    bs, ts, _ = x_sample.shape
    past = cache_k.shape[2]
    lam_init = 0.8 - 0.6 * math.exp(-0.3 * l)

    cvecs = jnp.concatenate([c_ctx[None, :], c, jnp.zeros((8 - 1 - bs, D_MODEL), F32)], axis=0)
    mods = _mods(cvecs, w_ada[l], b_ada[l]).reshape(8, N_MOD, D_MODEL)

    group = jnp.arange(MXU_DIM) // DA_HEAD_DIM
    w = dict(
        g_norm=g_norm[l],
        ffn1_w1=ffn1_w1[l], ffn1_w3=ffn1_w3[l], ffn1_w2=ffn1_w2[l],
        ffn2_w1=ffn2_w1[l], ffn2_w3=ffn2_w3[l], ffn2_w2=ffn2_w2[l],
        w_in_t=w_in[l].T,
        b_gate=jnp.pad(b_gate[l], (0, GATE_PAD - N_GATE_COLS)).reshape(1, GATE_PAD),
        bd=(group[:, None] == group[None, :]).astype(BF16),
        g_qn=jnp.tile(g_qn[l] * QK_LOG2_SCALE, MXU_DIM // DA_HEAD_DIM).reshape(1, MXU_DIM),
        g_kn=jnp.tile(g_kn[l], MXU_DIM // DA_HEAD_DIM).reshape(1, MXU_DIM),
        lam=jnp.stack([lam_q1[l], lam_k1[l], lam_q2[l], lam_k2[l]]),
        g_sub=g_sub[l], g_mh=g_mh[l],
        w_br_m=w_br_m[l], w_br_d=w_br_d[l], w_out=w_out[l],
    )

    xp, (new_k,), new_v, (new_c, new_n, new_m) = _layer(
        x_prompt.reshape(bp * tp, D_MODEL), mods, lambda r: 0, w, bp, tp, None, lam_init)

    ctx = (cache_k[:, l].transpose(0, 2, 3, 4, 1).reshape(bs, D_MODEL, past),
           cache_v[:, l].reshape(bs, past * DA_HEADS, DA_V_DIM),
           (state_C[:, l], state_n[:, l].reshape(bs, 2, ML_HEADS, 1, ML_HEAD_DIM),
            jnp.broadcast_to(state_m[:, l][..., None, None], (bs, 2, ML_HEADS, 1, LANES))))
    xs, _, _, _ = _layer(x_sample.reshape(bs * ts, D_MODEL), mods, lambda r: 1 + r // ts, w, bs, ts, ctx, lam_init)

    return (xp.reshape(bp, tp, D_MODEL), xs.reshape(bs, ts, D_MODEL),
            new_k.reshape(bp, 1, tp, DA_HEADS, 2, DA_HEAD_DIM),
            new_v.reshape(bp, 1, tp, DA_HEADS, DA_V_DIM),
            new_c[:, None], new_n.reshape(bp, 1, 2, ML_HEADS, ML_HEAD_DIM),
            new_m[..., 0, 0][:, None])
```

```python
import functools
import math

import jax
import jax.numpy as jnp
import numpy as np
from jax import lax
from jax.experimental import pallas as pl
from jax.experimental.pallas import tpu as pltpu

F32 = jnp.float32
BF16 = jnp.bfloat16

D_MODEL = 1024
D_FF = 2816
N_MOD = 9
GRID_W = 64
ML_HEADS = 4
ML_HEAD_DIM = 256
DA_HEADS = 8
DA_HEAD_DIM = 64
DA_V_DIM = 128
N_GATE_COLS = 16
CHUNK = 128
ROPE_BASE = 10000.0
QK_LOG2_SCALE = DA_HEAD_DIM ** -0.5 * math.log2(math.e)
EPS = 1e-6

LANES = 128
GATE_PAD = LANES
MXU_DIM = 256
VMEM_LIMIT = 56 * 1024 * 1024

FFN_TM = 512
FFN_TF = 256
PROJ_TM = 256
MERGE_TM = 512
ATT_TQ = 256
ATT_TK_MAX = 768


def _params(n_axes):
    return pltpu.CompilerParams(dimension_semantics=("arbitrary",) * n_axes,
                                vmem_limit_bytes=VMEM_LIMIT)


def _dot(a, b):
    return jnp.dot(a.astype(BF16), b.astype(BF16), preferred_element_type=F32)


def _dot_nt(a, b):
    return lax.dot_general(a.astype(BF16), b.astype(BF16), (((1,), (1,)), ((), ())),
                           preferred_element_type=F32)


def _dot_tn(a, b):
    return lax.dot_general(a.astype(BF16), b.astype(BF16), (((0,), (0,)), ((), ())),
                           preferred_element_type=F32)


def _sigmoid(x):
    return 1.0 / (1.0 + jnp.exp(-x))


def _log_sigmoid(x):
    return jnp.minimum(x, 0.0) - jnp.log1p(jnp.exp(-jnp.abs(x)))


def _modulated_norm(x, g, shift, scale):
    y = x * lax.rsqrt(jnp.mean(x * x, axis=-1, keepdims=True) + EPS) * g
    return y * (1.0 + scale) + shift


def _resident(shape):
    return pl.BlockSpec(shape, lambda *_: (0,) * len(shape), pipeline_mode=pl.Buffered(1))


def _mods_kernel(c_ref, w_ref, b_ref, o_ref):
    c = c_ref[...]
    o_ref[...] = _dot(c * _sigmoid(c), w_ref[...]) + b_ref[...]


def _mods(cvecs, w_ada, b_ada):
    n = N_MOD * D_MODEL
    tn = D_MODEL
    return pl.pallas_call(
        _mods_kernel,
        grid=(n // tn,),
        in_specs=[pl.BlockSpec((8, D_MODEL), lambda j: (0, 0)),
                  pl.BlockSpec((D_MODEL, tn), lambda j: (0, j)),
                  pl.BlockSpec((1, tn), lambda j: (0, j))],
        out_specs=pl.BlockSpec((8, tn), lambda j: (0, j)),
        out_shape=jax.ShapeDtypeStruct((8, n), F32),
        compiler_params=_params(1),
        name="adaln_mods",
    )(cvecs, w_ada, b_ada.reshape(1, n))


def _ffn_kernel(x_ref, m_ref, g_ref, w1_ref, w3_ref, w2_ref, o_ref, w1_sc, w3_sc, w2_sc, hh_sc, acc_sc, *, base):
    g = pl.program_id(0)
    nf = D_FF // FFN_TF
    norm = lambda x: _modulated_norm(x, g_ref[...], m_ref[base:base + 1, :], m_ref[base + 1:base + 2, :]).astype(BF16)
    finish = lambda x, acc: x + 0.5 * m_ref[base + 2:base + 3, :] * acc

    def tile(hh, f):
        a = _dot(hh, w1_sc[f])
        b = _dot(hh, w3_sc[f])
        return _dot(a * _sigmoid(a) * b, w2_sc[f])

    @pl.when(g < nf)
    def _():
        w1_sc[g] = w1_ref[...].astype(BF16)
        w3_sc[g] = w3_ref[...].astype(BF16)
        w2_sc[g] = w2_ref[...].astype(BF16)

        @pl.when(g == 0)
        def _():
            hh_sc[...] = norm(x_ref[...])
            acc_sc[...] = jnp.zeros(acc_sc.shape, F32)

        acc_sc[...] += tile(hh_sc[...], g)

        @pl.when(g == nf - 1)
        def _():
            o_ref[...] = finish(x_ref[...], acc_sc[...])

    @pl.when(g >= nf)
    def _():
        x = x_ref[...]
        hh = norm(x)
        acc = jnp.zeros(x.shape, F32)
        for f in range(nf):
            acc = acc + tile(hh, f)
        o_ref[...] = finish(x, acc)


def _ffn(x, mods, mod_index, g, w1, w3, w2, base):
    rows = x.shape[0]
    nf = D_FF // FFN_TF
    row_tile = lambda s: jnp.maximum(s - (nf - 1), 0)
    f_tile = lambda s: jnp.minimum(s, nf - 1)
    row = lambda s: (row_tile(s), 0)
    return pl.pallas_call(
        functools.partial(_ffn_kernel, base=base),
        grid=(nf - 1 + rows // FFN_TM,),
        in_specs=[pl.BlockSpec((FFN_TM, D_MODEL), row),
                  pl.BlockSpec((None, N_MOD, D_MODEL), lambda s: (mod_index(row_tile(s) * FFN_TM), 0, 0)),
                  _resident((1, D_MODEL)),
                  pl.BlockSpec((D_MODEL, FFN_TF), lambda s: (0, f_tile(s))),
                  pl.BlockSpec((D_MODEL, FFN_TF), lambda s: (0, f_tile(s))),
                  pl.BlockSpec((FFN_TF, D_MODEL), lambda s: (f_tile(s), 0))],
        out_specs=pl.BlockSpec((FFN_TM, D_MODEL), row),
        out_shape=jax.ShapeDtypeStruct((rows, D_MODEL), F32),
        scratch_shapes=[pltpu.VMEM((nf, D_MODEL, FFN_TF), BF16), pltpu.VMEM((nf, D_MODEL, FFN_TF), BF16),
                        pltpu.VMEM((nf, FFN_TF, D_MODEL), BF16),
                        pltpu.VMEM((FFN_TM, D_MODEL), BF16), pltpu.VMEM((FFN_TM, D_MODEL), F32)],
        compiler_params=_params(1),
        name="ffn",
    )(x, mods, g.reshape(1, D_MODEL), w1, w3, w2)


def _group_norm64(x, bd, g):
    ss = _dot(x * x, bd)
    return x * lax.rsqrt(ss * (1.0 / DA_HEAD_DIM) + EPS) * g


def _rope(x, cos, sin_signed):
    first = (lax.broadcasted_iota(jnp.int32, x.shape, 1) % 32) < 16
    partner = jnp.where(first, pltpu.roll(x, LANES - 16, 1), pltpu.roll(x, 16, 1))
    return x * cos + partner * sin_signed


N_SEG = 9
GATE_LO = 4 * D_MODEL


def _seg_start(s):
    return s * D_MODEL + N_GATE_COLS * (s >= 4)


def _proj_kernel(*refs, rope):
    (x_ref, m_ref, g_ref, wt_ref, wgt_ref, bg_ref, bd_ref, gq_ref, gk_ref), refs = refs[:9], refs[9:]
    if rope:
        (cos_ref, sin_ref), refs = refs[:2], refs[2:]
    (mq_ref, mk_ref, mv_ref, so_ref, dq_ref, dk_ref, dv_ref, sgm_ref, sgd_ref, gates_ref,
     w_sc, wg_sc, hh_sc) = refs
    step = pl.program_id(0)
    norm = lambda: _modulated_norm(x_ref[...], g_ref[...], m_ref[3:4, :], m_ref[4:5, :]).astype(BF16)

    def gates(hh):
        y = _dot_nt(hh, wg_sc[...]) + bg_ref[...]
        col = lax.broadcasted_iota(jnp.int32, y.shape, 1)
        gates_ref[...] = jnp.where((col // ML_HEADS) % 2 == 1, _log_sigmoid(y), y)

    def qk_segment(y, g_head_ref, o_ref):
        bd = bd_ref[...]
        for c in range(D_MODEL // MXU_DIM):
            cols = slice(c * MXU_DIM, (c + 1) * MXU_DIM)
            z = _group_norm64(y[:, cols], bd, g_head_ref[...])
            if rope:
                z = jnp.concatenate(
                    [_rope(z[:, k * LANES:(k + 1) * LANES], cos_ref[...], sin_ref[...])
                     for k in range(MXU_DIM // LANES)], axis=1)
            o_ref[:, cols] = z.astype(o_ref.dtype)

    def cache_key_segment(hh):
        y_t = _dot_nt(w_sc[5], hh)
        z = y_t.reshape(D_MODEL // DA_HEAD_DIM, DA_HEAD_DIM, y_t.shape[1])
        ms = jnp.mean(z * z, axis=1, keepdims=True)
        gain = jnp.broadcast_to(gk_ref[...], z.shape[1:])
        dk_ref[...] = (z * lax.rsqrt(ms + EPS) * gain).reshape(y_t.shape)

    def segment(s, hh):
        if s in (0, 2):
            o_ref = {0: mq_ref, 2: mv_ref}[s]
            y_t = _dot_nt(w_sc[s], hh) * ((ML_HEAD_DIM ** -0.5) if s == 0 else 1.0)
            for j in range(PROJ_TM // CHUNK):
                o_ref[j] = y_t[:, j * CHUNK:(j + 1) * CHUNK].astype(o_ref.dtype)
            return
        if s == 5 and not rope:
            cache_key_segment(hh)
            return
        y = _dot_nt(hh, w_sc[s])
        if s == 4:
            qk_segment(y, gq_ref, dq_ref)
        elif s == 5:
            qk_segment(y, gk_ref, dk_ref)
        elif s in (3, 7, 8):
            {3: so_ref, 7: sgm_ref, 8: sgd_ref}[s][...] = _sigmoid(y)
        elif s == 6 and not rope:
            dv_ref[...] = y.reshape(y.shape[0], DA_HEADS, DA_V_DIM)
        else:
            o_ref = {1: mk_ref, 6: dv_ref}[s]
            o_ref[...] = y.astype(o_ref.dtype)

    for s in range(N_SEG):
        @pl.when(step == s)
        def _(s=s):
            w_sc[s] = wt_ref[...].astype(BF16)
            if s == 0:
                wg_sc[...] = jnp.zeros(wg_sc.shape, BF16)
                wg_sc[0:N_GATE_COLS, :] = wgt_ref[...].astype(BF16)
                hh_sc[...] = norm()
                gates(hh_sc[...])
            segment(s, hh_sc[...])

    @pl.when(step >= N_SEG)
    def _():
        hh = norm()
        gates(hh)
        for s in range(N_SEG):
            segment(s, hh)


def _proj(x, mods, mod_index, g, w_in_t, b_gate, bd, gq, gk, rope_tabs, seq_len):
    rows = x.shape[0]
    rope = rope_tabs is not None
    row_tile = lambda s: jnp.maximum(s - (N_SEG - 1), 0)
    row = lambda s: (row_tile(s), 0)
    tile = pl.BlockSpec((PROJ_TM, D_MODEL), row)
    elem = lambda n: (pl.Element(n), pl.Element(D_MODEL))
    seg_row = lambda s: pl.multiple_of(_seg_start(jnp.minimum(s, N_SEG - 1)), N_GATE_COLS)
    in_specs = [tile,
                pl.BlockSpec((None, N_MOD, D_MODEL), lambda s: (mod_index(row_tile(s) * PROJ_TM), 0, 0)),
                _resident((1, D_MODEL)),
                pl.BlockSpec(elem(D_MODEL), lambda s: (seg_row(s), 0)),
                pl.BlockSpec(elem(N_GATE_COLS), lambda s: (GATE_LO, 0)),
                _resident((1, GATE_PAD)),
                _resident((MXU_DIM, MXU_DIM)),
                _resident((1, MXU_DIM)),
                _resident(gk.shape)]
    args = [x, mods, g.reshape(1, D_MODEL), w_in_t, w_in_t, b_gate, bd, gq, gk]
    if rope:
        tiles_per_seq = seq_len // PROJ_TM
        tab = pl.BlockSpec((PROJ_TM, LANES), lambda s: (row_tile(s) % tiles_per_seq, 0))
        in_specs += [tab, tab]
        args += list(rope_tabs)
    dtypes = [BF16, BF16, BF16, F32, BF16, BF16, BF16 if rope else F32, F32, F32]
    out_shape = [jax.ShapeDtypeStruct((rows, D_MODEL), dt) for dt in dtypes]
    out_shape.append(jax.ShapeDtypeStruct((rows, GATE_PAD), F32))
    out_specs = [tile] * 9 + [pl.BlockSpec((PROJ_TM, GATE_PAD), row)]
    slabs = PROJ_TM // CHUNK
    for j in (0, 2):
        out_shape[j] = jax.ShapeDtypeStruct((rows // CHUNK, D_MODEL, CHUNK), BF16)
        out_specs[j] = pl.BlockSpec((slabs, D_MODEL, CHUNK), lambda s: (row_tile(s), 0, 0))
    if not rope:
        assert seq_len == PROJ_TM
        out_shape[5] = jax.ShapeDtypeStruct((rows // seq_len, D_MODEL, seq_len), F32)
        out_specs[5] = pl.BlockSpec((None, D_MODEL, seq_len), lambda s: (row_tile(s), 0, 0))
        out_shape[6] = jax.ShapeDtypeStruct((rows, DA_HEADS, DA_V_DIM), F32)
        out_specs[6] = pl.BlockSpec((PROJ_TM, DA_HEADS, DA_V_DIM), lambda s: (row_tile(s), 0, 0))
    return pl.pallas_call(
        functools.partial(_proj_kernel, rope=rope),
        grid=(N_SEG - 1 + rows // PROJ_TM,),
        in_specs=in_specs,
        out_specs=out_specs,
        out_shape=out_shape,
        scratch_shapes=[pltpu.VMEM((N_SEG, D_MODEL, D_MODEL), BF16), pltpu.VMEM((GATE_PAD, D_MODEL), BF16),
                        pltpu.VMEM((PROJ_TM, D_MODEL), BF16)],
        compiler_params=_params(1),
        name="mixer_in_proj",
    )(*args)


def _lambda(lam_ref, lam_init):
    lam = lam_ref[...]
    s1 = jnp.sum(lam[0:1, :] * lam[1:2, :], axis=1, keepdims=True)
    s2 = jnp.sum(lam[2:3, :] * lam[3:4, :], axis=1, keepdims=True)
    return jnp.exp(s1) - jnp.exp(s2) + lam_init


def _attn_kernel(*refs, cached, seq, lam_init):
    if cached:
        q_ref, k_ref, v_ref, ck_ref, cv_ref, lam_ref, gs_ref, o_ref, kall_sc, vt_sc, s_sc = refs
    else:
        q_ref, k_ref, v_ref, lam_ref, gs_ref, o_ref, kall_sc, vt_sc, s_sc = refs
    def cache_order_kv(kt_ref, vr_ref, h, lo, n):
        kall_sc[h, lo:lo + n, :] = kt_ref[h * DA_V_DIM:(h + 1) * DA_V_DIM, :].T.astype(BF16)
        vt_sc[h, :, lo:lo + n] = vr_ref[pl.ds(h, n, stride=DA_HEADS), :].T.astype(BF16)

    @pl.when(pl.program_id(1) == 0)
    def _():
        for h in range(DA_HEADS):
            if cached:
                cols = slice(h * DA_V_DIM, (h + 1) * DA_V_DIM)
                kall_sc[h, 0:seq, :] = k_ref[:, cols].astype(BF16)
                vt_sc[h, :, 0:seq] = v_ref[:, cols].astype(F32).T.astype(BF16)
                cache_order_kv(ck_ref, cv_ref, h, seq, ck_ref.shape[1])
            else:
                cache_order_kv(k_ref, v_ref, h, 0, seq)

    lam = _lambda(lam_ref, lam_init)
    lane = lax.broadcasted_iota(jnp.int32, (1, DA_V_DIM), 1)
    comp_masks = [lane < DA_HEAD_DIM, lane >= DA_HEAD_DIM]
    tq = q_ref.shape[0]
    n_keys = kall_sc.shape[1]
    n_tiles = pl.cdiv(n_keys, ATT_TK_MAX)
    tk = n_keys // n_tiles

    def stacked_q(h):
        q = q_ref[:, h * DA_V_DIM:(h + 1) * DA_V_DIM].astype(BF16)
        return jnp.concatenate([jnp.where(m, q, jnp.zeros_like(q)) for m in comp_masks], axis=0)

    def score_tile(h, j, qq, m8):
        rows = slice(j * tk, (j + 1) * tk)
        st = _dot_nt(kall_sc[h, rows, :], qq)
        s_sc[h % 2, rows, :] = st
        t8 = jnp.max(st.reshape(tk // 8, 8, 2 * tq), axis=0)
        return t8 if m8 is None else jnp.maximum(m8, t8)

    def prob_tile(h, j, mx, d8, pv):
        rows = slice(j * tk, (j + 1) * tk)
        e = jnp.exp2(s_sc[h % 2, rows, :] - mx)
        s8 = jnp.sum(e.reshape(tk // 8, 8, 2 * tq), axis=0)
        p = _dot(vt_sc[h, :, rows], e)
        return (s8 if d8 is None else d8 + s8), (p if pv is None else pv + p)

    qq = stacked_q(0)
    m8 = None
    for j in range(n_tiles):
        m8 = score_tile(0, j, qq, m8)
    for h in range(DA_HEADS):
        mx = jnp.max(m8, axis=0, keepdims=True)
        if h + 1 < DA_HEADS:
            qq = stacked_q(h + 1)
        m8, d8, pv = None, None, None
        for j in range(n_tiles):
            if h + 1 < DA_HEADS:
                m8 = score_tile(h + 1, j, qq, m8)
            d8, pv = prob_tile(h, j, mx, d8, pv)
        pv = pv * (1.0 / jnp.sum(d8, axis=0, keepdims=True))
        out_t = pv[:, :tq] - lam * pv[:, tq:]
        out_t = out_t * lax.rsqrt(jnp.mean(out_t * out_t, axis=0, keepdims=True) + EPS)
        out = out_t.T * gs_ref[...]
        o_ref[:, h * DA_V_DIM:(h + 1) * DA_V_DIM] = (out * (1.0 - lam_init)).astype(o_ref.dtype)


def _attn(q, k, v, cache, lam_vecs, g_sub, batch, seq_len, lam_init):
    rows = q.shape[0]
    nq = seq_len // ATT_TQ
    q_spec = pl.BlockSpec((ATT_TQ, D_MODEL), lambda b, i: (b * nq + i, 0))
    cache_specs = lambda n: [pl.BlockSpec((None, D_MODEL, n), lambda b, i: (b, 0, 0)),
                             pl.BlockSpec((None, n * DA_HEADS, DA_V_DIM), lambda b, i: (b, 0, 0))]
    n_keys = seq_len
    if cache is not None:
        past = cache[0].shape[2]
        n_keys += past
        kv_spec = pl.BlockSpec((seq_len, D_MODEL), lambda b, i: (b, 0))
        in_specs = [q_spec, kv_spec, kv_spec] + cache_specs(past)
        args = [q, k, v] + list(cache)
    else:
        in_specs = [q_spec] + cache_specs(seq_len)
        args = [q, k, v.reshape(batch, seq_len * DA_HEADS, DA_V_DIM)]
    in_specs += [_resident((4, DA_HEAD_DIM)), _resident((1, DA_V_DIM))]
    args += [lam_vecs, g_sub.reshape(1, DA_V_DIM)]
    return pl.pallas_call(
        functools.partial(_attn_kernel, cached=cache is not None, seq=seq_len, lam_init=lam_init),
        grid=(batch, nq),
        in_specs=in_specs,
        out_specs=q_spec,
        out_shape=jax.ShapeDtypeStruct((rows, D_MODEL), BF16),
        scratch_shapes=[pltpu.VMEM((DA_HEADS, n_keys, DA_V_DIM), BF16),
                        pltpu.VMEM((DA_HEADS, DA_V_DIM, n_keys), BF16),
                        pltpu.VMEM((2, n_keys, 2 * ATT_TQ), F32)],
        compiler_params=_params(2),
        name="diff_attention",
    )(*args)


def _per_chain(fn, a, b):
    return jnp.stack([fn(a[i], b[i]) for i in range(a.shape[0])])


ML_EXT = 8


def _rows(x):
    return jnp.stack([x[i:i + 1, :] for i in range(x.shape[0])])


def _split3(x):
    hi = x.astype(BF16)
    r = x - hi.astype(F32)
    mid = r.astype(BF16)
    lo = (r - mid.astype(F32)).astype(BF16)
    return jnp.concatenate([hi, mid, lo], axis=1)


def _chunk_scan(x, reverse_rows, tri_prefix, tri_suffix):
    parts = _split3(x)
    return jnp.where(reverse_rows, _dot(parts, tri_suffix), _dot(parts, tri_prefix))


def _paired_value_matmul(v_t, sc):
    B, L, _ = sc.shape
    zero = jnp.zeros((L, L), sc.dtype)
    out = []
    for i in range(0, B, 2):
        lhs = jnp.concatenate([v_t[i], v_t[i + 1]], axis=1)
        rhs = jnp.concatenate([jnp.concatenate([sc[i], zero], axis=1),
                               jnp.concatenate([zero, sc[i + 1]], axis=1)], axis=0)
        both = _dot(lhs, rhs)
        out += [both[:, :L], both[:, L:]]
    return jnp.stack(out)


def _mlstm_step(k, q_t, v_t, ic, fc, CT, m, seen_t, reverse_rows):
    B, L = ic.shape
    d = k.shape[2]
    tri = lambda keep: jnp.concatenate([jnp.where(keep, 1.0, 0.0).astype(BF16)] * 3, axis=0)
    b2 = _chunk_scan(fc, reverse_rows, tri(seen_t[0]), tri(seen_t[B - 1]))
    u2 = ic - b2
    u_t = jnp.concatenate([u2, jnp.zeros((L - B, L), F32)], axis=0).T
    u_col = jnp.stack([jnp.broadcast_to(u_t[:, i:i + 1], (L, L)) for i in range(B)])
    b, i_g, f_g = _rows(b2), _rows(ic), _rows(fc)
    b_last = jnp.sum(f_g, axis=2, keepdims=True)

    log_d = jnp.where(seen_t, b + u_col, -jnp.inf)
    a = b + m
    m_t = jnp.maximum(a, jnp.max(log_d, axis=1, keepdims=True))
    dmat = jnp.exp(log_d - m_t)
    inter = jnp.exp(a - m_t)
    sc = _per_chain(_dot, k, q_t) * dmat
    cq = _per_chain(_dot, CT, q_t)
    num = _paired_value_matmul(v_t, sc.astype(BF16)) + inter * cq[:, :d, :]
    den = jnp.sum(sc, axis=1, keepdims=True) + inter * cq[:, d:d + 1, :]
    h_t = num * (1.0 / jnp.maximum(jnp.abs(den), jnp.exp(-m_t)))

    g = b_last - b + i_g
    m_new = jnp.maximum(b_last + m, jnp.max(g, axis=2, keepdims=True))
    w = jnp.exp(g - m_new)
    decay = jnp.exp(b_last + m - m_new)
    vw = jnp.concatenate([v_t.astype(F32), jnp.ones((B, ML_EXT, L), F32)], axis=1) * w
    CT_new = decay * CT + _per_chain(_dot, vw, k)
    return h_t, CT_new, m_new


def _mlstm_kernel(*refs, seq_len, has_state, emit_state):
    q_ref, k_ref, v_ref, so_ref, g_ref, gmh_ref = refs[:6]
    refs = refs[6:]
    if has_state:
        (c0_ref, n0_ref, m0_ref), refs = refs[:3], refs[3:]
    hm_ref, refs = refs[0], refs[1:]
    if emit_state:
        (c_out_ref, n_out_ref, m_out_ref), refs = refs[:3], refs[3:]
    ct_sc, m_sc, gr_sc, h_sc = refs

    nc = seq_len // CHUNK
    d_head = ML_HEAD_DIM
    chains = [(d, h) for d in range(2) for h in range(ML_HEADS)]
    n_chain = len(chains)
    for i, (d, h) in enumerate(chains):
        if has_state:
            ct_sc[i, 0:d_head, :] = c0_ref[d, h].T
            ct_sc[i, d_head:d_head + ML_EXT, :] = jnp.broadcast_to(n0_ref[d, h], (ML_EXT, d_head))
            m_sc[i] = m0_ref[d, h][:, 0:1]
        else:
            ct_sc[i] = jnp.zeros((d_head + ML_EXT, d_head), F32)
            m_sc[i] = jnp.zeros((1, 1), F32)
    for c in range(nc):
        gr_sc[c] = g_ref[c * CHUNK:(c + 1) * CHUNK, :].T

    s_idx = lax.broadcasted_iota(jnp.int32, (CHUNK, CHUNK), 0)
    t_idx = lax.broadcasted_iota(jnp.int32, (CHUNK, CHUNK), 1)
    seen_t = jnp.stack([s_idx <= t_idx] * ML_HEADS + [s_idx >= t_idx] * ML_HEADS)
    reverse_rows = lax.broadcasted_iota(jnp.int32, (n_chain, CHUNK), 0) >= ML_HEADS

    def step(c_fwd, c_bwd, rows_of):
        chunk_of = (c_fwd, c_bwd)
        g_fwd, g_bwd = gr_sc[c_fwd], gr_sc[c_bwd]
        lo = 2 * ML_HEADS
        ic = jnp.concatenate([g_fwd[0:ML_HEADS], g_bwd[lo:lo + ML_HEADS]], axis=0)
        fc = jnp.concatenate([g_fwd[ML_HEADS:lo], g_bwd[lo + ML_HEADS:2 * lo]], axis=0)
        head = [slice(h * d_head, (h + 1) * d_head) for _, h in chains]
        stack = lambda pick: jnp.stack([pick(i, chunk_of[d]) for i, (d, _) in enumerate(chains)])
        h_t, CT_new, m_new = _mlstm_step(
            stack(lambda i, c: k_ref[rows_of(c), head[i]]),
            stack(lambda i, c: q_ref[c, head[i], :]),
            stack(lambda i, c: v_ref[c, head[i], :]),
            ic, fc, ct_sc[...], m_sc[...], seen_t, reverse_rows)
        ct_sc[...] = CT_new
        m_sc[...] = m_new
        for i, (d, _) in enumerate(chains):
            h_sc[d, chunk_of[d], head[i], :] = h_t[i]

    if nc <= 2:
        for c in range(nc):
            step(c, nc - 1 - c, lambda cc: slice(cc * CHUNK, (cc + 1) * CHUNK))
    else:
        def body(c, carry):
            step(c, nc - 1 - c, lambda cc: pl.ds(pl.multiple_of(cc * CHUNK, CHUNK), CHUNK))
            return carry
        lax.fori_loop(0, nc, body, 0)

    for h in range(ML_HEADS):
        hcols = slice(h * d_head, (h + 1) * d_head)
        gain = jnp.broadcast_to(gmh_ref[h], (d_head, CHUNK))
        for c in range(nc):
            rows = slice(c * CHUNK, (c + 1) * CHUNK)
            hsum = h_sc[0, c, hcols, :] + h_sc[1, c, hcols, :]
            hn = hsum * lax.rsqrt(jnp.mean(hsum * hsum, axis=0, keepdims=True) + EPS) * gain
            hm_ref[rows, hcols] = (hn.T * so_ref[rows, hcols]).astype(hm_ref.dtype)
    if emit_state:
        for i, (d, h) in enumerate(chains):
            c_out_ref[d, h] = ct_sc[i, 0:d_head, :]
            n_out_ref[d, h] = ct_sc[i, d_head:d_head + 1, :]
            m_out_ref[d, h] = jnp.broadcast_to(m_sc[i], (1, LANES))


def _mlstm(q_t, k, v_t, so, gates, g_mh, state, batch, seq_len, emit_state):
    rows = k.shape[0]
    d = ML_HEAD_DIM
    nc = seq_len // CHUNK
    tile = pl.BlockSpec((seq_len, D_MODEL), lambda b: (b, 0))
    tile_t = pl.BlockSpec((nc, D_MODEL, CHUNK), lambda b: (b, 0, 0))
    in_specs = [tile_t, tile, tile_t, tile,
                pl.BlockSpec((seq_len, GATE_PAD), lambda b: (b, 0)),
                _resident((ML_HEADS, d, 1))]
    args = [q_t, k, v_t, so, gates, g_mh.reshape(ML_HEADS, d, 1)]
    c_spec = pl.BlockSpec((None, 2, ML_HEADS, d, d), lambda b: (b, 0, 0, 0, 0))
    n_spec = pl.BlockSpec((None, 2, ML_HEADS, 1, d), lambda b: (b, 0, 0, 0, 0))
    m_spec = pl.BlockSpec((None, 2, ML_HEADS, 1, LANES), lambda b: (b, 0, 0, 0, 0))
    if state is not None:
        in_specs += [c_spec, n_spec, m_spec]
        args += list(state)
    out_specs = [tile]
    out_shape = [jax.ShapeDtypeStruct((rows, D_MODEL), BF16)]
    if emit_state:
        out_specs += [c_spec, n_spec, m_spec]
        out_shape += [jax.ShapeDtypeStruct((batch, 2, ML_HEADS, d, d), F32),
                      jax.ShapeDtypeStruct((batch, 2, ML_HEADS, 1, d), F32),
                      jax.ShapeDtypeStruct((batch, 2, ML_HEADS, 1, LANES), F32)]
    n_state = 2 * ML_HEADS
    return pl.pallas_call(
        functools.partial(_mlstm_kernel, seq_len=seq_len, has_state=state is not None, emit_state=emit_state),
        grid=(batch,),
        in_specs=in_specs,
        out_specs=out_specs,
        out_shape=out_shape,
        scratch_shapes=[pltpu.VMEM((n_state, d + ML_EXT, d), F32), pltpu.VMEM((n_state, 1, 1), F32),
                        pltpu.VMEM((nc, GATE_PAD, CHUNK), F32),
                        pltpu.VMEM((2, nc, D_MODEL, CHUNK), F32)],
        compiler_params=_params(1),
        name="mlstm",
    )(*args)


def _merge_kernel(x_ref, m_ref, hm_ref, att_ref, sgm_ref, sgd_ref, wm_ref, wd_ref, wo_ref, o_ref, w_sc):
    @pl.when(pl.program_id(0) == 0)
    def _():
        for j, w_ref in enumerate((wm_ref, wd_ref, wo_ref)):
            w_sc[j] = w_ref[...].astype(BF16)

    y = sgm_ref[...] * _dot(hm_ref[...], w_sc[0]) + sgd_ref[...] * _dot(att_ref[...], w_sc[1])
    o_ref[...] = x_ref[...] + m_ref[5:6, :] * _dot(y, w_sc[2])


def _merge(x, mods, mod_index, hm, att, sgm, sgd, wm, wd, wo):
    rows = x.shape[0]
    tile = pl.BlockSpec((MERGE_TM, D_MODEL), lambda i: (i, 0))
    w_spec = _resident((D_MODEL, D_MODEL))
    return pl.pallas_call(
        _merge_kernel,
        grid=(rows // MERGE_TM,),
        in_specs=[tile, pl.BlockSpec((None, N_MOD, D_MODEL), lambda i: (mod_index(i * MERGE_TM), 0, 0)),
                  tile, tile, tile, tile, w_spec, w_spec, w_spec],
        out_specs=tile,
        out_shape=jax.ShapeDtypeStruct((rows, D_MODEL), F32),
        scratch_shapes=[pltpu.VMEM((3, D_MODEL, D_MODEL), BF16)],
        compiler_params=_params(1),
        name="branch_merge",
    )(x, mods, hm, att, sgm, sgd, wm, wd, wo)


def _rope_tables(seq_len):
    lane = np.arange(LANES)
    r = lane % 32
    freqs = np.power(np.float32(ROPE_BASE), -(r % 16).astype(np.float32) / np.float32(16.0))
    tok = np.arange(seq_len)
    pos = np.where((lane % 64 < 32)[None, :], (tok // GRID_W)[:, None], (tok % GRID_W)[:, None]).astype(np.float32)
    ang = pos * freqs[None, :]
    sign = np.where(r < 16, -1.0, 1.0).astype(np.float32)
    return jnp.asarray(np.cos(ang), F32), jnp.asarray(np.sin(ang) * sign[None, :], F32)


def _layer(x, mods, mod_index, w, batch, seq_len, ctx, lam_init):
    x = _ffn(x, mods, mod_index, w["g_norm"][0], w["ffn1_w1"], w["ffn1_w3"], w["ffn1_w2"], base=0)
    rope_tabs = None if ctx is None else _rope_tables(seq_len)
    mq, mk, mv, so, dq, dk, dv, sgm, sgd, gates = _proj(
        x, mods, mod_index, w["g_norm"][1], w["w_in_t"], w["b_gate"], w["bd"], w["g_qn"],
        w["g_kn_col"] if ctx is None else w["g_kn"], rope_tabs, seq_len)
    cache = None if ctx is None else (ctx[0], ctx[1])
    att = _attn(dq, dk, dv, cache, w["lam"], w["g_sub"], batch, seq_len, lam_init)
    state = None if ctx is None else ctx[2]
    res = _mlstm(mq, mk, mv, so, gates, w["g_mh"], state, batch, seq_len, emit_state=ctx is None)
    x = _merge(x, mods, mod_index, res[0], att, sgm, sgd, w["w_br_m"], w["w_br_d"], w["w_out"])
    x = _ffn(x, mods, mod_index, w["g_norm"][2], w["ffn2_w1"], w["ffn2_w3"], w["ffn2_w2"], base=6)
    return x, dk, dv, res[1:]


def kernel(x_prompt, x_sample, c, cache_k, cache_v, state_C, state_n, state_m, c_ctx, w_ada, b_ada, g_norm, ffn1_w1, ffn1_w3, ffn1_w2, ffn2_w1, ffn2_w3, ffn2_w2, w_in, b_gate, g_qn, g_kn, lam_q1, lam_k1, lam_q2, lam_k2, g_sub, g_mh, w_br_m, w_br_d, w_out):
    depth = w_ada.shape[0]
    assert depth == 1
    l = 0
    bp, tp, _ = x_prompt.shape
    bs, ts, _ = x_sample.shape
    past = cache_k.shape[2]
    lam_init = 0.8 - 0.6 * math.exp(-0.3 * l)

    cvecs = jnp.concatenate([c_ctx[None, :], c, jnp.zeros((8 - 1 - bs, D_MODEL), F32)], axis=0)
    mods = _mods(cvecs, w_ada[l], b_ada[l]).reshape(8, N_MOD, D_MODEL)

    group = jnp.arange(MXU_DIM) // DA_HEAD_DIM
    w = dict(
        g_norm=g_norm[l],
        ffn1_w1=ffn1_w1[l], ffn1_w3=ffn1_w3[l], ffn1_w2=ffn1_w2[l],
        ffn2_w1=ffn2_w1[l], ffn2_w3=ffn2_w3[l], ffn2_w2=ffn2_w2[l],
        w_in_t=w_in[l].T,
        b_gate=jnp.pad(b_gate[l], (0, GATE_PAD - N_GATE_COLS)).reshape(1, GATE_PAD),
        bd=(group[:, None] == group[None, :]).astype(BF16),
        g_qn=jnp.tile(g_qn[l] * QK_LOG2_SCALE, MXU_DIM // DA_HEAD_DIM).reshape(1, MXU_DIM),
        g_kn=jnp.tile(g_kn[l], MXU_DIM // DA_HEAD_DIM).reshape(1, MXU_DIM),
        g_kn_col=g_kn[l].reshape(DA_HEAD_DIM, 1),
        lam=jnp.stack([lam_q1[l], lam_k1[l], lam_q2[l], lam_k2[l]]),
        g_sub=g_sub[l], g_mh=g_mh[l],
        w_br_m=w_br_m[l], w_br_d=w_br_d[l], w_out=w_out[l],
    )

    xp, new_k_t, new_v, (new_c, new_n, new_m) = _layer(
        x_prompt.reshape(bp * tp, D_MODEL), mods, lambda r: 0, w, bp, tp, None, lam_init)

    ctx = (cache_k[:, l].transpose(0, 2, 3, 4, 1).reshape(bs, D_MODEL, past),
           cache_v[:, l].reshape(bs, past * DA_HEADS, DA_V_DIM),
           (state_C[:, l], state_n[:, l].reshape(bs, 2, ML_HEADS, 1, ML_HEAD_DIM),
            jnp.broadcast_to(state_m[:, l][..., None, None], (bs, 2, ML_HEADS, 1, LANES))))
    xs, _, _, _ = _layer(x_sample.reshape(bs * ts, D_MODEL), mods, lambda r: 1 + r // ts, w, bs, ts, ctx, lam_init)

    return (xp.reshape(bp, tp, D_MODEL), xs.reshape(bs, ts, D_MODEL),
            new_k_t.reshape(bp, DA_HEADS, 2, DA_HEAD_DIM, tp).transpose(0, 4, 1, 2, 3)[:, None],
            new_v.reshape(bp, 1, tp, DA_HEADS, DA_V_DIM),
            jnp.swapaxes(new_c, -1, -2)[:, None], new_n.reshape(bp, 1, 2, ML_HEADS, ML_HEAD_DIM),
            new_m[..., 0, 0][:, None])
```

```python
import functools
import math

import jax
import jax.numpy as jnp
import numpy as np
from jax import lax
from jax.experimental import pallas as pl
from jax.experimental.pallas import tpu as pltpu

F32 = jnp.float32
BF16 = jnp.bfloat16

D_MODEL = 1024
D_FF = 2816
N_MOD = 9
GRID_W = 64
ML_HEADS = 4
ML_HEAD_DIM = 256
DA_HEADS = 8
DA_HEAD_DIM = 64
DA_V_DIM = 128
N_GATE_COLS = 16
CHUNK = 128
ROPE_BASE = 10000.0
QK_LOG2_SCALE = DA_HEAD_DIM ** -0.5 * math.log2(math.e)
EPS = 1e-6

LANES = 128
GATE_PAD = LANES
MXU_DIM = 256
VMEM_LIMIT = 56 * 1024 * 1024

FFN_TM = 1024
FFN_TF = 256
PROJ_TM = 256
MERGE_TM = 512
ATT_TQ = 256
ATT_TK_MAX = 768


def _params(n_axes):
    return pltpu.CompilerParams(dimension_semantics=("arbitrary",) * n_axes,
                                vmem_limit_bytes=VMEM_LIMIT)


def _dot(a, b):
    return jnp.dot(a.astype(BF16), b.astype(BF16), preferred_element_type=F32)


def _dot_nt(a, b):
    return lax.dot_general(a.astype(BF16), b.astype(BF16), (((1,), (1,)), ((), ())),
                           preferred_element_type=F32)


def _dot_tn(a, b):
    return lax.dot_general(a.astype(BF16), b.astype(BF16), (((0,), (0,)), ((), ())),
                           preferred_element_type=F32)


def _sigmoid(x):
    return 1.0 / (1.0 + jnp.exp(-x))


def _log_sigmoid(x):
    return jnp.minimum(x, 0.0) - jnp.log1p(jnp.exp(-jnp.abs(x)))


def _modulated_norm(x, g, shift, scale):
    y = x * lax.rsqrt(jnp.mean(x * x, axis=-1, keepdims=True) + EPS) * g
    return y * (1.0 + scale) + shift


def _resident(shape):
    return pl.BlockSpec(shape, lambda *_: (0,) * len(shape), pipeline_mode=pl.Buffered(1))


def _mods_kernel(c_ref, w_ref, b_ref, o_ref):
    c = c_ref[...]
    o_ref[...] = _dot(c * _sigmoid(c), w_ref[...]) + b_ref[...]


def _mods(cvecs, w_ada, b_ada):
    n = N_MOD * D_MODEL
    tn = D_MODEL
    return pl.pallas_call(
        _mods_kernel,
        grid=(n // tn,),
        in_specs=[pl.BlockSpec((8, D_MODEL), lambda j: (0, 0)),
                  pl.BlockSpec((D_MODEL, tn), lambda j: (0, j)),
                  pl.BlockSpec((1, tn), lambda j: (0, j))],
        out_specs=pl.BlockSpec((8, tn), lambda j: (0, j)),
        out_shape=jax.ShapeDtypeStruct((8, n), F32),
        compiler_params=_params(1),
        name="adaln_mods",
    )(cvecs, w_ada, b_ada.reshape(1, n))


def _ffn_kernel(x_ref, m_ref, g_ref, w1_ref, w3_ref, w2_ref, o_ref, w1_sc, w3_sc, w2_sc, hh_sc, acc_sc, *, base):
    g = pl.program_id(0)
    nf = D_FF // FFN_TF
    norm = lambda x: _modulated_norm(x, g_ref[...], m_ref[base:base + 1, :], m_ref[base + 1:base + 2, :]).astype(BF16)
    finish = lambda x, acc: x + 0.5 * m_ref[base + 2:base + 3, :] * acc

    def tile(hh, f):
        a = _dot(hh, w1_sc[f])
        b = _dot(hh, w3_sc[f])
        return _dot(a * _sigmoid(a) * b, w2_sc[f])

    @pl.when(g < nf)
    def _():
        w1_sc[g] = w1_ref[...].astype(BF16)
        w3_sc[g] = w3_ref[...].astype(BF16)
        w2_sc[g] = w2_ref[...].astype(BF16)

        @pl.when(g == 0)
        def _():
            hh_sc[...] = norm(x_ref[...])
            acc_sc[...] = jnp.zeros(acc_sc.shape, F32)

        acc_sc[...] += tile(hh_sc[...], g)

        @pl.when(g == nf - 1)
        def _():
            o_ref[...] = finish(x_ref[...], acc_sc[...])

    @pl.when(g >= nf)
    def _():
        x = x_ref[...]
        hh = norm(x)
        acc = jnp.zeros(x.shape, F32)
        for f in range(nf):
            acc = acc + tile(hh, f)
        o_ref[...] = finish(x, acc)


def _ffn(x, mods, mod_index, g, w1, w3, w2, base):
    rows = x.shape[0]
    nf = D_FF // FFN_TF
    row_tile = lambda s: jnp.maximum(s - (nf - 1), 0)
    f_tile = lambda s: jnp.minimum(s, nf - 1)
    row = lambda s: (row_tile(s), 0)
    return pl.pallas_call(
        functools.partial(_ffn_kernel, base=base),
        grid=(nf - 1 + rows // FFN_TM,),
        in_specs=[pl.BlockSpec((FFN_TM, D_MODEL), row),
                  pl.BlockSpec((None, N_MOD, D_MODEL), lambda s: (mod_index(row_tile(s) * FFN_TM), 0, 0)),
                  _resident((1, D_MODEL)),
                  pl.BlockSpec((D_MODEL, FFN_TF), lambda s: (0, f_tile(s))),
                  pl.BlockSpec((D_MODEL, FFN_TF), lambda s: (0, f_tile(s))),
                  pl.BlockSpec((FFN_TF, D_MODEL), lambda s: (f_tile(s), 0))],
        out_specs=pl.BlockSpec((FFN_TM, D_MODEL), row),
        out_shape=jax.ShapeDtypeStruct((rows, D_MODEL), F32),
        scratch_shapes=[pltpu.VMEM((nf, D_MODEL, FFN_TF), BF16), pltpu.VMEM((nf, D_MODEL, FFN_TF), BF16),
                        pltpu.VMEM((nf, FFN_TF, D_MODEL), BF16),
                        pltpu.VMEM((FFN_TM, D_MODEL), BF16), pltpu.VMEM((FFN_TM, D_MODEL), F32)],
        compiler_params=_params(1),
        name="ffn",
    )(x, mods, g.reshape(1, D_MODEL), w1, w3, w2)


def _group_norm64(x, bd, g):
    ss = _dot(x * x, bd)
    return x * lax.rsqrt(ss * (1.0 / DA_HEAD_DIM) + EPS) * g


def _rope(x, cos, sin_signed):
    first = (lax.broadcasted_iota(jnp.int32, x.shape, 1) % 32) < 16
    partner = jnp.where(first, pltpu.roll(x, LANES - 16, 1), pltpu.roll(x, 16, 1))
    return x * cos + partner * sin_signed


N_SEG = 9
GATE_LO = 4 * D_MODEL


def _seg_start(s):
    return s * D_MODEL + N_GATE_COLS * (s >= 4)


def _proj_kernel(*refs, rope):
    (x_ref, m_ref, g_ref, wt_ref, wgt_ref, bg_ref, bd_ref, gq_ref, gk_ref), refs = refs[:9], refs[9:]
    if rope:
        (cos_ref, sin_ref), refs = refs[:2], refs[2:]
    (mq_ref, mk_ref, mv_ref, so_ref, dq_ref, dk_ref, dv_ref, sgm_ref, sgd_ref, gates_ref,
     w_sc, wg_sc, hh_sc) = refs
    step = pl.program_id(0)
    norm = lambda: _modulated_norm(x_ref[...], g_ref[...], m_ref[3:4, :], m_ref[4:5, :]).astype(BF16)

    def gates(hh):
        y = _dot_nt(hh, wg_sc[...]) + bg_ref[...]
        col = lax.broadcasted_iota(jnp.int32, y.shape, 1)
        gates_ref[...] = jnp.where((col // ML_HEADS) % 2 == 1, _log_sigmoid(y), y)

    def qk_segment(y, g_head_ref, o_ref):
        bd = bd_ref[...]
        for c in range(D_MODEL // MXU_DIM):
            cols = slice(c * MXU_DIM, (c + 1) * MXU_DIM)
            z = _group_norm64(y[:, cols], bd, g_head_ref[...])
            if rope:
                z = jnp.concatenate(
                    [_rope(z[:, k * LANES:(k + 1) * LANES], cos_ref[...], sin_ref[...])
                     for k in range(MXU_DIM // LANES)], axis=1)
            o_ref[:, cols] = z.astype(o_ref.dtype)

    def cache_key_segment(hh):
        y_t = _dot_nt(w_sc[5], hh)
        z = y_t.reshape(D_MODEL // DA_HEAD_DIM, DA_HEAD_DIM, y_t.shape[1])
        ms = jnp.mean(z * z, axis=1, keepdims=True)
        gain = jnp.broadcast_to(gk_ref[...], z.shape[1:])
        dk_ref[...] = (z * lax.rsqrt(ms + EPS) * gain).reshape(y_t.shape)

    def segment(s, hh):
        if s in (0, 2):
            o_ref = {0: mq_ref, 2: mv_ref}[s]
            y_t = _dot_nt(w_sc[s], hh) * ((ML_HEAD_DIM ** -0.5) if s == 0 else 1.0)
            for j in range(PROJ_TM // CHUNK):
                o_ref[j] = y_t[:, j * CHUNK:(j + 1) * CHUNK].astype(o_ref.dtype)
            return
        if s == 5 and not rope:
            cache_key_segment(hh)
            return
        y = _dot_nt(hh, w_sc[s])
        if s == 4:
            qk_segment(y, gq_ref, dq_ref)
        elif s == 5:
            qk_segment(y, gk_ref, dk_ref)
        elif s in (3, 7, 8):
            {3: so_ref, 7: sgm_ref, 8: sgd_ref}[s][...] = _sigmoid(y)
        elif s == 6 and not rope:
            dv_ref[...] = y.reshape(y.shape[0], DA_HEADS, DA_V_DIM)
        else:
            o_ref = {1: mk_ref, 6: dv_ref}[s]
            o_ref[...] = y.astype(o_ref.dtype)

    for s in range(N_SEG):
        @pl.when(step == s)
        def _(s=s):
            w_sc[s] = wt_ref[...].astype(BF16)
            if s == 0:
                wg_sc[...] = jnp.zeros(wg_sc.shape, BF16)
                wg_sc[0:N_GATE_COLS, :] = wgt_ref[...].astype(BF16)
                hh_sc[...] = norm()
                gates(hh_sc[...])
            segment(s, hh_sc[...])

    @pl.when(step >= N_SEG)
    def _():
        hh = norm()
        gates(hh)
        for s in range(N_SEG):
            segment(s, hh)


def _proj(x, mods, mod_index, g, w_in_t, b_gate, bd, gq, gk, rope_tabs, seq_len):
    rows = x.shape[0]
    rope = rope_tabs is not None
    row_tile = lambda s: jnp.maximum(s - (N_SEG - 1), 0)
    row = lambda s: (row_tile(s), 0)
    tile = pl.BlockSpec((PROJ_TM, D_MODEL), row)
    elem = lambda n: (pl.Element(n), pl.Element(D_MODEL))
    seg_row = lambda s: pl.multiple_of(_seg_start(jnp.minimum(s, N_SEG - 1)), N_GATE_COLS)
    in_specs = [tile,
                pl.BlockSpec((None, N_MOD, D_MODEL), lambda s: (mod_index(row_tile(s) * PROJ_TM), 0, 0)),
                _resident((1, D_MODEL)),
                pl.BlockSpec(elem(D_MODEL), lambda s: (seg_row(s), 0)),
                pl.BlockSpec(elem(N_GATE_COLS), lambda s: (GATE_LO, 0)),
                _resident((1, GATE_PAD)),
                _resident((MXU_DIM, MXU_DIM)),
                _resident((1, MXU_DIM)),
                _resident(gk.shape)]
    args = [x, mods, g.reshape(1, D_MODEL), w_in_t, w_in_t, b_gate, bd, gq, gk]
    if rope:
        tiles_per_seq = seq_len // PROJ_TM
        tab = pl.BlockSpec((PROJ_TM, LANES), lambda s: (row_tile(s) % tiles_per_seq, 0))
        in_specs += [tab, tab]
        args += list(rope_tabs)
    dtypes = [BF16, BF16, BF16, F32, BF16, BF16, BF16 if rope else F32, F32, F32]
    out_shape = [jax.ShapeDtypeStruct((rows, D_MODEL), dt) for dt in dtypes]
    out_shape.append(jax.ShapeDtypeStruct((rows, GATE_PAD), F32))
    out_specs = [tile] * 9 + [pl.BlockSpec((PROJ_TM, GATE_PAD), row)]
    slabs = PROJ_TM // CHUNK
    for j in (0, 2):
        out_shape[j] = jax.ShapeDtypeStruct((rows // CHUNK, D_MODEL, CHUNK), BF16)
        out_specs[j] = pl.BlockSpec((slabs, D_MODEL, CHUNK), lambda s: (row_tile(s), 0, 0))
    if not rope:
        assert seq_len == PROJ_TM
        out_shape[5] = jax.ShapeDtypeStruct((rows // seq_len, D_MODEL, seq_len), F32)
        out_specs[5] = pl.BlockSpec((None, D_MODEL, seq_len), lambda s: (row_tile(s), 0, 0))
        out_shape[6] = jax.ShapeDtypeStruct((rows, DA_HEADS, DA_V_DIM), F32)
        out_specs[6] = pl.BlockSpec((PROJ_TM, DA_HEADS, DA_V_DIM), lambda s: (row_tile(s), 0, 0))
    return pl.pallas_call(
        functools.partial(_proj_kernel, rope=rope),
        grid=(N_SEG - 1 + rows // PROJ_TM,),
        in_specs=in_specs,
        out_specs=out_specs,
        out_shape=out_shape,
        scratch_shapes=[pltpu.VMEM((N_SEG, D_MODEL, D_MODEL), BF16), pltpu.VMEM((GATE_PAD, D_MODEL), BF16),
                        pltpu.VMEM((PROJ_TM, D_MODEL), BF16)],
        compiler_params=_params(1),
        name="mixer_in_proj",
    )(*args)


def _lambda(lam_ref, lam_init):
    lam = lam_ref[...]
    s1 = jnp.sum(lam[0:1, :] * lam[1:2, :], axis=1, keepdims=True)
    s2 = jnp.sum(lam[2:3, :] * lam[3:4, :], axis=1, keepdims=True)
    return jnp.exp(s1) - jnp.exp(s2) + lam_init


def _attn_kernel(*refs, cached, seq, lam_init):
    if cached:
        q_ref, k_ref, v_ref, ck_ref, cv_ref, lam_ref, gs_ref, o_ref, kall_sc, vt_sc, s_sc = refs
    else:
        q_ref, k_ref, v_ref, lam_ref, gs_ref, o_ref, kall_sc, vt_sc, s_sc = refs
    def cache_order_kv(kt_ref, vr_ref, h, lo, n):
        kall_sc[h, lo:lo + n, :] = kt_ref[h * DA_V_DIM:(h + 1) * DA_V_DIM, :].T.astype(BF16)
        vt_sc[h, :, lo:lo + n] = vr_ref[pl.ds(h, n, stride=DA_HEADS), :].T.astype(BF16)

    @pl.when(pl.program_id(1) == 0)
    def _():
        for h in range(DA_HEADS):
            if cached:
                cols = slice(h * DA_V_DIM, (h + 1) * DA_V_DIM)
                kall_sc[h, 0:seq, :] = k_ref[:, cols].astype(BF16)
                vt_sc[h, :, 0:seq] = v_ref[:, cols].astype(F32).T.astype(BF16)
                cache_order_kv(ck_ref, cv_ref, h, seq, ck_ref.shape[1])
            else:
                cache_order_kv(k_ref, v_ref, h, 0, seq)

    lam = _lambda(lam_ref, lam_init)
    lane = lax.broadcasted_iota(jnp.int32, (1, DA_V_DIM), 1)
    comp_masks = [lane < DA_HEAD_DIM, lane >= DA_HEAD_DIM]
    tq = q_ref.shape[0]
    n_keys = kall_sc.shape[1]
    n_tiles = pl.cdiv(n_keys, ATT_TK_MAX)
    tk = n_keys // n_tiles

    def stacked_q(h):
        q = q_ref[:, h * DA_V_DIM:(h + 1) * DA_V_DIM].astype(BF16)
        return jnp.concatenate([jnp.where(m, q, jnp.zeros_like(q)) for m in comp_masks], axis=0)

    def score_tile(h, j, qq, m8):
        rows = slice(j * tk, (j + 1) * tk)
        st = _dot_nt(kall_sc[h, rows, :], qq)
        s_sc[h % 2, rows, :] = st
        t8 = jnp.max(st.reshape(tk // 8, 8, 2 * tq), axis=0)
        return t8 if m8 is None else jnp.maximum(m8, t8)

    def prob_tile(h, j, mx, d8, pv):
        rows = slice(j * tk, (j + 1) * tk)
        e = jnp.exp2(s_sc[h % 2, rows, :] - mx)
        s8 = jnp.sum(e.reshape(tk // 8, 8, 2 * tq), axis=0)
        p = _dot(vt_sc[h, :, rows], e)
        return (s8 if d8 is None else d8 + s8), (p if pv is None else pv + p)

    qq = stacked_q(0)
    m8 = None
    for j in range(n_tiles):
        m8 = score_tile(0, j, qq, m8)
    for h in range(DA_HEADS):
        mx = jnp.max(m8, axis=0, keepdims=True)
        if h + 1 < DA_HEADS:
            qq = stacked_q(h + 1)
        m8, d8, pv = None, None, None
        for j in range(n_tiles):
            if h + 1 < DA_HEADS:
                m8 = score_tile(h + 1, j, qq, m8)
            d8, pv = prob_tile(h, j, mx, d8, pv)
        pv = pv * (1.0 / jnp.sum(d8, axis=0, keepdims=True))
        out_t = pv[:, :tq] - lam * pv[:, tq:]
        out_t = out_t * lax.rsqrt(jnp.mean(out_t * out_t, axis=0, keepdims=True) + EPS)
        out = out_t.T * gs_ref[...]
        o_ref[:, h * DA_V_DIM:(h + 1) * DA_V_DIM] = (out * (1.0 - lam_init)).astype(o_ref.dtype)


def _attn(q, k, v, cache, lam_vecs, g_sub, batch, seq_len, lam_init):
    rows = q.shape[0]
    nq = seq_len // ATT_TQ
    q_spec = pl.BlockSpec((ATT_TQ, D_MODEL), lambda b, i: (b * nq + i, 0))
    cache_specs = lambda n: [pl.BlockSpec((None, D_MODEL, n), lambda b, i: (b, 0, 0)),
                             pl.BlockSpec((None, n * DA_HEADS, DA_V_DIM), lambda b, i: (b, 0, 0))]
    n_keys = seq_len
    if cache is not None:
        past = cache[0].shape[2]
        n_keys += past
        kv_spec = pl.BlockSpec((seq_len, D_MODEL), lambda b, i: (b, 0))
        in_specs = [q_spec, kv_spec, kv_spec] + cache_specs(past)
        args = [q, k, v] + list(cache)
    else:
        in_specs = [q_spec] + cache_specs(seq_len)
        args = [q, k, v.reshape(batch, seq_len * DA_HEADS, DA_V_DIM)]
    in_specs += [_resident((4, DA_HEAD_DIM)), _resident((1, DA_V_DIM))]
    args += [lam_vecs, g_sub.reshape(1, DA_V_DIM)]
    return pl.pallas_call(
        functools.partial(_attn_kernel, cached=cache is not None, seq=seq_len, lam_init=lam_init),
        grid=(batch, nq),
        in_specs=in_specs,
        out_specs=q_spec,
        out_shape=jax.ShapeDtypeStruct((rows, D_MODEL), BF16),
        scratch_shapes=[pltpu.VMEM((DA_HEADS, n_keys, DA_V_DIM), BF16),
                        pltpu.VMEM((DA_HEADS, DA_V_DIM, n_keys), BF16),
                        pltpu.VMEM((2, n_keys, 2 * ATT_TQ), F32)],
        compiler_params=_params(2),
        name="diff_attention",
    )(*args)


def _per_chain(fn, a, b):
    return jnp.stack([fn(a[i], b[i]) for i in range(a.shape[0])])


ML_EXT = 8


def _rows(x):
    return jnp.stack([x[i:i + 1, :] for i in range(x.shape[0])])


def _split3(x):
    hi = x.astype(BF16)
    r = x - hi.astype(F32)
    mid = r.astype(BF16)
    lo = (r - mid.astype(F32)).astype(BF16)
    return jnp.concatenate([hi, mid, lo], axis=1)


def _chunk_scan(x, reverse_rows, tri_prefix, tri_suffix):
    parts = _split3(x)
    return jnp.where(reverse_rows, _dot(parts, tri_suffix), _dot(parts, tri_prefix))


def _paired_value_matmul(v_t, sc):
    B, L, _ = sc.shape
    zero = jnp.zeros((L, L), sc.dtype)
    out = []
    for i in range(0, B, 2):
        lhs = jnp.concatenate([v_t[i], v_t[i + 1]], axis=1)
        rhs = jnp.concatenate([jnp.concatenate([sc[i], zero], axis=1),
                               jnp.concatenate([zero, sc[i + 1]], axis=1)], axis=0)
        both = _dot(lhs, rhs)
        out += [both[:, :L], both[:, L:]]
    return jnp.stack(out)


def _mlstm_step(k, q_t, v_t, ic, fc, CT, m, seen_t, reverse_rows):
    B, L = ic.shape
    d = k.shape[2]
    tri = lambda keep: jnp.concatenate([jnp.where(keep, 1.0, 0.0).astype(BF16)] * 3, axis=0)
    b2 = _chunk_scan(fc, reverse_rows, tri(seen_t[0]), tri(seen_t[B - 1]))
    u2 = ic - b2
    u_t = jnp.concatenate([u2, jnp.zeros((L - B, L), F32)], axis=0).T
    u_col = jnp.stack([jnp.broadcast_to(u_t[:, i:i + 1], (L, L)) for i in range(B)])
    b, i_g, f_g = _rows(b2), _rows(ic), _rows(fc)
    b_last = jnp.sum(f_g, axis=2, keepdims=True)

    log_d = jnp.where(seen_t, b + u_col, -jnp.inf)
    a = b + m
    m_t = jnp.maximum(a, jnp.max(log_d, axis=1, keepdims=True))
    dmat = jnp.exp(log_d - m_t)
    inter = jnp.exp(a - m_t)
    sc = _per_chain(_dot, k, q_t) * dmat
    cq = _per_chain(_dot, CT, q_t)
    num = _paired_value_matmul(v_t, sc.astype(BF16)) + inter * cq[:, :d, :]
    den = jnp.sum(sc, axis=1, keepdims=True) + inter * cq[:, d:d + 1, :]
    h_t = num * (1.0 / jnp.maximum(jnp.abs(den), jnp.exp(-m_t)))

    g = b_last - b + i_g
    m_new = jnp.maximum(b_last + m, jnp.max(g, axis=2, keepdims=True))
    w = jnp.exp(g - m_new)
    decay = jnp.exp(b_last + m - m_new)
    vw = jnp.concatenate([v_t.astype(F32), jnp.ones((B, ML_EXT, L), F32)], axis=1) * w
    CT_new = decay * CT + _per_chain(_dot, vw, k)
    return h_t, CT_new, m_new


def _mlstm_kernel(*refs, seq_len, has_state, emit_state):
    q_ref, k_ref, v_ref, so_ref, g_ref, gmh_ref = refs[:6]
    refs = refs[6:]
    if has_state:
        (c0_ref, n0_ref, m0_ref), refs = refs[:3], refs[3:]
    hm_ref, refs = refs[0], refs[1:]
    if emit_state:
        (c_out_ref, n_out_ref, m_out_ref), refs = refs[:3], refs[3:]
    ct_sc, m_sc, gr_sc, h_sc = refs

    nc = seq_len // CHUNK
    d_head = ML_HEAD_DIM
    chains = [(d, h) for d in range(2) for h in range(ML_HEADS)]
    n_chain = len(chains)
    for i, (d, h) in enumerate(chains):
        if has_state:
            ct_sc[i, 0:d_head, :] = c0_ref[d, h].T
            ct_sc[i, d_head:d_head + ML_EXT, :] = jnp.broadcast_to(n0_ref[d, h], (ML_EXT, d_head))
            m_sc[i] = m0_ref[d, h][:, 0:1]
        else:
            ct_sc[i] = jnp.zeros((d_head + ML_EXT, d_head), F32)
            m_sc[i] = jnp.zeros((1, 1), F32)
    for c in range(nc):
        gr_sc[c] = g_ref[c * CHUNK:(c + 1) * CHUNK, :].T

    s_idx = lax.broadcasted_iota(jnp.int32, (CHUNK, CHUNK), 0)
    t_idx = lax.broadcasted_iota(jnp.int32, (CHUNK, CHUNK), 1)
    seen_t = jnp.stack([s_idx <= t_idx] * ML_HEADS + [s_idx >= t_idx] * ML_HEADS)
    reverse_rows = lax.broadcasted_iota(jnp.int32, (n_chain, CHUNK), 0) >= ML_HEADS

    def step(c_fwd, c_bwd, rows_of):
        chunk_of = (c_fwd, c_bwd)
        g_fwd, g_bwd = gr_sc[c_fwd], gr_sc[c_bwd]
        lo = 2 * ML_HEADS
        ic = jnp.concatenate([g_fwd[0:ML_HEADS], g_bwd[lo:lo + ML_HEADS]], axis=0)
        fc = jnp.concatenate([g_fwd[ML_HEADS:lo], g_bwd[lo + ML_HEADS:2 * lo]], axis=0)
        head = [slice(h * d_head, (h + 1) * d_head) for _, h in chains]
        stack = lambda pick: jnp.stack([pick(i, chunk_of[d]) for i, (d, _) in enumerate(chains)])
        h_t, CT_new, m_new = _mlstm_step(
            stack(lambda i, c: k_ref[rows_of(c), head[i]]),
            stack(lambda i, c: q_ref[c, head[i], :]),
            stack(lambda i, c: v_ref[c, head[i], :]),
            ic, fc, ct_sc[...], m_sc[...], seen_t, reverse_rows)
        ct_sc[...] = CT_new
        m_sc[...] = m_new
        for i, (d, _) in enumerate(chains):
            h_sc[d, chunk_of[d], head[i], :] = h_t[i]

    if nc <= 2:
        for c in range(nc):
            step(c, nc - 1 - c, lambda cc: slice(cc * CHUNK, (cc + 1) * CHUNK))
    else:
        def body(c, carry):
            step(c, nc - 1 - c, lambda cc: pl.ds(pl.multiple_of(cc * CHUNK, CHUNK), CHUNK))
            return carry
        lax.fori_loop(0, nc, body, 0)

    for h in range(ML_HEADS):
        hcols = slice(h * d_head, (h + 1) * d_head)
        gain = jnp.broadcast_to(gmh_ref[h], (d_head, CHUNK))
        for c in range(nc):
            rows = slice(c * CHUNK, (c + 1) * CHUNK)
            hsum = h_sc[0, c, hcols, :] + h_sc[1, c, hcols, :]
            hn = hsum * lax.rsqrt(jnp.mean(hsum * hsum, axis=0, keepdims=True) + EPS) * gain
            hm_ref[rows, hcols] = (hn.T * so_ref[rows, hcols]).astype(hm_ref.dtype)
    if emit_state:
        for i, (d, h) in enumerate(chains):
            c_out_ref[d, h] = ct_sc[i, 0:d_head, :]
            n_out_ref[d, h] = ct_sc[i, d_head:d_head + 1, :]
            m_out_ref[d, h] = jnp.broadcast_to(m_sc[i], (1, LANES))


def _mlstm(q_t, k, v_t, so, gates, g_mh, state, batch, seq_len, emit_state):
    rows = k.shape[0]
    d = ML_HEAD_DIM
    nc = seq_len // CHUNK
    tile = pl.BlockSpec((seq_len, D_MODEL), lambda b: (b, 0))
    tile_t = pl.BlockSpec((nc, D_MODEL, CHUNK), lambda b: (b, 0, 0))
    in_specs = [tile_t, tile, tile_t, tile,
                pl.BlockSpec((seq_len, GATE_PAD), lambda b: (b, 0)),
                _resident((ML_HEADS, d, 1))]
    args = [q_t, k, v_t, so, gates, g_mh.reshape(ML_HEADS, d, 1)]
    c_spec = pl.BlockSpec((None, 2, ML_HEADS, d, d), lambda b: (b, 0, 0, 0, 0))
    n_spec = pl.BlockSpec((None, 2, ML_HEADS, 1, d), lambda b: (b, 0, 0, 0, 0))
    m_spec = pl.BlockSpec((None, 2, ML_HEADS, 1, LANES), lambda b: (b, 0, 0, 0, 0))
    if state is not None:
        in_specs += [c_spec, n_spec, m_spec]
        args += list(state)
    out_specs = [tile]
    out_shape = [jax.ShapeDtypeStruct((rows, D_MODEL), BF16)]
    if emit_state:
        out_specs += [c_spec, n_spec, m_spec]
        out_shape += [jax.ShapeDtypeStruct((batch, 2, ML_HEADS, d, d), F32),
                      jax.ShapeDtypeStruct((batch, 2, ML_HEADS, 1, d), F32),
                      jax.ShapeDtypeStruct((batch, 2, ML_HEADS, 1, LANES), F32)]
    n_state = 2 * ML_HEADS
    return pl.pallas_call(
        functools.partial(_mlstm_kernel, seq_len=seq_len, has_state=state is not None, emit_state=emit_state),
        grid=(batch,),
        in_specs=in_specs,
        out_specs=out_specs,
        out_shape=out_shape,
        scratch_shapes=[pltpu.VMEM((n_state, d + ML_EXT, d), F32), pltpu.VMEM((n_state, 1, 1), F32),
                        pltpu.VMEM((nc, GATE_PAD, CHUNK), F32),
                        pltpu.VMEM((2, nc, D_MODEL, CHUNK), F32)],
        compiler_params=_params(1),
        name="mlstm",
    )(*args)


def _merge_kernel(x_ref, m_ref, hm_ref, att_ref, sgm_ref, sgd_ref, wm_ref, wd_ref, wo_ref, o_ref, w_sc):
    @pl.when(pl.program_id(0) == 0)
    def _():
        for j, w_ref in enumerate((wm_ref, wd_ref, wo_ref)):
            w_sc[j] = w_ref[...].astype(BF16)

    y = sgm_ref[...] * _dot(hm_ref[...], w_sc[0]) + sgd_ref[...] * _dot(att_ref[...], w_sc[1])
    o_ref[...] = x_ref[...] + m_ref[5:6, :] * _dot(y, w_sc[2])


def _merge(x, mods, mod_index, hm, att, sgm, sgd, wm, wd, wo):
    rows = x.shape[0]
    tile = pl.BlockSpec((MERGE_TM, D_MODEL), lambda i: (i, 0))
    w_spec = _resident((D_MODEL, D_MODEL))
    return pl.pallas_call(
        _merge_kernel,
        grid=(rows // MERGE_TM,),
        in_specs=[tile, pl.BlockSpec((None, N_MOD, D_MODEL), lambda i: (mod_index(i * MERGE_TM), 0, 0)),
                  tile, tile, tile, tile, w_spec, w_spec, w_spec],
        out_specs=tile,
        out_shape=jax.ShapeDtypeStruct((rows, D_MODEL), F32),
        scratch_shapes=[pltpu.VMEM((3, D_MODEL, D_MODEL), BF16)],
        compiler_params=_params(1),
        name="branch_merge",
    )(x, mods, hm, att, sgm, sgd, wm, wd, wo)


def _rope_tables(seq_len):
    lane = np.arange(LANES)
    r = lane % 32
    freqs = np.power(np.float32(ROPE_BASE), -(r % 16).astype(np.float32) / np.float32(16.0))
    tok = np.arange(seq_len)
    pos = np.where((lane % 64 < 32)[None, :], (tok // GRID_W)[:, None], (tok % GRID_W)[:, None]).astype(np.float32)
    ang = pos * freqs[None, :]
    sign = np.where(r < 16, -1.0, 1.0).astype(np.float32)
    return jnp.asarray(np.cos(ang), F32), jnp.asarray(np.sin(ang) * sign[None, :], F32)


def _layer(x, mods, mod_index, w, batch, seq_len, ctx, lam_init):
    x = _ffn(x, mods, mod_index, w["g_norm"][0], w["ffn1_w1"], w["ffn1_w3"], w["ffn1_w2"], base=0)
    rope_tabs = None if ctx is None else _rope_tables(seq_len)
    mq, mk, mv, so, dq, dk, dv, sgm, sgd, gates = _proj(
        x, mods, mod_index, w["g_norm"][1], w["w_in_t"], w["b_gate"], w["bd"], w["g_qn"],
        w["g_kn_col"] if ctx is None else w["g_kn"], rope_tabs, seq_len)
    cache = None if ctx is None else (ctx[0], ctx[1])
    att = _attn(dq, dk, dv, cache, w["lam"], w["g_sub"], batch, seq_len, lam_init)
    state = None if ctx is None else ctx[2]
    res = _mlstm(mq, mk, mv, so, gates, w["g_mh"], state, batch, seq_len, emit_state=ctx is None)
    x = _merge(x, mods, mod_index, res[0], att, sgm, sgd, w["w_br_m"], w["w_br_d"], w["w_out"])
    x = _ffn(x, mods, mod_index, w["g_norm"][2], w["ffn2_w1"], w["ffn2_w3"], w["ffn2_w2"], base=6)
    return x, dk, dv, res[1:]


def kernel(x_prompt, x_sample, c, cache_k, cache_v, state_C, state_n, state_m, c_ctx, w_ada, b_ada, g_norm, ffn1_w1, ffn1_w3, ffn1_w2, ffn2_w1, ffn2_w3, ffn2_w2, w_in, b_gate, g_qn, g_kn, lam_q1, lam_k1, lam_q2, lam_k2, g_sub, g_mh, w_br_m, w_br_d, w_out):
    depth = w_ada.shape[0]
    assert depth == 1
    l = 0
    bp, tp, _ = x_prompt.shape
    bs, ts, _ = x_sample.shape
    past = cache_k.shape[2]
    lam_init = 0.8 - 0.6 * math.exp(-0.3 * l)

    cvecs = jnp.concatenate([c_ctx[None, :], c, jnp.zeros((8 - 1 - bs, D_MODEL), F32)], axis=0)
    mods = _mods(cvecs, w_ada[l], b_ada[l]).reshape(8, N_MOD, D_MODEL)

    group = jnp.arange(MXU_DIM) // DA_HEAD_DIM
    w = dict(
        g_norm=g_norm[l],
        ffn1_w1=ffn1_w1[l], ffn1_w3=ffn1_w3[l], ffn1_w2=ffn1_w2[l],
        ffn2_w1=ffn2_w1[l], ffn2_w3=ffn2_w3[l], ffn2_w2=ffn2_w2[l],
        w_in_t=w_in[l].T,
        b_gate=jnp.pad(b_gate[l], (0, GATE_PAD - N_GATE_COLS)).reshape(1, GATE_PAD),
        bd=(group[:, None] == group[None, :]).astype(BF16),
        g_qn=jnp.tile(g_qn[l] * QK_LOG2_SCALE, MXU_DIM // DA_HEAD_DIM).reshape(1, MXU_DIM),
        g_kn=jnp.tile(g_kn[l], MXU_DIM // DA_HEAD_DIM).reshape(1, MXU_DIM),
        g_kn_col=g_kn[l].reshape(DA_HEAD_DIM, 1),
        lam=jnp.stack([lam_q1[l], lam_k1[l], lam_q2[l], lam_k2[l]]),
        g_sub=g_sub[l], g_mh=g_mh[l],
        w_br_m=w_br_m[l], w_br_d=w_br_d[l], w_out=w_out[l],
    )

    xp, new_k_t, new_v, (new_c, new_n, new_m) = _layer(
        x_prompt.reshape(bp * tp, D_MODEL), mods, lambda r: 0, w, bp, tp, None, lam_init)

    ctx = (cache_k[:, l].transpose(0, 2, 3, 4, 1).reshape(bs, D_MODEL, past),
           cache_v[:, l].reshape(bs, past * DA_HEADS, DA_V_DIM),
           (state_C[:, l], state_n[:, l].reshape(bs, 2, ML_HEADS, 1, ML_HEAD_DIM),
            jnp.broadcast_to(state_m[:, l][..., None, None], (bs, 2, ML_HEADS, 1, LANES))))
    xs, _, _, _ = _layer(x_sample.reshape(bs * ts, D_MODEL), mods, lambda r: 1 + r // ts, w, bs, ts, ctx, lam_init)

    return (xp.reshape(bp, tp, D_MODEL), xs.reshape(bs, ts, D_MODEL),
            new_k_t.reshape(bp, DA_HEADS, 2, DA_HEAD_DIM, tp).transpose(0, 4, 1, 2, 3)[:, None],
            new_v.reshape(bp, 1, tp, DA_HEADS, DA_V_DIM),
            jnp.swapaxes(new_c, -1, -2)[:, None], new_n.reshape(bp, 1, 2, ML_HEADS, ML_HEAD_DIM),
            new_m[..., 0, 0][:, None])
```

```python
import functools
import math

import jax
import jax.numpy as jnp
import numpy as np
from jax import lax
from jax.experimental import pallas as pl
from jax.experimental.pallas import tpu as pltpu

F32 = jnp.float32
BF16 = jnp.bfloat16

D_MODEL = 1024
D_FF = 2816
N_MOD = 9
GRID_W = 64
ML_HEADS = 4
ML_HEAD_DIM = 256
DA_HEADS = 8
DA_HEAD_DIM = 64
DA_V_DIM = 128
N_GATE_COLS = 16
CHUNK = 128
ROPE_BASE = 10000.0
QK_LOG2_SCALE = DA_HEAD_DIM ** -0.5 * math.log2(math.e)
EPS = 1e-6

LANES = 128
GATE_PAD = LANES
MXU_DIM = 256
VMEM_LIMIT = 56 * 1024 * 1024

FFN_TM = 512
FFN_TF = 256
PROJ_TM = 256
MERGE_TM = 512
ATT_TQ = 256
ATT_TK_MAX = 768


def _params(n_axes):
    return pltpu.CompilerParams(dimension_semantics=("arbitrary",) * n_axes,
                                vmem_limit_bytes=VMEM_LIMIT)


def _dot(a, b):
    return jnp.dot(a.astype(BF16), b.astype(BF16), preferred_element_type=F32)


def _dot_nt(a, b):
    return lax.dot_general(a.astype(BF16), b.astype(BF16), (((1,), (1,)), ((), ())),
                           preferred_element_type=F32)


def _dot_tn(a, b):
    return lax.dot_general(a.astype(BF16), b.astype(BF16), (((0,), (0,)), ((), ())),
                           preferred_element_type=F32)


def _sigmoid(x):
    return 1.0 / (1.0 + jnp.exp(-x))


def _log_sigmoid(x):
    return jnp.minimum(x, 0.0) - jnp.log1p(jnp.exp(-jnp.abs(x)))


def _modulated_norm(x, g, shift, scale):
    y = x * lax.rsqrt(jnp.mean(x * x, axis=-1, keepdims=True) + EPS) * g
    return y * (1.0 + scale) + shift


def _resident(shape):
    return pl.BlockSpec(shape, lambda *_: (0,) * len(shape), pipeline_mode=pl.Buffered(1))


def _mods_kernel(c_ref, w_ref, b_ref, o_ref):
    c = c_ref[...]
    o_ref[...] = _dot(c * _sigmoid(c), w_ref[...]) + b_ref[...]


def _mods(cvecs, w_ada, b_ada):
    n = N_MOD * D_MODEL
    tn = D_MODEL
    return pl.pallas_call(
        _mods_kernel,
        grid=(n // tn,),
        in_specs=[pl.BlockSpec((8, D_MODEL), lambda j: (0, 0)),
                  pl.BlockSpec((D_MODEL, tn), lambda j: (0, j)),
                  pl.BlockSpec((1, tn), lambda j: (0, j))],
        out_specs=pl.BlockSpec((8, tn), lambda j: (0, j)),
        out_shape=jax.ShapeDtypeStruct((8, n), F32),
        compiler_params=_params(1),
        name="adaln_mods",
    )(cvecs, w_ada, b_ada.reshape(1, n))


def _ffn_kernel(*refs, base, in_tiles, out_tiles):
    n_in, n_out = len(in_tiles), len(out_tiles)
    x_refs, refs = refs[:n_in], refs[n_in:]
    (m_ref, g_ref, w1_ref, w3_ref, w2_ref), refs = refs[:5], refs[5:]
    o_refs, (w1_sc, w3_sc, w2_sc, hh_sc, acc_sc) = refs[:n_out], refs[n_out:]
    g = pl.program_id(0)
    nf = D_FF // FFN_TF
    row_tile = g - (nf - 1)
    norm = lambda x: _modulated_norm(x, g_ref[...], m_ref[base:base + 1, :], m_ref[base + 1:base + 2, :]).astype(BF16)
    finish = lambda x, acc: x + 0.5 * m_ref[base + 2:base + 3, :] * acc

    def tile(hh, f):
        a = _dot(hh, w1_sc[f])
        b = _dot(hh, w3_sc[f])
        return _dot(a * _sigmoid(a) * b, w2_sc[f])

    @pl.when(g < nf)
    def _():
        w1_sc[g] = w1_ref[...].astype(BF16)
        w3_sc[g] = w3_ref[...].astype(BF16)
        w2_sc[g] = w2_ref[...].astype(BF16)

        @pl.when(g == 0)
        def _():
            hh_sc[...] = norm(x_refs[0][...])
            acc_sc[...] = jnp.zeros(acc_sc.shape, F32)

        acc_sc[...] += tile(hh_sc[...], g)

        @pl.when(g == nf - 1)
        def _():
            o_refs[0][...] = finish(x_refs[0][...], acc_sc[...])

    def full_tile(x_ref, o_ref):
        x = x_ref[...]
        hh = norm(x)
        acc = jnp.zeros(x.shape, F32)
        for f in range(nf):
            acc = acc + tile(hh, f)
        o_ref[...] = finish(x, acc)

    bounds = sorted(set(np.cumsum((0,) + in_tiles).tolist()) | set(np.cumsum((0,) + out_tiles).tolist()))
    for lo, hi in zip(bounds[:-1], bounds[1:]):
        k_in = int(np.searchsorted(np.cumsum(in_tiles), lo, side="right"))
        k_out = int(np.searchsorted(np.cumsum(out_tiles), lo, side="right"))
        pl.when((g >= nf) & (row_tile >= lo) & (row_tile < hi))(
            functools.partial(full_tile, x_refs[k_in], o_refs[k_out]))


def _ffn(xs, out_rows, mods, mod_index, g, w1, w3, w2, base):
    nf = D_FF // FFN_TF
    in_tiles = tuple(x.shape[0] // FFN_TM for x in xs)
    out_tiles = tuple(r // FFN_TM for r in out_rows)
    assert sum(in_tiles) == sum(out_tiles)
    row_tile = lambda s: jnp.maximum(s - (nf - 1), 0)
    f_tile = lambda s: jnp.minimum(s, nf - 1)

    def part_spec(tiles, k):
        start = sum(tiles[:k])
        return pl.BlockSpec((FFN_TM, D_MODEL), lambda s: (jnp.clip(row_tile(s) - start, 0, tiles[k] - 1), 0))

    return pl.pallas_call(
        functools.partial(_ffn_kernel, base=base, in_tiles=in_tiles, out_tiles=out_tiles),
        grid=(nf - 1 + sum(in_tiles),),
        in_specs=[part_spec(in_tiles, k) for k in range(len(xs))] + [
            pl.BlockSpec((None, N_MOD, D_MODEL), lambda s: (mod_index(row_tile(s) * FFN_TM), 0, 0)),
            _resident((1, D_MODEL)),
            pl.BlockSpec((D_MODEL, FFN_TF), lambda s: (0, f_tile(s))),
            pl.BlockSpec((D_MODEL, FFN_TF), lambda s: (0, f_tile(s))),
            pl.BlockSpec((FFN_TF, D_MODEL), lambda s: (f_tile(s), 0))],
        out_specs=[part_spec(out_tiles, k) for k in range(len(out_rows))],
        out_shape=[jax.ShapeDtypeStruct((r, D_MODEL), F32) for r in out_rows],
        scratch_shapes=[pltpu.VMEM((nf, D_MODEL, FFN_TF), BF16), pltpu.VMEM((nf, D_MODEL, FFN_TF), BF16),
                        pltpu.VMEM((nf, FFN_TF, D_MODEL), BF16),
                        pltpu.VMEM((FFN_TM, D_MODEL), BF16), pltpu.VMEM((FFN_TM, D_MODEL), F32)],
        compiler_params=_params(1),
        name="ffn",
    )(*xs, mods, g.reshape(1, D_MODEL), w1, w3, w2)


def _group_norm64(x, bd, g):
    ss = _dot(x * x, bd)
    return x * lax.rsqrt(ss * (1.0 / DA_HEAD_DIM) + EPS) * g


def _rope(x, cos, sin_signed):
    first = (lax.broadcasted_iota(jnp.int32, x.shape, 1) % 32) < 16
    partner = jnp.where(first, pltpu.roll(x, LANES - 16, 1), pltpu.roll(x, 16, 1))
    return x * cos + partner * sin_signed


N_SEG = 9
GATE_LO = 4 * D_MODEL


def _seg_start(s):
    return s * D_MODEL + N_GATE_COLS * (s >= 4)


def _proj_kernel(*refs, rope):
    (x_ref, m_ref, g_ref, wt_ref, wgt_ref, bg_ref, bd_ref, gq_ref, gk_ref), refs = refs[:9], refs[9:]
    if rope:
        (cos_ref, sin_ref), refs = refs[:2], refs[2:]
    (mq_ref, mk_ref, mv_ref, so_ref, dq_ref, dk_ref, dv_ref, sgm_ref, sgd_ref, gates_ref,
     w_sc, wg_sc, hh_sc) = refs
    step = pl.program_id(0)
    norm = lambda: _modulated_norm(x_ref[...], g_ref[...], m_ref[3:4, :], m_ref[4:5, :]).astype(BF16)

    def gates(hh):
        y = _dot_nt(hh, wg_sc[...]) + bg_ref[...]
        col = lax.broadcasted_iota(jnp.int32, y.shape, 1)
        gates_ref[...] = jnp.where((col // ML_HEADS) % 2 == 1, _log_sigmoid(y), y)

    def qk_segment(y, g_head_ref, o_ref):
        bd = bd_ref[...]
        for c in range(D_MODEL // MXU_DIM):
            cols = slice(c * MXU_DIM, (c + 1) * MXU_DIM)
            z = _group_norm64(y[:, cols], bd, g_head_ref[...])
            if rope:
                z = jnp.concatenate(
                    [_rope(z[:, k * LANES:(k + 1) * LANES], cos_ref[...], sin_ref[...])
                     for k in range(MXU_DIM // LANES)], axis=1)
            o_ref[:, cols] = z.astype(o_ref.dtype)

    def cache_key_segment(hh):
        y_t = _dot_nt(w_sc[5], hh)
        z = y_t.reshape(D_MODEL // DA_HEAD_DIM, DA_HEAD_DIM, y_t.shape[1])
        ms = jnp.mean(z * z, axis=1, keepdims=True)
        gain = jnp.broadcast_to(gk_ref[...], z.shape[1:])
        dk_ref[...] = (z * lax.rsqrt(ms + EPS) * gain).reshape(y_t.shape)

    def segment(s, hh):
        if s in (0, 2):
            o_ref = {0: mq_ref, 2: mv_ref}[s]
            y_t = _dot_nt(w_sc[s], hh) * ((ML_HEAD_DIM ** -0.5) if s == 0 else 1.0)
            for j in range(PROJ_TM // CHUNK):
                o_ref[j] = y_t[:, j * CHUNK:(j + 1) * CHUNK].astype(o_ref.dtype)
            return
        if s == 5 and not rope:
            cache_key_segment(hh)
            return
        y = _dot_nt(hh, w_sc[s])
        if s == 4:
            qk_segment(y, gq_ref, dq_ref)
        elif s == 5:
            qk_segment(y, gk_ref, dk_ref)
        elif s in (3, 7, 8):
            {3: so_ref, 7: sgm_ref, 8: sgd_ref}[s][...] = _sigmoid(y)
        elif s == 6 and not rope:
            dv_ref[...] = y.reshape(y.shape[0], DA_HEADS, DA_V_DIM)
        else:
            o_ref = {1: mk_ref, 6: dv_ref}[s]
            o_ref[...] = y.astype(o_ref.dtype)

    for s in range(N_SEG):
        @pl.when(step == s)
        def _(s=s):
            w_sc[s] = wt_ref[...].astype(BF16)
            if s == 0:
                wg_sc[...] = jnp.zeros(wg_sc.shape, BF16)
                wg_sc[0:N_GATE_COLS, :] = wgt_ref[...].astype(BF16)
                hh_sc[...] = norm()
                gates(hh_sc[...])
            segment(s, hh_sc[...])

    @pl.when(step >= N_SEG)
    def _():
        hh = norm()
        gates(hh)
        for s in range(N_SEG):
            segment(s, hh)


def _proj(x, row0, rows, mods, mod_index, g, w_in_t, b_gate, bd, gq, gk, rope_tabs, seq_len):
    rope = rope_tabs is not None
    row_tile = lambda s: jnp.maximum(s - (N_SEG - 1), 0)
    row = lambda s: (row_tile(s), 0)
    tile = pl.BlockSpec((PROJ_TM, D_MODEL), row)
    elem = lambda n: (pl.Element(n), pl.Element(D_MODEL))
    seg_row = lambda s: pl.multiple_of(_seg_start(jnp.minimum(s, N_SEG - 1)), N_GATE_COLS)
    in_specs = [pl.BlockSpec((PROJ_TM, D_MODEL), lambda s: (row0 // PROJ_TM + row_tile(s), 0)),
                pl.BlockSpec((None, N_MOD, D_MODEL), lambda s: (mod_index(row0 + row_tile(s) * PROJ_TM), 0, 0)),
                _resident((1, D_MODEL)),
                pl.BlockSpec(elem(D_MODEL), lambda s: (seg_row(s), 0)),
                pl.BlockSpec(elem(N_GATE_COLS), lambda s: (GATE_LO, 0)),
                _resident((1, GATE_PAD)),
                _resident((MXU_DIM, MXU_DIM)),
                _resident((1, MXU_DIM)),
                _resident(gk.shape)]
    args = [x, mods, g.reshape(1, D_MODEL), w_in_t, w_in_t, b_gate, bd, gq, gk]
    if rope:
        tiles_per_seq = seq_len // PROJ_TM
        tab = pl.BlockSpec((PROJ_TM, LANES), lambda s: (row_tile(s) % tiles_per_seq, 0))
        in_specs += [tab, tab]
        args += list(rope_tabs)
    dtypes = [BF16, BF16, BF16, F32, BF16, BF16, BF16 if rope else F32, F32, F32]
    out_shape = [jax.ShapeDtypeStruct((rows, D_MODEL), dt) for dt in dtypes]
    out_shape.append(jax.ShapeDtypeStruct((rows, GATE_PAD), F32))
    out_specs = [tile] * 9 + [pl.BlockSpec((PROJ_TM, GATE_PAD), row)]
    slabs = PROJ_TM // CHUNK
    for j in (0, 2):
        out_shape[j] = jax.ShapeDtypeStruct((rows // CHUNK, D_MODEL, CHUNK), BF16)
        out_specs[j] = pl.BlockSpec((slabs, D_MODEL, CHUNK), lambda s: (row_tile(s), 0, 0))
    if not rope:
        assert seq_len == PROJ_TM
        out_shape[5] = jax.ShapeDtypeStruct((rows // seq_len, D_MODEL, seq_len), F32)
        out_specs[5] = pl.BlockSpec((None, D_MODEL, seq_len), lambda s: (row_tile(s), 0, 0))
        out_shape[6] = jax.ShapeDtypeStruct((rows, DA_HEADS, DA_V_DIM), F32)
        out_specs[6] = pl.BlockSpec((PROJ_TM, DA_HEADS, DA_V_DIM), lambda s: (row_tile(s), 0, 0))
    return pl.pallas_call(
        functools.partial(_proj_kernel, rope=rope),
        grid=(N_SEG - 1 + rows // PROJ_TM,),
        in_specs=in_specs,
        out_specs=out_specs,
        out_shape=out_shape,
        scratch_shapes=[pltpu.VMEM((N_SEG, D_MODEL, D_MODEL), BF16), pltpu.VMEM((GATE_PAD, D_MODEL), BF16),
                        pltpu.VMEM((PROJ_TM, D_MODEL), BF16)],
        compiler_params=_params(1),
        name="mixer_in_proj",
    )(*args)


def _lambda(lam_ref, lam_init):
    lam = lam_ref[...]
    s1 = jnp.sum(lam[0:1, :] * lam[1:2, :], axis=1, keepdims=True)
    s2 = jnp.sum(lam[2:3, :] * lam[3:4, :], axis=1, keepdims=True)
    return jnp.exp(s1) - jnp.exp(s2) + lam_init


def _attn_kernel(*refs, cached, seq, lam_init):
    if cached:
        q_ref, k_ref, v_ref, ck_ref, cv_ref, lam_ref, gs_ref, o_ref, kall_sc, vt_sc, s_sc = refs
    else:
        q_ref, k_ref, v_ref, lam_ref, gs_ref, o_ref, kall_sc, vt_sc, s_sc = refs
    def cache_order_kv(kt_ref, vr_ref, h, lo, n):
        kall_sc[h, lo:lo + n, :] = kt_ref[h * DA_V_DIM:(h + 1) * DA_V_DIM, :].T.astype(BF16)
        vt_sc[h, :, lo:lo + n] = vr_ref[pl.ds(h, n, stride=DA_HEADS), :].T.astype(BF16)

    @pl.when(pl.program_id(1) == 0)
    def _():
        for h in range(DA_HEADS):
            if cached:
                cols = slice(h * DA_V_DIM, (h + 1) * DA_V_DIM)
                kall_sc[h, 0:seq, :] = k_ref[:, cols].astype(BF16)
                vt_sc[h, :, 0:seq] = v_ref[:, cols].astype(F32).T.astype(BF16)
                cache_order_kv(ck_ref, cv_ref, h, seq, ck_ref.shape[1])
            else:
                cache_order_kv(k_ref, v_ref, h, 0, seq)

    lam = _lambda(lam_ref, lam_init)
    lane = lax.broadcasted_iota(jnp.int32, (1, DA_V_DIM), 1)
    comp_masks = [lane < DA_HEAD_DIM, lane >= DA_HEAD_DIM]
    tq = q_ref.shape[0]
    n_keys = kall_sc.shape[1]
    n_tiles = pl.cdiv(n_keys, ATT_TK_MAX)
    tk = n_keys // n_tiles

    def stacked_q(h):
        q = q_ref[:, h * DA_V_DIM:(h + 1) * DA_V_DIM].astype(BF16)
        return jnp.concatenate([jnp.where(m, q, jnp.zeros_like(q)) for m in comp_masks], axis=0)

    def score_tile(h, j, qq, m8):
        rows = slice(j * tk, (j + 1) * tk)
        st = _dot_nt(kall_sc[h, rows, :], qq)
        s_sc[h % 2, rows, :] = st
        t8 = jnp.max(st.reshape(tk // 8, 8, 2 * tq), axis=0)
        return t8 if m8 is None else jnp.maximum(m8, t8)

    def prob_tile(h, j, mx, d8, pv):
        rows = slice(j * tk, (j + 1) * tk)
        e = jnp.exp2(s_sc[h % 2, rows, :] - mx)
        s8 = jnp.sum(e.reshape(tk // 8, 8, 2 * tq), axis=0)
        p = _dot(vt_sc[h, :, rows], e)
        return (s8 if d8 is None else d8 + s8), (p if pv is None else pv + p)

    qq = stacked_q(0)
    m8 = None
    for j in range(n_tiles):
        m8 = score_tile(0, j, qq, m8)
    for h in range(DA_HEADS):
        mx = jnp.max(m8, axis=0, keepdims=True)
        if h + 1 < DA_HEADS:
            qq = stacked_q(h + 1)
        m8, d8, pv = None, None, None
        for j in range(n_tiles):
            if h + 1 < DA_HEADS:
                m8 = score_tile(h + 1, j, qq, m8)
            d8, pv = prob_tile(h, j, mx, d8, pv)
        pv = pv * (1.0 / jnp.sum(d8, axis=0, keepdims=True))
        out_t = pv[:, :tq] - lam * pv[:, tq:]
        out_t = out_t * lax.rsqrt(jnp.mean(out_t * out_t, axis=0, keepdims=True) + EPS)
        out = out_t.T * gs_ref[...]
        o_ref[:, h * DA_V_DIM:(h + 1) * DA_V_DIM] = (out * (1.0 - lam_init)).astype(o_ref.dtype)


def _attn(q, k, v, cache, lam_vecs, g_sub, batch, seq_len, lam_init):
    rows = q.shape[0]
    nq = seq_len // ATT_TQ
    q_spec = pl.BlockSpec((ATT_TQ, D_MODEL), lambda b, i: (b * nq + i, 0))
    cache_specs = lambda n: [pl.BlockSpec((None, D_MODEL, n), lambda b, i: (b, 0, 0)),
                             pl.BlockSpec((None, n * DA_HEADS, DA_V_DIM), lambda b, i: (b, 0, 0))]
    n_keys = seq_len
    if cache is not None:
        past = cache[0].shape[2]
        n_keys += past
        kv_spec = pl.BlockSpec((seq_len, D_MODEL), lambda b, i: (b, 0))
        in_specs = [q_spec, kv_spec, kv_spec] + cache_specs(past)
        args = [q, k, v] + list(cache)
    else:
        in_specs = [q_spec] + cache_specs(seq_len)
        args = [q, k, v.reshape(batch, seq_len * DA_HEADS, DA_V_DIM)]
    in_specs += [_resident((4, DA_HEAD_DIM)), _resident((1, DA_V_DIM))]
    args += [lam_vecs, g_sub.reshape(1, DA_V_DIM)]
    return pl.pallas_call(
        functools.partial(_attn_kernel, cached=cache is not None, seq=seq_len, lam_init=lam_init),
        grid=(batch, nq),
        in_specs=in_specs,
        out_specs=q_spec,
        out_shape=jax.ShapeDtypeStruct((rows, D_MODEL), BF16),
        scratch_shapes=[pltpu.VMEM((DA_HEADS, n_keys, DA_V_DIM), BF16),
                        pltpu.VMEM((DA_HEADS, DA_V_DIM, n_keys), BF16),
                        pltpu.VMEM((2, n_keys, 2 * ATT_TQ), F32)],
        compiler_params=_params(2),
        name="diff_attention",
    )(*args)


def _per_chain(fn, a, b):
    return jnp.stack([fn(a[i], b[i]) for i in range(a.shape[0])])


ML_EXT = 8


def _rows(x):
    return jnp.stack([x[i:i + 1, :] for i in range(x.shape[0])])


def _split3(x):
    hi = x.astype(BF16)
    r = x - hi.astype(F32)
    mid = r.astype(BF16)
    lo = (r - mid.astype(F32)).astype(BF16)
    return jnp.concatenate([hi, mid, lo], axis=1)


def _chunk_scan(x, reverse_rows, tri_prefix, tri_suffix):
    parts = _split3(x)
    return jnp.where(reverse_rows, _dot(parts, tri_suffix), _dot(parts, tri_prefix))


def _paired_value_matmul(v_t, sc):
    B, L, _ = sc.shape
    zero = jnp.zeros((L, L), sc.dtype)
    out = []
    for i in range(0, B, 2):
        lhs = jnp.concatenate([v_t[i], v_t[i + 1]], axis=1)
        rhs = jnp.concatenate([jnp.concatenate([sc[i], zero], axis=1),
                               jnp.concatenate([zero, sc[i + 1]], axis=1)], axis=0)
        both = _dot(lhs, rhs)
        out += [both[:, :L], both[:, L:]]
    return jnp.stack(out)


def _mlstm_step(k, q_t, v_t, ic, fc, CT, m, seen_t, reverse_rows):
    B, L = ic.shape
    d = k.shape[2]
    tri = lambda keep: jnp.concatenate([jnp.where(keep, 1.0, 0.0).astype(BF16)] * 3, axis=0)
    b2 = _chunk_scan(fc, reverse_rows, tri(seen_t[0]), tri(seen_t[B - 1]))
    u2 = ic - b2
    u_t = jnp.concatenate([u2, jnp.zeros((L - B, L), F32)], axis=0).T
    u_col = jnp.stack([jnp.broadcast_to(u_t[:, i:i + 1], (L, L)) for i in range(B)])
    b, i_g, f_g = _rows(b2), _rows(ic), _rows(fc)
    b_last = jnp.sum(f_g, axis=2, keepdims=True)

    log_d = jnp.where(seen_t, b + u_col, -jnp.inf)
    a = b + m
    m_t = jnp.maximum(a, jnp.max(log_d, axis=1, keepdims=True))
    dmat = jnp.exp(log_d - m_t)
    inter = jnp.exp(a - m_t)
    sc = _per_chain(_dot, k, q_t) * dmat
    cq = _per_chain(_dot, CT, q_t)
    num = _paired_value_matmul(v_t, sc.astype(BF16)) + inter * cq[:, :d, :]
    den = jnp.sum(sc, axis=1, keepdims=True) + inter * cq[:, d:d + 1, :]
    h_t = num * (1.0 / jnp.maximum(jnp.abs(den), jnp.exp(-m_t)))

    g = b_last - b + i_g
    m_new = jnp.maximum(b_last + m, jnp.max(g, axis=2, keepdims=True))
    w = jnp.exp(g - m_new)
    decay = jnp.exp(b_last + m - m_new)
    vw = jnp.concatenate([v_t.astype(F32), jnp.ones((B, ML_EXT, L), F32)], axis=1) * w
    CT_new = decay * CT + _per_chain(_dot, vw, k)
    return h_t, CT_new, m_new


def _mlstm_kernel(*refs, seq_len, has_state, emit_state):
    q_ref, k_ref, v_ref, so_ref, g_ref, gmh_ref = refs[:6]
    refs = refs[6:]
    if has_state:
        (c0_ref, n0_ref, m0_ref), refs = refs[:3], refs[3:]
    hm_ref, refs = refs[0], refs[1:]
    if emit_state:
        (c_out_ref, n_out_ref, m_out_ref), refs = refs[:3], refs[3:]
    ct_sc, m_sc, gr_sc, h_sc = refs

    nc = seq_len // CHUNK
    d_head = ML_HEAD_DIM
    chains = [(d, h) for d in range(2) for h in range(ML_HEADS)]
    n_chain = len(chains)
    for i, (d, h) in enumerate(chains):
        if has_state:
            ct_sc[i, 0:d_head, :] = c0_ref[d, h].T
            ct_sc[i, d_head:d_head + ML_EXT, :] = jnp.broadcast_to(n0_ref[d, h], (ML_EXT, d_head))
            m_sc[i] = m0_ref[d, h][:, 0:1]
        else:
            ct_sc[i] = jnp.zeros((d_head + ML_EXT, d_head), F32)
            m_sc[i] = jnp.zeros((1, 1), F32)
    for c in range(nc):
        gr_sc[c] = g_ref[c * CHUNK:(c + 1) * CHUNK, :].T

    s_idx = lax.broadcasted_iota(jnp.int32, (CHUNK, CHUNK), 0)
    t_idx = lax.broadcasted_iota(jnp.int32, (CHUNK, CHUNK), 1)
    seen_t = jnp.stack([s_idx <= t_idx] * ML_HEADS + [s_idx >= t_idx] * ML_HEADS)
    reverse_rows = lax.broadcasted_iota(jnp.int32, (n_chain, CHUNK), 0) >= ML_HEADS

    def step(c_fwd, c_bwd, rows_of):
        chunk_of = (c_fwd, c_bwd)
        g_fwd, g_bwd = gr_sc[c_fwd], gr_sc[c_bwd]
        lo = 2 * ML_HEADS
        ic = jnp.concatenate([g_fwd[0:ML_HEADS], g_bwd[lo:lo + ML_HEADS]], axis=0)
        fc = jnp.concatenate([g_fwd[ML_HEADS:lo], g_bwd[lo + ML_HEADS:2 * lo]], axis=0)
        head = [slice(h * d_head, (h + 1) * d_head) for _, h in chains]
        stack = lambda pick: jnp.stack([pick(i, chunk_of[d]) for i, (d, _) in enumerate(chains)])
        h_t, CT_new, m_new = _mlstm_step(
            stack(lambda i, c: k_ref[rows_of(c), head[i]]),
            stack(lambda i, c: q_ref[c, head[i], :]),
            stack(lambda i, c: v_ref[c, head[i], :]),
            ic, fc, ct_sc[...], m_sc[...], seen_t, reverse_rows)
        ct_sc[...] = CT_new
        m_sc[...] = m_new
        for i, (d, _) in enumerate(chains):
            h_sc[d, chunk_of[d], head[i], :] = h_t[i]

    if nc <= 2:
        for c in range(nc):
            step(c, nc - 1 - c, lambda cc: slice(cc * CHUNK, (cc + 1) * CHUNK))
    else:
        def body(c, carry):
            step(c, nc - 1 - c, lambda cc: pl.ds(pl.multiple_of(cc * CHUNK, CHUNK), CHUNK))
            return carry
        lax.fori_loop(0, nc, body, 0)

    for h in range(ML_HEADS):
        hcols = slice(h * d_head, (h + 1) * d_head)
        gain = jnp.broadcast_to(gmh_ref[h], (d_head, CHUNK))
        for c in range(nc):
            rows = slice(c * CHUNK, (c + 1) * CHUNK)
            hsum = h_sc[0, c, hcols, :] + h_sc[1, c, hcols, :]
            hn = hsum * lax.rsqrt(jnp.mean(hsum * hsum, axis=0, keepdims=True) + EPS) * gain
            hm_ref[rows, hcols] = (hn.T * so_ref[rows, hcols]).astype(hm_ref.dtype)
    if emit_state:
        for i, (d, h) in enumerate(chains):
            c_out_ref[d, h] = ct_sc[i, 0:d_head, :]
            n_out_ref[d, h] = ct_sc[i, d_head:d_head + 1, :]
            m_out_ref[d, h] = jnp.broadcast_to(m_sc[i], (1, LANES))


def _mlstm(q_t, k, v_t, so, gates, g_mh, state, batch, seq_len, emit_state):
    rows = k.shape[0]
    d = ML_HEAD_DIM
    nc = seq_len // CHUNK
    tile = pl.BlockSpec((seq_len, D_MODEL), lambda b: (b, 0))
    tile_t = pl.BlockSpec((nc, D_MODEL, CHUNK), lambda b: (b, 0, 0))
    in_specs = [tile_t, tile, tile_t, tile,
                pl.BlockSpec((seq_len, GATE_PAD), lambda b: (b, 0)),
                _resident((ML_HEADS, d, 1))]
    args = [q_t, k, v_t, so, gates, g_mh.reshape(ML_HEADS, d, 1)]
    c_spec = pl.BlockSpec((None, 2, ML_HEADS, d, d), lambda b: (b, 0, 0, 0, 0))
    n_spec = pl.BlockSpec((None, 2, ML_HEADS, 1, d), lambda b: (b, 0, 0, 0, 0))
    m_spec = pl.BlockSpec((None, 2, ML_HEADS, 1, LANES), lambda b: (b, 0, 0, 0, 0))
    if state is not None:
        in_specs += [c_spec, n_spec, m_spec]
        args += list(state)
    out_specs = [tile]
    out_shape = [jax.ShapeDtypeStruct((rows, D_MODEL), BF16)]
    if emit_state:
        out_specs += [c_spec, n_spec, m_spec]
        out_shape += [jax.ShapeDtypeStruct((batch, 2, ML_HEADS, d, d), F32),
                      jax.ShapeDtypeStruct((batch, 2, ML_HEADS, 1, d), F32),
                      jax.ShapeDtypeStruct((batch, 2, ML_HEADS, 1, LANES), F32)]
    n_state = 2 * ML_HEADS
    return pl.pallas_call(
        functools.partial(_mlstm_kernel, seq_len=seq_len, has_state=state is not None, emit_state=emit_state),
        grid=(batch,),
        in_specs=in_specs,
        out_specs=out_specs,
        out_shape=out_shape,
        scratch_shapes=[pltpu.VMEM((n_state, d + ML_EXT, d), F32), pltpu.VMEM((n_state, 1, 1), F32),
                        pltpu.VMEM((nc, GATE_PAD, CHUNK), F32),
                        pltpu.VMEM((2, nc, D_MODEL, CHUNK), F32)],
        compiler_params=_params(1),
        name="mlstm",
    )(*args)


def _merge_kernel(*refs, tiles):
    n = len(tiles)
    (x_ref, m_ref), refs = refs[:2], refs[2:]
    parts = [refs[j * n:(j + 1) * n] for j in range(4)]
    wm_ref, wd_ref, wo_ref, o_ref, w_sc = refs[4 * n:]
    i = pl.program_id(0)

    @pl.when(i == 0)
    def _():
        for j, w_ref in enumerate((wm_ref, wd_ref, wo_ref)):
            w_sc[j] = w_ref[...].astype(BF16)

    def tile(hm_ref, att_ref, sgm_ref, sgd_ref):
        y = sgm_ref[...] * _dot(hm_ref[...], w_sc[0]) + sgd_ref[...] * _dot(att_ref[...], w_sc[1])
        o_ref[...] = x_ref[...] + m_ref[5:6, :] * _dot(y, w_sc[2])

    starts = np.cumsum((0,) + tiles).tolist()
    for k in range(n):
        pl.when((i >= starts[k]) & (i < starts[k + 1]))(functools.partial(tile, *[p[k] for p in parts]))


def _merge(x, mods, mod_index, hm, att, sgm, sgd, wm, wd, wo):
    rows = x.shape[0]
    tiles = tuple(h.shape[0] // MERGE_TM for h in hm)
    assert sum(tiles) * MERGE_TM == rows
    tile = pl.BlockSpec((MERGE_TM, D_MODEL), lambda i: (i, 0))

    def part_spec(k):
        start = sum(tiles[:k])
        return pl.BlockSpec((MERGE_TM, D_MODEL), lambda i: (jnp.clip(i - start, 0, tiles[k] - 1), 0))

    part_specs = [part_spec(k) for k in range(len(tiles))]
    w_spec = _resident((D_MODEL, D_MODEL))
    return pl.pallas_call(
        functools.partial(_merge_kernel, tiles=tiles),
        grid=(rows // MERGE_TM,),
        in_specs=[tile, pl.BlockSpec((None, N_MOD, D_MODEL), lambda i: (mod_index(i * MERGE_TM), 0, 0))]
                 + part_specs * 4 + [w_spec, w_spec, w_spec],
        out_specs=tile,
        out_shape=jax.ShapeDtypeStruct((rows, D_MODEL), F32),
        scratch_shapes=[pltpu.VMEM((3, D_MODEL, D_MODEL), BF16)],
        compiler_params=_params(1),
        name="branch_merge",
    )(x, mods, *hm, *att, *sgm, *sgd, wm, wd, wo)


def _rope_tables(seq_len):
    lane = np.arange(LANES)
    r = lane % 32
    freqs = np.power(np.float32(ROPE_BASE), -(r % 16).astype(np.float32) / np.float32(16.0))
    tok = np.arange(seq_len)
    pos = np.where((lane % 64 < 32)[None, :], (tok // GRID_W)[:, None], (tok % GRID_W)[:, None]).astype(np.float32)
    ang = pos * freqs[None, :]
    sign = np.where(r < 16, -1.0, 1.0).astype(np.float32)
    return jnp.asarray(np.cos(ang), F32), jnp.asarray(np.sin(ang) * sign[None, :], F32)


def _mixer_branches(x, row0, mods, mod_index, w, batch, seq_len, ctx, lam_init):
    rope_tabs = None if ctx is None else _rope_tables(seq_len)
    mq, mk, mv, so, dq, dk, dv, sgm, sgd, gates = _proj(
        x, row0, batch * seq_len, mods, mod_index, w["g_norm"][1], w["w_in_t"], w["b_gate"], w["bd"], w["g_qn"],
        w["g_kn_col"] if ctx is None else w["g_kn"], rope_tabs, seq_len)
    cache = None if ctx is None else (ctx[0], ctx[1])
    att = _attn(dq, dk, dv, cache, w["lam"], w["g_sub"], batch, seq_len, lam_init)
    state = None if ctx is None else ctx[2]
    res = _mlstm(mq, mk, mv, so, gates, w["g_mh"], state, batch, seq_len, emit_state=ctx is None)
    return res[0], att, sgm, sgd, dk, dv, res[1:]


def kernel(x_prompt, x_sample, c, cache_k, cache_v, state_C, state_n, state_m, c_ctx, w_ada, b_ada, g_norm, ffn1_w1, ffn1_w3, ffn1_w2, ffn2_w1, ffn2_w3, ffn2_w2, w_in, b_gate, g_qn, g_kn, lam_q1, lam_k1, lam_q2, lam_k2, g_sub, g_mh, w_br_m, w_br_d, w_out):
    depth = w_ada.shape[0]
    assert depth == 1
    l = 0
    bp, tp, _ = x_prompt.shape
    bs, ts, _ = x_sample.shape
    past = cache_k.shape[2]
    lam_init = 0.8 - 0.6 * math.exp(-0.3 * l)

    cvecs = jnp.concatenate([c_ctx[None, :], c, jnp.zeros((8 - 1 - bs, D_MODEL), F32)], axis=0)
    mods = _mods(cvecs, w_ada[l], b_ada[l]).reshape(8, N_MOD, D_MODEL)

    group = jnp.arange(MXU_DIM) // DA_HEAD_DIM
    w = dict(
        g_norm=g_norm[l],
        ffn1_w1=ffn1_w1[l], ffn1_w3=ffn1_w3[l], ffn1_w2=ffn1_w2[l],
        ffn2_w1=ffn2_w1[l], ffn2_w3=ffn2_w3[l], ffn2_w2=ffn2_w2[l],
        w_in_t=w_in[l].T,
        b_gate=jnp.pad(b_gate[l], (0, GATE_PAD - N_GATE_COLS)).reshape(1, GATE_PAD),
        bd=(group[:, None] == group[None, :]).astype(BF16),
        g_qn=jnp.tile(g_qn[l] * QK_LOG2_SCALE, MXU_DIM // DA_HEAD_DIM).reshape(1, MXU_DIM),
        g_kn=jnp.tile(g_kn[l], MXU_DIM // DA_HEAD_DIM).reshape(1, MXU_DIM),
        g_kn_col=g_kn[l].reshape(DA_HEAD_DIM, 1),
        lam=jnp.stack([lam_q1[l], lam_k1[l], lam_q2[l], lam_k2[l]]),
        g_sub=g_sub[l], g_mh=g_mh[l],
        w_br_m=w_br_m[l], w_br_d=w_br_d[l], w_out=w_out[l],
    )

    n_ctx, n_lat = bp * tp, bs * ts
    mod_index = lambda r: jnp.where(r < n_ctx, 0, 1 + (r - n_ctx) // ts)
    ffn = lambda xs, out_rows, j, base: _ffn(xs, out_rows, mods, mod_index, w["g_norm"][j], w[f"ffn{1 + j // 2}_w1"],
                                             w[f"ffn{1 + j // 2}_w3"], w[f"ffn{1 + j // 2}_w2"], base=base)
    (x1,) = ffn((x_prompt.reshape(n_ctx, D_MODEL), x_sample.reshape(n_lat, D_MODEL)), (n_ctx + n_lat,), 0, 0)

    hm_p, att_p, sgm_p, sgd_p, new_k_t, new_v, (new_c, new_n, new_m) = _mixer_branches(
        x1, 0, mods, mod_index, w, bp, tp, None, lam_init)

    ctx = (cache_k[:, l].transpose(0, 2, 3, 4, 1).reshape(bs, D_MODEL, past),
           cache_v[:, l].reshape(bs, past * DA_HEADS, DA_V_DIM),
           (state_C[:, l], state_n[:, l].reshape(bs, 2, ML_HEADS, 1, ML_HEAD_DIM),
            jnp.broadcast_to(state_m[:, l][..., None, None], (bs, 2, ML_HEADS, 1, LANES))))
    hm_s, att_s, sgm_s, sgd_s, _, _, _ = _mixer_branches(x1, n_ctx, mods, mod_index, w, bs, ts, ctx, lam_init)

    x2 = _merge(x1, mods, mod_index, (hm_p, hm_s), (att_p, att_s), (sgm_p, sgm_s), (sgd_p, sgd_s),
                w["w_br_m"], w["w_br_d"], w["w_out"])
    xp, xs = ffn((x2,), (n_ctx, n_lat), 2, 6)

    return (xp.reshape(bp, tp, D_MODEL), xs.reshape(bs, ts, D_MODEL),
            new_k_t.reshape(bp, DA_HEADS, 2, DA_HEAD_DIM, tp).transpose(0, 4, 1, 2, 3)[:, None],
            new_v.reshape(bp, 1, tp, DA_HEADS, DA_V_DIM),
            jnp.swapaxes(new_c, -1, -2)[:, None], new_n.reshape(bp, 1, 2, ML_HEADS, ML_HEAD_DIM),
            new_m[..., 0, 0][:, None])
```

```python
import functools
import math

import jax
import jax.numpy as jnp
import numpy as np
from jax import lax
from jax.experimental import pallas as pl
from jax.experimental.pallas import tpu as pltpu

F32 = jnp.float32
BF16 = jnp.bfloat16

D_MODEL = 1024
D_FF = 2816
N_MOD = 9
GRID_W = 64
ML_HEADS = 4
ML_HEAD_DIM = 256
DA_HEADS = 8
DA_HEAD_DIM = 64
DA_V_DIM = 128
N_GATE_COLS = 16
CHUNK = 128
ROPE_BASE = 10000.0
QK_LOG2_SCALE = DA_HEAD_DIM ** -0.5 * math.log2(math.e)
EPS = 1e-6

LANES = 128
GATE_PAD = LANES
MXU_DIM = 256
VMEM_LIMIT = 56 * 1024 * 1024

FFN_TM = 512
FFN_TF = 256
PROJ_TM = 256
MERGE_TM = 512
ATT_TQ = 256
ATT_TK_MAX = 768


def _params(n_axes):
    return pltpu.CompilerParams(dimension_semantics=("arbitrary",) * n_axes,
                                vmem_limit_bytes=VMEM_LIMIT)


def _dot(a, b):
    return jnp.dot(a.astype(BF16), b.astype(BF16), preferred_element_type=F32)


def _dot_nt(a, b):
    return lax.dot_general(a.astype(BF16), b.astype(BF16), (((1,), (1,)), ((), ())),
                           preferred_element_type=F32)


def _dot_tn(a, b):
    return lax.dot_general(a.astype(BF16), b.astype(BF16), (((0,), (0,)), ((), ())),
                           preferred_element_type=F32)


def _sigmoid(x):
    return 1.0 / (1.0 + jnp.exp(-x))


def _log_sigmoid(x):
    return jnp.minimum(x, 0.0) - jnp.log1p(jnp.exp(-jnp.abs(x)))


def _modulated_norm(x, g, shift, scale):
    y = x * lax.rsqrt(jnp.mean(x * x, axis=-1, keepdims=True) + EPS) * g
    return y * (1.0 + scale) + shift


def _resident(shape):
    return pl.BlockSpec(shape, lambda *_: (0,) * len(shape), pipeline_mode=pl.Buffered(1))


def _mods_kernel(c_ref, w_ref, b_ref, o_ref):
    c = c_ref[...]
    o_ref[...] = _dot(c * _sigmoid(c), w_ref[...]) + b_ref[...]


def _mods(cvecs, w_ada, b_ada):
    n = N_MOD * D_MODEL
    tn = D_MODEL
    return pl.pallas_call(
        _mods_kernel,
        grid=(n // tn,),
        in_specs=[pl.BlockSpec((8, D_MODEL), lambda j: (0, 0)),
                  pl.BlockSpec((D_MODEL, tn), lambda j: (0, j)),
                  pl.BlockSpec((1, tn), lambda j: (0, j))],
        out_specs=pl.BlockSpec((8, tn), lambda j: (0, j)),
        out_shape=jax.ShapeDtypeStruct((8, n), F32),
        compiler_params=_params(1),
        name="adaln_mods",
    )(cvecs, w_ada, b_ada.reshape(1, n))


def _ffn_kernel(*refs, base, in_tiles, out_tiles):
    n_in, n_out = len(in_tiles), len(out_tiles)
    x_refs, refs = refs[:n_in], refs[n_in:]
    (m_ref, g_ref, w1_ref, w3_ref, w2_ref), refs = refs[:5], refs[5:]
    o_refs, (w1_sc, w3_sc, w2_sc, hh_sc, acc_sc) = refs[:n_out], refs[n_out:]
    g = pl.program_id(0)
    nf = D_FF // FFN_TF
    row_tile = g - (nf - 1)
    norm = lambda x: _modulated_norm(x, g_ref[...], m_ref[base:base + 1, :], m_ref[base + 1:base + 2, :]).astype(BF16)
    finish = lambda x, acc: x + 0.5 * m_ref[base + 2:base + 3, :] * acc

    def tile(hh, f):
        a = _dot(hh, w1_sc[f])
        b = _dot(hh, w3_sc[f])
        return _dot(a * _sigmoid(a) * b, w2_sc[f])

    @pl.when(g < nf)
    def _():
        w1_sc[g] = w1_ref[...].astype(BF16)
        w3_sc[g] = w3_ref[...].astype(BF16)
        w2_sc[g] = w2_ref[...].astype(BF16)

        @pl.when(g == 0)
        def _():
            hh_sc[...] = norm(x_refs[0][...])
            acc_sc[...] = jnp.zeros(acc_sc.shape, F32)

        acc_sc[...] += tile(hh_sc[...], g)

        @pl.when(g == nf - 1)
        def _():
            o_refs[0][...] = finish(x_refs[0][...], acc_sc[...])

    def full_tile(x_ref, o_ref):
        x = x_ref[...]
        hh = norm(x)
        acc = jnp.zeros(x.shape, F32)
        for f in range(nf):
            acc = acc + tile(hh, f)
        o_ref[...] = finish(x, acc)

    bounds = sorted(set(np.cumsum((0,) + in_tiles).tolist()) | set(np.cumsum((0,) + out_tiles).tolist()))
    for lo, hi in zip(bounds[:-1], bounds[1:]):
        k_in = int(np.searchsorted(np.cumsum(in_tiles), lo, side="right"))
        k_out = int(np.searchsorted(np.cumsum(out_tiles), lo, side="right"))
        pl.when((g >= nf) & (row_tile >= lo) & (row_tile < hi))(
            functools.partial(full_tile, x_refs[k_in], o_refs[k_out]))


def _ffn(xs, out_rows, mods, mod_index, g, w1, w3, w2, base):
    nf = D_FF // FFN_TF
    in_tiles = tuple(x.shape[0] // FFN_TM for x in xs)
    out_tiles = tuple(r // FFN_TM for r in out_rows)
    assert sum(in_tiles) == sum(out_tiles)
    row_tile = lambda s: jnp.maximum(s - (nf - 1), 0)
    f_tile = lambda s: jnp.minimum(s, nf - 1)

    def part_spec(tiles, k):
        start = sum(tiles[:k])
        return pl.BlockSpec((FFN_TM, D_MODEL), lambda s: (jnp.clip(row_tile(s) - start, 0, tiles[k] - 1), 0))

    return pl.pallas_call(
        functools.partial(_ffn_kernel, base=base, in_tiles=in_tiles, out_tiles=out_tiles),
        grid=(nf - 1 + sum(in_tiles),),
        in_specs=[part_spec(in_tiles, k) for k in range(len(xs))] + [
            pl.BlockSpec((None, N_MOD, D_MODEL), lambda s: (mod_index(row_tile(s) * FFN_TM), 0, 0)),
            _resident((1, D_MODEL)),
            pl.BlockSpec((D_MODEL, FFN_TF), lambda s: (0, f_tile(s))),
            pl.BlockSpec((D_MODEL, FFN_TF), lambda s: (0, f_tile(s))),
            pl.BlockSpec((FFN_TF, D_MODEL), lambda s: (f_tile(s), 0))],
        out_specs=[part_spec(out_tiles, k) for k in range(len(out_rows))],
        out_shape=[jax.ShapeDtypeStruct((r, D_MODEL), F32) for r in out_rows],
        scratch_shapes=[pltpu.VMEM((nf, D_MODEL, FFN_TF), BF16), pltpu.VMEM((nf, D_MODEL, FFN_TF), BF16),
                        pltpu.VMEM((nf, FFN_TF, D_MODEL), BF16),
                        pltpu.VMEM((FFN_TM, D_MODEL), BF16), pltpu.VMEM((FFN_TM, D_MODEL), F32)],
        compiler_params=_params(1),
        name="ffn",
    )(*xs, mods, g.reshape(1, D_MODEL), w1, w3, w2)


def _group_norm64(x, bd, g):
    ss = _dot(x * x, bd)
    return x * lax.rsqrt(ss * (1.0 / DA_HEAD_DIM) + EPS) * g


def _rope(x, cos, sin_signed):
    first = (lax.broadcasted_iota(jnp.int32, x.shape, 1) % 32) < 16
    partner = jnp.where(first, pltpu.roll(x, LANES - 16, 1), pltpu.roll(x, 16, 1))
    return x * cos + partner * sin_signed


N_SEG = 9
GATE_LO = 4 * D_MODEL


def _seg_start(s):
    return s * D_MODEL + N_GATE_COLS * (s >= 4)


def _proj_kernel(*refs, rope):
    (x_ref, m_ref, g_ref, wt_ref, wgt_ref, bg_ref, bd_ref, gq_ref, gk_ref), refs = refs[:9], refs[9:]
    if rope:
        (cos_ref, sin_ref), refs = refs[:2], refs[2:]
    (mq_ref, mk_ref, mv_ref, so_ref, dq_ref, dk_ref, dv_ref, sgm_ref, sgd_ref, gates_ref,
     w_sc, wg_sc, hh_sc) = refs
    step = pl.program_id(0)
    norm = lambda: _modulated_norm(x_ref[...], g_ref[...], m_ref[3:4, :], m_ref[4:5, :]).astype(BF16)

    def gates(hh):
        y = _dot_nt(hh, wg_sc[...]) + bg_ref[...]
        col = lax.broadcasted_iota(jnp.int32, y.shape, 1)
        gates_ref[...] = jnp.where((col // ML_HEADS) % 2 == 1, _log_sigmoid(y), y)

    def qk_segment(y, g_head_ref, o_ref):
        bd = bd_ref[...]
        for c in range(D_MODEL // MXU_DIM):
            cols = slice(c * MXU_DIM, (c + 1) * MXU_DIM)
            z = _group_norm64(y[:, cols], bd, g_head_ref[...])
            if rope:
                z = jnp.concatenate(
                    [_rope(z[:, k * LANES:(k + 1) * LANES], cos_ref[...], sin_ref[...])
                     for k in range(MXU_DIM // LANES)], axis=1)
            o_ref[:, cols] = z.astype(o_ref.dtype)

    def cache_key_segment(hh):
        y_t = _dot_nt(w_sc[5], hh)
        z = y_t.reshape(D_MODEL // DA_HEAD_DIM, DA_HEAD_DIM, y_t.shape[1])
        ms = jnp.mean(z * z, axis=1, keepdims=True)
        gain = jnp.broadcast_to(gk_ref[...], z.shape[1:])
        dk_ref[...] = (z * lax.rsqrt(ms + EPS) * gain).reshape(y_t.shape)

    def segment(s, hh):
        if s in (0, 2):
            o_ref = {0: mq_ref, 2: mv_ref}[s]
            y_t = _dot_nt(w_sc[s], hh) * ((ML_HEAD_DIM ** -0.5) if s == 0 else 1.0)
            for j in range(PROJ_TM // CHUNK):
                o_ref[j] = y_t[:, j * CHUNK:(j + 1) * CHUNK].astype(o_ref.dtype)
            return
        if s == 5 and not rope:
            cache_key_segment(hh)
            return
        y = _dot_nt(hh, w_sc[s])
        if s == 4:
            qk_segment(y, gq_ref, dq_ref)
        elif s == 5:
            qk_segment(y, gk_ref, dk_ref)
        elif s in (3, 7, 8):
            {3: so_ref, 7: sgm_ref, 8: sgd_ref}[s][...] = _sigmoid(y)
        elif s == 6 and not rope:
            dv_ref[...] = y.reshape(y.shape[0], DA_HEADS, DA_V_DIM)
        else:
            o_ref = {1: mk_ref, 6: dv_ref}[s]
            o_ref[...] = y.astype(o_ref.dtype)

    for s in range(N_SEG):
        @pl.when(step == s)
        def _(s=s):
            w_sc[s] = wt_ref[...].astype(BF16)
            if s == 0:
                wg_sc[...] = jnp.zeros(wg_sc.shape, BF16)
                wg_sc[0:N_GATE_COLS, :] = wgt_ref[...].astype(BF16)
                hh_sc[...] = norm()
                gates(hh_sc[...])
            segment(s, hh_sc[...])

    @pl.when(step >= N_SEG)
    def _():
        hh = norm()
        gates(hh)
        for s in range(N_SEG):
            segment(s, hh)


def _proj(x, row0, rows, mods, mod_index, g, w_in_t, b_gate, bd, gq, gk, rope_tabs, seq_len):
    rope = rope_tabs is not None
    row_tile = lambda s: jnp.maximum(s - (N_SEG - 1), 0)
    row = lambda s: (row_tile(s), 0)
    tile = pl.BlockSpec((PROJ_TM, D_MODEL), row)
    elem = lambda n: (pl.Element(n), pl.Element(D_MODEL))
    seg_row = lambda s: pl.multiple_of(_seg_start(jnp.minimum(s, N_SEG - 1)), N_GATE_COLS)
    in_specs = [pl.BlockSpec((PROJ_TM, D_MODEL), lambda s: (row0 // PROJ_TM + row_tile(s), 0)),
                pl.BlockSpec((None, N_MOD, D_MODEL), lambda s: (mod_index(row0 + row_tile(s) * PROJ_TM), 0, 0)),
                _resident((1, D_MODEL)),
                pl.BlockSpec(elem(D_MODEL), lambda s: (seg_row(s), 0)),
                pl.BlockSpec(elem(N_GATE_COLS), lambda s: (GATE_LO, 0)),
                _resident((1, GATE_PAD)),
                _resident((MXU_DIM, MXU_DIM)),
                _resident((1, MXU_DIM)),
                _resident(gk.shape)]
    args = [x, mods, g.reshape(1, D_MODEL), w_in_t, w_in_t, b_gate, bd, gq, gk]
    if rope:
        tiles_per_seq = seq_len // PROJ_TM
        tab = pl.BlockSpec((PROJ_TM, LANES), lambda s: (row_tile(s) % tiles_per_seq, 0))
        in_specs += [tab, tab]
        args += list(rope_tabs)
    dtypes = [BF16, BF16, BF16, F32, BF16, BF16, BF16 if rope else F32, F32, F32]
    out_shape = [jax.ShapeDtypeStruct((rows, D_MODEL), dt) for dt in dtypes]
    out_shape.append(jax.ShapeDtypeStruct((rows, GATE_PAD), F32))
    out_specs = [tile] * 9 + [pl.BlockSpec((PROJ_TM, GATE_PAD), row)]
    slabs = PROJ_TM // CHUNK
    for j in (0, 2):
        out_shape[j] = jax.ShapeDtypeStruct((rows // CHUNK, D_MODEL, CHUNK), BF16)
        out_specs[j] = pl.BlockSpec((slabs, D_MODEL, CHUNK), lambda s: (row_tile(s), 0, 0))
    if not rope:
        assert seq_len == PROJ_TM
        out_shape[5] = jax.ShapeDtypeStruct((rows // seq_len, D_MODEL, seq_len), F32)
        out_specs[5] = pl.BlockSpec((None, D_MODEL, seq_len), lambda s: (row_tile(s), 0, 0))
        out_shape[6] = jax.ShapeDtypeStruct((rows, DA_HEADS, DA_V_DIM), F32)
        out_specs[6] = pl.BlockSpec((PROJ_TM, DA_HEADS, DA_V_DIM), lambda s: (row_tile(s), 0, 0))
    return pl.pallas_call(
        functools.partial(_proj_kernel, rope=rope),
        grid=(N_SEG - 1 + rows // PROJ_TM,),
        in_specs=in_specs,
        out_specs=out_specs,
        out_shape=out_shape,
        scratch_shapes=[pltpu.VMEM((N_SEG, D_MODEL, D_MODEL), BF16), pltpu.VMEM((GATE_PAD, D_MODEL), BF16),
                        pltpu.VMEM((PROJ_TM, D_MODEL), BF16)],
        compiler_params=_params(1),
        name="mixer_in_proj",
    )(*args)


def _lambda(lam_ref, lam_init):
    lam = lam_ref[...]
    s1 = jnp.sum(lam[0:1, :] * lam[1:2, :], axis=1, keepdims=True)
    s2 = jnp.sum(lam[2:3, :] * lam[3:4, :], axis=1, keepdims=True)
    return jnp.exp(s1) - jnp.exp(s2) + lam_init


def _attn_kernel(*refs, cached, seq, lam_init):
    if cached:
        q_ref, k_ref, v_ref, ck_ref, cv_ref, lam_ref, gs_ref, o_ref, kall_sc, vt_sc, s_sc = refs
    else:
        q_ref, k_ref, v_ref, lam_ref, gs_ref, o_ref, kall_sc, vt_sc, s_sc = refs
    def cache_order_kv(kt_ref, vr_ref, h, lo, n):
        kall_sc[h, lo:lo + n, :] = kt_ref[h * DA_V_DIM:(h + 1) * DA_V_DIM, :].T.astype(BF16)
        vt_sc[h, :, lo:lo + n] = vr_ref[pl.ds(h, n, stride=DA_HEADS), :].T.astype(BF16)

    @pl.when(pl.program_id(1) == 0)
    def _():
        for h in range(DA_HEADS):
            if cached:
                cols = slice(h * DA_V_DIM, (h + 1) * DA_V_DIM)
                kall_sc[h, 0:seq, :] = k_ref[:, cols].astype(BF16)
                vt_sc[h, :, 0:seq] = v_ref[:, cols].astype(F32).T.astype(BF16)
                cache_order_kv(ck_ref, cv_ref, h, seq, ck_ref.shape[1])
            else:
                cache_order_kv(k_ref, v_ref, h, 0, seq)

    lam = _lambda(lam_ref, lam_init)
    lane = lax.broadcasted_iota(jnp.int32, (1, DA_V_DIM), 1)
    comp_masks = [lane < DA_HEAD_DIM, lane >= DA_HEAD_DIM]
    tq = q_ref.shape[0]
    n_keys = kall_sc.shape[1]
    n_tiles = pl.cdiv(n_keys, ATT_TK_MAX)
    tk = n_keys // n_tiles

    def stacked_q(h):
        q = q_ref[:, h * DA_V_DIM:(h + 1) * DA_V_DIM].astype(BF16)
        return jnp.concatenate([jnp.where(m, q, jnp.zeros_like(q)) for m in comp_masks], axis=0)

    def score_tile(h, j, qq, m8):
        rows = slice(j * tk, (j + 1) * tk)
        st = _dot_nt(kall_sc[h, rows, :], qq)
        s_sc[h % 2, rows, :] = st
        t8 = jnp.max(st.reshape(tk // 8, 8, 2 * tq), axis=0)
        return t8 if m8 is None else jnp.maximum(m8, t8)

    def prob_tile(h, j, mx, d8, pv):
        rows = slice(j * tk, (j + 1) * tk)
        e = jnp.exp2(s_sc[h % 2, rows, :] - mx)
        s8 = jnp.sum(e.reshape(tk // 8, 8, 2 * tq), axis=0)
        p = _dot(vt_sc[h, :, rows], e)
        return (s8 if d8 is None else d8 + s8), (p if pv is None else pv + p)

    qq = stacked_q(0)
    m8 = None
    for j in range(n_tiles):
        m8 = score_tile(0, j, qq, m8)
    for h in range(DA_HEADS):
        mx = jnp.max(m8, axis=0, keepdims=True)
        if h + 1 < DA_HEADS:
            qq = stacked_q(h + 1)
        m8, d8, pv = None, None, None
        for j in range(n_tiles):
            if h + 1 < DA_HEADS:
                m8 = score_tile(h + 1, j, qq, m8)
            d8, pv = prob_tile(h, j, mx, d8, pv)
        pv = pv * (1.0 / jnp.sum(d8, axis=0, keepdims=True))
        out_t = pv[:, :tq] - lam * pv[:, tq:]
        out_t = out_t * lax.rsqrt(jnp.mean(out_t * out_t, axis=0, keepdims=True) + EPS)
        out = out_t.T * gs_ref[...]
        o_ref[:, h * DA_V_DIM:(h + 1) * DA_V_DIM] = (out * (1.0 - lam_init)).astype(o_ref.dtype)


def _attn(q, k, v, cache, lam_vecs, g_sub, batch, seq_len, lam_init):
    rows = q.shape[0]
    nq = seq_len // ATT_TQ
    q_spec = pl.BlockSpec((ATT_TQ, D_MODEL), lambda b, i: (b * nq + i, 0))
    cache_specs = lambda n: [pl.BlockSpec((None, D_MODEL, n), lambda b, i: (b, 0, 0)),
                             pl.BlockSpec((None, n * DA_HEADS, DA_V_DIM), lambda b, i: (b, 0, 0))]
    n_keys = seq_len
    if cache is not None:
        past = cache[0].shape[2]
        n_keys += past
        kv_spec = pl.BlockSpec((seq_len, D_MODEL), lambda b, i: (b, 0))
        in_specs = [q_spec, kv_spec, kv_spec] + cache_specs(past)
        args = [q, k, v] + list(cache)
    else:
        in_specs = [q_spec] + cache_specs(seq_len)
        args = [q, k, v.reshape(batch, seq_len * DA_HEADS, DA_V_DIM)]
    in_specs += [_resident((4, DA_HEAD_DIM)), _resident((1, DA_V_DIM))]
    args += [lam_vecs, g_sub.reshape(1, DA_V_DIM)]
    return pl.pallas_call(
        functools.partial(_attn_kernel, cached=cache is not None, seq=seq_len, lam_init=lam_init),
        grid=(batch, nq),
        in_specs=in_specs,
        out_specs=q_spec,
        out_shape=jax.ShapeDtypeStruct((rows, D_MODEL), BF16),
        scratch_shapes=[pltpu.VMEM((DA_HEADS, n_keys, DA_V_DIM), BF16),
                        pltpu.VMEM((DA_HEADS, DA_V_DIM, n_keys), BF16),
                        pltpu.VMEM((2, n_keys, 2 * ATT_TQ), F32)],
        compiler_params=_params(2),
        name="diff_attention",
    )(*args)


def _per_chain(fn, a, b):
    return jnp.stack([fn(a[i], b[i]) for i in range(a.shape[0])])


ML_EXT = 8


def _rows(x):
    return jnp.stack([x[i:i + 1, :] for i in range(x.shape[0])])


def _split3(x):
    hi = x.astype(BF16)
    r = x - hi.astype(F32)
    mid = r.astype(BF16)
    lo = (r - mid.astype(F32)).astype(BF16)
    return jnp.concatenate([hi, mid, lo], axis=1)


def _chunk_scan(x, reverse_rows, tri_prefix, tri_suffix):
    parts = _split3(x)
    return jnp.where(reverse_rows, _dot(parts, tri_suffix), _dot(parts, tri_prefix))


def _paired_value_matmul(v_t, sc):
    B, L, _ = sc.shape
    zero = jnp.zeros((L, L), sc.dtype)
    out = []
    for i in range(0, B, 2):
        lhs = jnp.concatenate([v_t[i], v_t[i + 1]], axis=1)
        rhs = jnp.concatenate([jnp.concatenate([sc[i], zero], axis=1),
                               jnp.concatenate([zero, sc[i + 1]], axis=1)], axis=0)
        both = _dot(lhs, rhs)
        out += [both[:, :L], both[:, L:]]
    return jnp.stack(out)


def _mlstm_step(k, q_t, v_t, ic, fc, CT, m, seen_t, reverse_rows):
    B, L = ic.shape
    d = k.shape[2]
    tri = lambda keep: jnp.concatenate([jnp.where(keep, 1.0, 0.0).astype(BF16)] * 3, axis=0)
    b2 = _chunk_scan(fc, reverse_rows, tri(seen_t[0]), tri(seen_t[B - 1]))
    u2 = ic - b2
    u_t = jnp.concatenate([u2, jnp.zeros((L - B, L), F32)], axis=0).T
    u_col = jnp.stack([jnp.broadcast_to(u_t[:, i:i + 1], (L, L)) for i in range(B)])
    b, i_g, f_g = _rows(b2), _rows(ic), _rows(fc)
    b_last = jnp.sum(f_g, axis=2, keepdims=True)

    log_d = jnp.where(seen_t, b + u_col, -jnp.inf)
    a = b + m
    m_t = jnp.maximum(a, jnp.max(log_d, axis=1, keepdims=True))
    dmat = jnp.exp(log_d - m_t)
    inter = jnp.exp(a - m_t)
    sc = _per_chain(_dot, k, q_t) * dmat
    cq = _per_chain(_dot, CT, q_t)
    num = _paired_value_matmul(v_t, sc.astype(BF16)) + inter * cq[:, :d, :]
    den = jnp.sum(sc, axis=1, keepdims=True) + inter * cq[:, d:d + 1, :]
    h_t = num * (1.0 / jnp.maximum(jnp.abs(den), jnp.exp(-m_t)))

    g = b_last - b + i_g
    m_new = jnp.maximum(b_last + m, jnp.max(g, axis=2, keepdims=True))
    w = jnp.exp(g - m_new)
    decay = jnp.exp(b_last + m - m_new)
    vw = jnp.concatenate([v_t.astype(F32), jnp.ones((B, ML_EXT, L), F32)], axis=1) * w
    CT_new = decay * CT + _per_chain(_dot, vw, k)
    return h_t, CT_new, m_new


def _mlstm_kernel(*refs, seq_len, has_state, emit_state):
    q_ref, k_ref, v_ref, so_ref, g_ref, gmh_ref = refs[:6]
    refs = refs[6:]
    if has_state:
        (c0_ref, n0_ref, m0_ref), refs = refs[:3], refs[3:]
    hm_ref, refs = refs[0], refs[1:]
    if emit_state:
        (c_out_ref, n_out_ref, m_out_ref), refs = refs[:3], refs[3:]
    ct_sc, m_sc, gr_sc, h_sc = refs

    nc = seq_len // CHUNK
    d_head = ML_HEAD_DIM
    chains = [(d, h) for d in range(2) for h in range(ML_HEADS)]
    n_chain = len(chains)
    for i, (d, h) in enumerate(chains):
        if has_state:
            ct_sc[i, 0:d_head, :] = c0_ref[d, h].T
            ct_sc[i, d_head:d_head + ML_EXT, :] = jnp.broadcast_to(n0_ref[d, h], (ML_EXT, d_head))
            m_sc[i] = m0_ref[d, h][:, 0:1]
        else:
            ct_sc[i] = jnp.zeros((d_head + ML_EXT, d_head), F32)
            m_sc[i] = jnp.zeros((1, 1), F32)
    for c in range(nc):
        gr_sc[c] = g_ref[c * CHUNK:(c + 1) * CHUNK, :].T

    s_idx = lax.broadcasted_iota(jnp.int32, (CHUNK, CHUNK), 0)
    t_idx = lax.broadcasted_iota(jnp.int32, (CHUNK, CHUNK), 1)
    seen_t = jnp.stack([s_idx <= t_idx] * ML_HEADS + [s_idx >= t_idx] * ML_HEADS)
    reverse_rows = lax.broadcasted_iota(jnp.int32, (n_chain, CHUNK), 0) >= ML_HEADS

    def step(c_fwd, c_bwd, rows_of):
        chunk_of = (c_fwd, c_bwd)
        g_fwd, g_bwd = gr_sc[c_fwd], gr_sc[c_bwd]
        lo = 2 * ML_HEADS
        ic = jnp.concatenate([g_fwd[0:ML_HEADS], g_bwd[lo:lo + ML_HEADS]], axis=0)
        fc = jnp.concatenate([g_fwd[ML_HEADS:lo], g_bwd[lo + ML_HEADS:2 * lo]], axis=0)
        head = [slice(h * d_head, (h + 1) * d_head) for _, h in chains]
        stack = lambda pick: jnp.stack([pick(i, chunk_of[d]) for i, (d, _) in enumerate(chains)])
        h_t, CT_new, m_new = _mlstm_step(
            stack(lambda i, c: k_ref[rows_of(c), head[i]]),
            stack(lambda i, c: q_ref[c, head[i], :]),
            stack(lambda i, c: v_ref[c, head[i], :]),
            ic, fc, ct_sc[...], m_sc[...], seen_t, reverse_rows)
        ct_sc[...] = CT_new
        m_sc[...] = m_new
        for i, (d, _) in enumerate(chains):
            h_sc[d, chunk_of[d], head[i], :] = h_t[i]

    if nc <= 2:
        for c in range(nc):
            step(c, nc - 1 - c, lambda cc: slice(cc * CHUNK, (cc + 1) * CHUNK))
    else:
        def body(c, carry):
            step(c, nc - 1 - c, lambda cc: pl.ds(pl.multiple_of(cc * CHUNK, CHUNK), CHUNK))
            return carry
        lax.fori_loop(0, nc, body, 0)

    for h in range(ML_HEADS):
        hcols = slice(h * d_head, (h + 1) * d_head)
        gain = jnp.broadcast_to(gmh_ref[h], (d_head, CHUNK))
        for c in range(nc):
            rows = slice(c * CHUNK, (c + 1) * CHUNK)
            hsum = h_sc[0, c, hcols, :] + h_sc[1, c, hcols, :]
            hn = hsum * lax.rsqrt(jnp.mean(hsum * hsum, axis=0, keepdims=True) + EPS) * gain
            hm_ref[rows, hcols] = (hn.T * so_ref[rows, hcols]).astype(hm_ref.dtype)
    if emit_state:
        for i, (d, h) in enumerate(chains):
            c_out_ref[d, h] = ct_sc[i, 0:d_head, :].T
            n_out_ref[d, h] = ct_sc[i, d_head:d_head + 1, :]
            m_out_ref[d, h] = jnp.broadcast_to(m_sc[i], (1, LANES))


def _mlstm(q_t, k, v_t, so, gates, g_mh, state, batch, seq_len, emit_state):
    rows = k.shape[0]
    d = ML_HEAD_DIM
    nc = seq_len // CHUNK
    tile = pl.BlockSpec((seq_len, D_MODEL), lambda b: (b, 0))
    tile_t = pl.BlockSpec((nc, D_MODEL, CHUNK), lambda b: (b, 0, 0))
    in_specs = [tile_t, tile, tile_t, tile,
                pl.BlockSpec((seq_len, GATE_PAD), lambda b: (b, 0)),
                _resident((ML_HEADS, d, 1))]
    args = [q_t, k, v_t, so, gates, g_mh.reshape(ML_HEADS, d, 1)]
    c_spec = pl.BlockSpec((None, 2, ML_HEADS, d, d), lambda b: (b, 0, 0, 0, 0))
    n_spec = pl.BlockSpec((None, 2, ML_HEADS, 1, d), lambda b: (b, 0, 0, 0, 0))
    m_spec = pl.BlockSpec((None, 2, ML_HEADS, 1, LANES), lambda b: (b, 0, 0, 0, 0))
    if state is not None:
        in_specs += [c_spec, n_spec, m_spec]
        args += list(state)
    out_specs = [tile]
    out_shape = [jax.ShapeDtypeStruct((rows, D_MODEL), BF16)]
    if emit_state:
        out_specs += [c_spec, n_spec, m_spec]
        out_shape += [jax.ShapeDtypeStruct((batch, 2, ML_HEADS, d, d), F32),
                      jax.ShapeDtypeStruct((batch, 2, ML_HEADS, 1, d), F32),
                      jax.ShapeDtypeStruct((batch, 2, ML_HEADS, 1, LANES), F32)]
    n_state = 2 * ML_HEADS
    return pl.pallas_call(
        functools.partial(_mlstm_kernel, seq_len=seq_len, has_state=state is not None, emit_state=emit_state),
        grid=(batch,),
        in_specs=in_specs,
        out_specs=out_specs,
        out_shape=out_shape,
        scratch_shapes=[pltpu.VMEM((n_state, d + ML_EXT, d), F32), pltpu.VMEM((n_state, 1, 1), F32),
                        pltpu.VMEM((nc, GATE_PAD, CHUNK), F32),
                        pltpu.VMEM((2, nc, D_MODEL, CHUNK), F32)],
        compiler_params=_params(1),
        name="mlstm",
    )(*args)


def _merge_kernel(*refs, tiles):
    n = len(tiles)
    (x_ref, m_ref), refs = refs[:2], refs[2:]
    parts = [refs[j * n:(j + 1) * n] for j in range(4)]
    wm_ref, wd_ref, wo_ref, o_ref, w_sc = refs[4 * n:]
    i = pl.program_id(0)

    @pl.when(i == 0)
    def _():
        for j, w_ref in enumerate((wm_ref, wd_ref, wo_ref)):
            w_sc[j] = w_ref[...].astype(BF16)

    def tile(hm_ref, att_ref, sgm_ref, sgd_ref):
        y = sgm_ref[...] * _dot(hm_ref[...], w_sc[0]) + sgd_ref[...] * _dot(att_ref[...], w_sc[1])
        o_ref[...] = x_ref[...] + m_ref[5:6, :] * _dot(y, w_sc[2])

    starts = np.cumsum((0,) + tiles).tolist()
    for k in range(n):
        pl.when((i >= starts[k]) & (i < starts[k + 1]))(functools.partial(tile, *[p[k] for p in parts]))


def _merge(x, mods, mod_index, hm, att, sgm, sgd, wm, wd, wo):
    rows = x.shape[0]
    tiles = tuple(h.shape[0] // MERGE_TM for h in hm)
    assert sum(tiles) * MERGE_TM == rows
    tile = pl.BlockSpec((MERGE_TM, D_MODEL), lambda i: (i, 0))

    def part_spec(k):
        start = sum(tiles[:k])
        return pl.BlockSpec((MERGE_TM, D_MODEL), lambda i: (jnp.clip(i - start, 0, tiles[k] - 1), 0))

    part_specs = [part_spec(k) for k in range(len(tiles))]
    w_spec = _resident((D_MODEL, D_MODEL))
    return pl.pallas_call(
        functools.partial(_merge_kernel, tiles=tiles),
        grid=(rows // MERGE_TM,),
        in_specs=[tile, pl.BlockSpec((None, N_MOD, D_MODEL), lambda i: (mod_index(i * MERGE_TM), 0, 0))]
                 + part_specs * 4 + [w_spec, w_spec, w_spec],
        out_specs=tile,
        out_shape=jax.ShapeDtypeStruct((rows, D_MODEL), F32),
        scratch_shapes=[pltpu.VMEM((3, D_MODEL, D_MODEL), BF16)],
        compiler_params=_params(1),
        name="branch_merge",
    )(x, mods, *hm, *att, *sgm, *sgd, wm, wd, wo)


def _rope_tables(seq_len):
    lane = np.arange(LANES)
    r = lane % 32
    freqs = np.power(np.float32(ROPE_BASE), -(r % 16).astype(np.float32) / np.float32(16.0))
    tok = np.arange(seq_len)
    pos = np.where((lane % 64 < 32)[None, :], (tok // GRID_W)[:, None], (tok % GRID_W)[:, None]).astype(np.float32)
    ang = pos * freqs[None, :]
    sign = np.where(r < 16, -1.0, 1.0).astype(np.float32)
    return jnp.asarray(np.cos(ang), F32), jnp.asarray(np.sin(ang) * sign[None, :], F32)


def _mixer_branches(x, row0, mods, mod_index, w, batch, seq_len, ctx, lam_init):
    rope_tabs = None if ctx is None else _rope_tables(seq_len)
    mq, mk, mv, so, dq, dk, dv, sgm, sgd, gates = _proj(
        x, row0, batch * seq_len, mods, mod_index, w["g_norm"][1], w["w_in_t"], w["b_gate"], w["bd"], w["g_qn"],
        w["g_kn_col"] if ctx is None else w["g_kn"], rope_tabs, seq_len)
    cache = None if ctx is None else (ctx[0], ctx[1])
    att = _attn(dq, dk, dv, cache, w["lam"], w["g_sub"], batch, seq_len, lam_init)
    state = None if ctx is None else ctx[2]
    res = _mlstm(mq, mk, mv, so, gates, w["g_mh"], state, batch, seq_len, emit_state=ctx is None)
    return res[0], att, sgm, sgd, dk, dv, res[1:]


def kernel(x_prompt, x_sample, c, cache_k, cache_v, state_C, state_n, state_m, c_ctx, w_ada, b_ada, g_norm, ffn1_w1, ffn1_w3, ffn1_w2, ffn2_w1, ffn2_w3, ffn2_w2, w_in, b_gate, g_qn, g_kn, lam_q1, lam_k1, lam_q2, lam_k2, g_sub, g_mh, w_br_m, w_br_d, w_out):
    depth = w_ada.shape[0]
    assert depth == 1
    l = 0
    bp, tp, _ = x_prompt.shape
    bs, ts, _ = x_sample.shape
    past = cache_k.shape[2]
    lam_init = 0.8 - 0.6 * math.exp(-0.3 * l)

    cvecs = jnp.concatenate([c_ctx[None, :], c, jnp.zeros((8 - 1 - bs, D_MODEL), F32)], axis=0)
    mods = _mods(cvecs, w_ada[l], b_ada[l]).reshape(8, N_MOD, D_MODEL)

    group = jnp.arange(MXU_DIM) // DA_HEAD_DIM
    w = dict(
        g_norm=g_norm[l],
        ffn1_w1=ffn1_w1[l], ffn1_w3=ffn1_w3[l], ffn1_w2=ffn1_w2[l],
        ffn2_w1=ffn2_w1[l], ffn2_w3=ffn2_w3[l], ffn2_w2=ffn2_w2[l],
        w_in_t=w_in[l].T,
        b_gate=jnp.pad(b_gate[l], (0, GATE_PAD - N_GATE_COLS)).reshape(1, GATE_PAD),
        bd=(group[:, None] == group[None, :]).astype(BF16),
        g_qn=jnp.tile(g_qn[l] * QK_LOG2_SCALE, MXU_DIM // DA_HEAD_DIM).reshape(1, MXU_DIM),
        g_kn=jnp.tile(g_kn[l], MXU_DIM // DA_HEAD_DIM).reshape(1, MXU_DIM),
        g_kn_col=g_kn[l].reshape(DA_HEAD_DIM, 1),
        lam=jnp.stack([lam_q1[l], lam_k1[l], lam_q2[l], lam_k2[l]]),
        g_sub=g_sub[l], g_mh=g_mh[l],
        w_br_m=w_br_m[l], w_br_d=w_br_d[l], w_out=w_out[l],
    )

    n_ctx, n_lat = bp * tp, bs * ts
    mod_index = lambda r: jnp.where(r < n_ctx, 0, 1 + (r - n_ctx) // ts)
    ffn = lambda xs, out_rows, j, base: _ffn(xs, out_rows, mods, mod_index, w["g_norm"][j], w[f"ffn{1 + j // 2}_w1"],
                                             w[f"ffn{1 + j // 2}_w3"], w[f"ffn{1 + j // 2}_w2"], base=base)
    (x1,) = ffn((x_prompt.reshape(n_ctx, D_MODEL), x_sample.reshape(n_lat, D_MODEL)), (n_ctx + n_lat,), 0, 0)

    hm_p, att_p, sgm_p, sgd_p, new_k_t, new_v, (new_c, new_n, new_m) = _mixer_branches(
        x1, 0, mods, mod_index, w, bp, tp, None, lam_init)

    ctx = (cache_k[:, l].transpose(0, 2, 3, 4, 1).reshape(bs, D_MODEL, past),
           cache_v[:, l].reshape(bs, past * DA_HEADS, DA_V_DIM),
           (state_C[:, l], state_n[:, l].reshape(bs, 2, ML_HEADS, 1, ML_HEAD_DIM),
            jnp.broadcast_to(state_m[:, l][..., None, None], (bs, 2, ML_HEADS, 1, LANES))))
    hm_s, att_s, sgm_s, sgd_s, _, _, _ = _mixer_branches(x1, n_ctx, mods, mod_index, w, bs, ts, ctx, lam_init)

    x2 = _merge(x1, mods, mod_index, (hm_p, hm_s), (att_p, att_s), (sgm_p, sgm_s), (sgd_p, sgd_s),
                w["w_br_m"], w["w_br_d"], w["w_out"])
    xp, xs = ffn((x2,), (n_ctx, n_lat), 2, 6)

    return (xp.reshape(bp, tp, D_MODEL), xs.reshape(bs, ts, D_MODEL),
            new_k_t.reshape(bp, DA_HEADS, 2, DA_HEAD_DIM, tp).transpose(0, 4, 1, 2, 3)[:, None],
            new_v.reshape(bp, 1, tp, DA_HEADS, DA_V_DIM),
            new_c[:, None], new_n.reshape(bp, 1, 2, ML_HEADS, ML_HEAD_DIM),
            new_m[..., 0, 0][:, None])
```

```python
import functools
import math

import jax
import jax.numpy as jnp
import numpy as np
from jax import lax
from jax.experimental import pallas as pl
from jax.experimental.pallas import tpu as pltpu

F32 = jnp.float32
BF16 = jnp.bfloat16

D_MODEL = 1024
D_FF = 2816
N_MOD = 9
GRID_W = 64
ML_HEADS = 4
ML_HEAD_DIM = 256
DA_HEADS = 8
DA_HEAD_DIM = 64
DA_V_DIM = 128
N_GATE_COLS = 16
CHUNK = 128
ROPE_BASE = 10000.0
QK_LOG2_SCALE = DA_HEAD_DIM ** -0.5 * math.log2(math.e)
EPS = 1e-6

LANES = 128
GATE_PAD = LANES
MXU_DIM = 256
VMEM_LIMIT = 56 * 1024 * 1024

MODS_K_SPLIT = 4
FFN_TM = 512
FFN_TF = 256
PROJ_TM = 256
MERGE_TM = 512
ATT_TQ = 256
ATT_TK_MAX = 768


def _params(n_axes):
    return pltpu.CompilerParams(dimension_semantics=("arbitrary",) * n_axes,
                                vmem_limit_bytes=VMEM_LIMIT)


def _dot(a, b):
    return jnp.dot(a.astype(BF16), b.astype(BF16), preferred_element_type=F32)


def _dot_nt(a, b):
    return lax.dot_general(a.astype(BF16), b.astype(BF16), (((1,), (1,)), ((), ())),
                           preferred_element_type=F32)


def _dot_tn(a, b):
    return lax.dot_general(a.astype(BF16), b.astype(BF16), (((0,), (0,)), ((), ())),
                           preferred_element_type=F32)


def _sigmoid(x):
    return 1.0 / (1.0 + jnp.exp(-x))


def _log_sigmoid(x):
    return jnp.minimum(x, 0.0) - jnp.log1p(jnp.exp(-jnp.abs(x)))


def _modulated_norm(x, g, shift, scale):
    y = x * lax.rsqrt(jnp.mean(x * x, axis=-1, keepdims=True) + EPS) * g
    return y * (1.0 + scale) + shift


def _resident(shape):
    return pl.BlockSpec(shape, lambda *_: (0,) * len(shape), pipeline_mode=pl.Buffered(1))


def _mods_kernel(c_ref, *refs):
    *w_refs, b_ref, o_ref = refs
    c = c_ref[...]
    s = c * _sigmoid(c)
    kb = D_MODEL // len(w_refs)
    acc = b_ref[...]
    for j, w_ref in enumerate(w_refs):
        acc = acc + _dot(s[:, j * kb:(j + 1) * kb], w_ref[...])
    o_ref[...] = acc


def _mods(cvecs, w_ada, b_ada):
    n = N_MOD * D_MODEL
    tn = D_MODEL
    kb = D_MODEL // MODS_K_SPLIT
    w_specs = [pl.BlockSpec((kb, tn), lambda j, r=r: (r, j)) for r in range(MODS_K_SPLIT)]
    return pl.pallas_call(
        _mods_kernel,
        grid=(n // tn,),
        in_specs=[pl.BlockSpec((8, D_MODEL), lambda j: (0, 0))] + w_specs + [pl.BlockSpec((1, tn), lambda j: (0, j))],
        out_specs=pl.BlockSpec((8, tn), lambda j: (0, j)),
        out_shape=jax.ShapeDtypeStruct((8, n), F32),
        compiler_params=_params(1),
        name="adaln_mods",
    )(cvecs, *([w_ada] * MODS_K_SPLIT), b_ada.reshape(1, n))


def _ffn_kernel(*refs, base, in_tiles, out_tiles):
    n_in, n_out = len(in_tiles), len(out_tiles)
    x_refs, refs = refs[:n_in], refs[n_in:]
    (m_ref, g_ref, w1_ref, w3_ref, w2_ref), refs = refs[:5], refs[5:]
    o_refs, (w1_sc, w3_sc, w2_sc, hh_sc, acc_sc) = refs[:n_out], refs[n_out:]
    g = pl.program_id(0)
    nf = D_FF // FFN_TF
    row_tile = g - (nf - 1)
    norm = lambda x: _modulated_norm(x, g_ref[...], m_ref[base:base + 1, :], m_ref[base + 1:base + 2, :]).astype(BF16)
    finish = lambda x, acc: x + 0.5 * m_ref[base + 2:base + 3, :] * acc

    def tile(hh, f):
        a = _dot(hh, w1_sc[f])
        b = _dot(hh, w3_sc[f])
        return _dot(a * _sigmoid(a) * b, w2_sc[f])

    @pl.when(g < nf)
    def _():
        w1_sc[g] = w1_ref[...].astype(BF16)
        w3_sc[g] = w3_ref[...].astype(BF16)
        w2_sc[g] = w2_ref[...].astype(BF16)

        @pl.when(g == 0)
        def _():
            hh_sc[...] = norm(x_refs[0][...])
            acc_sc[...] = jnp.zeros(acc_sc.shape, F32)

        acc_sc[...] += tile(hh_sc[...], g)

        @pl.when(g == nf - 1)
        def _():
            o_refs[0][...] = finish(x_refs[0][...], acc_sc[...])

    def full_tile(x_ref, o_ref):
        x = x_ref[...]
        hh = norm(x)
        acc = jnp.zeros(x.shape, F32)
        for f in range(nf):
            acc = acc + tile(hh, f)
        o_ref[...] = finish(x, acc)

    bounds = sorted(set(np.cumsum((0,) + in_tiles).tolist()) | set(np.cumsum((0,) + out_tiles).tolist()))
    for lo, hi in zip(bounds[:-1], bounds[1:]):
        k_in = int(np.searchsorted(np.cumsum(in_tiles), lo, side="right"))
        k_out = int(np.searchsorted(np.cumsum(out_tiles), lo, side="right"))
        pl.when((g >= nf) & (row_tile >= lo) & (row_tile < hi))(
            functools.partial(full_tile, x_refs[k_in], o_refs[k_out]))


def _ffn(xs, out_rows, mods, mod_index, g, w1, w3, w2, base):
    nf = D_FF // FFN_TF
    in_tiles = tuple(x.shape[0] // FFN_TM for x in xs)
    out_tiles = tuple(r // FFN_TM for r in out_rows)
    assert sum(in_tiles) == sum(out_tiles)
    row_tile = lambda s: jnp.maximum(s - (nf - 1), 0)
    f_tile = lambda s: jnp.minimum(s, nf - 1)

    def part_spec(tiles, k):
        start = sum(tiles[:k])
        return pl.BlockSpec((FFN_TM, D_MODEL), lambda s: (jnp.clip(row_tile(s) - start, 0, tiles[k] - 1), 0))

    return pl.pallas_call(
        functools.partial(_ffn_kernel, base=base, in_tiles=in_tiles, out_tiles=out_tiles),
        grid=(nf - 1 + sum(in_tiles),),
        in_specs=[part_spec(in_tiles, k) for k in range(len(xs))] + [
            pl.BlockSpec((None, N_MOD, D_MODEL), lambda s: (mod_index(row_tile(s) * FFN_TM), 0, 0)),
            _resident((1, D_MODEL)),
            pl.BlockSpec((D_MODEL, FFN_TF), lambda s: (0, f_tile(s))),
            pl.BlockSpec((D_MODEL, FFN_TF), lambda s: (0, f_tile(s))),
            pl.BlockSpec((FFN_TF, D_MODEL), lambda s: (f_tile(s), 0))],
        out_specs=[part_spec(out_tiles, k) for k in range(len(out_rows))],
        out_shape=[jax.ShapeDtypeStruct((r, D_MODEL), F32) for r in out_rows],
        scratch_shapes=[pltpu.VMEM((nf, D_MODEL, FFN_TF), BF16), pltpu.VMEM((nf, D_MODEL, FFN_TF), BF16),
                        pltpu.VMEM((nf, FFN_TF, D_MODEL), BF16),
                        pltpu.VMEM((FFN_TM, D_MODEL), BF16), pltpu.VMEM((FFN_TM, D_MODEL), F32)],
        compiler_params=_params(1),
        name="ffn",
    )(*xs, mods, g.reshape(1, D_MODEL), w1, w3, w2)


def _group_norm64(x, bd, g):
    ss = _dot(x * x, bd)
    return x * lax.rsqrt(ss * (1.0 / DA_HEAD_DIM) + EPS) * g


def _rope(x, cos, sin_signed):
    first = (lax.broadcasted_iota(jnp.int32, x.shape, 1) % 32) < 16
    partner = jnp.where(first, pltpu.roll(x, LANES - 16, 1), pltpu.roll(x, 16, 1))
    return x * cos + partner * sin_signed


N_SEG = 9
GATE_LO = 4 * D_MODEL


def _seg_start(s):
    return s * D_MODEL + N_GATE_COLS * (s >= 4)


def _proj_kernel(*refs, rope):
    (x_ref, m_ref, g_ref, wt_ref, wgt_ref, bg_ref, bd_ref, gq_ref, gk_ref), refs = refs[:9], refs[9:]
    if rope:
        (cos_ref, sin_ref), refs = refs[:2], refs[2:]
    (mq_ref, mk_ref, mv_ref, so_ref, dq_ref, dk_ref, dv_ref, sgm_ref, sgd_ref, gates_ref,
     w_sc, wg_sc, hh_sc) = refs
    step = pl.program_id(0)
    norm = lambda: _modulated_norm(x_ref[...], g_ref[...], m_ref[3:4, :], m_ref[4:5, :]).astype(BF16)

    def gates(hh):
        y = _dot_nt(hh, wg_sc[...]) + bg_ref[...]
        col = lax.broadcasted_iota(jnp.int32, y.shape, 1)
        gates_ref[...] = jnp.where((col // ML_HEADS) % 2 == 1, _log_sigmoid(y), y)

    def qk_segment(y, g_head_ref, o_ref):
        bd = bd_ref[...]
        for c in range(D_MODEL // MXU_DIM):
            cols = slice(c * MXU_DIM, (c + 1) * MXU_DIM)
            z = _group_norm64(y[:, cols], bd, g_head_ref[...])
            if rope:
                z = jnp.concatenate(
                    [_rope(z[:, k * LANES:(k + 1) * LANES], cos_ref[...], sin_ref[...])
                     for k in range(MXU_DIM // LANES)], axis=1)
            o_ref[:, cols] = z.astype(o_ref.dtype)

    def cache_key_segment(hh):
        y_t = _dot_nt(w_sc[5], hh)
        z = y_t.reshape(D_MODEL // DA_HEAD_DIM, DA_HEAD_DIM, y_t.shape[1])
        ms = jnp.mean(z * z, axis=1, keepdims=True)
        gain = jnp.broadcast_to(gk_ref[...], z.shape[1:])
        dk_ref[...] = (z * lax.rsqrt(ms + EPS) * gain).reshape(y_t.shape)

    def segment(s, hh):
        if s in (0, 2):
            o_ref = {0: mq_ref, 2: mv_ref}[s]
            y_t = _dot_nt(w_sc[s], hh) * ((ML_HEAD_DIM ** -0.5) if s == 0 else 1.0)
            for j in range(PROJ_TM // CHUNK):
                o_ref[j] = y_t[:, j * CHUNK:(j + 1) * CHUNK].astype(o_ref.dtype)
            return
        if s == 5 and not rope:
            cache_key_segment(hh)
            return
        y = _dot_nt(hh, w_sc[s])
        if s == 4:
            qk_segment(y, gq_ref, dq_ref)
        elif s == 5:
            qk_segment(y, gk_ref, dk_ref)
        elif s in (3, 7, 8):
            {3: so_ref, 7: sgm_ref, 8: sgd_ref}[s][...] = _sigmoid(y)
        elif s == 6 and not rope:
            dv_ref[...] = y.reshape(y.shape[0], DA_HEADS, DA_V_DIM)
        else:
            o_ref = {1: mk_ref, 6: dv_ref}[s]
            o_ref[...] = y.astype(o_ref.dtype)

    for s in range(N_SEG):
        @pl.when(step == s)
        def _(s=s):
            w_sc[s] = wt_ref[...].astype(BF16)
            if s == 0:
                wg_sc[...] = jnp.zeros(wg_sc.shape, BF16)
                wg_sc[0:N_GATE_COLS, :] = wgt_ref[...].astype(BF16)
                hh_sc[...] = norm()
                gates(hh_sc[...])
            segment(s, hh_sc[...])

    @pl.when(step >= N_SEG)
    def _():
        hh = norm()
        gates(hh)
        for s in range(N_SEG):
            segment(s, hh)


def _proj(x, row0, rows, mods, mod_index, g, w_in_t, b_gate, bd, gq, gk, rope_tabs, seq_len):
    rope = rope_tabs is not None
    row_tile = lambda s: jnp.maximum(s - (N_SEG - 1), 0)
    row = lambda s: (row_tile(s), 0)
    tile = pl.BlockSpec((PROJ_TM, D_MODEL), row)
    elem = lambda n: (pl.Element(n), pl.Element(D_MODEL))
    seg_row = lambda s: pl.multiple_of(_seg_start(jnp.minimum(s, N_SEG - 1)), N_GATE_COLS)
    in_specs = [pl.BlockSpec((PROJ_TM, D_MODEL), lambda s: (row0 // PROJ_TM + row_tile(s), 0)),
                pl.BlockSpec((None, N_MOD, D_MODEL), lambda s: (mod_index(row0 + row_tile(s) * PROJ_TM), 0, 0)),
                _resident((1, D_MODEL)),
                pl.BlockSpec(elem(D_MODEL), lambda s: (seg_row(s), 0)),
                pl.BlockSpec(elem(N_GATE_COLS), lambda s: (GATE_LO, 0)),
                _resident((1, GATE_PAD)),
                _resident((MXU_DIM, MXU_DIM)),
                _resident((1, MXU_DIM)),
                _resident(gk.shape)]
    args = [x, mods, g.reshape(1, D_MODEL), w_in_t, w_in_t, b_gate, bd, gq, gk]
    if rope:
        tiles_per_seq = seq_len // PROJ_TM
        tab = pl.BlockSpec((PROJ_TM, LANES), lambda s: (row_tile(s) % tiles_per_seq, 0))
        in_specs += [tab, tab]
        args += list(rope_tabs)
    dtypes = [BF16, BF16, BF16, F32, BF16, BF16, BF16 if rope else F32, F32, F32]
    out_shape = [jax.ShapeDtypeStruct((rows, D_MODEL), dt) for dt in dtypes]
    out_shape.append(jax.ShapeDtypeStruct((rows, GATE_PAD), F32))
    out_specs = [tile] * 9 + [pl.BlockSpec((PROJ_TM, GATE_PAD), row)]
    slabs = PROJ_TM // CHUNK
    for j in (0, 2):
        out_shape[j] = jax.ShapeDtypeStruct((rows // CHUNK, D_MODEL, CHUNK), BF16)
        out_specs[j] = pl.BlockSpec((slabs, D_MODEL, CHUNK), lambda s: (row_tile(s), 0, 0))
    if not rope:
        assert seq_len == PROJ_TM
        out_shape[5] = jax.ShapeDtypeStruct((rows // seq_len, D_MODEL, seq_len), F32)
        out_specs[5] = pl.BlockSpec((None, D_MODEL, seq_len), lambda s: (row_tile(s), 0, 0))
        out_shape[6] = jax.ShapeDtypeStruct((rows, DA_HEADS, DA_V_DIM), F32)
        out_specs[6] = pl.BlockSpec((PROJ_TM, DA_HEADS, DA_V_DIM), lambda s: (row_tile(s), 0, 0))
    return pl.pallas_call(
        functools.partial(_proj_kernel, rope=rope),
        grid=(N_SEG - 1 + rows // PROJ_TM,),
        in_specs=in_specs,
        out_specs=out_specs,
        out_shape=out_shape,
        scratch_shapes=[pltpu.VMEM((N_SEG, D_MODEL, D_MODEL), BF16), pltpu.VMEM((GATE_PAD, D_MODEL), BF16),
                        pltpu.VMEM((PROJ_TM, D_MODEL), BF16)],
        compiler_params=_params(1),
        name="mixer_in_proj",
    )(*args)


def _lambda(lam_ref, lam_init):
    lam = lam_ref[...]
    s1 = jnp.sum(lam[0:1, :] * lam[1:2, :], axis=1, keepdims=True)
    s2 = jnp.sum(lam[2:3, :] * lam[3:4, :], axis=1, keepdims=True)
    return jnp.exp(s1) - jnp.exp(s2) + lam_init


def _attn_kernel(*refs, cached, seq, lam_init):
    if cached:
        q_ref, k_ref, v_ref, ck_ref, cv_ref, lam_ref, gs_ref, o_ref, kall_sc, vt_sc, s_sc = refs
    else:
        q_ref, k_ref, v_ref, lam_ref, gs_ref, o_ref, kall_sc, vt_sc, s_sc = refs
    def cache_order_kv(kt_ref, vr_ref, h, lo, n):
        kall_sc[h, lo:lo + n, :] = kt_ref[h * DA_V_DIM:(h + 1) * DA_V_DIM, :].T.astype(BF16)
        vt_sc[h, :, lo:lo + n] = vr_ref[pl.ds(h, n, stride=DA_HEADS), :].T.astype(BF16)

    @pl.when(pl.program_id(1) == 0)
    def _():
        for h in range(DA_HEADS):
            if cached:
                cols = slice(h * DA_V_DIM, (h + 1) * DA_V_DIM)
                kall_sc[h, 0:seq, :] = k_ref[:, cols].astype(BF16)
                vt_sc[h, :, 0:seq] = v_ref[:, cols].astype(F32).T.astype(BF16)
                cache_order_kv(ck_ref, cv_ref, h, seq, ck_ref.shape[1])
            else:
                cache_order_kv(k_ref, v_ref, h, 0, seq)

    lam = _lambda(lam_ref, lam_init)
    lane = lax.broadcasted_iota(jnp.int32, (1, DA_V_DIM), 1)
    comp_masks = [lane < DA_HEAD_DIM, lane >= DA_HEAD_DIM]
    tq = q_ref.shape[0]
    n_keys = kall_sc.shape[1]
    n_tiles = pl.cdiv(n_keys, ATT_TK_MAX)
    tk = n_keys // n_tiles

    def stacked_q(h):
        q = q_ref[:, h * DA_V_DIM:(h + 1) * DA_V_DIM].astype(BF16)
        return jnp.concatenate([jnp.where(m, q, jnp.zeros_like(q)) for m in comp_masks], axis=0)

    def score_tile(h, j, qq, m8):
        rows = slice(j * tk, (j + 1) * tk)
        st = _dot_nt(kall_sc[h, rows, :], qq)
        s_sc[h % 2, rows, :] = st
        t8 = jnp.max(st.reshape(tk // 8, 8, 2 * tq), axis=0)
        return t8 if m8 is None else jnp.maximum(m8, t8)

    def prob_tile(h, j, mx, d8, pv):
        rows = slice(j * tk, (j + 1) * tk)
        e = jnp.exp2(s_sc[h % 2, rows, :] - mx)
        s8 = jnp.sum(e.reshape(tk // 8, 8, 2 * tq), axis=0)
        p = _dot(vt_sc[h, :, rows], e)
        return (s8 if d8 is None else d8 + s8), (p if pv is None else pv + p)

    qq = stacked_q(0)
    m8 = None
    for j in range(n_tiles):
        m8 = score_tile(0, j, qq, m8)
    for h in range(DA_HEADS):
        mx = jnp.max(m8, axis=0, keepdims=True)
        if h + 1 < DA_HEADS:
            qq = stacked_q(h + 1)
        m8, d8, pv = None, None, None
        for j in range(n_tiles):
            if h + 1 < DA_HEADS:
                m8 = score_tile(h + 1, j, qq, m8)
            d8, pv = prob_tile(h, j, mx, d8, pv)
        pv = pv * (1.0 / jnp.sum(d8, axis=0, keepdims=True))
        out_t = pv[:, :tq] - lam * pv[:, tq:]
        out_t = out_t * lax.rsqrt(jnp.mean(out_t * out_t, axis=0, keepdims=True) + EPS)
        out = out_t.T * gs_ref[...]
        o_ref[:, h * DA_V_DIM:(h + 1) * DA_V_DIM] = (out * (1.0 - lam_init)).astype(o_ref.dtype)


def _attn(q, k, v, cache, lam_vecs, g_sub, batch, seq_len, lam_init):
    rows = q.shape[0]
    nq = seq_len // ATT_TQ
    q_spec = pl.BlockSpec((ATT_TQ, D_MODEL), lambda b, i: (b * nq + i, 0))
    cache_specs = lambda n: [pl.BlockSpec((None, D_MODEL, n), lambda b, i: (b, 0, 0)),
                             pl.BlockSpec((None, n * DA_HEADS, DA_V_DIM), lambda b, i: (b, 0, 0))]
    n_keys = seq_len
    if cache is not None:
        past = cache[0].shape[2]
        n_keys += past
        kv_spec = pl.BlockSpec((seq_len, D_MODEL), lambda b, i: (b, 0))
        in_specs = [q_spec, kv_spec, kv_spec] + cache_specs(past)
        args = [q, k, v] + list(cache)
    else:
        in_specs = [q_spec] + cache_specs(seq_len)
        args = [q, k, v.reshape(batch, seq_len * DA_HEADS, DA_V_DIM)]
    in_specs += [_resident((4, DA_HEAD_DIM)), _resident((1, DA_V_DIM))]
    args += [lam_vecs, g_sub.reshape(1, DA_V_DIM)]
    return pl.pallas_call(
        functools.partial(_attn_kernel, cached=cache is not None, seq=seq_len, lam_init=lam_init),
        grid=(batch, nq),
        in_specs=in_specs,
        out_specs=q_spec,
        out_shape=jax.ShapeDtypeStruct((rows, D_MODEL), BF16),
        scratch_shapes=[pltpu.VMEM((DA_HEADS, n_keys, DA_V_DIM), BF16),
                        pltpu.VMEM((DA_HEADS, DA_V_DIM, n_keys), BF16),
                        pltpu.VMEM((2, n_keys, 2 * ATT_TQ), F32)],
        compiler_params=_params(2),
        name="diff_attention",
    )(*args)


def _per_chain(fn, a, b):
    return jnp.stack([fn(a[i], b[i]) for i in range(a.shape[0])])


ML_EXT = 8


def _rows(x):
    return jnp.stack([x[i:i + 1, :] for i in range(x.shape[0])])


def _split3(x):
    hi = x.astype(BF16)
    r = x - hi.astype(F32)
    mid = r.astype(BF16)
    lo = (r - mid.astype(F32)).astype(BF16)
    return jnp.concatenate([hi, mid, lo], axis=1)


def _chunk_scan(x, reverse_rows, tri_prefix, tri_suffix):
    parts = _split3(x)
    return jnp.where(reverse_rows, _dot(parts, tri_suffix), _dot(parts, tri_prefix))


def _paired_value_matmul(v_t, sc):
    B, L, _ = sc.shape
    zero = jnp.zeros((L, L), sc.dtype)
    out = []
    for i in range(0, B, 2):
        lhs = jnp.concatenate([v_t[i], v_t[i + 1]], axis=1)
        rhs = jnp.concatenate([jnp.concatenate([sc[i], zero], axis=1),
                               jnp.concatenate([zero, sc[i + 1]], axis=1)], axis=0)
        both = _dot(lhs, rhs)
        out += [both[:, :L], both[:, L:]]
    return jnp.stack(out)


def _mlstm_step(k, q_t, v_t, ic, fc, CT, m, seen_t, reverse_rows):
    B, L = ic.shape
    d = k.shape[2]
    tri = lambda keep: jnp.concatenate([jnp.where(keep, 1.0, 0.0).astype(BF16)] * 3, axis=0)
    b2 = _chunk_scan(fc, reverse_rows, tri(seen_t[0]), tri(seen_t[B - 1]))
    u2 = ic - b2
    u_t = jnp.concatenate([u2, jnp.zeros((L - B, L), F32)], axis=0).T
    u_col = jnp.stack([jnp.broadcast_to(u_t[:, i:i + 1], (L, L)) for i in range(B)])
    b, i_g, f_g = _rows(b2), _rows(ic), _rows(fc)
    b_last = jnp.sum(f_g, axis=2, keepdims=True)

    log_d = jnp.where(seen_t, b + u_col, -jnp.inf)
    a = b + m
    m_t = jnp.maximum(a, jnp.max(log_d, axis=1, keepdims=True))
    dmat = jnp.exp(log_d - m_t)
    inter = jnp.exp(a - m_t)
    sc = _per_chain(_dot, k, q_t) * dmat
    cq = _per_chain(_dot, CT, q_t)
    num = _paired_value_matmul(v_t, sc.astype(BF16)) + inter * cq[:, :d, :]
    den = jnp.sum(sc, axis=1, keepdims=True) + inter * cq[:, d:d + 1, :]
    h_t = num * (1.0 / jnp.maximum(jnp.abs(den), jnp.exp(-m_t)))

    g = b_last - b + i_g
    m_new = jnp.maximum(b_last + m, jnp.max(g, axis=2, keepdims=True))
    w = jnp.exp(g - m_new)
    decay = jnp.exp(b_last + m - m_new)
    vw = jnp.concatenate([v_t.astype(F32), jnp.ones((B, ML_EXT, L), F32)], axis=1) * w
    CT_new = decay * CT + _per_chain(_dot, vw, k)
    return h_t, CT_new, m_new


def _mlstm_kernel(*refs, seq_len, has_state, emit_state):
    q_ref, k_ref, v_ref, so_ref, g_ref, gmh_ref = refs[:6]
    refs = refs[6:]
    if has_state:
        (c0_ref, n0_ref, m0_ref), refs = refs[:3], refs[3:]
    hm_ref, refs = refs[0], refs[1:]
    if emit_state:
        (c_out_ref, n_out_ref, m_out_ref), refs = refs[:3], refs[3:]
    ct_sc, m_sc, gr_sc, h_sc = refs

    nc = seq_len // CHUNK
    d_head = ML_HEAD_DIM
    chains = [(d, h) for d in range(2) for h in range(ML_HEADS)]
    n_chain = len(chains)
    for i, (d, h) in enumerate(chains):
        if has_state:
            ct_sc[i, 0:d_head, :] = c0_ref[d, h].T
            ct_sc[i, d_head:d_head + ML_EXT, :] = jnp.broadcast_to(n0_ref[d, h], (ML_EXT, d_head))
            m_sc[i] = m0_ref[d, h][:, 0:1]
        else:
            ct_sc[i] = jnp.zeros((d_head + ML_EXT, d_head), F32)
            m_sc[i] = jnp.zeros((1, 1), F32)
    for c in range(nc):
        gr_sc[c] = g_ref[c * CHUNK:(c + 1) * CHUNK, :].T

    s_idx = lax.broadcasted_iota(jnp.int32, (CHUNK, CHUNK), 0)
    t_idx = lax.broadcasted_iota(jnp.int32, (CHUNK, CHUNK), 1)
    seen_t = jnp.stack([s_idx <= t_idx] * ML_HEADS + [s_idx >= t_idx] * ML_HEADS)
    reverse_rows = lax.broadcasted_iota(jnp.int32, (n_chain, CHUNK), 0) >= ML_HEADS

    def step(c_fwd, c_bwd, rows_of):
        chunk_of = (c_fwd, c_bwd)
        g_fwd, g_bwd = gr_sc[c_fwd], gr_sc[c_bwd]
        lo = 2 * ML_HEADS
        ic = jnp.concatenate([g_fwd[0:ML_HEADS], g_bwd[lo:lo + ML_HEADS]], axis=0)
        fc = jnp.concatenate([g_fwd[ML_HEADS:lo], g_bwd[lo + ML_HEADS:2 * lo]], axis=0)
        head = [slice(h * d_head, (h + 1) * d_head) for _, h in chains]
        stack = lambda pick: jnp.stack([pick(i, chunk_of[d]) for i, (d, _) in enumerate(chains)])
        h_t, CT_new, m_new = _mlstm_step(
            stack(lambda i, c: k_ref[rows_of(c), head[i]]),
            stack(lambda i, c: q_ref[c, head[i], :]),
            stack(lambda i, c: v_ref[c, head[i], :]),
            ic, fc, ct_sc[...], m_sc[...], seen_t, reverse_rows)
        ct_sc[...] = CT_new
        m_sc[...] = m_new
        for i, (d, _) in enumerate(chains):
            h_sc[d, chunk_of[d], head[i], :] = h_t[i]

    if nc <= 2:
        for c in range(nc):
            step(c, nc - 1 - c, lambda cc: slice(cc * CHUNK, (cc + 1) * CHUNK))
    else:
        def body(c, carry):
            step(c, nc - 1 - c, lambda cc: pl.ds(pl.multiple_of(cc * CHUNK, CHUNK), CHUNK))
            return carry
        lax.fori_loop(0, nc, body, 0, unroll=2)

    for h in range(ML_HEADS):
        hcols = slice(h * d_head, (h + 1) * d_head)
        gain = jnp.broadcast_to(gmh_ref[h], (d_head, CHUNK))
        for c in range(nc):
            rows = slice(c * CHUNK, (c + 1) * CHUNK)
            hsum = h_sc[0, c, hcols, :] + h_sc[1, c, hcols, :]
            hn = hsum * lax.rsqrt(jnp.mean(hsum * hsum, axis=0, keepdims=True) + EPS) * gain
            hm_ref[rows, hcols] = (hn.T * so_ref[rows, hcols]).astype(hm_ref.dtype)
    if emit_state:
        for i, (d, h) in enumerate(chains):
            c_out_ref[d, h] = ct_sc[i, 0:d_head, :].T
            n_out_ref[d, h] = ct_sc[i, d_head:d_head + 1, :]
            m_out_ref[d, h] = jnp.broadcast_to(m_sc[i], (1, LANES))


def _mlstm(q_t, k, v_t, so, gates, g_mh, state, batch, seq_len, emit_state):
    rows = k.shape[0]
    d = ML_HEAD_DIM
    nc = seq_len // CHUNK
    tile = pl.BlockSpec((seq_len, D_MODEL), lambda b: (b, 0))
    tile_t = pl.BlockSpec((nc, D_MODEL, CHUNK), lambda b: (b, 0, 0))
    in_specs = [tile_t, tile, tile_t, tile,
                pl.BlockSpec((seq_len, GATE_PAD), lambda b: (b, 0)),
                _resident((ML_HEADS, d, 1))]
    args = [q_t, k, v_t, so, gates, g_mh.reshape(ML_HEADS, d, 1)]
    c_spec = pl.BlockSpec((None, 2, ML_HEADS, d, d), lambda b: (b, 0, 0, 0, 0))
    n_spec = pl.BlockSpec((None, 2, ML_HEADS, 1, d), lambda b: (b, 0, 0, 0, 0))
    m_spec = pl.BlockSpec((None, 2, ML_HEADS, 1, LANES), lambda b: (b, 0, 0, 0, 0))
    if state is not None:
        in_specs += [c_spec, n_spec, m_spec]
        args += list(state)
    out_specs = [tile]
    out_shape = [jax.ShapeDtypeStruct((rows, D_MODEL), BF16)]
    if emit_state:
        out_specs += [c_spec, n_spec, m_spec]
        out_shape += [jax.ShapeDtypeStruct((batch, 2, ML_HEADS, d, d), F32),
                      jax.ShapeDtypeStruct((batch, 2, ML_HEADS, 1, d), F32),
                      jax.ShapeDtypeStruct((batch, 2, ML_HEADS, 1, LANES), F32)]
    n_state = 2 * ML_HEADS
    return pl.pallas_call(
        functools.partial(_mlstm_kernel, seq_len=seq_len, has_state=state is not None, emit_state=emit_state),
        grid=(batch,),
        in_specs=in_specs,
        out_specs=out_specs,
        out_shape=out_shape,
        scratch_shapes=[pltpu.VMEM((n_state, d + ML_EXT, d), F32), pltpu.VMEM((n_state, 1, 1), F32),
                        pltpu.VMEM((nc, GATE_PAD, CHUNK), F32),
                        pltpu.VMEM((2, nc, D_MODEL, CHUNK), F32)],
        compiler_params=_params(1),
        name="mlstm",
    )(*args)


def _merge_kernel(*refs, tiles):
    n = len(tiles)
    (x_ref, m_ref), refs = refs[:2], refs[2:]
    parts = [refs[j * n:(j + 1) * n] for j in range(4)]
    wm_ref, wd_ref, wo_ref, o_ref, w_sc = refs[4 * n:]
    i = pl.program_id(0)

    @pl.when(i == 0)
    def _():
        for j, w_ref in enumerate((wm_ref, wd_ref, wo_ref)):
            w_sc[j] = w_ref[...].astype(BF16)

    def tile(hm_ref, att_ref, sgm_ref, sgd_ref):
        y = sgm_ref[...] * _dot(hm_ref[...], w_sc[0]) + sgd_ref[...] * _dot(att_ref[...], w_sc[1])
        o_ref[...] = x_ref[...] + m_ref[5:6, :] * _dot(y, w_sc[2])

    starts = np.cumsum((0,) + tiles).tolist()
    for k in range(n):
        pl.when((i >= starts[k]) & (i < starts[k + 1]))(functools.partial(tile, *[p[k] for p in parts]))


def _merge(x, mods, mod_index, hm, att, sgm, sgd, wm, wd, wo):
    rows = x.shape[0]
    tiles = tuple(h.shape[0] // MERGE_TM for h in hm)
    assert sum(tiles) * MERGE_TM == rows
    tile = pl.BlockSpec((MERGE_TM, D_MODEL), lambda i: (i, 0))

    def part_spec(k):
        start = sum(tiles[:k])
        return pl.BlockSpec((MERGE_TM, D_MODEL), lambda i: (jnp.clip(i - start, 0, tiles[k] - 1), 0))

    part_specs = [part_spec(k) for k in range(len(tiles))]
    w_spec = _resident((D_MODEL, D_MODEL))
    return pl.pallas_call(
        functools.partial(_merge_kernel, tiles=tiles),
        grid=(rows // MERGE_TM,),
        in_specs=[tile, pl.BlockSpec((None, N_MOD, D_MODEL), lambda i: (mod_index(i * MERGE_TM), 0, 0))]
                 + part_specs * 4 + [w_spec, w_spec, w_spec],
        out_specs=tile,
        out_shape=jax.ShapeDtypeStruct((rows, D_MODEL), F32),
        scratch_shapes=[pltpu.VMEM((3, D_MODEL, D_MODEL), BF16)],
        compiler_params=_params(1),
        name="branch_merge",
    )(x, mods, *hm, *att, *sgm, *sgd, wm, wd, wo)


def _rope_tables(seq_len):
    lane = np.arange(LANES)
    r = lane % 32
    freqs = np.power(np.float32(ROPE_BASE), -(r % 16).astype(np.float32) / np.float32(16.0))
    tok = np.arange(seq_len)
    pos = np.where((lane % 64 < 32)[None, :], (tok // GRID_W)[:, None], (tok % GRID_W)[:, None]).astype(np.float32)
    ang = pos * freqs[None, :]
    sign = np.where(r < 16, -1.0, 1.0).astype(np.float32)
    return jnp.asarray(np.cos(ang), F32), jnp.asarray(np.sin(ang) * sign[None, :], F32)


def _mixer_branches(x, row0, mods, mod_index, w, batch, seq_len, ctx, lam_init):
    rope_tabs = None if ctx is None else _rope_tables(seq_len)
    mq, mk, mv, so, dq, dk, dv, sgm, sgd, gates = _proj(
        x, row0, batch * seq_len, mods, mod_index, w["g_norm"][1], w["w_in_t"], w["b_gate"], w["bd"], w["g_qn"],
        w["g_kn_col"] if ctx is None else w["g_kn"], rope_tabs, seq_len)
    cache = None if ctx is None else (ctx[0], ctx[1])
    att = _attn(dq, dk, dv, cache, w["lam"], w["g_sub"], batch, seq_len, lam_init)
    state = None if ctx is None else ctx[2]
    res = _mlstm(mq, mk, mv, so, gates, w["g_mh"], state, batch, seq_len, emit_state=ctx is None)
    return res[0], att, sgm, sgd, dk, dv, res[1:]


def kernel(x_prompt, x_sample, c, cache_k, cache_v, state_C, state_n, state_m, c_ctx, w_ada, b_ada, g_norm, ffn1_w1, ffn1_w3, ffn1_w2, ffn2_w1, ffn2_w3, ffn2_w2, w_in, b_gate, g_qn, g_kn, lam_q1, lam_k1, lam_q2, lam_k2, g_sub, g_mh, w_br_m, w_br_d, w_out):
    depth = w_ada.shape[0]
    assert depth == 1
    l = 0
    bp, tp, _ = x_prompt.shape
    bs, ts, _ = x_sample.shape
    past = cache_k.shape[2]
    lam_init = 0.8 - 0.6 * math.exp(-0.3 * l)

    cvecs = jnp.concatenate([c_ctx[None, :], c, jnp.zeros((8 - 1 - bs, D_MODEL), F32)], axis=0)
    mods = _mods(cvecs, w_ada[l], b_ada[l]).reshape(8, N_MOD, D_MODEL)

    group = jnp.arange(MXU_DIM) // DA_HEAD_DIM
    w = dict(
        g_norm=g_norm[l],
        ffn1_w1=ffn1_w1[l], ffn1_w3=ffn1_w3[l], ffn1_w2=ffn1_w2[l],
        ffn2_w1=ffn2_w1[l], ffn2_w3=ffn2_w3[l], ffn2_w2=ffn2_w2[l],
        w_in_t=w_in[l].T,
        b_gate=jnp.pad(b_gate[l], (0, GATE_PAD - N_GATE_COLS)).reshape(1, GATE_PAD),
        bd=(group[:, None] == group[None, :]).astype(BF16),
        g_qn=jnp.tile(g_qn[l] * QK_LOG2_SCALE, MXU_DIM // DA_HEAD_DIM).reshape(1, MXU_DIM),
        g_kn=jnp.tile(g_kn[l], MXU_DIM // DA_HEAD_DIM).reshape(1, MXU_DIM),
        g_kn_col=g_kn[l].reshape(DA_HEAD_DIM, 1),
        lam=jnp.stack([lam_q1[l], lam_k1[l], lam_q2[l], lam_k2[l]]),
        g_sub=g_sub[l], g_mh=g_mh[l],
        w_br_m=w_br_m[l], w_br_d=w_br_d[l], w_out=w_out[l],
    )

    n_ctx, n_lat = bp * tp, bs * ts
    mod_index = lambda r: jnp.where(r < n_ctx, 0, 1 + (r - n_ctx) // ts)
    ffn = lambda xs, out_rows, j, base: _ffn(xs, out_rows, mods, mod_index, w["g_norm"][j], w[f"ffn{1 + j // 2}_w1"],
                                             w[f"ffn{1 + j // 2}_w3"], w[f"ffn{1 + j // 2}_w2"], base=base)
    (x1,) = ffn((x_prompt.reshape(n_ctx, D_MODEL), x_sample.reshape(n_lat, D_MODEL)), (n_ctx + n_lat,), 0, 0)

    hm_p, att_p, sgm_p, sgd_p, new_k_t, new_v, (new_c, new_n, new_m) = _mixer_branches(
        x1, 0, mods, mod_index, w, bp, tp, None, lam_init)

    ctx = (cache_k[:, l].transpose(0, 2, 3, 4, 1).reshape(bs, D_MODEL, past),
           cache_v[:, l].reshape(bs, past * DA_HEADS, DA_V_DIM),
           (state_C[:, l], state_n[:, l].reshape(bs, 2, ML_HEADS, 1, ML_HEAD_DIM),
            jnp.broadcast_to(state_m[:, l][..., None, None], (bs, 2, ML_HEADS, 1, LANES))))
    hm_s, att_s, sgm_s, sgd_s, _, _, _ = _mixer_branches(x1, n_ctx, mods, mod_index, w, bs, ts, ctx, lam_init)

    x2 = _merge(x1, mods, mod_index, (hm_p, hm_s), (att_p, att_s), (sgm_p, sgm_s), (sgd_p, sgd_s),
                w["w_br_m"], w["w_br_d"], w["w_out"])
    xp, xs = ffn((x2,), (n_ctx, n_lat), 2, 6)

    return (xp.reshape(bp, tp, D_MODEL), xs.reshape(bs, ts, D_MODEL),
            new_k_t.reshape(bp, DA_HEADS, 2, DA_HEAD_DIM, tp).transpose(0, 4, 1, 2, 3)[:, None],
            new_v.reshape(bp, 1, tp, DA_HEADS, DA_V_DIM),
            new_c[:, None], new_n.reshape(bp, 1, 2, ML_HEADS, ML_HEAD_DIM),
            new_m[..., 0, 0][:, None])
```

```python
import functools
import math

import jax
import jax.numpy as jnp
import numpy as np
from jax import lax
from jax.experimental import pallas as pl
from jax.experimental.pallas import tpu as pltpu

F32 = jnp.float32
BF16 = jnp.bfloat16

D_MODEL = 1024
D_FF = 2816
N_MOD = 9
GRID_W = 64
ML_HEADS = 4
ML_HEAD_DIM = 256
DA_HEADS = 8
DA_HEAD_DIM = 64
DA_V_DIM = 128
N_GATE_COLS = 16
CHUNK = 128
ROPE_BASE = 10000.0
QK_LOG2_SCALE = DA_HEAD_DIM ** -0.5 * math.log2(math.e)
EPS = 1e-6

LANES = 128
GATE_PAD = LANES
MXU_DIM = 256
VMEM_LIMIT = 56 * 1024 * 1024

MODS_K_SPLIT = 4
FFN_TM = 512
FFN_TF = 256
PROJ_TM = 256
MERGE_TM = 512
ATT_TQ = 256
ATT_TK_MAX = 768


def _params(n_axes):
    return pltpu.CompilerParams(dimension_semantics=("arbitrary",) * n_axes,
                                vmem_limit_bytes=VMEM_LIMIT)


def _dot(a, b):
    return jnp.dot(a.astype(BF16), b.astype(BF16), preferred_element_type=F32)


def _dot_nt(a, b):
    return lax.dot_general(a.astype(BF16), b.astype(BF16), (((1,), (1,)), ((), ())),
                           preferred_element_type=F32)


def _dot_tn(a, b):
    return lax.dot_general(a.astype(BF16), b.astype(BF16), (((0,), (0,)), ((), ())),
                           preferred_element_type=F32)


def _sigmoid(x):
    return 1.0 / (1.0 + jnp.exp(-x))


def _log_sigmoid(x):
    return jnp.minimum(x, 0.0) - jnp.log1p(jnp.exp(-jnp.abs(x)))


def _modulated_norm(x, g, shift, scale):
    y = x * lax.rsqrt(jnp.mean(x * x, axis=-1, keepdims=True) + EPS) * g
    return y * (1.0 + scale) + shift


def _resident(shape):
    return pl.BlockSpec(shape, lambda *_: (0,) * len(shape), pipeline_mode=pl.Buffered(1))


def _mods_kernel(c_ref, *refs):
    *w_refs, b_ref, o_ref = refs
    c = c_ref[...]
    s = c * _sigmoid(c)
    kb = D_MODEL // len(w_refs)
    acc = b_ref[...]
    for j, w_ref in enumerate(w_refs):
        acc = acc + _dot(s[:, j * kb:(j + 1) * kb], w_ref[...])
    o_ref[...] = acc


def _mods(cvecs, w_ada, b_ada):
    n = N_MOD * D_MODEL
    tn = D_MODEL
    kb = D_MODEL // MODS_K_SPLIT
    w_specs = [pl.BlockSpec((kb, tn), lambda j, r=r: (r, j)) for r in range(MODS_K_SPLIT)]
    return pl.pallas_call(
        _mods_kernel,
        grid=(n // tn,),
        in_specs=[pl.BlockSpec((8, D_MODEL), lambda j: (0, 0))] + w_specs + [pl.BlockSpec((1, tn), lambda j: (0, j))],
        out_specs=pl.BlockSpec((8, tn), lambda j: (0, j)),
        out_shape=jax.ShapeDtypeStruct((8, n), F32),
        compiler_params=_params(1),
        name="adaln_mods",
    )(cvecs, *([w_ada] * MODS_K_SPLIT), b_ada.reshape(1, n))


def _ffn_kernel(*refs, base, in_tiles, out_tiles, side_blocks):
    n_in, n_out = len(in_tiles), len(out_tiles)
    x_refs, refs = refs[:n_in], refs[n_in:]
    (m_ref, g_ref, w1_ref, w3_ref, w2_ref), refs = refs[:5], refs[5:]
    if side_blocks:
        side_in_ref, refs = refs[0], refs[1:]
        side_out_ref, refs = refs[n_out], refs[:n_out] + refs[n_out + 1:]
    o_refs, (w1_sc, w3_sc, w2_sc, hh_sc, acc_sc) = refs[:n_out], refs[n_out:]
    g = pl.program_id(0)
    nf = D_FF // FFN_TF
    row_tile = g - (nf - 1)
    norm = lambda x: _modulated_norm(x, g_ref[...], m_ref[base:base + 1, :], m_ref[base + 1:base + 2, :]).astype(BF16)
    finish = lambda x, acc: x + 0.5 * m_ref[base + 2:base + 3, :] * acc

    def tile(hh, f):
        a = _dot(hh, w1_sc[f])
        b = _dot(hh, w3_sc[f])
        return _dot(a * _sigmoid(a) * b, w2_sc[f])

    @pl.when(g < nf)
    def _():
        w1_sc[g] = w1_ref[...].astype(BF16)
        w3_sc[g] = w3_ref[...].astype(BF16)
        w2_sc[g] = w2_ref[...].astype(BF16)

        @pl.when(g == 0)
        def _():
            hh_sc[...] = norm(x_refs[0][...])
            acc_sc[...] = jnp.zeros(acc_sc.shape, F32)

        acc_sc[...] += tile(hh_sc[...], g)

        @pl.when(g == nf - 1)
        def _():
            o_refs[0][...] = finish(x_refs[0][...], acc_sc[...])

    if side_blocks:
        @pl.when((g >= nf) & (g - nf < side_blocks))
        def _():
            side_out_ref[...] = side_in_ref[...].astype(BF16)

    def full_tile(x_ref, o_ref):
        x = x_ref[...]
        hh = norm(x)
        acc = jnp.zeros(x.shape, F32)
        for f in range(nf):
            acc = acc + tile(hh, f)
        o_ref[...] = finish(x, acc)

    bounds = sorted(set(np.cumsum((0,) + in_tiles).tolist()) | set(np.cumsum((0,) + out_tiles).tolist()))
    for lo, hi in zip(bounds[:-1], bounds[1:]):
        k_in = int(np.searchsorted(np.cumsum(in_tiles), lo, side="right"))
        k_out = int(np.searchsorted(np.cumsum(out_tiles), lo, side="right"))
        pl.when((g >= nf) & (row_tile >= lo) & (row_tile < hi))(
            functools.partial(full_tile, x_refs[k_in], o_refs[k_out]))


def _ffn(xs, out_rows, mods, mod_index, g, w1, w3, w2, base, side=None):
    nf = D_FF // FFN_TF
    in_tiles = tuple(x.shape[0] // FFN_TM for x in xs)
    out_tiles = tuple(r // FFN_TM for r in out_rows)
    assert sum(in_tiles) == sum(out_tiles)
    side_in_specs, side_out_specs, side_out_shape, side_args, side_blocks = [], [], [], [], 0
    if side is not None:
        side_w, side_blocks, side_start = side
        assert side_blocks <= sum(in_tiles) - 1
        blk = lambda s: jnp.clip(s - nf, 0, side_blocks - 1)
        side_in_specs = [pl.BlockSpec((pl.Element(D_MODEL), pl.Element(D_MODEL)), lambda s: (side_start(blk(s)), 0))]
        side_out_specs = [pl.BlockSpec((None, D_MODEL, D_MODEL), lambda s: (blk(s), 0, 0))]
        side_out_shape = [jax.ShapeDtypeStruct((side_blocks, D_MODEL, D_MODEL), BF16)]
        side_args = [side_w]
    row_tile = lambda s: jnp.maximum(s - (nf - 1), 0)
    f_tile = lambda s: jnp.minimum(s, nf - 1)

    def part_spec(tiles, k):
        start = sum(tiles[:k])
        return pl.BlockSpec((FFN_TM, D_MODEL), lambda s: (jnp.clip(row_tile(s) - start, 0, tiles[k] - 1), 0))

    return pl.pallas_call(
        functools.partial(_ffn_kernel, base=base, in_tiles=in_tiles, out_tiles=out_tiles, side_blocks=side_blocks),
        grid=(nf - 1 + sum(in_tiles),),
        in_specs=[part_spec(in_tiles, k) for k in range(len(xs))] + [
            pl.BlockSpec((None, N_MOD, D_MODEL), lambda s: (mod_index(row_tile(s) * FFN_TM), 0, 0)),
            _resident((1, D_MODEL)),
            pl.BlockSpec((D_MODEL, FFN_TF), lambda s: (0, f_tile(s))),
            pl.BlockSpec((D_MODEL, FFN_TF), lambda s: (0, f_tile(s))),
            pl.BlockSpec((FFN_TF, D_MODEL), lambda s: (f_tile(s), 0))] + side_in_specs,
        out_specs=[part_spec(out_tiles, k) for k in range(len(out_rows))] + side_out_specs,
        out_shape=[jax.ShapeDtypeStruct((r, D_MODEL), F32) for r in out_rows] + side_out_shape,
        scratch_shapes=[pltpu.VMEM((nf, D_MODEL, FFN_TF), BF16), pltpu.VMEM((nf, D_MODEL, FFN_TF), BF16),
                        pltpu.VMEM((nf, FFN_TF, D_MODEL), BF16),
                        pltpu.VMEM((FFN_TM, D_MODEL), BF16), pltpu.VMEM((FFN_TM, D_MODEL), F32)],
        compiler_params=_params(1),
        name="ffn",
    )(*xs, mods, g.reshape(1, D_MODEL), w1, w3, w2, *side_args)


def _group_norm64(x, bd, g):
    ss = _dot(x * x, bd)
    return x * lax.rsqrt(ss * (1.0 / DA_HEAD_DIM) + EPS) * g


def _rope(x, cos, sin_signed):
    first = (lax.broadcasted_iota(jnp.int32, x.shape, 1) % 32) < 16
    partner = jnp.where(first, pltpu.roll(x, LANES - 16, 1), pltpu.roll(x, 16, 1))
    return x * cos + partner * sin_signed


N_SEG = 9
GATE_LO = 4 * D_MODEL


def _seg_start(s):
    return s * D_MODEL + N_GATE_COLS * (s >= 4)


def _proj_kernel(*refs, rope):
    (x_ref, m_ref, g_ref, wt_ref, wgt_ref, bg_ref, bd_ref, gq_ref, gk_ref), refs = refs[:9], refs[9:]
    if rope:
        (cos_ref, sin_ref), refs = refs[:2], refs[2:]
    (mq_ref, mk_ref, mv_ref, so_ref, dq_ref, dk_ref, dv_ref, sgm_ref, sgd_ref, gates_ref,
     w_sc, wg_sc, hh_sc) = refs
    step = pl.program_id(0)
    norm = lambda: _modulated_norm(x_ref[...], g_ref[...], m_ref[3:4, :], m_ref[4:5, :]).astype(BF16)

    def gates(hh):
        y = _dot_nt(hh, wg_sc[...]) + bg_ref[...]
        col = lax.broadcasted_iota(jnp.int32, y.shape, 1)
        gates_ref[...] = jnp.where((col // ML_HEADS) % 2 == 1, _log_sigmoid(y), y)

    def qk_segment(y, g_head_ref, o_ref):
        bd = bd_ref[...]
        for c in range(D_MODEL // MXU_DIM):
            cols = slice(c * MXU_DIM, (c + 1) * MXU_DIM)
            z = _group_norm64(y[:, cols], bd, g_head_ref[...])
            if rope:
                z = jnp.concatenate(
                    [_rope(z[:, k * LANES:(k + 1) * LANES], cos_ref[...], sin_ref[...])
                     for k in range(MXU_DIM // LANES)], axis=1)
            o_ref[:, cols] = z.astype(o_ref.dtype)

    def cache_key_segment(hh):
        y_t = _dot_nt(w_sc[5], hh)
        z = y_t.reshape(D_MODEL // DA_HEAD_DIM, DA_HEAD_DIM, y_t.shape[1])
        ms = jnp.mean(z * z, axis=1, keepdims=True)
        gain = jnp.broadcast_to(gk_ref[...], z.shape[1:])
        dk_ref[...] = (z * lax.rsqrt(ms + EPS) * gain).reshape(y_t.shape)

    def segment(s, hh):
        if s in (0, 2):
            o_ref = {0: mq_ref, 2: mv_ref}[s]
            y_t = _dot_nt(w_sc[s], hh) * ((ML_HEAD_DIM ** -0.5) if s == 0 else 1.0)
            for j in range(PROJ_TM // CHUNK):
                o_ref[j] = y_t[:, j * CHUNK:(j + 1) * CHUNK].astype(o_ref.dtype)
            return
        if s == 5 and not rope:
            cache_key_segment(hh)
            return
        y = _dot_nt(hh, w_sc[s])
        if s == 4:
            qk_segment(y, gq_ref, dq_ref)
        elif s == 5:
            qk_segment(y, gk_ref, dk_ref)
        elif s in (3, 7, 8):
            {3: so_ref, 7: sgm_ref, 8: sgd_ref}[s][...] = _sigmoid(y)
        elif s == 6 and not rope:
            dv_ref[...] = y.reshape(y.shape[0], DA_HEADS, DA_V_DIM)
        else:
            o_ref = {1: mk_ref, 6: dv_ref}[s]
            o_ref[...] = y.astype(o_ref.dtype)

    for s in range(N_SEG):
        @pl.when(step == s)
        def _(s=s):
            w_sc[s] = wt_ref[...]
            if s == 0:
                wg_sc[...] = jnp.zeros(wg_sc.shape, BF16)
                wg_sc[0:N_GATE_COLS, :] = wgt_ref[0:N_GATE_COLS, :]
                hh_sc[...] = norm()
                gates(hh_sc[...])
            segment(s, hh_sc[...])

    @pl.when(step >= N_SEG)
    def _():
        hh = norm()
        gates(hh)
        for s in range(N_SEG):
            segment(s, hh)


def _proj(x, row0, rows, mods, mod_index, g, w_seg, b_gate, bd, gq, gk, rope_tabs, seq_len):
    rope = rope_tabs is not None
    row_tile = lambda s: jnp.maximum(s - (N_SEG - 1), 0)
    row = lambda s: (row_tile(s), 0)
    tile = pl.BlockSpec((PROJ_TM, D_MODEL), row)
    w_block = (None, D_MODEL, D_MODEL)
    in_specs = [pl.BlockSpec((PROJ_TM, D_MODEL), lambda s: (row0 // PROJ_TM + row_tile(s), 0)),
                pl.BlockSpec((None, N_MOD, D_MODEL), lambda s: (mod_index(row0 + row_tile(s) * PROJ_TM), 0, 0)),
                _resident((1, D_MODEL)),
                pl.BlockSpec(w_block, lambda s: (jnp.minimum(s, N_SEG - 1), 0, 0)),
                pl.BlockSpec(w_block, lambda s: (N_SEG, 0, 0), pipeline_mode=pl.Buffered(1)),
                _resident((1, GATE_PAD)),
                _resident((MXU_DIM, MXU_DIM)),
                _resident((1, MXU_DIM)),
                _resident(gk.shape)]
    args = [x, mods, g.reshape(1, D_MODEL), w_seg, w_seg, b_gate, bd, gq, gk]
    if rope:
        tiles_per_seq = seq_len // PROJ_TM
        tab = pl.BlockSpec((PROJ_TM, LANES), lambda s: (row_tile(s) % tiles_per_seq, 0))
        in_specs += [tab, tab]
        args += list(rope_tabs)
    dtypes = [BF16, BF16, BF16, F32, BF16, BF16, BF16 if rope else F32, F32, F32]
    out_shape = [jax.ShapeDtypeStruct((rows, D_MODEL), dt) for dt in dtypes]
    out_shape.append(jax.ShapeDtypeStruct((rows, GATE_PAD), F32))
    out_specs = [tile] * 9 + [pl.BlockSpec((PROJ_TM, GATE_PAD), row)]
    slabs = PROJ_TM // CHUNK
    for j in (0, 2):
        out_shape[j] = jax.ShapeDtypeStruct((rows // CHUNK, D_MODEL, CHUNK), BF16)
        out_specs[j] = pl.BlockSpec((slabs, D_MODEL, CHUNK), lambda s: (row_tile(s), 0, 0))
    if not rope:
        assert seq_len == PROJ_TM
        out_shape[5] = jax.ShapeDtypeStruct((rows // seq_len, D_MODEL, seq_len), F32)
        out_specs[5] = pl.BlockSpec((None, D_MODEL, seq_len), lambda s: (row_tile(s), 0, 0))
        out_shape[6] = jax.ShapeDtypeStruct((rows, DA_HEADS, DA_V_DIM), F32)
        out_specs[6] = pl.BlockSpec((PROJ_TM, DA_HEADS, DA_V_DIM), lambda s: (row_tile(s), 0, 0))
    return pl.pallas_call(
        functools.partial(_proj_kernel, rope=rope),
        grid=(N_SEG - 1 + rows // PROJ_TM,),
        in_specs=in_specs,
        out_specs=out_specs,
        out_shape=out_shape,
        scratch_shapes=[pltpu.VMEM((N_SEG, D_MODEL, D_MODEL), BF16), pltpu.VMEM((GATE_PAD, D_MODEL), BF16),
                        pltpu.VMEM((PROJ_TM, D_MODEL), BF16)],
        compiler_params=_params(1),
        name="mixer_in_proj",
    )(*args)


def _lambda(lam_ref, lam_init):
    lam = lam_ref[...]
    s1 = jnp.sum(lam[0:1, :] * lam[1:2, :], axis=1, keepdims=True)
    s2 = jnp.sum(lam[2:3, :] * lam[3:4, :], axis=1, keepdims=True)
    return jnp.exp(s1) - jnp.exp(s2) + lam_init


def _attn_kernel(*refs, cached, seq, lam_init):
    if cached:
        q_ref, k_ref, v_ref, ck_ref, cv_ref, lam_ref, gs_ref, o_ref, kall_sc, vt_sc, s_sc = refs
    else:
        q_ref, k_ref, v_ref, lam_ref, gs_ref, o_ref, kall_sc, vt_sc, s_sc = refs
    def cache_order_kv(kt_ref, vr_ref, h, lo, n):
        kall_sc[h, lo:lo + n, :] = kt_ref[h * DA_V_DIM:(h + 1) * DA_V_DIM, :].T.astype(BF16)
        vt_sc[h, :, lo:lo + n] = vr_ref[pl.ds(h, n, stride=DA_HEADS), :].T.astype(BF16)

    @pl.when(pl.program_id(1) == 0)
    def _():
        for h in range(DA_HEADS):
            if cached:
                cols = slice(h * DA_V_DIM, (h + 1) * DA_V_DIM)
                kall_sc[h, 0:seq, :] = k_ref[:, cols].astype(BF16)
                vt_sc[h, :, 0:seq] = v_ref[:, cols].astype(F32).T.astype(BF16)
                cache_order_kv(ck_ref, cv_ref, h, seq, ck_ref.shape[1])
            else:
                cache_order_kv(k_ref, v_ref, h, 0, seq)

    lam = _lambda(lam_ref, lam_init)
    lane = lax.broadcasted_iota(jnp.int32, (1, DA_V_DIM), 1)
    comp_masks = [lane < DA_HEAD_DIM, lane >= DA_HEAD_DIM]
    tq = q_ref.shape[0]
    n_keys = kall_sc.shape[1]
    n_tiles = pl.cdiv(n_keys, ATT_TK_MAX)
    tk = n_keys // n_tiles

    def stacked_q(h):
        q = q_ref[:, h * DA_V_DIM:(h + 1) * DA_V_DIM].astype(BF16)
        return jnp.concatenate([jnp.where(m, q, jnp.zeros_like(q)) for m in comp_masks], axis=0)

    def score_tile(h, j, qq, m8):
        rows = slice(j * tk, (j + 1) * tk)
        st = _dot_nt(kall_sc[h, rows, :], qq)
        s_sc[h % 2, rows, :] = st
        t8 = jnp.max(st.reshape(tk // 8, 8, 2 * tq), axis=0)
        return t8 if m8 is None else jnp.maximum(m8, t8)

    def prob_tile(h, j, mx, d8, pv):
        rows = slice(j * tk, (j + 1) * tk)
        e = jnp.exp2(s_sc[h % 2, rows, :] - mx)
        s8 = jnp.sum(e.reshape(tk // 8, 8, 2 * tq), axis=0)
        p = _dot(vt_sc[h, :, rows], e)
        return (s8 if d8 is None else d8 + s8), (p if pv is None else pv + p)

    qq = stacked_q(0)
    m8 = None
    for j in range(n_tiles):
        m8 = score_tile(0, j, qq, m8)
    for h in range(DA_HEADS):
        mx = jnp.max(m8, axis=0, keepdims=True)
        if h + 1 < DA_HEADS:
            qq = stacked_q(h + 1)
        m8, d8, pv = None, None, None
        for j in range(n_tiles):
            if h + 1 < DA_HEADS:
                m8 = score_tile(h + 1, j, qq, m8)
            d8, pv = prob_tile(h, j, mx, d8, pv)
        pv = pv * (1.0 / jnp.sum(d8, axis=0, keepdims=True))
        out_t = pv[:, :tq] - lam * pv[:, tq:]
        out_t = out_t * lax.rsqrt(jnp.mean(out_t * out_t, axis=0, keepdims=True) + EPS)
        out = out_t.T * gs_ref[...]
        o_ref[:, h * DA_V_DIM:(h + 1) * DA_V_DIM] = (out * (1.0 - lam_init)).astype(o_ref.dtype)


def _attn(q, k, v, cache, lam_vecs, g_sub, batch, seq_len, lam_init):
    rows = q.shape[0]
    nq = seq_len // ATT_TQ
    q_spec = pl.BlockSpec((ATT_TQ, D_MODEL), lambda b, i: (b * nq + i, 0))
    cache_specs = lambda n: [pl.BlockSpec((None, D_MODEL, n), lambda b, i: (b, 0, 0)),
                             pl.BlockSpec((None, n * DA_HEADS, DA_V_DIM), lambda b, i: (b, 0, 0))]
    n_keys = seq_len
    if cache is not None:
        past = cache[0].shape[2]
        n_keys += past
        kv_spec = pl.BlockSpec((seq_len, D_MODEL), lambda b, i: (b, 0))
        in_specs = [q_spec, kv_spec, kv_spec] + cache_specs(past)
        args = [q, k, v] + list(cache)
    else:
        in_specs = [q_spec] + cache_specs(seq_len)
        args = [q, k, v.reshape(batch, seq_len * DA_HEADS, DA_V_DIM)]
    in_specs += [_resident((4, DA_HEAD_DIM)), _resident((1, DA_V_DIM))]
    args += [lam_vecs, g_sub.reshape(1, DA_V_DIM)]
    return pl.pallas_call(
        functools.partial(_attn_kernel, cached=cache is not None, seq=seq_len, lam_init=lam_init),
        grid=(batch, nq),
        in_specs=in_specs,
        out_specs=q_spec,
        out_shape=jax.ShapeDtypeStruct((rows, D_MODEL), BF16),
        scratch_shapes=[pltpu.VMEM((DA_HEADS, n_keys, DA_V_DIM), BF16),
                        pltpu.VMEM((DA_HEADS, DA_V_DIM, n_keys), BF16),
                        pltpu.VMEM((2, n_keys, 2 * ATT_TQ), F32)],
        compiler_params=_params(2),
        name="diff_attention",
    )(*args)


def _per_chain(fn, a, b):
    return jnp.stack([fn(a[i], b[i]) for i in range(a.shape[0])])


ML_EXT = 8


def _rows(x):
    return jnp.stack([x[i:i + 1, :] for i in range(x.shape[0])])


def _split3(x):
    hi = x.astype(BF16)
    r = x - hi.astype(F32)
    mid = r.astype(BF16)
    lo = (r - mid.astype(F32)).astype(BF16)
    return jnp.concatenate([hi, mid, lo], axis=1)


def _chunk_scan(x, reverse_rows, tri_prefix, tri_suffix):
    parts = _split3(x)
    return jnp.where(reverse_rows, _dot(parts, tri_suffix), _dot(parts, tri_prefix))


def _paired_value_matmul(v_t, sc):
    B, L, _ = sc.shape
    zero = jnp.zeros((L, L), sc.dtype)
    out = []
    for i in range(0, B, 2):
        lhs = jnp.concatenate([v_t[i], v_t[i + 1]], axis=1)
        rhs = jnp.concatenate([jnp.concatenate([sc[i], zero], axis=1),
                               jnp.concatenate([zero, sc[i + 1]], axis=1)], axis=0)
        both = _dot(lhs, rhs)
        out += [both[:, :L], both[:, L:]]
    return jnp.stack(out)


def _mlstm_step(k, q_t, v_t, ic, fc, CT, m, seen_t, reverse_rows):
    B, L = ic.shape
    d = k.shape[2]
    tri = lambda keep: jnp.concatenate([jnp.where(keep, 1.0, 0.0).astype(BF16)] * 3, axis=0)
    b2 = _chunk_scan(fc, reverse_rows, tri(seen_t[0]), tri(seen_t[B - 1]))
    u2 = ic - b2
    u_t = jnp.concatenate([u2, jnp.zeros((L - B, L), F32)], axis=0).T
    u_col = jnp.stack([jnp.broadcast_to(u_t[:, i:i + 1], (L, L)) for i in range(B)])
    b, i_g, f_g = _rows(b2), _rows(ic), _rows(fc)
    b_last = jnp.sum(f_g, axis=2, keepdims=True)

    log_d = jnp.where(seen_t, b + u_col, -jnp.inf)
    a = b + m
    m_t = jnp.maximum(a, jnp.max(log_d, axis=1, keepdims=True))
    dmat = jnp.exp(log_d - m_t)
    inter = jnp.exp(a - m_t)
    sc = _per_chain(_dot, k, q_t) * dmat
    cq = _per_chain(_dot, CT, q_t)
    num = _paired_value_matmul(v_t, sc.astype(BF16)) + inter * cq[:, :d, :]
    den = jnp.sum(sc, axis=1, keepdims=True) + inter * cq[:, d:d + 1, :]
    h_t = num * (1.0 / jnp.maximum(jnp.abs(den), jnp.exp(-m_t)))

    g = b_last - b + i_g
    m_new = jnp.maximum(b_last + m, jnp.max(g, axis=2, keepdims=True))
    w = jnp.exp(g - m_new)
    decay = jnp.exp(b_last + m - m_new)
    vw = jnp.concatenate([v_t.astype(F32), jnp.ones((B, ML_EXT, L), F32)], axis=1) * w
    CT_new = decay * CT + _per_chain(_dot, vw, k)
    return h_t, CT_new, m_new


def _mlstm_kernel(*refs, seq_len, has_state, emit_state):
    q_ref, k_ref, v_ref, so_ref, g_ref, gmh_ref = refs[:6]
    refs = refs[6:]
    if has_state:
        (c0_ref, n0_ref, m0_ref), refs = refs[:3], refs[3:]
    hm_ref, refs = refs[0], refs[1:]
    if emit_state:
        (c_out_ref, n_out_ref, m_out_ref), refs = refs[:3], refs[3:]
    ct_sc, m_sc, gr_sc, h_sc = refs

    nc = seq_len // CHUNK
    d_head = ML_HEAD_DIM
    chains = [(d, h) for d in range(2) for h in range(ML_HEADS)]
    n_chain = len(chains)
    for i, (d, h) in enumerate(chains):
        if has_state:
            ct_sc[i, 0:d_head, :] = c0_ref[d, h].T
            ct_sc[i, d_head:d_head + ML_EXT, :] = jnp.broadcast_to(n0_ref[d, h], (ML_EXT, d_head))
            m_sc[i] = m0_ref[d, h][:, 0:1]
        else:
            ct_sc[i] = jnp.zeros((d_head + ML_EXT, d_head), F32)
            m_sc[i] = jnp.zeros((1, 1), F32)
    for c in range(nc):
        gr_sc[c] = g_ref[c * CHUNK:(c + 1) * CHUNK, :].T

    s_idx = lax.broadcasted_iota(jnp.int32, (CHUNK, CHUNK), 0)
    t_idx = lax.broadcasted_iota(jnp.int32, (CHUNK, CHUNK), 1)
    seen_t = jnp.stack([s_idx <= t_idx] * ML_HEADS + [s_idx >= t_idx] * ML_HEADS)
    reverse_rows = lax.broadcasted_iota(jnp.int32, (n_chain, CHUNK), 0) >= ML_HEADS

    def step(c_fwd, c_bwd, rows_of):
        chunk_of = (c_fwd, c_bwd)
        g_fwd, g_bwd = gr_sc[c_fwd], gr_sc[c_bwd]
        lo = 2 * ML_HEADS
        ic = jnp.concatenate([g_fwd[0:ML_HEADS], g_bwd[lo:lo + ML_HEADS]], axis=0)
        fc = jnp.concatenate([g_fwd[ML_HEADS:lo], g_bwd[lo + ML_HEADS:2 * lo]], axis=0)
        head = [slice(h * d_head, (h + 1) * d_head) for _, h in chains]
        stack = lambda pick: jnp.stack([pick(i, chunk_of[d]) for i, (d, _) in enumerate(chains)])
        h_t, CT_new, m_new = _mlstm_step(
            stack(lambda i, c: k_ref[rows_of(c), head[i]]),
            stack(lambda i, c: q_ref[c, head[i], :]),
            stack(lambda i, c: v_ref[c, head[i], :]),
            ic, fc, ct_sc[...], m_sc[...], seen_t, reverse_rows)
        ct_sc[...] = CT_new
        m_sc[...] = m_new
        for i, (d, _) in enumerate(chains):
            h_sc[d, chunk_of[d], head[i], :] = h_t[i]

    if nc <= 2:
        for c in range(nc):
            step(c, nc - 1 - c, lambda cc: slice(cc * CHUNK, (cc + 1) * CHUNK))
    else:
        def body(c, carry):
            step(c, nc - 1 - c, lambda cc: pl.ds(pl.multiple_of(cc * CHUNK, CHUNK), CHUNK))
            return carry
        lax.fori_loop(0, nc, body, 0, unroll=2)

    for h in range(ML_HEADS):
        hcols = slice(h * d_head, (h + 1) * d_head)
        gain = jnp.broadcast_to(gmh_ref[h], (d_head, CHUNK))
        for c in range(nc):
            rows = slice(c * CHUNK, (c + 1) * CHUNK)
            hsum = h_sc[0, c, hcols, :] + h_sc[1, c, hcols, :]
            hn = hsum * lax.rsqrt(jnp.mean(hsum * hsum, axis=0, keepdims=True) + EPS) * gain
            hm_ref[rows, hcols] = (hn.T * so_ref[rows, hcols]).astype(hm_ref.dtype)
    if emit_state:
        for i, (d, h) in enumerate(chains):
            c_out_ref[d, h] = ct_sc[i, 0:d_head, :].T
            n_out_ref[d, h] = ct_sc[i, d_head:d_head + 1, :]
            m_out_ref[d, h] = jnp.broadcast_to(m_sc[i], (1, LANES))


def _mlstm(q_t, k, v_t, so, gates, g_mh, state, batch, seq_len, emit_state):
    rows = k.shape[0]
    d = ML_HEAD_DIM
    nc = seq_len // CHUNK
    tile = pl.BlockSpec((seq_len, D_MODEL), lambda b: (b, 0))
    tile_t = pl.BlockSpec((nc, D_MODEL, CHUNK), lambda b: (b, 0, 0))
    in_specs = [tile_t, tile, tile_t, tile,
                pl.BlockSpec((seq_len, GATE_PAD), lambda b: (b, 0)),
                _resident((ML_HEADS, d, 1))]
    args = [q_t, k, v_t, so, gates, g_mh.reshape(ML_HEADS, d, 1)]
    c_spec = pl.BlockSpec((None, 2, ML_HEADS, d, d), lambda b: (b, 0, 0, 0, 0))
    n_spec = pl.BlockSpec((None, 2, ML_HEADS, 1, d), lambda b: (b, 0, 0, 0, 0))
    m_spec = pl.BlockSpec((None, 2, ML_HEADS, 1, LANES), lambda b: (b, 0, 0, 0, 0))
    if state is not None:
        in_specs += [c_spec, n_spec, m_spec]
        args += list(state)
    out_specs = [tile]
    out_shape = [jax.ShapeDtypeStruct((rows, D_MODEL), BF16)]
    if emit_state:
        out_specs += [c_spec, n_spec, m_spec]
        out_shape += [jax.ShapeDtypeStruct((batch, 2, ML_HEADS, d, d), F32),
                      jax.ShapeDtypeStruct((batch, 2, ML_HEADS, 1, d), F32),
                      jax.ShapeDtypeStruct((batch, 2, ML_HEADS, 1, LANES), F32)]
    n_state = 2 * ML_HEADS
    return pl.pallas_call(
        functools.partial(_mlstm_kernel, seq_len=seq_len, has_state=state is not None, emit_state=emit_state),
        grid=(batch,),
        in_specs=in_specs,
        out_specs=out_specs,
        out_shape=out_shape,
        scratch_shapes=[pltpu.VMEM((n_state, d + ML_EXT, d), F32), pltpu.VMEM((n_state, 1, 1), F32),
                        pltpu.VMEM((nc, GATE_PAD, CHUNK), F32),
                        pltpu.VMEM((2, nc, D_MODEL, CHUNK), F32)],
        compiler_params=_params(1),
        name="mlstm",
    )(*args)


def _merge_kernel(*refs, tiles):
    n = len(tiles)
    (x_ref, m_ref), refs = refs[:2], refs[2:]
    parts = [refs[j * n:(j + 1) * n] for j in range(4)]
    wm_ref, wd_ref, wo_ref, o_ref, w_sc = refs[4 * n:]
    i = pl.program_id(0)

    @pl.when(i == 0)
    def _():
        for j, w_ref in enumerate((wm_ref, wd_ref, wo_ref)):
            w_sc[j] = w_ref[...].astype(BF16)

    def tile(hm_ref, att_ref, sgm_ref, sgd_ref):
        y = sgm_ref[...] * _dot(hm_ref[...], w_sc[0]) + sgd_ref[...] * _dot(att_ref[...], w_sc[1])
        o_ref[...] = x_ref[...] + m_ref[5:6, :] * _dot(y, w_sc[2])

    starts = np.cumsum((0,) + tiles).tolist()
    for k in range(n):
        pl.when((i >= starts[k]) & (i < starts[k + 1]))(functools.partial(tile, *[p[k] for p in parts]))


def _merge(x, mods, mod_index, hm, att, sgm, sgd, wm, wd, wo):
    rows = x.shape[0]
    tiles = tuple(h.shape[0] // MERGE_TM for h in hm)
    assert sum(tiles) * MERGE_TM == rows
    tile = pl.BlockSpec((MERGE_TM, D_MODEL), lambda i: (i, 0))

    def part_spec(k):
        start = sum(tiles[:k])
        return pl.BlockSpec((MERGE_TM, D_MODEL), lambda i: (jnp.clip(i - start, 0, tiles[k] - 1), 0))

    part_specs = [part_spec(k) for k in range(len(tiles))]
    w_spec = _resident((D_MODEL, D_MODEL))
    return pl.pallas_call(
        functools.partial(_merge_kernel, tiles=tiles),
        grid=(rows // MERGE_TM,),
        in_specs=[tile, pl.BlockSpec((None, N_MOD, D_MODEL), lambda i: (mod_index(i * MERGE_TM), 0, 0))]
                 + part_specs * 4 + [w_spec, w_spec, w_spec],
        out_specs=tile,
        out_shape=jax.ShapeDtypeStruct((rows, D_MODEL), F32),
        scratch_shapes=[pltpu.VMEM((3, D_MODEL, D_MODEL), BF16)],
        compiler_params=_params(1),
        name="branch_merge",
    )(x, mods, *hm, *att, *sgm, *sgd, wm, wd, wo)


def _rope_tables(seq_len):
    lane = np.arange(LANES)
    r = lane % 32
    freqs = np.power(np.float32(ROPE_BASE), -(r % 16).astype(np.float32) / np.float32(16.0))
    tok = np.arange(seq_len)
    pos = np.where((lane % 64 < 32)[None, :], (tok // GRID_W)[:, None], (tok % GRID_W)[:, None]).astype(np.float32)
    ang = pos * freqs[None, :]
    sign = np.where(r < 16, -1.0, 1.0).astype(np.float32)
    return jnp.asarray(np.cos(ang), F32), jnp.asarray(np.sin(ang) * sign[None, :], F32)


def _mixer_branches(x, row0, mods, mod_index, w, batch, seq_len, ctx, lam_init):
    rope_tabs = None if ctx is None else _rope_tables(seq_len)
    mq, mk, mv, so, dq, dk, dv, sgm, sgd, gates = _proj(
        x, row0, batch * seq_len, mods, mod_index, w["g_norm"][1], w["w_seg"], w["b_gate"], w["bd"], w["g_qn"],
        w["g_kn_col"] if ctx is None else w["g_kn"], rope_tabs, seq_len)
    cache = None if ctx is None else (ctx[0], ctx[1])
    att = _attn(dq, dk, dv, cache, w["lam"], w["g_sub"], batch, seq_len, lam_init)
    state = None if ctx is None else ctx[2]
    res = _mlstm(mq, mk, mv, so, gates, w["g_mh"], state, batch, seq_len, emit_state=ctx is None)
    return res[0], att, sgm, sgd, dk, dv, res[1:]


def kernel(x_prompt, x_sample, c, cache_k, cache_v, state_C, state_n, state_m, c_ctx, w_ada, b_ada, g_norm, ffn1_w1, ffn1_w3, ffn1_w2, ffn2_w1, ffn2_w3, ffn2_w2, w_in, b_gate, g_qn, g_kn, lam_q1, lam_k1, lam_q2, lam_k2, g_sub, g_mh, w_br_m, w_br_d, w_out):
    depth = w_ada.shape[0]
    assert depth == 1
    l = 0
    bp, tp, _ = x_prompt.shape
    bs, ts, _ = x_sample.shape
    past = cache_k.shape[2]
    lam_init = 0.8 - 0.6 * math.exp(-0.3 * l)

    cvecs = jnp.concatenate([c_ctx[None, :], c, jnp.zeros((8 - 1 - bs, D_MODEL), F32)], axis=0)
    mods = _mods(cvecs, w_ada[l], b_ada[l]).reshape(8, N_MOD, D_MODEL)

    group = jnp.arange(MXU_DIM) // DA_HEAD_DIM
    w = dict(
        g_norm=g_norm[l],
        ffn1_w1=ffn1_w1[l], ffn1_w3=ffn1_w3[l], ffn1_w2=ffn1_w2[l],
        ffn2_w1=ffn2_w1[l], ffn2_w3=ffn2_w3[l], ffn2_w2=ffn2_w2[l],
        w_in_t=w_in[l].T,
        b_gate=jnp.pad(b_gate[l], (0, GATE_PAD - N_GATE_COLS)).reshape(1, GATE_PAD),
        bd=(group[:, None] == group[None, :]).astype(BF16),
        g_qn=jnp.tile(g_qn[l] * QK_LOG2_SCALE, MXU_DIM // DA_HEAD_DIM).reshape(1, MXU_DIM),
        g_kn=jnp.tile(g_kn[l], MXU_DIM // DA_HEAD_DIM).reshape(1, MXU_DIM),
        g_kn_col=g_kn[l].reshape(DA_HEAD_DIM, 1),
        lam=jnp.stack([lam_q1[l], lam_k1[l], lam_q2[l], lam_k2[l]]),
        g_sub=g_sub[l], g_mh=g_mh[l],
        w_br_m=w_br_m[l], w_br_d=w_br_d[l], w_out=w_out[l],
    )

    n_ctx, n_lat = bp * tp, bs * ts
    mod_index = lambda r: jnp.where(r < n_ctx, 0, 1 + (r - n_ctx) // ts)
    seg_start = lambda j: pl.multiple_of(jnp.where(j < N_SEG, _seg_start(j), GATE_LO), N_GATE_COLS)
    x1, w["w_seg"] = _ffn((x_prompt.reshape(n_ctx, D_MODEL), x_sample.reshape(n_lat, D_MODEL)), (n_ctx + n_lat,),
                          mods, mod_index, w["g_norm"][0], w["ffn1_w1"], w["ffn1_w3"], w["ffn1_w2"], base=0,
                          side=(w["w_in_t"], N_SEG + 1, seg_start))

    hm_p, att_p, sgm_p, sgd_p, new_k_t, new_v, (new_c, new_n, new_m) = _mixer_branches(
        x1, 0, mods, mod_index, w, bp, tp, None, lam_init)

    ctx = (cache_k[:, l].transpose(0, 2, 3, 4, 1).reshape(bs, D_MODEL, past),
           cache_v[:, l].reshape(bs, past * DA_HEADS, DA_V_DIM),
           (state_C[:, l], state_n[:, l].reshape(bs, 2, ML_HEADS, 1, ML_HEAD_DIM),
            jnp.broadcast_to(state_m[:, l][..., None, None], (bs, 2, ML_HEADS, 1, LANES))))
    hm_s, att_s, sgm_s, sgd_s, _, _, _ = _mixer_branches(x1, n_ctx, mods, mod_index, w, bs, ts, ctx, lam_init)

    x2 = _merge(x1, mods, mod_index, (hm_p, hm_s), (att_p, att_s), (sgm_p, sgm_s), (sgd_p, sgd_s),
                w["w_br_m"], w["w_br_d"], w["w_out"])
    xp, xs = _ffn((x2,), (n_ctx, n_lat), mods, mod_index, w["g_norm"][2], w["ffn2_w1"], w["ffn2_w3"], w["ffn2_w2"],
                  base=6)

    return (xp.reshape(bp, tp, D_MODEL), xs.reshape(bs, ts, D_MODEL),
            new_k_t.reshape(bp, DA_HEADS, 2, DA_HEAD_DIM, tp).transpose(0, 4, 1, 2, 3)[:, None],
            new_v.reshape(bp, 1, tp, DA_HEADS, DA_V_DIM),
            new_c[:, None], new_n.reshape(bp, 1, 2, ML_HEADS, ML_HEAD_DIM),
            new_m[..., 0, 0][:, None])
```

```python
import functools
import math

import jax
import jax.numpy as jnp
import numpy as np
from jax import lax
from jax.experimental import pallas as pl
from jax.experimental.pallas import tpu as pltpu

F32 = jnp.float32
BF16 = jnp.bfloat16

D_MODEL = 1024
D_FF = 2816
N_MOD = 9
GRID_W = 64
ML_HEADS = 4
ML_HEAD_DIM = 256
DA_HEADS = 8
DA_HEAD_DIM = 64
DA_V_DIM = 128
N_GATE_COLS = 16
CHUNK = 128
ROPE_BASE = 10000.0
QK_LOG2_SCALE = DA_HEAD_DIM ** -0.5 * math.log2(math.e)
EPS = 1e-6

LANES = 128
GATE_PAD = LANES
MXU_DIM = 256
VMEM_LIMIT = 56 * 1024 * 1024

MODS_K_SPLIT = 4
FFN_TM = 512
FFN_TF = 256
PROJ_TM = 256
MERGE_TM = 512
ATT_TQ = 256
ATT_TK_MAX = 768


def _params(n_axes):
    return pltpu.CompilerParams(dimension_semantics=("arbitrary",) * n_axes,
                                vmem_limit_bytes=VMEM_LIMIT)


def _dot(a, b):
    return jnp.dot(a.astype(BF16), b.astype(BF16), preferred_element_type=F32)


def _dot_nt(a, b):
    return lax.dot_general(a.astype(BF16), b.astype(BF16), (((1,), (1,)), ((), ())),
                           preferred_element_type=F32)


def _dot_tn(a, b):
    return lax.dot_general(a.astype(BF16), b.astype(BF16), (((0,), (0,)), ((), ())),
                           preferred_element_type=F32)


def _sigmoid(x):
    return 1.0 / (1.0 + jnp.exp(-x))


def _log_sigmoid(x):
    return jnp.minimum(x, 0.0) - jnp.log1p(jnp.exp(-jnp.abs(x)))


def _modulated_norm(x, g, shift, scale):
    y = x * lax.rsqrt(jnp.mean(x * x, axis=-1, keepdims=True) + EPS) * g
    return y * (1.0 + scale) + shift


def _resident(shape):
    return pl.BlockSpec(shape, lambda *_: (0,) * len(shape), pipeline_mode=pl.Buffered(1))


def _mods_kernel(c_ref, *refs):
    *w_refs, b_ref, o_ref = refs
    c = c_ref[...]
    s = c * _sigmoid(c)
    kb = D_MODEL // len(w_refs)
    acc = b_ref[...]
    for j, w_ref in enumerate(w_refs):
        acc = acc + _dot(s[:, j * kb:(j + 1) * kb], w_ref[...])
    o_ref[...] = acc


def _mods(cvecs, w_ada, b_ada):
    n = N_MOD * D_MODEL
    tn = D_MODEL
    kb = D_MODEL // MODS_K_SPLIT
    w_specs = [pl.BlockSpec((kb, tn), lambda j, r=r: (r, j)) for r in range(MODS_K_SPLIT)]
    return pl.pallas_call(
        _mods_kernel,
        grid=(n // tn,),
        in_specs=[pl.BlockSpec((8, D_MODEL), lambda j: (0, 0))] + w_specs + [pl.BlockSpec((1, tn), lambda j: (0, j))],
        out_specs=pl.BlockSpec((8, tn), lambda j: (0, j)),
        out_shape=jax.ShapeDtypeStruct((8, n), F32),
        compiler_params=_params(1),
        name="adaln_mods",
    )(cvecs, *([w_ada] * MODS_K_SPLIT), b_ada.reshape(1, n))


def _mod_reader(m_ref, row):
    return lambda j: m_ref[pl.ds(row, 1), j * D_MODEL:(j + 1) * D_MODEL]


def _ffn_kernel(*refs, base, mod_index, in_tiles, out_tiles, side_blocks):
    n_in, n_out = len(in_tiles), len(out_tiles)
    x_refs, refs = refs[:n_in], refs[n_in:]
    (m_ref, g_ref, w1_ref, w3_ref, w2_ref), refs = refs[:5], refs[5:]
    if side_blocks:
        side_in_ref, refs = refs[0], refs[1:]
        side_out_ref, refs = refs[n_out], refs[:n_out] + refs[n_out + 1:]
    o_refs, (w1_sc, w3_sc, w2_sc, hh_sc, acc_sc) = refs[:n_out], refs[n_out:]
    g = pl.program_id(0)
    nf = D_FF // FFN_TF
    row_tile = g - (nf - 1)
    mod = _mod_reader(m_ref, mod_index(jnp.maximum(row_tile, 0) * FFN_TM))
    gain = g_ref[base // 3:base // 3 + 1, :]
    norm = lambda x: _modulated_norm(x, gain, mod(base), mod(base + 1)).astype(BF16)
    finish = lambda x, acc: x + 0.5 * mod(base + 2) * acc

    def tile(hh, f):
        a = _dot(hh, w1_sc[f])
        b = _dot(hh, w3_sc[f])
        return _dot(a * _sigmoid(a) * b, w2_sc[f])

    @pl.when(g < nf)
    def _():
        w1_sc[g] = w1_ref[...].astype(BF16)
        w3_sc[g] = w3_ref[...].astype(BF16)
        w2_sc[g] = w2_ref[...].astype(BF16)

        @pl.when(g == 0)
        def _():
            hh_sc[...] = norm(x_refs[0][...])
            acc_sc[...] = jnp.zeros(acc_sc.shape, F32)

        acc_sc[...] += tile(hh_sc[...], g)

        @pl.when(g == nf - 1)
        def _():
            o_refs[0][...] = finish(x_refs[0][...], acc_sc[...])

    if side_blocks:
        @pl.when((g >= nf) & (g - nf < side_blocks))
        def _():
            side_out_ref[...] = side_in_ref[...].astype(BF16)

    def full_tile(x_ref, o_ref):
        x = x_ref[...]
        hh = norm(x)
        acc = jnp.zeros(x.shape, F32)
        for f in range(nf):
            acc = acc + tile(hh, f)
        o_ref[...] = finish(x, acc)

    bounds = sorted(set(np.cumsum((0,) + in_tiles).tolist()) | set(np.cumsum((0,) + out_tiles).tolist()))
    for lo, hi in zip(bounds[:-1], bounds[1:]):
        k_in = int(np.searchsorted(np.cumsum(in_tiles), lo, side="right"))
        k_out = int(np.searchsorted(np.cumsum(out_tiles), lo, side="right"))
        pl.when((g >= nf) & (row_tile >= lo) & (row_tile < hi))(
            functools.partial(full_tile, x_refs[k_in], o_refs[k_out]))


def _ffn(xs, out_rows, mods, mod_index, g, w1, w3, w2, base, side=None):
    nf = D_FF // FFN_TF
    in_tiles = tuple(x.shape[0] // FFN_TM for x in xs)
    out_tiles = tuple(r // FFN_TM for r in out_rows)
    assert sum(in_tiles) == sum(out_tiles)
    side_in_specs, side_out_specs, side_out_shape, side_args, side_blocks = [], [], [], [], 0
    if side is not None:
        side_w, side_blocks, side_start = side
        assert side_blocks <= sum(in_tiles) - 1
        blk = lambda s: jnp.clip(s - nf, 0, side_blocks - 1)
        side_in_specs = [pl.BlockSpec((pl.Element(D_MODEL), pl.Element(D_MODEL)), lambda s: (side_start(blk(s)), 0))]
        side_out_specs = [pl.BlockSpec((None, D_MODEL, D_MODEL), lambda s: (blk(s), 0, 0))]
        side_out_shape = [jax.ShapeDtypeStruct((side_blocks, D_MODEL, D_MODEL), BF16)]
        side_args = [side_w]
    row_tile = lambda s: jnp.maximum(s - (nf - 1), 0)
    f_tile = lambda s: jnp.minimum(s, nf - 1)

    def part_spec(tiles, k):
        start = sum(tiles[:k])
        return pl.BlockSpec((FFN_TM, D_MODEL), lambda s: (jnp.clip(row_tile(s) - start, 0, tiles[k] - 1), 0))

    return pl.pallas_call(
        functools.partial(_ffn_kernel, base=base, mod_index=mod_index, in_tiles=in_tiles, out_tiles=out_tiles,
                          side_blocks=side_blocks),
        grid=(nf - 1 + sum(in_tiles),),
        in_specs=[part_spec(in_tiles, k) for k in range(len(xs))] + [
            _resident(mods.shape),
            _resident(g.shape),
            pl.BlockSpec((D_MODEL, FFN_TF), lambda s: (0, f_tile(s))),
            pl.BlockSpec((D_MODEL, FFN_TF), lambda s: (0, f_tile(s))),
            pl.BlockSpec((FFN_TF, D_MODEL), lambda s: (f_tile(s), 0))] + side_in_specs,
        out_specs=[part_spec(out_tiles, k) for k in range(len(out_rows))] + side_out_specs,
        out_shape=[jax.ShapeDtypeStruct((r, D_MODEL), F32) for r in out_rows] + side_out_shape,
        scratch_shapes=[pltpu.VMEM((nf, D_MODEL, FFN_TF), BF16), pltpu.VMEM((nf, D_MODEL, FFN_TF), BF16),
                        pltpu.VMEM((nf, FFN_TF, D_MODEL), BF16),
                        pltpu.VMEM((FFN_TM, D_MODEL), BF16), pltpu.VMEM((FFN_TM, D_MODEL), F32)],
        compiler_params=_params(1),
        name="ffn",
    )(*xs, mods, g, w1, w3, w2, *side_args)


def _group_norm64(x, bd, g):
    ss = _dot(x * x, bd)
    return x * lax.rsqrt(ss * (1.0 / DA_HEAD_DIM) + EPS) * g


def _rope(x, cos, sin_signed):
    first = (lax.broadcasted_iota(jnp.int32, x.shape, 1) % 32) < 16
    partner = jnp.where(first, pltpu.roll(x, LANES - 16, 1), pltpu.roll(x, 16, 1))
    return x * cos + partner * sin_signed


N_SEG = 9
GATE_LO = 4 * D_MODEL


def _seg_start(s):
    return s * D_MODEL + N_GATE_COLS * (s >= 4)


def _proj_kernel(*refs, rope, row_of_step):
    (x_ref, m_ref, g_ref, wt_ref, wgt_ref, bg_ref, bd_ref, gq_ref, gk_ref), refs = refs[:9], refs[9:]
    if rope:
        (cos_ref, sin_ref), refs = refs[:2], refs[2:]
    (mq_ref, mk_ref, mv_ref, so_ref, dq_ref, dk_ref, dv_ref, sgm_ref, sgd_ref, gates_ref,
     w_sc, wg_sc, hh_sc) = refs
    step = pl.program_id(0)
    mod = _mod_reader(m_ref, row_of_step(step))
    norm = lambda: _modulated_norm(x_ref[...], g_ref[1:2, :], mod(3), mod(4)).astype(BF16)
    groups = MXU_DIM // DA_HEAD_DIM
    tiled = lambda ref, scale: jnp.concatenate([ref[...] * scale] * groups, axis=1)
    gq_row, gk_row = tiled(gq_ref, QK_LOG2_SCALE), tiled(gk_ref, 1.0)
    bias = jnp.concatenate([bg_ref[...], jnp.zeros((1, GATE_PAD - N_GATE_COLS), F32)], axis=1)

    def gates(hh):
        y = _dot_nt(hh, wg_sc[...]) + bias
        col = lax.broadcasted_iota(jnp.int32, y.shape, 1)
        gates_ref[...] = jnp.where((col // ML_HEADS) % 2 == 1, _log_sigmoid(y), y)

    def qk_segment(y, gain_row, o_ref):
        bd = bd_ref[...]
        for c in range(D_MODEL // MXU_DIM):
            cols = slice(c * MXU_DIM, (c + 1) * MXU_DIM)
            z = _group_norm64(y[:, cols], bd, gain_row)
            if rope:
                z = jnp.concatenate(
                    [_rope(z[:, k * LANES:(k + 1) * LANES], cos_ref[...], sin_ref[...])
                     for k in range(MXU_DIM // LANES)], axis=1)
            o_ref[:, cols] = z.astype(o_ref.dtype)

    def cache_key_segment(hh):
        y_t = _dot_nt(w_sc[5], hh)
        z = y_t.reshape(D_MODEL // DA_HEAD_DIM, DA_HEAD_DIM, y_t.shape[1])
        ms = jnp.mean(z * z, axis=1, keepdims=True)
        gain = jnp.broadcast_to(gk_row[:, 0:LANES], (LANES, LANES)).T[0:DA_HEAD_DIM, :]
        gain = jnp.concatenate([gain] * (y_t.shape[1] // LANES), axis=1)
        dk_ref[...] = (z * lax.rsqrt(ms + EPS) * gain).reshape(y_t.shape)

    def segment(s, hh):
        if s in (0, 2):
            o_ref = {0: mq_ref, 2: mv_ref}[s]
            y_t = _dot_nt(w_sc[s], hh) * ((ML_HEAD_DIM ** -0.5) if s == 0 else 1.0)
            for j in range(PROJ_TM // CHUNK):
                o_ref[j] = y_t[:, j * CHUNK:(j + 1) * CHUNK].astype(o_ref.dtype)
            return
        if s == 5 and not rope:
            cache_key_segment(hh)
            return
        y = _dot_nt(hh, w_sc[s])
        if s == 4:
            qk_segment(y, gq_row, dq_ref)
        elif s == 5:
            qk_segment(y, gk_row, dk_ref)
        elif s in (3, 7, 8):
            {3: so_ref, 7: sgm_ref, 8: sgd_ref}[s][...] = _sigmoid(y)
        elif s == 6 and not rope:
            dv_ref[...] = y.reshape(y.shape[0], DA_HEADS, DA_V_DIM)
        else:
            o_ref = {1: mk_ref, 6: dv_ref}[s]
            o_ref[...] = y.astype(o_ref.dtype)

    for s in range(N_SEG):
        @pl.when(step == s)
        def _(s=s):
            w_sc[s] = wt_ref[...]
            if s == 0:
                wg_sc[...] = jnp.zeros(wg_sc.shape, BF16)
                wg_sc[0:N_GATE_COLS, :] = wgt_ref[0:N_GATE_COLS, :]
                hh_sc[...] = norm()
                gates(hh_sc[...])
            segment(s, hh_sc[...])

    @pl.when(step >= N_SEG)
    def _():
        hh = norm()
        gates(hh)
        for s in range(N_SEG):
            segment(s, hh)


def _proj(x, row0, rows, mods, mod_index, g, w_seg, b_gate, bd, gq, gk, rope_tabs, seq_len):
    rope = rope_tabs is not None
    row_tile = lambda s: jnp.maximum(s - (N_SEG - 1), 0)
    row = lambda s: (row_tile(s), 0)
    tile = pl.BlockSpec((PROJ_TM, D_MODEL), row)
    w_block = (None, D_MODEL, D_MODEL)
    in_specs = [pl.BlockSpec((PROJ_TM, D_MODEL), lambda s: (row0 // PROJ_TM + row_tile(s), 0)),
                _resident(mods.shape),
                _resident(g.shape),
                pl.BlockSpec(w_block, lambda s: (jnp.minimum(s, N_SEG - 1), 0, 0)),
                pl.BlockSpec(w_block, lambda s: (N_SEG, 0, 0), pipeline_mode=pl.Buffered(1)),
                _resident(b_gate.shape),
                _resident((MXU_DIM, MXU_DIM)),
                _resident(gq.shape),
                _resident(gk.shape)]
    args = [x, mods, g, w_seg, w_seg, b_gate, bd, gq, gk]
    if rope:
        tiles_per_seq = seq_len // PROJ_TM
        tab = pl.BlockSpec((PROJ_TM, LANES), lambda s: (row_tile(s) % tiles_per_seq, 0))
        in_specs += [tab, tab]
        args += list(rope_tabs)
    dtypes = [BF16, BF16, BF16, F32, BF16, BF16, BF16 if rope else F32, F32, F32]
    out_shape = [jax.ShapeDtypeStruct((rows, D_MODEL), dt) for dt in dtypes]
    out_shape.append(jax.ShapeDtypeStruct((rows, GATE_PAD), F32))
    out_specs = [tile] * 9 + [pl.BlockSpec((PROJ_TM, GATE_PAD), row)]
    slabs = PROJ_TM // CHUNK
    for j in (0, 2):
        out_shape[j] = jax.ShapeDtypeStruct((rows // CHUNK, D_MODEL, CHUNK), BF16)
        out_specs[j] = pl.BlockSpec((slabs, D_MODEL, CHUNK), lambda s: (row_tile(s), 0, 0))
    if not rope:
        assert seq_len == PROJ_TM
        out_shape[5] = jax.ShapeDtypeStruct((rows // seq_len, D_MODEL, seq_len), F32)
        out_specs[5] = pl.BlockSpec((None, D_MODEL, seq_len), lambda s: (row_tile(s), 0, 0))
        out_shape[6] = jax.ShapeDtypeStruct((rows, DA_HEADS, DA_V_DIM), F32)
        out_specs[6] = pl.BlockSpec((PROJ_TM, DA_HEADS, DA_V_DIM), lambda s: (row_tile(s), 0, 0))
    return pl.pallas_call(
        functools.partial(_proj_kernel, rope=rope,
                          row_of_step=lambda step: mod_index(row0 + row_tile(step) * PROJ_TM)),
        grid=(N_SEG - 1 + rows // PROJ_TM,),
        in_specs=in_specs,
        out_specs=out_specs,
        out_shape=out_shape,
        scratch_shapes=[pltpu.VMEM((N_SEG, D_MODEL, D_MODEL), BF16), pltpu.VMEM((GATE_PAD, D_MODEL), BF16),
                        pltpu.VMEM((PROJ_TM, D_MODEL), BF16)],
        compiler_params=_params(1),
        name="mixer_in_proj",
    )(*args)


def _lambda(lam_refs, lam_init):
    q1, k1, q2, k2 = (r[...] for r in lam_refs)
    s1 = jnp.sum(q1 * k1, axis=1, keepdims=True)
    s2 = jnp.sum(q2 * k2, axis=1, keepdims=True)
    return jnp.exp(s1) - jnp.exp(s2) + lam_init


def _attn_kernel(*refs, cached, seq, lam_init):
    if cached:
        (q_ref, k_ref, v_ref, ck_ref, cv_ref), refs = refs[:5], refs[5:]
    else:
        (q_ref, k_ref, v_ref), refs = refs[:3], refs[3:]
    lam_refs, (gs_ref, o_ref, kall_sc, vt_sc, s_sc) = refs[:4], refs[4:]
    def cache_order_kv(kt_ref, vr_ref, h, lo, n):
        kall_sc[h, lo:lo + n, :] = kt_ref[h * DA_V_DIM:(h + 1) * DA_V_DIM, :].T.astype(BF16)
        vt_sc[h, :, lo:lo + n] = vr_ref[pl.ds(h, n, stride=DA_HEADS), :].T.astype(BF16)

    @pl.when(pl.program_id(1) == 0)
    def _():
        for h in range(DA_HEADS):
            if cached:
                cols = slice(h * DA_V_DIM, (h + 1) * DA_V_DIM)
                kall_sc[h, 0:seq, :] = k_ref[:, cols].astype(BF16)
                vt_sc[h, :, 0:seq] = v_ref[:, cols].astype(F32).T.astype(BF16)
                cache_order_kv(ck_ref, cv_ref, h, seq, ck_ref.shape[1])
            else:
                cache_order_kv(k_ref, v_ref, h, 0, seq)

    lam = _lambda(lam_refs, lam_init)
    lane = lax.broadcasted_iota(jnp.int32, (1, DA_V_DIM), 1)
    comp_masks = [lane < DA_HEAD_DIM, lane >= DA_HEAD_DIM]
    tq = q_ref.shape[0]
    n_keys = kall_sc.shape[1]
    n_tiles = pl.cdiv(n_keys, ATT_TK_MAX)
    tk = n_keys // n_tiles

    def stacked_q(h):
        q = q_ref[:, h * DA_V_DIM:(h + 1) * DA_V_DIM].astype(BF16)
        return jnp.concatenate([jnp.where(m, q, jnp.zeros_like(q)) for m in comp_masks], axis=0)

    def score_tile(h, j, qq, m8):
        rows = slice(j * tk, (j + 1) * tk)
        st = _dot_nt(kall_sc[h, rows, :], qq)
        s_sc[h % 2, rows, :] = st
        t8 = jnp.max(st.reshape(tk // 8, 8, 2 * tq), axis=0)
        return t8 if m8 is None else jnp.maximum(m8, t8)

    def prob_tile(h, j, mx, d8, pv):
        rows = slice(j * tk, (j + 1) * tk)
        e = jnp.exp2(s_sc[h % 2, rows, :] - mx)
        s8 = jnp.sum(e.reshape(tk // 8, 8, 2 * tq), axis=0)
        p = _dot(vt_sc[h, :, rows], e)
        return (s8 if d8 is None else d8 + s8), (p if pv is None else pv + p)

    qq = stacked_q(0)
    m8 = None
    for j in range(n_tiles):
        m8 = score_tile(0, j, qq, m8)
    for h in range(DA_HEADS):
        mx = jnp.max(m8, axis=0, keepdims=True)
        if h + 1 < DA_HEADS:
            qq = stacked_q(h + 1)
        m8, d8, pv = None, None, None
        for j in range(n_tiles):
            if h + 1 < DA_HEADS:
                m8 = score_tile(h + 1, j, qq, m8)
            d8, pv = prob_tile(h, j, mx, d8, pv)
        pv = pv * (1.0 / jnp.sum(d8, axis=0, keepdims=True))
        out_t = pv[:, :tq] - lam * pv[:, tq:]
        out_t = out_t * lax.rsqrt(jnp.mean(out_t * out_t, axis=0, keepdims=True) + EPS)
        out = out_t.T * gs_ref[...]
        o_ref[:, h * DA_V_DIM:(h + 1) * DA_V_DIM] = (out * (1.0 - lam_init)).astype(o_ref.dtype)


def _attn(q, k, v, cache, lam_vecs, g_sub, batch, seq_len, lam_init):
    rows = q.shape[0]
    nq = seq_len // ATT_TQ
    q_spec = pl.BlockSpec((ATT_TQ, D_MODEL), lambda b, i: (b * nq + i, 0))
    cache_specs = lambda n: [pl.BlockSpec((None, D_MODEL, n), lambda b, i: (b, 0, 0)),
                             pl.BlockSpec((None, n * DA_HEADS, DA_V_DIM), lambda b, i: (b, 0, 0))]
    n_keys = seq_len
    if cache is not None:
        past = cache[0].shape[2]
        n_keys += past
        kv_spec = pl.BlockSpec((seq_len, D_MODEL), lambda b, i: (b, 0))
        in_specs = [q_spec, kv_spec, kv_spec] + cache_specs(past)
        args = [q, k, v] + list(cache)
    else:
        in_specs = [q_spec] + cache_specs(seq_len)
        args = [q, k, v.reshape(batch, seq_len * DA_HEADS, DA_V_DIM)]
    in_specs += [_resident((1, DA_HEAD_DIM))] * 4 + [_resident((1, DA_V_DIM))]
    args += list(lam_vecs) + [g_sub]
    return pl.pallas_call(
        functools.partial(_attn_kernel, cached=cache is not None, seq=seq_len, lam_init=lam_init),
        grid=(batch, nq),
        in_specs=in_specs,
        out_specs=q_spec,
        out_shape=jax.ShapeDtypeStruct((rows, D_MODEL), BF16),
        scratch_shapes=[pltpu.VMEM((DA_HEADS, n_keys, DA_V_DIM), BF16),
                        pltpu.VMEM((DA_HEADS, DA_V_DIM, n_keys), BF16),
                        pltpu.VMEM((2, n_keys, 2 * ATT_TQ), F32)],
        compiler_params=_params(2),
        name="diff_attention",
    )(*args)


def _per_chain(fn, a, b):
    return jnp.stack([fn(a[i], b[i]) for i in range(a.shape[0])])


ML_EXT = 8


def _rows(x):
    return jnp.stack([x[i:i + 1, :] for i in range(x.shape[0])])


def _split3(x):
    hi = x.astype(BF16)
    r = x - hi.astype(F32)
    mid = r.astype(BF16)
    lo = (r - mid.astype(F32)).astype(BF16)
    return jnp.concatenate([hi, mid, lo], axis=1)


def _chunk_scan(x, reverse_rows, tri_prefix, tri_suffix):
    parts = _split3(x)
    return jnp.where(reverse_rows, _dot(parts, tri_suffix), _dot(parts, tri_prefix))


def _paired_value_matmul(v_t, sc):
    B, L, _ = sc.shape
    zero = jnp.zeros((L, L), sc.dtype)
    out = []
    for i in range(0, B, 2):
        lhs = jnp.concatenate([v_t[i], v_t[i + 1]], axis=1)
        rhs = jnp.concatenate([jnp.concatenate([sc[i], zero], axis=1),
                               jnp.concatenate([zero, sc[i + 1]], axis=1)], axis=0)
        both = _dot(lhs, rhs)
        out += [both[:, :L], both[:, L:]]
    return jnp.stack(out)


def _mlstm_step(k, q_t, v_t, ic, fc, CT, m, seen_t, reverse_rows):
    B, L = ic.shape
    d = k.shape[2]
    tri = lambda keep: jnp.concatenate([jnp.where(keep, 1.0, 0.0).astype(BF16)] * 3, axis=0)
    b2 = _chunk_scan(fc, reverse_rows, tri(seen_t[0]), tri(seen_t[B - 1]))
    u2 = ic - b2
    u_t = jnp.concatenate([u2, jnp.zeros((L - B, L), F32)], axis=0).T
    u_col = jnp.stack([jnp.broadcast_to(u_t[:, i:i + 1], (L, L)) for i in range(B)])
    b, i_g, f_g = _rows(b2), _rows(ic), _rows(fc)
    b_last = jnp.sum(f_g, axis=2, keepdims=True)

    log_d = jnp.where(seen_t, b + u_col, -jnp.inf)
    a = b + m
    m_t = jnp.maximum(a, jnp.max(log_d, axis=1, keepdims=True))
    dmat = jnp.exp(log_d - m_t)
    inter = jnp.exp(a - m_t)
    sc = _per_chain(_dot, k, q_t) * dmat
    cq = _per_chain(_dot, CT, q_t)
    num = _paired_value_matmul(v_t, sc.astype(BF16)) + inter * cq[:, :d, :]
    den = jnp.sum(sc, axis=1, keepdims=True) + inter * cq[:, d:d + 1, :]
    h_t = num * (1.0 / jnp.maximum(jnp.abs(den), jnp.exp(-m_t)))

    g = b_last - b + i_g
    m_new = jnp.maximum(b_last + m, jnp.max(g, axis=2, keepdims=True))
    w = jnp.exp(g - m_new)
    decay = jnp.exp(b_last + m - m_new)
    vw = jnp.concatenate([v_t.astype(F32), jnp.ones((B, ML_EXT, L), F32)], axis=1) * w
    CT_new = decay * CT + _per_chain(_dot, vw, k)
    return h_t, CT_new, m_new


def _mlstm_kernel(*refs, seq_len, has_state, emit_state):
    q_ref, k_ref, v_ref, so_ref, g_ref, gmh_ref = refs[:6]
    refs = refs[6:]
    if has_state:
        (c0_ref, n0_ref, m0_ref), refs = refs[:3], refs[3:]
    hm_ref, refs = refs[0], refs[1:]
    if emit_state:
        (c_out_ref, n_out_ref, m_out_ref), refs = refs[:3], refs[3:]
    ct_sc, m_sc, gr_sc, h_sc = refs

    nc = seq_len // CHUNK
    d_head = ML_HEAD_DIM
    chains = [(d, h) for d in range(2) for h in range(ML_HEADS)]
    n_chain = len(chains)
    for i, (d, h) in enumerate(chains):
        if has_state:
            ct_sc[i, 0:d_head, :] = c0_ref[d, h].T
            ct_sc[i, d_head:d_head + ML_EXT, :] = jnp.broadcast_to(n0_ref[d, h:h + 1, :], (ML_EXT, d_head))
            m_sc[i] = jnp.full((1, 1), m0_ref[pl.program_id(0), 0, d, h], F32)
        else:
            ct_sc[i] = jnp.zeros((d_head + ML_EXT, d_head), F32)
            m_sc[i] = jnp.zeros((1, 1), F32)
    for c in range(nc):
        gr_sc[c] = g_ref[c * CHUNK:(c + 1) * CHUNK, :].T

    s_idx = lax.broadcasted_iota(jnp.int32, (CHUNK, CHUNK), 0)
    t_idx = lax.broadcasted_iota(jnp.int32, (CHUNK, CHUNK), 1)
    seen_t = jnp.stack([s_idx <= t_idx] * ML_HEADS + [s_idx >= t_idx] * ML_HEADS)
    reverse_rows = lax.broadcasted_iota(jnp.int32, (n_chain, CHUNK), 0) >= ML_HEADS

    def step(c_fwd, c_bwd, rows_of):
        chunk_of = (c_fwd, c_bwd)
        g_fwd, g_bwd = gr_sc[c_fwd], gr_sc[c_bwd]
        lo = 2 * ML_HEADS
        ic = jnp.concatenate([g_fwd[0:ML_HEADS], g_bwd[lo:lo + ML_HEADS]], axis=0)
        fc = jnp.concatenate([g_fwd[ML_HEADS:lo], g_bwd[lo + ML_HEADS:2 * lo]], axis=0)
        head = [slice(h * d_head, (h + 1) * d_head) for _, h in chains]
        stack = lambda pick: jnp.stack([pick(i, chunk_of[d]) for i, (d, _) in enumerate(chains)])
        h_t, CT_new, m_new = _mlstm_step(
            stack(lambda i, c: k_ref[rows_of(c), head[i]]),
            stack(lambda i, c: q_ref[c, head[i], :]),
            stack(lambda i, c: v_ref[c, head[i], :]),
            ic, fc, ct_sc[...], m_sc[...], seen_t, reverse_rows)
        ct_sc[...] = CT_new
        m_sc[...] = m_new
        for i, (d, _) in enumerate(chains):
            h_sc[d, chunk_of[d], head[i], :] = h_t[i]

    if nc <= 2:
        for c in range(nc):
            step(c, nc - 1 - c, lambda cc: slice(cc * CHUNK, (cc + 1) * CHUNK))
    else:
        def body(c, carry):
            step(c, nc - 1 - c, lambda cc: pl.ds(pl.multiple_of(cc * CHUNK, CHUNK), CHUNK))
            return carry
        lax.fori_loop(0, nc, body, 0, unroll=2)

    for h in range(ML_HEADS):
        hcols = slice(h * d_head, (h + 1) * d_head)
        gain = jnp.broadcast_to(gmh_ref[h:h + 1, :], (CHUNK, d_head)).T
        for c in range(nc):
            rows = slice(c * CHUNK, (c + 1) * CHUNK)
            hsum = h_sc[0, c, hcols, :] + h_sc[1, c, hcols, :]
            hn = hsum * lax.rsqrt(jnp.mean(hsum * hsum, axis=0, keepdims=True) + EPS) * gain
            hm_ref[rows, hcols] = (hn.T * so_ref[rows, hcols]).astype(hm_ref.dtype)
    if emit_state:
        for i, (d, h) in enumerate(chains):
            c_out_ref[d, h] = ct_sc[i, 0:d_head, :].T
            n_out_ref[d, h:h + 1, :] = ct_sc[i, d_head:d_head + 1, :]
            m_out_ref[d:d + 1, h:h + 1] = m_sc[i]


def _mlstm(q_t, k, v_t, so, gates, g_mh, state, batch, seq_len, emit_state):
    rows = k.shape[0]
    d = ML_HEAD_DIM
    nc = seq_len // CHUNK
    tile = pl.BlockSpec((seq_len, D_MODEL), lambda b: (b, 0))
    tile_t = pl.BlockSpec((nc, D_MODEL, CHUNK), lambda b: (b, 0, 0))
    in_specs = [tile_t, tile, tile_t, tile,
                pl.BlockSpec((seq_len, GATE_PAD), lambda b: (b, 0)),
                _resident((ML_HEADS, d))]
    args = [q_t, k, v_t, so, gates, g_mh]
    c_spec = pl.BlockSpec((None, None, 2, ML_HEADS, d, d), lambda b: (b, 0, 0, 0, 0, 0))
    n_spec = pl.BlockSpec((None, None, 2, ML_HEADS, d), lambda b: (b, 0, 0, 0, 0))
    if state is not None:
        in_specs += [c_spec, n_spec, pl.BlockSpec(memory_space=pltpu.SMEM)]
        args += list(state)
    out_specs = [tile]
    out_shape = [jax.ShapeDtypeStruct((rows, D_MODEL), BF16)]
    if emit_state:
        out_specs += [c_spec, n_spec, pl.BlockSpec((None, None, 2, ML_HEADS), lambda b: (b, 0, 0, 0))]
        out_shape += [jax.ShapeDtypeStruct((batch, 1, 2, ML_HEADS, d, d), F32),
                      jax.ShapeDtypeStruct((batch, 1, 2, ML_HEADS, d), F32),
                      jax.ShapeDtypeStruct((batch, 1, 2, ML_HEADS), F32)]
    n_state = 2 * ML_HEADS
    return pl.pallas_call(
        functools.partial(_mlstm_kernel, seq_len=seq_len, has_state=state is not None, emit_state=emit_state),
        grid=(batch,),
        in_specs=in_specs,
        out_specs=out_specs,
        out_shape=out_shape,
        scratch_shapes=[pltpu.VMEM((n_state, d + ML_EXT, d), F32), pltpu.VMEM((n_state, 1, 1), F32),
                        pltpu.VMEM((nc, GATE_PAD, CHUNK), F32),
                        pltpu.VMEM((2, nc, D_MODEL, CHUNK), F32)],
        compiler_params=_params(1),
        name="mlstm",
    )(*args)


def _merge_kernel(*refs, tiles, mod_index):
    n = len(tiles)
    (x_ref, m_ref), refs = refs[:2], refs[2:]
    parts = [refs[j * n:(j + 1) * n] for j in range(4)]
    wm_ref, wd_ref, wo_ref, o_ref, w_sc = refs[4 * n:]
    i = pl.program_id(0)

    @pl.when(i == 0)
    def _():
        for j, w_ref in enumerate((wm_ref, wd_ref, wo_ref)):
            w_sc[j] = w_ref[...].astype(BF16)

    def tile(hm_ref, att_ref, sgm_ref, sgd_ref):
        y = sgm_ref[...] * _dot(hm_ref[...], w_sc[0]) + sgd_ref[...] * _dot(att_ref[...], w_sc[1])
        o_ref[...] = x_ref[...] + _mod_reader(m_ref, mod_index(i * MERGE_TM))(5) * _dot(y, w_sc[2])

    starts = np.cumsum((0,) + tiles).tolist()
    for k in range(n):
        pl.when((i >= starts[k]) & (i < starts[k + 1]))(functools.partial(tile, *[p[k] for p in parts]))


def _merge(x, mods, mod_index, hm, att, sgm, sgd, wm, wd, wo):
    rows = x.shape[0]
    tiles = tuple(h.shape[0] // MERGE_TM for h in hm)
    assert sum(tiles) * MERGE_TM == rows
    tile = pl.BlockSpec((MERGE_TM, D_MODEL), lambda i: (i, 0))

    def part_spec(k):
        start = sum(tiles[:k])
        return pl.BlockSpec((MERGE_TM, D_MODEL), lambda i: (jnp.clip(i - start, 0, tiles[k] - 1), 0))

    part_specs = [part_spec(k) for k in range(len(tiles))]
    w_spec = _resident((D_MODEL, D_MODEL))
    return pl.pallas_call(
        functools.partial(_merge_kernel, tiles=tiles, mod_index=mod_index),
        grid=(rows // MERGE_TM,),
        in_specs=[tile, _resident(mods.shape)]
                 + part_specs * 4 + [w_spec, w_spec, w_spec],
        out_specs=tile,
        out_shape=jax.ShapeDtypeStruct((rows, D_MODEL), F32),
        scratch_shapes=[pltpu.VMEM((3, D_MODEL, D_MODEL), BF16)],
        compiler_params=_params(1),
        name="branch_merge",
    )(x, mods, *hm, *att, *sgm, *sgd, wm, wd, wo)


def _rope_tables(seq_len):
    lane = np.arange(LANES)
    r = lane % 32
    freqs = np.power(np.float32(ROPE_BASE), -(r % 16).astype(np.float32) / np.float32(16.0))
    tok = np.arange(seq_len)
    pos = np.where((lane % 64 < 32)[None, :], (tok // GRID_W)[:, None], (tok % GRID_W)[:, None]).astype(np.float32)
    ang = pos * freqs[None, :]
    sign = np.where(r < 16, -1.0, 1.0).astype(np.float32)
    return jnp.asarray(np.cos(ang), F32), jnp.asarray(np.sin(ang) * sign[None, :], F32)


def _mixer_branches(x, row0, mods, mod_index, w, batch, seq_len, ctx, lam_init):
    rope_tabs = None if ctx is None else _rope_tables(seq_len)
    mq, mk, mv, so, dq, dk, dv, sgm, sgd, gates = _proj(
        x, row0, batch * seq_len, mods, mod_index, w["g_norm"], w["w_seg"], w["b_gate"], w["bd"], w["g_qn"],
        w["g_kn"], rope_tabs, seq_len)
    cache = None if ctx is None else (ctx[0], ctx[1])
    att = _attn(dq, dk, dv, cache, w["lam"], w["g_sub"], batch, seq_len, lam_init)
    state = None if ctx is None else ctx[2]
    res = _mlstm(mq, mk, mv, so, gates, w["g_mh"], state, batch, seq_len, emit_state=ctx is None)
    return res[0], att, sgm, sgd, dk, dv, res[1:]


def kernel(x_prompt, x_sample, c, cache_k, cache_v, state_C, state_n, state_m, c_ctx, w_ada, b_ada, g_norm, ffn1_w1, ffn1_w3, ffn1_w2, ffn2_w1, ffn2_w3, ffn2_w2, w_in, b_gate, g_qn, g_kn, lam_q1, lam_k1, lam_q2, lam_k2, g_sub, g_mh, w_br_m, w_br_d, w_out):
    depth = w_ada.shape[0]
    assert depth == 1
    l = 0
    bp, tp, _ = x_prompt.shape
    bs, ts, _ = x_sample.shape
    past = cache_k.shape[2]
    lam_init = 0.8 - 0.6 * math.exp(-0.3 * l)

    cvecs = jnp.concatenate([c_ctx[None, :], c, jnp.zeros((8 - 1 - bs, D_MODEL), F32)], axis=0)
    mods = _mods(cvecs, w_ada[l], b_ada[l])

    group = np.arange(MXU_DIM) // DA_HEAD_DIM
    w = dict(
        g_norm=g_norm[l],
        ffn1_w1=ffn1_w1[l], ffn1_w3=ffn1_w3[l], ffn1_w2=ffn1_w2[l],
        ffn2_w1=ffn2_w1[l], ffn2_w3=ffn2_w3[l], ffn2_w2=ffn2_w2[l],
        w_in_t=w_in[l].T,
        b_gate=b_gate[l:l + 1],
        bd=jnp.asarray(group[:, None] == group[None, :], BF16),
        g_qn=g_qn[l:l + 1], g_kn=g_kn[l:l + 1],
        lam=(lam_q1[l:l + 1], lam_k1[l:l + 1], lam_q2[l:l + 1], lam_k2[l:l + 1]),
        g_sub=g_sub[l:l + 1], g_mh=g_mh[l],
        w_br_m=w_br_m[l], w_br_d=w_br_d[l], w_out=w_out[l],
    )

    n_ctx, n_lat = bp * tp, bs * ts
    mod_index = lambda r: jnp.where(r < n_ctx, 0, 1 + (r - n_ctx) // ts)
    seg_start = lambda j: pl.multiple_of(jnp.where(j < N_SEG, _seg_start(j), GATE_LO), N_GATE_COLS)
    x1, w["w_seg"] = _ffn((x_prompt.reshape(n_ctx, D_MODEL), x_sample.reshape(n_lat, D_MODEL)), (n_ctx + n_lat,),
                          mods, mod_index, w["g_norm"], w["ffn1_w1"], w["ffn1_w3"], w["ffn1_w2"], base=0,
                          side=(w["w_in_t"], N_SEG + 1, seg_start))

    hm_p, att_p, sgm_p, sgd_p, new_k_t, new_v, (new_c, new_n, new_m) = _mixer_branches(
        x1, 0, mods, mod_index, w, bp, tp, None, lam_init)

    ctx = (cache_k[:, l].transpose(0, 2, 3, 4, 1).reshape(bs, D_MODEL, past),
           cache_v[:, l].reshape(bs, past * DA_HEADS, DA_V_DIM),
           (state_C, state_n, state_m))
    hm_s, att_s, sgm_s, sgd_s, _, _, _ = _mixer_branches(x1, n_ctx, mods, mod_index, w, bs, ts, ctx, lam_init)

    x2 = _merge(x1, mods, mod_index, (hm_p, hm_s), (att_p, att_s), (sgm_p, sgm_s), (sgd_p, sgd_s),
                w["w_br_m"], w["w_br_d"], w["w_out"])
    xp, xs = _ffn((x2,), (n_ctx, n_lat), mods, mod_index, w["g_norm"], w["ffn2_w1"], w["ffn2_w3"], w["ffn2_w2"],
                  base=6)

    return (xp.reshape(bp, tp, D_MODEL), xs.reshape(bs, ts, D_MODEL),
            new_k_t.reshape(bp, DA_HEADS, 2, DA_HEAD_DIM, tp).transpose(0, 4, 1, 2, 3)[:, None],
            new_v.reshape(bp, 1, tp, DA_HEADS, DA_V_DIM),
            new_c, new_n, new_m)
```

```python
import functools
import math

import jax
import jax.numpy as jnp
import numpy as np
from jax import lax
from jax.experimental import pallas as pl
from jax.experimental.pallas import tpu as pltpu

F32 = jnp.float32
BF16 = jnp.bfloat16

D_MODEL = 1024
D_FF = 2816
N_MOD = 9
GRID_W = 64
ML_HEADS = 4
ML_HEAD_DIM = 256
DA_HEADS = 8
DA_HEAD_DIM = 64
DA_V_DIM = 128
N_GATE_COLS = 16
CHUNK = 128
ROPE_BASE = 10000.0
QK_LOG2_SCALE = DA_HEAD_DIM ** -0.5 * math.log2(math.e)
EPS = 1e-6

LANES = 128
GATE_PAD = LANES
MXU_DIM = 256
VMEM_LIMIT = 56 * 1024 * 1024

MODS_K_SPLIT = 4
FFN_TM = 512
FFN_TF = 256
PROJ_TM = 256
MERGE_TM = 512
ATT_TQ = 256
ATT_TK_MAX = 768


def _params(n_axes):
    return pltpu.CompilerParams(dimension_semantics=("arbitrary",) * n_axes,
                                vmem_limit_bytes=VMEM_LIMIT)


def _dot(a, b):
    return jnp.dot(a.astype(BF16), b.astype(BF16), preferred_element_type=F32)


def _dot_nt(a, b):
    return lax.dot_general(a.astype(BF16), b.astype(BF16), (((1,), (1,)), ((), ())),
                           preferred_element_type=F32)


def _dot_tn(a, b):
    return lax.dot_general(a.astype(BF16), b.astype(BF16), (((0,), (0,)), ((), ())),
                           preferred_element_type=F32)


def _sigmoid(x):
    return 1.0 / (1.0 + jnp.exp(-x))


def _log_sigmoid(x):
    return jnp.minimum(x, 0.0) - jnp.log1p(jnp.exp(-jnp.abs(x)))


def _modulated_norm(x, g, shift, scale):
    y = x * lax.rsqrt(jnp.mean(x * x, axis=-1, keepdims=True) + EPS) * g
    return y * (1.0 + scale) + shift


def _resident(shape):
    return pl.BlockSpec(shape, lambda *_: (0,) * len(shape), pipeline_mode=pl.Buffered(1))


def _mods_kernel(c_ref, *refs):
    *w_refs, b_ref, o_ref = refs
    c = c_ref[...]
    s = c * _sigmoid(c)
    kb = D_MODEL // len(w_refs)
    acc = b_ref[...]
    for j, w_ref in enumerate(w_refs):
        acc = acc + _dot(s[:, j * kb:(j + 1) * kb], w_ref[...])
    o_ref[...] = acc


def _mods(cvecs, w_ada, b_ada):
    n = N_MOD * D_MODEL
    tn = D_MODEL
    kb = D_MODEL // MODS_K_SPLIT
    w_specs = [pl.BlockSpec((kb, tn), lambda j, r=r: (r, j)) for r in range(MODS_K_SPLIT)]
    return pl.pallas_call(
        _mods_kernel,
        grid=(n // tn,),
        in_specs=[pl.BlockSpec((8, D_MODEL), lambda j: (0, 0))] + w_specs + [pl.BlockSpec((1, tn), lambda j: (0, j))],
        out_specs=pl.BlockSpec((8, tn), lambda j: (0, j)),
        out_shape=jax.ShapeDtypeStruct((8, n), F32),
        compiler_params=_params(1),
        name="adaln_mods",
    )(cvecs, *([w_ada] * MODS_K_SPLIT), b_ada.reshape(1, n))


def _mod_reader(m_ref, row):
    return lambda j: m_ref[pl.ds(row, 1), j * D_MODEL:(j + 1) * D_MODEL]


def _ffn_kernel(*refs, base, mod_index, in_tiles, out_tiles, side_blocks):
    n_in, n_out = len(in_tiles), len(out_tiles)
    x_refs, refs = refs[:n_in], refs[n_in:]
    (m_ref, g_ref, w1_ref, w3_ref, w2_ref), refs = refs[:5], refs[5:]
    if side_blocks:
        side_in_ref, refs = refs[0], refs[1:]
        side_out_ref, refs = refs[n_out], refs[:n_out] + refs[n_out + 1:]
    o_refs, (w1_sc, w3_sc, w2_sc, hh_sc, acc_sc) = refs[:n_out], refs[n_out:]
    g = pl.program_id(0)
    nf = D_FF // FFN_TF
    row_tile = g - (nf - 1)
    mod = _mod_reader(m_ref, mod_index(jnp.maximum(row_tile, 0) * FFN_TM))
    gain = g_ref[base // 3:base // 3 + 1, :]
    norm = lambda x: _modulated_norm(x, gain, mod(base), mod(base + 1)).astype(BF16)
    finish = lambda x, acc: x + 0.5 * mod(base + 2) * acc

    def tile(hh, f):
        a = _dot(hh, w1_sc[f])
        b = _dot(hh, w3_sc[f])
        return _dot(a * _sigmoid(a) * b, w2_sc[f])

    @pl.when(g < nf)
    def _():
        w1_sc[g] = w1_ref[...].astype(BF16)
        w3_sc[g] = w3_ref[...].astype(BF16)
        w2_sc[g] = w2_ref[...].astype(BF16)

        @pl.when(g == 0)
        def _():
            hh_sc[...] = norm(x_refs[0][...])
            acc_sc[...] = jnp.zeros(acc_sc.shape, F32)

        acc_sc[...] += tile(hh_sc[...], g)

        @pl.when(g == nf - 1)
        def _():
            o_refs[0][...] = finish(x_refs[0][...], acc_sc[...])

    if side_blocks:
        @pl.when((g >= nf) & (g - nf < side_blocks))
        def _():
            side_out_ref[...] = side_in_ref[...].astype(BF16)

    def full_tile(x_ref, o_ref):
        x = x_ref[...]
        hh = norm(x)
        acc = jnp.zeros(x.shape, F32)
        for f in range(nf):
            acc = acc + tile(hh, f)
        o_ref[...] = finish(x, acc)

    bounds = sorted(set(np.cumsum((0,) + in_tiles).tolist()) | set(np.cumsum((0,) + out_tiles).tolist()))
    for lo, hi in zip(bounds[:-1], bounds[1:]):
        k_in = int(np.searchsorted(np.cumsum(in_tiles), lo, side="right"))
        k_out = int(np.searchsorted(np.cumsum(out_tiles), lo, side="right"))
        pl.when((g >= nf) & (row_tile >= lo) & (row_tile < hi))(
            functools.partial(full_tile, x_refs[k_in], o_refs[k_out]))


def _ffn(xs, out_rows, mods, mod_index, g, w1, w3, w2, base, side=None):
    nf = D_FF // FFN_TF
    in_tiles = tuple(x.shape[0] // FFN_TM for x in xs)
    out_tiles = tuple(r // FFN_TM for r in out_rows)
    assert sum(in_tiles) == sum(out_tiles)
    side_in_specs, side_out_specs, side_out_shape, side_args, side_blocks = [], [], [], [], 0
    if side is not None:
        side_w, side_blocks, side_start = side
        assert side_blocks <= sum(in_tiles) - 1
        blk = lambda s: jnp.clip(s - nf, 0, side_blocks - 1)
        side_in_specs = [pl.BlockSpec((pl.Element(D_MODEL), pl.Element(D_MODEL)), lambda s: (side_start(blk(s)), 0))]
        side_out_specs = [pl.BlockSpec((None, D_MODEL, D_MODEL), lambda s: (blk(s), 0, 0))]
        side_out_shape = [jax.ShapeDtypeStruct((side_blocks, D_MODEL, D_MODEL), BF16)]
        side_args = [side_w]
    row_tile = lambda s: jnp.maximum(s - (nf - 1), 0)
    f_tile = lambda s: jnp.minimum(s, nf - 1)

    def part_spec(tiles, k):
        start = sum(tiles[:k])
        return pl.BlockSpec((FFN_TM, D_MODEL), lambda s: (jnp.clip(row_tile(s) - start, 0, tiles[k] - 1), 0))

    return pl.pallas_call(
        functools.partial(_ffn_kernel, base=base, mod_index=mod_index, in_tiles=in_tiles, out_tiles=out_tiles,
                          side_blocks=side_blocks),
        grid=(nf - 1 + sum(in_tiles),),
        in_specs=[part_spec(in_tiles, k) for k in range(len(xs))] + [
            _resident(mods.shape),
            _resident(g.shape),
            pl.BlockSpec((D_MODEL, FFN_TF), lambda s: (0, f_tile(s))),
            pl.BlockSpec((D_MODEL, FFN_TF), lambda s: (0, f_tile(s))),
            pl.BlockSpec((FFN_TF, D_MODEL), lambda s: (f_tile(s), 0))] + side_in_specs,
        out_specs=[part_spec(out_tiles, k) for k in range(len(out_rows))] + side_out_specs,
        out_shape=[jax.ShapeDtypeStruct((r, D_MODEL), F32) for r in out_rows] + side_out_shape,
        scratch_shapes=[pltpu.VMEM((nf, D_MODEL, FFN_TF), BF16), pltpu.VMEM((nf, D_MODEL, FFN_TF), BF16),
                        pltpu.VMEM((nf, FFN_TF, D_MODEL), BF16),
                        pltpu.VMEM((FFN_TM, D_MODEL), BF16), pltpu.VMEM((FFN_TM, D_MODEL), F32)],
        compiler_params=_params(1),
        name="ffn",
    )(*xs, mods, g, w1, w3, w2, *side_args)


def _group_norm64(x, bd, g):
    ss = _dot(x * x, bd)
    return x * lax.rsqrt(ss * (1.0 / DA_HEAD_DIM) + EPS) * g


def _rope(x, cos, sin_signed):
    first = (lax.broadcasted_iota(jnp.int32, x.shape, 1) % 32) < 16
    partner = jnp.where(first, pltpu.roll(x, LANES - 16, 1), pltpu.roll(x, 16, 1))
    return x * cos + partner * sin_signed


N_SEG = 9
PROJ_WARM = (N_SEG + 1) // 2
GATE_LO = 4 * D_MODEL


def _seg_start(s):
    return s * D_MODEL + N_GATE_COLS * (s >= 4)


def _proj_kernel(*refs, rope, row_of_step):
    (x_ref, m_ref, g_ref, wt_ref, bg_ref, bd_ref, gq_ref, gk_ref), refs = refs[:8], refs[8:]
    if rope:
        (cos_ref, sin_ref), refs = refs[:2], refs[2:]
    (mq_ref, mk_ref, mv_ref, so_ref, dq_ref, dk_ref, dv_ref, sgm_ref, sgd_ref, gates_ref,
     w_sc, wg_sc, hh_sc) = refs
    step = pl.program_id(0)
    mod = _mod_reader(m_ref, row_of_step(step))
    norm = lambda: _modulated_norm(x_ref[...], g_ref[1:2, :], mod(3), mod(4)).astype(BF16)
    groups = MXU_DIM // DA_HEAD_DIM
    tiled = lambda ref, scale: jnp.concatenate([ref[...] * scale] * groups, axis=1)
    gq_row, gk_row = tiled(gq_ref, QK_LOG2_SCALE), tiled(gk_ref, 1.0)
    bias = jnp.concatenate([bg_ref[...], jnp.zeros((1, GATE_PAD - N_GATE_COLS), F32)], axis=1)

    def gates(hh):
        y = _dot_nt(hh, wg_sc[...]) + bias
        col = lax.broadcasted_iota(jnp.int32, y.shape, 1)
        gates_ref[...] = jnp.where((col // ML_HEADS) % 2 == 1, _log_sigmoid(y), y)

    def qk_segment(y, gain_row, o_ref):
        bd = bd_ref[...]
        for c in range(D_MODEL // MXU_DIM):
            cols = slice(c * MXU_DIM, (c + 1) * MXU_DIM)
            z = _group_norm64(y[:, cols], bd, gain_row)
            if rope:
                z = jnp.concatenate(
                    [_rope(z[:, k * LANES:(k + 1) * LANES], cos_ref[...], sin_ref[...])
                     for k in range(MXU_DIM // LANES)], axis=1)
            o_ref[:, cols] = z.astype(o_ref.dtype)

    def cache_key_segment(hh):
        y_t = _dot_nt(w_sc[5], hh)
        z = y_t.reshape(D_MODEL // DA_HEAD_DIM, DA_HEAD_DIM, y_t.shape[1])
        ms = jnp.mean(z * z, axis=1, keepdims=True)
        gain = jnp.broadcast_to(gk_row[:, 0:LANES], (LANES, LANES)).T[0:DA_HEAD_DIM, :]
        gain = jnp.concatenate([gain] * (y_t.shape[1] // LANES), axis=1)
        dk_ref[...] = (z * lax.rsqrt(ms + EPS) * gain).reshape(y_t.shape)

    def segment(s, hh):
        if s in (0, 2):
            o_ref = {0: mq_ref, 2: mv_ref}[s]
            y_t = _dot_nt(w_sc[s], hh) * ((ML_HEAD_DIM ** -0.5) if s == 0 else 1.0)
            for j in range(PROJ_TM // CHUNK):
                o_ref[j] = y_t[:, j * CHUNK:(j + 1) * CHUNK].astype(o_ref.dtype)
            return
        if s == 5 and not rope:
            cache_key_segment(hh)
            return
        y = _dot_nt(hh, w_sc[s])
        if s == 4:
            qk_segment(y, gq_row, dq_ref)
        elif s == 5:
            qk_segment(y, gk_row, dk_ref)
        elif s in (3, 7, 8):
            {3: so_ref, 7: sgm_ref, 8: sgd_ref}[s][...] = _sigmoid(y)
        elif s == 6 and not rope:
            dv_ref[...] = y.reshape(y.shape[0], DA_HEADS, DA_V_DIM)
        else:
            o_ref = {1: mk_ref, 6: dv_ref}[s]
            o_ref[...] = y.astype(o_ref.dtype)

    for w in range(PROJ_WARM):
        @pl.when(step == w)
        def _(w=w):
            if w == 0:
                hh_sc[...] = norm()
            for j in range(2):
                s = 2 * w + j
                if s < N_SEG:
                    w_sc[s] = wt_ref[j]
                    segment(s, hh_sc[...])
                else:
                    wg_sc[...] = jnp.zeros(wg_sc.shape, BF16)
                    wg_sc[0:N_GATE_COLS, :] = wt_ref[j, 0:N_GATE_COLS, :]
                    gates(hh_sc[...])

    @pl.when(step >= PROJ_WARM)
    def _():
        hh = norm()
        gates(hh)
        for s in range(N_SEG):
            segment(s, hh)


def _proj(x, row0, rows, mods, mod_index, g, w_seg, b_gate, bd, gq, gk, rope_tabs, seq_len):
    rope = rope_tabs is not None
    row_tile = lambda s: jnp.maximum(s - (PROJ_WARM - 1), 0)
    row = lambda s: (row_tile(s), 0)
    tile = pl.BlockSpec((PROJ_TM, D_MODEL), row)
    w_block = (2, D_MODEL, D_MODEL)
    in_specs = [pl.BlockSpec((PROJ_TM, D_MODEL), lambda s: (row0 // PROJ_TM + row_tile(s), 0)),
                _resident(mods.shape),
                _resident(g.shape),
                pl.BlockSpec(w_block, lambda s: (jnp.minimum(s, PROJ_WARM - 1), 0, 0)),
                _resident(b_gate.shape),
                _resident((MXU_DIM, MXU_DIM)),
                _resident(gq.shape),
                _resident(gk.shape)]
    args = [x, mods, g, w_seg, b_gate, bd, gq, gk]
    if rope:
        tiles_per_seq = seq_len // PROJ_TM
        tab = pl.BlockSpec((PROJ_TM, LANES), lambda s: (row_tile(s) % tiles_per_seq, 0))
        in_specs += [tab, tab]
        args += list(rope_tabs)
    dtypes = [BF16, BF16, BF16, F32, BF16, BF16, BF16 if rope else F32, F32, F32]
    out_shape = [jax.ShapeDtypeStruct((rows, D_MODEL), dt) for dt in dtypes]
    out_shape.append(jax.ShapeDtypeStruct((rows, GATE_PAD), F32))
    out_specs = [tile] * 9 + [pl.BlockSpec((PROJ_TM, GATE_PAD), row)]
    slabs = PROJ_TM // CHUNK
    for j in (0, 2):
        out_shape[j] = jax.ShapeDtypeStruct((rows // CHUNK, D_MODEL, CHUNK), BF16)
        out_specs[j] = pl.BlockSpec((slabs, D_MODEL, CHUNK), lambda s: (row_tile(s), 0, 0))
    if not rope:
        assert seq_len == PROJ_TM
        out_shape[5] = jax.ShapeDtypeStruct((rows // seq_len, D_MODEL, seq_len), F32)
        out_specs[5] = pl.BlockSpec((None, D_MODEL, seq_len), lambda s: (row_tile(s), 0, 0))
        out_shape[6] = jax.ShapeDtypeStruct((rows, DA_HEADS, DA_V_DIM), F32)
        out_specs[6] = pl.BlockSpec((PROJ_TM, DA_HEADS, DA_V_DIM), lambda s: (row_tile(s), 0, 0))
    return pl.pallas_call(
        functools.partial(_proj_kernel, rope=rope,
                          row_of_step=lambda step: mod_index(row0 + row_tile(step) * PROJ_TM)),
        grid=(PROJ_WARM - 1 + rows // PROJ_TM,),
        in_specs=in_specs,
        out_specs=out_specs,
        out_shape=out_shape,
        scratch_shapes=[pltpu.VMEM((N_SEG, D_MODEL, D_MODEL), BF16), pltpu.VMEM((GATE_PAD, D_MODEL), BF16),
                        pltpu.VMEM((PROJ_TM, D_MODEL), BF16)],
        compiler_params=_params(1),
        name="mixer_in_proj",
    )(*args)


def _lambda(lam_refs, lam_init):
    q1, k1, q2, k2 = (r[...] for r in lam_refs)
    s1 = jnp.sum(q1 * k1, axis=1, keepdims=True)
    s2 = jnp.sum(q2 * k2, axis=1, keepdims=True)
    return jnp.exp(s1) - jnp.exp(s2) + lam_init


def _attn_kernel(*refs, cached, seq, lam_init):
    if cached:
        (q_ref, k_ref, v_ref, ck_ref, cv_ref), refs = refs[:5], refs[5:]
    else:
        (q_ref, k_ref, v_ref), refs = refs[:3], refs[3:]
    lam_refs, (gs_ref, o_ref, kall_sc, vt_sc, s_sc) = refs[:4], refs[4:]
    def cache_order_kv(kt_ref, vr_ref, h, lo, n):
        kall_sc[h, lo:lo + n, :] = kt_ref[h * DA_V_DIM:(h + 1) * DA_V_DIM, :].T.astype(BF16)
        vt_sc[h, :, lo:lo + n] = vr_ref[pl.ds(h, n, stride=DA_HEADS), :].T.astype(BF16)

    @pl.when(pl.program_id(1) == 0)
    def _():
        for h in range(DA_HEADS):
            if cached:
                cols = slice(h * DA_V_DIM, (h + 1) * DA_V_DIM)
                kall_sc[h, 0:seq, :] = k_ref[:, cols].astype(BF16)
                vt_sc[h, :, 0:seq] = v_ref[:, cols].astype(F32).T.astype(BF16)
                cache_order_kv(ck_ref, cv_ref, h, seq, ck_ref.shape[1])
            else:
                cache_order_kv(k_ref, v_ref, h, 0, seq)

    lam = _lambda(lam_refs, lam_init)
    lane = lax.broadcasted_iota(jnp.int32, (1, DA_V_DIM), 1)
    comp_masks = [lane < DA_HEAD_DIM, lane >= DA_HEAD_DIM]
    tq = q_ref.shape[0]
    n_keys = kall_sc.shape[1]
    n_tiles = pl.cdiv(n_keys, ATT_TK_MAX)
    tk = n_keys // n_tiles

    def stacked_q(h):
        q = q_ref[:, h * DA_V_DIM:(h + 1) * DA_V_DIM].astype(BF16)
        return jnp.concatenate([jnp.where(m, q, jnp.zeros_like(q)) for m in comp_masks], axis=0)

    def score_tile(h, j, qq, m8):
        rows = slice(j * tk, (j + 1) * tk)
        st = _dot_nt(kall_sc[h, rows, :], qq)
        s_sc[h % 2, rows, :] = st
        t8 = jnp.max(st.reshape(tk // 8, 8, 2 * tq), axis=0)
        return t8 if m8 is None else jnp.maximum(m8, t8)

    def prob_tile(h, j, mx, d8, pv):
        rows = slice(j * tk, (j + 1) * tk)
        e = jnp.exp2(s_sc[h % 2, rows, :] - mx)
        s8 = jnp.sum(e.reshape(tk // 8, 8, 2 * tq), axis=0)
        p = _dot(vt_sc[h, :, rows], e)
        return (s8 if d8 is None else d8 + s8), (p if pv is None else pv + p)

    qq = stacked_q(0)
    m8 = None
    for j in range(n_tiles):
        m8 = score_tile(0, j, qq, m8)
    for h in range(DA_HEADS):
        mx = jnp.max(m8, axis=0, keepdims=True)
        if h + 1 < DA_HEADS:
            qq = stacked_q(h + 1)
        m8, d8, pv = None, None, None
        for j in range(n_tiles):
            if h + 1 < DA_HEADS:
                m8 = score_tile(h + 1, j, qq, m8)
            d8, pv = prob_tile(h, j, mx, d8, pv)
        pv = pv * (1.0 / jnp.sum(d8, axis=0, keepdims=True))
        out_t = pv[:, :tq] - lam * pv[:, tq:]
        out_t = out_t * lax.rsqrt(jnp.mean(out_t * out_t, axis=0, keepdims=True) + EPS)
        out = out_t.T * gs_ref[...]
        o_ref[:, h * DA_V_DIM:(h + 1) * DA_V_DIM] = (out * (1.0 - lam_init)).astype(o_ref.dtype)


def _attn(q, k, v, cache, lam_vecs, g_sub, batch, seq_len, lam_init):
    rows = q.shape[0]
    nq = seq_len // ATT_TQ
    q_spec = pl.BlockSpec((ATT_TQ, D_MODEL), lambda b, i: (b * nq + i, 0))
    cache_specs = lambda n: [pl.BlockSpec((None, D_MODEL, n), lambda b, i: (b, 0, 0)),
                             pl.BlockSpec((None, n * DA_HEADS, DA_V_DIM), lambda b, i: (b, 0, 0))]
    n_keys = seq_len
    if cache is not None:
        past = cache[0].shape[2]
        n_keys += past
        kv_spec = pl.BlockSpec((seq_len, D_MODEL), lambda b, i: (b, 0))
        in_specs = [q_spec, kv_spec, kv_spec] + cache_specs(past)
        args = [q, k, v] + list(cache)
    else:
        in_specs = [q_spec] + cache_specs(seq_len)
        args = [q, k, v.reshape(batch, seq_len * DA_HEADS, DA_V_DIM)]
    in_specs += [_resident((1, DA_HEAD_DIM))] * 4 + [_resident((1, DA_V_DIM))]
    args += list(lam_vecs) + [g_sub]
    return pl.pallas_call(
        functools.partial(_attn_kernel, cached=cache is not None, seq=seq_len, lam_init=lam_init),
        grid=(batch, nq),
        in_specs=in_specs,
        out_specs=q_spec,
        out_shape=jax.ShapeDtypeStruct((rows, D_MODEL), BF16),
        scratch_shapes=[pltpu.VMEM((DA_HEADS, n_keys, DA_V_DIM), BF16),
                        pltpu.VMEM((DA_HEADS, DA_V_DIM, n_keys), BF16),
                        pltpu.VMEM((2, n_keys, 2 * ATT_TQ), F32)],
        compiler_params=_params(2),
        name="diff_attention",
    )(*args)


def _per_chain(fn, a, b):
    return jnp.stack([fn(a[i], b[i]) for i in range(a.shape[0])])


ML_EXT = 8


def _rows(x):
    return jnp.stack([x[i:i + 1, :] for i in range(x.shape[0])])


def _split3(x):
    hi = x.astype(BF16)
    r = x - hi.astype(F32)
    mid = r.astype(BF16)
    lo = (r - mid.astype(F32)).astype(BF16)
    return jnp.concatenate([hi, mid, lo], axis=1)


def _chunk_scan(x, reverse_rows, tri_prefix, tri_suffix):
    parts = _split3(x)
    return jnp.where(reverse_rows, _dot(parts, tri_suffix), _dot(parts, tri_prefix))


def _paired_value_matmul(v_t, sc):
    B, L, _ = sc.shape
    zero = jnp.zeros((L, L), sc.dtype)
    out = []
    for i in range(0, B, 2):
        lhs = jnp.concatenate([v_t[i], v_t[i + 1]], axis=1)
        rhs = jnp.concatenate([jnp.concatenate([sc[i], zero], axis=1),
                               jnp.concatenate([zero, sc[i + 1]], axis=1)], axis=0)
        both = _dot(lhs, rhs)
        out += [both[:, :L], both[:, L:]]
    return jnp.stack(out)


def _mlstm_step(k, q_t, v_t, ic, fc, CT, m, seen_t, reverse_rows):
    B, L = ic.shape
    d = k.shape[2]
    tri = lambda keep: jnp.concatenate([jnp.where(keep, 1.0, 0.0).astype(BF16)] * 3, axis=0)
    b2 = _chunk_scan(fc, reverse_rows, tri(seen_t[0]), tri(seen_t[B - 1]))
    u2 = ic - b2
    u_t = jnp.concatenate([u2, jnp.zeros((L - B, L), F32)], axis=0).T
    u_col = jnp.stack([jnp.broadcast_to(u_t[:, i:i + 1], (L, L)) for i in range(B)])
    b, i_g, f_g = _rows(b2), _rows(ic), _rows(fc)
    b_last = jnp.sum(f_g, axis=2, keepdims=True)

    log_d = jnp.where(seen_t, b + u_col, -jnp.inf)
    a = b + m
    m_t = jnp.maximum(a, jnp.max(log_d, axis=1, keepdims=True))
    dmat = jnp.exp(log_d - m_t)
    inter = jnp.exp(a - m_t)
    sc = _per_chain(_dot, k, q_t) * dmat
    cq = _per_chain(_dot, CT, q_t)
    num = _paired_value_matmul(v_t, sc.astype(BF16)) + inter * cq[:, :d, :]
    den = jnp.sum(sc, axis=1, keepdims=True) + inter * cq[:, d:d + 1, :]
    h_t = num * (1.0 / jnp.maximum(jnp.abs(den), jnp.exp(-m_t)))

    g = b_last - b + i_g
    m_new = jnp.maximum(b_last + m, jnp.max(g, axis=2, keepdims=True))
    w = jnp.exp(g - m_new)
    decay = jnp.exp(b_last + m - m_new)
    vw = jnp.concatenate([v_t.astype(F32), jnp.ones((B, ML_EXT, L), F32)], axis=1) * w
    CT_new = decay * CT + _per_chain(_dot, vw, k)
    return h_t, CT_new, m_new


def _mlstm_kernel(*refs, seq_len, has_state, emit_state):
    q_ref, k_ref, v_ref, so_ref, g_ref, gmh_ref = refs[:6]
    refs = refs[6:]
    if has_state:
        (c0_ref, n0_ref, m0_ref), refs = refs[:3], refs[3:]
    hm_ref, refs = refs[0], refs[1:]
    if emit_state:
        (c_out_ref, n_out_ref, m_out_ref), refs = refs[:3], refs[3:]
    ct_sc, m_sc, gr_sc, h_sc = refs

    nc = seq_len // CHUNK
    d_head = ML_HEAD_DIM
    chains = [(d, h) for d in range(2) for h in range(ML_HEADS)]
    n_chain = len(chains)
    for i, (d, h) in enumerate(chains):
        if has_state:
            ct_sc[i, 0:d_head, :] = c0_ref[d, h].T
            ct_sc[i, d_head:d_head + ML_EXT, :] = jnp.broadcast_to(n0_ref[d, h:h + 1, :], (ML_EXT, d_head))
            m_sc[i] = jnp.full((1, 1), m0_ref[pl.program_id(0), 0, d, h], F32)
        else:
            ct_sc[i] = jnp.zeros((d_head + ML_EXT, d_head), F32)
            m_sc[i] = jnp.zeros((1, 1), F32)
    for c in range(nc):
        gr_sc[c] = g_ref[c * CHUNK:(c + 1) * CHUNK, :].T

    s_idx = lax.broadcasted_iota(jnp.int32, (CHUNK, CHUNK), 0)
    t_idx = lax.broadcasted_iota(jnp.int32, (CHUNK, CHUNK), 1)
    seen_t = jnp.stack([s_idx <= t_idx] * ML_HEADS + [s_idx >= t_idx] * ML_HEADS)
    reverse_rows = lax.broadcasted_iota(jnp.int32, (n_chain, CHUNK), 0) >= ML_HEADS

    def step(c_fwd, c_bwd, rows_of):
        chunk_of = (c_fwd, c_bwd)
        g_fwd, g_bwd = gr_sc[c_fwd], gr_sc[c_bwd]
        lo = 2 * ML_HEADS
        ic = jnp.concatenate([g_fwd[0:ML_HEADS], g_bwd[lo:lo + ML_HEADS]], axis=0)
        fc = jnp.concatenate([g_fwd[ML_HEADS:lo], g_bwd[lo + ML_HEADS:2 * lo]], axis=0)
        head = [slice(h * d_head, (h + 1) * d_head) for _, h in chains]
        stack = lambda pick: jnp.stack([pick(i, chunk_of[d]) for i, (d, _) in enumerate(chains)])
        h_t, CT_new, m_new = _mlstm_step(
            stack(lambda i, c: k_ref[rows_of(c), head[i]]),
            stack(lambda i, c: q_ref[c, head[i], :]),
            stack(lambda i, c: v_ref[c, head[i], :]),
            ic, fc, ct_sc[...], m_sc[...], seen_t, reverse_rows)
        ct_sc[...] = CT_new
        m_sc[...] = m_new
        for i, (d, _) in enumerate(chains):
            h_sc[d, chunk_of[d], head[i], :] = h_t[i]

    if nc <= 2:
        for c in range(nc):
            step(c, nc - 1 - c, lambda cc: slice(cc * CHUNK, (cc + 1) * CHUNK))
    else:
        def body(c, carry):
            step(c, nc - 1 - c, lambda cc: pl.ds(pl.multiple_of(cc * CHUNK, CHUNK), CHUNK))
            return carry
        lax.fori_loop(0, nc, body, 0, unroll=2)

    for h in range(ML_HEADS):
        hcols = slice(h * d_head, (h + 1) * d_head)
        gain = jnp.broadcast_to(gmh_ref[h:h + 1, :], (CHUNK, d_head)).T
        for c in range(nc):
            rows = slice(c * CHUNK, (c + 1) * CHUNK)
            hsum = h_sc[0, c, hcols, :] + h_sc[1, c, hcols, :]
            hn = hsum * lax.rsqrt(jnp.mean(hsum * hsum, axis=0, keepdims=True) + EPS) * gain
            hm_ref[rows, hcols] = (hn.T * so_ref[rows, hcols]).astype(hm_ref.dtype)
    if emit_state:
        for i, (d, h) in enumerate(chains):
            c_out_ref[d, h] = ct_sc[i, 0:d_head, :].T
            n_out_ref[d, h:h + 1, :] = ct_sc[i, d_head:d_head + 1, :]
            m_out_ref[d:d + 1, h:h + 1] = m_sc[i]


def _mlstm(q_t, k, v_t, so, gates, g_mh, state, batch, seq_len, emit_state):
    rows = k.shape[0]
    d = ML_HEAD_DIM
    nc = seq_len // CHUNK
    tile = pl.BlockSpec((seq_len, D_MODEL), lambda b: (b, 0))
    tile_t = pl.BlockSpec((nc, D_MODEL, CHUNK), lambda b: (b, 0, 0))
    in_specs = [tile_t, tile, tile_t, tile,
                pl.BlockSpec((seq_len, GATE_PAD), lambda b: (b, 0)),
                _resident((ML_HEADS, d))]
    args = [q_t, k, v_t, so, gates, g_mh]
    c_spec = pl.BlockSpec((None, None, 2, ML_HEADS, d, d), lambda b: (b, 0, 0, 0, 0, 0))
    n_spec = pl.BlockSpec((None, None, 2, ML_HEADS, d), lambda b: (b, 0, 0, 0, 0))
    if state is not None:
        in_specs += [c_spec, n_spec, pl.BlockSpec(memory_space=pltpu.SMEM)]
        args += list(state)
    out_specs = [tile]
    out_shape = [jax.ShapeDtypeStruct((rows, D_MODEL), BF16)]
    if emit_state:
        out_specs += [c_spec, n_spec, pl.BlockSpec((None, None, 2, ML_HEADS), lambda b: (b, 0, 0, 0))]
        out_shape += [jax.ShapeDtypeStruct((batch, 1, 2, ML_HEADS, d, d), F32),
                      jax.ShapeDtypeStruct((batch, 1, 2, ML_HEADS, d), F32),
                      jax.ShapeDtypeStruct((batch, 1, 2, ML_HEADS), F32)]
    n_state = 2 * ML_HEADS
    return pl.pallas_call(
        functools.partial(_mlstm_kernel, seq_len=seq_len, has_state=state is not None, emit_state=emit_state),
        grid=(batch,),
        in_specs=in_specs,
        out_specs=out_specs,
        out_shape=out_shape,
        scratch_shapes=[pltpu.VMEM((n_state, d + ML_EXT, d), F32), pltpu.VMEM((n_state, 1, 1), F32),
                        pltpu.VMEM((nc, GATE_PAD, CHUNK), F32),
                        pltpu.VMEM((2, nc, D_MODEL, CHUNK), F32)],
        compiler_params=_params(1),
        name="mlstm",
    )(*args)


def _merge_kernel(*refs, tiles, mod_index):
    n = len(tiles)
    (x_ref, m_ref), refs = refs[:2], refs[2:]
    parts = [refs[j * n:(j + 1) * n] for j in range(4)]
    wm_ref, wd_ref, wo_ref, o_ref, w_sc = refs[4 * n:]
    i = pl.program_id(0)

    @pl.when(i == 0)
    def _():
        for j, w_ref in enumerate((wm_ref, wd_ref, wo_ref)):
            w_sc[j] = w_ref[...].astype(BF16)

    def tile(hm_ref, att_ref, sgm_ref, sgd_ref):
        y = sgm_ref[...] * _dot(hm_ref[...], w_sc[0]) + sgd_ref[...] * _dot(att_ref[...], w_sc[1])
        o_ref[...] = x_ref[...] + _mod_reader(m_ref, mod_index(i * MERGE_TM))(5) * _dot(y, w_sc[2])

    starts = np.cumsum((0,) + tiles).tolist()
    for k in range(n):
        pl.when((i >= starts[k]) & (i < starts[k + 1]))(functools.partial(tile, *[p[k] for p in parts]))


def _merge(x, mods, mod_index, hm, att, sgm, sgd, wm, wd, wo):
    rows = x.shape[0]
    tiles = tuple(h.shape[0] // MERGE_TM for h in hm)
    assert sum(tiles) * MERGE_TM == rows
    tile = pl.BlockSpec((MERGE_TM, D_MODEL), lambda i: (i, 0))

    def part_spec(k):
        start = sum(tiles[:k])
        return pl.BlockSpec((MERGE_TM, D_MODEL), lambda i: (jnp.clip(i - start, 0, tiles[k] - 1), 0))

    part_specs = [part_spec(k) for k in range(len(tiles))]
    w_spec = _resident((D_MODEL, D_MODEL))
    return pl.pallas_call(
        functools.partial(_merge_kernel, tiles=tiles, mod_index=mod_index),
        grid=(rows // MERGE_TM,),
        in_specs=[tile, _resident(mods.shape)]
                 + part_specs * 4 + [w_spec, w_spec, w_spec],
        out_specs=tile,
        out_shape=jax.ShapeDtypeStruct((rows, D_MODEL), F32),
        scratch_shapes=[pltpu.VMEM((3, D_MODEL, D_MODEL), BF16)],
        compiler_params=_params(1),
        name="branch_merge",
    )(x, mods, *hm, *att, *sgm, *sgd, wm, wd, wo)


def _rope_tables(seq_len):
    lane = np.arange(LANES)
    r = lane % 32
    freqs = np.power(np.float32(ROPE_BASE), -(r % 16).astype(np.float32) / np.float32(16.0))
    tok = np.arange(seq_len)
    pos = np.where((lane % 64 < 32)[None, :], (tok // GRID_W)[:, None], (tok % GRID_W)[:, None]).astype(np.float32)
    ang = pos * freqs[None, :]
    sign = np.where(r < 16, -1.0, 1.0).astype(np.float32)
    return jnp.asarray(np.cos(ang), F32), jnp.asarray(np.sin(ang) * sign[None, :], F32)


def _mixer_branches(x, row0, mods, mod_index, w, batch, seq_len, ctx, lam_init):
    rope_tabs = None if ctx is None else _rope_tables(seq_len)
    mq, mk, mv, so, dq, dk, dv, sgm, sgd, gates = _proj(
        x, row0, batch * seq_len, mods, mod_index, w["g_norm"], w["w_seg"], w["b_gate"], w["bd"], w["g_qn"],
        w["g_kn"], rope_tabs, seq_len)
    cache = None if ctx is None else (ctx[0], ctx[1])
    att = _attn(dq, dk, dv, cache, w["lam"], w["g_sub"], batch, seq_len, lam_init)
    state = None if ctx is None else ctx[2]
    res = _mlstm(mq, mk, mv, so, gates, w["g_mh"], state, batch, seq_len, emit_state=ctx is None)
    return res[0], att, sgm, sgd, dk, dv, res[1:]


def kernel(x_prompt, x_sample, c, cache_k, cache_v, state_C, state_n, state_m, c_ctx, w_ada, b_ada, g_norm, ffn1_w1, ffn1_w3, ffn1_w2, ffn2_w1, ffn2_w3, ffn2_w2, w_in, b_gate, g_qn, g_kn, lam_q1, lam_k1, lam_q2, lam_k2, g_sub, g_mh, w_br_m, w_br_d, w_out):
    depth = w_ada.shape[0]
    assert depth == 1
    l = 0
    bp, tp, _ = x_prompt.shape
    bs, ts, _ = x_sample.shape
    past = cache_k.shape[2]
    lam_init = 0.8 - 0.6 * math.exp(-0.3 * l)

    cvecs = jnp.concatenate([c_ctx[None, :], c, jnp.zeros((8 - 1 - bs, D_MODEL), F32)], axis=0)
    mods = _mods(cvecs, w_ada[l], b_ada[l])

    group = np.arange(MXU_DIM) // DA_HEAD_DIM
    w = dict(
        g_norm=g_norm[l],
        ffn1_w1=ffn1_w1[l], ffn1_w3=ffn1_w3[l], ffn1_w2=ffn1_w2[l],
        ffn2_w1=ffn2_w1[l], ffn2_w3=ffn2_w3[l], ffn2_w2=ffn2_w2[l],
        w_in_t=w_in[l].T,
        b_gate=b_gate[l:l + 1],
        bd=jnp.asarray(group[:, None] == group[None, :], BF16),
        g_qn=g_qn[l:l + 1], g_kn=g_kn[l:l + 1],
        lam=(lam_q1[l:l + 1], lam_k1[l:l + 1], lam_q2[l:l + 1], lam_k2[l:l + 1]),
        g_sub=g_sub[l:l + 1], g_mh=g_mh[l],
        w_br_m=w_br_m[l], w_br_d=w_br_d[l], w_out=w_out[l],
    )

    n_ctx, n_lat = bp * tp, bs * ts
    mod_index = lambda r: jnp.where(r < n_ctx, 0, 1 + (r - n_ctx) // ts)
    seg_start = lambda j: pl.multiple_of(jnp.where(j < N_SEG, _seg_start(j), GATE_LO), N_GATE_COLS)
    x1, w["w_seg"] = _ffn((x_prompt.reshape(n_ctx, D_MODEL), x_sample.reshape(n_lat, D_MODEL)), (n_ctx + n_lat,),
                          mods, mod_index, w["g_norm"], w["ffn1_w1"], w["ffn1_w3"], w["ffn1_w2"], base=0,
                          side=(w["w_in_t"], N_SEG + 1, seg_start))

    hm_p, att_p, sgm_p, sgd_p, new_k_t, new_v, (new_c, new_n, new_m) = _mixer_branches(
        x1, 0, mods, mod_index, w, bp, tp, None, lam_init)

    ctx = (cache_k[:, l].transpose(0, 2, 3, 4, 1).reshape(bs, D_MODEL, past),
           cache_v[:, l].reshape(bs, past * DA_HEADS, DA_V_DIM),
           (state_C, state_n, state_m))
    hm_s, att_s, sgm_s, sgd_s, _, _, _ = _mixer_branches(x1, n_ctx, mods, mod_index, w, bs, ts, ctx, lam_init)

    x2 = _merge(x1, mods, mod_index, (hm_p, hm_s), (att_p, att_s), (sgm_p, sgm_s), (sgd_p, sgd_s),
                w["w_br_m"], w["w_br_d"], w["w_out"])
    xp, xs = _ffn((x2,), (n_ctx, n_lat), mods, mod_index, w["g_norm"], w["ffn2_w1"], w["ffn2_w3"], w["ffn2_w2"],
                  base=6)

    return (xp.reshape(bp, tp, D_MODEL), xs.reshape(bs, ts, D_MODEL),
            new_k_t.reshape(bp, DA_HEADS, 2, DA_HEAD_DIM, tp).transpose(0, 4, 1, 2, 3)[:, None],
            new_v.reshape(bp, 1, tp, DA_HEADS, DA_V_DIM),
            new_c, new_n, new_m)
```

```python
import functools
import math

import jax
import jax.numpy as jnp
import numpy as np
from jax import lax
from jax.experimental import pallas as pl
from jax.experimental.pallas import tpu as pltpu

F32 = jnp.float32
BF16 = jnp.bfloat16

D_MODEL = 1024
D_FF = 2816
N_MOD = 9
GRID_W = 64
ML_HEADS = 4
ML_HEAD_DIM = 256
DA_HEADS = 8
DA_HEAD_DIM = 64
DA_V_DIM = 128
N_GATE_COLS = 16
CHUNK = 128
ROPE_BASE = 10000.0
QK_LOG2_SCALE = DA_HEAD_DIM ** -0.5 * math.log2(math.e)
EPS = 1e-6

LANES = 128
GATE_PAD = LANES
MXU_DIM = 256
VMEM_LIMIT = 56 * 1024 * 1024

MODS_K_SPLIT = 4
FFN_TM = 512
FFN_TF = 256
PROJ_TM = 256
MERGE_TM = 512
ATT_TQ = 256
ATT_TK_MAX = 768
ATT_SHORT_SEQ_BATCH = 2


def _params(n_axes):
    return pltpu.CompilerParams(dimension_semantics=("arbitrary",) * n_axes,
                                vmem_limit_bytes=VMEM_LIMIT)


def _dot(a, b):
    return jnp.dot(a.astype(BF16), b.astype(BF16), preferred_element_type=F32)


def _dot_nt(a, b):
    return lax.dot_general(a.astype(BF16), b.astype(BF16), (((1,), (1,)), ((), ())),
                           preferred_element_type=F32)


def _dot_tn(a, b):
    return lax.dot_general(a.astype(BF16), b.astype(BF16), (((0,), (0,)), ((), ())),
                           preferred_element_type=F32)


def _sigmoid(x):
    return 1.0 / (1.0 + jnp.exp(-x))


def _log_sigmoid(x):
    return jnp.minimum(x, 0.0) - jnp.log1p(jnp.exp(-jnp.abs(x)))


def _modulated_norm(x, g, shift, scale):
    y = x * lax.rsqrt(jnp.mean(x * x, axis=-1, keepdims=True) + EPS) * g
    return y * (1.0 + scale) + shift


def _resident(shape):
    return pl.BlockSpec(shape, lambda *_: (0,) * len(shape), pipeline_mode=pl.Buffered(1))


def _mods_kernel(c_ref, *refs):
    *w_refs, b_ref, o_ref = refs
    c = c_ref[...]
    s = c * _sigmoid(c)
    kb = D_MODEL // len(w_refs)
    acc = b_ref[...]
    for j, w_ref in enumerate(w_refs):
        acc = acc + _dot(s[:, j * kb:(j + 1) * kb], w_ref[...])
    o_ref[...] = acc


def _mods(cvecs, w_ada, b_ada):
    n = N_MOD * D_MODEL
    tn = D_MODEL
    kb = D_MODEL // MODS_K_SPLIT
    w_specs = [pl.BlockSpec((kb, tn), lambda j, r=r: (r, j)) for r in range(MODS_K_SPLIT)]
    return pl.pallas_call(
        _mods_kernel,
        grid=(n // tn,),
        in_specs=[pl.BlockSpec((8, D_MODEL), lambda j: (0, 0))] + w_specs + [pl.BlockSpec((1, tn), lambda j: (0, j))],
        out_specs=pl.BlockSpec((8, tn), lambda j: (0, j)),
        out_shape=jax.ShapeDtypeStruct((8, n), F32),
        compiler_params=_params(1),
        name="adaln_mods",
    )(cvecs, *([w_ada] * MODS_K_SPLIT), b_ada.reshape(1, n))


def _mod_reader(m_ref, row):
    return lambda j: m_ref[pl.ds(row, 1), j * D_MODEL:(j + 1) * D_MODEL]


def _ffn_kernel(*refs, base, mod_index, in_tiles, out_tiles, side_blocks):
    n_in, n_out = len(in_tiles), len(out_tiles)
    x_refs, refs = refs[:n_in], refs[n_in:]
    (m_ref, g_ref, w1_ref, w3_ref, w2_ref), refs = refs[:5], refs[5:]
    if side_blocks:
        side_in_ref, refs = refs[0], refs[1:]
        side_out_ref, refs = refs[n_out], refs[:n_out] + refs[n_out + 1:]
    o_refs, (w1_sc, w3_sc, w2_sc, hh_sc, acc_sc) = refs[:n_out], refs[n_out:]
    g = pl.program_id(0)
    nf = D_FF // FFN_TF
    row_tile = g - (nf - 1)
    mod = _mod_reader(m_ref, mod_index(jnp.maximum(row_tile, 0) * FFN_TM))
    gain = g_ref[base // 3:base // 3 + 1, :]
    norm = lambda x: _modulated_norm(x, gain, mod(base), mod(base + 1)).astype(BF16)
    finish = lambda x, acc: x + 0.5 * mod(base + 2) * acc

    def tile(hh, f):
        a = _dot(hh, w1_sc[f])
        b = _dot(hh, w3_sc[f])
        return _dot(a * _sigmoid(a) * b, w2_sc[f])

    @pl.when(g < nf)
    def _():
        w1_sc[g] = w1_ref[...].astype(BF16)
        w3_sc[g] = w3_ref[...].astype(BF16)
        w2_sc[g] = w2_ref[...].astype(BF16)

        @pl.when(g == 0)
        def _():
            hh_sc[...] = norm(x_refs[0][...])
            acc_sc[...] = jnp.zeros(acc_sc.shape, F32)

        acc_sc[...] += tile(hh_sc[...], g)

        @pl.when(g == nf - 1)
        def _():
            o_refs[0][...] = finish(x_refs[0][...], acc_sc[...])

    if side_blocks:
        @pl.when((g >= nf) & (g - nf < side_blocks))
        def _():
            side_out_ref[...] = side_in_ref[...].astype(BF16)

    def full_tile(x_ref, o_ref):
        x = x_ref[...]
        hh = norm(x)
        acc = jnp.zeros(x.shape, F32)
        for f in range(nf):
            acc = acc + tile(hh, f)
        o_ref[...] = finish(x, acc)

    bounds = sorted(set(np.cumsum((0,) + in_tiles).tolist()) | set(np.cumsum((0,) + out_tiles).tolist()))
    for lo, hi in zip(bounds[:-1], bounds[1:]):
        k_in = int(np.searchsorted(np.cumsum(in_tiles), lo, side="right"))
        k_out = int(np.searchsorted(np.cumsum(out_tiles), lo, side="right"))
        pl.when((g >= nf) & (row_tile >= lo) & (row_tile < hi))(
            functools.partial(full_tile, x_refs[k_in], o_refs[k_out]))


def _ffn(xs, out_rows, mods, mod_index, g, w1, w3, w2, base, side=None):
    nf = D_FF // FFN_TF
    in_tiles = tuple(x.shape[0] // FFN_TM for x in xs)
    out_tiles = tuple(r // FFN_TM for r in out_rows)
    assert sum(in_tiles) == sum(out_tiles)
    side_in_specs, side_out_specs, side_out_shape, side_args, side_blocks = [], [], [], [], 0
    if side is not None:
        side_w, side_blocks, side_start = side
        assert side_blocks <= sum(in_tiles) - 1
        blk = lambda s: jnp.clip(s - nf, 0, side_blocks - 1)
        side_in_specs = [pl.BlockSpec((pl.Element(D_MODEL), pl.Element(D_MODEL)), lambda s: (side_start(blk(s)), 0))]
        side_out_specs = [pl.BlockSpec((None, D_MODEL, D_MODEL), lambda s: (blk(s), 0, 0))]
        side_out_shape = [jax.ShapeDtypeStruct((side_blocks, D_MODEL, D_MODEL), BF16)]
        side_args = [side_w]
    row_tile = lambda s: jnp.maximum(s - (nf - 1), 0)
    f_tile = lambda s: jnp.minimum(s, nf - 1)

    def part_spec(tiles, k):
        start = sum(tiles[:k])
        return pl.BlockSpec((FFN_TM, D_MODEL), lambda s: (jnp.clip(row_tile(s) - start, 0, tiles[k] - 1), 0))

    return pl.pallas_call(
        functools.partial(_ffn_kernel, base=base, mod_index=mod_index, in_tiles=in_tiles, out_tiles=out_tiles,
                          side_blocks=side_blocks),
        grid=(nf - 1 + sum(in_tiles),),
        in_specs=[part_spec(in_tiles, k) for k in range(len(xs))] + [
            _resident(mods.shape),
            _resident(g.shape),
            pl.BlockSpec((D_MODEL, FFN_TF), lambda s: (0, f_tile(s))),
            pl.BlockSpec((D_MODEL, FFN_TF), lambda s: (0, f_tile(s))),
            pl.BlockSpec((FFN_TF, D_MODEL), lambda s: (f_tile(s), 0))] + side_in_specs,
        out_specs=[part_spec(out_tiles, k) for k in range(len(out_rows))] + side_out_specs,
        out_shape=[jax.ShapeDtypeStruct((r, D_MODEL), F32) for r in out_rows] + side_out_shape,
        scratch_shapes=[pltpu.VMEM((nf, D_MODEL, FFN_TF), BF16), pltpu.VMEM((nf, D_MODEL, FFN_TF), BF16),
                        pltpu.VMEM((nf, FFN_TF, D_MODEL), BF16),
                        pltpu.VMEM((FFN_TM, D_MODEL), BF16), pltpu.VMEM((FFN_TM, D_MODEL), F32)],
        compiler_params=_params(1),
        name="ffn",
    )(*xs, mods, g, w1, w3, w2, *side_args)


def _group_norm64(x, bd, g):
    ss = _dot(x * x, bd)
    return x * lax.rsqrt(ss * (1.0 / DA_HEAD_DIM) + EPS) * g


def _rope(x, cos, sin_signed):
    first = (lax.broadcasted_iota(jnp.int32, x.shape, 1) % 32) < 16
    partner = jnp.where(first, pltpu.roll(x, LANES - 16, 1), pltpu.roll(x, 16, 1))
    return x * cos + partner * sin_signed


N_SEG = 9
PROJ_WARM = (N_SEG + 1) // 2
GATE_LO = 4 * D_MODEL


def _seg_start(s):
    return s * D_MODEL + N_GATE_COLS * (s >= 4)


def _proj_kernel(*refs, rope, row_of_step):
    (x_ref, m_ref, g_ref, wt_ref, bg_ref, bd_ref, gq_ref, gk_ref), refs = refs[:8], refs[8:]
    if rope:
        (cos_ref, sin_ref), refs = refs[:2], refs[2:]
    (mq_ref, mk_ref, mv_ref, so_ref, dq_ref, dk_ref, dv_ref, sgm_ref, sgd_ref, gates_ref,
     w_sc, wg_sc, hh_sc) = refs
    step = pl.program_id(0)
    mod = _mod_reader(m_ref, row_of_step(step))
    norm = lambda: _modulated_norm(x_ref[...], g_ref[1:2, :], mod(3), mod(4)).astype(BF16)
    groups = MXU_DIM // DA_HEAD_DIM
    tiled = lambda ref, scale: jnp.concatenate([ref[...] * scale] * groups, axis=1)
    gq_row, gk_row = tiled(gq_ref, QK_LOG2_SCALE), tiled(gk_ref, 1.0)
    bias = jnp.concatenate([bg_ref[...], jnp.zeros((1, GATE_PAD - N_GATE_COLS), F32)], axis=1)

    def gates(hh):
        y = _dot_nt(hh, wg_sc[...]) + bias
        col = lax.broadcasted_iota(jnp.int32, y.shape, 1)
        gates_ref[...] = jnp.where((col // ML_HEADS) % 2 == 1, _log_sigmoid(y), y)

    def qk_segment(y, gain_row, o_ref):
        bd = bd_ref[...]
        for c in range(D_MODEL // MXU_DIM):
            cols = slice(c * MXU_DIM, (c + 1) * MXU_DIM)
            z = _group_norm64(y[:, cols], bd, gain_row)
            if rope:
                z = jnp.concatenate(
                    [_rope(z[:, k * LANES:(k + 1) * LANES], cos_ref[...], sin_ref[...])
                     for k in range(MXU_DIM // LANES)], axis=1)
            o_ref[:, cols] = z.astype(o_ref.dtype)

    def cache_key_segment(hh):
        y_t = _dot_nt(w_sc[5], hh)
        z = y_t.reshape(D_MODEL // DA_HEAD_DIM, DA_HEAD_DIM, y_t.shape[1])
        ms = jnp.mean(z * z, axis=1, keepdims=True)
        gain = jnp.broadcast_to(gk_row[:, 0:LANES], (LANES, LANES)).T[0:DA_HEAD_DIM, :]
        gain = jnp.concatenate([gain] * (y_t.shape[1] // LANES), axis=1)
        dk_ref[...] = (z * lax.rsqrt(ms + EPS) * gain).reshape(y_t.shape)

    def segment(s, hh):
        if s in (0, 2):
            o_ref = {0: mq_ref, 2: mv_ref}[s]
            y_t = _dot_nt(w_sc[s], hh) * ((ML_HEAD_DIM ** -0.5) if s == 0 else 1.0)
            for j in range(PROJ_TM // CHUNK):
                o_ref[j] = y_t[:, j * CHUNK:(j + 1) * CHUNK].astype(o_ref.dtype)
            return
        if s == 5 and not rope:
            cache_key_segment(hh)
            return
        y = _dot_nt(hh, w_sc[s])
        if s == 4:
            qk_segment(y, gq_row, dq_ref)
        elif s == 5:
            qk_segment(y, gk_row, dk_ref)
        elif s in (3, 7, 8):
            {3: so_ref, 7: sgm_ref, 8: sgd_ref}[s][...] = _sigmoid(y)
        elif s == 6 and not rope:
            dv_ref[...] = y.reshape(y.shape[0], DA_HEADS, DA_V_DIM)
        else:
            o_ref = {1: mk_ref, 6: dv_ref}[s]
            o_ref[...] = y.astype(o_ref.dtype)

    for w in range(PROJ_WARM):
        @pl.when(step == w)
        def _(w=w):
            if w == 0:
                hh_sc[...] = norm()
            for j in range(2):
                s = 2 * w + j
                if s < N_SEG:
                    w_sc[s] = wt_ref[j]
                    segment(s, hh_sc[...])
                else:
                    wg_sc[...] = jnp.zeros(wg_sc.shape, BF16)
                    wg_sc[0:N_GATE_COLS, :] = wt_ref[j, 0:N_GATE_COLS, :]
                    gates(hh_sc[...])

    @pl.when(step >= PROJ_WARM)
    def _():
        hh = norm()
        gates(hh)
        for s in range(N_SEG):
            segment(s, hh)


def _proj(x, row0, rows, mods, mod_index, g, w_seg, b_gate, bd, gq, gk, rope_tabs, seq_len):
    rope = rope_tabs is not None
    row_tile = lambda s: jnp.maximum(s - (PROJ_WARM - 1), 0)
    row = lambda s: (row_tile(s), 0)
    tile = pl.BlockSpec((PROJ_TM, D_MODEL), row)
    w_block = (2, D_MODEL, D_MODEL)
    in_specs = [pl.BlockSpec((PROJ_TM, D_MODEL), lambda s: (row0 // PROJ_TM + row_tile(s), 0)),
                _resident(mods.shape),
                _resident(g.shape),
                pl.BlockSpec(w_block, lambda s: (jnp.minimum(s, PROJ_WARM - 1), 0, 0)),
                _resident(b_gate.shape),
                _resident((MXU_DIM, MXU_DIM)),
                _resident(gq.shape),
                _resident(gk.shape)]
    args = [x, mods, g, w_seg, b_gate, bd, gq, gk]
    if rope:
        tiles_per_seq = seq_len // PROJ_TM
        tab = pl.BlockSpec((PROJ_TM, LANES), lambda s: (row_tile(s) % tiles_per_seq, 0))
        in_specs += [tab, tab]
        args += list(rope_tabs)
    dtypes = [BF16, BF16, BF16, F32, BF16, BF16, BF16 if rope else F32, F32, F32]
    out_shape = [jax.ShapeDtypeStruct((rows, D_MODEL), dt) for dt in dtypes]
    out_shape.append(jax.ShapeDtypeStruct((rows, GATE_PAD), F32))
    out_specs = [tile] * 9 + [pl.BlockSpec((PROJ_TM, GATE_PAD), row)]
    slabs = PROJ_TM // CHUNK
    for j in (0, 2):
        out_shape[j] = jax.ShapeDtypeStruct((rows // CHUNK, D_MODEL, CHUNK), BF16)
        out_specs[j] = pl.BlockSpec((slabs, D_MODEL, CHUNK), lambda s: (row_tile(s), 0, 0))
    if not rope:
        assert seq_len == PROJ_TM
        out_shape[5] = jax.ShapeDtypeStruct((rows // seq_len, D_MODEL, seq_len), F32)
        out_specs[5] = pl.BlockSpec((None, D_MODEL, seq_len), lambda s: (row_tile(s), 0, 0))
        out_shape[6] = jax.ShapeDtypeStruct((rows, DA_HEADS, DA_V_DIM), F32)
        out_specs[6] = pl.BlockSpec((PROJ_TM, DA_HEADS, DA_V_DIM), lambda s: (row_tile(s), 0, 0))
    return pl.pallas_call(
        functools.partial(_proj_kernel, rope=rope,
                          row_of_step=lambda step: mod_index(row0 + row_tile(step) * PROJ_TM)),
        grid=(PROJ_WARM - 1 + rows // PROJ_TM,),
        in_specs=in_specs,
        out_specs=out_specs,
        out_shape=out_shape,
        scratch_shapes=[pltpu.VMEM((N_SEG, D_MODEL, D_MODEL), BF16), pltpu.VMEM((GATE_PAD, D_MODEL), BF16),
                        pltpu.VMEM((PROJ_TM, D_MODEL), BF16)],
        compiler_params=_params(1),
        name="mixer_in_proj",
    )(*args)


def _lambda(lam_refs, lam_init):
    q1, k1, q2, k2 = (r[...] for r in lam_refs)
    s1 = jnp.sum(q1 * k1, axis=1, keepdims=True)
    s2 = jnp.sum(q2 * k2, axis=1, keepdims=True)
    return jnp.exp(s1) - jnp.exp(s2) + lam_init


def _attn_kernel(*refs, cached, seq, nb, lam_init):
    if cached:
        (q_ref, k_ref, v_ref, ck_ref, cv_ref), refs = refs[:5], refs[5:]
    else:
        (q_ref, k_ref, v_ref), refs = refs[:3], refs[3:]
    lam_refs, (gs_ref, o_ref, kall_sc, vt_sc, s_sc) = refs[:4], refs[4:]
    units = [(bb, h) for bb in range(nb) for h in range(DA_HEADS)]

    def cache_order_kv(kt_ref, vr_ref, u, h, lo, n):
        kall_sc[u, lo:lo + n, :] = kt_ref[h * DA_V_DIM:(h + 1) * DA_V_DIM, :].T.astype(BF16)
        vt_sc[u, :, lo:lo + n] = vr_ref[pl.ds(h, n, stride=DA_HEADS), :].T.astype(BF16)

    @pl.when(pl.program_id(1) == 0)
    def _():
        for u, (bb, h) in enumerate(units):
            if cached:
                cols = slice(h * DA_V_DIM, (h + 1) * DA_V_DIM)
                kall_sc[u, 0:seq, :] = k_ref[:, cols].astype(BF16)
                vt_sc[u, :, 0:seq] = v_ref[:, cols].astype(F32).T.astype(BF16)
                cache_order_kv(ck_ref.at[bb], cv_ref.at[bb], u, h, seq, ck_ref.shape[2])
            else:
                cache_order_kv(k_ref.at[bb], v_ref.at[bb], u, h, 0, seq)

    lam = _lambda(lam_refs, lam_init)
    lane = lax.broadcasted_iota(jnp.int32, (1, DA_V_DIM), 1)
    comp_masks = [lane < DA_HEAD_DIM, lane >= DA_HEAD_DIM]
    tq = q_ref.shape[0] // nb
    n_keys = kall_sc.shape[1]
    n_tiles = pl.cdiv(n_keys, ATT_TK_MAX)
    tk = n_keys // n_tiles

    def stacked_q(u):
        bb, h = units[u]
        q = q_ref[bb * tq:(bb + 1) * tq, h * DA_V_DIM:(h + 1) * DA_V_DIM].astype(BF16)
        return jnp.concatenate([jnp.where(m, q, jnp.zeros_like(q)) for m in comp_masks], axis=0)

    def score_tile(h, j, qq, m8):
        rows = slice(j * tk, (j + 1) * tk)
        st = _dot_nt(kall_sc[h, rows, :], qq)
        s_sc[h % 2, rows, :] = st
        t8 = jnp.max(st.reshape(tk // 8, 8, 2 * tq), axis=0)
        return t8 if m8 is None else jnp.maximum(m8, t8)

    def prob_tile(h, j, mx, d8, pv):
        rows = slice(j * tk, (j + 1) * tk)
        e = jnp.exp2(s_sc[h % 2, rows, :] - mx)
        s8 = jnp.sum(e.reshape(tk // 8, 8, 2 * tq), axis=0)
        p = _dot(vt_sc[h, :, rows], e)
        return (s8 if d8 is None else d8 + s8), (p if pv is None else pv + p)

    qq = stacked_q(0)
    m8 = None
    for j in range(n_tiles):
        m8 = score_tile(0, j, qq, m8)
    for h in range(len(units)):
        mx = jnp.max(m8, axis=0, keepdims=True)
        if h + 1 < len(units):
            qq = stacked_q(h + 1)
        m8, d8, pv = None, None, None
        for j in range(n_tiles):
            if h + 1 < len(units):
                m8 = score_tile(h + 1, j, qq, m8)
            d8, pv = prob_tile(h, j, mx, d8, pv)
        pv = pv * (1.0 / jnp.sum(d8, axis=0, keepdims=True))
        out_t = pv[:, :tq] - lam * pv[:, tq:]
        out_t = out_t * lax.rsqrt(jnp.mean(out_t * out_t, axis=0, keepdims=True) + EPS)
        out = out_t.T * gs_ref[...]
        bb, head = units[h]
        o_ref[bb * tq:(bb + 1) * tq, head * DA_V_DIM:(head + 1) * DA_V_DIM] = (
            out * (1.0 - lam_init)).astype(o_ref.dtype)


def _attn(q, k, v, cache, lam_vecs, g_sub, batch, seq_len, lam_init):
    rows = q.shape[0]
    nq = seq_len // ATT_TQ
    nb = ATT_SHORT_SEQ_BATCH if (cache is None and nq == 1) else 1
    assert batch % nb == 0
    q_spec = pl.BlockSpec((nb * ATT_TQ, D_MODEL), lambda b, i: (b * nq + i, 0))
    cache_specs = lambda n: [pl.BlockSpec((nb, D_MODEL, n), lambda b, i: (b, 0, 0)),
                             pl.BlockSpec((nb, n * DA_HEADS, DA_V_DIM), lambda b, i: (b, 0, 0))]
    n_keys = seq_len
    if cache is not None:
        past = cache[0].shape[2]
        n_keys += past
        kv_spec = pl.BlockSpec((seq_len, D_MODEL), lambda b, i: (b, 0))
        in_specs = [q_spec, kv_spec, kv_spec] + cache_specs(past)
        args = [q, k, v] + list(cache)
    else:
        in_specs = [q_spec] + cache_specs(seq_len)
        args = [q, k, v.reshape(batch, seq_len * DA_HEADS, DA_V_DIM)]
    in_specs += [_resident((1, DA_HEAD_DIM))] * 4 + [_resident((1, DA_V_DIM))]
    args += list(lam_vecs) + [g_sub]
    return pl.pallas_call(
        functools.partial(_attn_kernel, cached=cache is not None, seq=seq_len, nb=nb, lam_init=lam_init),
        grid=(batch // nb, nq),
        in_specs=in_specs,
        out_specs=q_spec,
        out_shape=jax.ShapeDtypeStruct((rows, D_MODEL), BF16),
        scratch_shapes=[pltpu.VMEM((nb * DA_HEADS, n_keys, DA_V_DIM), BF16),
                        pltpu.VMEM((nb * DA_HEADS, DA_V_DIM, n_keys), BF16),
                        pltpu.VMEM((2, n_keys, 2 * ATT_TQ), F32)],
        compiler_params=_params(2),
        name="diff_attention",
    )(*args)


def _per_chain(fn, a, b):
    return jnp.stack([fn(a[i], b[i]) for i in range(a.shape[0])])


ML_EXT = 8
ML_SHORT_SEQ_BATCH = 2


def _rows(x):
    return jnp.stack([x[i:i + 1, :] for i in range(x.shape[0])])


def _split3(x):
    hi = x.astype(BF16)
    r = x - hi.astype(F32)
    mid = r.astype(BF16)
    lo = (r - mid.astype(F32)).astype(BF16)
    return jnp.concatenate([hi, mid, lo], axis=1)


def _chunk_scan(x, reverse_rows, tri_prefix, tri_suffix):
    parts = _split3(x)
    return jnp.where(reverse_rows, _dot(parts, tri_suffix), _dot(parts, tri_prefix))


def _paired_value_matmul(v_t, sc):
    B, L, _ = sc.shape
    zero = jnp.zeros((L, L), sc.dtype)
    out = []
    for i in range(0, B, 2):
        lhs = jnp.concatenate([v_t[i], v_t[i + 1]], axis=1)
        rhs = jnp.concatenate([jnp.concatenate([sc[i], zero], axis=1),
                               jnp.concatenate([zero, sc[i + 1]], axis=1)], axis=0)
        both = _dot(lhs, rhs)
        out += [both[:, :L], both[:, L:]]
    return jnp.stack(out)


def _mlstm_step(k, q_t, v_t, ic, fc, CT, m, seen_t, reverse_rows):
    B, L = ic.shape
    d = k.shape[2]
    tri = lambda keep: jnp.concatenate([jnp.where(keep, 1.0, 0.0).astype(BF16)] * 3, axis=0)
    b2 = _chunk_scan(fc, reverse_rows, tri(seen_t[0]), tri(seen_t[B - 1]))
    u2 = ic - b2
    u_t = jnp.concatenate([u2, jnp.zeros((L - B, L), F32)], axis=0).T
    u_col = jnp.stack([jnp.broadcast_to(u_t[:, i:i + 1], (L, L)) for i in range(B)])
    b, i_g, f_g = _rows(b2), _rows(ic), _rows(fc)
    b_last = jnp.sum(f_g, axis=2, keepdims=True)

    log_d = jnp.where(seen_t, b + u_col, -jnp.inf)
    a = b + m
    m_t = jnp.maximum(a, jnp.max(log_d, axis=1, keepdims=True))
    dmat = jnp.exp(log_d - m_t)
    inter = jnp.exp(a - m_t)
    sc = _per_chain(_dot, k, q_t) * dmat
    cq = _per_chain(_dot, CT, q_t)
    num = _paired_value_matmul(v_t, sc.astype(BF16)) + inter * cq[:, :d, :]
    den = jnp.sum(sc, axis=1, keepdims=True) + inter * cq[:, d:d + 1, :]
    h_t = num * (1.0 / jnp.maximum(jnp.abs(den), jnp.exp(-m_t)))

    g = b_last - b + i_g
    m_new = jnp.maximum(b_last + m, jnp.max(g, axis=2, keepdims=True))
    w = jnp.exp(g - m_new)
    decay = jnp.exp(b_last + m - m_new)
    vw = jnp.concatenate([v_t.astype(F32), jnp.ones((B, ML_EXT, L), F32)], axis=1) * w
    CT_new = decay * CT + _per_chain(_dot, vw, k)
    return h_t, CT_new, m_new


def _mlstm_kernel(*refs, seq_len, nb, has_state, emit_state):
    q_ref, k_ref, v_ref, so_ref, g_ref, gmh_ref = refs[:6]
    refs = refs[6:]
    if has_state:
        (c0_ref, n0_ref, m0_ref), refs = refs[:3], refs[3:]
    hm_ref, refs = refs[0], refs[1:]
    if emit_state:
        (c_out_ref, n_out_ref, m_out_ref), refs = refs[:3], refs[3:]
    ct_sc, m_sc, gr_sc, h_sc = refs

    nc = seq_len // CHUNK
    d_head = ML_HEAD_DIM
    chains = [(bb, d, h) for bb in range(nb) for d in range(2) for h in range(ML_HEADS)]
    n_chain = len(chains)
    for i, (bb, d, h) in enumerate(chains):
        if has_state:
            ct_sc[i, 0:d_head, :] = c0_ref[bb, d, h].T
            ct_sc[i, d_head:d_head + ML_EXT, :] = jnp.broadcast_to(n0_ref[bb, d, h:h + 1, :], (ML_EXT, d_head))
            m_sc[i] = jnp.full((1, 1), m0_ref[pl.program_id(0) * nb + bb, 0, d, h], F32)
        else:
            ct_sc[i] = jnp.zeros((d_head + ML_EXT, d_head), F32)
            m_sc[i] = jnp.zeros((1, 1), F32)
    for c in range(nb * nc):
        gr_sc[c] = g_ref[c * CHUNK:(c + 1) * CHUNK, :].T

    s_idx = lax.broadcasted_iota(jnp.int32, (CHUNK, CHUNK), 0)
    t_idx = lax.broadcasted_iota(jnp.int32, (CHUNK, CHUNK), 1)
    seen_t = jnp.stack([(s_idx >= t_idx) if d else (s_idx <= t_idx) for _, d, _ in chains])
    reverse_rows = (lax.broadcasted_iota(jnp.int32, (n_chain, CHUNK), 0) // ML_HEADS) % 2 == 1

    def step(c_fwd, c_bwd, rows_of):
        chunk_of = (c_fwd, c_bwd)
        lo = 2 * ML_HEADS
        ic, fc = [], []
        for bb in range(nb):
            g_fwd, g_bwd = gr_sc[bb * nc + c_fwd], gr_sc[bb * nc + c_bwd]
            ic += [g_fwd[0:ML_HEADS], g_bwd[lo:lo + ML_HEADS]]
            fc += [g_fwd[ML_HEADS:lo], g_bwd[lo + ML_HEADS:2 * lo]]
        head = [slice(h * d_head, (h + 1) * d_head) for _, _, h in chains]
        stack = lambda pick: jnp.stack([pick(i, bb * nc + chunk_of[d]) for i, (bb, d, _) in enumerate(chains)])
        h_t, CT_new, m_new = _mlstm_step(
            stack(lambda i, c: k_ref[rows_of(c), head[i]]),
            stack(lambda i, c: q_ref[c, head[i], :]),
            stack(lambda i, c: v_ref[c, head[i], :]),
            jnp.concatenate(ic, axis=0), jnp.concatenate(fc, axis=0), ct_sc[...], m_sc[...], seen_t, reverse_rows)
        ct_sc[...] = CT_new
        m_sc[...] = m_new
        for i, (bb, d, _) in enumerate(chains):
            h_sc[d, bb * nc + chunk_of[d], head[i], :] = h_t[i]

    if nc <= 2:
        for c in range(nc):
            step(c, nc - 1 - c, lambda cc: slice(cc * CHUNK, (cc + 1) * CHUNK))
    else:
        def body(c, carry):
            step(c, nc - 1 - c, lambda cc: pl.ds(pl.multiple_of(cc * CHUNK, CHUNK), CHUNK))
            return carry
        lax.fori_loop(0, nc, body, 0, unroll=2)

    for h in range(ML_HEADS):
        hcols = slice(h * d_head, (h + 1) * d_head)
        gain = jnp.broadcast_to(gmh_ref[h:h + 1, :], (CHUNK, d_head)).T
        for c in range(nb * nc):
            rows = slice(c * CHUNK, (c + 1) * CHUNK)
            hsum = h_sc[0, c, hcols, :] + h_sc[1, c, hcols, :]
            hn = hsum * lax.rsqrt(jnp.mean(hsum * hsum, axis=0, keepdims=True) + EPS) * gain
            hm_ref[rows, hcols] = (hn.T * so_ref[rows, hcols]).astype(hm_ref.dtype)
    if emit_state:
        for i, (bb, d, h) in enumerate(chains):
            c_out_ref[bb, d, h] = ct_sc[i, 0:d_head, :].T
            n_out_ref[bb, d, h:h + 1, :] = ct_sc[i, d_head:d_head + 1, :]
            m_out_ref[bb, d:d + 1, h:h + 1] = m_sc[i]


def _mlstm(q_t, k, v_t, so, gates, g_mh, state, batch, seq_len, emit_state):
    rows = k.shape[0]
    d = ML_HEAD_DIM
    nc = seq_len // CHUNK
    nb = ML_SHORT_SEQ_BATCH if nc <= 2 else 1
    assert batch % nb == 0
    tile = pl.BlockSpec((nb * seq_len, D_MODEL), lambda b: (b, 0))
    tile_t = pl.BlockSpec((nb * nc, D_MODEL, CHUNK), lambda b: (b, 0, 0))
    in_specs = [tile_t, tile, tile_t, tile,
                pl.BlockSpec((nb * seq_len, GATE_PAD), lambda b: (b, 0)),
                _resident((ML_HEADS, d))]
    args = [q_t, k, v_t, so, gates, g_mh]
    c_spec = pl.BlockSpec((nb, None, 2, ML_HEADS, d, d), lambda b: (b, 0, 0, 0, 0, 0))
    n_spec = pl.BlockSpec((nb, None, 2, ML_HEADS, d), lambda b: (b, 0, 0, 0, 0))
    if state is not None:
        in_specs += [c_spec, n_spec, pl.BlockSpec(memory_space=pltpu.SMEM)]
        args += list(state)
    out_specs = [tile]
    out_shape = [jax.ShapeDtypeStruct((rows, D_MODEL), BF16)]
    if emit_state:
        out_specs += [c_spec, n_spec, pl.BlockSpec((nb, None, 2, ML_HEADS), lambda b: (b, 0, 0, 0))]
        out_shape += [jax.ShapeDtypeStruct((batch, 1, 2, ML_HEADS, d, d), F32),
                      jax.ShapeDtypeStruct((batch, 1, 2, ML_HEADS, d), F32),
                      jax.ShapeDtypeStruct((batch, 1, 2, ML_HEADS), F32)]
    n_state = 2 * ML_HEADS * nb
    return pl.pallas_call(
        functools.partial(_mlstm_kernel, seq_len=seq_len, nb=nb, has_state=state is not None,
                          emit_state=emit_state),
        grid=(batch // nb,),
        in_specs=in_specs,
        out_specs=out_specs,
        out_shape=out_shape,
        scratch_shapes=[pltpu.VMEM((n_state, d + ML_EXT, d), F32), pltpu.VMEM((n_state, 1, 1), F32),
                        pltpu.VMEM((nb * nc, GATE_PAD, CHUNK), F32),
                        pltpu.VMEM((2, nb * nc, D_MODEL, CHUNK), F32)],
        compiler_params=_params(1),
        name="mlstm",
    )(*args)


def _merge_kernel(*refs, tiles, mod_index):
    n = len(tiles)
    (x_ref, m_ref), refs = refs[:2], refs[2:]
    parts = [refs[j * n:(j + 1) * n] for j in range(4)]
    wm_ref, wd_ref, wo_ref, o_ref, w_sc = refs[4 * n:]
    i = pl.program_id(0)

    @pl.when(i == 0)
    def _():
        for j, w_ref in enumerate((wm_ref, wd_ref, wo_ref)):
            w_sc[j] = w_ref[...].astype(BF16)

    def tile(hm_ref, att_ref, sgm_ref, sgd_ref):
        y = sgm_ref[...] * _dot(hm_ref[...], w_sc[0]) + sgd_ref[...] * _dot(att_ref[...], w_sc[1])
        o_ref[...] = x_ref[...] + _mod_reader(m_ref, mod_index(i * MERGE_TM))(5) * _dot(y, w_sc[2])

    starts = np.cumsum((0,) + tiles).tolist()
    for k in range(n):
        pl.when((i >= starts[k]) & (i < starts[k + 1]))(functools.partial(tile, *[p[k] for p in parts]))


def _merge(x, mods, mod_index, hm, att, sgm, sgd, wm, wd, wo):
    rows = x.shape[0]
    tiles = tuple(h.shape[0] // MERGE_TM for h in hm)
    assert sum(tiles) * MERGE_TM == rows
    tile = pl.BlockSpec((MERGE_TM, D_MODEL), lambda i: (i, 0))

    def part_spec(k):
        start = sum(tiles[:k])
        return pl.BlockSpec((MERGE_TM, D_MODEL), lambda i: (jnp.clip(i - start, 0, tiles[k] - 1), 0))

    part_specs = [part_spec(k) for k in range(len(tiles))]
    w_spec = _resident((D_MODEL, D_MODEL))
    return pl.pallas_call(
        functools.partial(_merge_kernel, tiles=tiles, mod_index=mod_index),
        grid=(rows // MERGE_TM,),
        in_specs=[tile, _resident(mods.shape)]
                 + part_specs * 4 + [w_spec, w_spec, w_spec],
        out_specs=tile,
        out_shape=jax.ShapeDtypeStruct((rows, D_MODEL), F32),
        scratch_shapes=[pltpu.VMEM((3, D_MODEL, D_MODEL), BF16)],
        compiler_params=_params(1),
        name="branch_merge",
    )(x, mods, *hm, *att, *sgm, *sgd, wm, wd, wo)


def _rope_tables(seq_len):
    lane = np.arange(LANES)
    r = lane % 32
    freqs = np.power(np.float32(ROPE_BASE), -(r % 16).astype(np.float32) / np.float32(16.0))
    tok = np.arange(seq_len)
    pos = np.where((lane % 64 < 32)[None, :], (tok // GRID_W)[:, None], (tok % GRID_W)[:, None]).astype(np.float32)
    ang = pos * freqs[None, :]
    sign = np.where(r < 16, -1.0, 1.0).astype(np.float32)
    return jnp.asarray(np.cos(ang), F32), jnp.asarray(np.sin(ang) * sign[None, :], F32)


def _mixer_branches(x, row0, mods, mod_index, w, batch, seq_len, ctx, lam_init):
    rope_tabs = None if ctx is None else _rope_tables(seq_len)
    mq, mk, mv, so, dq, dk, dv, sgm, sgd, gates = _proj(
        x, row0, batch * seq_len, mods, mod_index, w["g_norm"], w["w_seg"], w["b_gate"], w["bd"], w["g_qn"],
        w["g_kn"], rope_tabs, seq_len)
    cache = None if ctx is None else (ctx[0], ctx[1])
    att = _attn(dq, dk, dv, cache, w["lam"], w["g_sub"], batch, seq_len, lam_init)
    state = None if ctx is None else ctx[2]
    res = _mlstm(mq, mk, mv, so, gates, w["g_mh"], state, batch, seq_len, emit_state=ctx is None)
    return res[0], att, sgm, sgd, dk, dv, res[1:]


def kernel(x_prompt, x_sample, c, cache_k, cache_v, state_C, state_n, state_m, c_ctx, w_ada, b_ada, g_norm, ffn1_w1, ffn1_w3, ffn1_w2, ffn2_w1, ffn2_w3, ffn2_w2, w_in, b_gate, g_qn, g_kn, lam_q1, lam_k1, lam_q2, lam_k2, g_sub, g_mh, w_br_m, w_br_d, w_out):
    depth = w_ada.shape[0]
    assert depth == 1
    l = 0
    bp, tp, _ = x_prompt.shape
    bs, ts, _ = x_sample.shape
    past = cache_k.shape[2]
    lam_init = 0.8 - 0.6 * math.exp(-0.3 * l)

    cvecs = jnp.concatenate([c_ctx[None, :], c, jnp.zeros((8 - 1 - bs, D_MODEL), F32)], axis=0)
    mods = _mods(cvecs, w_ada[l], b_ada[l])

    group = np.arange(MXU_DIM) // DA_HEAD_DIM
    w = dict(
        g_norm=g_norm[l],
        ffn1_w1=ffn1_w1[l], ffn1_w3=ffn1_w3[l], ffn1_w2=ffn1_w2[l],
        ffn2_w1=ffn2_w1[l], ffn2_w3=ffn2_w3[l], ffn2_w2=ffn2_w2[l],
        w_in_t=w_in[l].T,
        b_gate=b_gate[l:l + 1],
        bd=jnp.asarray(group[:, None] == group[None, :], BF16),
        g_qn=g_qn[l:l + 1], g_kn=g_kn[l:l + 1],
        lam=(lam_q1[l:l + 1], lam_k1[l:l + 1], lam_q2[l:l + 1], lam_k2[l:l + 1]),
        g_sub=g_sub[l:l + 1], g_mh=g_mh[l],
        w_br_m=w_br_m[l], w_br_d=w_br_d[l], w_out=w_out[l],
    )

    n_ctx, n_lat = bp * tp, bs * ts
    mod_index = lambda r: jnp.where(r < n_ctx, 0, 1 + (r - n_ctx) // ts)
    seg_start = lambda j: pl.multiple_of(jnp.where(j < N_SEG, _seg_start(j), GATE_LO), N_GATE_COLS)
    x1, w["w_seg"] = _ffn((x_prompt.reshape(n_ctx, D_MODEL), x_sample.reshape(n_lat, D_MODEL)), (n_ctx + n_lat,),
                          mods, mod_index, w["g_norm"], w["ffn1_w1"], w["ffn1_w3"], w["ffn1_w2"], base=0,
                          side=(w["w_in_t"], N_SEG + 1, seg_start))

    hm_p, att_p, sgm_p, sgd_p, new_k_t, new_v, (new_c, new_n, new_m) = _mixer_branches(
        x1, 0, mods, mod_index, w, bp, tp, None, lam_init)

    ctx = (cache_k[:, l].transpose(0, 2, 3, 4, 1).reshape(bs, D_MODEL, past),
           cache_v[:, l].reshape(bs, past * DA_HEADS, DA_V_DIM),
           (state_C, state_n, state_m))
    hm_s, att_s, sgm_s, sgd_s, _, _, _ = _mixer_branches(x1, n_ctx, mods, mod_index, w, bs, ts, ctx, lam_init)

    x2 = _merge(x1, mods, mod_index, (hm_p, hm_s), (att_p, att_s), (sgm_p, sgm_s), (sgd_p, sgd_s),
                w["w_br_m"], w["w_br_d"], w["w_out"])
    xp, xs = _ffn((x2,), (n_ctx, n_lat), mods, mod_index, w["g_norm"], w["ffn2_w1"], w["ffn2_w3"], w["ffn2_w2"],
                  base=6)

    return (xp.reshape(bp, tp, D_MODEL), xs.reshape(bs, ts, D_MODEL),
            new_k_t.reshape(bp, DA_HEADS, 2, DA_HEAD_DIM, tp).transpose(0, 4, 1, 2, 3)[:, None],
            new_v.reshape(bp, 1, tp, DA_HEADS, DA_V_DIM),
            new_c, new_n, new_m)
```

```python
import functools
import math

import jax
import jax.numpy as jnp
import numpy as np
from jax import lax
from jax.experimental import pallas as pl
from jax.experimental.pallas import tpu as pltpu

F32 = jnp.float32
BF16 = jnp.bfloat16

D_MODEL = 1024
D_FF = 2816
N_MOD = 9
GRID_W = 64
ML_HEADS = 4
ML_HEAD_DIM = 256
DA_HEADS = 8
DA_HEAD_DIM = 64
DA_V_DIM = 128
N_GATE_COLS = 16
CHUNK = 128
ROPE_BASE = 10000.0
QK_LOG2_SCALE = DA_HEAD_DIM ** -0.5 * math.log2(math.e)
EPS = 1e-6

LANES = 128
GATE_PAD = LANES
MXU_DIM = 256
VMEM_LIMIT = 56 * 1024 * 1024

MOD_ROWS = 8
FFN_TM = 512
FFN_TF = 256
PROJ_TM = 256
MERGE_TM = 512
ATT_TQ = 256
ATT_TK_MAX = 768
ATT_SHORT_SEQ_BATCH = 2


def _params(n_axes):
    return pltpu.CompilerParams(dimension_semantics=("arbitrary",) * n_axes,
                                vmem_limit_bytes=VMEM_LIMIT)


def _dot(a, b):
    return jnp.dot(a.astype(BF16), b.astype(BF16), preferred_element_type=F32)


def _dot_nt(a, b):
    return lax.dot_general(a.astype(BF16), b.astype(BF16), (((1,), (1,)), ((), ())),
                           preferred_element_type=F32)


def _sigmoid(x):
    return 1.0 / (1.0 + jnp.exp(-x))


def _log_sigmoid(x):
    return jnp.minimum(x, 0.0) - jnp.log1p(jnp.exp(-jnp.abs(x)))


def _modulated_norm(x, g, shift, scale):
    y = x * lax.rsqrt(jnp.mean(x * x, axis=-1, keepdims=True) + EPS) * g
    return y * (1.0 + scale) + shift


def _resident(shape):
    return pl.BlockSpec(shape, lambda *_: (0,) * len(shape), pipeline_mode=pl.Buffered(1))


def _mods_kernel(c_ref, w_ref, b_ref, o_ref):
    c = c_ref[...]
    o_ref[...] = _dot(c * _sigmoid(c), w_ref[...]) + b_ref[...]


def _mods(cvecs, w_ada, b_ada):
    n = N_MOD * D_MODEL
    tn = D_MODEL
    return pl.pallas_call(
        _mods_kernel,
        grid=(n // tn,),
        in_specs=[pl.BlockSpec((MOD_ROWS, D_MODEL), lambda j: (0, 0)),
                  pl.BlockSpec((D_MODEL, tn), lambda j: (0, j)),
                  pl.BlockSpec((1, tn), lambda j: (0, j))],
        out_specs=pl.BlockSpec((MOD_ROWS, tn), lambda j: (0, j)),
        out_shape=jax.ShapeDtypeStruct((MOD_ROWS, n), F32),
        compiler_params=_params(1),
        name="adaln_mods",
    )(cvecs, w_ada, b_ada.reshape(1, n))


def _mod_reader(m_ref, row):
    return lambda j: m_ref[pl.ds(row, 1), j * D_MODEL:(j + 1) * D_MODEL]


def _ffn_kernel(*refs, base, mod_index, in_tiles, out_tiles, side_blocks):
    n_in, n_out = len(in_tiles), len(out_tiles)
    x_refs, refs = refs[:n_in], refs[n_in:]
    (m_ref, g_ref, w1_ref, w3_ref, w2_ref), refs = refs[:5], refs[5:]
    if side_blocks:
        side_in_ref, refs = refs[0], refs[1:]
        side_out_ref, refs = refs[n_out], refs[:n_out] + refs[n_out + 1:]
    o_refs, (w1_sc, w3_sc, w2_sc, hh_sc, acc_sc) = refs[:n_out], refs[n_out:]
    g = pl.program_id(0)
    nf = D_FF // FFN_TF
    row_tile = g - (nf - 1)
    mod = _mod_reader(m_ref, mod_index(jnp.maximum(row_tile, 0) * FFN_TM))
    gain = g_ref[base // 3:base // 3 + 1, :]
    norm = lambda x: _modulated_norm(x, gain, mod(base), mod(base + 1)).astype(BF16)
    finish = lambda x, acc: x + 0.5 * mod(base + 2) * acc

    def tile(hh, f):
        a = _dot(hh, w1_sc[f])
        b = _dot(hh, w3_sc[f])
        return _dot(a * _sigmoid(a) * b, w2_sc[f])

    @pl.when(g < nf)
    def _():
        w1_sc[g] = w1_ref[...].astype(BF16)
        w3_sc[g] = w3_ref[...].astype(BF16)
        w2_sc[g] = w2_ref[...].astype(BF16)

        @pl.when(g == 0)
        def _():
            hh_sc[...] = norm(x_refs[0][...])
            acc_sc[...] = jnp.zeros(acc_sc.shape, F32)

        acc_sc[...] += tile(hh_sc[...], g)

        @pl.when(g == nf - 1)
        def _():
            o_refs[0][...] = finish(x_refs[0][...], acc_sc[...])

    if side_blocks:
        @pl.when((g >= nf) & (g - nf < side_blocks))
        def _():
            side_out_ref[...] = side_in_ref[...].astype(BF16)

    def full_tile(x_ref, o_ref):
        x = x_ref[...]
        hh = norm(x)
        acc = jnp.zeros(x.shape, F32)
        for f in range(nf):
            acc = acc + tile(hh, f)
        o_ref[...] = finish(x, acc)

    bounds = sorted(set(np.cumsum((0,) + in_tiles).tolist()) | set(np.cumsum((0,) + out_tiles).tolist()))
    for lo, hi in zip(bounds[:-1], bounds[1:]):
        k_in = int(np.searchsorted(np.cumsum(in_tiles), lo, side="right"))
        k_out = int(np.searchsorted(np.cumsum(out_tiles), lo, side="right"))
        pl.when((g >= nf) & (row_tile >= lo) & (row_tile < hi))(
            functools.partial(full_tile, x_refs[k_in], o_refs[k_out]))


def _ffn(xs, out_rows, mods, mod_index, g, w1, w3, w2, base, side=None):
    nf = D_FF // FFN_TF
    in_tiles = tuple(x.shape[0] // FFN_TM for x in xs)
    out_tiles = tuple(r // FFN_TM for r in out_rows)
    assert sum(in_tiles) == sum(out_tiles)
    side_in_specs, side_out_specs, side_out_shape, side_args, side_blocks = [], [], [], [], 0
    if side is not None:
        side_w, side_blocks, side_start = side
        assert side_blocks <= sum(in_tiles) - 1
        blk = lambda s: jnp.clip(s - nf, 0, side_blocks - 1)
        side_in_specs = [pl.BlockSpec((pl.Element(D_MODEL), pl.Element(D_MODEL)), lambda s: (side_start(blk(s)), 0))]
        side_out_specs = [pl.BlockSpec((None, D_MODEL, D_MODEL), lambda s: (blk(s), 0, 0))]
        side_out_shape = [jax.ShapeDtypeStruct((side_blocks, D_MODEL, D_MODEL), BF16)]
        side_args = [side_w]
    row_tile = lambda s: jnp.maximum(s - (nf - 1), 0)
    f_tile = lambda s: jnp.minimum(s, nf - 1)

    def part_spec(tiles, k):
        start = sum(tiles[:k])
        return pl.BlockSpec((FFN_TM, D_MODEL), lambda s: (jnp.clip(row_tile(s) - start, 0, tiles[k] - 1), 0))

    return pl.pallas_call(
        functools.partial(_ffn_kernel, base=base, mod_index=mod_index, in_tiles=in_tiles, out_tiles=out_tiles,
                          side_blocks=side_blocks),
        grid=(nf - 1 + sum(in_tiles),),
        in_specs=[part_spec(in_tiles, k) for k in range(len(xs))] + [
            _resident(mods.shape),
            _resident(g.shape),
            pl.BlockSpec((D_MODEL, FFN_TF), lambda s: (0, f_tile(s))),
            pl.BlockSpec((D_MODEL, FFN_TF), lambda s: (0, f_tile(s))),
            pl.BlockSpec((FFN_TF, D_MODEL), lambda s: (f_tile(s), 0))] + side_in_specs,
        out_specs=[part_spec(out_tiles, k) for k in range(len(out_rows))] + side_out_specs,
        out_shape=[jax.ShapeDtypeStruct((r, D_MODEL), F32) for r in out_rows] + side_out_shape,
        scratch_shapes=[pltpu.VMEM((nf, D_MODEL, FFN_TF), BF16), pltpu.VMEM((nf, D_MODEL, FFN_TF), BF16),
                        pltpu.VMEM((nf, FFN_TF, D_MODEL), BF16),
                        pltpu.VMEM((FFN_TM, D_MODEL), BF16), pltpu.VMEM((FFN_TM, D_MODEL), F32)],
        compiler_params=_params(1),
        name="ffn",
    )(*xs, mods, g, w1, w3, w2, *side_args)


def _group_norm64(x, bd, g):
    ss = _dot(x * x, bd)
    return x * lax.rsqrt(ss * (1.0 / DA_HEAD_DIM) + EPS) * g


def _rope(x, cos, sin_signed):
    first = (lax.broadcasted_iota(jnp.int32, x.shape, 1) % 32) < 16
    partner = jnp.where(first, pltpu.roll(x, LANES - 16, 1), pltpu.roll(x, 16, 1))
    return x * cos + partner * sin_signed


N_SEG = 9
PROJ_WARM = (N_SEG + 1) // 2
GATE_LO = 4 * D_MODEL


def _seg_start(s):
    return s * D_MODEL + N_GATE_COLS * (s >= 4)


def _proj_kernel(*refs, rope, row_of_step):
    (x_ref, m_ref, g_ref, wt_ref, bg_ref, bd_ref, gq_ref, gk_ref), refs = refs[:8], refs[8:]
    if rope:
        (cos_ref, sin_ref), refs = refs[:2], refs[2:]
    (mq_ref, mk_ref, mv_ref, so_ref, dq_ref, dk_ref, dv_ref, sgm_ref, sgd_ref, gates_ref,
     w_sc, wg_sc, hh_sc) = refs
    step = pl.program_id(0)
    mod = _mod_reader(m_ref, row_of_step(step))
    norm = lambda: _modulated_norm(x_ref[...], g_ref[1:2, :], mod(3), mod(4)).astype(BF16)
    groups = MXU_DIM // DA_HEAD_DIM
    tiled = lambda ref, scale: jnp.concatenate([ref[...] * scale] * groups, axis=1)
    gq_row, gk_row = tiled(gq_ref, QK_LOG2_SCALE), tiled(gk_ref, 1.0)
    bias = jnp.concatenate([bg_ref[...], jnp.zeros((1, GATE_PAD - N_GATE_COLS), F32)], axis=1)

    def gates(hh):
        y = _dot_nt(hh, wg_sc[...]) + bias
        col = lax.broadcasted_iota(jnp.int32, y.shape, 1)
        gates_ref[...] = jnp.where((col // ML_HEADS) % 2 == 1, _log_sigmoid(y), y)

    def qk_segment(y, gain_row, o_ref):
        bd = bd_ref[...]
        for c in range(D_MODEL // MXU_DIM):
            cols = slice(c * MXU_DIM, (c + 1) * MXU_DIM)
            z = _group_norm64(y[:, cols], bd, gain_row)
            if rope:
                z = jnp.concatenate(
                    [_rope(z[:, k * LANES:(k + 1) * LANES], cos_ref[...], sin_ref[...])
                     for k in range(MXU_DIM // LANES)], axis=1)
            o_ref[:, cols] = z.astype(o_ref.dtype)

    def cache_key_segment(hh):
        y_t = _dot_nt(w_sc[5], hh)
        z = y_t.reshape(D_MODEL // DA_HEAD_DIM, DA_HEAD_DIM, y_t.shape[1])
        ms = jnp.mean(z * z, axis=1, keepdims=True)
        gain = jnp.broadcast_to(gk_row[:, 0:LANES], (LANES, LANES)).T[0:DA_HEAD_DIM, :]
        gain = jnp.concatenate([gain] * (y_t.shape[1] // LANES), axis=1)
        dk_ref[...] = (z * lax.rsqrt(ms + EPS) * gain).reshape(y_t.shape)

    def segment(s, hh):
        if s in (0, 2):
            o_ref = {0: mq_ref, 2: mv_ref}[s]
            y_t = _dot_nt(w_sc[s], hh) * ((ML_HEAD_DIM ** -0.5) if s == 0 else 1.0)
            for j in range(PROJ_TM // CHUNK):
                o_ref[j] = y_t[:, j * CHUNK:(j + 1) * CHUNK].astype(o_ref.dtype)
            return
        if s == 5 and not rope:
            cache_key_segment(hh)
            return
        y = _dot_nt(hh, w_sc[s])
        if s == 4:
            qk_segment(y, gq_row, dq_ref)
        elif s == 5:
            qk_segment(y, gk_row, dk_ref)
        elif s in (3, 7, 8):
            {3: so_ref, 7: sgm_ref, 8: sgd_ref}[s][...] = _sigmoid(y)
        elif s == 6 and not rope:
            dv_ref[...] = y.reshape(y.shape[0], DA_HEADS, DA_V_DIM)
        else:
            o_ref = {1: mk_ref, 6: dv_ref}[s]
            o_ref[...] = y.astype(o_ref.dtype)

    for w in range(PROJ_WARM):
        @pl.when(step == w)
        def _(w=w):
            if w == 0:
                hh_sc[...] = norm()
            for j in range(2):
                s = 2 * w + j
                if s < N_SEG:
                    w_sc[s] = wt_ref[j]
                    segment(s, hh_sc[...])
                else:
                    wg_sc[...] = jnp.zeros(wg_sc.shape, BF16)
                    wg_sc[0:N_GATE_COLS, :] = wt_ref[j, 0:N_GATE_COLS, :]
                    gates(hh_sc[...])

    @pl.when(step >= PROJ_WARM)
    def _():
        hh = norm()
        gates(hh)
        for s in range(N_SEG):
            segment(s, hh)


def _proj(x, row0, rows, mods, mod_index, g, w_seg, b_gate, bd, gq, gk, rope_tabs, seq_len):
    rope = rope_tabs is not None
    row_tile = lambda s: jnp.maximum(s - (PROJ_WARM - 1), 0)
    row = lambda s: (row_tile(s), 0)
    tile = pl.BlockSpec((PROJ_TM, D_MODEL), row)
    w_block = (2, D_MODEL, D_MODEL)
    in_specs = [pl.BlockSpec((PROJ_TM, D_MODEL), lambda s: (row0 // PROJ_TM + row_tile(s), 0)),
                _resident(mods.shape),
                _resident(g.shape),
                pl.BlockSpec(w_block, lambda s: (jnp.minimum(s, PROJ_WARM - 1), 0, 0)),
                _resident(b_gate.shape),
                _resident((MXU_DIM, MXU_DIM)),
                _resident(gq.shape),
                _resident(gk.shape)]
    args = [x, mods, g, w_seg, b_gate, bd, gq, gk]
    if rope:
        tiles_per_seq = seq_len // PROJ_TM
        tab = pl.BlockSpec((PROJ_TM, LANES), lambda s: (row_tile(s) % tiles_per_seq, 0))
        in_specs += [tab, tab]
        args += list(rope_tabs)
    dtypes = [BF16, BF16, BF16, F32, BF16, BF16, BF16 if rope else F32, F32, F32]
    out_shape = [jax.ShapeDtypeStruct((rows, D_MODEL), dt) for dt in dtypes]
    out_shape.append(jax.ShapeDtypeStruct((rows, GATE_PAD), F32))
    out_specs = [tile] * 9 + [pl.BlockSpec((PROJ_TM, GATE_PAD), row)]
    slabs = PROJ_TM // CHUNK
    for j in (0, 2):
        out_shape[j] = jax.ShapeDtypeStruct((rows // CHUNK, D_MODEL, CHUNK), BF16)
        out_specs[j] = pl.BlockSpec((slabs, D_MODEL, CHUNK), lambda s: (row_tile(s), 0, 0))
    if not rope:
        assert seq_len == PROJ_TM
        out_shape[5] = jax.ShapeDtypeStruct((rows // seq_len, D_MODEL, seq_len), F32)
        out_specs[5] = pl.BlockSpec((None, D_MODEL, seq_len), lambda s: (row_tile(s), 0, 0))
        out_shape[6] = jax.ShapeDtypeStruct((rows, DA_HEADS, DA_V_DIM), F32)
        out_specs[6] = pl.BlockSpec((PROJ_TM, DA_HEADS, DA_V_DIM), lambda s: (row_tile(s), 0, 0))
    return pl.pallas_call(
        functools.partial(_proj_kernel, rope=rope,
                          row_of_step=lambda step: mod_index(row0 + row_tile(step) * PROJ_TM)),
        grid=(PROJ_WARM - 1 + rows // PROJ_TM,),
        in_specs=in_specs,
        out_specs=out_specs,
        out_shape=out_shape,
        scratch_shapes=[pltpu.VMEM((N_SEG, D_MODEL, D_MODEL), BF16), pltpu.VMEM((GATE_PAD, D_MODEL), BF16),
                        pltpu.VMEM((PROJ_TM, D_MODEL), BF16)],
        compiler_params=_params(1),
        name="mixer_in_proj",
    )(*args)


def _lambda(lam_refs, lam_init):
    q1, k1, q2, k2 = (r[...] for r in lam_refs)
    s1 = jnp.sum(q1 * k1, axis=1, keepdims=True)
    s2 = jnp.sum(q2 * k2, axis=1, keepdims=True)
    return jnp.exp(s1) - jnp.exp(s2) + lam_init


def _attn_kernel(*refs, cached, seq, nb, lam_init):
    if cached:
        (q_ref, k_ref, v_ref, ck_ref, cv_ref), refs = refs[:5], refs[5:]
    else:
        (q_ref, k_ref, v_ref), refs = refs[:3], refs[3:]
    lam_refs, (gs_ref, o_ref, kall_sc, vt_sc, s_sc) = refs[:4], refs[4:]
    units = [(bb, h) for bb in range(nb) for h in range(DA_HEADS)]

    def cache_order_kv(kt_ref, vr_ref, u, h, lo, n):
        kall_sc[u, lo:lo + n, :] = kt_ref[h * DA_V_DIM:(h + 1) * DA_V_DIM, :].T.astype(BF16)
        vt_sc[u, :, lo:lo + n] = vr_ref[pl.ds(h, n, stride=DA_HEADS), :].T.astype(BF16)

    @pl.when(pl.program_id(1) == 0)
    def _():
        for u, (bb, h) in enumerate(units):
            if cached:
                cols = slice(h * DA_V_DIM, (h + 1) * DA_V_DIM)
                kall_sc[u, 0:seq, :] = k_ref[:, cols].astype(BF16)
                vt_sc[u, :, 0:seq] = v_ref[:, cols].astype(F32).T.astype(BF16)
                cache_order_kv(ck_ref.at[bb], cv_ref.at[bb], u, h, seq, ck_ref.shape[2])
            else:
                cache_order_kv(k_ref.at[bb], v_ref.at[bb], u, h, 0, seq)

    lam = _lambda(lam_refs, lam_init)
    sub_gain = gs_ref[...] * (1.0 - lam_init)
    lane = lax.broadcasted_iota(jnp.int32, (1, DA_V_DIM), 1)
    comp_masks = [lane < DA_HEAD_DIM, lane >= DA_HEAD_DIM]
    tq = q_ref.shape[0] // nb
    n_keys = kall_sc.shape[1]
    n_tiles = pl.cdiv(n_keys, ATT_TK_MAX)
    tk = n_keys // n_tiles

    def stacked_q(u):
        bb, h = units[u]
        q = q_ref[bb * tq:(bb + 1) * tq, h * DA_V_DIM:(h + 1) * DA_V_DIM].astype(BF16)
        return jnp.concatenate([jnp.where(m, q, jnp.zeros_like(q)) for m in comp_masks], axis=0)

    def score_tile(h, j, qq, m8):
        rows = slice(j * tk, (j + 1) * tk)
        st = _dot_nt(kall_sc[h, rows, :], qq)
        s_sc[h % 2, rows, :] = st
        t8 = jnp.max(st.reshape(tk // 8, 8, 2 * tq), axis=0)
        return t8 if m8 is None else jnp.maximum(m8, t8)

    def prob_tile(h, j, mx, d8, pv):
        rows = slice(j * tk, (j + 1) * tk)
        e = jnp.exp2(s_sc[h % 2, rows, :] - mx)
        s8 = jnp.sum(e.reshape(tk // 8, 8, 2 * tq), axis=0)
        p = _dot(vt_sc[h, :, rows], e)
        return (s8 if d8 is None else d8 + s8), (p if pv is None else pv + p)

    qq = stacked_q(0)
    m8 = None
    for j in range(n_tiles):
        m8 = score_tile(0, j, qq, m8)
    for h in range(len(units)):
        mx = jnp.max(m8, axis=0, keepdims=True)
        if h + 1 < len(units):
            qq = stacked_q(h + 1)
        m8, d8, pv = None, None, None
        for j in range(n_tiles):
            if h + 1 < len(units):
                m8 = score_tile(h + 1, j, qq, m8)
            d8, pv = prob_tile(h, j, mx, d8, pv)
        inv = 1.0 / jnp.sum(d8, axis=0, keepdims=True)
        out_t = pv[:, :tq] * inv[:, :tq] - pv[:, tq:] * (lam * inv[:, tq:])
        out_t = out_t * lax.rsqrt(jnp.mean(out_t * out_t, axis=0, keepdims=True) + EPS)
        bb, head = units[h]
        o_ref[bb * tq:(bb + 1) * tq, head * DA_V_DIM:(head + 1) * DA_V_DIM] = (out_t.T * sub_gain).astype(o_ref.dtype)


def _attn(q, k, v, cache, lam_vecs, g_sub, batch, seq_len, lam_init):
    rows = q.shape[0]
    nq = seq_len // ATT_TQ
    nb = ATT_SHORT_SEQ_BATCH if (cache is None and nq == 1) else 1
    assert batch % nb == 0
    q_spec = pl.BlockSpec((nb * ATT_TQ, D_MODEL), lambda b, i: (b * nq + i, 0))
    cache_specs = lambda n: [pl.BlockSpec((nb, D_MODEL, n), lambda b, i: (b, 0, 0)),
                             pl.BlockSpec((nb, n * DA_HEADS, DA_V_DIM), lambda b, i: (b, 0, 0))]
    n_keys = seq_len
    if cache is not None:
        past = cache[0].shape[2]
        n_keys += past
        kv_spec = pl.BlockSpec((seq_len, D_MODEL), lambda b, i: (b, 0))
        in_specs = [q_spec, kv_spec, kv_spec] + cache_specs(past)
        args = [q, k, v] + list(cache)
    else:
        in_specs = [q_spec] + cache_specs(seq_len)
        args = [q, k, v.reshape(batch, seq_len * DA_HEADS, DA_V_DIM)]
    in_specs += [_resident((1, DA_HEAD_DIM))] * 4 + [_resident((1, DA_V_DIM))]
    args += list(lam_vecs) + [g_sub]
    return pl.pallas_call(
        functools.partial(_attn_kernel, cached=cache is not None, seq=seq_len, nb=nb, lam_init=lam_init),
        grid=(batch // nb, nq),
        in_specs=in_specs,
        out_specs=q_spec,
        out_shape=jax.ShapeDtypeStruct((rows, D_MODEL), BF16),
        scratch_shapes=[pltpu.VMEM((nb * DA_HEADS, n_keys, DA_V_DIM), BF16),
                        pltpu.VMEM((nb * DA_HEADS, DA_V_DIM, n_keys), BF16),
                        pltpu.VMEM((2, n_keys, 2 * ATT_TQ), F32)],
        compiler_params=_params(2),
        name="diff_attention",
    )(*args)


def _per_chain(fn, a, b):
    return jnp.stack([fn(a[i], b[i]) for i in range(a.shape[0])])


ML_EXT = 8
ML_SHORT_SEQ_BATCH = 2


def _rows(x):
    return jnp.stack([x[i:i + 1, :] for i in range(x.shape[0])])


def _split3(x):
    hi = x.astype(BF16)
    r = x - hi.astype(F32)
    mid = r.astype(BF16)
    lo = (r - mid.astype(F32)).astype(BF16)
    return jnp.concatenate([hi, mid, lo], axis=1)


def _chunk_scan(x, reverse_rows, tri_prefix, tri_suffix):
    parts = _split3(x)
    return jnp.where(reverse_rows, _dot(parts, tri_suffix), _dot(parts, tri_prefix))


def _paired_value_matmul(v_t, sc):
    B, L, _ = sc.shape
    zero = jnp.zeros((L, L), sc.dtype)
    out = []
    for i in range(0, B, 2):
        lhs = jnp.concatenate([v_t[i], v_t[i + 1]], axis=1)
        rhs = jnp.concatenate([jnp.concatenate([sc[i], zero], axis=1),
                               jnp.concatenate([zero, sc[i + 1]], axis=1)], axis=0)
        both = _dot(lhs, rhs)
        out += [both[:, :L], both[:, L:]]
    return jnp.stack(out)


def _mlstm_step(k, q_t, v_t, ic, fc, CT, m, seen_t, reverse_rows):
    B, L = ic.shape
    d = k.shape[2]
    tri = lambda keep: jnp.concatenate([jnp.where(keep, 1.0, 0.0).astype(BF16)] * 3, axis=0)
    b2 = _chunk_scan(fc, reverse_rows, tri(seen_t[0]), tri(seen_t[B - 1]))
    u2 = ic - b2
    u_t = jnp.concatenate([u2, jnp.zeros((L - B, L), F32)], axis=0).T
    u_col = jnp.stack([jnp.broadcast_to(u_t[:, i:i + 1], (L, L)) for i in range(B)])
    b, i_g, f_g = _rows(b2), _rows(ic), _rows(fc)
    b_last = jnp.sum(f_g, axis=2, keepdims=True)

    log_d = jnp.where(seen_t, b + u_col, -jnp.inf)
    a = b + m
    m_t = jnp.maximum(a, jnp.max(log_d, axis=1, keepdims=True))
    dmat = jnp.exp(log_d - m_t)
    inter = jnp.exp(a - m_t)
    sc = _per_chain(_dot, k, q_t) * dmat
    cq = _per_chain(_dot, CT, q_t)
    num = _paired_value_matmul(v_t, sc.astype(BF16)) + inter * cq[:, :d, :]
    den = jnp.sum(sc, axis=1, keepdims=True) + inter * cq[:, d:d + 1, :]
    h_t = num * (1.0 / jnp.maximum(jnp.abs(den), jnp.exp(-m_t)))

    g = b_last - b + i_g
    m_new = jnp.maximum(b_last + m, jnp.max(g, axis=2, keepdims=True))
    w = jnp.exp(g - m_new)
    decay = jnp.exp(b_last + m - m_new)
    vw = jnp.concatenate([v_t.astype(F32), jnp.ones((B, ML_EXT, L), F32)], axis=1) * w
    CT_new = decay * CT + _per_chain(_dot, vw, k)
    return h_t, CT_new, m_new


def _mlstm_kernel(*refs, seq_len, nb, has_state, emit_state):
    q_ref, k_ref, v_ref, so_ref, g_ref, gmh_ref = refs[:6]
    refs = refs[6:]
    if has_state:
        (c0_ref, n0_ref, m0_ref), refs = refs[:3], refs[3:]
    hm_ref, refs = refs[0], refs[1:]
    if emit_state:
        (c_out_ref, n_out_ref, m_out_ref), refs = refs[:3], refs[3:]
    ct_sc, m_sc, gr_sc, h_sc = refs

    nc = seq_len // CHUNK
    d_head = ML_HEAD_DIM
    chains = [(bb, d, h) for bb in range(nb) for d in range(2) for h in range(ML_HEADS)]
    n_chain = len(chains)
    for i, (bb, d, h) in enumerate(chains):
        if has_state:
            ct_sc[i, 0:d_head, :] = c0_ref[bb, d, h].T
            ct_sc[i, d_head:d_head + ML_EXT, :] = jnp.broadcast_to(n0_ref[bb, d, h:h + 1, :], (ML_EXT, d_head))
            m_sc[i] = jnp.full((1, 1), m0_ref[pl.program_id(0) * nb + bb, 0, d, h], F32)
        else:
            ct_sc[i] = jnp.zeros((d_head + ML_EXT, d_head), F32)
            m_sc[i] = jnp.zeros((1, 1), F32)
    for c in range(nb * nc):
        gr_sc[c] = g_ref[c * CHUNK:(c + 1) * CHUNK, :].T

    s_idx = lax.broadcasted_iota(jnp.int32, (CHUNK, CHUNK), 0)
    t_idx = lax.broadcasted_iota(jnp.int32, (CHUNK, CHUNK), 1)
    seen_t = jnp.stack([(s_idx >= t_idx) if d else (s_idx <= t_idx) for _, d, _ in chains])
    reverse_rows = (lax.broadcasted_iota(jnp.int32, (n_chain, CHUNK), 0) // ML_HEADS) % 2 == 1

    def step(c_fwd, c_bwd, rows_of):
        chunk_of = (c_fwd, c_bwd)
        lo = 2 * ML_HEADS
        ic, fc = [], []
        for bb in range(nb):
            g_fwd, g_bwd = gr_sc[bb * nc + c_fwd], gr_sc[bb * nc + c_bwd]
            ic += [g_fwd[0:ML_HEADS], g_bwd[lo:lo + ML_HEADS]]
            fc += [g_fwd[ML_HEADS:lo], g_bwd[lo + ML_HEADS:2 * lo]]
        head = [slice(h * d_head, (h + 1) * d_head) for _, _, h in chains]
        stack = lambda pick: jnp.stack([pick(i, bb * nc + chunk_of[d]) for i, (bb, d, _) in enumerate(chains)])
        h_t, CT_new, m_new = _mlstm_step(
            stack(lambda i, c: k_ref[rows_of(c), head[i]]),
            stack(lambda i, c: q_ref[c, head[i], :]),
            stack(lambda i, c: v_ref[c, head[i], :]),
            jnp.concatenate(ic, axis=0), jnp.concatenate(fc, axis=0), ct_sc[...], m_sc[...], seen_t, reverse_rows)
        ct_sc[...] = CT_new
        m_sc[...] = m_new
        for i, (bb, d, _) in enumerate(chains):
            h_sc[d, bb * nc + chunk_of[d], head[i], :] = h_t[i]

    if nc <= 2:
        for c in range(nc):
            step(c, nc - 1 - c, lambda cc: slice(cc * CHUNK, (cc + 1) * CHUNK))
    else:
        def body(c, carry):
            step(c, nc - 1 - c, lambda cc: pl.ds(pl.multiple_of(cc * CHUNK, CHUNK), CHUNK))
            return carry
        lax.fori_loop(0, nc, body, 0, unroll=2)

    for h in range(ML_HEADS):
        hcols = slice(h * d_head, (h + 1) * d_head)
        gain = jnp.broadcast_to(gmh_ref[h:h + 1, :], (CHUNK, d_head)).T
        for c in range(nb * nc):
            rows = slice(c * CHUNK, (c + 1) * CHUNK)
            hsum = h_sc[0, c, hcols, :] + h_sc[1, c, hcols, :]
            hn = hsum * lax.rsqrt(jnp.mean(hsum * hsum, axis=0, keepdims=True) + EPS) * gain
            hm_ref[rows, hcols] = (hn.T * so_ref[rows, hcols]).astype(hm_ref.dtype)
    if emit_state:
        for i, (bb, d, h) in enumerate(chains):
            c_out_ref[bb, d, h] = ct_sc[i, 0:d_head, :].T
            n_out_ref[bb, d, h:h + 1, :] = ct_sc[i, d_head:d_head + 1, :]
            m_out_ref[bb, d:d + 1, h:h + 1] = m_sc[i]


def _mlstm(q_t, k, v_t, so, gates, g_mh, state, batch, seq_len, emit_state):
    rows = k.shape[0]
    d = ML_HEAD_DIM
    nc = seq_len // CHUNK
    nb = ML_SHORT_SEQ_BATCH if nc <= 2 else 1
    assert batch % nb == 0
    tile = pl.BlockSpec((nb * seq_len, D_MODEL), lambda b: (b, 0))
    tile_t = pl.BlockSpec((nb * nc, D_MODEL, CHUNK), lambda b: (b, 0, 0))
    in_specs = [tile_t, tile, tile_t, tile,
                pl.BlockSpec((nb * seq_len, GATE_PAD), lambda b: (b, 0)),
                _resident((ML_HEADS, d))]
    args = [q_t, k, v_t, so, gates, g_mh]
    c_spec = pl.BlockSpec((nb, None, 2, ML_HEADS, d, d), lambda b: (b, 0, 0, 0, 0, 0))
    n_spec = pl.BlockSpec((nb, None, 2, ML_HEADS, d), lambda b: (b, 0, 0, 0, 0))
    if state is not None:
        in_specs += [c_spec, n_spec, pl.BlockSpec(memory_space=pltpu.SMEM)]
        args += list(state)
    out_specs = [tile]
    out_shape = [jax.ShapeDtypeStruct((rows, D_MODEL), BF16)]
    if emit_state:
        out_specs += [c_spec, n_spec, pl.BlockSpec((nb, None, 2, ML_HEADS), lambda b: (b, 0, 0, 0))]
        out_shape += [jax.ShapeDtypeStruct((batch, 1, 2, ML_HEADS, d, d), F32),
                      jax.ShapeDtypeStruct((batch, 1, 2, ML_HEADS, d), F32),
                      jax.ShapeDtypeStruct((batch, 1, 2, ML_HEADS), F32)]
    n_state = 2 * ML_HEADS * nb
    return pl.pallas_call(
        functools.partial(_mlstm_kernel, seq_len=seq_len, nb=nb, has_state=state is not None,
                          emit_state=emit_state),
        grid=(batch // nb,),
        in_specs=in_specs,
        out_specs=out_specs,
        out_shape=out_shape,
        scratch_shapes=[pltpu.VMEM((n_state, d + ML_EXT, d), F32), pltpu.VMEM((n_state, 1, 1), F32),
                        pltpu.VMEM((nb * nc, GATE_PAD, CHUNK), F32),
                        pltpu.VMEM((2, nb * nc, D_MODEL, CHUNK), F32)],
        compiler_params=_params(1),
        name="mlstm",
    )(*args)


def _merge_kernel(*refs, tiles, mod_index):
    n = len(tiles)
    (x_ref, m_ref), refs = refs[:2], refs[2:]
    parts = [refs[j * n:(j + 1) * n] for j in range(4)]
    wm_ref, wd_ref, wo_ref, o_ref, w_sc = refs[4 * n:]
    i = pl.program_id(0)

    @pl.when(i == 0)
    def _():
        for j, w_ref in enumerate((wm_ref, wd_ref, wo_ref)):
            w_sc[j] = w_ref[...].astype(BF16)

    def tile(hm_ref, att_ref, sgm_ref, sgd_ref):
        y = sgm_ref[...] * _dot(hm_ref[...], w_sc[0]) + sgd_ref[...] * _dot(att_ref[...], w_sc[1])
        o_ref[...] = x_ref[...] + _mod_reader(m_ref, mod_index(i * MERGE_TM))(5) * _dot(y, w_sc[2])

    starts = np.cumsum((0,) + tiles).tolist()
    for k in range(n):
        pl.when((i >= starts[k]) & (i < starts[k + 1]))(functools.partial(tile, *[p[k] for p in parts]))


def _merge(x, mods, mod_index, hm, att, sgm, sgd, wm, wd, wo):
    rows = x.shape[0]
    tiles = tuple(h.shape[0] // MERGE_TM for h in hm)
    assert sum(tiles) * MERGE_TM == rows
    tile = pl.BlockSpec((MERGE_TM, D_MODEL), lambda i: (i, 0))

    def part_spec(k):
        start = sum(tiles[:k])
        return pl.BlockSpec((MERGE_TM, D_MODEL), lambda i: (jnp.clip(i - start, 0, tiles[k] - 1), 0))

    part_specs = [part_spec(k) for k in range(len(tiles))]
    w_spec = _resident((D_MODEL, D_MODEL))
    return pl.pallas_call(
        functools.partial(_merge_kernel, tiles=tiles, mod_index=mod_index),
        grid=(rows // MERGE_TM,),
        in_specs=[tile, _resident(mods.shape)]
                 + part_specs * 4 + [w_spec, w_spec, w_spec],
        out_specs=tile,
        out_shape=jax.ShapeDtypeStruct((rows, D_MODEL), F32),
        scratch_shapes=[pltpu.VMEM((3, D_MODEL, D_MODEL), BF16)],
        compiler_params=_params(1),
        name="branch_merge",
    )(x, mods, *hm, *att, *sgm, *sgd, wm, wd, wo)


def _rope_tables(seq_len):
    lane = np.arange(LANES)
    r = lane % 32
    freqs = np.power(np.float32(ROPE_BASE), -(r % 16).astype(np.float32) / np.float32(16.0))
    tok = np.arange(seq_len)
    pos = np.where((lane % 64 < 32)[None, :], (tok // GRID_W)[:, None], (tok % GRID_W)[:, None]).astype(np.float32)
    ang = pos * freqs[None, :]
    sign = np.where(r < 16, -1.0, 1.0).astype(np.float32)
    return jnp.asarray(np.cos(ang), F32), jnp.asarray(np.sin(ang) * sign[None, :], F32)


def _mixer_branches(x, row0, mods, mod_index, w, batch, seq_len, ctx, lam_init):
    rope_tabs = None if ctx is None else _rope_tables(seq_len)
    mq, mk, mv, so, dq, dk, dv, sgm, sgd, gates = _proj(
        x, row0, batch * seq_len, mods, mod_index, w["g_norm"], w["w_seg"], w["b_gate"], w["bd"], w["g_qn"],
        w["g_kn"], rope_tabs, seq_len)
    cache = None if ctx is None else (ctx[0], ctx[1])
    att = _attn(dq, dk, dv, cache, w["lam"], w["g_sub"], batch, seq_len, lam_init)
    state = None if ctx is None else ctx[2]
    res = _mlstm(mq, mk, mv, so, gates, w["g_mh"], state, batch, seq_len, emit_state=ctx is None)
    return res[0], att, sgm, sgd, dk, dv, res[1:]


def kernel(x_prompt, x_sample, c, cache_k, cache_v, state_C, state_n, state_m, c_ctx, w_ada, b_ada, g_norm, ffn1_w1, ffn1_w3, ffn1_w2, ffn2_w1, ffn2_w3, ffn2_w2, w_in, b_gate, g_qn, g_kn, lam_q1, lam_k1, lam_q2, lam_k2, g_sub, g_mh, w_br_m, w_br_d, w_out):
    depth = w_ada.shape[0]
    assert depth == 1
    l = 0
    bp, tp, _ = x_prompt.shape
    bs, ts, _ = x_sample.shape
    past = cache_k.shape[2]
    lam_init = 0.8 - 0.6 * math.exp(-0.3 * l)

    cvecs = jnp.concatenate([c_ctx[None, :], c, jnp.zeros((MOD_ROWS - 1 - bs, D_MODEL), F32)], axis=0)
    mods = _mods(cvecs, w_ada[l], b_ada[l])

    group = np.arange(MXU_DIM) // DA_HEAD_DIM
    w = dict(
        g_norm=g_norm[l],
        ffn1_w1=ffn1_w1[l], ffn1_w3=ffn1_w3[l], ffn1_w2=ffn1_w2[l],
        ffn2_w1=ffn2_w1[l], ffn2_w3=ffn2_w3[l], ffn2_w2=ffn2_w2[l],
        w_in_t=w_in[l].T,
        b_gate=b_gate[l:l + 1],
        bd=jnp.asarray(group[:, None] == group[None, :], BF16),
        g_qn=g_qn[l:l + 1], g_kn=g_kn[l:l + 1],
        lam=(lam_q1[l:l + 1], lam_k1[l:l + 1], lam_q2[l:l + 1], lam_k2[l:l + 1]),
        g_sub=g_sub[l:l + 1], g_mh=g_mh[l],
        w_br_m=w_br_m[l], w_br_d=w_br_d[l], w_out=w_out[l],
    )

    n_ctx, n_lat = bp * tp, bs * ts
    mod_index = lambda r: jnp.where(r < n_ctx, 0, 1 + (r - n_ctx) // ts)
    seg_start = lambda j: pl.multiple_of(jnp.where(j < N_SEG, _seg_start(j), GATE_LO), N_GATE_COLS)
    x1, w["w_seg"] = _ffn((x_prompt.reshape(n_ctx, D_MODEL), x_sample.reshape(n_lat, D_MODEL)), (n_ctx + n_lat,),
                          mods, mod_index, w["g_norm"], w["ffn1_w1"], w["ffn1_w3"], w["ffn1_w2"], base=0,
                          side=(w["w_in_t"], N_SEG + 1, seg_start))

    hm_p, att_p, sgm_p, sgd_p, new_k_t, new_v, (new_c, new_n, new_m) = _mixer_branches(
        x1, 0, mods, mod_index, w, bp, tp, None, lam_init)

    ctx = (cache_k[:, l].transpose(0, 2, 3, 4, 1).reshape(bs, D_MODEL, past),
           cache_v[:, l].reshape(bs, past * DA_HEADS, DA_V_DIM),
           (state_C, state_n, state_m))
    hm_s, att_s, sgm_s, sgd_s, _, _, _ = _mixer_branches(x1, n_ctx, mods, mod_index, w, bs, ts, ctx, lam_init)

    x2 = _merge(x1, mods, mod_index, (hm_p, hm_s), (att_p, att_s), (sgm_p, sgm_s), (sgd_p, sgd_s),
                w["w_br_m"], w["w_br_d"], w["w_out"])
    xp, xs = _ffn((x2,), (n_ctx, n_lat), mods, mod_index, w["g_norm"], w["ffn2_w1"], w["ffn2_w3"], w["ffn2_w2"],
                  base=6)

    return (xp.reshape(bp, tp, D_MODEL), xs.reshape(bs, ts, D_MODEL),
            new_k_t.reshape(bp, DA_HEADS, 2, DA_HEAD_DIM, tp).transpose(0, 4, 1, 2, 3)[:, None],
            new_v.reshape(bp, 1, tp, DA_HEADS, DA_V_DIM),
            new_c, new_n, new_m)
```

```python
import functools
import math

import jax
import jax.numpy as jnp
import numpy as np
from jax import lax
from jax.experimental import pallas as pl
from jax.experimental.pallas import tpu as pltpu

F32 = jnp.float32
BF16 = jnp.bfloat16

D_MODEL = 1024
D_FF = 2816
N_MOD = 9
GRID_W = 64
ML_HEADS = 4
ML_HEAD_DIM = 256
DA_HEADS = 8
DA_HEAD_DIM = 64
DA_V_DIM = 128
N_GATE_COLS = 16
CHUNK = 128
ROPE_BASE = 10000.0
QK_LOG2_SCALE = DA_HEAD_DIM ** -0.5 * math.log2(math.e)
EPS = 1e-6

LANES = 128
GATE_PAD = LANES
MXU_DIM = 256
VMEM_LIMIT = 56 * 1024 * 1024

MOD_ROWS = 8
FFN_TM = 512
FFN_TF = 256
PROJ_TM = 256
MERGE_TM = 512
ATT_TQ = 256
ATT_TK_MAX = 768
ATT_SHORT_SEQ_BATCH = 4


def _params(n_axes):
    return pltpu.CompilerParams(dimension_semantics=("arbitrary",) * n_axes,
                                vmem_limit_bytes=VMEM_LIMIT)


def _dot(a, b):
    return jnp.dot(a.astype(BF16), b.astype(BF16), preferred_element_type=F32)


def _dot_nt(a, b):
    return lax.dot_general(a.astype(BF16), b.astype(BF16), (((1,), (1,)), ((), ())),
                           preferred_element_type=F32)


def _sigmoid(x):
    return 1.0 / (1.0 + jnp.exp(-x))


def _log_sigmoid(x):
    return jnp.minimum(x, 0.0) - jnp.log1p(jnp.exp(-jnp.abs(x)))


def _modulated_norm(x, g, shift, scale):
    y = x * lax.rsqrt(jnp.mean(x * x, axis=-1, keepdims=True) + EPS) * g
    return y * (1.0 + scale) + shift


def _resident(shape):
    return pl.BlockSpec(shape, lambda *_: (0,) * len(shape), pipeline_mode=pl.Buffered(1))


def _mods_kernel(c_ref, w_ref, b_ref, o_ref):
    c = c_ref[...]
    o_ref[...] = _dot(c * _sigmoid(c), w_ref[...]) + b_ref[...]


def _mods(cvecs, w_ada, b_ada):
    n = N_MOD * D_MODEL
    tn = D_MODEL
    return pl.pallas_call(
        _mods_kernel,
        grid=(n // tn,),
        in_specs=[pl.BlockSpec((MOD_ROWS, D_MODEL), lambda j: (0, 0)),
                  pl.BlockSpec((D_MODEL, tn), lambda j: (0, j)),
                  pl.BlockSpec((1, tn), lambda j: (0, j))],
        out_specs=pl.BlockSpec((MOD_ROWS, tn), lambda j: (0, j)),
        out_shape=jax.ShapeDtypeStruct((MOD_ROWS, n), F32),
        compiler_params=_params(1),
        name="adaln_mods",
    )(cvecs, w_ada, b_ada.reshape(1, n))


def _mod_reader(m_ref, row):
    return lambda j: m_ref[pl.ds(row, 1), j * D_MODEL:(j + 1) * D_MODEL]


def _ffn_kernel(*refs, base, mod_index, in_tiles, out_tiles, side_blocks):
    n_in, n_out = len(in_tiles), len(out_tiles)
    x_refs, refs = refs[:n_in], refs[n_in:]
    (m_ref, g_ref, w1_ref, w3_ref, w2_ref), refs = refs[:5], refs[5:]
    if side_blocks:
        side_in_ref, refs = refs[0], refs[1:]
        side_out_ref, refs = refs[n_out], refs[:n_out] + refs[n_out + 1:]
    o_refs, (w1_sc, w3_sc, w2_sc, hh_sc, acc_sc) = refs[:n_out], refs[n_out:]
    g = pl.program_id(0)
    nf = D_FF // FFN_TF
    row_tile = g - (nf - 1)
    mod = _mod_reader(m_ref, mod_index(jnp.maximum(row_tile, 0) * FFN_TM))
    gain = g_ref[base // 3:base // 3 + 1, :]
    norm = lambda x: _modulated_norm(x, gain, mod(base), mod(base + 1)).astype(BF16)
    finish = lambda x, acc: x + 0.5 * mod(base + 2) * acc

    def tile(hh, f):
        a = _dot(hh, w1_sc[f])
        b = _dot(hh, w3_sc[f])
        return _dot(a * _sigmoid(a) * b, w2_sc[f])

    @pl.when(g < nf)
    def _():
        w1_sc[g] = w1_ref[...].astype(BF16)
        w3_sc[g] = w3_ref[...].astype(BF16)
        w2_sc[g] = w2_ref[...].astype(BF16)

        @pl.when(g == 0)
        def _():
            hh_sc[...] = norm(x_refs[0][...])
            acc_sc[...] = jnp.zeros(acc_sc.shape, F32)

        acc_sc[...] += tile(hh_sc[...], g)

        @pl.when(g == nf - 1)
        def _():
            o_refs[0][...] = finish(x_refs[0][...], acc_sc[...])

    if side_blocks:
        @pl.when((g >= nf) & (g - nf < side_blocks))
        def _():
            side_out_ref[...] = side_in_ref[...].astype(BF16)

    def full_tile(x_ref, o_ref):
        x = x_ref[...]
        hh = norm(x)
        acc = jnp.zeros(x.shape, F32)
        for f in range(nf):
            acc = acc + tile(hh, f)
        o_ref[...] = finish(x, acc)

    bounds = sorted(set(np.cumsum((0,) + in_tiles).tolist()) | set(np.cumsum((0,) + out_tiles).tolist()))
    for lo, hi in zip(bounds[:-1], bounds[1:]):
        k_in = int(np.searchsorted(np.cumsum(in_tiles), lo, side="right"))
        k_out = int(np.searchsorted(np.cumsum(out_tiles), lo, side="right"))
        pl.when((g >= nf) & (row_tile >= lo) & (row_tile < hi))(
            functools.partial(full_tile, x_refs[k_in], o_refs[k_out]))


def _ffn(xs, out_rows, mods, mod_index, g, w1, w3, w2, base, side=None):
    nf = D_FF // FFN_TF
    in_tiles = tuple(x.shape[0] // FFN_TM for x in xs)
    out_tiles = tuple(r // FFN_TM for r in out_rows)
    assert sum(in_tiles) == sum(out_tiles)
    side_in_specs, side_out_specs, side_out_shape, side_args, side_blocks = [], [], [], [], 0
    if side is not None:
        side_w, side_blocks, side_start = side
        assert side_blocks <= sum(in_tiles) - 1
        blk = lambda s: jnp.clip(s - nf, 0, side_blocks - 1)
        side_in_specs = [pl.BlockSpec((pl.Element(D_MODEL), pl.Element(D_MODEL)), lambda s: (side_start(blk(s)), 0))]
        side_out_specs = [pl.BlockSpec((None, D_MODEL, D_MODEL), lambda s: (blk(s), 0, 0))]
        side_out_shape = [jax.ShapeDtypeStruct((side_blocks, D_MODEL, D_MODEL), BF16)]
        side_args = [side_w]
    row_tile = lambda s: jnp.maximum(s - (nf - 1), 0)
    f_tile = lambda s: jnp.minimum(s, nf - 1)

    def part_spec(tiles, k):
        start = sum(tiles[:k])
        return pl.BlockSpec((FFN_TM, D_MODEL), lambda s: (jnp.clip(row_tile(s) - start, 0, tiles[k] - 1), 0))

    return pl.pallas_call(
        functools.partial(_ffn_kernel, base=base, mod_index=mod_index, in_tiles=in_tiles, out_tiles=out_tiles,
                          side_blocks=side_blocks),
        grid=(nf - 1 + sum(in_tiles),),
        in_specs=[part_spec(in_tiles, k) for k in range(len(xs))] + [
            _resident(mods.shape),
            _resident(g.shape),
            pl.BlockSpec((D_MODEL, FFN_TF), lambda s: (0, f_tile(s))),
            pl.BlockSpec((D_MODEL, FFN_TF), lambda s: (0, f_tile(s))),
            pl.BlockSpec((FFN_TF, D_MODEL), lambda s: (f_tile(s), 0))] + side_in_specs,
        out_specs=[part_spec(out_tiles, k) for k in range(len(out_rows))] + side_out_specs,
        out_shape=[jax.ShapeDtypeStruct((r, D_MODEL), F32) for r in out_rows] + side_out_shape,
        scratch_shapes=[pltpu.VMEM((nf, D_MODEL, FFN_TF), BF16), pltpu.VMEM((nf, D_MODEL, FFN_TF), BF16),
                        pltpu.VMEM((nf, FFN_TF, D_MODEL), BF16),
                        pltpu.VMEM((FFN_TM, D_MODEL), BF16), pltpu.VMEM((FFN_TM, D_MODEL), F32)],
        compiler_params=_params(1),
        name="ffn",
    )(*xs, mods, g, w1, w3, w2, *side_args)


def _group_norm64(x, bd, g):
    ss = _dot(x * x, bd)
    return x * lax.rsqrt(ss * (1.0 / DA_HEAD_DIM) + EPS) * g


def _rope(x, cos, sin_signed):
    first = (lax.broadcasted_iota(jnp.int32, x.shape, 1) % 32) < 16
    partner = jnp.where(first, pltpu.roll(x, LANES - 16, 1), pltpu.roll(x, 16, 1))
    return x * cos + partner * sin_signed


N_SEG = 9
PROJ_WARM = (N_SEG + 1) // 2
GATE_LO = 4 * D_MODEL


def _seg_start(s):
    return s * D_MODEL + N_GATE_COLS * (s >= 4)


def _proj_kernel(*refs, rope, row_of_step):
    (x_ref, m_ref, g_ref, wt_ref, bg_ref, bd_ref, gq_ref, gk_ref), refs = refs[:8], refs[8:]
    if rope:
        (cos_ref, sin_ref), refs = refs[:2], refs[2:]
    (mq_ref, mk_ref, mv_ref, so_ref, dq_ref, dk_ref, dv_ref, sgm_ref, sgd_ref, gates_ref,
     w_sc, wg_sc, hh_sc) = refs
    step = pl.program_id(0)
    mod = _mod_reader(m_ref, row_of_step(step))
    norm = lambda: _modulated_norm(x_ref[...], g_ref[1:2, :], mod(3), mod(4)).astype(BF16)
    groups = MXU_DIM // DA_HEAD_DIM
    tiled = lambda ref, scale: jnp.concatenate([ref[...] * scale] * groups, axis=1)
    gq_row, gk_row = tiled(gq_ref, QK_LOG2_SCALE), tiled(gk_ref, 1.0)
    bias = jnp.concatenate([bg_ref[...], jnp.zeros((1, GATE_PAD - N_GATE_COLS), F32)], axis=1)

    def gates(hh):
        y = _dot_nt(hh, wg_sc[...]) + bias
        col = lax.broadcasted_iota(jnp.int32, y.shape, 1)
        gates_ref[...] = jnp.where((col // ML_HEADS) % 2 == 1, _log_sigmoid(y), y)

    def qk_segment(y, gain_row, o_ref):
        bd = bd_ref[...]
        for c in range(D_MODEL // MXU_DIM):
            cols = slice(c * MXU_DIM, (c + 1) * MXU_DIM)
            z = _group_norm64(y[:, cols], bd, gain_row)
            if rope:
                z = jnp.concatenate(
                    [_rope(z[:, k * LANES:(k + 1) * LANES], cos_ref[...], sin_ref[...])
                     for k in range(MXU_DIM // LANES)], axis=1)
            o_ref[:, cols] = z.astype(o_ref.dtype)

    def cache_key_segment(hh):
        y_t = _dot_nt(w_sc[5], hh)
        z = y_t.reshape(D_MODEL // DA_HEAD_DIM, DA_HEAD_DIM, y_t.shape[1])
        ms = jnp.mean(z * z, axis=1, keepdims=True)
        gain = jnp.broadcast_to(gk_row[:, 0:LANES], (LANES, LANES)).T[0:DA_HEAD_DIM, :]
        gain = jnp.concatenate([gain] * (y_t.shape[1] // LANES), axis=1)
        dk_ref[...] = (z * lax.rsqrt(ms + EPS) * gain).reshape(y_t.shape)

    def segment(s, hh):
        if s in (0, 2):
            o_ref = {0: mq_ref, 2: mv_ref}[s]
            y_t = _dot_nt(w_sc[s], hh) * ((ML_HEAD_DIM ** -0.5) if s == 0 else 1.0)
            for j in range(PROJ_TM // CHUNK):
                o_ref[j] = y_t[:, j * CHUNK:(j + 1) * CHUNK].astype(o_ref.dtype)
            return
        if s == 5 and not rope:
            cache_key_segment(hh)
            return
        y = _dot_nt(hh, w_sc[s])
        if s == 4:
            qk_segment(y, gq_row, dq_ref)
        elif s == 5:
            qk_segment(y, gk_row, dk_ref)
        elif s in (3, 7, 8):
            {3: so_ref, 7: sgm_ref, 8: sgd_ref}[s][...] = _sigmoid(y)
        elif s == 6 and not rope:
            dv_ref[...] = y.reshape(y.shape[0], DA_HEADS, DA_V_DIM)
        else:
            o_ref = {1: mk_ref, 6: dv_ref}[s]
            o_ref[...] = y.astype(o_ref.dtype)

    for w in range(PROJ_WARM):
        @pl.when(step == w)
        def _(w=w):
            if w == 0:
                hh_sc[...] = norm()
            for j in range(2):
                s = 2 * w + j
                if s < N_SEG:
                    w_sc[s] = wt_ref[j]
                    segment(s, hh_sc[...])
                else:
                    wg_sc[...] = jnp.zeros(wg_sc.shape, BF16)
                    wg_sc[0:N_GATE_COLS, :] = wt_ref[j, 0:N_GATE_COLS, :]
                    gates(hh_sc[...])

    @pl.when(step >= PROJ_WARM)
    def _():
        hh = norm()
        gates(hh)
        for s in range(N_SEG):
            segment(s, hh)


def _proj(x, row0, rows, mods, mod_index, g, w_seg, b_gate, bd, gq, gk, rope_tabs, seq_len):
    rope = rope_tabs is not None
    row_tile = lambda s: jnp.maximum(s - (PROJ_WARM - 1), 0)
    row = lambda s: (row_tile(s), 0)
    tile = pl.BlockSpec((PROJ_TM, D_MODEL), row)
    w_block = (2, D_MODEL, D_MODEL)
    in_specs = [pl.BlockSpec((PROJ_TM, D_MODEL), lambda s: (row0 // PROJ_TM + row_tile(s), 0)),
                _resident(mods.shape),
                _resident(g.shape),
                pl.BlockSpec(w_block, lambda s: (jnp.minimum(s, PROJ_WARM - 1), 0, 0)),
                _resident(b_gate.shape),
                _resident((MXU_DIM, MXU_DIM)),
                _resident(gq.shape),
                _resident(gk.shape)]
    args = [x, mods, g, w_seg, b_gate, bd, gq, gk]
    if rope:
        tiles_per_seq = seq_len // PROJ_TM
        tab = pl.BlockSpec((PROJ_TM, LANES), lambda s: (row_tile(s) % tiles_per_seq, 0))
        in_specs += [tab, tab]
        args += list(rope_tabs)
    dtypes = [BF16, BF16, BF16, F32, BF16, BF16, BF16 if rope else F32, F32, F32]
    out_shape = [jax.ShapeDtypeStruct((rows, D_MODEL), dt) for dt in dtypes]
    out_shape.append(jax.ShapeDtypeStruct((rows, GATE_PAD), F32))
    out_specs = [tile] * 9 + [pl.BlockSpec((PROJ_TM, GATE_PAD), row)]
    slabs = PROJ_TM // CHUNK
    for j in (0, 2):
        out_shape[j] = jax.ShapeDtypeStruct((rows // CHUNK, D_MODEL, CHUNK), BF16)
        out_specs[j] = pl.BlockSpec((slabs, D_MODEL, CHUNK), lambda s: (row_tile(s), 0, 0))
    if not rope:
        assert seq_len == PROJ_TM
        out_shape[5] = jax.ShapeDtypeStruct((rows // seq_len, D_MODEL, seq_len), F32)
        out_specs[5] = pl.BlockSpec((None, D_MODEL, seq_len), lambda s: (row_tile(s), 0, 0))
        out_shape[6] = jax.ShapeDtypeStruct((rows, DA_HEADS, DA_V_DIM), F32)
        out_specs[6] = pl.BlockSpec((PROJ_TM, DA_HEADS, DA_V_DIM), lambda s: (row_tile(s), 0, 0))
    return pl.pallas_call(
        functools.partial(_proj_kernel, rope=rope,
                          row_of_step=lambda step: mod_index(row0 + row_tile(step) * PROJ_TM)),
        grid=(PROJ_WARM - 1 + rows // PROJ_TM,),
        in_specs=in_specs,
        out_specs=out_specs,
        out_shape=out_shape,
        scratch_shapes=[pltpu.VMEM((N_SEG, D_MODEL, D_MODEL), BF16), pltpu.VMEM((GATE_PAD, D_MODEL), BF16),
                        pltpu.VMEM((PROJ_TM, D_MODEL), BF16)],
        compiler_params=_params(1),
        name="mixer_in_proj",
    )(*args)


def _lambda(lam_refs, lam_init):
    q1, k1, q2, k2 = (r[...] for r in lam_refs)
    s1 = jnp.sum(q1 * k1, axis=1, keepdims=True)
    s2 = jnp.sum(q2 * k2, axis=1, keepdims=True)
    return jnp.exp(s1) - jnp.exp(s2) + lam_init


def _attn_kernel(*refs, cached, seq, nb, lam_init):
    if cached:
        (q_ref, k_ref, v_ref, ck_ref, cv_ref), refs = refs[:5], refs[5:]
    else:
        (q_ref, k_ref, v_ref), refs = refs[:3], refs[3:]
    lam_refs, (gs_ref, o_ref, kall_sc, vt_sc, s_sc) = refs[:4], refs[4:]
    units = [(bb, h) for bb in range(nb) for h in range(DA_HEADS)]

    def cache_order_kv(kt_ref, vr_ref, u, h, lo, n):
        kall_sc[u, lo:lo + n, :] = kt_ref[h * DA_V_DIM:(h + 1) * DA_V_DIM, :].T.astype(BF16)
        vt_sc[u, :, lo:lo + n] = vr_ref[pl.ds(h, n, stride=DA_HEADS), :].T.astype(BF16)

    @pl.when(pl.program_id(1) == 0)
    def _():
        for u, (bb, h) in enumerate(units):
            if cached:
                cols = slice(h * DA_V_DIM, (h + 1) * DA_V_DIM)
                kall_sc[u, 0:seq, :] = k_ref[:, cols].astype(BF16)
                vt_sc[u, :, 0:seq] = v_ref[:, cols].astype(F32).T.astype(BF16)
                cache_order_kv(ck_ref.at[bb], cv_ref.at[bb], u, h, seq, ck_ref.shape[2])
            else:
                cache_order_kv(k_ref.at[bb], v_ref.at[bb], u, h, 0, seq)

    lam = _lambda(lam_refs, lam_init)
    sub_gain = gs_ref[...] * (1.0 - lam_init)
    lane = lax.broadcasted_iota(jnp.int32, (1, DA_V_DIM), 1)
    comp_masks = [lane < DA_HEAD_DIM, lane >= DA_HEAD_DIM]
    tq = q_ref.shape[0] // nb
    n_keys = kall_sc.shape[1]
    n_tiles = pl.cdiv(n_keys, ATT_TK_MAX)
    tk = n_keys // n_tiles

    def stacked_q(u):
        bb, h = units[u]
        q = q_ref[bb * tq:(bb + 1) * tq, h * DA_V_DIM:(h + 1) * DA_V_DIM].astype(BF16)
        return jnp.concatenate([jnp.where(m, q, jnp.zeros_like(q)) for m in comp_masks], axis=0)

    def score_tile(h, j, qq, m8):
        rows = slice(j * tk, (j + 1) * tk)
        st = _dot_nt(kall_sc[h, rows, :], qq)
        s_sc[h % 2, rows, :] = st
        t8 = jnp.max(st.reshape(tk // 8, 8, 2 * tq), axis=0)
        return t8 if m8 is None else jnp.maximum(m8, t8)

    def prob_tile(h, j, mx, d8, pv):
        rows = slice(j * tk, (j + 1) * tk)
        e = jnp.exp2(s_sc[h % 2, rows, :] - mx)
        s8 = jnp.sum(e.reshape(tk // 8, 8, 2 * tq), axis=0)
        p = _dot(vt_sc[h, :, rows], e)
        return (s8 if d8 is None else d8 + s8), (p if pv is None else pv + p)

    qq = stacked_q(0)
    m8 = None
    for j in range(n_tiles):
        m8 = score_tile(0, j, qq, m8)
    for h in range(len(units)):
        mx = jnp.max(m8, axis=0, keepdims=True)
        if h + 1 < len(units):
            qq = stacked_q(h + 1)
        m8, d8, pv = None, None, None
        for j in range(n_tiles):
            if h + 1 < len(units):
                m8 = score_tile(h + 1, j, qq, m8)
            d8, pv = prob_tile(h, j, mx, d8, pv)
        inv = 1.0 / jnp.sum(d8, axis=0, keepdims=True)
        out_t = pv[:, :tq] * inv[:, :tq] - pv[:, tq:] * (lam * inv[:, tq:])
        out_t = out_t * lax.rsqrt(jnp.mean(out_t * out_t, axis=0, keepdims=True) + EPS)
        bb, head = units[h]
        o_ref[bb * tq:(bb + 1) * tq, head * DA_V_DIM:(head + 1) * DA_V_DIM] = (out_t.T * sub_gain).astype(o_ref.dtype)


def _attn(q, k, v, cache, lam_vecs, g_sub, batch, seq_len, lam_init):
    rows = q.shape[0]
    nq = seq_len // ATT_TQ
    nb = ATT_SHORT_SEQ_BATCH if (cache is None and nq == 1) else 1
    assert batch % nb == 0
    q_spec = pl.BlockSpec((nb * ATT_TQ, D_MODEL), lambda b, i: (b * nq + i, 0))
    cache_specs = lambda n: [pl.BlockSpec((nb, D_MODEL, n), lambda b, i: (b, 0, 0)),
                             pl.BlockSpec((nb, n * DA_HEADS, DA_V_DIM), lambda b, i: (b, 0, 0))]
    n_keys = seq_len
    if cache is not None:
        past = cache[0].shape[2]
        n_keys += past
        kv_spec = pl.BlockSpec((seq_len, D_MODEL), lambda b, i: (b, 0))
        in_specs = [q_spec, kv_spec, kv_spec] + cache_specs(past)
        args = [q, k, v] + list(cache)
    else:
        in_specs = [q_spec] + cache_specs(seq_len)
        args = [q, k, v.reshape(batch, seq_len * DA_HEADS, DA_V_DIM)]
    in_specs += [_resident((1, DA_HEAD_DIM))] * 4 + [_resident((1, DA_V_DIM))]
    args += list(lam_vecs) + [g_sub]
    return pl.pallas_call(
        functools.partial(_attn_kernel, cached=cache is not None, seq=seq_len, nb=nb, lam_init=lam_init),
        grid=(batch // nb, nq),
        in_specs=in_specs,
        out_specs=q_spec,
        out_shape=jax.ShapeDtypeStruct((rows, D_MODEL), BF16),
        scratch_shapes=[pltpu.VMEM((nb * DA_HEADS, n_keys, DA_V_DIM), BF16),
                        pltpu.VMEM((nb * DA_HEADS, DA_V_DIM, n_keys), BF16),
                        pltpu.VMEM((2, n_keys, 2 * ATT_TQ), F32)],
        compiler_params=_params(2),
        name="diff_attention",
    )(*args)


def _per_chain(fn, a, b):
    return jnp.stack([fn(a[i], b[i]) for i in range(a.shape[0])])


ML_EXT = 8
ML_SHORT_SEQ_BATCH = 2


def _rows(x):
    return jnp.stack([x[i:i + 1, :] for i in range(x.shape[0])])


def _split3(x):
    hi = x.astype(BF16)
    r = x - hi.astype(F32)
    mid = r.astype(BF16)
    lo = (r - mid.astype(F32)).astype(BF16)
    return jnp.concatenate([hi, mid, lo], axis=1)


def _chunk_scan(x, reverse_rows, tri_prefix, tri_suffix):
    parts = _split3(x)
    return jnp.where(reverse_rows, _dot(parts, tri_suffix), _dot(parts, tri_prefix))


def _paired_value_matmul(v_t, sc):
    B, L, _ = sc.shape
    zero = jnp.zeros((L, L), sc.dtype)
    out = []
    for i in range(0, B, 2):
        lhs = jnp.concatenate([v_t[i], v_t[i + 1]], axis=1)
        rhs = jnp.concatenate([jnp.concatenate([sc[i], zero], axis=1),
                               jnp.concatenate([zero, sc[i + 1]], axis=1)], axis=0)
        both = _dot(lhs, rhs)
        out += [both[:, :L], both[:, L:]]
    return jnp.stack(out)


def _mlstm_step(k, q_t, v_t, ic, fc, CT, m, seen_t, reverse_rows):
    B, L = ic.shape
    d = k.shape[2]
    tri = lambda keep: jnp.concatenate([jnp.where(keep, 1.0, 0.0).astype(BF16)] * 3, axis=0)
    b2 = _chunk_scan(fc, reverse_rows, tri(seen_t[0]), tri(seen_t[B - 1]))
    u2 = ic - b2
    u_t = jnp.concatenate([u2, jnp.zeros((L - B, L), F32)], axis=0).T
    u_col = jnp.stack([jnp.broadcast_to(u_t[:, i:i + 1], (L, L)) for i in range(B)])
    b, i_g, f_g = _rows(b2), _rows(ic), _rows(fc)
    b_last = jnp.sum(f_g, axis=2, keepdims=True)

    log_d = jnp.where(seen_t, b + u_col, -jnp.inf)
    a = b + m
    m_t = jnp.maximum(a, jnp.max(log_d, axis=1, keepdims=True))
    dmat = jnp.exp(log_d - m_t)
    inter = jnp.exp(a - m_t)
    sc = _per_chain(_dot, k, q_t) * dmat
    cq = _per_chain(_dot, CT, q_t)
    num = _paired_value_matmul(v_t, sc.astype(BF16)) + inter * cq[:, :d, :]
    den = jnp.sum(sc, axis=1, keepdims=True) + inter * cq[:, d:d + 1, :]
    h_t = num * (1.0 / jnp.maximum(jnp.abs(den), jnp.exp(-m_t)))

    g = b_last - b + i_g
    m_new = jnp.maximum(b_last + m, jnp.max(g, axis=2, keepdims=True))
    w = jnp.exp(g - m_new)
    decay = jnp.exp(b_last + m - m_new)
    vw = jnp.concatenate([v_t.astype(F32), jnp.ones((B, ML_EXT, L), F32)], axis=1) * w
    CT_new = decay * CT + _per_chain(_dot, vw, k)
    return h_t, CT_new, m_new


def _mlstm_kernel(*refs, seq_len, nb, has_state, emit_state):
    q_ref, k_ref, v_ref, so_ref, g_ref, gmh_ref = refs[:6]
    refs = refs[6:]
    if has_state:
        (c0_ref, n0_ref, m0_ref), refs = refs[:3], refs[3:]
    hm_ref, refs = refs[0], refs[1:]
    if emit_state:
        (c_out_ref, n_out_ref, m_out_ref), refs = refs[:3], refs[3:]
    ct_sc, m_sc, gr_sc, h_sc = refs

    nc = seq_len // CHUNK
    d_head = ML_HEAD_DIM
    chains = [(bb, d, h) for bb in range(nb) for d in range(2) for h in range(ML_HEADS)]
    n_chain = len(chains)
    for i, (bb, d, h) in enumerate(chains):
        if has_state:
            ct_sc[i, 0:d_head, :] = c0_ref[bb, d, h].T
            ct_sc[i, d_head:d_head + ML_EXT, :] = jnp.broadcast_to(n0_ref[bb, d, h:h + 1, :], (ML_EXT, d_head))
            m_sc[i] = jnp.full((1, 1), m0_ref[pl.program_id(0) * nb + bb, 0, d, h], F32)
        else:
            ct_sc[i] = jnp.zeros((d_head + ML_EXT, d_head), F32)
            m_sc[i] = jnp.zeros((1, 1), F32)
    for c in range(nb * nc):
        gr_sc[c] = g_ref[c * CHUNK:(c + 1) * CHUNK, :].T

    s_idx = lax.broadcasted_iota(jnp.int32, (CHUNK, CHUNK), 0)
    t_idx = lax.broadcasted_iota(jnp.int32, (CHUNK, CHUNK), 1)
    seen_t = jnp.stack([(s_idx >= t_idx) if d else (s_idx <= t_idx) for _, d, _ in chains])
    reverse_rows = (lax.broadcasted_iota(jnp.int32, (n_chain, CHUNK), 0) // ML_HEADS) % 2 == 1

    def step(c_fwd, c_bwd, rows_of):
        chunk_of = (c_fwd, c_bwd)
        lo = 2 * ML_HEADS
        ic, fc = [], []
        for bb in range(nb):
            g_fwd, g_bwd = gr_sc[bb * nc + c_fwd], gr_sc[bb * nc + c_bwd]
            ic += [g_fwd[0:ML_HEADS], g_bwd[lo:lo + ML_HEADS]]
            fc += [g_fwd[ML_HEADS:lo], g_bwd[lo + ML_HEADS:2 * lo]]
        head = [slice(h * d_head, (h + 1) * d_head) for _, _, h in chains]
        stack = lambda pick: jnp.stack([pick(i, bb * nc + chunk_of[d]) for i, (bb, d, _) in enumerate(chains)])
        h_t, CT_new, m_new = _mlstm_step(
            stack(lambda i, c: k_ref[rows_of(c), head[i]]),
            stack(lambda i, c: q_ref[c, head[i], :]),
            stack(lambda i, c: v_ref[c, head[i], :]),
            jnp.concatenate(ic, axis=0), jnp.concatenate(fc, axis=0), ct_sc[...], m_sc[...], seen_t, reverse_rows)
        ct_sc[...] = CT_new
        m_sc[...] = m_new
        for i, (bb, d, _) in enumerate(chains):
            h_sc[d, bb * nc + chunk_of[d], head[i], :] = h_t[i]

    if nc <= 2:
        for c in range(nc):
            step(c, nc - 1 - c, lambda cc: slice(cc * CHUNK, (cc + 1) * CHUNK))
    else:
        def body(c, carry):
            step(c, nc - 1 - c, lambda cc: pl.ds(pl.multiple_of(cc * CHUNK, CHUNK), CHUNK))
            return carry
        lax.fori_loop(0, nc, body, 0, unroll=2)

    for h in range(ML_HEADS):
        hcols = slice(h * d_head, (h + 1) * d_head)
        gain = jnp.broadcast_to(gmh_ref[h:h + 1, :], (CHUNK, d_head)).T
        for c in range(nb * nc):
            rows = slice(c * CHUNK, (c + 1) * CHUNK)
            hsum = h_sc[0, c, hcols, :] + h_sc[1, c, hcols, :]
            hn = hsum * lax.rsqrt(jnp.mean(hsum * hsum, axis=0, keepdims=True) + EPS) * gain
            hm_ref[rows, hcols] = (hn.T * so_ref[rows, hcols]).astype(hm_ref.dtype)
    if emit_state:
        for i, (bb, d, h) in enumerate(chains):
            c_out_ref[bb, d, h] = ct_sc[i, 0:d_head, :].T
            n_out_ref[bb, d, h:h + 1, :] = ct_sc[i, d_head:d_head + 1, :]
            m_out_ref[bb, d:d + 1, h:h + 1] = m_sc[i]


def _mlstm(q_t, k, v_t, so, gates, g_mh, state, batch, seq_len, emit_state):
    rows = k.shape[0]
    d = ML_HEAD_DIM
    nc = seq_len // CHUNK
    nb = ML_SHORT_SEQ_BATCH if nc <= 2 else 1
    assert batch % nb == 0
    tile = pl.BlockSpec((nb * seq_len, D_MODEL), lambda b: (b, 0))
    tile_t = pl.BlockSpec((nb * nc, D_MODEL, CHUNK), lambda b: (b, 0, 0))
    in_specs = [tile_t, tile, tile_t, tile,
                pl.BlockSpec((nb * seq_len, GATE_PAD), lambda b: (b, 0)),
                _resident((ML_HEADS, d))]
    args = [q_t, k, v_t, so, gates, g_mh]
    c_spec = pl.BlockSpec((nb, None, 2, ML_HEADS, d, d), lambda b: (b, 0, 0, 0, 0, 0))
    n_spec = pl.BlockSpec((nb, None, 2, ML_HEADS, d), lambda b: (b, 0, 0, 0, 0))
    if state is not None:
        in_specs += [c_spec, n_spec, pl.BlockSpec(memory_space=pltpu.SMEM)]
        args += list(state)
    out_specs = [tile]
    out_shape = [jax.ShapeDtypeStruct((rows, D_MODEL), BF16)]
    if emit_state:
        out_specs += [c_spec, n_spec, pl.BlockSpec((nb, None, 2, ML_HEADS), lambda b: (b, 0, 0, 0))]
        out_shape += [jax.ShapeDtypeStruct((batch, 1, 2, ML_HEADS, d, d), F32),
                      jax.ShapeDtypeStruct((batch, 1, 2, ML_HEADS, d), F32),
                      jax.ShapeDtypeStruct((batch, 1, 2, ML_HEADS), F32)]
    n_state = 2 * ML_HEADS * nb
    return pl.pallas_call(
        functools.partial(_mlstm_kernel, seq_len=seq_len, nb=nb, has_state=state is not None,
                          emit_state=emit_state),
        grid=(batch // nb,),
        in_specs=in_specs,
        out_specs=out_specs,
        out_shape=out_shape,
        scratch_shapes=[pltpu.VMEM((n_state, d + ML_EXT, d), F32), pltpu.VMEM((n_state, 1, 1), F32),
                        pltpu.VMEM((nb * nc, GATE_PAD, CHUNK), F32),
                        pltpu.VMEM((2, nb * nc, D_MODEL, CHUNK), F32)],
        compiler_params=_params(1),
        name="mlstm",
    )(*args)


def _merge_kernel(*refs, tiles, mod_index):
    n = len(tiles)
    (x_ref, m_ref), refs = refs[:2], refs[2:]
    parts = [refs[j * n:(j + 1) * n] for j in range(4)]
    wm_ref, wd_ref, wo_ref, o_ref, w_sc = refs[4 * n:]
    i = pl.program_id(0)

    @pl.when(i == 0)
    def _():
        for j, w_ref in enumerate((wm_ref, wd_ref, wo_ref)):
            w_sc[j] = w_ref[...].astype(BF16)

    def tile(hm_ref, att_ref, sgm_ref, sgd_ref):
        y = sgm_ref[...] * _dot(hm_ref[...], w_sc[0]) + sgd_ref[...] * _dot(att_ref[...], w_sc[1])
        o_ref[...] = x_ref[...] + _mod_reader(m_ref, mod_index(i * MERGE_TM))(5) * _dot(y, w_sc[2])

    starts = np.cumsum((0,) + tiles).tolist()
    for k in range(n):
        pl.when((i >= starts[k]) & (i < starts[k + 1]))(functools.partial(tile, *[p[k] for p in parts]))


def _merge(x, mods, mod_index, hm, att, sgm, sgd, wm, wd, wo):
    rows = x.shape[0]
    tiles = tuple(h.shape[0] // MERGE_TM for h in hm)
    assert sum(tiles) * MERGE_TM == rows
    tile = pl.BlockSpec((MERGE_TM, D_MODEL), lambda i: (i, 0))

    def part_spec(k):
        start = sum(tiles[:k])
        return pl.BlockSpec((MERGE_TM, D_MODEL), lambda i: (jnp.clip(i - start, 0, tiles[k] - 1), 0))

    part_specs = [part_spec(k) for k in range(len(tiles))]
    w_spec = _resident((D_MODEL, D_MODEL))
    return pl.pallas_call(
        functools.partial(_merge_kernel, tiles=tiles, mod_index=mod_index),
        grid=(rows // MERGE_TM,),
        in_specs=[tile, _resident(mods.shape)]
                 + part_specs * 4 + [w_spec, w_spec, w_spec],
        out_specs=tile,
        out_shape=jax.ShapeDtypeStruct((rows, D_MODEL), F32),
        scratch_shapes=[pltpu.VMEM((3, D_MODEL, D_MODEL), BF16)],
        compiler_params=_params(1),
        name="branch_merge",
    )(x, mods, *hm, *att, *sgm, *sgd, wm, wd, wo)


def _rope_tables(seq_len):
    lane = np.arange(LANES)
    r = lane % 32
    freqs = np.power(np.float32(ROPE_BASE), -(r % 16).astype(np.float32) / np.float32(16.0))
    tok = np.arange(seq_len)
    pos = np.where((lane % 64 < 32)[None, :], (tok // GRID_W)[:, None], (tok % GRID_W)[:, None]).astype(np.float32)
    ang = pos * freqs[None, :]
    sign = np.where(r < 16, -1.0, 1.0).astype(np.float32)
    return jnp.asarray(np.cos(ang), F32), jnp.asarray(np.sin(ang) * sign[None, :], F32)


def _mixer_branches(x, row0, mods, mod_index, w, batch, seq_len, ctx, lam_init):
    rope_tabs = None if ctx is None else _rope_tables(seq_len)
    mq, mk, mv, so, dq, dk, dv, sgm, sgd, gates = _proj(
        x, row0, batch * seq_len, mods, mod_index, w["g_norm"], w["w_seg"], w["b_gate"], w["bd"], w["g_qn"],
        w["g_kn"], rope_tabs, seq_len)
    cache = None if ctx is None else (ctx[0], ctx[1])
    att = _attn(dq, dk, dv, cache, w["lam"], w["g_sub"], batch, seq_len, lam_init)
    state = None if ctx is None else ctx[2]
    res = _mlstm(mq, mk, mv, so, gates, w["g_mh"], state, batch, seq_len, emit_state=ctx is None)
    return res[0], att, sgm, sgd, dk, dv, res[1:]


def kernel(x_prompt, x_sample, c, cache_k, cache_v, state_C, state_n, state_m, c_ctx, w_ada, b_ada, g_norm, ffn1_w1, ffn1_w3, ffn1_w2, ffn2_w1, ffn2_w3, ffn2_w2, w_in, b_gate, g_qn, g_kn, lam_q1, lam_k1, lam_q2, lam_k2, g_sub, g_mh, w_br_m, w_br_d, w_out):
    depth = w_ada.shape[0]
    assert depth == 1
    l = 0
    bp, tp, _ = x_prompt.shape
    bs, ts, _ = x_sample.shape
    past = cache_k.shape[2]
    lam_init = 0.8 - 0.6 * math.exp(-0.3 * l)

    cvecs = jnp.concatenate([c_ctx[None, :], c, jnp.zeros((MOD_ROWS - 1 - bs, D_MODEL), F32)], axis=0)
    mods = _mods(cvecs, w_ada[l], b_ada[l])

    group = np.arange(MXU_DIM) // DA_HEAD_DIM
    w = dict(
        g_norm=g_norm[l],
        ffn1_w1=ffn1_w1[l], ffn1_w3=ffn1_w3[l], ffn1_w2=ffn1_w2[l],
        ffn2_w1=ffn2_w1[l], ffn2_w3=ffn2_w3[l], ffn2_w2=ffn2_w2[l],
        w_in_t=w_in[l].T,
        b_gate=b_gate[l:l + 1],
        bd=jnp.asarray(group[:, None] == group[None, :], BF16),
        g_qn=g_qn[l:l + 1], g_kn=g_kn[l:l + 1],
        lam=(lam_q1[l:l + 1], lam_k1[l:l + 1], lam_q2[l:l + 1], lam_k2[l:l + 1]),
        g_sub=g_sub[l:l + 1], g_mh=g_mh[l],
        w_br_m=w_br_m[l], w_br_d=w_br_d[l], w_out=w_out[l],
    )

    n_ctx, n_lat = bp * tp, bs * ts
    mod_index = lambda r: jnp.where(r < n_ctx, 0, 1 + (r - n_ctx) // ts)
    seg_start = lambda j: pl.multiple_of(jnp.where(j < N_SEG, _seg_start(j), GATE_LO), N_GATE_COLS)
    x1, w["w_seg"] = _ffn((x_prompt.reshape(n_ctx, D_MODEL), x_sample.reshape(n_lat, D_MODEL)), (n_ctx + n_lat,),
                          mods, mod_index, w["g_norm"], w["ffn1_w1"], w["ffn1_w3"], w["ffn1_w2"], base=0,
                          side=(w["w_in_t"], N_SEG + 1, seg_start))

    hm_p, att_p, sgm_p, sgd_p, new_k_t, new_v, (new_c, new_n, new_m) = _mixer_branches(
        x1, 0, mods, mod_index, w, bp, tp, None, lam_init)

    ctx = (cache_k[:, l].transpose(0, 2, 3, 4, 1).reshape(bs, D_MODEL, past),
           cache_v[:, l].reshape(bs, past * DA_HEADS, DA_V_DIM),
           (state_C, state_n, state_m))
    hm_s, att_s, sgm_s, sgd_s, _, _, _ = _mixer_branches(x1, n_ctx, mods, mod_index, w, bs, ts, ctx, lam_init)

    x2 = _merge(x1, mods, mod_index, (hm_p, hm_s), (att_p, att_s), (sgm_p, sgm_s), (sgd_p, sgd_s),
                w["w_br_m"], w["w_br_d"], w["w_out"])
    xp, xs = _ffn((x2,), (n_ctx, n_lat), mods, mod_index, w["g_norm"], w["ffn2_w1"], w["ffn2_w3"], w["ffn2_w2"],
                  base=6)

    return (xp.reshape(bp, tp, D_MODEL), xs.reshape(bs, ts, D_MODEL),
            new_k_t.reshape(bp, DA_HEADS, 2, DA_HEAD_DIM, tp).transpose(0, 4, 1, 2, 3)[:, None],
            new_v.reshape(bp, 1, tp, DA_HEADS, DA_V_DIM),
            new_c, new_n, new_m)
```

```python
import functools
import math

import jax
import jax.numpy as jnp
import numpy as np
from jax import lax
from jax.experimental import pallas as pl
from jax.experimental.pallas import tpu as pltpu

F32 = jnp.float32
BF16 = jnp.bfloat16

D_MODEL = 1024
D_FF = 2816
N_MOD = 9
GRID_W = 64
ML_HEADS = 4
ML_HEAD_DIM = 256
DA_HEADS = 8
DA_HEAD_DIM = 64
DA_V_DIM = 128
N_GATE_COLS = 16
CHUNK = 256
ROPE_BASE = 10000.0
QK_LOG2_SCALE = DA_HEAD_DIM ** -0.5 * math.log2(math.e)
EPS = 1e-6

LANES = 128
GATE_PAD = LANES
MXU_DIM = 256
VMEM_LIMIT = 56 * 1024 * 1024

MOD_ROWS = 8
FFN_TM = 512
FFN_TF = 256
PROJ_TM = 256
MERGE_TM = 512
ATT_TQ = 256
ATT_TK_MAX = 768
ATT_SHORT_SEQ_BATCH = 2


def _params(n_axes):
    return pltpu.CompilerParams(dimension_semantics=("arbitrary",) * n_axes,
                                vmem_limit_bytes=VMEM_LIMIT)


def _dot(a, b):
    return jnp.dot(a.astype(BF16), b.astype(BF16), preferred_element_type=F32)


def _dot_nt(a, b):
    return lax.dot_general(a.astype(BF16), b.astype(BF16), (((1,), (1,)), ((), ())),
                           preferred_element_type=F32)


def _sigmoid(x):
    return 1.0 / (1.0 + jnp.exp(-x))


def _log_sigmoid(x):
    return jnp.minimum(x, 0.0) - jnp.log1p(jnp.exp(-jnp.abs(x)))


def _modulated_norm(x, g, shift, scale):
    y = x * lax.rsqrt(jnp.mean(x * x, axis=-1, keepdims=True) + EPS) * g
    return y * (1.0 + scale) + shift


def _resident(shape):
    return pl.BlockSpec(shape, lambda *_: (0,) * len(shape), pipeline_mode=pl.Buffered(1))


def _mods_kernel(c_ref, w_ref, b_ref, o_ref):
    c = c_ref[...]
    o_ref[...] = _dot(c * _sigmoid(c), w_ref[...]) + b_ref[...]


def _mods(cvecs, w_ada, b_ada):
    n = N_MOD * D_MODEL
    tn = D_MODEL
    return pl.pallas_call(
        _mods_kernel,
        grid=(n // tn,),
        in_specs=[pl.BlockSpec((MOD_ROWS, D_MODEL), lambda j: (0, 0)),
                  pl.BlockSpec((D_MODEL, tn), lambda j: (0, j)),
                  pl.BlockSpec((1, tn), lambda j: (0, j))],
        out_specs=pl.BlockSpec((MOD_ROWS, tn), lambda j: (0, j)),
        out_shape=jax.ShapeDtypeStruct((MOD_ROWS, n), F32),
        compiler_params=_params(1),
        name="adaln_mods",
    )(cvecs, w_ada, b_ada.reshape(1, n))


def _mod_reader(m_ref, row):
    return lambda j: m_ref[pl.ds(row, 1), j * D_MODEL:(j + 1) * D_MODEL]


def _ffn_kernel(*refs, base, mod_index, in_tiles, out_tiles, side_blocks):
    n_in, n_out = len(in_tiles), len(out_tiles)
    x_refs, refs = refs[:n_in], refs[n_in:]
    (m_ref, g_ref, w1_ref, w3_ref, w2_ref), refs = refs[:5], refs[5:]
    if side_blocks:
        side_in_ref, refs = refs[0], refs[1:]
        side_out_ref, refs = refs[n_out], refs[:n_out] + refs[n_out + 1:]
    o_refs, (w1_sc, w3_sc, w2_sc, hh_sc, acc_sc) = refs[:n_out], refs[n_out:]
    g = pl.program_id(0)
    nf = D_FF // FFN_TF
    row_tile = g - (nf - 1)
    mod = _mod_reader(m_ref, mod_index(jnp.maximum(row_tile, 0) * FFN_TM))
    gain = g_ref[base // 3:base // 3 + 1, :]
    norm = lambda x: _modulated_norm(x, gain, mod(base), mod(base + 1)).astype(BF16)
    finish = lambda x, acc: x + 0.5 * mod(base + 2) * acc

    def tile(hh, f):
        a = _dot(hh, w1_sc[f])
        b = _dot(hh, w3_sc[f])
        return _dot(a * _sigmoid(a) * b, w2_sc[f])

    @pl.when(g < nf)
    def _():
        w1_sc[g] = w1_ref[...].astype(BF16)
        w3_sc[g] = w3_ref[...].astype(BF16)
        w2_sc[g] = w2_ref[...].astype(BF16)

        @pl.when(g == 0)
        def _():
            hh_sc[...] = norm(x_refs[0][...])
            acc_sc[...] = jnp.zeros(acc_sc.shape, F32)

        acc_sc[...] += tile(hh_sc[...], g)

        @pl.when(g == nf - 1)
        def _():
            o_refs[0][...] = finish(x_refs[0][...], acc_sc[...])

    if side_blocks:
        @pl.when((g >= nf) & (g - nf < side_blocks))
        def _():
            side_out_ref[...] = side_in_ref[...].astype(BF16)

    def full_tile(x_ref, o_ref):
        x = x_ref[...]
        hh = norm(x)
        acc = jnp.zeros(x.shape, F32)
        for f in range(nf):
            acc = acc + tile(hh, f)
        o_ref[...] = finish(x, acc)

    bounds = sorted(set(np.cumsum((0,) + in_tiles).tolist()) | set(np.cumsum((0,) + out_tiles).tolist()))
    for lo, hi in zip(bounds[:-1], bounds[1:]):
        k_in = int(np.searchsorted(np.cumsum(in_tiles), lo, side="right"))
        k_out = int(np.searchsorted(np.cumsum(out_tiles), lo, side="right"))
        pl.when((g >= nf) & (row_tile >= lo) & (row_tile < hi))(
            functools.partial(full_tile, x_refs[k_in], o_refs[k_out]))


def _ffn(xs, out_rows, mods, mod_index, g, w1, w3, w2, base, side=None):
    nf = D_FF // FFN_TF
    in_tiles = tuple(x.shape[0] // FFN_TM for x in xs)
    out_tiles = tuple(r // FFN_TM for r in out_rows)
    assert sum(in_tiles) == sum(out_tiles)
    side_in_specs, side_out_specs, side_out_shape, side_args, side_blocks = [], [], [], [], 0
    if side is not None:
        side_w, side_blocks, side_start = side
        assert side_blocks <= sum(in_tiles) - 1
        blk = lambda s: jnp.clip(s - nf, 0, side_blocks - 1)
        side_in_specs = [pl.BlockSpec((pl.Element(D_MODEL), pl.Element(D_MODEL)), lambda s: (side_start(blk(s)), 0))]
        side_out_specs = [pl.BlockSpec((None, D_MODEL, D_MODEL), lambda s: (blk(s), 0, 0))]
        side_out_shape = [jax.ShapeDtypeStruct((side_blocks, D_MODEL, D_MODEL), BF16)]
        side_args = [side_w]
    row_tile = lambda s: jnp.maximum(s - (nf - 1), 0)
    f_tile = lambda s: jnp.minimum(s, nf - 1)

    def part_spec(tiles, k):
        start = sum(tiles[:k])
        return pl.BlockSpec((FFN_TM, D_MODEL), lambda s: (jnp.clip(row_tile(s) - start, 0, tiles[k] - 1), 0))

    return pl.pallas_call(
        functools.partial(_ffn_kernel, base=base, mod_index=mod_index, in_tiles=in_tiles, out_tiles=out_tiles,
                          side_blocks=side_blocks),
        grid=(nf - 1 + sum(in_tiles),),
        in_specs=[part_spec(in_tiles, k) for k in range(len(xs))] + [
            _resident(mods.shape),
            _resident(g.shape),
            pl.BlockSpec((D_MODEL, FFN_TF), lambda s: (0, f_tile(s))),
            pl.BlockSpec((D_MODEL, FFN_TF), lambda s: (0, f_tile(s))),
            pl.BlockSpec((FFN_TF, D_MODEL), lambda s: (f_tile(s), 0))] + side_in_specs,
        out_specs=[part_spec(out_tiles, k) for k in range(len(out_rows))] + side_out_specs,
        out_shape=[jax.ShapeDtypeStruct((r, D_MODEL), F32) for r in out_rows] + side_out_shape,
        scratch_shapes=[pltpu.VMEM((nf, D_MODEL, FFN_TF), BF16), pltpu.VMEM((nf, D_MODEL, FFN_TF), BF16),
                        pltpu.VMEM((nf, FFN_TF, D_MODEL), BF16),
                        pltpu.VMEM((FFN_TM, D_MODEL), BF16), pltpu.VMEM((FFN_TM, D_MODEL), F32)],
        compiler_params=_params(1),
        name="ffn",
    )(*xs, mods, g, w1, w3, w2, *side_args)


def _group_norm64(x, bd, g):
    ss = _dot(x * x, bd)
    return x * lax.rsqrt(ss * (1.0 / DA_HEAD_DIM) + EPS) * g


def _rope(x, cos, sin_signed):
    first = (lax.broadcasted_iota(jnp.int32, x.shape, 1) % 32) < 16
    partner = jnp.where(first, pltpu.roll(x, LANES - 16, 1), pltpu.roll(x, 16, 1))
    return x * cos + partner * sin_signed


N_SEG = 9
PROJ_WARM = (N_SEG + 1) // 2
GATE_LO = 4 * D_MODEL


def _seg_start(s):
    return s * D_MODEL + N_GATE_COLS * (s >= 4)


def _proj_kernel(*refs, rope, row_of_step):
    (x_ref, m_ref, g_ref, wt_ref, bg_ref, bd_ref, gq_ref, gk_ref), refs = refs[:8], refs[8:]
    if rope:
        (cos_ref, sin_ref), refs = refs[:2], refs[2:]
    (mq_ref, mk_ref, mv_ref, so_ref, dq_ref, dk_ref, dv_ref, sgm_ref, sgd_ref, gates_ref,
     w_sc, wg_sc, hh_sc) = refs
    step = pl.program_id(0)
    mod = _mod_reader(m_ref, row_of_step(step))
    norm = lambda: _modulated_norm(x_ref[...], g_ref[1:2, :], mod(3), mod(4)).astype(BF16)
    groups = MXU_DIM // DA_HEAD_DIM
    tiled = lambda ref, scale: jnp.concatenate([ref[...] * scale] * groups, axis=1)
    gq_row, gk_row = tiled(gq_ref, QK_LOG2_SCALE), tiled(gk_ref, 1.0)
    bias = jnp.concatenate([bg_ref[...], jnp.zeros((1, GATE_PAD - N_GATE_COLS), F32)], axis=1)

    def gates(hh):
        y = _dot_nt(hh, wg_sc[...]) + bias
        col = lax.broadcasted_iota(jnp.int32, y.shape, 1)
        gates_ref[...] = jnp.where((col // ML_HEADS) % 2 == 1, _log_sigmoid(y), y)

    def qk_segment(y, gain_row, o_ref):
        bd = bd_ref[...]
        for c in range(D_MODEL // MXU_DIM):
            cols = slice(c * MXU_DIM, (c + 1) * MXU_DIM)
            z = _group_norm64(y[:, cols], bd, gain_row)
            if rope:
                z = jnp.concatenate(
                    [_rope(z[:, k * LANES:(k + 1) * LANES], cos_ref[...], sin_ref[...])
                     for k in range(MXU_DIM // LANES)], axis=1)
            o_ref[:, cols] = z.astype(o_ref.dtype)

    def cache_key_segment(hh):
        y_t = _dot_nt(w_sc[5], hh)
        z = y_t.reshape(D_MODEL // DA_HEAD_DIM, DA_HEAD_DIM, y_t.shape[1])
        ms = jnp.mean(z * z, axis=1, keepdims=True)
        gain = jnp.broadcast_to(gk_row[:, 0:LANES], (LANES, LANES)).T[0:DA_HEAD_DIM, :]
        gain = jnp.concatenate([gain] * (y_t.shape[1] // LANES), axis=1)
        dk_ref[...] = (z * lax.rsqrt(ms + EPS) * gain).reshape(y_t.shape)

    def segment(s, hh):
        if s in (0, 2):
            o_ref = {0: mq_ref, 2: mv_ref}[s]
            y_t = _dot_nt(w_sc[s], hh) * ((ML_HEAD_DIM ** -0.5) if s == 0 else 1.0)
            for j in range(PROJ_TM // CHUNK):
                o_ref[j] = y_t[:, j * CHUNK:(j + 1) * CHUNK].astype(o_ref.dtype)
            return
        if s == 5 and not rope:
            cache_key_segment(hh)
            return
        y = _dot_nt(hh, w_sc[s])
        if s == 4:
            qk_segment(y, gq_row, dq_ref)
        elif s == 5:
            qk_segment(y, gk_row, dk_ref)
        elif s in (3, 7, 8):
            {3: so_ref, 7: sgm_ref, 8: sgd_ref}[s][...] = _sigmoid(y)
        elif s == 6 and not rope:
            dv_ref[...] = y.reshape(y.shape[0], DA_HEADS, DA_V_DIM)
        else:
            o_ref = {1: mk_ref, 6: dv_ref}[s]
            o_ref[...] = y.astype(o_ref.dtype)

    for w in range(PROJ_WARM):
        @pl.when(step == w)
        def _(w=w):
            if w == 0:
                hh_sc[...] = norm()
            for j in range(2):
                s = 2 * w + j
                if s < N_SEG:
                    w_sc[s] = wt_ref[j]
                    segment(s, hh_sc[...])
                else:
                    wg_sc[...] = jnp.zeros(wg_sc.shape, BF16)
                    wg_sc[0:N_GATE_COLS, :] = wt_ref[j, 0:N_GATE_COLS, :]
                    gates(hh_sc[...])

    @pl.when(step >= PROJ_WARM)
    def _():
        hh = norm()
        gates(hh)
        for s in range(N_SEG):
            segment(s, hh)


def _proj(x, row0, rows, mods, mod_index, g, w_seg, b_gate, bd, gq, gk, rope_tabs, seq_len):
    rope = rope_tabs is not None
    row_tile = lambda s: jnp.maximum(s - (PROJ_WARM - 1), 0)
    row = lambda s: (row_tile(s), 0)
    tile = pl.BlockSpec((PROJ_TM, D_MODEL), row)
    w_block = (2, D_MODEL, D_MODEL)
    in_specs = [pl.BlockSpec((PROJ_TM, D_MODEL), lambda s: (row0 // PROJ_TM + row_tile(s), 0)),
                _resident(mods.shape),
                _resident(g.shape),
                pl.BlockSpec(w_block, lambda s: (jnp.minimum(s, PROJ_WARM - 1), 0, 0)),
                _resident(b_gate.shape),
                _resident((MXU_DIM, MXU_DIM)),
                _resident(gq.shape),
                _resident(gk.shape)]
    args = [x, mods, g, w_seg, b_gate, bd, gq, gk]
    if rope:
        tiles_per_seq = seq_len // PROJ_TM
        tab = pl.BlockSpec((PROJ_TM, LANES), lambda s: (row_tile(s) % tiles_per_seq, 0))
        in_specs += [tab, tab]
        args += list(rope_tabs)
    dtypes = [BF16, BF16, BF16, F32, BF16, BF16, BF16 if rope else F32, F32, F32]
    out_shape = [jax.ShapeDtypeStruct((rows, D_MODEL), dt) for dt in dtypes]
    out_shape.append(jax.ShapeDtypeStruct((rows, GATE_PAD), F32))
    out_specs = [tile] * 9 + [pl.BlockSpec((PROJ_TM, GATE_PAD), row)]
    slabs = PROJ_TM // CHUNK
    for j in (0, 2):
        out_shape[j] = jax.ShapeDtypeStruct((rows // CHUNK, D_MODEL, CHUNK), BF16)
        out_specs[j] = pl.BlockSpec((slabs, D_MODEL, CHUNK), lambda s: (row_tile(s), 0, 0))
    if not rope:
        assert seq_len == PROJ_TM
        out_shape[5] = jax.ShapeDtypeStruct((rows // seq_len, D_MODEL, seq_len), F32)
        out_specs[5] = pl.BlockSpec((None, D_MODEL, seq_len), lambda s: (row_tile(s), 0, 0))
        out_shape[6] = jax.ShapeDtypeStruct((rows, DA_HEADS, DA_V_DIM), F32)
        out_specs[6] = pl.BlockSpec((PROJ_TM, DA_HEADS, DA_V_DIM), lambda s: (row_tile(s), 0, 0))
    return pl.pallas_call(
        functools.partial(_proj_kernel, rope=rope,
                          row_of_step=lambda step: mod_index(row0 + row_tile(step) * PROJ_TM)),
        grid=(PROJ_WARM - 1 + rows // PROJ_TM,),
        in_specs=in_specs,
        out_specs=out_specs,
        out_shape=out_shape,
        scratch_shapes=[pltpu.VMEM((N_SEG, D_MODEL, D_MODEL), BF16), pltpu.VMEM((GATE_PAD, D_MODEL), BF16),
                        pltpu.VMEM((PROJ_TM, D_MODEL), BF16)],
        compiler_params=_params(1),
        name="mixer_in_proj",
    )(*args)


def _lambda(lam_refs, lam_init):
    q1, k1, q2, k2 = (r[...] for r in lam_refs)
    s1 = jnp.sum(q1 * k1, axis=1, keepdims=True)
    s2 = jnp.sum(q2 * k2, axis=1, keepdims=True)
    return jnp.exp(s1) - jnp.exp(s2) + lam_init


def _attn_kernel(*refs, cached, seq, nb, lam_init):
    if cached:
        (q_ref, k_ref, v_ref, ck_ref, cv_ref), refs = refs[:5], refs[5:]
    else:
        (q_ref, k_ref, v_ref), refs = refs[:3], refs[3:]
    lam_refs, (gs_ref, o_ref, kall_sc, vt_sc, s_sc) = refs[:4], refs[4:]
    units = [(bb, h) for bb in range(nb) for h in range(DA_HEADS)]

    def cache_order_kv(kt_ref, vr_ref, u, h, lo, n):
        kall_sc[u, lo:lo + n, :] = kt_ref[h * DA_V_DIM:(h + 1) * DA_V_DIM, :].T.astype(BF16)
        vt_sc[u, :, lo:lo + n] = vr_ref[pl.ds(h, n, stride=DA_HEADS), :].T.astype(BF16)

    @pl.when(pl.program_id(1) == 0)
    def _():
        for u, (bb, h) in enumerate(units):
            if cached:
                cols = slice(h * DA_V_DIM, (h + 1) * DA_V_DIM)
                kall_sc[u, 0:seq, :] = k_ref[:, cols].astype(BF16)
                vt_sc[u, :, 0:seq] = v_ref[:, cols].astype(F32).T.astype(BF16)
                cache_order_kv(ck_ref.at[bb], cv_ref.at[bb], u, h, seq, ck_ref.shape[2])
            else:
                cache_order_kv(k_ref.at[bb], v_ref.at[bb], u, h, 0, seq)

    lam = _lambda(lam_refs, lam_init)
    sub_gain = gs_ref[...] * (1.0 - lam_init)
    lane = lax.broadcasted_iota(jnp.int32, (1, DA_V_DIM), 1)
    comp_masks = [lane < DA_HEAD_DIM, lane >= DA_HEAD_DIM]
    tq = q_ref.shape[0] // nb
    n_keys = kall_sc.shape[1]
    n_tiles = pl.cdiv(n_keys, ATT_TK_MAX)
    tk = n_keys // n_tiles

    def stacked_q(u):
        bb, h = units[u]
        q = q_ref[bb * tq:(bb + 1) * tq, h * DA_V_DIM:(h + 1) * DA_V_DIM].astype(BF16)
        return jnp.concatenate([jnp.where(m, q, jnp.zeros_like(q)) for m in comp_masks], axis=0)

    def score_tile(h, j, qq, m8):
        rows = slice(j * tk, (j + 1) * tk)
        st = _dot_nt(kall_sc[h, rows, :], qq)
        s_sc[h % 2, rows, :] = st
        t8 = jnp.max(st.reshape(tk // 8, 8, 2 * tq), axis=0)
        return t8 if m8 is None else jnp.maximum(m8, t8)

    def prob_tile(h, j, mx, d8, pv):
        rows = slice(j * tk, (j + 1) * tk)
        e = jnp.exp2(s_sc[h % 2, rows, :] - mx)
        s8 = jnp.sum(e.reshape(tk // 8, 8, 2 * tq), axis=0)
        p = _dot(vt_sc[h, :, rows], e)
        return (s8 if d8 is None else d8 + s8), (p if pv is None else pv + p)

    qq = stacked_q(0)
    m8 = None
    for j in range(n_tiles):
        m8 = score_tile(0, j, qq, m8)
    for h in range(len(units)):
        mx = jnp.max(m8, axis=0, keepdims=True)
        if h + 1 < len(units):
            qq = stacked_q(h + 1)
        m8, d8, pv = None, None, None
        for j in range(n_tiles):
            if h + 1 < len(units):
                m8 = score_tile(h + 1, j, qq, m8)
            d8, pv = prob_tile(h, j, mx, d8, pv)
        inv = 1.0 / jnp.sum(d8, axis=0, keepdims=True)
        out_t = pv[:, :tq] * inv[:, :tq] - pv[:, tq:] * (lam * inv[:, tq:])
        out_t = out_t * lax.rsqrt(jnp.mean(out_t * out_t, axis=0, keepdims=True) + EPS)
        bb, head = units[h]
        o_ref[bb * tq:(bb + 1) * tq, head * DA_V_DIM:(head + 1) * DA_V_DIM] = (out_t.T * sub_gain).astype(o_ref.dtype)


def _attn(q, k, v, cache, lam_vecs, g_sub, batch, seq_len, lam_init):
    rows = q.shape[0]
    nq = seq_len // ATT_TQ
    nb = ATT_SHORT_SEQ_BATCH if (cache is None and nq == 1) else 1
    assert batch % nb == 0
    q_spec = pl.BlockSpec((nb * ATT_TQ, D_MODEL), lambda b, i: (b * nq + i, 0))
    cache_specs = lambda n: [pl.BlockSpec((nb, D_MODEL, n), lambda b, i: (b, 0, 0)),
                             pl.BlockSpec((nb, n * DA_HEADS, DA_V_DIM), lambda b, i: (b, 0, 0))]
    n_keys = seq_len
    if cache is not None:
        past = cache[0].shape[2]
        n_keys += past
        kv_spec = pl.BlockSpec((seq_len, D_MODEL), lambda b, i: (b, 0))
        in_specs = [q_spec, kv_spec, kv_spec] + cache_specs(past)
        args = [q, k, v] + list(cache)
    else:
        in_specs = [q_spec] + cache_specs(seq_len)
        args = [q, k, v.reshape(batch, seq_len * DA_HEADS, DA_V_DIM)]
    in_specs += [_resident((1, DA_HEAD_DIM))] * 4 + [_resident((1, DA_V_DIM))]
    args += list(lam_vecs) + [g_sub]
    return pl.pallas_call(
        functools.partial(_attn_kernel, cached=cache is not None, seq=seq_len, nb=nb, lam_init=lam_init),
        grid=(batch // nb, nq),
        in_specs=in_specs,
        out_specs=q_spec,
        out_shape=jax.ShapeDtypeStruct((rows, D_MODEL), BF16),
        scratch_shapes=[pltpu.VMEM((nb * DA_HEADS, n_keys, DA_V_DIM), BF16),
                        pltpu.VMEM((nb * DA_HEADS, DA_V_DIM, n_keys), BF16),
                        pltpu.VMEM((2, n_keys, 2 * ATT_TQ), F32)],
        compiler_params=_params(2),
        name="diff_attention",
    )(*args)


def _per_chain(fn, a, b):
    return jnp.stack([fn(a[i], b[i]) for i in range(a.shape[0])])


ML_EXT = 8
ML_SHORT_SEQ_BATCH = 2


def _rows(x):
    return jnp.stack([x[i:i + 1, :] for i in range(x.shape[0])])


def _split3(x):
    hi = x.astype(BF16)
    r = x - hi.astype(F32)
    mid = r.astype(BF16)
    lo = (r - mid.astype(F32)).astype(BF16)
    return jnp.concatenate([hi, mid, lo], axis=1)


def _chunk_scan(x, reverse_rows, tri_prefix, tri_suffix):
    parts = _split3(x)
    return jnp.where(reverse_rows, _dot(parts, tri_suffix), _dot(parts, tri_prefix))


def _paired_value_matmul(v_t, sc):
    B, L, _ = sc.shape
    zero = jnp.zeros((L, L), sc.dtype)
    out = []
    for i in range(0, B, 2):
        lhs = jnp.concatenate([v_t[i], v_t[i + 1]], axis=1)
        rhs = jnp.concatenate([jnp.concatenate([sc[i], zero], axis=1),
                               jnp.concatenate([zero, sc[i + 1]], axis=1)], axis=0)
        both = _dot(lhs, rhs)
        out += [both[:, :L], both[:, L:]]
    return jnp.stack(out)


def _mlstm_step(k, q_t, v_t, ic, fc, CT, m, seen_t, reverse_rows):
    B, L = ic.shape
    d = k.shape[2]
    tri = lambda keep: jnp.concatenate([jnp.where(keep, 1.0, 0.0).astype(BF16)] * 3, axis=0)
    b2 = _chunk_scan(fc, reverse_rows, tri(seen_t[0]), tri(seen_t[B - 1]))
    u2 = ic - b2
    u_t = jnp.concatenate([u2, jnp.zeros((L - B, L), F32)], axis=0).T
    u_col = jnp.stack([jnp.broadcast_to(u_t[:, i:i + 1], (L, L)) for i in range(B)])
    b, i_g, f_g = _rows(b2), _rows(ic), _rows(fc)
    b_last = jnp.sum(f_g, axis=2, keepdims=True)

    log_d = jnp.where(seen_t, b + u_col, -jnp.inf)
    a = b + m
    m_t = jnp.maximum(a, jnp.max(log_d, axis=1, keepdims=True))
    dmat = jnp.exp(log_d - m_t)
    inter = jnp.exp(a - m_t)
    sc = _per_chain(_dot, k, q_t) * dmat
    cq = _per_chain(_dot, CT, q_t)
    value_matmul = _paired_value_matmul if 2 * L <= MXU_DIM else functools.partial(_per_chain, _dot)
    num = value_matmul(v_t, sc.astype(BF16)) + inter * cq[:, :d, :]
    den = jnp.sum(sc, axis=1, keepdims=True) + inter * cq[:, d:d + 1, :]
    h_t = num * (1.0 / jnp.maximum(jnp.abs(den), jnp.exp(-m_t)))

    g = b_last - b + i_g
    m_new = jnp.maximum(b_last + m, jnp.max(g, axis=2, keepdims=True))
    w = jnp.exp(g - m_new)
    decay = jnp.exp(b_last + m - m_new)
    vw = jnp.concatenate([v_t.astype(F32), jnp.ones((B, ML_EXT, L), F32)], axis=1) * w
    CT_new = decay * CT + _per_chain(_dot, vw, k)
    return h_t, CT_new, m_new


def _mlstm_kernel(*refs, seq_len, nb, has_state, emit_state):
    q_ref, k_ref, v_ref, so_ref, g_ref, gmh_ref = refs[:6]
    refs = refs[6:]
    if has_state:
        (c0_ref, n0_ref, m0_ref), refs = refs[:3], refs[3:]
    hm_ref, refs = refs[0], refs[1:]
    if emit_state:
        (c_out_ref, n_out_ref, m_out_ref), refs = refs[:3], refs[3:]
    ct_sc, m_sc, gr_sc, h_sc = refs

    nc = seq_len // CHUNK
    d_head = ML_HEAD_DIM
    chains = [(bb, d, h) for bb in range(nb) for d in range(2) for h in range(ML_HEADS)]
    n_chain = len(chains)
    for i, (bb, d, h) in enumerate(chains):
        if has_state:
            ct_sc[i, 0:d_head, :] = c0_ref[bb, d, h].T
            ct_sc[i, d_head:d_head + ML_EXT, :] = jnp.broadcast_to(n0_ref[bb, d, h:h + 1, :], (ML_EXT, d_head))
            m_sc[i] = jnp.full((1, 1), m0_ref[pl.program_id(0) * nb + bb, 0, d, h], F32)
        else:
            ct_sc[i] = jnp.zeros((d_head + ML_EXT, d_head), F32)
            m_sc[i] = jnp.zeros((1, 1), F32)
    for c in range(nb * nc):
        gr_sc[c] = g_ref[c * CHUNK:(c + 1) * CHUNK, :].T

    s_idx = lax.broadcasted_iota(jnp.int32, (CHUNK, CHUNK), 0)
    t_idx = lax.broadcasted_iota(jnp.int32, (CHUNK, CHUNK), 1)
    seen_t = jnp.stack([(s_idx >= t_idx) if d else (s_idx <= t_idx) for _, d, _ in chains])
    reverse_rows = (lax.broadcasted_iota(jnp.int32, (n_chain, CHUNK), 0) // ML_HEADS) % 2 == 1

    def step(c_fwd, c_bwd, rows_of):
        chunk_of = (c_fwd, c_bwd)
        lo = 2 * ML_HEADS
        ic, fc = [], []
        for bb in range(nb):
            g_fwd, g_bwd = gr_sc[bb * nc + c_fwd], gr_sc[bb * nc + c_bwd]
            ic += [g_fwd[0:ML_HEADS], g_bwd[lo:lo + ML_HEADS]]
            fc += [g_fwd[ML_HEADS:lo], g_bwd[lo + ML_HEADS:2 * lo]]
        head = [slice(h * d_head, (h + 1) * d_head) for _, _, h in chains]
        stack = lambda pick: jnp.stack([pick(i, bb * nc + chunk_of[d]) for i, (bb, d, _) in enumerate(chains)])
        h_t, CT_new, m_new = _mlstm_step(
            stack(lambda i, c: k_ref[rows_of(c), head[i]]),
            stack(lambda i, c: q_ref[c, head[i], :]),
            stack(lambda i, c: v_ref[c, head[i], :]),
            jnp.concatenate(ic, axis=0), jnp.concatenate(fc, axis=0), ct_sc[...], m_sc[...], seen_t, reverse_rows)
        ct_sc[...] = CT_new
        m_sc[...] = m_new
        for i, (bb, d, _) in enumerate(chains):
            h_sc[d, bb * nc + chunk_of[d], head[i], :] = h_t[i]

    if nc <= 2:
        for c in range(nc):
            step(c, nc - 1 - c, lambda cc: slice(cc * CHUNK, (cc + 1) * CHUNK))
    else:
        def body(c, carry):
            step(c, nc - 1 - c, lambda cc: pl.ds(pl.multiple_of(cc * CHUNK, CHUNK), CHUNK))
            return carry
        lax.fori_loop(0, nc, body, 0, unroll=2)

    for h in range(ML_HEADS):
        hcols = slice(h * d_head, (h + 1) * d_head)
        gain = jnp.broadcast_to(gmh_ref[h:h + 1, :], (CHUNK, d_head)).T
        for c in range(nb * nc):
            rows = slice(c * CHUNK, (c + 1) * CHUNK)
            hsum = h_sc[0, c, hcols, :] + h_sc[1, c, hcols, :]
            hn = hsum * lax.rsqrt(jnp.mean(hsum * hsum, axis=0, keepdims=True) + EPS) * gain
            hm_ref[rows, hcols] = (hn.T * so_ref[rows, hcols]).astype(hm_ref.dtype)
    if emit_state:
        for i, (bb, d, h) in enumerate(chains):
            c_out_ref[bb, d, h] = ct_sc[i, 0:d_head, :].T
            n_out_ref[bb, d, h:h + 1, :] = ct_sc[i, d_head:d_head + 1, :]
            m_out_ref[bb, d:d + 1, h:h + 1] = m_sc[i]


def _mlstm(q_t, k, v_t, so, gates, g_mh, state, batch, seq_len, emit_state):
    rows = k.shape[0]
    d = ML_HEAD_DIM
    nc = seq_len // CHUNK
    nb = ML_SHORT_SEQ_BATCH if nc <= 2 else 1
    assert batch % nb == 0
    tile = pl.BlockSpec((nb * seq_len, D_MODEL), lambda b: (b, 0))
    tile_t = pl.BlockSpec((nb * nc, D_MODEL, CHUNK), lambda b: (b, 0, 0))
    in_specs = [tile_t, tile, tile_t, tile,
                pl.BlockSpec((nb * seq_len, GATE_PAD), lambda b: (b, 0)),
                _resident((ML_HEADS, d))]
    args = [q_t, k, v_t, so, gates, g_mh]
    c_spec = pl.BlockSpec((nb, None, 2, ML_HEADS, d, d), lambda b: (b, 0, 0, 0, 0, 0))
    n_spec = pl.BlockSpec((nb, None, 2, ML_HEADS, d), lambda b: (b, 0, 0, 0, 0))
    if state is not None:
        in_specs += [c_spec, n_spec, pl.BlockSpec(memory_space=pltpu.SMEM)]
        args += list(state)
    out_specs = [tile]
    out_shape = [jax.ShapeDtypeStruct((rows, D_MODEL), BF16)]
    if emit_state:
        out_specs += [c_spec, n_spec, pl.BlockSpec((nb, None, 2, ML_HEADS), lambda b: (b, 0, 0, 0))]
        out_shape += [jax.ShapeDtypeStruct((batch, 1, 2, ML_HEADS, d, d), F32),
                      jax.ShapeDtypeStruct((batch, 1, 2, ML_HEADS, d), F32),
                      jax.ShapeDtypeStruct((batch, 1, 2, ML_HEADS), F32)]
    n_state = 2 * ML_HEADS * nb
    return pl.pallas_call(
        functools.partial(_mlstm_kernel, seq_len=seq_len, nb=nb, has_state=state is not None,
                          emit_state=emit_state),
        grid=(batch // nb,),
        in_specs=in_specs,
        out_specs=out_specs,
        out_shape=out_shape,
        scratch_shapes=[pltpu.VMEM((n_state, d + ML_EXT, d), F32), pltpu.VMEM((n_state, 1, 1), F32),
                        pltpu.VMEM((nb * nc, GATE_PAD, CHUNK), F32),
                        pltpu.VMEM((2, nb * nc, D_MODEL, CHUNK), F32)],
        compiler_params=_params(1),
        name="mlstm",
    )(*args)


def _merge_kernel(*refs, tiles, mod_index):
    n = len(tiles)
    (x_ref, m_ref), refs = refs[:2], refs[2:]
    parts = [refs[j * n:(j + 1) * n] for j in range(4)]
    wm_ref, wd_ref, wo_ref, o_ref, w_sc = refs[4 * n:]
    i = pl.program_id(0)

    @pl.when(i == 0)
    def _():
        for j, w_ref in enumerate((wm_ref, wd_ref, wo_ref)):
            w_sc[j] = w_ref[...].astype(BF16)

    def tile(hm_ref, att_ref, sgm_ref, sgd_ref):
        y = sgm_ref[...] * _dot(hm_ref[...], w_sc[0]) + sgd_ref[...] * _dot(att_ref[...], w_sc[1])
        o_ref[...] = x_ref[...] + _mod_reader(m_ref, mod_index(i * MERGE_TM))(5) * _dot(y, w_sc[2])

    starts = np.cumsum((0,) + tiles).tolist()
    for k in range(n):
        pl.when((i >= starts[k]) & (i < starts[k + 1]))(functools.partial(tile, *[p[k] for p in parts]))


def _merge(x, mods, mod_index, hm, att, sgm, sgd, wm, wd, wo):
    rows = x.shape[0]
    tiles = tuple(h.shape[0] // MERGE_TM for h in hm)
    assert sum(tiles) * MERGE_TM == rows
    tile = pl.BlockSpec((MERGE_TM, D_MODEL), lambda i: (i, 0))

    def part_spec(k):
        start = sum(tiles[:k])
        return pl.BlockSpec((MERGE_TM, D_MODEL), lambda i: (jnp.clip(i - start, 0, tiles[k] - 1), 0))

    part_specs = [part_spec(k) for k in range(len(tiles))]
    w_spec = _resident((D_MODEL, D_MODEL))
    return pl.pallas_call(
        functools.partial(_merge_kernel, tiles=tiles, mod_index=mod_index),
        grid=(rows // MERGE_TM,),
        in_specs=[tile, _resident(mods.shape)]
                 + part_specs * 4 + [w_spec, w_spec, w_spec],
        out_specs=tile,
        out_shape=jax.ShapeDtypeStruct((rows, D_MODEL), F32),
        scratch_shapes=[pltpu.VMEM((3, D_MODEL, D_MODEL), BF16)],
        compiler_params=_params(1),
        name="branch_merge",
    )(x, mods, *hm, *att, *sgm, *sgd, wm, wd, wo)


def _rope_tables(seq_len):
    lane = np.arange(LANES)
    r = lane % 32
    freqs = np.power(np.float32(ROPE_BASE), -(r % 16).astype(np.float32) / np.float32(16.0))
    tok = np.arange(seq_len)
    pos = np.where((lane % 64 < 32)[None, :], (tok // GRID_W)[:, None], (tok % GRID_W)[:, None]).astype(np.float32)
    ang = pos * freqs[None, :]
    sign = np.where(r < 16, -1.0, 1.0).astype(np.float32)
    return jnp.asarray(np.cos(ang), F32), jnp.asarray(np.sin(ang) * sign[None, :], F32)


def _mixer_branches(x, row0, mods, mod_index, w, batch, seq_len, ctx, lam_init):
    rope_tabs = None if ctx is None else _rope_tables(seq_len)
    mq, mk, mv, so, dq, dk, dv, sgm, sgd, gates = _proj(
        x, row0, batch * seq_len, mods, mod_index, w["g_norm"], w["w_seg"], w["b_gate"], w["bd"], w["g_qn"],
        w["g_kn"], rope_tabs, seq_len)
    cache = None if ctx is None else (ctx[0], ctx[1])
    att = _attn(dq, dk, dv, cache, w["lam"], w["g_sub"], batch, seq_len, lam_init)
    state = None if ctx is None else ctx[2]
    res = _mlstm(mq, mk, mv, so, gates, w["g_mh"], state, batch, seq_len, emit_state=ctx is None)
    return res[0], att, sgm, sgd, dk, dv, res[1:]


def kernel(x_prompt, x_sample, c, cache_k, cache_v, state_C, state_n, state_m, c_ctx, w_ada, b_ada, g_norm, ffn1_w1, ffn1_w3, ffn1_w2, ffn2_w1, ffn2_w3, ffn2_w2, w_in, b_gate, g_qn, g_kn, lam_q1, lam_k1, lam_q2, lam_k2, g_sub, g_mh, w_br_m, w_br_d, w_out):
    depth = w_ada.shape[0]
    assert depth == 1
    l = 0
    bp, tp, _ = x_prompt.shape
    bs, ts, _ = x_sample.shape
    past = cache_k.shape[2]
    lam_init = 0.8 - 0.6 * math.exp(-0.3 * l)

    cvecs = jnp.concatenate([c_ctx[None, :], c, jnp.zeros((MOD_ROWS - 1 - bs, D_MODEL), F32)], axis=0)
    mods = _mods(cvecs, w_ada[l], b_ada[l])

    group = np.arange(MXU_DIM) // DA_HEAD_DIM
    w = dict(
        g_norm=g_norm[l],
        ffn1_w1=ffn1_w1[l], ffn1_w3=ffn1_w3[l], ffn1_w2=ffn1_w2[l],
        ffn2_w1=ffn2_w1[l], ffn2_w3=ffn2_w3[l], ffn2_w2=ffn2_w2[l],
        w_in_t=w_in[l].T,
        b_gate=b_gate[l:l + 1],
        bd=jnp.asarray(group[:, None] == group[None, :], BF16),
        g_qn=g_qn[l:l + 1], g_kn=g_kn[l:l + 1],
        lam=(lam_q1[l:l + 1], lam_k1[l:l + 1], lam_q2[l:l + 1], lam_k2[l:l + 1]),
        g_sub=g_sub[l:l + 1], g_mh=g_mh[l],
        w_br_m=w_br_m[l], w_br_d=w_br_d[l], w_out=w_out[l],
    )

    n_ctx, n_lat = bp * tp, bs * ts
    mod_index = lambda r: jnp.where(r < n_ctx, 0, 1 + (r - n_ctx) // ts)
    seg_start = lambda j: pl.multiple_of(jnp.where(j < N_SEG, _seg_start(j), GATE_LO), N_GATE_COLS)
    x1, w["w_seg"] = _ffn((x_prompt.reshape(n_ctx, D_MODEL), x_sample.reshape(n_lat, D_MODEL)), (n_ctx + n_lat,),
                          mods, mod_index, w["g_norm"], w["ffn1_w1"], w["ffn1_w3"], w["ffn1_w2"], base=0,
                          side=(w["w_in_t"], N_SEG + 1, seg_start))

    hm_p, att_p, sgm_p, sgd_p, new_k_t, new_v, (new_c, new_n, new_m) = _mixer_branches(
        x1, 0, mods, mod_index, w, bp, tp, None, lam_init)

    ctx = (cache_k[:, l].transpose(0, 2, 3, 4, 1).reshape(bs, D_MODEL, past),
           cache_v[:, l].reshape(bs, past * DA_HEADS, DA_V_DIM),
           (state_C, state_n, state_m))
    hm_s, att_s, sgm_s, sgd_s, _, _, _ = _mixer_branches(x1, n_ctx, mods, mod_index, w, bs, ts, ctx, lam_init)

    x2 = _merge(x1, mods, mod_index, (hm_p, hm_s), (att_p, att_s), (sgm_p, sgm_s), (sgd_p, sgd_s),
                w["w_br_m"], w["w_br_d"], w["w_out"])
    xp, xs = _ffn((x2,), (n_ctx, n_lat), mods, mod_index, w["g_norm"], w["ffn2_w1"], w["ffn2_w3"], w["ffn2_w2"],
                  base=6)

    return (xp.reshape(bp, tp, D_MODEL), xs.reshape(bs, ts, D_MODEL),
            new_k_t.reshape(bp, DA_HEADS, 2, DA_HEAD_DIM, tp).transpose(0, 4, 1, 2, 3)[:, None],
            new_v.reshape(bp, 1, tp, DA_HEADS, DA_V_DIM),
            new_c, new_n, new_m)
```

```python
import functools
import math

import jax
import jax.numpy as jnp
import numpy as np
from jax import lax
from jax.experimental import pallas as pl
from jax.experimental.pallas import tpu as pltpu

F32 = jnp.float32
BF16 = jnp.bfloat16

D_MODEL = 1024
D_FF = 2816
N_MOD = 9
GRID_W = 64
ML_HEADS = 4
ML_HEAD_DIM = 256
DA_HEADS = 8
DA_HEAD_DIM = 64
DA_V_DIM = 128
N_GATE_COLS = 16
CHUNK = 256
ROPE_BASE = 10000.0
QK_LOG2_SCALE = DA_HEAD_DIM ** -0.5 * math.log2(math.e)
EPS = 1e-6

LANES = 128
GATE_PAD = LANES
MXU_DIM = 256
VMEM_LIMIT = 56 * 1024 * 1024

MOD_ROWS = 8
FFN_TM = 512
FFN_TF = 256
PROJ_TM = 256
MERGE_TM = 512
ATT_TQ = 256
ATT_TK_MAX = 768
ATT_SHORT_SEQ_BATCH = 2


def _params(n_axes):
    return pltpu.CompilerParams(dimension_semantics=("arbitrary",) * n_axes,
                                vmem_limit_bytes=VMEM_LIMIT)


def _dot(a, b):
    return jnp.dot(a.astype(BF16), b.astype(BF16), preferred_element_type=F32)


def _dot_nt(a, b):
    return lax.dot_general(a.astype(BF16), b.astype(BF16), (((1,), (1,)), ((), ())),
                           preferred_element_type=F32)


def _sigmoid(x):
    return 1.0 / (1.0 + jnp.exp(-x))


def _log_sigmoid(x):
    return jnp.minimum(x, 0.0) - jnp.log1p(jnp.exp(-jnp.abs(x)))


def _modulated_norm(x, g, shift, scale):
    y = x * lax.rsqrt(jnp.mean(x * x, axis=-1, keepdims=True) + EPS) * g
    return y * (1.0 + scale) + shift


def _resident(shape):
    return pl.BlockSpec(shape, lambda *_: (0,) * len(shape), pipeline_mode=pl.Buffered(1))


def _mods_kernel(c_ref, w_ref, b_ref, o_ref):
    c = c_ref[...]
    o_ref[...] = _dot(c * _sigmoid(c), w_ref[...]) + b_ref[...]


def _mods(cvecs, w_ada, b_ada):
    n = N_MOD * D_MODEL
    tn = D_MODEL
    return pl.pallas_call(
        _mods_kernel,
        grid=(n // tn,),
        in_specs=[pl.BlockSpec((MOD_ROWS, D_MODEL), lambda j: (0, 0)),
                  pl.BlockSpec((D_MODEL, tn), lambda j: (0, j)),
                  pl.BlockSpec((1, tn), lambda j: (0, j))],
        out_specs=pl.BlockSpec((MOD_ROWS, tn), lambda j: (0, j)),
        out_shape=jax.ShapeDtypeStruct((MOD_ROWS, n), F32),
        compiler_params=_params(1),
        name="adaln_mods",
    )(cvecs, w_ada, b_ada.reshape(1, n))


def _mod_reader(m_ref, row):
    return lambda j: m_ref[pl.ds(row, 1), j * D_MODEL:(j + 1) * D_MODEL]


def _ffn_kernel(*refs, base, mod_index, in_tiles, out_tiles, side_blocks):
    n_in, n_out = len(in_tiles), len(out_tiles)
    x_refs, refs = refs[:n_in], refs[n_in:]
    (m_ref, g_ref, w1_ref, w3_ref, w2_ref), refs = refs[:5], refs[5:]
    if side_blocks:
        side_in_ref, refs = refs[0], refs[1:]
        side_out_ref, refs = refs[n_out], refs[:n_out] + refs[n_out + 1:]
    o_refs, (w1_sc, w3_sc, w2_sc, hh_sc, acc_sc) = refs[:n_out], refs[n_out:]
    g = pl.program_id(0)
    nf = D_FF // FFN_TF
    row_tile = g - (nf - 1)
    mod = _mod_reader(m_ref, mod_index(jnp.maximum(row_tile, 0) * FFN_TM))
    gain = g_ref[base // 3:base // 3 + 1, :]
    norm = lambda x: _modulated_norm(x, gain, mod(base), mod(base + 1)).astype(BF16)
    finish = lambda x, acc: x + 0.5 * mod(base + 2) * acc

    def tile(hh, f):
        a = _dot(hh, w1_sc[f])
        b = _dot(hh, w3_sc[f])
        return _dot(a * _sigmoid(a) * b, w2_sc[f])

    @pl.when(g < nf)
    def _():
        w1_sc[g] = w1_ref[...].astype(BF16)
        w3_sc[g] = w3_ref[...].astype(BF16)
        w2_sc[g] = w2_ref[...].astype(BF16)

        @pl.when(g == 0)
        def _():
            hh_sc[...] = norm(x_refs[0][...])
            acc_sc[...] = jnp.zeros(acc_sc.shape, F32)

        acc_sc[...] += tile(hh_sc[...], g)

        @pl.when(g == nf - 1)
        def _():
            o_refs[0][...] = finish(x_refs[0][...], acc_sc[...])

    if side_blocks:
        @pl.when((g >= nf) & (g - nf < side_blocks))
        def _():
            side_out_ref[...] = side_in_ref[...].astype(BF16)

    def full_tile(x_ref, o_ref):
        x = x_ref[...]
        hh = norm(x)
        acc = jnp.zeros(x.shape, F32)
        for f in range(nf):
            acc = acc + tile(hh, f)
        o_ref[...] = finish(x, acc)

    bounds = sorted(set(np.cumsum((0,) + in_tiles).tolist()) | set(np.cumsum((0,) + out_tiles).tolist()))
    for lo, hi in zip(bounds[:-1], bounds[1:]):
        k_in = int(np.searchsorted(np.cumsum(in_tiles), lo, side="right"))
        k_out = int(np.searchsorted(np.cumsum(out_tiles), lo, side="right"))
        pl.when((g >= nf) & (row_tile >= lo) & (row_tile < hi))(
            functools.partial(full_tile, x_refs[k_in], o_refs[k_out]))


def _ffn(xs, out_rows, mods, mod_index, g, w1, w3, w2, base, side=None):
    nf = D_FF // FFN_TF
    in_tiles = tuple(x.shape[0] // FFN_TM for x in xs)
    out_tiles = tuple(r // FFN_TM for r in out_rows)
    assert sum(in_tiles) == sum(out_tiles)
    side_in_specs, side_out_specs, side_out_shape, side_args, side_blocks = [], [], [], [], 0
    if side is not None:
        side_w, side_blocks, side_start = side
        assert side_blocks <= sum(in_tiles) - 1
        blk = lambda s: jnp.clip(s - nf, 0, side_blocks - 1)
        side_in_specs = [pl.BlockSpec((pl.Element(D_MODEL), pl.Element(D_MODEL)), lambda s: (side_start(blk(s)), 0))]
        side_out_specs = [pl.BlockSpec((None, D_MODEL, D_MODEL), lambda s: (blk(s), 0, 0))]
        side_out_shape = [jax.ShapeDtypeStruct((side_blocks, D_MODEL, D_MODEL), BF16)]
        side_args = [side_w]
    row_tile = lambda s: jnp.maximum(s - (nf - 1), 0)
    f_tile = lambda s: jnp.minimum(s, nf - 1)

    def part_spec(tiles, k):
        start = sum(tiles[:k])
        return pl.BlockSpec((FFN_TM, D_MODEL), lambda s: (jnp.clip(row_tile(s) - start, 0, tiles[k] - 1), 0))

    return pl.pallas_call(
        functools.partial(_ffn_kernel, base=base, mod_index=mod_index, in_tiles=in_tiles, out_tiles=out_tiles,
                          side_blocks=side_blocks),
        grid=(nf - 1 + sum(in_tiles),),
        in_specs=[part_spec(in_tiles, k) for k in range(len(xs))] + [
            _resident(mods.shape),
            _resident(g.shape),
            pl.BlockSpec((D_MODEL, FFN_TF), lambda s: (0, f_tile(s))),
            pl.BlockSpec((D_MODEL, FFN_TF), lambda s: (0, f_tile(s))),
            pl.BlockSpec((FFN_TF, D_MODEL), lambda s: (f_tile(s), 0))] + side_in_specs,
        out_specs=[part_spec(out_tiles, k) for k in range(len(out_rows))] + side_out_specs,
        out_shape=[jax.ShapeDtypeStruct((r, D_MODEL), F32) for r in out_rows] + side_out_shape,
        scratch_shapes=[pltpu.VMEM((nf, D_MODEL, FFN_TF), BF16), pltpu.VMEM((nf, D_MODEL, FFN_TF), BF16),
                        pltpu.VMEM((nf, FFN_TF, D_MODEL), BF16),
                        pltpu.VMEM((FFN_TM, D_MODEL), BF16), pltpu.VMEM((FFN_TM, D_MODEL), F32)],
        compiler_params=_params(1),
        name="ffn",
    )(*xs, mods, g, w1, w3, w2, *side_args)


def _group_norm64(x, bd, g):
    ss = _dot(x * x, bd)
    return x * lax.rsqrt(ss * (1.0 / DA_HEAD_DIM) + EPS) * g


def _rope(x, cos, sin_signed):
    first = (lax.broadcasted_iota(jnp.int32, x.shape, 1) % 32) < 16
    partner = jnp.where(first, pltpu.roll(x, LANES - 16, 1), pltpu.roll(x, 16, 1))
    return x * cos + partner * sin_signed


N_SEG = 9
PROJ_WARM = (N_SEG + 1) // 2
GATE_LO = 4 * D_MODEL


def _seg_start(s):
    return s * D_MODEL + N_GATE_COLS * (s >= 4)


def _proj_kernel(*refs, rope, row_of_step):
    (x_ref, m_ref, g_ref, wt_ref, bg_ref, bd_ref, gq_ref, gk_ref), refs = refs[:8], refs[8:]
    if rope:
        (cos_ref, sin_ref), refs = refs[:2], refs[2:]
    (mq_ref, mk_ref, mv_ref, so_ref, dq_ref, dk_ref, dv_ref, sgm_ref, sgd_ref, gates_ref,
     w_sc, wg_sc, hh_sc) = refs
    step = pl.program_id(0)
    mod = _mod_reader(m_ref, row_of_step(step))
    norm = lambda: _modulated_norm(x_ref[...], g_ref[1:2, :], mod(3), mod(4)).astype(BF16)
    groups = MXU_DIM // DA_HEAD_DIM
    tiled = lambda ref, scale: jnp.concatenate([ref[...] * scale] * groups, axis=1)
    gq_row, gk_row = tiled(gq_ref, QK_LOG2_SCALE), tiled(gk_ref, 1.0)
    bias = jnp.concatenate([bg_ref[...], jnp.zeros((1, GATE_PAD - N_GATE_COLS), F32)], axis=1)

    def gates(hh):
        y = _dot_nt(hh, wg_sc[...]) + bias
        col = lax.broadcasted_iota(jnp.int32, y.shape, 1)
        gates_ref[...] = jnp.where((col // ML_HEADS) % 2 == 1, _log_sigmoid(y), y)

    def qk_segment(y, gain_row, o_ref):
        bd = bd_ref[...]
        for c in range(D_MODEL // MXU_DIM):
            cols = slice(c * MXU_DIM, (c + 1) * MXU_DIM)
            z = _group_norm64(y[:, cols], bd, gain_row)
            if rope:
                z = jnp.concatenate(
                    [_rope(z[:, k * LANES:(k + 1) * LANES], cos_ref[...], sin_ref[...])
                     for k in range(MXU_DIM // LANES)], axis=1)
            o_ref[:, cols] = z.astype(o_ref.dtype)

    def cache_key_segment(hh):
        y_t = _dot_nt(w_sc[5], hh)
        z = y_t.reshape(D_MODEL // DA_HEAD_DIM, DA_HEAD_DIM, y_t.shape[1])
        ms = jnp.mean(z * z, axis=1, keepdims=True)
        gain = jnp.broadcast_to(gk_row[:, 0:LANES], (LANES, LANES)).T[0:DA_HEAD_DIM, :]
        gain = jnp.concatenate([gain] * (y_t.shape[1] // LANES), axis=1)
        dk_ref[...] = (z * lax.rsqrt(ms + EPS) * gain).reshape(y_t.shape)

    def segment(s, hh):
        if s in (0, 2):
            o_ref = {0: mq_ref, 2: mv_ref}[s]
            y_t = _dot_nt(w_sc[s], hh) * ((ML_HEAD_DIM ** -0.5) if s == 0 else 1.0)
            for j in range(PROJ_TM // CHUNK):
                o_ref[j] = y_t[:, j * CHUNK:(j + 1) * CHUNK].astype(o_ref.dtype)
            return
        if s == 5 and not rope:
            cache_key_segment(hh)
            return
        y = _dot_nt(hh, w_sc[s])
        if s == 4:
            qk_segment(y, gq_row, dq_ref)
        elif s == 5:
            qk_segment(y, gk_row, dk_ref)
        elif s in (3, 7, 8):
            {3: so_ref, 7: sgm_ref, 8: sgd_ref}[s][...] = _sigmoid(y)
        elif s == 6 and not rope:
            dv_ref[...] = y.reshape(y.shape[0], DA_HEADS, DA_V_DIM)
        else:
            o_ref = {1: mk_ref, 6: dv_ref}[s]
            o_ref[...] = y.astype(o_ref.dtype)

    for w in range(PROJ_WARM):
        @pl.when(step == w)
        def _(w=w):
            if w == 0:
                hh_sc[...] = norm()
            for j in range(2):
                s = 2 * w + j
                if s < N_SEG:
                    w_sc[s] = wt_ref[j]
                    segment(s, hh_sc[...])
                else:
                    wg_sc[...] = jnp.zeros(wg_sc.shape, BF16)
                    wg_sc[0:N_GATE_COLS, :] = wt_ref[j, 0:N_GATE_COLS, :]
                    gates(hh_sc[...])

    @pl.when(step >= PROJ_WARM)
    def _():
        hh = norm()
        gates(hh)
        for s in range(N_SEG):
            segment(s, hh)


def _proj(x, row0, rows, mods, mod_index, g, w_seg, b_gate, bd, gq, gk, rope_tabs, seq_len):
    rope = rope_tabs is not None
    row_tile = lambda s: jnp.maximum(s - (PROJ_WARM - 1), 0)
    row = lambda s: (row_tile(s), 0)
    tile = pl.BlockSpec((PROJ_TM, D_MODEL), row)
    w_block = (2, D_MODEL, D_MODEL)
    in_specs = [pl.BlockSpec((PROJ_TM, D_MODEL), lambda s: (row0 // PROJ_TM + row_tile(s), 0)),
                _resident(mods.shape),
                _resident(g.shape),
                pl.BlockSpec(w_block, lambda s: (jnp.minimum(s, PROJ_WARM - 1), 0, 0)),
                _resident(b_gate.shape),
                _resident((MXU_DIM, MXU_DIM)),
                _resident(gq.shape),
                _resident(gk.shape)]
    args = [x, mods, g, w_seg, b_gate, bd, gq, gk]
    if rope:
        tiles_per_seq = seq_len // PROJ_TM
        tab = pl.BlockSpec((PROJ_TM, LANES), lambda s: (row_tile(s) % tiles_per_seq, 0))
        in_specs += [tab, tab]
        args += list(rope_tabs)
    dtypes = [BF16, BF16, BF16, F32, BF16, BF16, BF16 if rope else F32, F32, F32]
    out_shape = [jax.ShapeDtypeStruct((rows, D_MODEL), dt) for dt in dtypes]
    out_shape.append(jax.ShapeDtypeStruct((rows, GATE_PAD), F32))
    out_specs = [tile] * 9 + [pl.BlockSpec((PROJ_TM, GATE_PAD), row)]
    slabs = PROJ_TM // CHUNK
    for j in (0, 2):
        out_shape[j] = jax.ShapeDtypeStruct((rows // CHUNK, D_MODEL, CHUNK), BF16)
        out_specs[j] = pl.BlockSpec((slabs, D_MODEL, CHUNK), lambda s: (row_tile(s), 0, 0))
    if not rope:
        assert seq_len == PROJ_TM
        out_shape[5] = jax.ShapeDtypeStruct((rows // seq_len, D_MODEL, seq_len), F32)
        out_specs[5] = pl.BlockSpec((None, D_MODEL, seq_len), lambda s: (row_tile(s), 0, 0))
        out_shape[6] = jax.ShapeDtypeStruct((rows, DA_HEADS, DA_V_DIM), F32)
        out_specs[6] = pl.BlockSpec((PROJ_TM, DA_HEADS, DA_V_DIM), lambda s: (row_tile(s), 0, 0))
    return pl.pallas_call(
        functools.partial(_proj_kernel, rope=rope,
                          row_of_step=lambda step: mod_index(row0 + row_tile(step) * PROJ_TM)),
        grid=(PROJ_WARM - 1 + rows // PROJ_TM,),
        in_specs=in_specs,
        out_specs=out_specs,
        out_shape=out_shape,
        scratch_shapes=[pltpu.VMEM((N_SEG, D_MODEL, D_MODEL), BF16), pltpu.VMEM((GATE_PAD, D_MODEL), BF16),
                        pltpu.VMEM((PROJ_TM, D_MODEL), BF16)],
        compiler_params=_params(1),
        name="mixer_in_proj",
    )(*args)


def _lambda(lam_refs, lam_init):
    q1, k1, q2, k2 = (r[...] for r in lam_refs)
    s1 = jnp.sum(q1 * k1, axis=1, keepdims=True)
    s2 = jnp.sum(q2 * k2, axis=1, keepdims=True)
    return jnp.exp(s1) - jnp.exp(s2) + lam_init


def _attn_kernel(*refs, cached, seq, nb, lam_init):
    if cached:
        (q_ref, k_ref, v_ref, ck_ref, cv_ref), refs = refs[:5], refs[5:]
    else:
        (q_ref, k_ref, v_ref), refs = refs[:3], refs[3:]
    lam_refs, (gs_ref, o_ref, kall_sc, vt_sc, s_sc) = refs[:4], refs[4:]
    units = [(bb, h) for bb in range(nb) for h in range(DA_HEADS)]

    def cache_order_kv(kt_ref, vr_ref, u, h, lo, n):
        kall_sc[u, lo:lo + n, :] = kt_ref[h * DA_V_DIM:(h + 1) * DA_V_DIM, :].T.astype(BF16)
        vt_sc[u, :, lo:lo + n] = vr_ref[pl.ds(h, n, stride=DA_HEADS), :].T.astype(BF16)

    @pl.when(pl.program_id(1) == 0)
    def _():
        for u, (bb, h) in enumerate(units):
            if cached:
                cols = slice(h * DA_V_DIM, (h + 1) * DA_V_DIM)
                kall_sc[u, 0:seq, :] = k_ref[:, cols].astype(BF16)
                vt_sc[u, :, 0:seq] = v_ref[:, cols].astype(F32).T.astype(BF16)
                cache_order_kv(ck_ref.at[bb], cv_ref.at[bb], u, h, seq, ck_ref.shape[2])
            else:
                cache_order_kv(k_ref.at[bb], v_ref.at[bb], u, h, 0, seq)

    lam = _lambda(lam_refs, lam_init)
    sub_gain = gs_ref[...] * (1.0 - lam_init)
    lane = lax.broadcasted_iota(jnp.int32, (1, DA_V_DIM), 1)
    comp_masks = [lane < DA_HEAD_DIM, lane >= DA_HEAD_DIM]
    tq = q_ref.shape[0] // nb
    n_keys = kall_sc.shape[1]
    n_tiles = pl.cdiv(n_keys, ATT_TK_MAX)
    tk = n_keys // n_tiles

    def stacked_q(u):
        bb, h = units[u]
        q = q_ref[bb * tq:(bb + 1) * tq, h * DA_V_DIM:(h + 1) * DA_V_DIM].astype(BF16)
        return jnp.concatenate([jnp.where(m, q, jnp.zeros_like(q)) for m in comp_masks], axis=0)

    def score_tile(h, j, qq, m8):
        rows = slice(j * tk, (j + 1) * tk)
        st = _dot_nt(kall_sc[h, rows, :], qq)
        s_sc[h % 2, rows, :] = st
        t8 = jnp.max(st.reshape(tk // 8, 8, 2 * tq), axis=0)
        return t8 if m8 is None else jnp.maximum(m8, t8)

    def prob_tile(h, j, mx, d8, pv):
        rows = slice(j * tk, (j + 1) * tk)
        e = jnp.exp2(s_sc[h % 2, rows, :] - mx)
        s8 = jnp.sum(e.reshape(tk // 8, 8, 2 * tq), axis=0)
        p = _dot(vt_sc[h, :, rows], e)
        return (s8 if d8 is None else d8 + s8), (p if pv is None else pv + p)

    qq = stacked_q(0)
    m8 = None
    for j in range(n_tiles):
        m8 = score_tile(0, j, qq, m8)
    for h in range(len(units)):
        mx = jnp.max(m8, axis=0, keepdims=True)
        if h + 1 < len(units):
            qq = stacked_q(h + 1)
        m8, d8, pv = None, None, None
        for j in range(n_tiles):
            if h + 1 < len(units):
                m8 = score_tile(h + 1, j, qq, m8)
            d8, pv = prob_tile(h, j, mx, d8, pv)
        inv = 1.0 / jnp.sum(d8, axis=0, keepdims=True)
        out_t = pv[:, :tq] * inv[:, :tq] - pv[:, tq:] * (lam * inv[:, tq:])
        out_t = out_t * lax.rsqrt(jnp.mean(out_t * out_t, axis=0, keepdims=True) + EPS)
        bb, head = units[h]
        o_ref[bb * tq:(bb + 1) * tq, head * DA_V_DIM:(head + 1) * DA_V_DIM] = (out_t.T * sub_gain).astype(o_ref.dtype)


def _attn(q, k, v, cache, lam_vecs, g_sub, batch, seq_len, lam_init):
    rows = q.shape[0]
    nq = seq_len // ATT_TQ
    nb = ATT_SHORT_SEQ_BATCH if (cache is None and nq == 1) else 1
    assert batch % nb == 0
    q_spec = pl.BlockSpec((nb * ATT_TQ, D_MODEL), lambda b, i: (b * nq + i, 0))
    cache_specs = lambda n: [pl.BlockSpec((nb, D_MODEL, n), lambda b, i: (b, 0, 0)),
                             pl.BlockSpec((nb, n * DA_HEADS, DA_V_DIM), lambda b, i: (b, 0, 0))]
    n_keys = seq_len
    if cache is not None:
        past = cache[0].shape[2]
        n_keys += past
        kv_spec = pl.BlockSpec((seq_len, D_MODEL), lambda b, i: (b, 0))
        in_specs = [q_spec, kv_spec, kv_spec] + cache_specs(past)
        args = [q, k, v] + list(cache)
    else:
        in_specs = [q_spec] + cache_specs(seq_len)
        args = [q, k, v.reshape(batch, seq_len * DA_HEADS, DA_V_DIM)]
    in_specs += [_resident((1, DA_HEAD_DIM))] * 4 + [_resident((1, DA_V_DIM))]
    args += list(lam_vecs) + [g_sub]
    return pl.pallas_call(
        functools.partial(_attn_kernel, cached=cache is not None, seq=seq_len, nb=nb, lam_init=lam_init),
        grid=(batch // nb, nq),
        in_specs=in_specs,
        out_specs=q_spec,
        out_shape=jax.ShapeDtypeStruct((rows, D_MODEL), BF16),
        scratch_shapes=[pltpu.VMEM((nb * DA_HEADS, n_keys, DA_V_DIM), BF16),
                        pltpu.VMEM((nb * DA_HEADS, DA_V_DIM, n_keys), BF16),
                        pltpu.VMEM((2, n_keys, 2 * ATT_TQ), F32)],
        compiler_params=_params(2),
        name="diff_attention",
    )(*args)


def _per_chain(fn, a, b):
    return jnp.stack([fn(a[i], b[i]) for i in range(a.shape[0])])


ML_EXT = 8
ML_SHORT_SEQ_BATCH = 2


def _rows(x):
    return jnp.stack([x[i:i + 1, :] for i in range(x.shape[0])])


def _split3(x):
    hi = x.astype(BF16)
    r = x - hi.astype(F32)
    mid = r.astype(BF16)
    lo = (r - mid.astype(F32)).astype(BF16)
    return jnp.concatenate([hi, mid, lo], axis=1)


def _chunk_scan(x, reverse_rows, tri_prefix, tri_suffix):
    parts = _split3(x)
    return jnp.where(reverse_rows, _dot(parts, tri_suffix), _dot(parts, tri_prefix))


def _paired_value_matmul(v_t, sc):
    B, L, _ = sc.shape
    zero = jnp.zeros((L, L), sc.dtype)
    out = []
    for i in range(0, B, 2):
        lhs = jnp.concatenate([v_t[i], v_t[i + 1]], axis=1)
        rhs = jnp.concatenate([jnp.concatenate([sc[i], zero], axis=1),
                               jnp.concatenate([zero, sc[i + 1]], axis=1)], axis=0)
        both = _dot(lhs, rhs)
        out += [both[:, :L], both[:, L:]]
    return jnp.stack(out)


def _mlstm_step(k, q_t, v_t, ic, fc, CT, m, seen_t, reverse_rows):
    B, L = ic.shape
    d = k.shape[2]
    zero_state = CT is None
    if zero_state:
        m = jnp.zeros((B, 1, 1), F32)
    tri = lambda keep: jnp.concatenate([jnp.where(keep, 1.0, 0.0).astype(BF16)] * 3, axis=0)
    b2 = _chunk_scan(fc, reverse_rows, tri(seen_t[0]), tri(seen_t[B - 1]))
    u2 = ic - b2
    u_t = jnp.concatenate([u2, jnp.zeros((L - B, L), F32)], axis=0).T
    u_col = jnp.stack([jnp.broadcast_to(u_t[:, i:i + 1], (L, L)) for i in range(B)])
    b, i_g, f_g = _rows(b2), _rows(ic), _rows(fc)
    b_last = jnp.sum(f_g, axis=2, keepdims=True)

    log_d = jnp.where(seen_t, b + u_col, -jnp.inf)
    a = b + m
    m_t = jnp.maximum(a, jnp.max(log_d, axis=1, keepdims=True))
    dmat = jnp.exp(log_d - m_t)
    sc = _per_chain(_dot, k, q_t) * dmat
    value_matmul = _paired_value_matmul if 2 * L <= MXU_DIM else functools.partial(_per_chain, _dot)
    num = value_matmul(v_t, sc.astype(BF16))
    den = jnp.sum(sc, axis=1, keepdims=True)
    if not zero_state:
        inter = jnp.exp(a - m_t)
        cq = _per_chain(_dot, CT, q_t)
        num = num + inter * cq[:, :d, :]
        den = den + inter * cq[:, d:d + 1, :]
    h_t = num * (1.0 / jnp.maximum(jnp.abs(den), jnp.exp(-m_t)))

    g = b_last - b + i_g
    m_new = jnp.maximum(b_last + m, jnp.max(g, axis=2, keepdims=True))
    w = jnp.exp(g - m_new)
    decay = jnp.exp(b_last + m - m_new)
    vw = jnp.concatenate([v_t.astype(F32), jnp.ones((B, ML_EXT, L), F32)], axis=1) * w
    CT_new = _per_chain(_dot, vw, k)
    if not zero_state:
        CT_new = decay * CT + CT_new
    return h_t, CT_new, m_new


def _mlstm_kernel(*refs, seq_len, nb, has_state, emit_state):
    q_ref, k_ref, v_ref, so_ref, g_ref, gmh_ref = refs[:6]
    refs = refs[6:]
    if has_state:
        (c0_ref, n0_ref, m0_ref), refs = refs[:3], refs[3:]
    hm_ref, refs = refs[0], refs[1:]
    if emit_state:
        (c_out_ref, n_out_ref, m_out_ref), refs = refs[:3], refs[3:]
    ct_sc, m_sc, gr_sc, h_sc = refs

    nc = seq_len // CHUNK
    d_head = ML_HEAD_DIM
    chains = [(bb, d, h) for bb in range(nb) for d in range(2) for h in range(ML_HEADS)]
    n_chain = len(chains)
    for i, (bb, d, h) in enumerate(chains):
        if has_state:
            ct_sc[i, 0:d_head, :] = c0_ref[bb, d, h].T
            ct_sc[i, d_head:d_head + ML_EXT, :] = jnp.broadcast_to(n0_ref[bb, d, h:h + 1, :], (ML_EXT, d_head))
            m_sc[i] = jnp.full((1, 1), m0_ref[pl.program_id(0) * nb + bb, 0, d, h], F32)
        elif nc > 2:
            ct_sc[i] = jnp.zeros((d_head + ML_EXT, d_head), F32)
            m_sc[i] = jnp.zeros((1, 1), F32)
    for c in range(nb * nc):
        gr_sc[c] = g_ref[c * CHUNK:(c + 1) * CHUNK, :].T

    s_idx = lax.broadcasted_iota(jnp.int32, (CHUNK, CHUNK), 0)
    t_idx = lax.broadcasted_iota(jnp.int32, (CHUNK, CHUNK), 1)
    seen_t = jnp.stack([(s_idx >= t_idx) if d else (s_idx <= t_idx) for _, d, _ in chains])
    reverse_rows = (lax.broadcasted_iota(jnp.int32, (n_chain, CHUNK), 0) // ML_HEADS) % 2 == 1

    def step(c_fwd, c_bwd, rows_of, zero_state=False):
        chunk_of = (c_fwd, c_bwd)
        lo = 2 * ML_HEADS
        ic, fc = [], []
        for bb in range(nb):
            g_fwd, g_bwd = gr_sc[bb * nc + c_fwd], gr_sc[bb * nc + c_bwd]
            ic += [g_fwd[0:ML_HEADS], g_bwd[lo:lo + ML_HEADS]]
            fc += [g_fwd[ML_HEADS:lo], g_bwd[lo + ML_HEADS:2 * lo]]
        head = [slice(h * d_head, (h + 1) * d_head) for _, _, h in chains]
        stack = lambda pick: jnp.stack([pick(i, bb * nc + chunk_of[d]) for i, (bb, d, _) in enumerate(chains)])
        h_t, CT_new, m_new = _mlstm_step(
            stack(lambda i, c: k_ref[rows_of(c), head[i]]),
            stack(lambda i, c: q_ref[c, head[i], :]),
            stack(lambda i, c: v_ref[c, head[i], :]),
            jnp.concatenate(ic, axis=0), jnp.concatenate(fc, axis=0),
            None if zero_state else ct_sc[...], None if zero_state else m_sc[...], seen_t, reverse_rows)
        ct_sc[...] = CT_new
        m_sc[...] = m_new
        for i, (bb, d, _) in enumerate(chains):
            h_sc[d, bb * nc + chunk_of[d], head[i], :] = h_t[i]

    if nc <= 2:
        for c in range(nc):
            step(c, nc - 1 - c, lambda cc: slice(cc * CHUNK, (cc + 1) * CHUNK),
                 zero_state=(c == 0 and not has_state))
    else:
        def body(c, carry):
            step(c, nc - 1 - c, lambda cc: pl.ds(pl.multiple_of(cc * CHUNK, CHUNK), CHUNK))
            return carry
        lax.fori_loop(0, nc, body, 0, unroll=2)

    for h in range(ML_HEADS):
        hcols = slice(h * d_head, (h + 1) * d_head)
        gain = jnp.broadcast_to(gmh_ref[h:h + 1, :], (CHUNK, d_head)).T
        for c in range(nb * nc):
            rows = slice(c * CHUNK, (c + 1) * CHUNK)
            hsum = h_sc[0, c, hcols, :] + h_sc[1, c, hcols, :]
            hn = hsum * lax.rsqrt(jnp.mean(hsum * hsum, axis=0, keepdims=True) + EPS) * gain
            hm_ref[rows, hcols] = (hn.T * so_ref[rows, hcols]).astype(hm_ref.dtype)
    if emit_state:
        for i, (bb, d, h) in enumerate(chains):
            c_out_ref[bb, d, h] = ct_sc[i, 0:d_head, :].T
            n_out_ref[bb, d, h:h + 1, :] = ct_sc[i, d_head:d_head + 1, :]
            m_out_ref[bb, d:d + 1, h:h + 1] = m_sc[i]


def _mlstm(q_t, k, v_t, so, gates, g_mh, state, batch, seq_len, emit_state):
    rows = k.shape[0]
    d = ML_HEAD_DIM
    nc = seq_len // CHUNK
    nb = ML_SHORT_SEQ_BATCH if nc <= 2 else 1
    assert batch % nb == 0
    tile = pl.BlockSpec((nb * seq_len, D_MODEL), lambda b: (b, 0))
    tile_t = pl.BlockSpec((nb * nc, D_MODEL, CHUNK), lambda b: (b, 0, 0))
    in_specs = [tile_t, tile, tile_t, tile,
                pl.BlockSpec((nb * seq_len, GATE_PAD), lambda b: (b, 0)),
                _resident((ML_HEADS, d))]
    args = [q_t, k, v_t, so, gates, g_mh]
    c_spec = pl.BlockSpec((nb, None, 2, ML_HEADS, d, d), lambda b: (b, 0, 0, 0, 0, 0))
    n_spec = pl.BlockSpec((nb, None, 2, ML_HEADS, d), lambda b: (b, 0, 0, 0, 0))
    if state is not None:
        in_specs += [c_spec, n_spec, pl.BlockSpec(memory_space=pltpu.SMEM)]
        args += list(state)
    out_specs = [tile]
    out_shape = [jax.ShapeDtypeStruct((rows, D_MODEL), BF16)]
    if emit_state:
        out_specs += [c_spec, n_spec, pl.BlockSpec((nb, None, 2, ML_HEADS), lambda b: (b, 0, 0, 0))]
        out_shape += [jax.ShapeDtypeStruct((batch, 1, 2, ML_HEADS, d, d), F32),
                      jax.ShapeDtypeStruct((batch, 1, 2, ML_HEADS, d), F32),
                      jax.ShapeDtypeStruct((batch, 1, 2, ML_HEADS), F32)]
    n_state = 2 * ML_HEADS * nb
    return pl.pallas_call(
        functools.partial(_mlstm_kernel, seq_len=seq_len, nb=nb, has_state=state is not None,
                          emit_state=emit_state),
        grid=(batch // nb,),
        in_specs=in_specs,
        out_specs=out_specs,
        out_shape=out_shape,
        scratch_shapes=[pltpu.VMEM((n_state, d + ML_EXT, d), F32), pltpu.VMEM((n_state, 1, 1), F32),
                        pltpu.VMEM((nb * nc, GATE_PAD, CHUNK), F32),
                        pltpu.VMEM((2, nb * nc, D_MODEL, CHUNK), F32)],
        compiler_params=_params(1),
        name="mlstm",
    )(*args)


def _merge_kernel(*refs, tiles, mod_index):
    n = len(tiles)
    (x_ref, m_ref), refs = refs[:2], refs[2:]
    parts = [refs[j * n:(j + 1) * n] for j in range(4)]
    wm_ref, wd_ref, wo_ref, o_ref, w_sc = refs[4 * n:]
    i = pl.program_id(0)

    @pl.when(i == 0)
    def _():
        for j, w_ref in enumerate((wm_ref, wd_ref, wo_ref)):
            w_sc[j] = w_ref[...].astype(BF16)

    def tile(hm_ref, att_ref, sgm_ref, sgd_ref):
        y = sgm_ref[...] * _dot(hm_ref[...], w_sc[0]) + sgd_ref[...] * _dot(att_ref[...], w_sc[1])
        o_ref[...] = x_ref[...] + _mod_reader(m_ref, mod_index(i * MERGE_TM))(5) * _dot(y, w_sc[2])

    starts = np.cumsum((0,) + tiles).tolist()
    for k in range(n):
        pl.when((i >= starts[k]) & (i < starts[k + 1]))(functools.partial(tile, *[p[k] for p in parts]))


def _merge(x, mods, mod_index, hm, att, sgm, sgd, wm, wd, wo):
    rows = x.shape[0]
    tiles = tuple(h.shape[0] // MERGE_TM for h in hm)
    assert sum(tiles) * MERGE_TM == rows
    tile = pl.BlockSpec((MERGE_TM, D_MODEL), lambda i: (i, 0))

    def part_spec(k):
        start = sum(tiles[:k])
        return pl.BlockSpec((MERGE_TM, D_MODEL), lambda i: (jnp.clip(i - start, 0, tiles[k] - 1), 0))

    part_specs = [part_spec(k) for k in range(len(tiles))]
    w_spec = _resident((D_MODEL, D_MODEL))
    return pl.pallas_call(
        functools.partial(_merge_kernel, tiles=tiles, mod_index=mod_index),
        grid=(rows // MERGE_TM,),
        in_specs=[tile, _resident(mods.shape)]
                 + part_specs * 4 + [w_spec, w_spec, w_spec],
        out_specs=tile,
        out_shape=jax.ShapeDtypeStruct((rows, D_MODEL), F32),
        scratch_shapes=[pltpu.VMEM((3, D_MODEL, D_MODEL), BF16)],
        compiler_params=_params(1),
        name="branch_merge",
    )(x, mods, *hm, *att, *sgm, *sgd, wm, wd, wo)


def _rope_tables(seq_len):
    lane = np.arange(LANES)
    r = lane % 32
    freqs = np.power(np.float32(ROPE_BASE), -(r % 16).astype(np.float32) / np.float32(16.0))
    tok = np.arange(seq_len)
    pos = np.where((lane % 64 < 32)[None, :], (tok // GRID_W)[:, None], (tok % GRID_W)[:, None]).astype(np.float32)
    ang = pos * freqs[None, :]
    sign = np.where(r < 16, -1.0, 1.0).astype(np.float32)
    return jnp.asarray(np.cos(ang), F32), jnp.asarray(np.sin(ang) * sign[None, :], F32)


def _mixer_branches(x, row0, mods, mod_index, w, batch, seq_len, ctx, lam_init):
    rope_tabs = None if ctx is None else _rope_tables(seq_len)
    mq, mk, mv, so, dq, dk, dv, sgm, sgd, gates = _proj(
        x, row0, batch * seq_len, mods, mod_index, w["g_norm"], w["w_seg"], w["b_gate"], w["bd"], w["g_qn"],
        w["g_kn"], rope_tabs, seq_len)
    cache = None if ctx is None else (ctx[0], ctx[1])
    att = _attn(dq, dk, dv, cache, w["lam"], w["g_sub"], batch, seq_len, lam_init)
    state = None if ctx is None else ctx[2]
    res = _mlstm(mq, mk, mv, so, gates, w["g_mh"], state, batch, seq_len, emit_state=ctx is None)
    return res[0], att, sgm, sgd, dk, dv, res[1:]


def kernel(x_prompt, x_sample, c, cache_k, cache_v, state_C, state_n, state_m, c_ctx, w_ada, b_ada, g_norm, ffn1_w1, ffn1_w3, ffn1_w2, ffn2_w1, ffn2_w3, ffn2_w2, w_in, b_gate, g_qn, g_kn, lam_q1, lam_k1, lam_q2, lam_k2, g_sub, g_mh, w_br_m, w_br_d, w_out):
    depth = w_ada.shape[0]
    assert depth == 1
    l = 0
    bp, tp, _ = x_prompt.shape
    bs, ts, _ = x_sample.shape
    past = cache_k.shape[2]
    lam_init = 0.8 - 0.6 * math.exp(-0.3 * l)

    cvecs = jnp.concatenate([c_ctx[None, :], c, jnp.zeros((MOD_ROWS - 1 - bs, D_MODEL), F32)], axis=0)
    mods = _mods(cvecs, w_ada[l], b_ada[l])

    group = np.arange(MXU_DIM) // DA_HEAD_DIM
    w = dict(
        g_norm=g_norm[l],
        ffn1_w1=ffn1_w1[l], ffn1_w3=ffn1_w3[l], ffn1_w2=ffn1_w2[l],
        ffn2_w1=ffn2_w1[l], ffn2_w3=ffn2_w3[l], ffn2_w2=ffn2_w2[l],
        w_in_t=w_in[l].T,
        b_gate=b_gate[l:l + 1],
        bd=jnp.asarray(group[:, None] == group[None, :], BF16),
        g_qn=g_qn[l:l + 1], g_kn=g_kn[l:l + 1],
        lam=(lam_q1[l:l + 1], lam_k1[l:l + 1], lam_q2[l:l + 1], lam_k2[l:l + 1]),
        g_sub=g_sub[l:l + 1], g_mh=g_mh[l],
        w_br_m=w_br_m[l], w_br_d=w_br_d[l], w_out=w_out[l],
    )

    n_ctx, n_lat = bp * tp, bs * ts
    mod_index = lambda r: jnp.where(r < n_ctx, 0, 1 + (r - n_ctx) // ts)
    seg_start = lambda j: pl.multiple_of(jnp.where(j < N_SEG, _seg_start(j), GATE_LO), N_GATE_COLS)
    x1, w["w_seg"] = _ffn((x_prompt.reshape(n_ctx, D_MODEL), x_sample.reshape(n_lat, D_MODEL)), (n_ctx + n_lat,),
                          mods, mod_index, w["g_norm"], w["ffn1_w1"], w["ffn1_w3"], w["ffn1_w2"], base=0,
                          side=(w["w_in_t"], N_SEG + 1, seg_start))

    hm_p, att_p, sgm_p, sgd_p, new_k_t, new_v, (new_c, new_n, new_m) = _mixer_branches(
        x1, 0, mods, mod_index, w, bp, tp, None, lam_init)

    ctx = (cache_k[:, l].transpose(0, 2, 3, 4, 1).reshape(bs, D_MODEL, past),
           cache_v[:, l].reshape(bs, past * DA_HEADS, DA_V_DIM),
           (state_C, state_n, state_m))
    hm_s, att_s, sgm_s, sgd_s, _, _, _ = _mixer_branches(x1, n_ctx, mods, mod_index, w, bs, ts, ctx, lam_init)

    x2 = _merge(x1, mods, mod_index, (hm_p, hm_s), (att_p, att_s), (sgm_p, sgm_s), (sgd_p, sgd_s),
                w["w_br_m"], w["w_br_d"], w["w_out"])
    xp, xs = _ffn((x2,), (n_ctx, n_lat), mods, mod_index, w["g_norm"], w["ffn2_w1"], w["ffn2_w3"], w["ffn2_w2"],
                  base=6)

    return (xp.reshape(bp, tp, D_MODEL), xs.reshape(bs, ts, D_MODEL),
            new_k_t.reshape(bp, DA_HEADS, 2, DA_HEAD_DIM, tp).transpose(0, 4, 1, 2, 3)[:, None],
            new_v.reshape(bp, 1, tp, DA_HEADS, DA_V_DIM),
            new_c, new_n, new_m)
```

```python
import functools
import math

import jax
import jax.numpy as jnp
import numpy as np
from jax import lax
from jax.experimental import pallas as pl
from jax.experimental.pallas import tpu as pltpu

F32 = jnp.float32
BF16 = jnp.bfloat16

D_MODEL = 1024
D_FF = 2816
N_MOD = 9
GRID_W = 64
ML_HEADS = 4
ML_HEAD_DIM = 256
DA_HEADS = 8
DA_HEAD_DIM = 64
DA_V_DIM = 128
N_GATE_COLS = 16
CHUNK = 256
ROPE_BASE = 10000.0
QK_LOG2_SCALE = DA_HEAD_DIM ** -0.5 * math.log2(math.e)
EPS = 1e-6

LANES = 128
SUBLANES = 8
GATE_PAD = LANES
MXU_DIM = 256
VMEM_LIMIT = 56 * 1024 * 1024

MOD_ROWS = SUBLANES
FFN_TM = 512
FFN_TF = 256
PROJ_TM = 256
MERGE_TM = 512
ATT_TQ = 256
ATT_TK_MAX = 768
ATT_SHORT_SEQ_BATCH = 2


def _params(n_axes):
    return pltpu.CompilerParams(dimension_semantics=("arbitrary",) * n_axes,
                                vmem_limit_bytes=VMEM_LIMIT)


def _dot(a, b):
    return jnp.dot(a.astype(BF16), b.astype(BF16), preferred_element_type=F32)


def _dot_nt(a, b):
    return lax.dot_general(a.astype(BF16), b.astype(BF16), (((1,), (1,)), ((), ())),
                           preferred_element_type=F32)


def _sigmoid(x):
    return 1.0 / (1.0 + jnp.exp(-x))


def _log_sigmoid(x):
    return jnp.minimum(x, 0.0) - jnp.log1p(jnp.exp(-jnp.abs(x)))


def _modulated_norm(x, g, shift, scale):
    y = x * lax.rsqrt(jnp.mean(x * x, axis=-1, keepdims=True) + EPS) * g
    return y * (1.0 + scale) + shift


def _resident(shape):
    return pl.BlockSpec(shape, lambda *_: (0,) * len(shape), pipeline_mode=pl.Buffered(1))


def _mods_kernel(c_ref, w_ref, b_ref, o_ref):
    c = c_ref[...]
    o_ref[...] = _dot(c * _sigmoid(c), w_ref[...]) + b_ref[...]


def _mods(cvecs, w_ada, b_ada):
    n = N_MOD * D_MODEL
    tn = D_MODEL
    return pl.pallas_call(
        _mods_kernel,
        grid=(n // tn,),
        in_specs=[pl.BlockSpec((MOD_ROWS, D_MODEL), lambda j: (0, 0)),
                  pl.BlockSpec((D_MODEL, tn), lambda j: (0, j)),
                  pl.BlockSpec((1, tn), lambda j: (0, j))],
        out_specs=pl.BlockSpec((MOD_ROWS, tn), lambda j: (0, j)),
        out_shape=jax.ShapeDtypeStruct((MOD_ROWS, n), F32),
        compiler_params=_params(1),
        name="adaln_mods",
    )(cvecs, w_ada, b_ada.reshape(1, n))


def _mod_reader(m_ref, row):
    return lambda j: m_ref[pl.ds(row, 1), j * D_MODEL:(j + 1) * D_MODEL]


def _ffn_kernel(*refs, base, mod_index, in_tiles, out_tiles, side_blocks):
    n_in, n_out = len(in_tiles), len(out_tiles)
    x_refs, refs = refs[:n_in], refs[n_in:]
    (m_ref, g_ref, w1_ref, w3_ref, w2_ref), refs = refs[:5], refs[5:]
    if side_blocks:
        side_in_ref, refs = refs[0], refs[1:]
        side_out_ref, refs = refs[n_out], refs[:n_out] + refs[n_out + 1:]
    o_refs, (w1_sc, w3_sc, w2_sc, hh_sc, acc_sc) = refs[:n_out], refs[n_out:]
    g = pl.program_id(0)
    nf = D_FF // FFN_TF
    row_tile = g - (nf - 1)
    mod = _mod_reader(m_ref, mod_index(jnp.maximum(row_tile, 0) * FFN_TM))
    gain = g_ref[base // 3:base // 3 + 1, :]
    norm = lambda x: _modulated_norm(x, gain, mod(base), mod(base + 1)).astype(BF16)
    finish = lambda x, acc: x + 0.5 * mod(base + 2) * acc

    def tile(hh, f):
        a = _dot(hh, w1_sc[f])
        b = _dot(hh, w3_sc[f])
        return _dot(a * _sigmoid(a) * b, w2_sc[f])

    @pl.when(g < nf)
    def _():
        w1_sc[g] = w1_ref[...].astype(BF16)
        w3_sc[g] = w3_ref[...].astype(BF16)
        w2_sc[g] = w2_ref[...].astype(BF16)

        @pl.when(g == 0)
        def _():
            hh_sc[...] = norm(x_refs[0][...])
            acc_sc[...] = jnp.zeros(acc_sc.shape, F32)

        acc_sc[...] += tile(hh_sc[...], g)

        @pl.when(g == nf - 1)
        def _():
            o_refs[0][...] = finish(x_refs[0][...], acc_sc[...])

    if side_blocks:
        @pl.when((g >= nf) & (g - nf < side_blocks))
        def _():
            side_out_ref[...] = side_in_ref[...].astype(BF16)

    def full_tile(x_ref, o_ref):
        x = x_ref[...]
        hh = norm(x)
        acc = jnp.zeros(x.shape, F32)
        for f in range(nf):
            acc = acc + tile(hh, f)
        o_ref[...] = finish(x, acc)

    bounds = sorted(set(np.cumsum((0,) + in_tiles).tolist()) | set(np.cumsum((0,) + out_tiles).tolist()))
    for lo, hi in zip(bounds[:-1], bounds[1:]):
        k_in = int(np.searchsorted(np.cumsum(in_tiles), lo, side="right"))
        k_out = int(np.searchsorted(np.cumsum(out_tiles), lo, side="right"))
        pl.when((g >= nf) & (row_tile >= lo) & (row_tile < hi))(
            functools.partial(full_tile, x_refs[k_in], o_refs[k_out]))


def _ffn(xs, out_rows, mods, mod_index, g, w1, w3, w2, base, side=None):
    nf = D_FF // FFN_TF
    in_tiles = tuple(x.shape[0] // FFN_TM for x in xs)
    out_tiles = tuple(r // FFN_TM for r in out_rows)
    assert sum(in_tiles) == sum(out_tiles)
    side_in_specs, side_out_specs, side_out_shape, side_args, side_blocks = [], [], [], [], 0
    if side is not None:
        side_w, side_blocks, side_start = side
        assert side_blocks <= sum(in_tiles) - 1
        blk = lambda s: jnp.clip(s - nf, 0, side_blocks - 1)
        side_in_specs = [pl.BlockSpec((pl.Element(D_MODEL), pl.Element(D_MODEL)), lambda s: (side_start(blk(s)), 0))]
        side_out_specs = [pl.BlockSpec((None, D_MODEL, D_MODEL), lambda s: (blk(s), 0, 0))]
        side_out_shape = [jax.ShapeDtypeStruct((side_blocks, D_MODEL, D_MODEL), BF16)]
        side_args = [side_w]
    row_tile = lambda s: jnp.maximum(s - (nf - 1), 0)
    f_tile = lambda s: jnp.minimum(s, nf - 1)

    def part_spec(tiles, k):
        start = sum(tiles[:k])
        return pl.BlockSpec((FFN_TM, D_MODEL), lambda s: (jnp.clip(row_tile(s) - start, 0, tiles[k] - 1), 0))

    return pl.pallas_call(
        functools.partial(_ffn_kernel, base=base, mod_index=mod_index, in_tiles=in_tiles, out_tiles=out_tiles,
                          side_blocks=side_blocks),
        grid=(nf - 1 + sum(in_tiles),),
        in_specs=[part_spec(in_tiles, k) for k in range(len(xs))] + [
            _resident(mods.shape),
            _resident(g.shape),
            pl.BlockSpec((D_MODEL, FFN_TF), lambda s: (0, f_tile(s))),
            pl.BlockSpec((D_MODEL, FFN_TF), lambda s: (0, f_tile(s))),
            pl.BlockSpec((FFN_TF, D_MODEL), lambda s: (f_tile(s), 0))] + side_in_specs,
        out_specs=[part_spec(out_tiles, k) for k in range(len(out_rows))] + side_out_specs,
        out_shape=[jax.ShapeDtypeStruct((r, D_MODEL), F32) for r in out_rows] + side_out_shape,
        scratch_shapes=[pltpu.VMEM((nf, D_MODEL, FFN_TF), BF16), pltpu.VMEM((nf, D_MODEL, FFN_TF), BF16),
                        pltpu.VMEM((nf, FFN_TF, D_MODEL), BF16),
                        pltpu.VMEM((FFN_TM, D_MODEL), BF16), pltpu.VMEM((FFN_TM, D_MODEL), F32)],
        compiler_params=_params(1),
        name="ffn",
    )(*xs, mods, g, w1, w3, w2, *side_args)


def _group_norm64(x, bd, g):
    ss = _dot(x * x, bd)
    return x * lax.rsqrt(ss * (1.0 / DA_HEAD_DIM) + EPS) * g


ROPE_AXIS_DIM = DA_HEAD_DIM // 2
ROPE_HALF = ROPE_AXIS_DIM // 2


def _rope(x, cos, sin_signed):
    first = (lax.broadcasted_iota(jnp.int32, x.shape, 1) % ROPE_AXIS_DIM) < ROPE_HALF
    partner = jnp.where(first, pltpu.roll(x, LANES - ROPE_HALF, 1), pltpu.roll(x, ROPE_HALF, 1))
    return x * cos + partner * sin_signed


N_SEG = 9
PROJ_WARM = (N_SEG + 1) // 2
GATE_LO = 4 * D_MODEL


def _seg_start(s):
    return s * D_MODEL + N_GATE_COLS * (s >= 4)


def _proj_kernel(*refs, rope, row_of_step):
    (x_ref, m_ref, g_ref, wt_ref, bg_ref, bd_ref, gq_ref, gk_ref), refs = refs[:8], refs[8:]
    if rope:
        (cos_ref, sin_ref), refs = refs[:2], refs[2:]
    (mq_ref, mk_ref, mv_ref, so_ref, dq_ref, dk_ref, dv_ref, sgm_ref, sgd_ref, gates_ref,
     w_sc, wg_sc, hh_sc) = refs
    step = pl.program_id(0)
    mod = _mod_reader(m_ref, row_of_step(step))
    norm = lambda: _modulated_norm(x_ref[...], g_ref[1:2, :], mod(3), mod(4)).astype(BF16)
    groups = MXU_DIM // DA_HEAD_DIM
    tiled = lambda ref, scale: jnp.concatenate([ref[...] * scale] * groups, axis=1)
    gq_row, gk_row = tiled(gq_ref, QK_LOG2_SCALE), tiled(gk_ref, 1.0)
    bias = jnp.concatenate([bg_ref[...], jnp.zeros((1, GATE_PAD - N_GATE_COLS), F32)], axis=1)

    def gates(hh):
        y = _dot_nt(hh, wg_sc[...]) + bias
        col = lax.broadcasted_iota(jnp.int32, y.shape, 1)
        gates_ref[...] = jnp.where((col // ML_HEADS) % 2 == 1, _log_sigmoid(y), y)

    def qk_segment(y, gain_row, o_ref):
        bd = bd_ref[...]
        for c in range(D_MODEL // MXU_DIM):
            cols = slice(c * MXU_DIM, (c + 1) * MXU_DIM)
            z = _group_norm64(y[:, cols], bd, gain_row)
            if rope:
                z = jnp.concatenate(
                    [_rope(z[:, k * LANES:(k + 1) * LANES], cos_ref[...], sin_ref[...])
                     for k in range(MXU_DIM // LANES)], axis=1)
            o_ref[:, cols] = z.astype(o_ref.dtype)

    def cache_key_segment(hh):
        y_t = _dot_nt(w_sc[5], hh)
        z = y_t.reshape(D_MODEL // DA_HEAD_DIM, DA_HEAD_DIM, y_t.shape[1])
        ms = jnp.mean(z * z, axis=1, keepdims=True)
        gain = jnp.broadcast_to(gk_row[:, 0:LANES], (LANES, LANES)).T[0:DA_HEAD_DIM, :]
        gain = jnp.concatenate([gain] * (y_t.shape[1] // LANES), axis=1)
        dk_ref[...] = (z * lax.rsqrt(ms + EPS) * gain).reshape(y_t.shape)

    def segment(s, hh):
        if s in (0, 2):
            o_ref = {0: mq_ref, 2: mv_ref}[s]
            y_t = _dot_nt(w_sc[s], hh) * ((ML_HEAD_DIM ** -0.5) if s == 0 else 1.0)
            for j in range(PROJ_TM // CHUNK):
                o_ref[j] = y_t[:, j * CHUNK:(j + 1) * CHUNK].astype(o_ref.dtype)
            return
        if s == 5 and not rope:
            cache_key_segment(hh)
            return
        y = _dot_nt(hh, w_sc[s])
        if s == 4:
            qk_segment(y, gq_row, dq_ref)
        elif s == 5:
            qk_segment(y, gk_row, dk_ref)
        elif s in (3, 7, 8):
            {3: so_ref, 7: sgm_ref, 8: sgd_ref}[s][...] = _sigmoid(y)
        elif s == 6 and not rope:
            dv_ref[...] = y.reshape(y.shape[0], DA_HEADS, DA_V_DIM)
        else:
            o_ref = {1: mk_ref, 6: dv_ref}[s]
            o_ref[...] = y.astype(o_ref.dtype)

    for w in range(PROJ_WARM):
        @pl.when(step == w)
        def _(w=w):
            if w == 0:
                hh_sc[...] = norm()
            for j in range(2):
                s = 2 * w + j
                if s < N_SEG:
                    w_sc[s] = wt_ref[j]
                    segment(s, hh_sc[...])
                else:
                    wg_sc[...] = jnp.zeros(wg_sc.shape, BF16)
                    wg_sc[0:N_GATE_COLS, :] = wt_ref[j, 0:N_GATE_COLS, :]
                    gates(hh_sc[...])

    @pl.when(step >= PROJ_WARM)
    def _():
        hh = norm()
        gates(hh)
        for s in range(N_SEG):
            segment(s, hh)


def _proj(x, row0, rows, mods, mod_index, g, w_seg, b_gate, bd, gq, gk, rope_tabs, seq_len):
    rope = rope_tabs is not None
    row_tile = lambda s: jnp.maximum(s - (PROJ_WARM - 1), 0)
    row = lambda s: (row_tile(s), 0)
    tile = pl.BlockSpec((PROJ_TM, D_MODEL), row)
    w_block = (2, D_MODEL, D_MODEL)
    in_specs = [pl.BlockSpec((PROJ_TM, D_MODEL), lambda s: (row0 // PROJ_TM + row_tile(s), 0)),
                _resident(mods.shape),
                _resident(g.shape),
                pl.BlockSpec(w_block, lambda s: (jnp.minimum(s, PROJ_WARM - 1), 0, 0)),
                _resident(b_gate.shape),
                _resident((MXU_DIM, MXU_DIM)),
                _resident(gq.shape),
                _resident(gk.shape)]
    args = [x, mods, g, w_seg, b_gate, bd, gq, gk]
    if rope:
        tiles_per_seq = seq_len // PROJ_TM
        tab = pl.BlockSpec((PROJ_TM, LANES), lambda s: (row_tile(s) % tiles_per_seq, 0))
        in_specs += [tab, tab]
        args += list(rope_tabs)
    dtypes = [BF16, BF16, BF16, F32, BF16, BF16, BF16 if rope else F32, F32, F32]
    out_shape = [jax.ShapeDtypeStruct((rows, D_MODEL), dt) for dt in dtypes]
    out_shape.append(jax.ShapeDtypeStruct((rows, GATE_PAD), F32))
    out_specs = [tile] * 9 + [pl.BlockSpec((PROJ_TM, GATE_PAD), row)]
    slabs = PROJ_TM // CHUNK
    for j in (0, 2):
        out_shape[j] = jax.ShapeDtypeStruct((rows // CHUNK, D_MODEL, CHUNK), BF16)
        out_specs[j] = pl.BlockSpec((slabs, D_MODEL, CHUNK), lambda s: (row_tile(s), 0, 0))
    if not rope:
        assert seq_len == PROJ_TM
        out_shape[5] = jax.ShapeDtypeStruct((rows // seq_len, D_MODEL, seq_len), F32)
        out_specs[5] = pl.BlockSpec((None, D_MODEL, seq_len), lambda s: (row_tile(s), 0, 0))
        out_shape[6] = jax.ShapeDtypeStruct((rows, DA_HEADS, DA_V_DIM), F32)
        out_specs[6] = pl.BlockSpec((PROJ_TM, DA_HEADS, DA_V_DIM), lambda s: (row_tile(s), 0, 0))
    return pl.pallas_call(
        functools.partial(_proj_kernel, rope=rope,
                          row_of_step=lambda step: mod_index(row0 + row_tile(step) * PROJ_TM)),
        grid=(PROJ_WARM - 1 + rows // PROJ_TM,),
        in_specs=in_specs,
        out_specs=out_specs,
        out_shape=out_shape,
        scratch_shapes=[pltpu.VMEM((N_SEG, D_MODEL, D_MODEL), BF16), pltpu.VMEM((GATE_PAD, D_MODEL), BF16),
                        pltpu.VMEM((PROJ_TM, D_MODEL), BF16)],
        compiler_params=_params(1),
        name="mixer_in_proj",
    )(*args)


def _lambda(lam_refs, lam_init):
    q1, k1, q2, k2 = (r[...] for r in lam_refs)
    s1 = jnp.sum(q1 * k1, axis=1, keepdims=True)
    s2 = jnp.sum(q2 * k2, axis=1, keepdims=True)
    return jnp.exp(s1) - jnp.exp(s2) + lam_init


def _attn_kernel(*refs, cached, seq, nb, lam_init):
    if cached:
        (q_ref, k_ref, v_ref, ck_ref, cv_ref), refs = refs[:5], refs[5:]
    else:
        (q_ref, k_ref, v_ref), refs = refs[:3], refs[3:]
    lam_refs, (gs_ref, o_ref, kall_sc, vt_sc, s_sc) = refs[:4], refs[4:]
    units = [(bb, h) for bb in range(nb) for h in range(DA_HEADS)]

    def cache_order_kv(kt_ref, vr_ref, u, h, lo, n):
        kall_sc[u, lo:lo + n, :] = kt_ref[h * DA_V_DIM:(h + 1) * DA_V_DIM, :].T.astype(BF16)
        vt_sc[u, :, lo:lo + n] = vr_ref[pl.ds(h, n, stride=DA_HEADS), :].T.astype(BF16)

    @pl.when(pl.program_id(1) == 0)
    def _():
        for u, (bb, h) in enumerate(units):
            if cached:
                cols = slice(h * DA_V_DIM, (h + 1) * DA_V_DIM)
                kall_sc[u, 0:seq, :] = k_ref[:, cols].astype(BF16)
                vt_sc[u, :, 0:seq] = v_ref[:, cols].astype(F32).T.astype(BF16)
                cache_order_kv(ck_ref.at[bb], cv_ref.at[bb], u, h, seq, ck_ref.shape[2])
            else:
                cache_order_kv(k_ref.at[bb], v_ref.at[bb], u, h, 0, seq)

    lam = _lambda(lam_refs, lam_init)
    sub_gain = gs_ref[...] * (1.0 - lam_init)
    lane = lax.broadcasted_iota(jnp.int32, (1, DA_V_DIM), 1)
    comp_masks = [lane < DA_HEAD_DIM, lane >= DA_HEAD_DIM]
    tq = q_ref.shape[0] // nb
    n_keys = kall_sc.shape[1]
    n_tiles = pl.cdiv(n_keys, ATT_TK_MAX)
    tk = n_keys // n_tiles

    def stacked_q(u):
        bb, h = units[u]
        q = q_ref[bb * tq:(bb + 1) * tq, h * DA_V_DIM:(h + 1) * DA_V_DIM].astype(BF16)
        return jnp.concatenate([jnp.where(m, q, jnp.zeros_like(q)) for m in comp_masks], axis=0)

    def score_tile(h, j, qq, m8):
        rows = slice(j * tk, (j + 1) * tk)
        st = _dot_nt(kall_sc[h, rows, :], qq)
        s_sc[h % 2, rows, :] = st
        t8 = jnp.max(st.reshape(tk // SUBLANES, SUBLANES, 2 * tq), axis=0)
        return t8 if m8 is None else jnp.maximum(m8, t8)

    def prob_tile(h, j, mx, d8, pv):
        rows = slice(j * tk, (j + 1) * tk)
        e = jnp.exp2(s_sc[h % 2, rows, :] - mx)
        s8 = jnp.sum(e.reshape(tk // SUBLANES, SUBLANES, 2 * tq), axis=0)
        p = _dot(vt_sc[h, :, rows], e)
        return (s8 if d8 is None else d8 + s8), (p if pv is None else pv + p)

    qq = stacked_q(0)
    m8 = None
    for j in range(n_tiles):
        m8 = score_tile(0, j, qq, m8)
    for h in range(len(units)):
        mx = jnp.max(m8, axis=0, keepdims=True)
        if h + 1 < len(units):
            qq = stacked_q(h + 1)
        m8, d8, pv = None, None, None
        for j in range(n_tiles):
            if h + 1 < len(units):
                m8 = score_tile(h + 1, j, qq, m8)
            d8, pv = prob_tile(h, j, mx, d8, pv)
        inv = 1.0 / jnp.sum(d8, axis=0, keepdims=True)
        out_t = pv[:, :tq] * inv[:, :tq] - pv[:, tq:] * (lam * inv[:, tq:])
        out_t = out_t * lax.rsqrt(jnp.mean(out_t * out_t, axis=0, keepdims=True) + EPS)
        bb, head = units[h]
        o_ref[bb * tq:(bb + 1) * tq, head * DA_V_DIM:(head + 1) * DA_V_DIM] = (out_t.T * sub_gain).astype(o_ref.dtype)


def _attn(q, k, v, cache, lam_vecs, g_sub, batch, seq_len, lam_init):
    rows = q.shape[0]
    nq = seq_len // ATT_TQ
    nb = ATT_SHORT_SEQ_BATCH if (cache is None and nq == 1) else 1
    assert batch % nb == 0
    q_spec = pl.BlockSpec((nb * ATT_TQ, D_MODEL), lambda b, i: (b * nq + i, 0))
    cache_specs = lambda n: [pl.BlockSpec((nb, D_MODEL, n), lambda b, i: (b, 0, 0)),
                             pl.BlockSpec((nb, n * DA_HEADS, DA_V_DIM), lambda b, i: (b, 0, 0))]
    n_keys = seq_len
    if cache is not None:
        past = cache[0].shape[2]
        n_keys += past
        kv_spec = pl.BlockSpec((seq_len, D_MODEL), lambda b, i: (b, 0))
        in_specs = [q_spec, kv_spec, kv_spec] + cache_specs(past)
        args = [q, k, v] + list(cache)
    else:
        in_specs = [q_spec] + cache_specs(seq_len)
        args = [q, k, v.reshape(batch, seq_len * DA_HEADS, DA_V_DIM)]
    in_specs += [_resident((1, DA_HEAD_DIM))] * 4 + [_resident((1, DA_V_DIM))]
    args += list(lam_vecs) + [g_sub]
    return pl.pallas_call(
        functools.partial(_attn_kernel, cached=cache is not None, seq=seq_len, nb=nb, lam_init=lam_init),
        grid=(batch // nb, nq),
        in_specs=in_specs,
        out_specs=q_spec,
        out_shape=jax.ShapeDtypeStruct((rows, D_MODEL), BF16),
        scratch_shapes=[pltpu.VMEM((nb * DA_HEADS, n_keys, DA_V_DIM), BF16),
                        pltpu.VMEM((nb * DA_HEADS, DA_V_DIM, n_keys), BF16),
                        pltpu.VMEM((2, n_keys, 2 * ATT_TQ), F32)],
        compiler_params=_params(2),
        name="diff_attention",
    )(*args)


def _per_chain(fn, a, b):
    return jnp.stack([fn(a[i], b[i]) for i in range(a.shape[0])])


ML_EXT = SUBLANES
ML_SHORT_SEQ_BATCH = 2


def _rows(x):
    return jnp.stack([x[i:i + 1, :] for i in range(x.shape[0])])


def _split3(x):
    hi = x.astype(BF16)
    r = x - hi.astype(F32)
    mid = r.astype(BF16)
    lo = (r - mid.astype(F32)).astype(BF16)
    return jnp.concatenate([hi, mid, lo], axis=1)


def _chunk_scan(x, reverse_rows, tri_prefix, tri_suffix):
    parts = _split3(x)
    return jnp.where(reverse_rows, _dot(parts, tri_suffix), _dot(parts, tri_prefix))


def _paired_value_matmul(v_t, sc):
    B, L, _ = sc.shape
    zero = jnp.zeros((L, L), sc.dtype)
    out = []
    for i in range(0, B, 2):
        lhs = jnp.concatenate([v_t[i], v_t[i + 1]], axis=1)
        rhs = jnp.concatenate([jnp.concatenate([sc[i], zero], axis=1),
                               jnp.concatenate([zero, sc[i + 1]], axis=1)], axis=0)
        both = _dot(lhs, rhs)
        out += [both[:, :L], both[:, L:]]
    return jnp.stack(out)


def _mlstm_step(k, q_t, v_t, ic, fc, CT, m, seen_t, reverse_rows):
    B, L = ic.shape
    d = k.shape[2]
    zero_state = CT is None
    if zero_state:
        m = jnp.zeros((B, 1, 1), F32)
    tri = lambda keep: jnp.concatenate([jnp.where(keep, 1.0, 0.0).astype(BF16)] * 3, axis=0)
    b2 = _chunk_scan(fc, reverse_rows, tri(seen_t[0]), tri(seen_t[B - 1]))
    u2 = ic - b2
    u_t = jnp.concatenate([u2, jnp.zeros((L - B, L), F32)], axis=0).T
    u_col = jnp.stack([jnp.broadcast_to(u_t[:, i:i + 1], (L, L)) for i in range(B)])
    b, i_g, f_g = _rows(b2), _rows(ic), _rows(fc)
    b_last = jnp.sum(f_g, axis=2, keepdims=True)

    log_d = jnp.where(seen_t, b + u_col, -jnp.inf)
    a = b + m
    m_t = jnp.maximum(a, jnp.max(log_d, axis=1, keepdims=True))
    dmat = jnp.exp(log_d - m_t)
    sc = _per_chain(_dot, k, q_t) * dmat
    value_matmul = _paired_value_matmul if 2 * L <= MXU_DIM else functools.partial(_per_chain, _dot)
    num = value_matmul(v_t, sc.astype(BF16))
    den = jnp.sum(sc, axis=1, keepdims=True)
    if not zero_state:
        inter = jnp.exp(a - m_t)
        cq = _per_chain(_dot, CT, q_t)
        num = num + inter * cq[:, :d, :]
        den = den + inter * cq[:, d:d + 1, :]
    h_t = num * (1.0 / jnp.maximum(jnp.abs(den), jnp.exp(-m_t)))

    g = b_last - b + i_g
    m_new = jnp.maximum(b_last + m, jnp.max(g, axis=2, keepdims=True))
    w = jnp.exp(g - m_new)
    decay = jnp.exp(b_last + m - m_new)
    vw = jnp.concatenate([v_t.astype(F32), jnp.ones((B, ML_EXT, L), F32)], axis=1) * w
    CT_new = _per_chain(_dot, vw, k)
    if not zero_state:
        CT_new = decay * CT + CT_new
    return h_t, CT_new, m_new


def _mlstm_kernel(*refs, seq_len, nb, has_state, emit_state):
    q_ref, k_ref, v_ref, so_ref, g_ref, gmh_ref = refs[:6]
    refs = refs[6:]
    if has_state:
        (c0_ref, n0_ref, m0_ref), refs = refs[:3], refs[3:]
    hm_ref, refs = refs[0], refs[1:]
    if emit_state:
        (c_out_ref, n_out_ref, m_out_ref), refs = refs[:3], refs[3:]
    ct_sc, m_sc, gr_sc, h_sc = refs

    nc = seq_len // CHUNK
    d_head = ML_HEAD_DIM
    chains = [(bb, d, h) for bb in range(nb) for d in range(2) for h in range(ML_HEADS)]
    n_chain = len(chains)
    for i, (bb, d, h) in enumerate(chains):
        if has_state:
            ct_sc[i, 0:d_head, :] = c0_ref[bb, d, h].T
            ct_sc[i, d_head:d_head + ML_EXT, :] = jnp.broadcast_to(n0_ref[bb, d, h:h + 1, :], (ML_EXT, d_head))
            m_sc[i] = jnp.full((1, 1), m0_ref[pl.program_id(0) * nb + bb, 0, d, h], F32)
        elif nc > 2:
            ct_sc[i] = jnp.zeros((d_head + ML_EXT, d_head), F32)
            m_sc[i] = jnp.zeros((1, 1), F32)
    for c in range(nb * nc):
        gr_sc[c] = g_ref[c * CHUNK:(c + 1) * CHUNK, :].T

    s_idx = lax.broadcasted_iota(jnp.int32, (CHUNK, CHUNK), 0)
    t_idx = lax.broadcasted_iota(jnp.int32, (CHUNK, CHUNK), 1)
    seen_t = jnp.stack([(s_idx >= t_idx) if d else (s_idx <= t_idx) for _, d, _ in chains])
    reverse_rows = (lax.broadcasted_iota(jnp.int32, (n_chain, CHUNK), 0) // ML_HEADS) % 2 == 1

    def step(c_fwd, c_bwd, rows_of, zero_state=False):
        chunk_of = (c_fwd, c_bwd)
        lo = 2 * ML_HEADS
        ic, fc = [], []
        for bb in range(nb):
            g_fwd, g_bwd = gr_sc[bb * nc + c_fwd], gr_sc[bb * nc + c_bwd]
            ic += [g_fwd[0:ML_HEADS], g_bwd[lo:lo + ML_HEADS]]
            fc += [g_fwd[ML_HEADS:lo], g_bwd[lo + ML_HEADS:2 * lo]]
        head = [slice(h * d_head, (h + 1) * d_head) for _, _, h in chains]
        stack = lambda pick: jnp.stack([pick(i, bb * nc + chunk_of[d]) for i, (bb, d, _) in enumerate(chains)])
        h_t, CT_new, m_new = _mlstm_step(
            stack(lambda i, c: k_ref[rows_of(c), head[i]]),
            stack(lambda i, c: q_ref[c, head[i], :]),
            stack(lambda i, c: v_ref[c, head[i], :]),
            jnp.concatenate(ic, axis=0), jnp.concatenate(fc, axis=0),
            None if zero_state else ct_sc[...], None if zero_state else m_sc[...], seen_t, reverse_rows)
        ct_sc[...] = CT_new
        m_sc[...] = m_new
        for i, (bb, d, _) in enumerate(chains):
            h_sc[d, bb * nc + chunk_of[d], head[i], :] = h_t[i]

    if nc <= 2:
        for c in range(nc):
            step(c, nc - 1 - c, lambda cc: slice(cc * CHUNK, (cc + 1) * CHUNK),
                 zero_state=(c == 0 and not has_state))
    else:
        def body(c, carry):
            step(c, nc - 1 - c, lambda cc: pl.ds(pl.multiple_of(cc * CHUNK, CHUNK), CHUNK))
            return carry
        lax.fori_loop(0, nc, body, 0, unroll=2)

    for h in range(ML_HEADS):
        hcols = slice(h * d_head, (h + 1) * d_head)
        gain = jnp.broadcast_to(gmh_ref[h:h + 1, :], (CHUNK, d_head)).T
        for c in range(nb * nc):
            rows = slice(c * CHUNK, (c + 1) * CHUNK)
            hsum = h_sc[0, c, hcols, :] + h_sc[1, c, hcols, :]
            hn = hsum * lax.rsqrt(jnp.mean(hsum * hsum, axis=0, keepdims=True) + EPS) * gain
            hm_ref[rows, hcols] = (hn.T * so_ref[rows, hcols]).astype(hm_ref.dtype)
    if emit_state:
        for i, (bb, d, h) in enumerate(chains):
            c_out_ref[bb, d, h] = ct_sc[i, 0:d_head, :].T
            n_out_ref[bb, d, h:h + 1, :] = ct_sc[i, d_head:d_head + 1, :]
            m_out_ref[bb, d:d + 1, h:h + 1] = m_sc[i]


def _mlstm(q_t, k, v_t, so, gates, g_mh, state, batch, seq_len, emit_state):
    rows = k.shape[0]
    d = ML_HEAD_DIM
    nc = seq_len // CHUNK
    nb = ML_SHORT_SEQ_BATCH if nc <= 2 else 1
    assert batch % nb == 0
    tile = pl.BlockSpec((nb * seq_len, D_MODEL), lambda b: (b, 0))
    tile_t = pl.BlockSpec((nb * nc, D_MODEL, CHUNK), lambda b: (b, 0, 0))
    in_specs = [tile_t, tile, tile_t, tile,
                pl.BlockSpec((nb * seq_len, GATE_PAD), lambda b: (b, 0)),
                _resident((ML_HEADS, d))]
    args = [q_t, k, v_t, so, gates, g_mh]
    c_spec = pl.BlockSpec((nb, None, 2, ML_HEADS, d, d), lambda b: (b, 0, 0, 0, 0, 0))
    n_spec = pl.BlockSpec((nb, None, 2, ML_HEADS, d), lambda b: (b, 0, 0, 0, 0))
    if state is not None:
        in_specs += [c_spec, n_spec, pl.BlockSpec(memory_space=pltpu.SMEM)]
        args += list(state)
    out_specs = [tile]
    out_shape = [jax.ShapeDtypeStruct((rows, D_MODEL), BF16)]
    if emit_state:
        out_specs += [c_spec, n_spec, pl.BlockSpec((nb, None, 2, ML_HEADS), lambda b: (b, 0, 0, 0))]
        out_shape += [jax.ShapeDtypeStruct((batch, 1, 2, ML_HEADS, d, d), F32),
                      jax.ShapeDtypeStruct((batch, 1, 2, ML_HEADS, d), F32),
                      jax.ShapeDtypeStruct((batch, 1, 2, ML_HEADS), F32)]
    n_state = 2 * ML_HEADS * nb
    return pl.pallas_call(
        functools.partial(_mlstm_kernel, seq_len=seq_len, nb=nb, has_state=state is not None,
                          emit_state=emit_state),
        grid=(batch // nb,),
        in_specs=in_specs,
        out_specs=out_specs,
        out_shape=out_shape,
        scratch_shapes=[pltpu.VMEM((n_state, d + ML_EXT, d), F32), pltpu.VMEM((n_state, 1, 1), F32),
                        pltpu.VMEM((nb * nc, GATE_PAD, CHUNK), F32),
                        pltpu.VMEM((2, nb * nc, D_MODEL, CHUNK), F32)],
        compiler_params=_params(1),
        name="mlstm",
    )(*args)


def _merge_kernel(*refs, tiles, mod_index):
    n = len(tiles)
    (x_ref, m_ref), refs = refs[:2], refs[2:]
    parts = [refs[j * n:(j + 1) * n] for j in range(4)]
    wm_ref, wd_ref, wo_ref, o_ref, w_sc = refs[4 * n:]
    i = pl.program_id(0)

    @pl.when(i == 0)
    def _():
        for j, w_ref in enumerate((wm_ref, wd_ref, wo_ref)):
            w_sc[j] = w_ref[...].astype(BF16)

    def tile(hm_ref, att_ref, sgm_ref, sgd_ref):
        y = sgm_ref[...] * _dot(hm_ref[...], w_sc[0]) + sgd_ref[...] * _dot(att_ref[...], w_sc[1])
        o_ref[...] = x_ref[...] + _mod_reader(m_ref, mod_index(i * MERGE_TM))(5) * _dot(y, w_sc[2])

    starts = np.cumsum((0,) + tiles).tolist()
    for k in range(n):
        pl.when((i >= starts[k]) & (i < starts[k + 1]))(functools.partial(tile, *[p[k] for p in parts]))


def _merge(x, mods, mod_index, hm, att, sgm, sgd, wm, wd, wo):
    rows = x.shape[0]
    tiles = tuple(h.shape[0] // MERGE_TM for h in hm)
    assert sum(tiles) * MERGE_TM == rows
    tile = pl.BlockSpec((MERGE_TM, D_MODEL), lambda i: (i, 0))

    def part_spec(k):
        start = sum(tiles[:k])
        return pl.BlockSpec((MERGE_TM, D_MODEL), lambda i: (jnp.clip(i - start, 0, tiles[k] - 1), 0))

    part_specs = [part_spec(k) for k in range(len(tiles))]
    w_spec = _resident((D_MODEL, D_MODEL))
    return pl.pallas_call(
        functools.partial(_merge_kernel, tiles=tiles, mod_index=mod_index),
        grid=(rows // MERGE_TM,),
        in_specs=[tile, _resident(mods.shape)]
                 + part_specs * 4 + [w_spec, w_spec, w_spec],
        out_specs=tile,
        out_shape=jax.ShapeDtypeStruct((rows, D_MODEL), F32),
        scratch_shapes=[pltpu.VMEM((3, D_MODEL, D_MODEL), BF16)],
        compiler_params=_params(1),
        name="branch_merge",
    )(x, mods, *hm, *att, *sgm, *sgd, wm, wd, wo)


def _rope_tables(seq_len):
    lane = np.arange(LANES)
    r = lane % ROPE_AXIS_DIM
    freqs = np.power(np.float32(ROPE_BASE), -(r % ROPE_HALF).astype(np.float32) / np.float32(ROPE_HALF))
    tok = np.arange(seq_len)
    by_row = (lane % DA_HEAD_DIM < ROPE_AXIS_DIM)[None, :]
    pos = np.where(by_row, (tok // GRID_W)[:, None], (tok % GRID_W)[:, None]).astype(np.float32)
    ang = pos * freqs[None, :]
    sign = np.where(r < ROPE_HALF, -1.0, 1.0).astype(np.float32)
    return jnp.asarray(np.cos(ang), F32), jnp.asarray(np.sin(ang) * sign[None, :], F32)


def _mixer_branches(x, row0, mods, mod_index, w, batch, seq_len, ctx, lam_init):
    rope_tabs = None if ctx is None else _rope_tables(seq_len)
    mq, mk, mv, so, dq, dk, dv, sgm, sgd, gates = _proj(
        x, row0, batch * seq_len, mods, mod_index, w["g_norm"], w["w_seg"], w["b_gate"], w["bd"], w["g_qn"],
        w["g_kn"], rope_tabs, seq_len)
    cache = None if ctx is None else (ctx[0], ctx[1])
    att = _attn(dq, dk, dv, cache, w["lam"], w["g_sub"], batch, seq_len, lam_init)
    state = None if ctx is None else ctx[2]
    res = _mlstm(mq, mk, mv, so, gates, w["g_mh"], state, batch, seq_len, emit_state=ctx is None)
    return res[0], att, sgm, sgd, dk, dv, res[1:]


def kernel(x_prompt, x_sample, c, cache_k, cache_v, state_C, state_n, state_m, c_ctx, w_ada, b_ada, g_norm, ffn1_w1, ffn1_w3, ffn1_w2, ffn2_w1, ffn2_w3, ffn2_w2, w_in, b_gate, g_qn, g_kn, lam_q1, lam_k1, lam_q2, lam_k2, g_sub, g_mh, w_br_m, w_br_d, w_out):
    depth = w_ada.shape[0]
    assert depth == 1
    l = 0
    bp, tp, _ = x_prompt.shape
    bs, ts, _ = x_sample.shape
    past = cache_k.shape[2]
    lam_init = 0.8 - 0.6 * math.exp(-0.3 * l)

    cvecs = jnp.concatenate([c_ctx[None, :], c, jnp.zeros((MOD_ROWS - 1 - bs, D_MODEL), F32)], axis=0)
    mods = _mods(cvecs, w_ada[l], b_ada[l])

    group = np.arange(MXU_DIM) // DA_HEAD_DIM
    w = dict(
        g_norm=g_norm[l],
        ffn1_w1=ffn1_w1[l], ffn1_w3=ffn1_w3[l], ffn1_w2=ffn1_w2[l],
        ffn2_w1=ffn2_w1[l], ffn2_w3=ffn2_w3[l], ffn2_w2=ffn2_w2[l],
        w_in_t=w_in[l].T,
        b_gate=b_gate[l:l + 1],
        bd=jnp.asarray(group[:, None] == group[None, :], BF16),
        g_qn=g_qn[l:l + 1], g_kn=g_kn[l:l + 1],
        lam=(lam_q1[l:l + 1], lam_k1[l:l + 1], lam_q2[l:l + 1], lam_k2[l:l + 1]),
        g_sub=g_sub[l:l + 1], g_mh=g_mh[l],
        w_br_m=w_br_m[l], w_br_d=w_br_d[l], w_out=w_out[l],
    )

    n_ctx, n_lat = bp * tp, bs * ts
    mod_index = lambda r: jnp.where(r < n_ctx, 0, 1 + (r - n_ctx) // ts)
    seg_start = lambda j: pl.multiple_of(jnp.where(j < N_SEG, _seg_start(j), GATE_LO), N_GATE_COLS)
    x1, w["w_seg"] = _ffn((x_prompt.reshape(n_ctx, D_MODEL), x_sample.reshape(n_lat, D_MODEL)), (n_ctx + n_lat,),
                          mods, mod_index, w["g_norm"], w["ffn1_w1"], w["ffn1_w3"], w["ffn1_w2"], base=0,
                          side=(w["w_in_t"], N_SEG + 1, seg_start))

    hm_p, att_p, sgm_p, sgd_p, new_k_t, new_v, (new_c, new_n, new_m) = _mixer_branches(
        x1, 0, mods, mod_index, w, bp, tp, None, lam_init)

    ctx = (cache_k[:, l].transpose(0, 2, 3, 4, 1).reshape(bs, D_MODEL, past),
           cache_v[:, l].reshape(bs, past * DA_HEADS, DA_V_DIM),
           (state_C, state_n, state_m))
    hm_s, att_s, sgm_s, sgd_s, _, _, _ = _mixer_branches(x1, n_ctx, mods, mod_index, w, bs, ts, ctx, lam_init)

    x2 = _merge(x1, mods, mod_index, (hm_p, hm_s), (att_p, att_s), (sgm_p, sgm_s), (sgd_p, sgd_s),
                w["w_br_m"], w["w_br_d"], w["w_out"])
    xp, xs = _ffn((x2,), (n_ctx, n_lat), mods, mod_index, w["g_norm"], w["ffn2_w1"], w["ffn2_w3"], w["ffn2_w2"],
                  base=6)

    return (xp.reshape(bp, tp, D_MODEL), xs.reshape(bs, ts, D_MODEL),
            new_k_t.reshape(bp, DA_HEADS, 2, DA_HEAD_DIM, tp).transpose(0, 4, 1, 2, 3)[:, None],
            new_v.reshape(bp, 1, tp, DA_HEADS, DA_V_DIM),
            new_c, new_n, new_m)
```

```python
import functools
import math

import jax
import jax.numpy as jnp
import numpy as np
from jax import lax
from jax.experimental import pallas as pl
from jax.experimental.pallas import tpu as pltpu

F32 = jnp.float32
BF16 = jnp.bfloat16

D_MODEL = 1024
D_FF = 2816
N_MOD = 9
GRID_W = 64
ML_HEADS = 4
ML_HEAD_DIM = 256
DA_HEADS = 8
DA_HEAD_DIM = 64
DA_V_DIM = 128
N_GATE_COLS = 16
CHUNK = 256
ROPE_BASE = 10000.0
QK_LOG2_SCALE = DA_HEAD_DIM ** -0.5 * math.log2(math.e)
EPS = 1e-6

LANES = 128
SUBLANES = 8
GATE_PAD = LANES
MXU_DIM = 256
VMEM_LIMIT = 56 * 1024 * 1024

SIDE_PIECES = 8
MOD_ROWS = SUBLANES
FFN_TM = 512
FFN_TF = 256
PROJ_TM = 256
MERGE_TM = 512
ATT_TQ = 256
ATT_TK_MAX = 768
ATT_SHORT_SEQ_BATCH = 2


def _params(n_axes):
    return pltpu.CompilerParams(dimension_semantics=("arbitrary",) * n_axes,
                                vmem_limit_bytes=VMEM_LIMIT)


def _dot(a, b):
    return jnp.dot(a.astype(BF16), b.astype(BF16), preferred_element_type=F32)


def _dot_nt(a, b):
    return lax.dot_general(a.astype(BF16), b.astype(BF16), (((1,), (1,)), ((), ())),
                           preferred_element_type=F32)


def _sigmoid(x):
    return 1.0 / (1.0 + jnp.exp(-x))


def _log_sigmoid(x):
    return jnp.minimum(x, 0.0) - jnp.log1p(jnp.exp(-jnp.abs(x)))


def _modulated_norm(x, g, shift, scale):
    y = x * lax.rsqrt(jnp.mean(x * x, axis=-1, keepdims=True) + EPS) * g
    return y * (1.0 + scale) + shift


def _resident(shape):
    return pl.BlockSpec(shape, lambda *_: (0,) * len(shape), pipeline_mode=pl.Buffered(1))


def _mods_kernel(c_ref, w_ref, b_ref, o_ref):
    c = c_ref[...]
    o_ref[...] = _dot(c * _sigmoid(c), w_ref[...]) + b_ref[...]


def _mods(cvecs, w_ada, b_ada):
    n = N_MOD * D_MODEL
    tn = D_MODEL
    return pl.pallas_call(
        _mods_kernel,
        grid=(n // tn,),
        in_specs=[pl.BlockSpec((MOD_ROWS, D_MODEL), lambda j: (0, 0)),
                  pl.BlockSpec((D_MODEL, tn), lambda j: (0, j)),
                  pl.BlockSpec((1, tn), lambda j: (0, j))],
        out_specs=pl.BlockSpec((MOD_ROWS, tn), lambda j: (0, j)),
        out_shape=jax.ShapeDtypeStruct((MOD_ROWS, n), F32),
        compiler_params=_params(1),
        name="adaln_mods",
    )(cvecs, w_ada, b_ada.reshape(1, n))


def _mod_reader(m_ref, row):
    return lambda j: m_ref[pl.ds(row, 1), j * D_MODEL:(j + 1) * D_MODEL]


def _ffn_kernel(*refs, base, mod_index, in_tiles, out_tiles, side_blocks):
    n_in, n_out = len(in_tiles), len(out_tiles)
    x_refs, refs = refs[:n_in], refs[n_in:]
    (m_ref, g_ref, w1_ref, w3_ref, w2_ref), refs = refs[:5], refs[5:]
    if side_blocks:
        side_in_ref, refs = refs[0], refs[1:]
        side_out_ref, refs = refs[n_out], refs[:n_out] + refs[n_out + 1:]
    o_refs, (w1_sc, w3_sc, w2_sc, hh_sc, acc_sc) = refs[:n_out], refs[n_out:]
    g = pl.program_id(0)
    nf = D_FF // FFN_TF
    row_tile = g - (nf - 1)
    mod = _mod_reader(m_ref, mod_index(jnp.maximum(row_tile, 0) * FFN_TM))
    gain = g_ref[base // 3:base // 3 + 1, :]
    norm = lambda x: _modulated_norm(x, gain, mod(base), mod(base + 1)).astype(BF16)
    finish = lambda x, acc: x + 0.5 * mod(base + 2) * acc

    def tile(hh, f):
        a = _dot(hh, w1_sc[f])
        b = _dot(hh, w3_sc[f])
        return _dot(a * _sigmoid(a) * b, w2_sc[f])

    @pl.when(g < nf)
    def _():
        w1_sc[g] = w1_ref[...].astype(BF16)
        w3_sc[g] = w3_ref[...].astype(BF16)
        w2_sc[g] = w2_ref[...].astype(BF16)

        @pl.when(g == 0)
        def _():
            hh_sc[...] = norm(x_refs[0][...])
            acc_sc[...] = jnp.zeros(acc_sc.shape, F32)

        acc_sc[...] += tile(hh_sc[...], g)

        @pl.when(g == nf - 1)
        def _():
            o_refs[0][...] = finish(x_refs[0][...], acc_sc[...])

    def full_tile(x_ref, o_ref):
        x = x_ref[...]
        hh = norm(x)
        acc = jnp.zeros(x.shape, F32)
        for f in range(nf):
            acc = acc + tile(hh, f)
            if side_blocks and f < SIDE_PIECES:
                rows = slice(f * (D_MODEL // SIDE_PIECES), (f + 1) * (D_MODEL // SIDE_PIECES))
                side_out_ref[rows, :] = side_in_ref[rows, :].astype(BF16)
        o_ref[...] = finish(x, acc)

    bounds = sorted(set(np.cumsum((0,) + in_tiles).tolist()) | set(np.cumsum((0,) + out_tiles).tolist()))
    for lo, hi in zip(bounds[:-1], bounds[1:]):
        k_in = int(np.searchsorted(np.cumsum(in_tiles), lo, side="right"))
        k_out = int(np.searchsorted(np.cumsum(out_tiles), lo, side="right"))
        pl.when((g >= nf) & (row_tile >= lo) & (row_tile < hi))(
            functools.partial(full_tile, x_refs[k_in], o_refs[k_out]))


def _ffn(xs, out_rows, mods, mod_index, g, w1, w3, w2, base, side=None):
    nf = D_FF // FFN_TF
    in_tiles = tuple(x.shape[0] // FFN_TM for x in xs)
    out_tiles = tuple(r // FFN_TM for r in out_rows)
    assert sum(in_tiles) == sum(out_tiles)
    side_in_specs, side_out_specs, side_out_shape, side_args, side_blocks = [], [], [], [], 0
    if side is not None:
        side_w, side_blocks, side_start = side
        assert side_blocks <= sum(in_tiles) - 1
        blk = lambda s: jnp.clip(s - nf, 0, side_blocks - 1)
        side_in_specs = [pl.BlockSpec((pl.Element(D_MODEL), pl.Element(D_MODEL)), lambda s: (side_start(blk(s)), 0))]
        side_out_specs = [pl.BlockSpec((None, D_MODEL, D_MODEL), lambda s: (blk(s), 0, 0))]
        side_out_shape = [jax.ShapeDtypeStruct((side_blocks, D_MODEL, D_MODEL), BF16)]
        side_args = [side_w]
    row_tile = lambda s: jnp.maximum(s - (nf - 1), 0)
    f_tile = lambda s: jnp.minimum(s, nf - 1)

    def part_spec(tiles, k):
        start = sum(tiles[:k])
        return pl.BlockSpec((FFN_TM, D_MODEL), lambda s: (jnp.clip(row_tile(s) - start, 0, tiles[k] - 1), 0))

    return pl.pallas_call(
        functools.partial(_ffn_kernel, base=base, mod_index=mod_index, in_tiles=in_tiles, out_tiles=out_tiles,
                          side_blocks=side_blocks),
        grid=(nf - 1 + sum(in_tiles),),
        in_specs=[part_spec(in_tiles, k) for k in range(len(xs))] + [
            _resident(mods.shape),
            _resident(g.shape),
            pl.BlockSpec((D_MODEL, FFN_TF), lambda s: (0, f_tile(s))),
            pl.BlockSpec((D_MODEL, FFN_TF), lambda s: (0, f_tile(s))),
            pl.BlockSpec((FFN_TF, D_MODEL), lambda s: (f_tile(s), 0))] + side_in_specs,
        out_specs=[part_spec(out_tiles, k) for k in range(len(out_rows))] + side_out_specs,
        out_shape=[jax.ShapeDtypeStruct((r, D_MODEL), F32) for r in out_rows] + side_out_shape,
        scratch_shapes=[pltpu.VMEM((nf, D_MODEL, FFN_TF), BF16), pltpu.VMEM((nf, D_MODEL, FFN_TF), BF16),
                        pltpu.VMEM((nf, FFN_TF, D_MODEL), BF16),
                        pltpu.VMEM((FFN_TM, D_MODEL), BF16), pltpu.VMEM((FFN_TM, D_MODEL), F32)],
        compiler_params=_params(1),
        name="ffn",
    )(*xs, mods, g, w1, w3, w2, *side_args)


def _group_norm64(x, bd, g):
    ss = _dot(x * x, bd)
    return x * lax.rsqrt(ss * (1.0 / DA_HEAD_DIM) + EPS) * g


ROPE_AXIS_DIM = DA_HEAD_DIM // 2
ROPE_HALF = ROPE_AXIS_DIM // 2


def _rope(x, cos, sin_signed):
    first = (lax.broadcasted_iota(jnp.int32, x.shape, 1) % ROPE_AXIS_DIM) < ROPE_HALF
    partner = jnp.where(first, pltpu.roll(x, LANES - ROPE_HALF, 1), pltpu.roll(x, ROPE_HALF, 1))
    return x * cos + partner * sin_signed


N_SEG = 9
PROJ_WARM = (N_SEG + 1) // 2
GATE_LO = 4 * D_MODEL


def _seg_start(s):
    return s * D_MODEL + N_GATE_COLS * (s >= 4)


def _proj_kernel(*refs, rope, row_of_step):
    (x_ref, m_ref, g_ref, wt_ref, bg_ref, bd_ref, gq_ref, gk_ref), refs = refs[:8], refs[8:]
    if rope:
        (cos_ref, sin_ref), refs = refs[:2], refs[2:]
    (mq_ref, mk_ref, mv_ref, so_ref, dq_ref, dk_ref, dv_ref, sgm_ref, sgd_ref, gates_ref,
     w_sc, wg_sc, hh_sc) = refs
    step = pl.program_id(0)
    mod = _mod_reader(m_ref, row_of_step(step))
    norm = lambda: _modulated_norm(x_ref[...], g_ref[1:2, :], mod(3), mod(4)).astype(BF16)
    groups = MXU_DIM // DA_HEAD_DIM
    tiled = lambda ref, scale: jnp.concatenate([ref[...] * scale] * groups, axis=1)
    gq_row, gk_row = tiled(gq_ref, QK_LOG2_SCALE), tiled(gk_ref, 1.0)
    bias = jnp.concatenate([bg_ref[...], jnp.zeros((1, GATE_PAD - N_GATE_COLS), F32)], axis=1)

    def gates(hh):
        y = _dot_nt(hh, wg_sc[...]) + bias
        col = lax.broadcasted_iota(jnp.int32, y.shape, 1)
        gates_ref[...] = jnp.where((col // ML_HEADS) % 2 == 1, _log_sigmoid(y), y)

    def qk_segment(y, gain_row, o_ref):
        bd = bd_ref[...]
        for c in range(D_MODEL // MXU_DIM):
            cols = slice(c * MXU_DIM, (c + 1) * MXU_DIM)
            z = _group_norm64(y[:, cols], bd, gain_row)
            if rope:
                z = jnp.concatenate(
                    [_rope(z[:, k * LANES:(k + 1) * LANES], cos_ref[...], sin_ref[...])
                     for k in range(MXU_DIM // LANES)], axis=1)
            o_ref[:, cols] = z.astype(o_ref.dtype)

    def cache_key_segment(hh):
        y_t = _dot_nt(w_sc[5], hh)
        z = y_t.reshape(D_MODEL // DA_HEAD_DIM, DA_HEAD_DIM, y_t.shape[1])
        ms = jnp.mean(z * z, axis=1, keepdims=True)
        gain = jnp.broadcast_to(gk_row[:, 0:LANES], (LANES, LANES)).T[0:DA_HEAD_DIM, :]
        gain = jnp.concatenate([gain] * (y_t.shape[1] // LANES), axis=1)
        dk_ref[...] = (z * lax.rsqrt(ms + EPS) * gain).reshape(y_t.shape)

    def segment(s, hh):
        if s in (0, 2):
            o_ref = {0: mq_ref, 2: mv_ref}[s]
            y_t = _dot_nt(w_sc[s], hh) * ((ML_HEAD_DIM ** -0.5) if s == 0 else 1.0)
            for j in range(PROJ_TM // CHUNK):
                o_ref[j] = y_t[:, j * CHUNK:(j + 1) * CHUNK].astype(o_ref.dtype)
            return
        if s == 5 and not rope:
            cache_key_segment(hh)
            return
        y = _dot_nt(hh, w_sc[s])
        if s == 4:
            qk_segment(y, gq_row, dq_ref)
        elif s == 5:
            qk_segment(y, gk_row, dk_ref)
        elif s in (3, 7, 8):
            {3: so_ref, 7: sgm_ref, 8: sgd_ref}[s][...] = _sigmoid(y)
        elif s == 6 and not rope:
            dv_ref[...] = y.reshape(y.shape[0], DA_HEADS, DA_V_DIM)
        else:
            o_ref = {1: mk_ref, 6: dv_ref}[s]
            o_ref[...] = y.astype(o_ref.dtype)

    for w in range(PROJ_WARM):
        @pl.when(step == w)
        def _(w=w):
            if w == 0:
                hh_sc[...] = norm()
            for j in range(2):
                s = 2 * w + j
                if s < N_SEG:
                    w_sc[s] = wt_ref[j]
                    segment(s, hh_sc[...])
                else:
                    wg_sc[...] = jnp.zeros(wg_sc.shape, BF16)
                    wg_sc[0:N_GATE_COLS, :] = wt_ref[j, 0:N_GATE_COLS, :]
                    gates(hh_sc[...])

    @pl.when(step >= PROJ_WARM)
    def _():
        hh = norm()
        gates(hh)
        for s in range(N_SEG):
            segment(s, hh)


def _proj(x, row0, rows, mods, mod_index, g, w_seg, b_gate, bd, gq, gk, rope_tabs, seq_len):
    rope = rope_tabs is not None
    row_tile = lambda s: jnp.maximum(s - (PROJ_WARM - 1), 0)
    row = lambda s: (row_tile(s), 0)
    tile = pl.BlockSpec((PROJ_TM, D_MODEL), row)
    w_block = (2, D_MODEL, D_MODEL)
    in_specs = [pl.BlockSpec((PROJ_TM, D_MODEL), lambda s: (row0 // PROJ_TM + row_tile(s), 0)),
                _resident(mods.shape),
                _resident(g.shape),
                pl.BlockSpec(w_block, lambda s: (jnp.minimum(s, PROJ_WARM - 1), 0, 0)),
                _resident(b_gate.shape),
                _resident((MXU_DIM, MXU_DIM)),
                _resident(gq.shape),
                _resident(gk.shape)]
    args = [x, mods, g, w_seg, b_gate, bd, gq, gk]
    if rope:
        tiles_per_seq = seq_len // PROJ_TM
        tab = pl.BlockSpec((PROJ_TM, LANES), lambda s: (row_tile(s) % tiles_per_seq, 0))
        in_specs += [tab, tab]
        args += list(rope_tabs)
    dtypes = [BF16, BF16, BF16, F32, BF16, BF16, BF16 if rope else F32, F32, F32]
    out_shape = [jax.ShapeDtypeStruct((rows, D_MODEL), dt) for dt in dtypes]
    out_shape.append(jax.ShapeDtypeStruct((rows, GATE_PAD), F32))
    out_specs = [tile] * 9 + [pl.BlockSpec((PROJ_TM, GATE_PAD), row)]
    slabs = PROJ_TM // CHUNK
    for j in (0, 2):
        out_shape[j] = jax.ShapeDtypeStruct((rows // CHUNK, D_MODEL, CHUNK), BF16)
        out_specs[j] = pl.BlockSpec((slabs, D_MODEL, CHUNK), lambda s: (row_tile(s), 0, 0))
    if not rope:
        assert seq_len == PROJ_TM
        out_shape[5] = jax.ShapeDtypeStruct((rows // seq_len, D_MODEL, seq_len), F32)
        out_specs[5] = pl.BlockSpec((None, D_MODEL, seq_len), lambda s: (row_tile(s), 0, 0))
        out_shape[6] = jax.ShapeDtypeStruct((rows, DA_HEADS, DA_V_DIM), F32)
        out_specs[6] = pl.BlockSpec((PROJ_TM, DA_HEADS, DA_V_DIM), lambda s: (row_tile(s), 0, 0))
    return pl.pallas_call(
        functools.partial(_proj_kernel, rope=rope,
                          row_of_step=lambda step: mod_index(row0 + row_tile(step) * PROJ_TM)),
        grid=(PROJ_WARM - 1 + rows // PROJ_TM,),
        in_specs=in_specs,
        out_specs=out_specs,
        out_shape=out_shape,
        scratch_shapes=[pltpu.VMEM((N_SEG, D_MODEL, D_MODEL), BF16), pltpu.VMEM((GATE_PAD, D_MODEL), BF16),
                        pltpu.VMEM((PROJ_TM, D_MODEL), BF16)],
        compiler_params=_params(1),
        name="mixer_in_proj",
    )(*args)


def _lambda(lam_refs, lam_init):
    q1, k1, q2, k2 = (r[...] for r in lam_refs)
    s1 = jnp.sum(q1 * k1, axis=1, keepdims=True)
    s2 = jnp.sum(q2 * k2, axis=1, keepdims=True)
    return jnp.exp(s1) - jnp.exp(s2) + lam_init


def _attn_kernel(*refs, cached, seq, nb, lam_init):
    if cached:
        (q_ref, k_ref, v_ref, ck_ref, cv_ref), refs = refs[:5], refs[5:]
    else:
        (q_ref, k_ref, v_ref), refs = refs[:3], refs[3:]
    lam_refs, (gs_ref, o_ref, kall_sc, vt_sc, s_sc) = refs[:4], refs[4:]
    units = [(bb, h) for bb in range(nb) for h in range(DA_HEADS)]

    def cache_order_kv(kt_ref, vr_ref, u, h, lo, n):
        kall_sc[u, lo:lo + n, :] = kt_ref[h * DA_V_DIM:(h + 1) * DA_V_DIM, :].T.astype(BF16)
        vt_sc[u, :, lo:lo + n] = vr_ref[pl.ds(h, n, stride=DA_HEADS), :].T.astype(BF16)

    @pl.when(pl.program_id(1) == 0)
    def _():
        for u, (bb, h) in enumerate(units):
            if cached:
                cols = slice(h * DA_V_DIM, (h + 1) * DA_V_DIM)
                kall_sc[u, 0:seq, :] = k_ref[:, cols].astype(BF16)
                vt_sc[u, :, 0:seq] = v_ref[:, cols].astype(F32).T.astype(BF16)
                cache_order_kv(ck_ref.at[bb], cv_ref.at[bb], u, h, seq, ck_ref.shape[2])
            else:
                cache_order_kv(k_ref.at[bb], v_ref.at[bb], u, h, 0, seq)

    lam = _lambda(lam_refs, lam_init)
    sub_gain = gs_ref[...] * (1.0 - lam_init)
    lane = lax.broadcasted_iota(jnp.int32, (1, DA_V_DIM), 1)
    comp_masks = [lane < DA_HEAD_DIM, lane >= DA_HEAD_DIM]
    tq = q_ref.shape[0] // nb
    n_keys = kall_sc.shape[1]
    n_tiles = pl.cdiv(n_keys, ATT_TK_MAX)
    tk = n_keys // n_tiles

    def stacked_q(u):
        bb, h = units[u]
        q = q_ref[bb * tq:(bb + 1) * tq, h * DA_V_DIM:(h + 1) * DA_V_DIM].astype(BF16)
        return jnp.concatenate([jnp.where(m, q, jnp.zeros_like(q)) for m in comp_masks], axis=0)

    def score_tile(h, j, qq, m8):
        rows = slice(j * tk, (j + 1) * tk)
        st = _dot_nt(kall_sc[h, rows, :], qq)
        s_sc[h % 2, rows, :] = st
        t8 = jnp.max(st.reshape(tk // SUBLANES, SUBLANES, 2 * tq), axis=0)
        return t8 if m8 is None else jnp.maximum(m8, t8)

    def prob_tile(h, j, mx, d8, pv):
        rows = slice(j * tk, (j + 1) * tk)
        e = jnp.exp2(s_sc[h % 2, rows, :] - mx)
        s8 = jnp.sum(e.reshape(tk // SUBLANES, SUBLANES, 2 * tq), axis=0)
        p = _dot(vt_sc[h, :, rows], e)
        return (s8 if d8 is None else d8 + s8), (p if pv is None else pv + p)

    qq = stacked_q(0)
    m8 = None
    for j in range(n_tiles):
        m8 = score_tile(0, j, qq, m8)
    for h in range(len(units)):
        mx = jnp.max(m8, axis=0, keepdims=True)
        if h + 1 < len(units):
            qq = stacked_q(h + 1)
        m8, d8, pv = None, None, None
        for j in range(n_tiles):
            if h + 1 < len(units):
                m8 = score_tile(h + 1, j, qq, m8)
            d8, pv = prob_tile(h, j, mx, d8, pv)
        inv = 1.0 / jnp.sum(d8, axis=0, keepdims=True)
        out_t = pv[:, :tq] * inv[:, :tq] - pv[:, tq:] * (lam * inv[:, tq:])
        out_t = out_t * lax.rsqrt(jnp.mean(out_t * out_t, axis=0, keepdims=True) + EPS)
        bb, head = units[h]
        o_ref[bb * tq:(bb + 1) * tq, head * DA_V_DIM:(head + 1) * DA_V_DIM] = (out_t.T * sub_gain).astype(o_ref.dtype)


def _attn(q, k, v, cache, lam_vecs, g_sub, batch, seq_len, lam_init):
    rows = q.shape[0]
    nq = seq_len // ATT_TQ
    nb = ATT_SHORT_SEQ_BATCH if (cache is None and nq == 1) else 1
    assert batch % nb == 0
    q_spec = pl.BlockSpec((nb * ATT_TQ, D_MODEL), lambda b, i: (b * nq + i, 0))
    cache_specs = lambda n: [pl.BlockSpec((nb, D_MODEL, n), lambda b, i: (b, 0, 0)),
                             pl.BlockSpec((nb, n * DA_HEADS, DA_V_DIM), lambda b, i: (b, 0, 0))]
    n_keys = seq_len
    if cache is not None:
        past = cache[0].shape[2]
        n_keys += past
        kv_spec = pl.BlockSpec((seq_len, D_MODEL), lambda b, i: (b, 0))
        in_specs = [q_spec, kv_spec, kv_spec] + cache_specs(past)
        args = [q, k, v] + list(cache)
    else:
        in_specs = [q_spec] + cache_specs(seq_len)
        args = [q, k, v.reshape(batch, seq_len * DA_HEADS, DA_V_DIM)]
    in_specs += [_resident((1, DA_HEAD_DIM))] * 4 + [_resident((1, DA_V_DIM))]
    args += list(lam_vecs) + [g_sub]
    return pl.pallas_call(
        functools.partial(_attn_kernel, cached=cache is not None, seq=seq_len, nb=nb, lam_init=lam_init),
        grid=(batch // nb, nq),
        in_specs=in_specs,
        out_specs=q_spec,
        out_shape=jax.ShapeDtypeStruct((rows, D_MODEL), BF16),
        scratch_shapes=[pltpu.VMEM((nb * DA_HEADS, n_keys, DA_V_DIM), BF16),
                        pltpu.VMEM((nb * DA_HEADS, DA_V_DIM, n_keys), BF16),
                        pltpu.VMEM((2, n_keys, 2 * ATT_TQ), F32)],
        compiler_params=_params(2),
        name="diff_attention",
    )(*args)


def _per_chain(fn, a, b):
    return jnp.stack([fn(a[i], b[i]) for i in range(a.shape[0])])


ML_EXT = SUBLANES
ML_SHORT_SEQ_BATCH = 2


def _rows(x):
    return jnp.stack([x[i:i + 1, :] for i in range(x.shape[0])])


def _split3(x):
    hi = x.astype(BF16)
    r = x - hi.astype(F32)
    mid = r.astype(BF16)
    lo = (r - mid.astype(F32)).astype(BF16)
    return jnp.concatenate([hi, mid, lo], axis=1)


def _chunk_scan(x, reverse_rows, tri_prefix, tri_suffix):
    parts = _split3(x)
    return jnp.where(reverse_rows, _dot(parts, tri_suffix), _dot(parts, tri_prefix))


def _paired_value_matmul(v_t, sc):
    B, L, _ = sc.shape
    zero = jnp.zeros((L, L), sc.dtype)
    out = []
    for i in range(0, B, 2):
        lhs = jnp.concatenate([v_t[i], v_t[i + 1]], axis=1)
        rhs = jnp.concatenate([jnp.concatenate([sc[i], zero], axis=1),
                               jnp.concatenate([zero, sc[i + 1]], axis=1)], axis=0)
        both = _dot(lhs, rhs)
        out += [both[:, :L], both[:, L:]]
    return jnp.stack(out)


def _mlstm_step(k, q_t, v_t, ic, fc, CT, m, seen_t, reverse_rows):
    B, L = ic.shape
    d = k.shape[2]
    zero_state = CT is None
    if zero_state:
        m = jnp.zeros((B, 1, 1), F32)
    tri = lambda keep: jnp.concatenate([jnp.where(keep, 1.0, 0.0).astype(BF16)] * 3, axis=0)
    b2 = _chunk_scan(fc, reverse_rows, tri(seen_t[0]), tri(seen_t[B - 1]))
    u2 = ic - b2
    u_t = jnp.concatenate([u2, jnp.zeros((L - B, L), F32)], axis=0).T
    u_col = jnp.stack([jnp.broadcast_to(u_t[:, i:i + 1], (L, L)) for i in range(B)])
    b, i_g, f_g = _rows(b2), _rows(ic), _rows(fc)
    b_last = jnp.sum(f_g, axis=2, keepdims=True)

    log_d = jnp.where(seen_t, b + u_col, -jnp.inf)
    a = b + m
    m_t = jnp.maximum(a, jnp.max(log_d, axis=1, keepdims=True))
    dmat = jnp.exp(log_d - m_t)
    sc = _per_chain(_dot, k, q_t) * dmat
    value_matmul = _paired_value_matmul if 2 * L <= MXU_DIM else functools.partial(_per_chain, _dot)
    num = value_matmul(v_t, sc.astype(BF16))
    den = jnp.sum(sc, axis=1, keepdims=True)
    if not zero_state:
        inter = jnp.exp(a - m_t)
        cq = _per_chain(_dot, CT, q_t)
        num = num + inter * cq[:, :d, :]
        den = den + inter * cq[:, d:d + 1, :]
    h_t = num * (1.0 / jnp.maximum(jnp.abs(den), jnp.exp(-m_t)))

    g = b_last - b + i_g
    m_new = jnp.maximum(b_last + m, jnp.max(g, axis=2, keepdims=True))
    w = jnp.exp(g - m_new)
    decay = jnp.exp(b_last + m - m_new)
    vw = jnp.concatenate([v_t.astype(F32), jnp.ones((B, ML_EXT, L), F32)], axis=1) * w
    CT_new = _per_chain(_dot, vw, k)
    if not zero_state:
        CT_new = decay * CT + CT_new
    return h_t, CT_new, m_new


def _mlstm_kernel(*refs, seq_len, nb, has_state, emit_state):
    q_ref, k_ref, v_ref, so_ref, g_ref, gmh_ref = refs[:6]
    refs = refs[6:]
    if has_state:
        (c0_ref, n0_ref, m0_ref), refs = refs[:3], refs[3:]
    hm_ref, refs = refs[0], refs[1:]
    if emit_state:
        (c_out_ref, n_out_ref, m_out_ref), refs = refs[:3], refs[3:]
    ct_sc, m_sc, gr_sc, h_sc = refs

    nc = seq_len // CHUNK
    d_head = ML_HEAD_DIM
    chains = [(bb, d, h) for bb in range(nb) for d in range(2) for h in range(ML_HEADS)]
    n_chain = len(chains)
    for i, (bb, d, h) in enumerate(chains):
        if has_state:
            ct_sc[i, 0:d_head, :] = c0_ref[bb, d, h].T
            ct_sc[i, d_head:d_head + ML_EXT, :] = jnp.broadcast_to(n0_ref[bb, d, h:h + 1, :], (ML_EXT, d_head))
            m_sc[i] = jnp.full((1, 1), m0_ref[pl.program_id(0) * nb + bb, 0, d, h], F32)
        elif nc > 2:
            ct_sc[i] = jnp.zeros((d_head + ML_EXT, d_head), F32)
            m_sc[i] = jnp.zeros((1, 1), F32)
    for c in range(nb * nc):
        gr_sc[c] = g_ref[c * CHUNK:(c + 1) * CHUNK, :].T

    s_idx = lax.broadcasted_iota(jnp.int32, (CHUNK, CHUNK), 0)
    t_idx = lax.broadcasted_iota(jnp.int32, (CHUNK, CHUNK), 1)
    seen_t = jnp.stack([(s_idx >= t_idx) if d else (s_idx <= t_idx) for _, d, _ in chains])
    reverse_rows = (lax.broadcasted_iota(jnp.int32, (n_chain, CHUNK), 0) // ML_HEADS) % 2 == 1

    def step(c_fwd, c_bwd, rows_of, zero_state=False):
        chunk_of = (c_fwd, c_bwd)
        lo = 2 * ML_HEADS
        ic, fc = [], []
        for bb in range(nb):
            g_fwd, g_bwd = gr_sc[bb * nc + c_fwd], gr_sc[bb * nc + c_bwd]
            ic += [g_fwd[0:ML_HEADS], g_bwd[lo:lo + ML_HEADS]]
            fc += [g_fwd[ML_HEADS:lo], g_bwd[lo + ML_HEADS:2 * lo]]
        head = [slice(h * d_head, (h + 1) * d_head) for _, _, h in chains]
        stack = lambda pick: jnp.stack([pick(i, bb * nc + chunk_of[d]) for i, (bb, d, _) in enumerate(chains)])
        h_t, CT_new, m_new = _mlstm_step(
            stack(lambda i, c: k_ref[rows_of(c), head[i]]),
            stack(lambda i, c: q_ref[c, head[i], :]),
            stack(lambda i, c: v_ref[c, head[i], :]),
            jnp.concatenate(ic, axis=0), jnp.concatenate(fc, axis=0),
            None if zero_state else ct_sc[...], None if zero_state else m_sc[...], seen_t, reverse_rows)
        ct_sc[...] = CT_new
        m_sc[...] = m_new
        for i, (bb, d, _) in enumerate(chains):
            h_sc[d, bb * nc + chunk_of[d], head[i], :] = h_t[i]

    if nc <= 2:
        for c in range(nc):
            step(c, nc - 1 - c, lambda cc: slice(cc * CHUNK, (cc + 1) * CHUNK),
                 zero_state=(c == 0 and not has_state))
    else:
        def body(c, carry):
            step(c, nc - 1 - c, lambda cc: pl.ds(pl.multiple_of(cc * CHUNK, CHUNK), CHUNK))
            return carry
        lax.fori_loop(0, nc, body, 0, unroll=2)

    for h in range(ML_HEADS):
        hcols = slice(h * d_head, (h + 1) * d_head)
        gain = jnp.broadcast_to(gmh_ref[h:h + 1, :], (CHUNK, d_head)).T
        for c in range(nb * nc):
            rows = slice(c * CHUNK, (c + 1) * CHUNK)
            hsum = h_sc[0, c, hcols, :] + h_sc[1, c, hcols, :]
            hn = hsum * lax.rsqrt(jnp.mean(hsum * hsum, axis=0, keepdims=True) + EPS) * gain
            hm_ref[rows, hcols] = (hn.T * so_ref[rows, hcols]).astype(hm_ref.dtype)
    if emit_state:
        for i, (bb, d, h) in enumerate(chains):
            c_out_ref[bb, d, h] = ct_sc[i, 0:d_head, :].T
            n_out_ref[bb, d, h:h + 1, :] = ct_sc[i, d_head:d_head + 1, :]
            m_out_ref[bb, d:d + 1, h:h + 1] = m_sc[i]


def _mlstm(q_t, k, v_t, so, gates, g_mh, state, batch, seq_len, emit_state):
    rows = k.shape[0]
    d = ML_HEAD_DIM
    nc = seq_len // CHUNK
    nb = ML_SHORT_SEQ_BATCH if nc <= 2 else 1
    assert batch % nb == 0
    tile = pl.BlockSpec((nb * seq_len, D_MODEL), lambda b: (b, 0))
    tile_t = pl.BlockSpec((nb * nc, D_MODEL, CHUNK), lambda b: (b, 0, 0))
    in_specs = [tile_t, tile, tile_t, tile,
                pl.BlockSpec((nb * seq_len, GATE_PAD), lambda b: (b, 0)),
                _resident((ML_HEADS, d))]
    args = [q_t, k, v_t, so, gates, g_mh]
    c_spec = pl.BlockSpec((nb, None, 2, ML_HEADS, d, d), lambda b: (b, 0, 0, 0, 0, 0))
    n_spec = pl.BlockSpec((nb, None, 2, ML_HEADS, d), lambda b: (b, 0, 0, 0, 0))
    if state is not None:
        in_specs += [c_spec, n_spec, pl.BlockSpec(memory_space=pltpu.SMEM)]
        args += list(state)
    out_specs = [tile]
    out_shape = [jax.ShapeDtypeStruct((rows, D_MODEL), BF16)]
    if emit_state:
        out_specs += [c_spec, n_spec, pl.BlockSpec((nb, None, 2, ML_HEADS), lambda b: (b, 0, 0, 0))]
        out_shape += [jax.ShapeDtypeStruct((batch, 1, 2, ML_HEADS, d, d), F32),
                      jax.ShapeDtypeStruct((batch, 1, 2, ML_HEADS, d), F32),
                      jax.ShapeDtypeStruct((batch, 1, 2, ML_HEADS), F32)]
    n_state = 2 * ML_HEADS * nb
    return pl.pallas_call(
        functools.partial(_mlstm_kernel, seq_len=seq_len, nb=nb, has_state=state is not None,
                          emit_state=emit_state),
        grid=(batch // nb,),
        in_specs=in_specs,
        out_specs=out_specs,
        out_shape=out_shape,
        scratch_shapes=[pltpu.VMEM((n_state, d + ML_EXT, d), F32), pltpu.VMEM((n_state, 1, 1), F32),
                        pltpu.VMEM((nb * nc, GATE_PAD, CHUNK), F32),
                        pltpu.VMEM((2, nb * nc, D_MODEL, CHUNK), F32)],
        compiler_params=_params(1),
        name="mlstm",
    )(*args)


def _merge_kernel(*refs, tiles, mod_index):
    n = len(tiles)
    (x_ref, m_ref), refs = refs[:2], refs[2:]
    parts = [refs[j * n:(j + 1) * n] for j in range(4)]
    wm_ref, wd_ref, wo_ref, o_ref, w_sc = refs[4 * n:]
    i = pl.program_id(0)

    @pl.when(i == 0)
    def _():
        for j, w_ref in enumerate((wm_ref, wd_ref, wo_ref)):
            w_sc[j] = w_ref[...].astype(BF16)

    def tile(hm_ref, att_ref, sgm_ref, sgd_ref):
        y = sgm_ref[...] * _dot(hm_ref[...], w_sc[0]) + sgd_ref[...] * _dot(att_ref[...], w_sc[1])
        o_ref[...] = x_ref[...] + _mod_reader(m_ref, mod_index(i * MERGE_TM))(5) * _dot(y, w_sc[2])

    starts = np.cumsum((0,) + tiles).tolist()
    for k in range(n):
        pl.when((i >= starts[k]) & (i < starts[k + 1]))(functools.partial(tile, *[p[k] for p in parts]))


def _merge(x, mods, mod_index, hm, att, sgm, sgd, wm, wd, wo):
    rows = x.shape[0]
    tiles = tuple(h.shape[0] // MERGE_TM for h in hm)
    assert sum(tiles) * MERGE_TM == rows
    tile = pl.BlockSpec((MERGE_TM, D_MODEL), lambda i: (i, 0))

    def part_spec(k):
        start = sum(tiles[:k])
        return pl.BlockSpec((MERGE_TM, D_MODEL), lambda i: (jnp.clip(i - start, 0, tiles[k] - 1), 0))

    part_specs = [part_spec(k) for k in range(len(tiles))]
    w_spec = _resident((D_MODEL, D_MODEL))
    return pl.pallas_call(
        functools.partial(_merge_kernel, tiles=tiles, mod_index=mod_index),
        grid=(rows // MERGE_TM,),
        in_specs=[tile, _resident(mods.shape)]
                 + part_specs * 4 + [w_spec, w_spec, w_spec],
        out_specs=tile,
        out_shape=jax.ShapeDtypeStruct((rows, D_MODEL), F32),
        scratch_shapes=[pltpu.VMEM((3, D_MODEL, D_MODEL), BF16)],
        compiler_params=_params(1),
        name="branch_merge",
    )(x, mods, *hm, *att, *sgm, *sgd, wm, wd, wo)


def _rope_tables(seq_len):
    lane = np.arange(LANES)
    r = lane % ROPE_AXIS_DIM
    freqs = np.power(np.float32(ROPE_BASE), -(r % ROPE_HALF).astype(np.float32) / np.float32(ROPE_HALF))
    tok = np.arange(seq_len)
    by_row = (lane % DA_HEAD_DIM < ROPE_AXIS_DIM)[None, :]
    pos = np.where(by_row, (tok // GRID_W)[:, None], (tok % GRID_W)[:, None]).astype(np.float32)
    ang = pos * freqs[None, :]
    sign = np.where(r < ROPE_HALF, -1.0, 1.0).astype(np.float32)
    return jnp.asarray(np.cos(ang), F32), jnp.asarray(np.sin(ang) * sign[None, :], F32)


def _mixer_branches(x, row0, mods, mod_index, w, batch, seq_len, ctx, lam_init):
    rope_tabs = None if ctx is None else _rope_tables(seq_len)
    mq, mk, mv, so, dq, dk, dv, sgm, sgd, gates = _proj(
        x, row0, batch * seq_len, mods, mod_index, w["g_norm"], w["w_seg"], w["b_gate"], w["bd"], w["g_qn"],
        w["g_kn"], rope_tabs, seq_len)
    cache = None if ctx is None else (ctx[0], ctx[1])
    att = _attn(dq, dk, dv, cache, w["lam"], w["g_sub"], batch, seq_len, lam_init)
    state = None if ctx is None else ctx[2]
    res = _mlstm(mq, mk, mv, so, gates, w["g_mh"], state, batch, seq_len, emit_state=ctx is None)
    return res[0], att, sgm, sgd, dk, dv, res[1:]


def kernel(x_prompt, x_sample, c, cache_k, cache_v, state_C, state_n, state_m, c_ctx, w_ada, b_ada, g_norm, ffn1_w1, ffn1_w3, ffn1_w2, ffn2_w1, ffn2_w3, ffn2_w2, w_in, b_gate, g_qn, g_kn, lam_q1, lam_k1, lam_q2, lam_k2, g_sub, g_mh, w_br_m, w_br_d, w_out):
    depth = w_ada.shape[0]
    assert depth == 1
    l = 0
    bp, tp, _ = x_prompt.shape
    bs, ts, _ = x_sample.shape
    past = cache_k.shape[2]
    lam_init = 0.8 - 0.6 * math.exp(-0.3 * l)

    cvecs = jnp.concatenate([c_ctx[None, :], c, jnp.zeros((MOD_ROWS - 1 - bs, D_MODEL), F32)], axis=0)
    mods = _mods(cvecs, w_ada[l], b_ada[l])

    group = np.arange(MXU_DIM) // DA_HEAD_DIM
    w = dict(
        g_norm=g_norm[l],
        ffn1_w1=ffn1_w1[l], ffn1_w3=ffn1_w3[l], ffn1_w2=ffn1_w2[l],
        ffn2_w1=ffn2_w1[l], ffn2_w3=ffn2_w3[l], ffn2_w2=ffn2_w2[l],
        w_in_t=w_in[l].T,
        b_gate=b_gate[l:l + 1],
        bd=jnp.asarray(group[:, None] == group[None, :], BF16),
        g_qn=g_qn[l:l + 1], g_kn=g_kn[l:l + 1],
        lam=(lam_q1[l:l + 1], lam_k1[l:l + 1], lam_q2[l:l + 1], lam_k2[l:l + 1]),
        g_sub=g_sub[l:l + 1], g_mh=g_mh[l],
        w_br_m=w_br_m[l], w_br_d=w_br_d[l], w_out=w_out[l],
    )

    n_ctx, n_lat = bp * tp, bs * ts
    mod_index = lambda r: jnp.where(r < n_ctx, 0, 1 + (r - n_ctx) // ts)
    seg_start = lambda j: pl.multiple_of(jnp.where(j < N_SEG, _seg_start(j), GATE_LO), N_GATE_COLS)
    x1, w["w_seg"] = _ffn((x_prompt.reshape(n_ctx, D_MODEL), x_sample.reshape(n_lat, D_MODEL)), (n_ctx + n_lat,),
                          mods, mod_index, w["g_norm"], w["ffn1_w1"], w["ffn1_w3"], w["ffn1_w2"], base=0,
                          side=(w["w_in_t"], N_SEG + 1, seg_start))

    hm_p, att_p, sgm_p, sgd_p, new_k_t, new_v, (new_c, new_n, new_m) = _mixer_branches(
        x1, 0, mods, mod_index, w, bp, tp, None, lam_init)

    ctx = (cache_k[:, l].transpose(0, 2, 3, 4, 1).reshape(bs, D_MODEL, past),
           cache_v[:, l].reshape(bs, past * DA_HEADS, DA_V_DIM),
           (state_C, state_n, state_m))
    hm_s, att_s, sgm_s, sgd_s, _, _, _ = _mixer_branches(x1, n_ctx, mods, mod_index, w, bs, ts, ctx, lam_init)

    x2 = _merge(x1, mods, mod_index, (hm_p, hm_s), (att_p, att_s), (sgm_p, sgm_s), (sgd_p, sgd_s),
                w["w_br_m"], w["w_br_d"], w["w_out"])
    xp, xs = _ffn((x2,), (n_ctx, n_lat), mods, mod_index, w["g_norm"], w["ffn2_w1"], w["ffn2_w3"], w["ffn2_w2"],
                  base=6)

    return (xp.reshape(bp, tp, D_MODEL), xs.reshape(bs, ts, D_MODEL),
            new_k_t.reshape(bp, DA_HEADS, 2, DA_HEAD_DIM, tp).transpose(0, 4, 1, 2, 3)[:, None],
            new_v.reshape(bp, 1, tp, DA_HEADS, DA_V_DIM),
            new_c, new_n, new_m)
```

```python
import functools
import math

import jax
import jax.numpy as jnp
import numpy as np
from jax import lax
from jax.experimental import pallas as pl
from jax.experimental.pallas import tpu as pltpu

F32 = jnp.float32
BF16 = jnp.bfloat16

D_MODEL = 1024
D_FF = 2816
N_MOD = 9
GRID_W = 64
ML_HEADS = 4
ML_HEAD_DIM = 256
DA_HEADS = 8
DA_HEAD_DIM = 64
DA_V_DIM = 128
N_GATE_COLS = 16
CHUNK = 256
ROPE_BASE = 10000.0
QK_LOG2_SCALE = DA_HEAD_DIM ** -0.5 * math.log2(math.e)
EPS = 1e-6

LANES = 128
SUBLANES = 8
GATE_PAD = LANES
MXU_DIM = 256
VMEM_LIMIT = 56 * 1024 * 1024

SIDE_PIECES = 8
MOD_ROWS = SUBLANES
FFN_TM = 512
FFN_TF = 256
PROJ_TM = 256
MERGE_TM = 512
ATT_TQ = 256
ATT_TK_MAX = 768
ATT_SHORT_SEQ_BATCH = 2


def _params(n_axes):
    return pltpu.CompilerParams(dimension_semantics=("arbitrary",) * n_axes,
                                vmem_limit_bytes=VMEM_LIMIT)


def _dot(a, b):
    return jnp.dot(a.astype(BF16), b.astype(BF16), preferred_element_type=F32)


def _dot_nt(a, b):
    return lax.dot_general(a.astype(BF16), b.astype(BF16), (((1,), (1,)), ((), ())),
                           preferred_element_type=F32)


def _sigmoid(x):
    return 1.0 / (1.0 + jnp.exp(-x))


def _log_sigmoid(x):
    return jnp.minimum(x, 0.0) - jnp.log1p(jnp.exp(-jnp.abs(x)))


def _modulated_norm(x, g, shift, scale):
    y = x * lax.rsqrt(jnp.mean(x * x, axis=-1, keepdims=True) + EPS) * g
    return y * (1.0 + scale) + shift


def _resident(shape):
    return pl.BlockSpec(shape, lambda *_: (0,) * len(shape), pipeline_mode=pl.Buffered(1))


def _mods_kernel(c_ref, w_ref, b_ref, o_ref):
    c = c_ref[...]
    o_ref[...] = _dot(c * _sigmoid(c), w_ref[...]) + b_ref[...]


def _mods(cvecs, w_ada, b_ada):
    n = N_MOD * D_MODEL
    tn = D_MODEL
    return pl.pallas_call(
        _mods_kernel,
        grid=(n // tn,),
        in_specs=[pl.BlockSpec((MOD_ROWS, D_MODEL), lambda j: (0, 0)),
                  pl.BlockSpec((D_MODEL, tn), lambda j: (0, j)),
                  pl.BlockSpec((1, tn), lambda j: (0, j))],
        out_specs=pl.BlockSpec((MOD_ROWS, tn), lambda j: (0, j)),
        out_shape=jax.ShapeDtypeStruct((MOD_ROWS, n), F32),
        compiler_params=_params(1),
        name="adaln_mods",
    )(cvecs, w_ada, b_ada.reshape(1, n))


def _mod_reader(m_ref, row):
    return lambda j: m_ref[pl.ds(row, 1), j * D_MODEL:(j + 1) * D_MODEL]


def _ffn_kernel(*refs, base, mod_index, in_tiles, out_tiles, side_blocks):
    n_in, n_out = len(in_tiles), len(out_tiles)
    x_refs, refs = refs[:n_in], refs[n_in:]
    (m_ref, g_ref, w1_ref, w3_ref, w2_ref), refs = refs[:5], refs[5:]
    if side_blocks:
        side_in_ref, refs = refs[0], refs[1:]
        side_out_ref, refs = refs[n_out], refs[:n_out] + refs[n_out + 1:]
    o_refs, (w1_sc, w3_sc, w2_sc, hh_sc, acc_sc) = refs[:n_out], refs[n_out:]
    g = pl.program_id(0)
    nf = D_FF // FFN_TF
    row_tile = g - (nf - 1)
    mod = _mod_reader(m_ref, mod_index(jnp.maximum(row_tile, 0) * FFN_TM))
    gain = g_ref[base // 3:base // 3 + 1, :]
    norm = lambda x: _modulated_norm(x, gain, mod(base), mod(base + 1)).astype(BF16)
    finish = lambda x, acc: x + 0.5 * mod(base + 2) * acc

    def tile(hh, f):
        a = _dot(hh, w1_sc[f])
        b = _dot(hh, w3_sc[f])
        return _dot(a * _sigmoid(a) * b, w2_sc[f])

    @pl.when(g < nf)
    def _():
        w1_sc[g] = w1_ref[...].astype(BF16)
        w3_sc[g] = w3_ref[...].astype(BF16)
        w2_sc[g] = w2_ref[...].astype(BF16)

        @pl.when(g == 0)
        def _():
            hh_sc[...] = norm(x_refs[0][...])
            acc_sc[...] = jnp.zeros(acc_sc.shape, F32)

        acc_sc[...] += tile(hh_sc[...], g)

        @pl.when(g == nf - 1)
        def _():
            o_refs[0][...] = finish(x_refs[0][...], acc_sc[...])

    def full_tile(x_ref, o_ref):
        x = x_ref[...]
        hh = norm(x)
        acc = jnp.zeros(x.shape, F32)
        for f in range(nf):
            acc = acc + tile(hh, f)
            if side_blocks and f < SIDE_PIECES:
                rows = slice(f * (D_MODEL // SIDE_PIECES), (f + 1) * (D_MODEL // SIDE_PIECES))
                side_out_ref[rows, :] = side_in_ref[rows, :].astype(BF16)
        o_ref[...] = finish(x, acc)

    bounds = sorted(set(np.cumsum((0,) + in_tiles).tolist()) | set(np.cumsum((0,) + out_tiles).tolist()))
    for lo, hi in zip(bounds[:-1], bounds[1:]):
        k_in = int(np.searchsorted(np.cumsum(in_tiles), lo, side="right"))
        k_out = int(np.searchsorted(np.cumsum(out_tiles), lo, side="right"))
        pl.when((g >= nf) & (row_tile >= lo) & (row_tile < hi))(
            functools.partial(full_tile, x_refs[k_in], o_refs[k_out]))


def _ffn(xs, out_rows, mods, mod_index, g, w1, w3, w2, base, side=None):
    nf = D_FF // FFN_TF
    in_tiles = tuple(x.shape[0] // FFN_TM for x in xs)
    out_tiles = tuple(r // FFN_TM for r in out_rows)
    assert sum(in_tiles) == sum(out_tiles)
    side_in_specs, side_out_specs, side_out_shape, side_args, side_blocks = [], [], [], [], 0
    if side is not None:
        side_w, side_blocks, side_start = side
        assert side_blocks <= sum(in_tiles) - 1
        blk = lambda s: jnp.clip(s - nf, 0, side_blocks - 1)
        side_in_specs = [pl.BlockSpec((pl.Element(D_MODEL), pl.Element(D_MODEL)), lambda s: (side_start(blk(s)), 0))]
        side_out_specs = [pl.BlockSpec((None, D_MODEL, D_MODEL), lambda s: (blk(s), 0, 0))]
        side_out_shape = [jax.ShapeDtypeStruct((side_blocks, D_MODEL, D_MODEL), BF16)]
        side_args = [side_w]
    row_tile = lambda s: jnp.maximum(s - (nf - 1), 0)
    f_tile = lambda s: jnp.minimum(s, nf - 1)

    def part_spec(tiles, k):
        start = sum(tiles[:k])
        return pl.BlockSpec((FFN_TM, D_MODEL), lambda s: (jnp.clip(row_tile(s) - start, 0, tiles[k] - 1), 0))

    return pl.pallas_call(
        functools.partial(_ffn_kernel, base=base, mod_index=mod_index, in_tiles=in_tiles, out_tiles=out_tiles,
                          side_blocks=side_blocks),
        grid=(nf - 1 + sum(in_tiles),),
        in_specs=[part_spec(in_tiles, k) for k in range(len(xs))] + [
            _resident(mods.shape),
            _resident(g.shape),
            pl.BlockSpec((D_MODEL, FFN_TF), lambda s: (0, f_tile(s))),
            pl.BlockSpec((D_MODEL, FFN_TF), lambda s: (0, f_tile(s))),
            pl.BlockSpec((FFN_TF, D_MODEL), lambda s: (f_tile(s), 0))] + side_in_specs,
        out_specs=[part_spec(out_tiles, k) for k in range(len(out_rows))] + side_out_specs,
        out_shape=[jax.ShapeDtypeStruct((r, D_MODEL), F32) for r in out_rows] + side_out_shape,
        scratch_shapes=[pltpu.VMEM((nf, D_MODEL, FFN_TF), BF16), pltpu.VMEM((nf, D_MODEL, FFN_TF), BF16),
                        pltpu.VMEM((nf, FFN_TF, D_MODEL), BF16),
                        pltpu.VMEM((FFN_TM, D_MODEL), BF16), pltpu.VMEM((FFN_TM, D_MODEL), F32)],
        compiler_params=_params(1),
        name="ffn",
    )(*xs, mods, g, w1, w3, w2, *side_args)


def _group_norm64(x, bd, g):
    ss = _dot(x * x, bd)
    return x * lax.rsqrt(ss * (1.0 / DA_HEAD_DIM) + EPS) * g


ROPE_AXIS_DIM = DA_HEAD_DIM // 2
ROPE_HALF = ROPE_AXIS_DIM // 2


def _rope(x, cos, sin_signed):
    first = (lax.broadcasted_iota(jnp.int32, x.shape, 1) % ROPE_AXIS_DIM) < ROPE_HALF
    partner = jnp.where(first, pltpu.roll(x, LANES - ROPE_HALF, 1), pltpu.roll(x, ROPE_HALF, 1))
    return x * cos + partner * sin_signed


N_SEG = 9
PROJ_WARM = (N_SEG + 1) // 2
GATE_LO = 4 * D_MODEL


def _seg_start(s):
    return s * D_MODEL + N_GATE_COLS * (s >= 4)


def _proj_kernel(*refs, ctx_tiles, row_of_step):
    (x_ref, m_ref, g_ref, wt_ref, bg_ref, bd_ref, gq_ref, gk_ref, cos_ref, sin_ref), refs = refs[:10], refs[10:]
    (mq_ref, mk_ref, mv_ref, so_ref, dq_ref, sgm_ref, sgd_ref, gates_ref,
     dk_ctx_ref, dv_ctx_ref, dk_lat_ref, dv_lat_ref, w_sc, wg_sc, hh_sc) = refs
    step = pl.program_id(0)
    mod = _mod_reader(m_ref, row_of_step(step))
    norm = lambda: _modulated_norm(x_ref[...], g_ref[1:2, :], mod(3), mod(4)).astype(BF16)
    groups = MXU_DIM // DA_HEAD_DIM
    tiled = lambda ref, scale: jnp.concatenate([ref[...] * scale] * groups, axis=1)
    gq_row, gk_row = tiled(gq_ref, QK_LOG2_SCALE), tiled(gk_ref, 1.0)
    bias = jnp.concatenate([bg_ref[...], jnp.zeros((1, GATE_PAD - N_GATE_COLS), F32)], axis=1)

    def gates(hh):
        y = _dot_nt(hh, wg_sc[...]) + bias
        col = lax.broadcasted_iota(jnp.int32, y.shape, 1)
        gates_ref[...] = jnp.where((col // ML_HEADS) % 2 == 1, _log_sigmoid(y), y)

    def qk_segment(y, gain_row, o_ref, rope):
        bd = bd_ref[...]
        for c in range(D_MODEL // MXU_DIM):
            cols = slice(c * MXU_DIM, (c + 1) * MXU_DIM)
            z = _group_norm64(y[:, cols], bd, gain_row)
            if rope:
                z = jnp.concatenate(
                    [_rope(z[:, k * LANES:(k + 1) * LANES], cos_ref[...], sin_ref[...])
                     for k in range(MXU_DIM // LANES)], axis=1)
            o_ref[:, cols] = z.astype(o_ref.dtype)

    def cache_key_segment(hh):
        y_t = _dot_nt(w_sc[5], hh)
        z = y_t.reshape(D_MODEL // DA_HEAD_DIM, DA_HEAD_DIM, y_t.shape[1])
        ms = jnp.mean(z * z, axis=1, keepdims=True)
        gain = jnp.broadcast_to(gk_row[:, 0:LANES], (LANES, LANES)).T[0:DA_HEAD_DIM, :]
        gain = jnp.concatenate([gain] * (y_t.shape[1] // LANES), axis=1)
        dk_ctx_ref[...] = (z * lax.rsqrt(ms + EPS) * gain).reshape(y_t.shape)

    def segment(s, hh, ctx):
        if s in (0, 2):
            o_ref = {0: mq_ref, 2: mv_ref}[s]
            y_t = _dot_nt(w_sc[s], hh) * ((ML_HEAD_DIM ** -0.5) if s == 0 else 1.0)
            for j in range(PROJ_TM // CHUNK):
                o_ref[j] = y_t[:, j * CHUNK:(j + 1) * CHUNK].astype(o_ref.dtype)
            return
        if s == 5 and ctx:
            cache_key_segment(hh)
            return
        y = _dot_nt(hh, w_sc[s])
        if s == 4:
            qk_segment(y, gq_row, dq_ref, rope=not ctx)
        elif s == 5:
            qk_segment(y, gk_row, dk_lat_ref, rope=True)
        elif s in (3, 7, 8):
            {3: so_ref, 7: sgm_ref, 8: sgd_ref}[s][...] = _sigmoid(y)
        elif s == 6 and ctx:
            dv_ctx_ref[...] = y.reshape(y.shape[0], DA_HEADS, DA_V_DIM)
        elif s == 6:
            dv_lat_ref[...] = y.astype(dv_lat_ref.dtype)
        else:
            mk_ref[...] = y.astype(mk_ref.dtype)

    for w in range(PROJ_WARM):
        @pl.when(step == w)
        def _(w=w):
            if w == 0:
                hh_sc[...] = norm()
            for j in range(2):
                s = 2 * w + j
                if s < N_SEG:
                    w_sc[s] = wt_ref[j]
                    segment(s, hh_sc[...], ctx=True)
                else:
                    wg_sc[...] = jnp.zeros(wg_sc.shape, BF16)
                    wg_sc[0:N_GATE_COLS, :] = wt_ref[j, 0:N_GATE_COLS, :]
                    gates(hh_sc[...])

    def full_tile(ctx):
        hh = norm()
        gates(hh)
        for s in range(N_SEG):
            segment(s, hh, ctx)

    is_ctx = step - (PROJ_WARM - 1) < ctx_tiles
    pl.when((step >= PROJ_WARM) & is_ctx)(functools.partial(full_tile, True))
    pl.when((step >= PROJ_WARM) & jnp.logical_not(is_ctx))(functools.partial(full_tile, False))


def _proj(x, n_ctx, seq_ctx, seq_lat, mods, mod_index, g, w_seg, b_gate, bd, gq, gk, rope_tabs):
    rows = x.shape[0]
    n_lat = rows - n_ctx
    assert seq_ctx == PROJ_TM
    ctx_tiles, lat_tiles = n_ctx // PROJ_TM, n_lat // PROJ_TM
    row_tile = lambda s: jnp.maximum(s - (PROJ_WARM - 1), 0)
    ctx_tile = lambda s: jnp.minimum(row_tile(s), ctx_tiles - 1)
    lat_tile = lambda s: jnp.clip(row_tile(s) - ctx_tiles, 0, lat_tiles - 1)
    row = lambda s: (row_tile(s), 0)
    tile = pl.BlockSpec((PROJ_TM, D_MODEL), row)
    w_block = (2, D_MODEL, D_MODEL)
    tab = pl.BlockSpec((PROJ_TM, LANES), lambda s: (lat_tile(s) % (seq_lat // PROJ_TM), 0))
    in_specs = [tile,
                _resident(mods.shape),
                _resident(g.shape),
                pl.BlockSpec(w_block, lambda s: (jnp.minimum(s, PROJ_WARM - 1), 0, 0)),
                _resident(b_gate.shape),
                _resident((MXU_DIM, MXU_DIM)),
                _resident(gq.shape),
                _resident(gk.shape),
                tab, tab]
    slab = pl.BlockSpec((PROJ_TM // CHUNK, D_MODEL, CHUNK), lambda s: (row_tile(s), 0, 0))
    slab_shape = jax.ShapeDtypeStruct((rows // CHUNK, D_MODEL, CHUNK), BF16)
    full = lambda dt: jax.ShapeDtypeStruct((rows, D_MODEL), dt)
    out_specs = [slab, tile, slab, tile, tile, tile, tile, pl.BlockSpec((PROJ_TM, GATE_PAD), row),
                 pl.BlockSpec((None, D_MODEL, seq_ctx), lambda s: (ctx_tile(s), 0, 0)),
                 pl.BlockSpec((PROJ_TM, DA_HEADS, DA_V_DIM), lambda s: (ctx_tile(s), 0, 0)),
                 pl.BlockSpec((PROJ_TM, D_MODEL), lambda s: (lat_tile(s), 0)),
                 pl.BlockSpec((PROJ_TM, D_MODEL), lambda s: (lat_tile(s), 0))]
    out_shape = [slab_shape, full(BF16), slab_shape, full(F32), full(BF16), full(F32), full(F32),
                 jax.ShapeDtypeStruct((rows, GATE_PAD), F32),
                 jax.ShapeDtypeStruct((n_ctx // seq_ctx, D_MODEL, seq_ctx), F32),
                 jax.ShapeDtypeStruct((n_ctx, DA_HEADS, DA_V_DIM), F32),
                 jax.ShapeDtypeStruct((n_lat, D_MODEL), BF16),
                 jax.ShapeDtypeStruct((n_lat, D_MODEL), BF16)]
    return pl.pallas_call(
        functools.partial(_proj_kernel, ctx_tiles=ctx_tiles,
                          row_of_step=lambda step: mod_index(row_tile(step) * PROJ_TM)),
        grid=(PROJ_WARM - 1 + rows // PROJ_TM,),
        in_specs=in_specs,
        out_specs=out_specs,
        out_shape=out_shape,
        scratch_shapes=[pltpu.VMEM((N_SEG, D_MODEL, D_MODEL), BF16), pltpu.VMEM((GATE_PAD, D_MODEL), BF16),
                        pltpu.VMEM((PROJ_TM, D_MODEL), BF16)],
        compiler_params=_params(1),
        name="mixer_in_proj",
    )(x, mods, g, w_seg, b_gate, bd, gq, gk, *rope_tabs)


def _lambda(lam_refs, lam_init):
    q1, k1, q2, k2 = (r[...] for r in lam_refs)
    s1 = jnp.sum(q1 * k1, axis=1, keepdims=True)
    s2 = jnp.sum(q2 * k2, axis=1, keepdims=True)
    return jnp.exp(s1) - jnp.exp(s2) + lam_init


def _attn_kernel(*refs, cached, seq, nb, lam_init):
    if cached:
        (q_ref, k_ref, v_ref, ck_ref, cv_ref), refs = refs[:5], refs[5:]
    else:
        (q_ref, k_ref, v_ref), refs = refs[:3], refs[3:]
    lam_refs, (gs_ref, o_ref, kall_sc, vt_sc, s_sc) = refs[:4], refs[4:]
    units = [(bb, h) for bb in range(nb) for h in range(DA_HEADS)]

    def cache_order_kv(kt_ref, vr_ref, u, h, lo, n):
        kall_sc[u, lo:lo + n, :] = kt_ref[h * DA_V_DIM:(h + 1) * DA_V_DIM, :].T.astype(BF16)
        vt_sc[u, :, lo:lo + n] = vr_ref[pl.ds(h, n, stride=DA_HEADS), :].T.astype(BF16)

    @pl.when(pl.program_id(1) == 0)
    def _():
        for u, (bb, h) in enumerate(units):
            if cached:
                cols = slice(h * DA_V_DIM, (h + 1) * DA_V_DIM)
                kall_sc[u, 0:seq, :] = k_ref[:, cols].astype(BF16)
                vt_sc[u, :, 0:seq] = v_ref[:, cols].astype(F32).T.astype(BF16)
                cache_order_kv(ck_ref.at[bb], cv_ref.at[bb], u, h, seq, ck_ref.shape[2])
            else:
                cache_order_kv(k_ref.at[bb], v_ref.at[bb], u, h, 0, seq)

    lam = _lambda(lam_refs, lam_init)
    sub_gain = gs_ref[...] * (1.0 - lam_init)
    lane = lax.broadcasted_iota(jnp.int32, (1, DA_V_DIM), 1)
    comp_masks = [lane < DA_HEAD_DIM, lane >= DA_HEAD_DIM]
    tq = q_ref.shape[0] // nb
    n_keys = kall_sc.shape[1]
    n_tiles = pl.cdiv(n_keys, ATT_TK_MAX)
    tk = n_keys // n_tiles

    def stacked_q(u):
        bb, h = units[u]
        q = q_ref[bb * tq:(bb + 1) * tq, h * DA_V_DIM:(h + 1) * DA_V_DIM].astype(BF16)
        return jnp.concatenate([jnp.where(m, q, jnp.zeros_like(q)) for m in comp_masks], axis=0)

    def score_tile(h, j, qq, m8):
        rows = slice(j * tk, (j + 1) * tk)
        st = _dot_nt(kall_sc[h, rows, :], qq)
        s_sc[h % 2, rows, :] = st
        t8 = jnp.max(st.reshape(tk // SUBLANES, SUBLANES, 2 * tq), axis=0)
        return t8 if m8 is None else jnp.maximum(m8, t8)

    def prob_tile(h, j, mx, d8, pv):
        rows = slice(j * tk, (j + 1) * tk)
        e = jnp.exp2(s_sc[h % 2, rows, :] - mx)
        s8 = jnp.sum(e.reshape(tk // SUBLANES, SUBLANES, 2 * tq), axis=0)
        p = _dot(vt_sc[h, :, rows], e)
        return (s8 if d8 is None else d8 + s8), (p if pv is None else pv + p)

    qq = stacked_q(0)
    m8 = None
    for j in range(n_tiles):
        m8 = score_tile(0, j, qq, m8)
    for h in range(len(units)):
        mx = jnp.max(m8, axis=0, keepdims=True)
        if h + 1 < len(units):
            qq = stacked_q(h + 1)
        m8, d8, pv = None, None, None
        for j in range(n_tiles):
            if h + 1 < len(units):
                m8 = score_tile(h + 1, j, qq, m8)
            d8, pv = prob_tile(h, j, mx, d8, pv)
        inv = 1.0 / jnp.sum(d8, axis=0, keepdims=True)
        out_t = pv[:, :tq] * inv[:, :tq] - pv[:, tq:] * (lam * inv[:, tq:])
        out_t = out_t * lax.rsqrt(jnp.mean(out_t * out_t, axis=0, keepdims=True) + EPS)
        bb, head = units[h]
        o_ref[bb * tq:(bb + 1) * tq, head * DA_V_DIM:(head + 1) * DA_V_DIM] = (out_t.T * sub_gain).astype(o_ref.dtype)


def _attn(q, q_row0, k, v, cache, lam_vecs, g_sub, batch, seq_len, lam_init):
    rows = batch * seq_len
    nq = seq_len // ATT_TQ
    nb = ATT_SHORT_SEQ_BATCH if (cache is None and nq == 1) else 1
    assert batch % nb == 0
    q_spec = pl.BlockSpec((nb * ATT_TQ, D_MODEL), lambda b, i: (q_row0 // (nb * ATT_TQ) + b * nq + i, 0))
    o_spec = pl.BlockSpec((nb * ATT_TQ, D_MODEL), lambda b, i: (b * nq + i, 0))
    cache_specs = lambda n: [pl.BlockSpec((nb, D_MODEL, n), lambda b, i: (b, 0, 0)),
                             pl.BlockSpec((nb, n * DA_HEADS, DA_V_DIM), lambda b, i: (b, 0, 0))]
    n_keys = seq_len
    if cache is not None:
        past = cache[0].shape[2]
        n_keys += past
        kv_spec = pl.BlockSpec((seq_len, D_MODEL), lambda b, i: (b, 0))
        in_specs = [q_spec, kv_spec, kv_spec] + cache_specs(past)
        args = [q, k, v] + list(cache)
    else:
        in_specs = [q_spec] + cache_specs(seq_len)
        args = [q, k, v.reshape(batch, seq_len * DA_HEADS, DA_V_DIM)]
    in_specs += [_resident((1, DA_HEAD_DIM))] * 4 + [_resident((1, DA_V_DIM))]
    args += list(lam_vecs) + [g_sub]
    return pl.pallas_call(
        functools.partial(_attn_kernel, cached=cache is not None, seq=seq_len, nb=nb, lam_init=lam_init),
        grid=(batch // nb, nq),
        in_specs=in_specs,
        out_specs=o_spec,
        out_shape=jax.ShapeDtypeStruct((rows, D_MODEL), BF16),
        scratch_shapes=[pltpu.VMEM((nb * DA_HEADS, n_keys, DA_V_DIM), BF16),
                        pltpu.VMEM((nb * DA_HEADS, DA_V_DIM, n_keys), BF16),
                        pltpu.VMEM((2, n_keys, 2 * ATT_TQ), F32)],
        compiler_params=_params(2),
        name="diff_attention",
    )(*args)


def _per_chain(fn, a, b):
    return jnp.stack([fn(a[i], b[i]) for i in range(a.shape[0])])


ML_EXT = SUBLANES
ML_SHORT_SEQ_BATCH = 2


def _rows(x):
    return jnp.stack([x[i:i + 1, :] for i in range(x.shape[0])])


def _split3(x):
    hi = x.astype(BF16)
    r = x - hi.astype(F32)
    mid = r.astype(BF16)
    lo = (r - mid.astype(F32)).astype(BF16)
    return jnp.concatenate([hi, mid, lo], axis=1)


def _chunk_scan(x, reverse_rows, tri_prefix, tri_suffix):
    parts = _split3(x)
    return jnp.where(reverse_rows, _dot(parts, tri_suffix), _dot(parts, tri_prefix))


def _paired_value_matmul(v_t, sc):
    B, L, _ = sc.shape
    zero = jnp.zeros((L, L), sc.dtype)
    out = []
    for i in range(0, B, 2):
        lhs = jnp.concatenate([v_t[i], v_t[i + 1]], axis=1)
        rhs = jnp.concatenate([jnp.concatenate([sc[i], zero], axis=1),
                               jnp.concatenate([zero, sc[i + 1]], axis=1)], axis=0)
        both = _dot(lhs, rhs)
        out += [both[:, :L], both[:, L:]]
    return jnp.stack(out)


def _mlstm_step(k, q_t, v_t, ic, fc, CT, m, seen_t, reverse_rows):
    B, L = ic.shape
    d = k.shape[2]
    zero_state = CT is None
    if zero_state:
        m = jnp.zeros((B, 1, 1), F32)
    tri = lambda keep: jnp.concatenate([jnp.where(keep, 1.0, 0.0).astype(BF16)] * 3, axis=0)
    b2 = _chunk_scan(fc, reverse_rows, tri(seen_t[0]), tri(seen_t[B - 1]))
    u2 = ic - b2
    u_t = jnp.concatenate([u2, jnp.zeros((L - B, L), F32)], axis=0).T
    u_col = jnp.stack([jnp.broadcast_to(u_t[:, i:i + 1], (L, L)) for i in range(B)])
    b, i_g, f_g = _rows(b2), _rows(ic), _rows(fc)
    b_last = jnp.sum(f_g, axis=2, keepdims=True)

    log_d = jnp.where(seen_t, b + u_col, -jnp.inf)
    a = b + m
    m_t = jnp.maximum(a, jnp.max(log_d, axis=1, keepdims=True))
    dmat = jnp.exp(log_d - m_t)
    sc = _per_chain(_dot, k, q_t) * dmat
    value_matmul = _paired_value_matmul if 2 * L <= MXU_DIM else functools.partial(_per_chain, _dot)
    num = value_matmul(v_t, sc.astype(BF16))
    den = jnp.sum(sc, axis=1, keepdims=True)
    if not zero_state:
        inter = jnp.exp(a - m_t)
        cq = _per_chain(_dot, CT, q_t)
        num = num + inter * cq[:, :d, :]
        den = den + inter * cq[:, d:d + 1, :]
    h_t = num * (1.0 / jnp.maximum(jnp.abs(den), jnp.exp(-m_t)))

    g = b_last - b + i_g
    m_new = jnp.maximum(b_last + m, jnp.max(g, axis=2, keepdims=True))
    w = jnp.exp(g - m_new)
    decay = jnp.exp(b_last + m - m_new)
    vw = jnp.concatenate([v_t.astype(F32), jnp.ones((B, ML_EXT, L), F32)], axis=1) * w
    CT_new = _per_chain(_dot, vw, k)
    if not zero_state:
        CT_new = decay * CT + CT_new
    return h_t, CT_new, m_new


def _mlstm_kernel(*refs, seq_len, nb, has_state, emit_state):
    q_ref, k_ref, v_ref, so_ref, g_ref, gmh_ref = refs[:6]
    refs = refs[6:]
    if has_state:
        (c0_ref, n0_ref, m0_ref), refs = refs[:3], refs[3:]
    hm_ref, refs = refs[0], refs[1:]
    if emit_state:
        (c_out_ref, n_out_ref, m_out_ref), refs = refs[:3], refs[3:]
    ct_sc, m_sc, gr_sc, h_sc = refs

    nc = seq_len // CHUNK
    d_head = ML_HEAD_DIM
    chains = [(bb, d, h) for bb in range(nb) for d in range(2) for h in range(ML_HEADS)]
    n_chain = len(chains)
    for i, (bb, d, h) in enumerate(chains):
        if has_state:
            ct_sc[i, 0:d_head, :] = c0_ref[bb, d, h].T
            ct_sc[i, d_head:d_head + ML_EXT, :] = jnp.broadcast_to(n0_ref[bb, d, h:h + 1, :], (ML_EXT, d_head))
            m_sc[i] = jnp.full((1, 1), m0_ref[pl.program_id(0) * nb + bb, 0, d, h], F32)
        elif nc > 2:
            ct_sc[i] = jnp.zeros((d_head + ML_EXT, d_head), F32)
            m_sc[i] = jnp.zeros((1, 1), F32)
    for c in range(nb * nc):
        gr_sc[c] = g_ref[c * CHUNK:(c + 1) * CHUNK, :].T

    s_idx = lax.broadcasted_iota(jnp.int32, (CHUNK, CHUNK), 0)
    t_idx = lax.broadcasted_iota(jnp.int32, (CHUNK, CHUNK), 1)
    seen_t = jnp.stack([(s_idx >= t_idx) if d else (s_idx <= t_idx) for _, d, _ in chains])
    reverse_rows = (lax.broadcasted_iota(jnp.int32, (n_chain, CHUNK), 0) // ML_HEADS) % 2 == 1

    def step(c_fwd, c_bwd, rows_of, zero_state=False):
        chunk_of = (c_fwd, c_bwd)
        lo = 2 * ML_HEADS
        ic, fc = [], []
        for bb in range(nb):
            g_fwd, g_bwd = gr_sc[bb * nc + c_fwd], gr_sc[bb * nc + c_bwd]
            ic += [g_fwd[0:ML_HEADS], g_bwd[lo:lo + ML_HEADS]]
            fc += [g_fwd[ML_HEADS:lo], g_bwd[lo + ML_HEADS:2 * lo]]
        head = [slice(h * d_head, (h + 1) * d_head) for _, _, h in chains]
        stack = lambda pick: jnp.stack([pick(i, bb * nc + chunk_of[d]) for i, (bb, d, _) in enumerate(chains)])
        h_t, CT_new, m_new = _mlstm_step(
            stack(lambda i, c: k_ref[rows_of(c), head[i]]),
            stack(lambda i, c: q_ref[c, head[i], :]),
            stack(lambda i, c: v_ref[c, head[i], :]),
            jnp.concatenate(ic, axis=0), jnp.concatenate(fc, axis=0),
            None if zero_state else ct_sc[...], None if zero_state else m_sc[...], seen_t, reverse_rows)
        ct_sc[...] = CT_new
        m_sc[...] = m_new
        for i, (bb, d, _) in enumerate(chains):
            h_sc[d, bb * nc + chunk_of[d], head[i], :] = h_t[i]

    if nc <= 2:
        for c in range(nc):
            step(c, nc - 1 - c, lambda cc: slice(cc * CHUNK, (cc + 1) * CHUNK),
                 zero_state=(c == 0 and not has_state))
    else:
        def body(c, carry):
            step(c, nc - 1 - c, lambda cc: pl.ds(pl.multiple_of(cc * CHUNK, CHUNK), CHUNK))
            return carry
        lax.fori_loop(0, nc, body, 0, unroll=2)

    for h in range(ML_HEADS):
        hcols = slice(h * d_head, (h + 1) * d_head)
        gain = jnp.broadcast_to(gmh_ref[h:h + 1, :], (CHUNK, d_head)).T
        for c in range(nb * nc):
            rows = slice(c * CHUNK, (c + 1) * CHUNK)
            hsum = h_sc[0, c, hcols, :] + h_sc[1, c, hcols, :]
            hn = hsum * lax.rsqrt(jnp.mean(hsum * hsum, axis=0, keepdims=True) + EPS) * gain
            hm_ref[rows, hcols] = (hn.T * so_ref[rows, hcols]).astype(hm_ref.dtype)
    if emit_state:
        for i, (bb, d, h) in enumerate(chains):
            c_out_ref[bb, d, h] = ct_sc[i, 0:d_head, :].T
            n_out_ref[bb, d, h:h + 1, :] = ct_sc[i, d_head:d_head + 1, :]
            m_out_ref[bb, d:d + 1, h:h + 1] = m_sc[i]


def _mlstm(q_t, k, v_t, so, gates, row0, g_mh, state, batch, seq_len, emit_state):
    rows = batch * seq_len
    d = ML_HEAD_DIM
    nc = seq_len // CHUNK
    nb = ML_SHORT_SEQ_BATCH if nc <= 2 else 1
    assert batch % nb == 0
    assert row0 % (nb * seq_len) == 0
    first = row0 // (nb * seq_len)
    tile = pl.BlockSpec((nb * seq_len, D_MODEL), lambda b: (first + b, 0))
    tile_t = pl.BlockSpec((nb * nc, D_MODEL, CHUNK), lambda b: (first + b, 0, 0))
    in_specs = [tile_t, tile, tile_t, tile,
                pl.BlockSpec((nb * seq_len, GATE_PAD), lambda b: (first + b, 0)),
                _resident((ML_HEADS, d))]
    args = [q_t, k, v_t, so, gates, g_mh]
    c_spec = pl.BlockSpec((nb, None, 2, ML_HEADS, d, d), lambda b: (b, 0, 0, 0, 0, 0))
    n_spec = pl.BlockSpec((nb, None, 2, ML_HEADS, d), lambda b: (b, 0, 0, 0, 0))
    if state is not None:
        in_specs += [c_spec, n_spec, pl.BlockSpec(memory_space=pltpu.SMEM)]
        args += list(state)
    out_specs = [pl.BlockSpec((nb * seq_len, D_MODEL), lambda b: (b, 0))]
    out_shape = [jax.ShapeDtypeStruct((rows, D_MODEL), BF16)]
    if emit_state:
        out_specs += [c_spec, n_spec, pl.BlockSpec((nb, None, 2, ML_HEADS), lambda b: (b, 0, 0, 0))]
        out_shape += [jax.ShapeDtypeStruct((batch, 1, 2, ML_HEADS, d, d), F32),
                      jax.ShapeDtypeStruct((batch, 1, 2, ML_HEADS, d), F32),
                      jax.ShapeDtypeStruct((batch, 1, 2, ML_HEADS), F32)]
    n_state = 2 * ML_HEADS * nb
    return pl.pallas_call(
        functools.partial(_mlstm_kernel, seq_len=seq_len, nb=nb, has_state=state is not None,
                          emit_state=emit_state),
        grid=(batch // nb,),
        in_specs=in_specs,
        out_specs=out_specs,
        out_shape=out_shape,
        scratch_shapes=[pltpu.VMEM((n_state, d + ML_EXT, d), F32), pltpu.VMEM((n_state, 1, 1), F32),
                        pltpu.VMEM((nb * nc, GATE_PAD, CHUNK), F32),
                        pltpu.VMEM((2, nb * nc, D_MODEL, CHUNK), F32)],
        compiler_params=_params(1),
        name="mlstm",
    )(*args)


def _merge_kernel(*refs, tiles, mod_index):
    n = len(tiles)
    (x_ref, m_ref, sgm_ref, sgd_ref), refs = refs[:4], refs[4:]
    parts = [refs[j * n:(j + 1) * n] for j in range(2)]
    wm_ref, wd_ref, wo_ref, o_ref, w_sc = refs[2 * n:]
    i = pl.program_id(0)

    @pl.when(i == 0)
    def _():
        for j, w_ref in enumerate((wm_ref, wd_ref, wo_ref)):
            w_sc[j] = w_ref[...].astype(BF16)

    def tile(hm_ref, att_ref):
        y = sgm_ref[...] * _dot(hm_ref[...], w_sc[0]) + sgd_ref[...] * _dot(att_ref[...], w_sc[1])
        o_ref[...] = x_ref[...] + _mod_reader(m_ref, mod_index(i * MERGE_TM))(5) * _dot(y, w_sc[2])

    starts = np.cumsum((0,) + tiles).tolist()
    for k in range(n):
        pl.when((i >= starts[k]) & (i < starts[k + 1]))(functools.partial(tile, *[p[k] for p in parts]))


def _merge(x, mods, mod_index, hm, att, sgm, sgd, wm, wd, wo):
    rows = x.shape[0]
    tiles = tuple(h.shape[0] // MERGE_TM for h in hm)
    assert sum(tiles) * MERGE_TM == rows
    tile = pl.BlockSpec((MERGE_TM, D_MODEL), lambda i: (i, 0))

    def part_spec(k):
        start = sum(tiles[:k])
        return pl.BlockSpec((MERGE_TM, D_MODEL), lambda i: (jnp.clip(i - start, 0, tiles[k] - 1), 0))

    part_specs = [part_spec(k) for k in range(len(tiles))]
    w_spec = _resident((D_MODEL, D_MODEL))
    return pl.pallas_call(
        functools.partial(_merge_kernel, tiles=tiles, mod_index=mod_index),
        grid=(rows // MERGE_TM,),
        in_specs=[tile, _resident(mods.shape), tile, tile]
                 + part_specs * 2 + [w_spec, w_spec, w_spec],
        out_specs=tile,
        out_shape=jax.ShapeDtypeStruct((rows, D_MODEL), F32),
        scratch_shapes=[pltpu.VMEM((3, D_MODEL, D_MODEL), BF16)],
        compiler_params=_params(1),
        name="branch_merge",
    )(x, mods, sgm, sgd, *hm, *att, wm, wd, wo)


def _rope_tables(seq_len):
    lane = np.arange(LANES)
    r = lane % ROPE_AXIS_DIM
    freqs = np.power(np.float32(ROPE_BASE), -(r % ROPE_HALF).astype(np.float32) / np.float32(ROPE_HALF))
    tok = np.arange(seq_len)
    by_row = (lane % DA_HEAD_DIM < ROPE_AXIS_DIM)[None, :]
    pos = np.where(by_row, (tok // GRID_W)[:, None], (tok % GRID_W)[:, None]).astype(np.float32)
    ang = pos * freqs[None, :]
    sign = np.where(r < ROPE_HALF, -1.0, 1.0).astype(np.float32)
    return jnp.asarray(np.cos(ang), F32), jnp.asarray(np.sin(ang) * sign[None, :], F32)


def kernel(x_prompt, x_sample, c, cache_k, cache_v, state_C, state_n, state_m, c_ctx, w_ada, b_ada, g_norm, ffn1_w1, ffn1_w3, ffn1_w2, ffn2_w1, ffn2_w3, ffn2_w2, w_in, b_gate, g_qn, g_kn, lam_q1, lam_k1, lam_q2, lam_k2, g_sub, g_mh, w_br_m, w_br_d, w_out):
    depth = w_ada.shape[0]
    assert depth == 1
    l = 0
    bp, tp, _ = x_prompt.shape
    bs, ts, _ = x_sample.shape
    past = cache_k.shape[2]
    lam_init = 0.8 - 0.6 * math.exp(-0.3 * l)

    cvecs = jnp.concatenate([c_ctx[None, :], c, jnp.zeros((MOD_ROWS - 1 - bs, D_MODEL), F32)], axis=0)
    mods = _mods(cvecs, w_ada[l], b_ada[l])

    group = np.arange(MXU_DIM) // DA_HEAD_DIM
    w = dict(
        g_norm=g_norm[l],
        ffn1_w1=ffn1_w1[l], ffn1_w3=ffn1_w3[l], ffn1_w2=ffn1_w2[l],
        ffn2_w1=ffn2_w1[l], ffn2_w3=ffn2_w3[l], ffn2_w2=ffn2_w2[l],
        w_in_t=w_in[l].T,
        b_gate=b_gate[l:l + 1],
        bd=jnp.asarray(group[:, None] == group[None, :], BF16),
        g_qn=g_qn[l:l + 1], g_kn=g_kn[l:l + 1],
        lam=(lam_q1[l:l + 1], lam_k1[l:l + 1], lam_q2[l:l + 1], lam_k2[l:l + 1]),
        g_sub=g_sub[l:l + 1], g_mh=g_mh[l],
        w_br_m=w_br_m[l], w_br_d=w_br_d[l], w_out=w_out[l],
    )

    n_ctx, n_lat = bp * tp, bs * ts
    mod_index = lambda r: jnp.where(r < n_ctx, 0, 1 + (r - n_ctx) // ts)
    seg_start = lambda j: pl.multiple_of(jnp.where(j < N_SEG, _seg_start(j), GATE_LO), N_GATE_COLS)
    x1, w["w_seg"] = _ffn((x_prompt.reshape(n_ctx, D_MODEL), x_sample.reshape(n_lat, D_MODEL)), (n_ctx + n_lat,),
                          mods, mod_index, w["g_norm"], w["ffn1_w1"], w["ffn1_w3"], w["ffn1_w2"], base=0,
                          side=(w["w_in_t"], N_SEG + 1, seg_start))

    mq, mk, mv, so, dq, sgm, sgd, gates, new_k_t, new_v, dk_lat, dv_lat = _proj(
        x1, n_ctx, tp, ts, mods, mod_index, w["g_norm"], w["w_seg"], w["b_gate"], w["bd"], w["g_qn"], w["g_kn"],
        _rope_tables(ts))
    att_p = _attn(dq, 0, new_k_t, new_v, None, w["lam"], w["g_sub"], bp, tp, lam_init)
    hm_p, new_c, new_n, new_m = _mlstm(mq, mk, mv, so, gates, 0, w["g_mh"], None, bp, tp, emit_state=True)

    cache = (cache_k[:, l].transpose(0, 2, 3, 4, 1).reshape(bs, D_MODEL, past),
             cache_v[:, l].reshape(bs, past * DA_HEADS, DA_V_DIM))
    att_s = _attn(dq, n_ctx, dk_lat, dv_lat, cache, w["lam"], w["g_sub"], bs, ts, lam_init)
    (hm_s,) = _mlstm(mq, mk, mv, so, gates, n_ctx, w["g_mh"], (state_C, state_n, state_m), bs, ts,
                     emit_state=False)

    x2 = _merge(x1, mods, mod_index, (hm_p, hm_s), (att_p, att_s), sgm, sgd, w["w_br_m"], w["w_br_d"], w["w_out"])
    xp, xs = _ffn((x2,), (n_ctx, n_lat), mods, mod_index, w["g_norm"], w["ffn2_w1"], w["ffn2_w3"], w["ffn2_w2"],
                  base=6)

    return (xp.reshape(bp, tp, D_MODEL), xs.reshape(bs, ts, D_MODEL),
            new_k_t.reshape(bp, DA_HEADS, 2, DA_HEAD_DIM, tp).transpose(0, 4, 1, 2, 3)[:, None],
            new_v.reshape(bp, 1, tp, DA_HEADS, DA_V_DIM),
            new_c, new_n, new_m)
```

```python
import functools
import math

import jax
import jax.numpy as jnp
import numpy as np
from jax import lax
from jax.experimental import pallas as pl
from jax.experimental.pallas import tpu as pltpu

F32 = jnp.float32
BF16 = jnp.bfloat16

D_MODEL = 1024
D_FF = 2816
N_MOD = 9
GRID_W = 64
ML_HEADS = 4
ML_HEAD_DIM = 256
DA_HEADS = 8
DA_HEAD_DIM = 64
DA_V_DIM = 128
N_GATE_COLS = 16
CHUNK = 256
ROPE_BASE = 10000.0
QK_LOG2_SCALE = DA_HEAD_DIM ** -0.5 * math.log2(math.e)
EPS = 1e-6

LANES = 128
SUBLANES = 8
GATE_PAD = LANES
MXU_DIM = 256
VMEM_LIMIT = 56 * 1024 * 1024

SIDE_PIECES = 8
MODS_TN = 2304
MOD_ROWS = SUBLANES
FFN_TM = 512
FFN_TF = 256
PROJ_TM = 256
MERGE_TM = 512
ATT_TQ = 256
ATT_TK_MAX = 768
ATT_SHORT_SEQ_BATCH = 2


def _params(n_axes):
    return pltpu.CompilerParams(dimension_semantics=("arbitrary",) * n_axes,
                                vmem_limit_bytes=VMEM_LIMIT)


def _dot(a, b):
    return jnp.dot(a.astype(BF16), b.astype(BF16), preferred_element_type=F32)


def _dot_nt(a, b):
    return lax.dot_general(a.astype(BF16), b.astype(BF16), (((1,), (1,)), ((), ())),
                           preferred_element_type=F32)


def _sigmoid(x):
    return 1.0 / (1.0 + jnp.exp(-x))


def _log_sigmoid(x):
    return jnp.minimum(x, 0.0) - jnp.log1p(jnp.exp(-jnp.abs(x)))


def _modulated_norm(x, g, shift, scale):
    y = x * lax.rsqrt(jnp.mean(x * x, axis=-1, keepdims=True) + EPS) * g
    return y * (1.0 + scale) + shift


def _resident(shape):
    return pl.BlockSpec(shape, lambda *_: (0,) * len(shape), pipeline_mode=pl.Buffered(1))


def _mods_kernel(c_ref, w_ref, b_ref, o_ref):
    c = c_ref[...]
    o_ref[...] = _dot(c * _sigmoid(c), w_ref[...]) + b_ref[...]


def _mods(cvecs, w_ada, b_ada):
    n = N_MOD * D_MODEL
    tn = MODS_TN
    return pl.pallas_call(
        _mods_kernel,
        grid=(n // tn,),
        in_specs=[pl.BlockSpec((MOD_ROWS, D_MODEL), lambda j: (0, 0)),
                  pl.BlockSpec((D_MODEL, tn), lambda j: (0, j)),
                  pl.BlockSpec((1, tn), lambda j: (0, j))],
        out_specs=pl.BlockSpec((MOD_ROWS, tn), lambda j: (0, j)),
        out_shape=jax.ShapeDtypeStruct((MOD_ROWS, n), F32),
        compiler_params=_params(1),
        name="adaln_mods",
    )(cvecs, w_ada, b_ada.reshape(1, n))


def _mod_reader(m_ref, row):
    return lambda j: m_ref[pl.ds(row, 1), j * D_MODEL:(j + 1) * D_MODEL]


def _ffn_kernel(*refs, base, mod_index, in_tiles, out_tiles, side_blocks):
    n_in, n_out = len(in_tiles), len(out_tiles)
    x_refs, refs = refs[:n_in], refs[n_in:]
    (m_ref, g_ref, w1_ref, w3_ref, w2_ref), refs = refs[:5], refs[5:]
    if side_blocks:
        side_in_ref, refs = refs[0], refs[1:]
        side_out_ref, refs = refs[n_out], refs[:n_out] + refs[n_out + 1:]
    o_refs, (w1_sc, w3_sc, w2_sc, hh_sc, acc_sc) = refs[:n_out], refs[n_out:]
    g = pl.program_id(0)
    nf = D_FF // FFN_TF
    row_tile = g - (nf - 1)
    mod = _mod_reader(m_ref, mod_index(jnp.maximum(row_tile, 0) * FFN_TM))
    gain = g_ref[base // 3:base // 3 + 1, :]
    norm = lambda x: _modulated_norm(x, gain, mod(base), mod(base + 1)).astype(BF16)
    finish = lambda x, acc: x + 0.5 * mod(base + 2) * acc

    def tile(hh, f):
        a = _dot(hh, w1_sc[f])
        b = _dot(hh, w3_sc[f])
        return _dot(a * _sigmoid(a) * b, w2_sc[f])

    @pl.when(g < nf)
    def _():
        w1_sc[g] = w1_ref[...].astype(BF16)
        w3_sc[g] = w3_ref[...].astype(BF16)
        w2_sc[g] = w2_ref[...].astype(BF16)

        @pl.when(g == 0)
        def _():
            hh_sc[...] = norm(x_refs[0][...])
            acc_sc[...] = jnp.zeros(acc_sc.shape, F32)

        acc_sc[...] += tile(hh_sc[...], g)

        @pl.when(g == nf - 1)
        def _():
            o_refs[0][...] = finish(x_refs[0][...], acc_sc[...])

    def full_tile(x_ref, o_ref):
        x = x_ref[...]
        hh = norm(x)
        acc = jnp.zeros(x.shape, F32)
        for f in range(nf):
            acc = acc + tile(hh, f)
            if side_blocks and f < SIDE_PIECES:
                rows = slice(f * (D_MODEL // SIDE_PIECES), (f + 1) * (D_MODEL // SIDE_PIECES))
                side_out_ref[rows, :] = side_in_ref[rows, :].astype(BF16)
        o_ref[...] = finish(x, acc)

    bounds = sorted(set(np.cumsum((0,) + in_tiles).tolist()) | set(np.cumsum((0,) + out_tiles).tolist()))
    for lo, hi in zip(bounds[:-1], bounds[1:]):
        k_in = int(np.searchsorted(np.cumsum(in_tiles), lo, side="right"))
        k_out = int(np.searchsorted(np.cumsum(out_tiles), lo, side="right"))
        pl.when((g >= nf) & (row_tile >= lo) & (row_tile < hi))(
            functools.partial(full_tile, x_refs[k_in], o_refs[k_out]))


def _ffn(xs, out_rows, mods, mod_index, g, w1, w3, w2, base, side=None):
    nf = D_FF // FFN_TF
    in_tiles = tuple(x.shape[0] // FFN_TM for x in xs)
    out_tiles = tuple(r // FFN_TM for r in out_rows)
    assert sum(in_tiles) == sum(out_tiles)
    side_in_specs, side_out_specs, side_out_shape, side_args, side_blocks = [], [], [], [], 0
    if side is not None:
        side_w, side_blocks, side_start = side
        assert side_blocks <= sum(in_tiles) - 1
        blk = lambda s: jnp.clip(s - nf, 0, side_blocks - 1)
        side_in_specs = [pl.BlockSpec((pl.Element(D_MODEL), pl.Element(D_MODEL)), lambda s: (side_start(blk(s)), 0))]
        side_out_specs = [pl.BlockSpec((None, D_MODEL, D_MODEL), lambda s: (blk(s), 0, 0))]
        side_out_shape = [jax.ShapeDtypeStruct((side_blocks, D_MODEL, D_MODEL), BF16)]
        side_args = [side_w]
    row_tile = lambda s: jnp.maximum(s - (nf - 1), 0)
    f_tile = lambda s: jnp.minimum(s, nf - 1)

    def part_spec(tiles, k):
        start = sum(tiles[:k])
        return pl.BlockSpec((FFN_TM, D_MODEL), lambda s: (jnp.clip(row_tile(s) - start, 0, tiles[k] - 1), 0))

    return pl.pallas_call(
        functools.partial(_ffn_kernel, base=base, mod_index=mod_index, in_tiles=in_tiles, out_tiles=out_tiles,
                          side_blocks=side_blocks),
        grid=(nf - 1 + sum(in_tiles),),
        in_specs=[part_spec(in_tiles, k) for k in range(len(xs))] + [
            _resident(mods.shape),
            _resident(g.shape),
            pl.BlockSpec((D_MODEL, FFN_TF), lambda s: (0, f_tile(s))),
            pl.BlockSpec((D_MODEL, FFN_TF), lambda s: (0, f_tile(s))),
            pl.BlockSpec((FFN_TF, D_MODEL), lambda s: (f_tile(s), 0))] + side_in_specs,
        out_specs=[part_spec(out_tiles, k) for k in range(len(out_rows))] + side_out_specs,
        out_shape=[jax.ShapeDtypeStruct((r, D_MODEL), F32) for r in out_rows] + side_out_shape,
        scratch_shapes=[pltpu.VMEM((nf, D_MODEL, FFN_TF), BF16), pltpu.VMEM((nf, D_MODEL, FFN_TF), BF16),
                        pltpu.VMEM((nf, FFN_TF, D_MODEL), BF16),
                        pltpu.VMEM((FFN_TM, D_MODEL), BF16), pltpu.VMEM((FFN_TM, D_MODEL), F32)],
        compiler_params=_params(1),
        name="ffn",
    )(*xs, mods, g, w1, w3, w2, *side_args)


def _group_norm64(x, bd, g):
    ss = _dot(x * x, bd)
    return x * lax.rsqrt(ss * (1.0 / DA_HEAD_DIM) + EPS) * g


ROPE_AXIS_DIM = DA_HEAD_DIM // 2
ROPE_HALF = ROPE_AXIS_DIM // 2


def _rope(x, cos, sin_signed):
    first = (lax.broadcasted_iota(jnp.int32, x.shape, 1) % ROPE_AXIS_DIM) < ROPE_HALF
    partner = jnp.where(first, pltpu.roll(x, LANES - ROPE_HALF, 1), pltpu.roll(x, ROPE_HALF, 1))
    return x * cos + partner * sin_signed


N_SEG = 9
PROJ_WARM = (N_SEG + 1) // 2
GATE_LO = 4 * D_MODEL


def _seg_start(s):
    return s * D_MODEL + N_GATE_COLS * (s >= 4)


def _proj_kernel(*refs, ctx_tiles, row_of_step):
    (x_ref, m_ref, g_ref, wt_ref, bg_ref, bd_ref, gq_ref, gk_ref, cos_ref, sin_ref), refs = refs[:10], refs[10:]
    (mq_ref, mk_ref, mv_ref, so_ref, dq_ref, sgm_ref, sgd_ref, gates_ref,
     dk_ctx_ref, dv_ctx_ref, dk_lat_ref, dv_lat_ref, w_sc, wg_sc, hh_sc) = refs
    step = pl.program_id(0)
    mod = _mod_reader(m_ref, row_of_step(step))
    norm = lambda: _modulated_norm(x_ref[...], g_ref[1:2, :], mod(3), mod(4)).astype(BF16)
    groups = MXU_DIM // DA_HEAD_DIM
    tiled = lambda ref, scale: jnp.concatenate([ref[...] * scale] * groups, axis=1)
    gq_row, gk_row = tiled(gq_ref, QK_LOG2_SCALE), tiled(gk_ref, 1.0)
    bias = jnp.concatenate([bg_ref[...], jnp.zeros((1, GATE_PAD - N_GATE_COLS), F32)], axis=1)

    def gates(hh):
        y = _dot_nt(hh, wg_sc[...]) + bias
        col = lax.broadcasted_iota(jnp.int32, y.shape, 1)
        gates_ref[...] = jnp.where((col // ML_HEADS) % 2 == 1, _log_sigmoid(y), y)

    def qk_segment(y, gain_row, o_ref, rope):
        bd = bd_ref[...]
        for c in range(D_MODEL // MXU_DIM):
            cols = slice(c * MXU_DIM, (c + 1) * MXU_DIM)
            z = _group_norm64(y[:, cols], bd, gain_row)
            if rope:
                z = jnp.concatenate(
                    [_rope(z[:, k * LANES:(k + 1) * LANES], cos_ref[...], sin_ref[...])
                     for k in range(MXU_DIM // LANES)], axis=1)
            o_ref[:, cols] = z.astype(o_ref.dtype)

    def cache_key_segment(hh):
        y_t = _dot_nt(w_sc[5], hh)
        z = y_t.reshape(D_MODEL // DA_HEAD_DIM, DA_HEAD_DIM, y_t.shape[1])
        ms = jnp.mean(z * z, axis=1, keepdims=True)
        gain = jnp.broadcast_to(gk_row[:, 0:LANES], (LANES, LANES)).T[0:DA_HEAD_DIM, :]
        gain = jnp.concatenate([gain] * (y_t.shape[1] // LANES), axis=1)
        dk_ctx_ref[...] = (z * lax.rsqrt(ms + EPS) * gain).reshape(y_t.shape)

    def segment(s, hh, ctx):
        if s in (0, 2):
            o_ref = {0: mq_ref, 2: mv_ref}[s]
            y_t = _dot_nt(w_sc[s], hh) * ((ML_HEAD_DIM ** -0.5) if s == 0 else 1.0)
            for j in range(PROJ_TM // CHUNK):
                o_ref[j] = y_t[:, j * CHUNK:(j + 1) * CHUNK].astype(o_ref.dtype)
            return
        if s == 5 and ctx:
            cache_key_segment(hh)
            return
        y = _dot_nt(hh, w_sc[s])
        if s == 4:
            qk_segment(y, gq_row, dq_ref, rope=not ctx)
        elif s == 5:
            qk_segment(y, gk_row, dk_lat_ref, rope=True)
        elif s in (3, 7, 8):
            {3: so_ref, 7: sgm_ref, 8: sgd_ref}[s][...] = _sigmoid(y)
        elif s == 6 and ctx:
            dv_ctx_ref[...] = y.reshape(y.shape[0], DA_HEADS, DA_V_DIM)
        elif s == 6:
            dv_lat_ref[...] = y.astype(dv_lat_ref.dtype)
        else:
            mk_ref[...] = y.astype(mk_ref.dtype)

    for w in range(PROJ_WARM):
        @pl.when(step == w)
        def _(w=w):
            if w == 0:
                hh_sc[...] = norm()
            for j in range(2):
                s = 2 * w + j
                if s < N_SEG:
                    w_sc[s] = wt_ref[j]
                    segment(s, hh_sc[...], ctx=True)
                else:
                    wg_sc[...] = jnp.zeros(wg_sc.shape, BF16)
                    wg_sc[0:N_GATE_COLS, :] = wt_ref[j, 0:N_GATE_COLS, :]
                    gates(hh_sc[...])

    def full_tile(ctx):
        hh = norm()
        gates(hh)
        for s in range(N_SEG):
            segment(s, hh, ctx)

    is_ctx = step - (PROJ_WARM - 1) < ctx_tiles
    pl.when((step >= PROJ_WARM) & is_ctx)(functools.partial(full_tile, True))
    pl.when((step >= PROJ_WARM) & jnp.logical_not(is_ctx))(functools.partial(full_tile, False))


def _proj(x, n_ctx, seq_ctx, seq_lat, mods, mod_index, g, w_seg, b_gate, bd, gq, gk, rope_tabs):
    rows = x.shape[0]
    n_lat = rows - n_ctx
    assert seq_ctx == PROJ_TM
    ctx_tiles, lat_tiles = n_ctx // PROJ_TM, n_lat // PROJ_TM
    row_tile = lambda s: jnp.maximum(s - (PROJ_WARM - 1), 0)
    ctx_tile = lambda s: jnp.minimum(row_tile(s), ctx_tiles - 1)
    lat_tile = lambda s: jnp.clip(row_tile(s) - ctx_tiles, 0, lat_tiles - 1)
    row = lambda s: (row_tile(s), 0)
    tile = pl.BlockSpec((PROJ_TM, D_MODEL), row)
    w_block = (2, D_MODEL, D_MODEL)
    tab = pl.BlockSpec((PROJ_TM, LANES), lambda s: (lat_tile(s) % (seq_lat // PROJ_TM), 0))
    in_specs = [tile,
                _resident(mods.shape),
                _resident(g.shape),
                pl.BlockSpec(w_block, lambda s: (jnp.minimum(s, PROJ_WARM - 1), 0, 0)),
                _resident(b_gate.shape),
                _resident((MXU_DIM, MXU_DIM)),
                _resident(gq.shape),
                _resident(gk.shape),
                tab, tab]
    slab = pl.BlockSpec((PROJ_TM // CHUNK, D_MODEL, CHUNK), lambda s: (row_tile(s), 0, 0))
    slab_shape = jax.ShapeDtypeStruct((rows // CHUNK, D_MODEL, CHUNK), BF16)
    full = lambda dt: jax.ShapeDtypeStruct((rows, D_MODEL), dt)
    out_specs = [slab, tile, slab, tile, tile, tile, tile, pl.BlockSpec((PROJ_TM, GATE_PAD), row),
                 pl.BlockSpec((None, D_MODEL, seq_ctx), lambda s: (ctx_tile(s), 0, 0)),
                 pl.BlockSpec((PROJ_TM, DA_HEADS, DA_V_DIM), lambda s: (ctx_tile(s), 0, 0)),
                 pl.BlockSpec((PROJ_TM, D_MODEL), lambda s: (lat_tile(s), 0)),
                 pl.BlockSpec((PROJ_TM, D_MODEL), lambda s: (lat_tile(s), 0))]
    out_shape = [slab_shape, full(BF16), slab_shape, full(F32), full(BF16), full(F32), full(F32),
                 jax.ShapeDtypeStruct((rows, GATE_PAD), F32),
                 jax.ShapeDtypeStruct((n_ctx // seq_ctx, D_MODEL, seq_ctx), F32),
                 jax.ShapeDtypeStruct((n_ctx, DA_HEADS, DA_V_DIM), F32),
                 jax.ShapeDtypeStruct((n_lat, D_MODEL), BF16),
                 jax.ShapeDtypeStruct((n_lat, D_MODEL), BF16)]
    return pl.pallas_call(
        functools.partial(_proj_kernel, ctx_tiles=ctx_tiles,
                          row_of_step=lambda step: mod_index(row_tile(step) * PROJ_TM)),
        grid=(PROJ_WARM - 1 + rows // PROJ_TM,),
        in_specs=in_specs,
        out_specs=out_specs,
        out_shape=out_shape,
        scratch_shapes=[pltpu.VMEM((N_SEG, D_MODEL, D_MODEL), BF16), pltpu.VMEM((GATE_PAD, D_MODEL), BF16),
                        pltpu.VMEM((PROJ_TM, D_MODEL), BF16)],
        compiler_params=_params(1),
        name="mixer_in_proj",
    )(x, mods, g, w_seg, b_gate, bd, gq, gk, *rope_tabs)


def _lambda(lam_refs, lam_init):
    q1, k1, q2, k2 = (r[...] for r in lam_refs)
    s1 = jnp.sum(q1 * k1, axis=1, keepdims=True)
    s2 = jnp.sum(q2 * k2, axis=1, keepdims=True)
    return jnp.exp(s1) - jnp.exp(s2) + lam_init


def _attn_kernel(*refs, cached, seq, nb, lam_init):
    if cached:
        (q_ref, k_ref, v_ref, ck_ref, cv_ref), refs = refs[:5], refs[5:]
    else:
        (q_ref, k_ref, v_ref), refs = refs[:3], refs[3:]
    lam_refs, (gs_ref, o_ref, kall_sc, vt_sc, s_sc) = refs[:4], refs[4:]
    units = [(bb, h) for bb in range(nb) for h in range(DA_HEADS)]

    def cache_order_kv(kt_ref, vr_ref, u, h, lo, n):
        kall_sc[u, lo:lo + n, :] = kt_ref[h * DA_V_DIM:(h + 1) * DA_V_DIM, :].T.astype(BF16)
        vt_sc[u, :, lo:lo + n] = vr_ref[pl.ds(h, n, stride=DA_HEADS), :].T.astype(BF16)

    @pl.when(pl.program_id(1) == 0)
    def _():
        for u, (bb, h) in enumerate(units):
            if cached:
                cols = slice(h * DA_V_DIM, (h + 1) * DA_V_DIM)
                kall_sc[u, 0:seq, :] = k_ref[:, cols].astype(BF16)
                vt_sc[u, :, 0:seq] = v_ref[:, cols].astype(F32).T.astype(BF16)
                cache_order_kv(ck_ref.at[bb], cv_ref.at[bb], u, h, seq, ck_ref.shape[2])
            else:
                cache_order_kv(k_ref.at[bb], v_ref.at[bb], u, h, 0, seq)

    lam = _lambda(lam_refs, lam_init)
    sub_gain = gs_ref[...] * (1.0 - lam_init)
    lane = lax.broadcasted_iota(jnp.int32, (1, DA_V_DIM), 1)
    comp_masks = [lane < DA_HEAD_DIM, lane >= DA_HEAD_DIM]
    tq = q_ref.shape[0] // nb
    n_keys = kall_sc.shape[1]
    n_tiles = pl.cdiv(n_keys, ATT_TK_MAX)
    tk = n_keys // n_tiles

    def stacked_q(u):
        bb, h = units[u]
        q = q_ref[bb * tq:(bb + 1) * tq, h * DA_V_DIM:(h + 1) * DA_V_DIM].astype(BF16)
        return jnp.concatenate([jnp.where(m, q, jnp.zeros_like(q)) for m in comp_masks], axis=0)

    def score_tile(h, j, qq, m8):
        rows = slice(j * tk, (j + 1) * tk)
        st = _dot_nt(kall_sc[h, rows, :], qq)
        s_sc[h % 2, rows, :] = st
        t8 = jnp.max(st.reshape(tk // SUBLANES, SUBLANES, 2 * tq), axis=0)
        return t8 if m8 is None else jnp.maximum(m8, t8)

    def prob_tile(h, j, mx, d8, pv):
        rows = slice(j * tk, (j + 1) * tk)
        e = jnp.exp2(s_sc[h % 2, rows, :] - mx)
        s8 = jnp.sum(e.reshape(tk // SUBLANES, SUBLANES, 2 * tq), axis=0)
        p = _dot(vt_sc[h, :, rows], e)
        return (s8 if d8 is None else d8 + s8), (p if pv is None else pv + p)

    qq = stacked_q(0)
    m8 = None
    for j in range(n_tiles):
        m8 = score_tile(0, j, qq, m8)
    for h in range(len(units)):
        mx = jnp.max(m8, axis=0, keepdims=True)
        if h + 1 < len(units):
            qq = stacked_q(h + 1)
        m8, d8, pv = None, None, None
        for j in range(n_tiles):
            if h + 1 < len(units):
                m8 = score_tile(h + 1, j, qq, m8)
            d8, pv = prob_tile(h, j, mx, d8, pv)
        inv = 1.0 / jnp.sum(d8, axis=0, keepdims=True)
        out_t = pv[:, :tq] * inv[:, :tq] - pv[:, tq:] * (lam * inv[:, tq:])
        out_t = out_t * lax.rsqrt(jnp.mean(out_t * out_t, axis=0, keepdims=True) + EPS)
        bb, head = units[h]
        o_ref[bb * tq:(bb + 1) * tq, head * DA_V_DIM:(head + 1) * DA_V_DIM] = (out_t.T * sub_gain).astype(o_ref.dtype)


def _attn(q, q_row0, k, v, cache, lam_vecs, g_sub, batch, seq_len, lam_init):
    rows = batch * seq_len
    nq = seq_len // ATT_TQ
    nb = ATT_SHORT_SEQ_BATCH if (cache is None and nq == 1) else 1
    assert batch % nb == 0
    q_spec = pl.BlockSpec((nb * ATT_TQ, D_MODEL), lambda b, i: (q_row0 // (nb * ATT_TQ) + b * nq + i, 0))
    o_spec = pl.BlockSpec((nb * ATT_TQ, D_MODEL), lambda b, i: (b * nq + i, 0))
    cache_specs = lambda n: [pl.BlockSpec((nb, D_MODEL, n), lambda b, i: (b, 0, 0)),
                             pl.BlockSpec((nb, n * DA_HEADS, DA_V_DIM), lambda b, i: (b, 0, 0))]
    n_keys = seq_len
    if cache is not None:
        past = cache[0].shape[2]
        n_keys += past
        kv_spec = pl.BlockSpec((seq_len, D_MODEL), lambda b, i: (b, 0))
        in_specs = [q_spec, kv_spec, kv_spec] + cache_specs(past)
        args = [q, k, v] + list(cache)
    else:
        in_specs = [q_spec] + cache_specs(seq_len)
        args = [q, k, v.reshape(batch, seq_len * DA_HEADS, DA_V_DIM)]
    in_specs += [_resident((1, DA_HEAD_DIM))] * 4 + [_resident((1, DA_V_DIM))]
    args += list(lam_vecs) + [g_sub]
    return pl.pallas_call(
        functools.partial(_attn_kernel, cached=cache is not None, seq=seq_len, nb=nb, lam_init=lam_init),
        grid=(batch // nb, nq),
        in_specs=in_specs,
        out_specs=o_spec,
        out_shape=jax.ShapeDtypeStruct((rows, D_MODEL), BF16),
        scratch_shapes=[pltpu.VMEM((nb * DA_HEADS, n_keys, DA_V_DIM), BF16),
                        pltpu.VMEM((nb * DA_HEADS, DA_V_DIM, n_keys), BF16),
                        pltpu.VMEM((2, n_keys, 2 * ATT_TQ), F32)],
        compiler_params=_params(2),
        name="diff_attention",
    )(*args)


def _per_chain(fn, a, b):
    return jnp.stack([fn(a[i], b[i]) for i in range(a.shape[0])])


ML_EXT = SUBLANES
ML_SHORT_SEQ_BATCH = 2


def _rows(x):
    return jnp.stack([x[i:i + 1, :] for i in range(x.shape[0])])


def _split3(x):
    hi = x.astype(BF16)
    r = x - hi.astype(F32)
    mid = r.astype(BF16)
    lo = (r - mid.astype(F32)).astype(BF16)
    return jnp.concatenate([hi, mid, lo], axis=1)


def _chunk_scan(x, reverse_rows, tri_prefix, tri_suffix):
    parts = _split3(x)
    return jnp.where(reverse_rows, _dot(parts, tri_suffix), _dot(parts, tri_prefix))


def _paired_value_matmul(v_t, sc):
    B, L, _ = sc.shape
    zero = jnp.zeros((L, L), sc.dtype)
    out = []
    for i in range(0, B, 2):
        lhs = jnp.concatenate([v_t[i], v_t[i + 1]], axis=1)
        rhs = jnp.concatenate([jnp.concatenate([sc[i], zero], axis=1),
                               jnp.concatenate([zero, sc[i + 1]], axis=1)], axis=0)
        both = _dot(lhs, rhs)
        out += [both[:, :L], both[:, L:]]
    return jnp.stack(out)


def _mlstm_step(k, q_t, v_t, ic, fc, CT, m, seen_t, reverse_rows):
    B, L = ic.shape
    d = k.shape[2]
    zero_state = CT is None
    if zero_state:
        m = jnp.zeros((B, 1, 1), F32)
    tri = lambda keep: jnp.concatenate([jnp.where(keep, 1.0, 0.0).astype(BF16)] * 3, axis=0)
    b2 = _chunk_scan(fc, reverse_rows, tri(seen_t[0]), tri(seen_t[B - 1]))
    u2 = ic - b2
    u_t = jnp.concatenate([u2, jnp.zeros((L - B, L), F32)], axis=0).T
    u_col = jnp.stack([jnp.broadcast_to(u_t[:, i:i + 1], (L, L)) for i in range(B)])
    b, i_g, f_g = _rows(b2), _rows(ic), _rows(fc)
    b_last = jnp.sum(f_g, axis=2, keepdims=True)

    log_d = jnp.where(seen_t, b + u_col, -jnp.inf)
    a = b + m
    m_t = jnp.maximum(a, jnp.max(log_d, axis=1, keepdims=True))
    dmat = jnp.exp(log_d - m_t)
    sc = _per_chain(_dot, k, q_t) * dmat
    value_matmul = _paired_value_matmul if 2 * L <= MXU_DIM else functools.partial(_per_chain, _dot)
    num = value_matmul(v_t, sc.astype(BF16))
    den = jnp.sum(sc, axis=1, keepdims=True)
    if not zero_state:
        inter = jnp.exp(a - m_t)
        cq = _per_chain(_dot, CT, q_t)
        num = num + inter * cq[:, :d, :]
        den = den + inter * cq[:, d:d + 1, :]
    h_t = num * (1.0 / jnp.maximum(jnp.abs(den), jnp.exp(-m_t)))

    g = b_last - b + i_g
    m_new = jnp.maximum(b_last + m, jnp.max(g, axis=2, keepdims=True))
    w = jnp.exp(g - m_new)
    decay = jnp.exp(b_last + m - m_new)
    vw = jnp.concatenate([v_t.astype(F32), jnp.ones((B, ML_EXT, L), F32)], axis=1) * w
    CT_new = _per_chain(_dot, vw, k)
    if not zero_state:
        CT_new = decay * CT + CT_new
    return h_t, CT_new, m_new


def _mlstm_kernel(*refs, seq_len, nb, has_state, emit_state):
    q_ref, k_ref, v_ref, so_ref, g_ref, gmh_ref = refs[:6]
    refs = refs[6:]
    if has_state:
        (c0_ref, n0_ref, m0_ref), refs = refs[:3], refs[3:]
    hm_ref, refs = refs[0], refs[1:]
    if emit_state:
        (c_out_ref, n_out_ref, m_out_ref), refs = refs[:3], refs[3:]
    ct_sc, m_sc, gr_sc, h_sc = refs

    nc = seq_len // CHUNK
    d_head = ML_HEAD_DIM
    chains = [(bb, d, h) for bb in range(nb) for d in range(2) for h in range(ML_HEADS)]
    n_chain = len(chains)
    for i, (bb, d, h) in enumerate(chains):
        if has_state:
            ct_sc[i, 0:d_head, :] = c0_ref[bb, d, h].T
            ct_sc[i, d_head:d_head + ML_EXT, :] = jnp.broadcast_to(n0_ref[bb, d, h:h + 1, :], (ML_EXT, d_head))
            m_sc[i] = jnp.full((1, 1), m0_ref[pl.program_id(0) * nb + bb, 0, d, h], F32)
        elif nc > 2:
            ct_sc[i] = jnp.zeros((d_head + ML_EXT, d_head), F32)
            m_sc[i] = jnp.zeros((1, 1), F32)
    for c in range(nb * nc):
        gr_sc[c] = g_ref[c * CHUNK:(c + 1) * CHUNK, :].T

    s_idx = lax.broadcasted_iota(jnp.int32, (CHUNK, CHUNK), 0)
    t_idx = lax.broadcasted_iota(jnp.int32, (CHUNK, CHUNK), 1)
    seen_t = jnp.stack([(s_idx >= t_idx) if d else (s_idx <= t_idx) for _, d, _ in chains])
    reverse_rows = (lax.broadcasted_iota(jnp.int32, (n_chain, CHUNK), 0) // ML_HEADS) % 2 == 1

    def step(c_fwd, c_bwd, rows_of, zero_state=False):
        chunk_of = (c_fwd, c_bwd)
        lo = 2 * ML_HEADS
        ic, fc = [], []
        for bb in range(nb):
            g_fwd, g_bwd = gr_sc[bb * nc + c_fwd], gr_sc[bb * nc + c_bwd]
            ic += [g_fwd[0:ML_HEADS], g_bwd[lo:lo + ML_HEADS]]
            fc += [g_fwd[ML_HEADS:lo], g_bwd[lo + ML_HEADS:2 * lo]]
        head = [slice(h * d_head, (h + 1) * d_head) for _, _, h in chains]
        stack = lambda pick: jnp.stack([pick(i, bb * nc + chunk_of[d]) for i, (bb, d, _) in enumerate(chains)])
        h_t, CT_new, m_new = _mlstm_step(
            stack(lambda i, c: k_ref[rows_of(c), head[i]]),
            stack(lambda i, c: q_ref[c, head[i], :]),
            stack(lambda i, c: v_ref[c, head[i], :]),
            jnp.concatenate(ic, axis=0), jnp.concatenate(fc, axis=0),
            None if zero_state else ct_sc[...], None if zero_state else m_sc[...], seen_t, reverse_rows)
        ct_sc[...] = CT_new
        m_sc[...] = m_new
        for i, (bb, d, _) in enumerate(chains):
            h_sc[d, bb * nc + chunk_of[d], head[i], :] = h_t[i]

    if nc <= 2:
        for c in range(nc):
            step(c, nc - 1 - c, lambda cc: slice(cc * CHUNK, (cc + 1) * CHUNK),
                 zero_state=(c == 0 and not has_state))
    else:
        def body(c, carry):
            step(c, nc - 1 - c, lambda cc: pl.ds(pl.multiple_of(cc * CHUNK, CHUNK), CHUNK))
            return carry
        lax.fori_loop(0, nc, body, 0, unroll=2)

    for h in range(ML_HEADS):
        hcols = slice(h * d_head, (h + 1) * d_head)
        gain = jnp.broadcast_to(gmh_ref[h:h + 1, :], (CHUNK, d_head)).T
        for c in range(nb * nc):
            rows = slice(c * CHUNK, (c + 1) * CHUNK)
            hsum = h_sc[0, c, hcols, :] + h_sc[1, c, hcols, :]
            hn = hsum * lax.rsqrt(jnp.mean(hsum * hsum, axis=0, keepdims=True) + EPS) * gain
            hm_ref[rows, hcols] = (hn.T * so_ref[rows, hcols]).astype(hm_ref.dtype)
    if emit_state:
        for i, (bb, d, h) in enumerate(chains):
            c_out_ref[bb, d, h] = ct_sc[i, 0:d_head, :].T
            n_out_ref[bb, d, h:h + 1, :] = ct_sc[i, d_head:d_head + 1, :]
            m_out_ref[bb, d:d + 1, h:h + 1] = m_sc[i]


def _mlstm(q_t, k, v_t, so, gates, row0, g_mh, state, batch, seq_len, emit_state):
    rows = batch * seq_len
    d = ML_HEAD_DIM
    nc = seq_len // CHUNK
    nb = ML_SHORT_SEQ_BATCH if nc <= 2 else 1
    assert batch % nb == 0
    assert row0 % (nb * seq_len) == 0
    first = row0 // (nb * seq_len)
    tile = pl.BlockSpec((nb * seq_len, D_MODEL), lambda b: (first + b, 0))
    tile_t = pl.BlockSpec((nb * nc, D_MODEL, CHUNK), lambda b: (first + b, 0, 0))
    in_specs = [tile_t, tile, tile_t, tile,
                pl.BlockSpec((nb * seq_len, GATE_PAD), lambda b: (first + b, 0)),
                _resident((ML_HEADS, d))]
    args = [q_t, k, v_t, so, gates, g_mh]
    c_spec = pl.BlockSpec((nb, None, 2, ML_HEADS, d, d), lambda b: (b, 0, 0, 0, 0, 0))
    n_spec = pl.BlockSpec((nb, None, 2, ML_HEADS, d), lambda b: (b, 0, 0, 0, 0))
    if state is not None:
        in_specs += [c_spec, n_spec, pl.BlockSpec(memory_space=pltpu.SMEM)]
        args += list(state)
    out_specs = [pl.BlockSpec((nb * seq_len, D_MODEL), lambda b: (b, 0))]
    out_shape = [jax.ShapeDtypeStruct((rows, D_MODEL), BF16)]
    if emit_state:
        out_specs += [c_spec, n_spec, pl.BlockSpec((nb, None, 2, ML_HEADS), lambda b: (b, 0, 0, 0))]
        out_shape += [jax.ShapeDtypeStruct((batch, 1, 2, ML_HEADS, d, d), F32),
                      jax.ShapeDtypeStruct((batch, 1, 2, ML_HEADS, d), F32),
                      jax.ShapeDtypeStruct((batch, 1, 2, ML_HEADS), F32)]
    n_state = 2 * ML_HEADS * nb
    return pl.pallas_call(
        functools.partial(_mlstm_kernel, seq_len=seq_len, nb=nb, has_state=state is not None,
                          emit_state=emit_state),
        grid=(batch // nb,),
        in_specs=in_specs,
        out_specs=out_specs,
        out_shape=out_shape,
        scratch_shapes=[pltpu.VMEM((n_state, d + ML_EXT, d), F32), pltpu.VMEM((n_state, 1, 1), F32),
                        pltpu.VMEM((nb * nc, GATE_PAD, CHUNK), F32),
                        pltpu.VMEM((2, nb * nc, D_MODEL, CHUNK), F32)],
        compiler_params=_params(1),
        name="mlstm",
    )(*args)


def _merge_kernel(*refs, tiles, mod_index):
    n = len(tiles)
    (x_ref, m_ref, sgm_ref, sgd_ref), refs = refs[:4], refs[4:]
    parts = [refs[j * n:(j + 1) * n] for j in range(2)]
    wm_ref, wd_ref, wo_ref, o_ref, w_sc = refs[2 * n:]
    i = pl.program_id(0)

    @pl.when(i == 0)
    def _():
        for j, w_ref in enumerate((wm_ref, wd_ref, wo_ref)):
            w_sc[j] = w_ref[...].astype(BF16)

    def tile(hm_ref, att_ref):
        y = sgm_ref[...] * _dot(hm_ref[...], w_sc[0]) + sgd_ref[...] * _dot(att_ref[...], w_sc[1])
        o_ref[...] = x_ref[...] + _mod_reader(m_ref, mod_index(i * MERGE_TM))(5) * _dot(y, w_sc[2])

    starts = np.cumsum((0,) + tiles).tolist()
    for k in range(n):
        pl.when((i >= starts[k]) & (i < starts[k + 1]))(functools.partial(tile, *[p[k] for p in parts]))


def _merge(x, mods, mod_index, hm, att, sgm, sgd, wm, wd, wo):
    rows = x.shape[0]
    tiles = tuple(h.shape[0] // MERGE_TM for h in hm)
    assert sum(tiles) * MERGE_TM == rows
    tile = pl.BlockSpec((MERGE_TM, D_MODEL), lambda i: (i, 0))

    def part_spec(k):
        start = sum(tiles[:k])
        return pl.BlockSpec((MERGE_TM, D_MODEL), lambda i: (jnp.clip(i - start, 0, tiles[k] - 1), 0))

    part_specs = [part_spec(k) for k in range(len(tiles))]
    w_spec = _resident((D_MODEL, D_MODEL))
    return pl.pallas_call(
        functools.partial(_merge_kernel, tiles=tiles, mod_index=mod_index),
        grid=(rows // MERGE_TM,),
        in_specs=[tile, _resident(mods.shape), tile, tile]
                 + part_specs * 2 + [w_spec, w_spec, w_spec],
        out_specs=tile,
        out_shape=jax.ShapeDtypeStruct((rows, D_MODEL), F32),
        scratch_shapes=[pltpu.VMEM((3, D_MODEL, D_MODEL), BF16)],
        compiler_params=_params(1),
        name="branch_merge",
    )(x, mods, sgm, sgd, *hm, *att, wm, wd, wo)


def _rope_tables(seq_len):
    lane = np.arange(LANES)
    r = lane % ROPE_AXIS_DIM
    freqs = np.power(np.float32(ROPE_BASE), -(r % ROPE_HALF).astype(np.float32) / np.float32(ROPE_HALF))
    tok = np.arange(seq_len)
    by_row = (lane % DA_HEAD_DIM < ROPE_AXIS_DIM)[None, :]
    pos = np.where(by_row, (tok // GRID_W)[:, None], (tok % GRID_W)[:, None]).astype(np.float32)
    ang = pos * freqs[None, :]
    sign = np.where(r < ROPE_HALF, -1.0, 1.0).astype(np.float32)
    return jnp.asarray(np.cos(ang), F32), jnp.asarray(np.sin(ang) * sign[None, :], F32)


def kernel(x_prompt, x_sample, c, cache_k, cache_v, state_C, state_n, state_m, c_ctx, w_ada, b_ada, g_norm, ffn1_w1, ffn1_w3, ffn1_w2, ffn2_w1, ffn2_w3, ffn2_w2, w_in, b_gate, g_qn, g_kn, lam_q1, lam_k1, lam_q2, lam_k2, g_sub, g_mh, w_br_m, w_br_d, w_out):
    depth = w_ada.shape[0]
    assert depth == 1
    l = 0
    bp, tp, _ = x_prompt.shape
    bs, ts, _ = x_sample.shape
    past = cache_k.shape[2]
    lam_init = 0.8 - 0.6 * math.exp(-0.3 * l)

    cvecs = jnp.concatenate([c_ctx[None, :], c, jnp.zeros((MOD_ROWS - 1 - bs, D_MODEL), F32)], axis=0)
    mods = _mods(cvecs, w_ada[l], b_ada[l])

    group = np.arange(MXU_DIM) // DA_HEAD_DIM
    w = dict(
        g_norm=g_norm[l],
        ffn1_w1=ffn1_w1[l], ffn1_w3=ffn1_w3[l], ffn1_w2=ffn1_w2[l],
        ffn2_w1=ffn2_w1[l], ffn2_w3=ffn2_w3[l], ffn2_w2=ffn2_w2[l],
        w_in_t=w_in[l].T,
        b_gate=b_gate[l:l + 1],
        bd=jnp.asarray(group[:, None] == group[None, :], BF16),
        g_qn=g_qn[l:l + 1], g_kn=g_kn[l:l + 1],
        lam=(lam_q1[l:l + 1], lam_k1[l:l + 1], lam_q2[l:l + 1], lam_k2[l:l + 1]),
        g_sub=g_sub[l:l + 1], g_mh=g_mh[l],
        w_br_m=w_br_m[l], w_br_d=w_br_d[l], w_out=w_out[l],
    )

    n_ctx, n_lat = bp * tp, bs * ts
    mod_index = lambda r: jnp.where(r < n_ctx, 0, 1 + (r - n_ctx) // ts)
    seg_start = lambda j: pl.multiple_of(jnp.where(j < N_SEG, _seg_start(j), GATE_LO), N_GATE_COLS)
    x1, w["w_seg"] = _ffn((x_prompt.reshape(n_ctx, D_MODEL), x_sample.reshape(n_lat, D_MODEL)), (n_ctx + n_lat,),
                          mods, mod_index, w["g_norm"], w["ffn1_w1"], w["ffn1_w3"], w["ffn1_w2"], base=0,
                          side=(w["w_in_t"], N_SEG + 1, seg_start))

    mq, mk, mv, so, dq, sgm, sgd, gates, new_k_t, new_v, dk_lat, dv_lat = _proj(
        x1, n_ctx, tp, ts, mods, mod_index, w["g_norm"], w["w_seg"], w["b_gate"], w["bd"], w["g_qn"], w["g_kn"],
        _rope_tables(ts))
    att_p = _attn(dq, 0, new_k_t, new_v, None, w["lam"], w["g_sub"], bp, tp, lam_init)
    hm_p, new_c, new_n, new_m = _mlstm(mq, mk, mv, so, gates, 0, w["g_mh"], None, bp, tp, emit_state=True)

    cache = (cache_k[:, l].transpose(0, 2, 3, 4, 1).reshape(bs, D_MODEL, past),
             cache_v[:, l].reshape(bs, past * DA_HEADS, DA_V_DIM))
    att_s = _attn(dq, n_ctx, dk_lat, dv_lat, cache, w["lam"], w["g_sub"], bs, ts, lam_init)
    (hm_s,) = _mlstm(mq, mk, mv, so, gates, n_ctx, w["g_mh"], (state_C, state_n, state_m), bs, ts,
                     emit_state=False)

    x2 = _merge(x1, mods, mod_index, (hm_p, hm_s), (att_p, att_s), sgm, sgd, w["w_br_m"], w["w_br_d"], w["w_out"])
    xp, xs = _ffn((x2,), (n_ctx, n_lat), mods, mod_index, w["g_norm"], w["ffn2_w1"], w["ffn2_w3"], w["ffn2_w2"],
                  base=6)

    return (xp.reshape(bp, tp, D_MODEL), xs.reshape(bs, ts, D_MODEL),
            new_k_t.reshape(bp, DA_HEADS, 2, DA_HEAD_DIM, tp).transpose(0, 4, 1, 2, 3)[:, None],
            new_v.reshape(bp, 1, tp, DA_HEADS, DA_V_DIM),
            new_c, new_n, new_m)
```

```python
import functools
import math

import jax
import jax.numpy as jnp
import numpy as np
from jax import lax
from jax.experimental import pallas as pl
from jax.experimental.pallas import tpu as pltpu

F32 = jnp.float32
BF16 = jnp.bfloat16

D_MODEL = 1024
D_FF = 2816
N_MOD = 9
GRID_W = 64
ML_HEADS = 4
ML_HEAD_DIM = 256
DA_HEADS = 8
DA_HEAD_DIM = 64
DA_V_DIM = 128
N_GATE_COLS = 16
CHUNK = 256
ROPE_BASE = 10000.0
QK_LOG2_SCALE = DA_HEAD_DIM ** -0.5 * math.log2(math.e)
EPS = 1e-6

LANES = 128
SUBLANES = 8
GATE_PAD = LANES
MXU_DIM = 256
VMEM_LIMIT = 56 * 1024 * 1024

SIDE_PIECES = 8
MOD_ROWS = SUBLANES
FFN_TM = 512
FFN_TF = 256
PROJ_TM = 256
MERGE_TM = 512
ATT_TQ = 256
ATT_TK_MAX = 768
ATT_SHORT_SEQ_BATCH = 2
ATT_DEN_ROWS = 16


def _params(n_axes):
    return pltpu.CompilerParams(dimension_semantics=("arbitrary",) * n_axes,
                                vmem_limit_bytes=VMEM_LIMIT)


def _dot(a, b):
    return jnp.dot(a.astype(BF16), b.astype(BF16), preferred_element_type=F32)


def _dot_nt(a, b):
    return lax.dot_general(a.astype(BF16), b.astype(BF16), (((1,), (1,)), ((), ())),
                           preferred_element_type=F32)


def _sigmoid(x):
    return 1.0 / (1.0 + jnp.exp(-x))


def _log_sigmoid(x):
    return jnp.minimum(x, 0.0) - jnp.log1p(jnp.exp(-jnp.abs(x)))


def _modulated_norm(x, g, shift, scale):
    y = x * lax.rsqrt(jnp.mean(x * x, axis=-1, keepdims=True) + EPS) * g
    return y * (1.0 + scale) + shift


def _resident(shape):
    return pl.BlockSpec(shape, lambda *_: (0,) * len(shape), pipeline_mode=pl.Buffered(1))


def _mods_kernel(c_ref, w_ref, b_ref, o_ref):
    c = c_ref[...]
    o_ref[...] = _dot(c * _sigmoid(c), w_ref[...]) + b_ref[...]


def _mods(cvecs, w_ada, b_ada):
    n = N_MOD * D_MODEL
    tn = D_MODEL
    return pl.pallas_call(
        _mods_kernel,
        grid=(n // tn,),
        in_specs=[pl.BlockSpec((MOD_ROWS, D_MODEL), lambda j: (0, 0)),
                  pl.BlockSpec((D_MODEL, tn), lambda j: (0, j)),
                  pl.BlockSpec((1, tn), lambda j: (0, j))],
        out_specs=pl.BlockSpec((MOD_ROWS, tn), lambda j: (0, j)),
        out_shape=jax.ShapeDtypeStruct((MOD_ROWS, n), F32),
        compiler_params=_params(1),
        name="adaln_mods",
    )(cvecs, w_ada, b_ada.reshape(1, n))


def _mod_reader(m_ref, row):
    return lambda j: m_ref[pl.ds(row, 1), j * D_MODEL:(j + 1) * D_MODEL]


def _ffn_kernel(*refs, base, mod_index, in_tiles, out_tiles, side_blocks):
    n_in, n_out = len(in_tiles), len(out_tiles)
    x_refs, refs = refs[:n_in], refs[n_in:]
    (m_ref, g_ref, w1_ref, w3_ref, w2_ref), refs = refs[:5], refs[5:]
    if side_blocks:
        side_in_ref, refs = refs[0], refs[1:]
        side_out_ref, refs = refs[n_out], refs[:n_out] + refs[n_out + 1:]
    o_refs, (w1_sc, w3_sc, w2_sc, hh_sc, acc_sc) = refs[:n_out], refs[n_out:]
    g = pl.program_id(0)
    nf = D_FF // FFN_TF
    row_tile = g - (nf - 1)
    mod = _mod_reader(m_ref, mod_index(jnp.maximum(row_tile, 0) * FFN_TM))
    gain = g_ref[base // 3:base // 3 + 1, :]
    norm = lambda x: _modulated_norm(x, gain, mod(base), mod(base + 1)).astype(BF16)
    finish = lambda x, acc: x + 0.5 * mod(base + 2) * acc

    def tile(hh, f):
        a = _dot(hh, w1_sc[f])
        b = _dot(hh, w3_sc[f])
        return _dot(a * _sigmoid(a) * b, w2_sc[f])

    @pl.when(g < nf)
    def _():
        w1_sc[g] = w1_ref[...].astype(BF16)
        w3_sc[g] = w3_ref[...].astype(BF16)
        w2_sc[g] = w2_ref[...].astype(BF16)

        @pl.when(g == 0)
        def _():
            hh_sc[...] = norm(x_refs[0][...])
            acc_sc[...] = jnp.zeros(acc_sc.shape, F32)

        acc_sc[...] += tile(hh_sc[...], g)

        @pl.when(g == nf - 1)
        def _():
            o_refs[0][...] = finish(x_refs[0][...], acc_sc[...])

    def full_tile(x_ref, o_ref):
        x = x_ref[...]
        hh = norm(x)
        acc = jnp.zeros(x.shape, F32)
        for f in range(nf):
            acc = acc + tile(hh, f)
            if side_blocks and f < SIDE_PIECES:
                rows = slice(f * (D_MODEL // SIDE_PIECES), (f + 1) * (D_MODEL // SIDE_PIECES))
                side_out_ref[rows, :] = side_in_ref[rows, :].astype(BF16)
        o_ref[...] = finish(x, acc)

    bounds = sorted(set(np.cumsum((0,) + in_tiles).tolist()) | set(np.cumsum((0,) + out_tiles).tolist()))
    for lo, hi in zip(bounds[:-1], bounds[1:]):
        k_in = int(np.searchsorted(np.cumsum(in_tiles), lo, side="right"))
        k_out = int(np.searchsorted(np.cumsum(out_tiles), lo, side="right"))
        pl.when((g >= nf) & (row_tile >= lo) & (row_tile < hi))(
            functools.partial(full_tile, x_refs[k_in], o_refs[k_out]))


def _ffn(xs, out_rows, mods, mod_index, g, w1, w3, w2, base, side=None):
    nf = D_FF // FFN_TF
    in_tiles = tuple(x.shape[0] // FFN_TM for x in xs)
    out_tiles = tuple(r // FFN_TM for r in out_rows)
    assert sum(in_tiles) == sum(out_tiles)
    side_in_specs, side_out_specs, side_out_shape, side_args, side_blocks = [], [], [], [], 0
    if side is not None:
        side_w, side_blocks, side_start = side
        assert side_blocks <= sum(in_tiles) - 1
        blk = lambda s: jnp.clip(s - nf, 0, side_blocks - 1)
        side_in_specs = [pl.BlockSpec((pl.Element(D_MODEL), pl.Element(D_MODEL)), lambda s: (side_start(blk(s)), 0))]
        side_out_specs = [pl.BlockSpec((None, D_MODEL, D_MODEL), lambda s: (blk(s), 0, 0))]
        side_out_shape = [jax.ShapeDtypeStruct((side_blocks, D_MODEL, D_MODEL), BF16)]
        side_args = [side_w]
    row_tile = lambda s: jnp.maximum(s - (nf - 1), 0)
    f_tile = lambda s: jnp.minimum(s, nf - 1)

    def part_spec(tiles, k):
        start = sum(tiles[:k])
        return pl.BlockSpec((FFN_TM, D_MODEL), lambda s: (jnp.clip(row_tile(s) - start, 0, tiles[k] - 1), 0))

    return pl.pallas_call(
        functools.partial(_ffn_kernel, base=base, mod_index=mod_index, in_tiles=in_tiles, out_tiles=out_tiles,
                          side_blocks=side_blocks),
        grid=(nf - 1 + sum(in_tiles),),
        in_specs=[part_spec(in_tiles, k) for k in range(len(xs))] + [
            _resident(mods.shape),
            _resident(g.shape),
            pl.BlockSpec((D_MODEL, FFN_TF), lambda s: (0, f_tile(s))),
            pl.BlockSpec((D_MODEL, FFN_TF), lambda s: (0, f_tile(s))),
            pl.BlockSpec((FFN_TF, D_MODEL), lambda s: (f_tile(s), 0))] + side_in_specs,
        out_specs=[part_spec(out_tiles, k) for k in range(len(out_rows))] + side_out_specs,
        out_shape=[jax.ShapeDtypeStruct((r, D_MODEL), F32) for r in out_rows] + side_out_shape,
        scratch_shapes=[pltpu.VMEM((nf, D_MODEL, FFN_TF), BF16), pltpu.VMEM((nf, D_MODEL, FFN_TF), BF16),
                        pltpu.VMEM((nf, FFN_TF, D_MODEL), BF16),
                        pltpu.VMEM((FFN_TM, D_MODEL), BF16), pltpu.VMEM((FFN_TM, D_MODEL), F32)],
        compiler_params=_params(1),
        name="ffn",
    )(*xs, mods, g, w1, w3, w2, *side_args)


def _group_norm64(x, bd, g):
    ss = _dot(x * x, bd)
    return x * lax.rsqrt(ss * (1.0 / DA_HEAD_DIM) + EPS) * g


ROPE_AXIS_DIM = DA_HEAD_DIM // 2
ROPE_HALF = ROPE_AXIS_DIM // 2


def _rope(x, cos, sin_signed):
    first = (lax.broadcasted_iota(jnp.int32, x.shape, 1) % ROPE_AXIS_DIM) < ROPE_HALF
    partner = jnp.where(first, pltpu.roll(x, LANES - ROPE_HALF, 1), pltpu.roll(x, ROPE_HALF, 1))
    return x * cos + partner * sin_signed


N_SEG = 9
PROJ_WARM = (N_SEG + 1) // 2
GATE_LO = 4 * D_MODEL


def _seg_start(s):
    return s * D_MODEL + N_GATE_COLS * (s >= 4)


def _proj_kernel(*refs, ctx_tiles, row_of_step):
    (x_ref, m_ref, g_ref, wt_ref, bg_ref, bd_ref, gq_ref, gk_ref, cos_ref, sin_ref), refs = refs[:10], refs[10:]
    (mq_ref, mk_ref, mv_ref, so_ref, dq_ref, sgm_ref, sgd_ref, gates_ref,
     dk_ctx_ref, dv_ctx_ref, dk_lat_ref, dv_lat_ref, w_sc, wg_sc, hh_sc) = refs
    step = pl.program_id(0)
    mod = _mod_reader(m_ref, row_of_step(step))
    norm = lambda: _modulated_norm(x_ref[...], g_ref[1:2, :], mod(3), mod(4)).astype(BF16)
    groups = MXU_DIM // DA_HEAD_DIM
    tiled = lambda ref, scale: jnp.concatenate([ref[...] * scale] * groups, axis=1)
    gq_row, gk_row = tiled(gq_ref, QK_LOG2_SCALE), tiled(gk_ref, 1.0)
    bias = jnp.concatenate([bg_ref[...], jnp.zeros((1, GATE_PAD - N_GATE_COLS), F32)], axis=1)

    def gates(hh):
        y = _dot_nt(hh, wg_sc[...]) + bias
        col = lax.broadcasted_iota(jnp.int32, y.shape, 1)
        gates_ref[...] = jnp.where((col // ML_HEADS) % 2 == 1, _log_sigmoid(y), y)

    def qk_segment(y, gain_row, o_ref, rope):
        bd = bd_ref[...]
        for c in range(D_MODEL // MXU_DIM):
            cols = slice(c * MXU_DIM, (c + 1) * MXU_DIM)
            z = _group_norm64(y[:, cols], bd, gain_row)
            if rope:
                z = jnp.concatenate(
                    [_rope(z[:, k * LANES:(k + 1) * LANES], cos_ref[...], sin_ref[...])
                     for k in range(MXU_DIM // LANES)], axis=1)
            o_ref[:, cols] = z.astype(o_ref.dtype)

    def cache_key_segment(hh):
        y_t = _dot_nt(w_sc[5], hh)
        z = y_t.reshape(D_MODEL // DA_HEAD_DIM, DA_HEAD_DIM, y_t.shape[1])
        ms = jnp.mean(z * z, axis=1, keepdims=True)
        gain = jnp.broadcast_to(gk_row[:, 0:LANES], (LANES, LANES)).T[0:DA_HEAD_DIM, :]
        gain = jnp.concatenate([gain] * (y_t.shape[1] // LANES), axis=1)
        dk_ctx_ref[...] = (z * lax.rsqrt(ms + EPS) * gain).reshape(y_t.shape)

    def segment(s, hh, ctx):
        if s in (0, 2):
            o_ref = {0: mq_ref, 2: mv_ref}[s]
            y_t = _dot_nt(w_sc[s], hh) * ((ML_HEAD_DIM ** -0.5) if s == 0 else 1.0)
            for j in range(PROJ_TM // CHUNK):
                o_ref[j] = y_t[:, j * CHUNK:(j + 1) * CHUNK].astype(o_ref.dtype)
            return
        if s == 5 and ctx:
            cache_key_segment(hh)
            return
        y = _dot_nt(hh, w_sc[s])
        if s == 4:
            qk_segment(y, gq_row, dq_ref, rope=not ctx)
        elif s == 5:
            qk_segment(y, gk_row, dk_lat_ref, rope=True)
        elif s in (3, 7, 8):
            {3: so_ref, 7: sgm_ref, 8: sgd_ref}[s][...] = _sigmoid(y)
        elif s == 6 and ctx:
            dv_ctx_ref[...] = y.reshape(y.shape[0], DA_HEADS, DA_V_DIM)
        elif s == 6:
            dv_lat_ref[...] = y.astype(dv_lat_ref.dtype)
        else:
            mk_ref[...] = y.astype(mk_ref.dtype)

    for w in range(PROJ_WARM):
        @pl.when(step == w)
        def _(w=w):
            if w == 0:
                hh_sc[...] = norm()
            for j in range(2):
                s = 2 * w + j
                if s < N_SEG:
                    w_sc[s] = wt_ref[j]
                    segment(s, hh_sc[...], ctx=True)
                else:
                    wg_sc[...] = jnp.zeros(wg_sc.shape, BF16)
                    wg_sc[0:N_GATE_COLS, :] = wt_ref[j, 0:N_GATE_COLS, :]
                    gates(hh_sc[...])

    def full_tile(ctx):
        hh = norm()
        gates(hh)
        for s in range(N_SEG):
            segment(s, hh, ctx)

    is_ctx = step - (PROJ_WARM - 1) < ctx_tiles
    pl.when((step >= PROJ_WARM) & is_ctx)(functools.partial(full_tile, True))
    pl.when((step >= PROJ_WARM) & jnp.logical_not(is_ctx))(functools.partial(full_tile, False))


def _proj(x, n_ctx, seq_ctx, seq_lat, mods, mod_index, g, w_seg, b_gate, bd, gq, gk, rope_tabs):
    rows = x.shape[0]
    n_lat = rows - n_ctx
    assert seq_ctx == PROJ_TM
    ctx_tiles, lat_tiles = n_ctx // PROJ_TM, n_lat // PROJ_TM
    row_tile = lambda s: jnp.maximum(s - (PROJ_WARM - 1), 0)
    ctx_tile = lambda s: jnp.minimum(row_tile(s), ctx_tiles - 1)
    lat_tile = lambda s: jnp.clip(row_tile(s) - ctx_tiles, 0, lat_tiles - 1)
    row = lambda s: (row_tile(s), 0)
    tile = pl.BlockSpec((PROJ_TM, D_MODEL), row)
    w_block = (2, D_MODEL, D_MODEL)
    tab = pl.BlockSpec((PROJ_TM, LANES), lambda s: (lat_tile(s) % (seq_lat // PROJ_TM), 0))
    in_specs = [tile,
                _resident(mods.shape),
                _resident(g.shape),
                pl.BlockSpec(w_block, lambda s: (jnp.minimum(s, PROJ_WARM - 1), 0, 0)),
                _resident(b_gate.shape),
                _resident((MXU_DIM, MXU_DIM)),
                _resident(gq.shape),
                _resident(gk.shape),
                tab, tab]
    slab = pl.BlockSpec((PROJ_TM // CHUNK, D_MODEL, CHUNK), lambda s: (row_tile(s), 0, 0))
    slab_shape = jax.ShapeDtypeStruct((rows // CHUNK, D_MODEL, CHUNK), BF16)
    full = lambda dt: jax.ShapeDtypeStruct((rows, D_MODEL), dt)
    out_specs = [slab, tile, slab, tile, tile, tile, tile, pl.BlockSpec((PROJ_TM, GATE_PAD), row),
                 pl.BlockSpec((None, D_MODEL, seq_ctx), lambda s: (ctx_tile(s), 0, 0)),
                 pl.BlockSpec((PROJ_TM, DA_HEADS, DA_V_DIM), lambda s: (ctx_tile(s), 0, 0)),
                 pl.BlockSpec((PROJ_TM, D_MODEL), lambda s: (lat_tile(s), 0)),
                 pl.BlockSpec((PROJ_TM, D_MODEL), lambda s: (lat_tile(s), 0))]
    out_shape = [slab_shape, full(BF16), slab_shape, full(F32), full(BF16), full(F32), full(F32),
                 jax.ShapeDtypeStruct((rows, GATE_PAD), F32),
                 jax.ShapeDtypeStruct((n_ctx // seq_ctx, D_MODEL, seq_ctx), F32),
                 jax.ShapeDtypeStruct((n_ctx, DA_HEADS, DA_V_DIM), F32),
                 jax.ShapeDtypeStruct((n_lat, D_MODEL), BF16),
                 jax.ShapeDtypeStruct((n_lat, D_MODEL), BF16)]
    return pl.pallas_call(
        functools.partial(_proj_kernel, ctx_tiles=ctx_tiles,
                          row_of_step=lambda step: mod_index(row_tile(step) * PROJ_TM)),
        grid=(PROJ_WARM - 1 + rows // PROJ_TM,),
        in_specs=in_specs,
        out_specs=out_specs,
        out_shape=out_shape,
        scratch_shapes=[pltpu.VMEM((N_SEG, D_MODEL, D_MODEL), BF16), pltpu.VMEM((GATE_PAD, D_MODEL), BF16),
                        pltpu.VMEM((PROJ_TM, D_MODEL), BF16)],
        compiler_params=_params(1),
        name="mixer_in_proj",
    )(x, mods, g, w_seg, b_gate, bd, gq, gk, *rope_tabs)


def _lambda(lam_refs, lam_init):
    q1, k1, q2, k2 = (r[...] for r in lam_refs)
    s1 = jnp.sum(q1 * k1, axis=1, keepdims=True)
    s2 = jnp.sum(q2 * k2, axis=1, keepdims=True)
    return jnp.exp(s1) - jnp.exp(s2) + lam_init


def _attn_kernel(*refs, cached, seq, nb, lam_init):
    if cached:
        (q_ref, k_ref, v_ref, ck_ref, cv_ref), refs = refs[:5], refs[5:]
    else:
        (q_ref, k_ref, v_ref), refs = refs[:3], refs[3:]
    lam_refs, (gs_ref, o_ref, kall_sc, vt_sc, s_sc) = refs[:4], refs[4:]
    units = [(bb, h) for bb in range(nb) for h in range(DA_HEADS)]

    def cache_order_kv(kt_ref, vr_ref, u, h, lo, n):
        kall_sc[u, lo:lo + n, :] = kt_ref[h * DA_V_DIM:(h + 1) * DA_V_DIM, :].T.astype(BF16)
        vt_sc[u, 0:DA_V_DIM, lo:lo + n] = vr_ref[pl.ds(h, n, stride=DA_HEADS), :].T.astype(BF16)

    @pl.when(pl.program_id(1) == 0)
    def _():
        for u, (bb, h) in enumerate(units):
            vt_sc[u, DA_V_DIM:, :] = jnp.ones((ATT_DEN_ROWS, vt_sc.shape[2]), BF16)
            if cached:
                cols = slice(h * DA_V_DIM, (h + 1) * DA_V_DIM)
                kall_sc[u, 0:seq, :] = k_ref[:, cols].astype(BF16)
                vt_sc[u, 0:DA_V_DIM, 0:seq] = v_ref[:, cols].astype(F32).T.astype(BF16)
                cache_order_kv(ck_ref.at[bb], cv_ref.at[bb], u, h, seq, ck_ref.shape[2])
            else:
                cache_order_kv(k_ref.at[bb], v_ref.at[bb], u, h, 0, seq)

    lam = _lambda(lam_refs, lam_init)
    sub_gain = gs_ref[...] * (1.0 - lam_init)
    lane = lax.broadcasted_iota(jnp.int32, (1, DA_V_DIM), 1)
    comp_masks = [lane < DA_HEAD_DIM, lane >= DA_HEAD_DIM]
    tq = q_ref.shape[0] // nb
    n_keys = kall_sc.shape[1]
    n_tiles = pl.cdiv(n_keys, ATT_TK_MAX)
    tk = n_keys // n_tiles

    def stacked_q(u):
        bb, h = units[u]
        q = q_ref[bb * tq:(bb + 1) * tq, h * DA_V_DIM:(h + 1) * DA_V_DIM].astype(BF16)
        return jnp.concatenate([jnp.where(m, q, jnp.zeros_like(q)) for m in comp_masks], axis=0)

    def score_tile(h, j, qq, m8):
        rows = slice(j * tk, (j + 1) * tk)
        st = _dot_nt(kall_sc[h, rows, :], qq)
        s_sc[h % 2, rows, :] = st
        t8 = jnp.max(st.reshape(tk // SUBLANES, SUBLANES, 2 * tq), axis=0)
        return t8 if m8 is None else jnp.maximum(m8, t8)

    def prob_tile(h, j, mx, pv):
        rows = slice(j * tk, (j + 1) * tk)
        e = jnp.exp2(s_sc[h % 2, rows, :] - mx)
        p = _dot(vt_sc[h, :, rows], e)
        return p if pv is None else pv + p

    qq = stacked_q(0)
    m8 = None
    for j in range(n_tiles):
        m8 = score_tile(0, j, qq, m8)
    for h in range(len(units)):
        mx = jnp.max(m8, axis=0, keepdims=True)
        if h + 1 < len(units):
            qq = stacked_q(h + 1)
        m8, pv = None, None
        for j in range(n_tiles):
            if h + 1 < len(units):
                m8 = score_tile(h + 1, j, qq, m8)
            pv = prob_tile(h, j, mx, pv)
        inv = 1.0 / pv[DA_V_DIM:DA_V_DIM + 1, :]
        out_t = pv[:DA_V_DIM, :tq] * inv[:, :tq] - pv[:DA_V_DIM, tq:] * (lam * inv[:, tq:])
        out_t = out_t * lax.rsqrt(jnp.mean(out_t * out_t, axis=0, keepdims=True) + EPS)
        bb, head = units[h]
        o_ref[bb * tq:(bb + 1) * tq, head * DA_V_DIM:(head + 1) * DA_V_DIM] = (out_t.T * sub_gain).astype(o_ref.dtype)


def _attn(q, q_row0, k, v, cache, lam_vecs, g_sub, batch, seq_len, lam_init):
    rows = batch * seq_len
    nq = seq_len // ATT_TQ
    nb = ATT_SHORT_SEQ_BATCH if (cache is None and nq == 1) else 1
    assert batch % nb == 0
    q_spec = pl.BlockSpec((nb * ATT_TQ, D_MODEL), lambda b, i: (q_row0 // (nb * ATT_TQ) + b * nq + i, 0))
    o_spec = pl.BlockSpec((nb * ATT_TQ, D_MODEL), lambda b, i: (b * nq + i, 0))
    cache_specs = lambda n: [pl.BlockSpec((nb, D_MODEL, n), lambda b, i: (b, 0, 0)),
                             pl.BlockSpec((nb, n * DA_HEADS, DA_V_DIM), lambda b, i: (b, 0, 0))]
    n_keys = seq_len
    if cache is not None:
        past = cache[0].shape[2]
        n_keys += past
        kv_spec = pl.BlockSpec((seq_len, D_MODEL), lambda b, i: (b, 0))
        in_specs = [q_spec, kv_spec, kv_spec] + cache_specs(past)
        args = [q, k, v] + list(cache)
    else:
        in_specs = [q_spec] + cache_specs(seq_len)
        args = [q, k, v.reshape(batch, seq_len * DA_HEADS, DA_V_DIM)]
    in_specs += [_resident((1, DA_HEAD_DIM))] * 4 + [_resident((1, DA_V_DIM))]
    args += list(lam_vecs) + [g_sub]
    return pl.pallas_call(
        functools.partial(_attn_kernel, cached=cache is not None, seq=seq_len, nb=nb, lam_init=lam_init),
        grid=(batch // nb, nq),
        in_specs=in_specs,
        out_specs=o_spec,
        out_shape=jax.ShapeDtypeStruct((rows, D_MODEL), BF16),
        scratch_shapes=[pltpu.VMEM((nb * DA_HEADS, n_keys, DA_V_DIM), BF16),
                        pltpu.VMEM((nb * DA_HEADS, DA_V_DIM + ATT_DEN_ROWS, n_keys), BF16),
                        pltpu.VMEM((2, n_keys, 2 * ATT_TQ), F32)],
        compiler_params=_params(2),
        name="diff_attention",
    )(*args)


def _per_chain(fn, a, b):
    return jnp.stack([fn(a[i], b[i]) for i in range(a.shape[0])])


ML_EXT = SUBLANES
ML_SHORT_SEQ_BATCH = 2


def _rows(x):
    return jnp.stack([x[i:i + 1, :] for i in range(x.shape[0])])


def _split3(x):
    hi = x.astype(BF16)
    r = x - hi.astype(F32)
    mid = r.astype(BF16)
    lo = (r - mid.astype(F32)).astype(BF16)
    return jnp.concatenate([hi, mid, lo], axis=1)


def _chunk_scan(x, reverse_rows, tri_prefix, tri_suffix):
    parts = _split3(x)
    return jnp.where(reverse_rows, _dot(parts, tri_suffix), _dot(parts, tri_prefix))


def _paired_value_matmul(v_t, sc):
    B, L, _ = sc.shape
    zero = jnp.zeros((L, L), sc.dtype)
    out = []
    for i in range(0, B, 2):
        lhs = jnp.concatenate([v_t[i], v_t[i + 1]], axis=1)
        rhs = jnp.concatenate([jnp.concatenate([sc[i], zero], axis=1),
                               jnp.concatenate([zero, sc[i + 1]], axis=1)], axis=0)
        both = _dot(lhs, rhs)
        out += [both[:, :L], both[:, L:]]
    return jnp.stack(out)


def _mlstm_step(k, q_t, v_t, ic, fc, CT, m, seen_t, reverse_rows):
    B, L = ic.shape
    d = k.shape[2]
    zero_state = CT is None
    if zero_state:
        m = jnp.zeros((B, 1, 1), F32)
    tri = lambda keep: jnp.concatenate([jnp.where(keep, 1.0, 0.0).astype(BF16)] * 3, axis=0)
    b2 = _chunk_scan(fc, reverse_rows, tri(seen_t[0]), tri(seen_t[B - 1]))
    u2 = ic - b2
    u_t = jnp.concatenate([u2, jnp.zeros((L - B, L), F32)], axis=0).T
    u_col = jnp.stack([jnp.broadcast_to(u_t[:, i:i + 1], (L, L)) for i in range(B)])
    b, i_g, f_g = _rows(b2), _rows(ic), _rows(fc)
    b_last = jnp.sum(f_g, axis=2, keepdims=True)

    log_d = jnp.where(seen_t, b + u_col, -jnp.inf)
    a = b + m
    m_t = jnp.maximum(a, jnp.max(log_d, axis=1, keepdims=True))
    dmat = jnp.exp(log_d - m_t)
    sc = _per_chain(_dot, k, q_t) * dmat
    value_matmul = _paired_value_matmul if 2 * L <= MXU_DIM else functools.partial(_per_chain, _dot)
    num = value_matmul(v_t, sc.astype(BF16))
    den = jnp.sum(sc, axis=1, keepdims=True)
    if not zero_state:
        inter = jnp.exp(a - m_t)
        cq = _per_chain(_dot, CT, q_t)
        num = num + inter * cq[:, :d, :]
        den = den + inter * cq[:, d:d + 1, :]
    h_t = num * (1.0 / jnp.maximum(jnp.abs(den), jnp.exp(-m_t)))

    g = b_last - b + i_g
    m_new = jnp.maximum(b_last + m, jnp.max(g, axis=2, keepdims=True))
    w = jnp.exp(g - m_new)
    decay = jnp.exp(b_last + m - m_new)
    vw = jnp.concatenate([v_t.astype(F32), jnp.ones((B, ML_EXT, L), F32)], axis=1) * w
    CT_new = _per_chain(_dot, vw, k)
    if not zero_state:
        CT_new = decay * CT + CT_new
    return h_t, CT_new, m_new


def _mlstm_kernel(*refs, seq_len, nb, has_state, emit_state):
    q_ref, k_ref, v_ref, so_ref, g_ref, gmh_ref = refs[:6]
    refs = refs[6:]
    if has_state:
        (c0_ref, n0_ref, m0_ref), refs = refs[:3], refs[3:]
    hm_ref, refs = refs[0], refs[1:]
    if emit_state:
        (c_out_ref, n_out_ref, m_out_ref), refs = refs[:3], refs[3:]
    ct_sc, m_sc, gr_sc, h_sc = refs

    nc = seq_len // CHUNK
    d_head = ML_HEAD_DIM
    chains = [(bb, d, h) for bb in range(nb) for d in range(2) for h in range(ML_HEADS)]
    n_chain = len(chains)
    for i, (bb, d, h) in enumerate(chains):
        if has_state:
            ct_sc[i, 0:d_head, :] = c0_ref[bb, d, h].T
            ct_sc[i, d_head:d_head + ML_EXT, :] = jnp.broadcast_to(n0_ref[bb, d, h:h + 1, :], (ML_EXT, d_head))
            m_sc[i] = jnp.full((1, 1), m0_ref[pl.program_id(0) * nb + bb, 0, d, h], F32)
        elif nc > 2:
            ct_sc[i] = jnp.zeros((d_head + ML_EXT, d_head), F32)
            m_sc[i] = jnp.zeros((1, 1), F32)
    for c in range(nb * nc):
        gr_sc[c] = g_ref[c * CHUNK:(c + 1) * CHUNK, :].T

    s_idx = lax.broadcasted_iota(jnp.int32, (CHUNK, CHUNK), 0)
    t_idx = lax.broadcasted_iota(jnp.int32, (CHUNK, CHUNK), 1)
    seen_t = jnp.stack([(s_idx >= t_idx) if d else (s_idx <= t_idx) for _, d, _ in chains])
    reverse_rows = (lax.broadcasted_iota(jnp.int32, (n_chain, CHUNK), 0) // ML_HEADS) % 2 == 1

    def step(c_fwd, c_bwd, rows_of, zero_state=False):
        chunk_of = (c_fwd, c_bwd)
        lo = 2 * ML_HEADS
        ic, fc = [], []
        for bb in range(nb):
            g_fwd, g_bwd = gr_sc[bb * nc + c_fwd], gr_sc[bb * nc + c_bwd]
            ic += [g_fwd[0:ML_HEADS], g_bwd[lo:lo + ML_HEADS]]
            fc += [g_fwd[ML_HEADS:lo], g_bwd[lo + ML_HEADS:2 * lo]]
        head = [slice(h * d_head, (h + 1) * d_head) for _, _, h in chains]
        stack = lambda pick: jnp.stack([pick(i, bb * nc + chunk_of[d]) for i, (bb, d, _) in enumerate(chains)])
        h_t, CT_new, m_new = _mlstm_step(
            stack(lambda i, c: k_ref[rows_of(c), head[i]]),
            stack(lambda i, c: q_ref[c, head[i], :]),
            stack(lambda i, c: v_ref[c, head[i], :]),
            jnp.concatenate(ic, axis=0), jnp.concatenate(fc, axis=0),
            None if zero_state else ct_sc[...], None if zero_state else m_sc[...], seen_t, reverse_rows)
        ct_sc[...] = CT_new
        m_sc[...] = m_new
        for i, (bb, d, _) in enumerate(chains):
            h_sc[d, bb * nc + chunk_of[d], head[i], :] = h_t[i]

    if nc <= 2:
        for c in range(nc):
            step(c, nc - 1 - c, lambda cc: slice(cc * CHUNK, (cc + 1) * CHUNK),
                 zero_state=(c == 0 and not has_state))
    else:
        def body(c, carry):
            step(c, nc - 1 - c, lambda cc: pl.ds(pl.multiple_of(cc * CHUNK, CHUNK), CHUNK))
            return carry
        lax.fori_loop(0, nc, body, 0, unroll=2)

    for h in range(ML_HEADS):
        hcols = slice(h * d_head, (h + 1) * d_head)
        gain = jnp.broadcast_to(gmh_ref[h:h + 1, :], (CHUNK, d_head)).T
        for c in range(nb * nc):
            rows = slice(c * CHUNK, (c + 1) * CHUNK)
            hsum = h_sc[0, c, hcols, :] + h_sc[1, c, hcols, :]
            hn = hsum * lax.rsqrt(jnp.mean(hsum * hsum, axis=0, keepdims=True) + EPS) * gain
            hm_ref[rows, hcols] = (hn.T * so_ref[rows, hcols]).astype(hm_ref.dtype)
    if emit_state:
        for i, (bb, d, h) in enumerate(chains):
            c_out_ref[bb, d, h] = ct_sc[i, 0:d_head, :].T
            n_out_ref[bb, d, h:h + 1, :] = ct_sc[i, d_head:d_head + 1, :]
            m_out_ref[bb, d:d + 1, h:h + 1] = m_sc[i]


def _mlstm(q_t, k, v_t, so, gates, row0, g_mh, state, batch, seq_len, emit_state):
    rows = batch * seq_len
    d = ML_HEAD_DIM
    nc = seq_len // CHUNK
    nb = ML_SHORT_SEQ_BATCH if nc <= 2 else 1
    assert batch % nb == 0
    assert row0 % (nb * seq_len) == 0
    first = row0 // (nb * seq_len)
    tile = pl.BlockSpec((nb * seq_len, D_MODEL), lambda b: (first + b, 0))
    tile_t = pl.BlockSpec((nb * nc, D_MODEL, CHUNK), lambda b: (first + b, 0, 0))
    in_specs = [tile_t, tile, tile_t, tile,
                pl.BlockSpec((nb * seq_len, GATE_PAD), lambda b: (first + b, 0)),
                _resident((ML_HEADS, d))]
    args = [q_t, k, v_t, so, gates, g_mh]
    c_spec = pl.BlockSpec((nb, None, 2, ML_HEADS, d, d), lambda b: (b, 0, 0, 0, 0, 0))
    n_spec = pl.BlockSpec((nb, None, 2, ML_HEADS, d), lambda b: (b, 0, 0, 0, 0))
    if state is not None:
        in_specs += [c_spec, n_spec, pl.BlockSpec(memory_space=pltpu.SMEM)]
        args += list(state)
    out_specs = [pl.BlockSpec((nb * seq_len, D_MODEL), lambda b: (b, 0))]
    out_shape = [jax.ShapeDtypeStruct((rows, D_MODEL), BF16)]
    if emit_state:
        out_specs += [c_spec, n_spec, pl.BlockSpec((nb, None, 2, ML_HEADS), lambda b: (b, 0, 0, 0))]
        out_shape += [jax.ShapeDtypeStruct((batch, 1, 2, ML_HEADS, d, d), F32),
                      jax.ShapeDtypeStruct((batch, 1, 2, ML_HEADS, d), F32),
                      jax.ShapeDtypeStruct((batch, 1, 2, ML_HEADS), F32)]
    n_state = 2 * ML_HEADS * nb
    return pl.pallas_call(
        functools.partial(_mlstm_kernel, seq_len=seq_len, nb=nb, has_state=state is not None,
                          emit_state=emit_state),
        grid=(batch // nb,),
        in_specs=in_specs,
        out_specs=out_specs,
        out_shape=out_shape,
        scratch_shapes=[pltpu.VMEM((n_state, d + ML_EXT, d), F32), pltpu.VMEM((n_state, 1, 1), F32),
                        pltpu.VMEM((nb * nc, GATE_PAD, CHUNK), F32),
                        pltpu.VMEM((2, nb * nc, D_MODEL, CHUNK), F32)],
        compiler_params=_params(1),
        name="mlstm",
    )(*args)


def _merge_kernel(*refs, tiles, mod_index):
    n = len(tiles)
    (x_ref, m_ref, sgm_ref, sgd_ref), refs = refs[:4], refs[4:]
    parts = [refs[j * n:(j + 1) * n] for j in range(2)]
    wm_ref, wd_ref, wo_ref, o_ref, w_sc = refs[2 * n:]
    i = pl.program_id(0)

    @pl.when(i == 0)
    def _():
        for j, w_ref in enumerate((wm_ref, wd_ref, wo_ref)):
            w_sc[j] = w_ref[...].astype(BF16)

    def tile(hm_ref, att_ref):
        y = sgm_ref[...] * _dot(hm_ref[...], w_sc[0]) + sgd_ref[...] * _dot(att_ref[...], w_sc[1])
        o_ref[...] = x_ref[...] + _mod_reader(m_ref, mod_index(i * MERGE_TM))(5) * _dot(y, w_sc[2])

    starts = np.cumsum((0,) + tiles).tolist()
    for k in range(n):
        pl.when((i >= starts[k]) & (i < starts[k + 1]))(functools.partial(tile, *[p[k] for p in parts]))


def _merge(x, mods, mod_index, hm, att, sgm, sgd, wm, wd, wo):
    rows = x.shape[0]
    tiles = tuple(h.shape[0] // MERGE_TM for h in hm)
    assert sum(tiles) * MERGE_TM == rows
    tile = pl.BlockSpec((MERGE_TM, D_MODEL), lambda i: (i, 0))

    def part_spec(k):
        start = sum(tiles[:k])
        return pl.BlockSpec((MERGE_TM, D_MODEL), lambda i: (jnp.clip(i - start, 0, tiles[k] - 1), 0))

    part_specs = [part_spec(k) for k in range(len(tiles))]
    w_spec = _resident((D_MODEL, D_MODEL))
    return pl.pallas_call(
        functools.partial(_merge_kernel, tiles=tiles, mod_index=mod_index),
        grid=(rows // MERGE_TM,),
        in_specs=[tile, _resident(mods.shape), tile, tile]
                 + part_specs * 2 + [w_spec, w_spec, w_spec],
        out_specs=tile,
        out_shape=jax.ShapeDtypeStruct((rows, D_MODEL), F32),
        scratch_shapes=[pltpu.VMEM((3, D_MODEL, D_MODEL), BF16)],
        compiler_params=_params(1),
        name="branch_merge",
    )(x, mods, sgm, sgd, *hm, *att, wm, wd, wo)


def _rope_tables(seq_len):
    lane = np.arange(LANES)
    r = lane % ROPE_AXIS_DIM
    freqs = np.power(np.float32(ROPE_BASE), -(r % ROPE_HALF).astype(np.float32) / np.float32(ROPE_HALF))
    tok = np.arange(seq_len)
    by_row = (lane % DA_HEAD_DIM < ROPE_AXIS_DIM)[None, :]
    pos = np.where(by_row, (tok // GRID_W)[:, None], (tok % GRID_W)[:, None]).astype(np.float32)
    ang = pos * freqs[None, :]
    sign = np.where(r < ROPE_HALF, -1.0, 1.0).astype(np.float32)
    return jnp.asarray(np.cos(ang), F32), jnp.asarray(np.sin(ang) * sign[None, :], F32)


def kernel(x_prompt, x_sample, c, cache_k, cache_v, state_C, state_n, state_m, c_ctx, w_ada, b_ada, g_norm, ffn1_w1, ffn1_w3, ffn1_w2, ffn2_w1, ffn2_w3, ffn2_w2, w_in, b_gate, g_qn, g_kn, lam_q1, lam_k1, lam_q2, lam_k2, g_sub, g_mh, w_br_m, w_br_d, w_out):
    depth = w_ada.shape[0]
    assert depth == 1
    l = 0
    bp, tp, _ = x_prompt.shape
    bs, ts, _ = x_sample.shape
    past = cache_k.shape[2]
    lam_init = 0.8 - 0.6 * math.exp(-0.3 * l)

    cvecs = jnp.concatenate([c_ctx[None, :], c, jnp.zeros((MOD_ROWS - 1 - bs, D_MODEL), F32)], axis=0)
    mods = _mods(cvecs, w_ada[l], b_ada[l])

    group = np.arange(MXU_DIM) // DA_HEAD_DIM
    w = dict(
        g_norm=g_norm[l],
        ffn1_w1=ffn1_w1[l], ffn1_w3=ffn1_w3[l], ffn1_w2=ffn1_w2[l],
        ffn2_w1=ffn2_w1[l], ffn2_w3=ffn2_w3[l], ffn2_w2=ffn2_w2[l],
        w_in_t=w_in[l].T,
        b_gate=b_gate[l:l + 1],
        bd=jnp.asarray(group[:, None] == group[None, :], BF16),
        g_qn=g_qn[l:l + 1], g_kn=g_kn[l:l + 1],
        lam=(lam_q1[l:l + 1], lam_k1[l:l + 1], lam_q2[l:l + 1], lam_k2[l:l + 1]),
        g_sub=g_sub[l:l + 1], g_mh=g_mh[l],
        w_br_m=w_br_m[l], w_br_d=w_br_d[l], w_out=w_out[l],
    )

    n_ctx, n_lat = bp * tp, bs * ts
    mod_index = lambda r: jnp.where(r < n_ctx, 0, 1 + (r - n_ctx) // ts)
    seg_start = lambda j: pl.multiple_of(jnp.where(j < N_SEG, _seg_start(j), GATE_LO), N_GATE_COLS)
    x1, w["w_seg"] = _ffn((x_prompt.reshape(n_ctx, D_MODEL), x_sample.reshape(n_lat, D_MODEL)), (n_ctx + n_lat,),
                          mods, mod_index, w["g_norm"], w["ffn1_w1"], w["ffn1_w3"], w["ffn1_w2"], base=0,
                          side=(w["w_in_t"], N_SEG + 1, seg_start))

    mq, mk, mv, so, dq, sgm, sgd, gates, new_k_t, new_v, dk_lat, dv_lat = _proj(
        x1, n_ctx, tp, ts, mods, mod_index, w["g_norm"], w["w_seg"], w["b_gate"], w["bd"], w["g_qn"], w["g_kn"],
        _rope_tables(ts))
    att_p = _attn(dq, 0, new_k_t, new_v, None, w["lam"], w["g_sub"], bp, tp, lam_init)
    hm_p, new_c, new_n, new_m = _mlstm(mq, mk, mv, so, gates, 0, w["g_mh"], None, bp, tp, emit_state=True)

    cache = (cache_k[:, l].transpose(0, 2, 3, 4, 1).reshape(bs, D_MODEL, past),
             cache_v[:, l].reshape(bs, past * DA_HEADS, DA_V_DIM))
    att_s = _attn(dq, n_ctx, dk_lat, dv_lat, cache, w["lam"], w["g_sub"], bs, ts, lam_init)
    (hm_s,) = _mlstm(mq, mk, mv, so, gates, n_ctx, w["g_mh"], (state_C, state_n, state_m), bs, ts,
                     emit_state=False)

    x2 = _merge(x1, mods, mod_index, (hm_p, hm_s), (att_p, att_s), sgm, sgd, w["w_br_m"], w["w_br_d"], w["w_out"])
    xp, xs = _ffn((x2,), (n_ctx, n_lat), mods, mod_index, w["g_norm"], w["ffn2_w1"], w["ffn2_w3"], w["ffn2_w2"],
                  base=6)

    return (xp.reshape(bp, tp, D_MODEL), xs.reshape(bs, ts, D_MODEL),
            new_k_t.reshape(bp, DA_HEADS, 2, DA_HEAD_DIM, tp).transpose(0, 4, 1, 2, 3)[:, None],
            new_v.reshape(bp, 1, tp, DA_HEADS, DA_V_DIM),
            new_c, new_n, new_m)
```

```python
import functools
import math

import jax
import jax.numpy as jnp
import numpy as np
from jax import lax
from jax.experimental import pallas as pl
from jax.experimental.pallas import tpu as pltpu

F32 = jnp.float32
BF16 = jnp.bfloat16

D_MODEL = 1024
D_FF = 2816
N_MOD = 9
GRID_W = 64
ML_HEADS = 4
ML_HEAD_DIM = 256
DA_HEADS = 8
DA_HEAD_DIM = 64
DA_V_DIM = 128
N_GATE_COLS = 16
CHUNK = 256
ROPE_BASE = 10000.0
QK_LOG2_SCALE = DA_HEAD_DIM ** -0.5 * math.log2(math.e)
EPS = 1e-6

LANES = 128
SUBLANES = 8
GATE_PAD = LANES
MXU_DIM = 256
VMEM_LIMIT = 56 * 1024 * 1024

SIDE_PIECES = 8
MOD_ROWS = SUBLANES
FFN_TM = 512
FFN_TF = 256
PROJ_TM = 256
MERGE_TM = 512
ATT_TQ = 256
ATT_TK_MAX = 768
ATT_SHORT_SEQ_BATCH = 2


def _params(n_axes):
    return pltpu.CompilerParams(dimension_semantics=("arbitrary",) * n_axes,
                                vmem_limit_bytes=VMEM_LIMIT)


def _dot(a, b):
    return jnp.dot(a.astype(BF16), b.astype(BF16), preferred_element_type=F32)


def _dot_nt(a, b):
    return lax.dot_general(a.astype(BF16), b.astype(BF16), (((1,), (1,)), ((), ())),
                           preferred_element_type=F32)


def _sigmoid(x):
    return 1.0 / (1.0 + jnp.exp(-x))


def _log_sigmoid(x):
    return jnp.minimum(x, 0.0) - jnp.log1p(jnp.exp(-jnp.abs(x)))


def _modulated_norm(x, g, shift, scale):
    y = x * lax.rsqrt(jnp.mean(x * x, axis=-1, keepdims=True) + EPS) * g
    return y * (1.0 + scale) + shift


def _resident(shape):
    return pl.BlockSpec(shape, lambda *_: (0,) * len(shape), pipeline_mode=pl.Buffered(1))


MODS_TN = 512
MODS_IN_FLIGHT = 4


def _mods_kernel(c_ref, w_hbm, b_ref, o_ref, w_sc, sem):
    n_chunks = o_ref.shape[1] // MODS_TN

    def copy(j):
        slot = j % MODS_IN_FLIGHT
        return pltpu.make_async_copy(w_hbm.at[:, j * MODS_TN:(j + 1) * MODS_TN], w_sc.at[slot], sem.at[slot])

    for j in range(min(MODS_IN_FLIGHT, n_chunks)):
        copy(j).start()
    c = c_ref[...]
    a = (c * _sigmoid(c)).astype(BF16)
    for j in range(n_chunks):
        cols = slice(j * MODS_TN, (j + 1) * MODS_TN)
        copy(j).wait()
        o_ref[:, cols] = _dot(a, w_sc[j % MODS_IN_FLIGHT]) + b_ref[:, cols]
        if j + MODS_IN_FLIGHT < n_chunks:
            copy(j + MODS_IN_FLIGHT).start()


def _mods(cvecs, w_ada, b_ada):
    n = N_MOD * D_MODEL
    assert n % MODS_TN == 0
    return pl.pallas_call(
        _mods_kernel,
        in_specs=[pl.BlockSpec(memory_space=pltpu.VMEM),
                  pl.BlockSpec(memory_space=pl.ANY),
                  pl.BlockSpec(memory_space=pltpu.VMEM)],
        out_specs=pl.BlockSpec(memory_space=pltpu.VMEM),
        out_shape=jax.ShapeDtypeStruct((MOD_ROWS, n), F32),
        scratch_shapes=[pltpu.VMEM((MODS_IN_FLIGHT, D_MODEL, MODS_TN), F32),
                        pltpu.SemaphoreType.DMA((MODS_IN_FLIGHT,))],
        compiler_params=pltpu.CompilerParams(vmem_limit_bytes=VMEM_LIMIT),
        name="adaln_mods",
    )(cvecs, w_ada, b_ada.reshape(1, n))


def _mod_reader(m_ref, row):
    return lambda j: m_ref[pl.ds(row, 1), j * D_MODEL:(j + 1) * D_MODEL]


def _ffn_kernel(*refs, base, mod_index, in_tiles, out_tiles, side_blocks):
    n_in, n_out = len(in_tiles), len(out_tiles)
    x_refs, refs = refs[:n_in], refs[n_in:]
    (m_ref, g_ref, w1_ref, w3_ref, w2_ref), refs = refs[:5], refs[5:]
    if side_blocks:
        side_in_ref, refs = refs[0], refs[1:]
        side_out_ref, refs = refs[n_out], refs[:n_out] + refs[n_out + 1:]
    o_refs, (w1_sc, w3_sc, w2_sc, hh_sc, acc_sc) = refs[:n_out], refs[n_out:]
    g = pl.program_id(0)
    nf = D_FF // FFN_TF
    row_tile = g - (nf - 1)
    mod = _mod_reader(m_ref, mod_index(jnp.maximum(row_tile, 0) * FFN_TM))
    gain = g_ref[base // 3:base // 3 + 1, :]
    norm = lambda x: _modulated_norm(x, gain, mod(base), mod(base + 1)).astype(BF16)
    finish = lambda x, acc: x + 0.5 * mod(base + 2) * acc

    def tile(hh, f):
        a = _dot(hh, w1_sc[f])
        b = _dot(hh, w3_sc[f])
        return _dot(a * _sigmoid(a) * b, w2_sc[f])

    @pl.when(g < nf)
    def _():
        w1_sc[g] = w1_ref[...].astype(BF16)
        w3_sc[g] = w3_ref[...].astype(BF16)
        w2_sc[g] = w2_ref[...].astype(BF16)

        @pl.when(g == 0)
        def _():
            hh_sc[...] = norm(x_refs[0][...])
            acc_sc[...] = jnp.zeros(acc_sc.shape, F32)

        acc_sc[...] += tile(hh_sc[...], g)

        @pl.when(g == nf - 1)
        def _():
            o_refs[0][...] = finish(x_refs[0][...], acc_sc[...])

    def full_tile(x_ref, o_ref):
        x = x_ref[...]
        hh = norm(x)
        acc = jnp.zeros(x.shape, F32)
        for f in range(nf):
            acc = acc + tile(hh, f)
            if side_blocks and f < SIDE_PIECES:
                rows = slice(f * (D_MODEL // SIDE_PIECES), (f + 1) * (D_MODEL // SIDE_PIECES))
                side_out_ref[rows, :] = side_in_ref[rows, :].astype(BF16)
        o_ref[...] = finish(x, acc)

    bounds = sorted(set(np.cumsum((0,) + in_tiles).tolist()) | set(np.cumsum((0,) + out_tiles).tolist()))
    for lo, hi in zip(bounds[:-1], bounds[1:]):
        k_in = int(np.searchsorted(np.cumsum(in_tiles), lo, side="right"))
        k_out = int(np.searchsorted(np.cumsum(out_tiles), lo, side="right"))
        pl.when((g >= nf) & (row_tile >= lo) & (row_tile < hi))(
            functools.partial(full_tile, x_refs[k_in], o_refs[k_out]))


def _ffn(xs, out_rows, mods, mod_index, g, w1, w3, w2, base, side=None):
    nf = D_FF // FFN_TF
    in_tiles = tuple(x.shape[0] // FFN_TM for x in xs)
    out_tiles = tuple(r // FFN_TM for r in out_rows)
    assert sum(in_tiles) == sum(out_tiles)
    side_in_specs, side_out_specs, side_out_shape, side_args, side_blocks = [], [], [], [], 0
    if side is not None:
        side_w, side_blocks, side_start = side
        assert side_blocks <= sum(in_tiles) - 1
        blk = lambda s: jnp.clip(s - nf, 0, side_blocks - 1)
        side_in_specs = [pl.BlockSpec((pl.Element(D_MODEL), pl.Element(D_MODEL)), lambda s: (side_start(blk(s)), 0))]
        side_out_specs = [pl.BlockSpec((None, D_MODEL, D_MODEL), lambda s: (blk(s), 0, 0))]
        side_out_shape = [jax.ShapeDtypeStruct((side_blocks, D_MODEL, D_MODEL), BF16)]
        side_args = [side_w]
    row_tile = lambda s: jnp.maximum(s - (nf - 1), 0)
    f_tile = lambda s: jnp.minimum(s, nf - 1)

    def part_spec(tiles, k):
        start = sum(tiles[:k])
        return pl.BlockSpec((FFN_TM, D_MODEL), lambda s: (jnp.clip(row_tile(s) - start, 0, tiles[k] - 1), 0))

    return pl.pallas_call(
        functools.partial(_ffn_kernel, base=base, mod_index=mod_index, in_tiles=in_tiles, out_tiles=out_tiles,
                          side_blocks=side_blocks),
        grid=(nf - 1 + sum(in_tiles),),
        in_specs=[part_spec(in_tiles, k) for k in range(len(xs))] + [
            _resident(mods.shape),
            _resident(g.shape),
            pl.BlockSpec((D_MODEL, FFN_TF), lambda s: (0, f_tile(s))),
            pl.BlockSpec((D_MODEL, FFN_TF), lambda s: (0, f_tile(s))),
            pl.BlockSpec((FFN_TF, D_MODEL), lambda s: (f_tile(s), 0))] + side_in_specs,
        out_specs=[part_spec(out_tiles, k) for k in range(len(out_rows))] + side_out_specs,
        out_shape=[jax.ShapeDtypeStruct((r, D_MODEL), F32) for r in out_rows] + side_out_shape,
        scratch_shapes=[pltpu.VMEM((nf, D_MODEL, FFN_TF), BF16), pltpu.VMEM((nf, D_MODEL, FFN_TF), BF16),
                        pltpu.VMEM((nf, FFN_TF, D_MODEL), BF16),
                        pltpu.VMEM((FFN_TM, D_MODEL), BF16), pltpu.VMEM((FFN_TM, D_MODEL), F32)],
        compiler_params=_params(1),
        name="ffn",
    )(*xs, mods, g, w1, w3, w2, *side_args)


def _group_norm64(x, bd, g):
    ss = _dot(x * x, bd)
    return x * lax.rsqrt(ss * (1.0 / DA_HEAD_DIM) + EPS) * g


ROPE_AXIS_DIM = DA_HEAD_DIM // 2
ROPE_HALF = ROPE_AXIS_DIM // 2


def _rope(x, cos, sin_signed):
    first = (lax.broadcasted_iota(jnp.int32, x.shape, 1) % ROPE_AXIS_DIM) < ROPE_HALF
    partner = jnp.where(first, pltpu.roll(x, LANES - ROPE_HALF, 1), pltpu.roll(x, ROPE_HALF, 1))
    return x * cos + partner * sin_signed


N_SEG = 9
PROJ_WARM = (N_SEG + 1) // 2
GATE_LO = 4 * D_MODEL


def _seg_start(s):
    return s * D_MODEL + N_GATE_COLS * (s >= 4)


def _proj_kernel(*refs, ctx_tiles, row_of_step):
    (x_ref, m_ref, g_ref, wt_ref, bg_ref, bd_ref, gq_ref, gk_ref, cos_ref, sin_ref), refs = refs[:10], refs[10:]
    (mq_ref, mk_ref, mv_ref, so_ref, dq_ref, sgm_ref, sgd_ref, gates_ref,
     dk_ctx_ref, dv_ctx_ref, dk_lat_ref, dv_lat_ref, w_sc, wg_sc, hh_sc) = refs
    step = pl.program_id(0)
    mod = _mod_reader(m_ref, row_of_step(step))
    norm = lambda: _modulated_norm(x_ref[...], g_ref[1:2, :], mod(3), mod(4)).astype(BF16)
    groups = MXU_DIM // DA_HEAD_DIM
    tiled = lambda ref, scale: jnp.concatenate([ref[...] * scale] * groups, axis=1)
    gq_row, gk_row = tiled(gq_ref, QK_LOG2_SCALE), tiled(gk_ref, 1.0)
    bias = jnp.concatenate([bg_ref[...], jnp.zeros((1, GATE_PAD - N_GATE_COLS), F32)], axis=1)

    def gates(hh):
        y = _dot_nt(hh, wg_sc[...]) + bias
        col = lax.broadcasted_iota(jnp.int32, y.shape, 1)
        gates_ref[...] = jnp.where((col // ML_HEADS) % 2 == 1, _log_sigmoid(y), y)

    def qk_segment(y, gain_row, o_ref, rope):
        bd = bd_ref[...]
        for c in range(D_MODEL // MXU_DIM):
            cols = slice(c * MXU_DIM, (c + 1) * MXU_DIM)
            z = _group_norm64(y[:, cols], bd, gain_row)
            if rope:
                z = jnp.concatenate(
                    [_rope(z[:, k * LANES:(k + 1) * LANES], cos_ref[...], sin_ref[...])
                     for k in range(MXU_DIM // LANES)], axis=1)
            o_ref[:, cols] = z.astype(o_ref.dtype)

    def cache_key_segment(hh):
        y_t = _dot_nt(w_sc[5], hh)
        z = y_t.reshape(D_MODEL // DA_HEAD_DIM, DA_HEAD_DIM, y_t.shape[1])
        ms = jnp.mean(z * z, axis=1, keepdims=True)
        gain = jnp.broadcast_to(gk_row[:, 0:LANES], (LANES, LANES)).T[0:DA_HEAD_DIM, :]
        gain = jnp.concatenate([gain] * (y_t.shape[1] // LANES), axis=1)
        dk_ctx_ref[...] = (z * lax.rsqrt(ms + EPS) * gain).reshape(y_t.shape)

    def segment(s, hh, ctx):
        if s in (0, 2):
            o_ref = {0: mq_ref, 2: mv_ref}[s]
            y_t = _dot_nt(w_sc[s], hh) * ((ML_HEAD_DIM ** -0.5) if s == 0 else 1.0)
            for j in range(PROJ_TM // CHUNK):
                o_ref[j] = y_t[:, j * CHUNK:(j + 1) * CHUNK].astype(o_ref.dtype)
            return
        if s == 5 and ctx:
            cache_key_segment(hh)
            return
        y = _dot_nt(hh, w_sc[s])
        if s == 4:
            qk_segment(y, gq_row, dq_ref, rope=not ctx)
        elif s == 5:
            qk_segment(y, gk_row, dk_lat_ref, rope=True)
        elif s in (3, 7, 8):
            {3: so_ref, 7: sgm_ref, 8: sgd_ref}[s][...] = _sigmoid(y)
        elif s == 6 and ctx:
            dv_ctx_ref[...] = y.reshape(y.shape[0], DA_HEADS, DA_V_DIM)
        elif s == 6:
            dv_lat_ref[...] = y.astype(dv_lat_ref.dtype)
        else:
            mk_ref[...] = y.astype(mk_ref.dtype)

    for w in range(PROJ_WARM):
        @pl.when(step == w)
        def _(w=w):
            if w == 0:
                hh_sc[...] = norm()
            for j in range(2):
                s = 2 * w + j
                if s < N_SEG:
                    w_sc[s] = wt_ref[j]
                    segment(s, hh_sc[...], ctx=True)
                else:
                    wg_sc[...] = jnp.zeros(wg_sc.shape, BF16)
                    wg_sc[0:N_GATE_COLS, :] = wt_ref[j, 0:N_GATE_COLS, :]
                    gates(hh_sc[...])

    def full_tile(ctx):
        hh = norm()
        gates(hh)
        for s in range(N_SEG):
            segment(s, hh, ctx)

    is_ctx = step - (PROJ_WARM - 1) < ctx_tiles
    pl.when((step >= PROJ_WARM) & is_ctx)(functools.partial(full_tile, True))
    pl.when((step >= PROJ_WARM) & jnp.logical_not(is_ctx))(functools.partial(full_tile, False))


def _proj(x, n_ctx, seq_ctx, seq_lat, mods, mod_index, g, w_seg, b_gate, bd, gq, gk, rope_tabs):
    rows = x.shape[0]
    n_lat = rows - n_ctx
    assert seq_ctx == PROJ_TM
    ctx_tiles, lat_tiles = n_ctx // PROJ_TM, n_lat // PROJ_TM
    row_tile = lambda s: jnp.maximum(s - (PROJ_WARM - 1), 0)
    ctx_tile = lambda s: jnp.minimum(row_tile(s), ctx_tiles - 1)
    lat_tile = lambda s: jnp.clip(row_tile(s) - ctx_tiles, 0, lat_tiles - 1)
    row = lambda s: (row_tile(s), 0)
    tile = pl.BlockSpec((PROJ_TM, D_MODEL), row)
    w_block = (2, D_MODEL, D_MODEL)
    tab = pl.BlockSpec((PROJ_TM, LANES), lambda s: (lat_tile(s) % (seq_lat // PROJ_TM), 0))
    in_specs = [tile,
                _resident(mods.shape),
                _resident(g.shape),
                pl.BlockSpec(w_block, lambda s: (jnp.minimum(s, PROJ_WARM - 1), 0, 0)),
                _resident(b_gate.shape),
                _resident((MXU_DIM, MXU_DIM)),
                _resident(gq.shape),
                _resident(gk.shape),
                tab, tab]
    slab = pl.BlockSpec((PROJ_TM // CHUNK, D_MODEL, CHUNK), lambda s: (row_tile(s), 0, 0))
    slab_shape = jax.ShapeDtypeStruct((rows // CHUNK, D_MODEL, CHUNK), BF16)
    full = lambda dt: jax.ShapeDtypeStruct((rows, D_MODEL), dt)
    out_specs = [slab, tile, slab, tile, tile, tile, tile, pl.BlockSpec((PROJ_TM, GATE_PAD), row),
                 pl.BlockSpec((None, D_MODEL, seq_ctx), lambda s: (ctx_tile(s), 0, 0)),
                 pl.BlockSpec((PROJ_TM, DA_HEADS, DA_V_DIM), lambda s: (ctx_tile(s), 0, 0)),
                 pl.BlockSpec((PROJ_TM, D_MODEL), lambda s: (lat_tile(s), 0)),
                 pl.BlockSpec((PROJ_TM, D_MODEL), lambda s: (lat_tile(s), 0))]
    out_shape = [slab_shape, full(BF16), slab_shape, full(F32), full(BF16), full(F32), full(F32),
                 jax.ShapeDtypeStruct((rows, GATE_PAD), F32),
                 jax.ShapeDtypeStruct((n_ctx // seq_ctx, D_MODEL, seq_ctx), F32),
                 jax.ShapeDtypeStruct((n_ctx, DA_HEADS, DA_V_DIM), F32),
                 jax.ShapeDtypeStruct((n_lat, D_MODEL), BF16),
                 jax.ShapeDtypeStruct((n_lat, D_MODEL), BF16)]
    return pl.pallas_call(
        functools.partial(_proj_kernel, ctx_tiles=ctx_tiles,
                          row_of_step=lambda step: mod_index(row_tile(step) * PROJ_TM)),
        grid=(PROJ_WARM - 1 + rows // PROJ_TM,),
        in_specs=in_specs,
        out_specs=out_specs,
        out_shape=out_shape,
        scratch_shapes=[pltpu.VMEM((N_SEG, D_MODEL, D_MODEL), BF16), pltpu.VMEM((GATE_PAD, D_MODEL), BF16),
                        pltpu.VMEM((PROJ_TM, D_MODEL), BF16)],
        compiler_params=_params(1),
        name="mixer_in_proj",
    )(x, mods, g, w_seg, b_gate, bd, gq, gk, *rope_tabs)


def _lambda(lam_refs, lam_init):
    q1, k1, q2, k2 = (r[...] for r in lam_refs)
    s1 = jnp.sum(q1 * k1, axis=1, keepdims=True)
    s2 = jnp.sum(q2 * k2, axis=1, keepdims=True)
    return jnp.exp(s1) - jnp.exp(s2) + lam_init


def _attn_kernel(*refs, cached, seq, nb, lam_init):
    if cached:
        (q_ref, k_ref, v_ref, ck_ref, cv_ref), refs = refs[:5], refs[5:]
    else:
        (q_ref, k_ref, v_ref), refs = refs[:3], refs[3:]
    lam_refs, (gs_ref, o_ref, kall_sc, vt_sc, s_sc) = refs[:4], refs[4:]
    units = [(bb, h) for bb in range(nb) for h in range(DA_HEADS)]

    def cache_order_kv(kt_ref, vr_ref, u, h, lo, n):
        kall_sc[u, lo:lo + n, :] = kt_ref[h * DA_V_DIM:(h + 1) * DA_V_DIM, :].T.astype(BF16)
        vt_sc[u, :, lo:lo + n] = vr_ref[pl.ds(h, n, stride=DA_HEADS), :].T.astype(BF16)

    @pl.when(pl.program_id(1) == 0)
    def _():
        for u, (bb, h) in enumerate(units):
            if cached:
                cols = slice(h * DA_V_DIM, (h + 1) * DA_V_DIM)
                kall_sc[u, 0:seq, :] = k_ref[:, cols].astype(BF16)
                vt_sc[u, :, 0:seq] = v_ref[:, cols].astype(F32).T.astype(BF16)
                cache_order_kv(ck_ref.at[bb], cv_ref.at[bb], u, h, seq, ck_ref.shape[2])
            else:
                cache_order_kv(k_ref.at[bb], v_ref.at[bb], u, h, 0, seq)

    lam = _lambda(lam_refs, lam_init)
    sub_gain = gs_ref[...] * (1.0 - lam_init)
    lane = lax.broadcasted_iota(jnp.int32, (1, DA_V_DIM), 1)
    comp_masks = [lane < DA_HEAD_DIM, lane >= DA_HEAD_DIM]
    tq = q_ref.shape[0] // nb
    n_keys = kall_sc.shape[1]
    n_tiles = pl.cdiv(n_keys, ATT_TK_MAX)
    tk = n_keys // n_tiles

    def stacked_q(u):
        bb, h = units[u]
        q = q_ref[bb * tq:(bb + 1) * tq, h * DA_V_DIM:(h + 1) * DA_V_DIM].astype(BF16)
        return jnp.concatenate([jnp.where(m, q, jnp.zeros_like(q)) for m in comp_masks], axis=0)

    def score_tile(h, j, qq, m8):
        rows = slice(j * tk, (j + 1) * tk)
        st = _dot_nt(kall_sc[h, rows, :], qq)
        s_sc[h % 2, rows, :] = st
        t8 = jnp.max(st.reshape(tk // SUBLANES, SUBLANES, 2 * tq), axis=0)
        return t8 if m8 is None else jnp.maximum(m8, t8)

    def prob_tile(h, j, mx, d8, pv):
        rows = slice(j * tk, (j + 1) * tk)
        e = jnp.exp2(s_sc[h % 2, rows, :] - mx)
        s8 = jnp.sum(e.reshape(tk // SUBLANES, SUBLANES, 2 * tq), axis=0)
        p = _dot(vt_sc[h, :, rows], e)
        return (s8 if d8 is None else d8 + s8), (p if pv is None else pv + p)

    qq = stacked_q(0)
    m8 = None
    for j in range(n_tiles):
        m8 = score_tile(0, j, qq, m8)
    for h in range(len(units)):
        mx = jnp.max(m8, axis=0, keepdims=True)
        if h + 1 < len(units):
            qq = stacked_q(h + 1)
        m8, d8, pv = None, None, None
        for j in range(n_tiles):
            if h + 1 < len(units):
                m8 = score_tile(h + 1, j, qq, m8)
            d8, pv = prob_tile(h, j, mx, d8, pv)
        inv = 1.0 / jnp.sum(d8, axis=0, keepdims=True)
        out_t = pv[:, :tq] * inv[:, :tq] - pv[:, tq:] * (lam * inv[:, tq:])
        out_t = out_t * lax.rsqrt(jnp.mean(out_t * out_t, axis=0, keepdims=True) + EPS)
        bb, head = units[h]
        o_ref[bb * tq:(bb + 1) * tq, head * DA_V_DIM:(head + 1) * DA_V_DIM] = (out_t.T * sub_gain).astype(o_ref.dtype)


def _attn(q, q_row0, k, v, cache, lam_vecs, g_sub, batch, seq_len, lam_init):
    rows = batch * seq_len
    nq = seq_len // ATT_TQ
    nb = ATT_SHORT_SEQ_BATCH if (cache is None and nq == 1) else 1
    assert batch % nb == 0
    q_spec = pl.BlockSpec((nb * ATT_TQ, D_MODEL), lambda b, i: (q_row0 // (nb * ATT_TQ) + b * nq + i, 0))
    o_spec = pl.BlockSpec((nb * ATT_TQ, D_MODEL), lambda b, i: (b * nq + i, 0))
    cache_specs = lambda n: [pl.BlockSpec((nb, D_MODEL, n), lambda b, i: (b, 0, 0)),
                             pl.BlockSpec((nb, n * DA_HEADS, DA_V_DIM), lambda b, i: (b, 0, 0))]
    n_keys = seq_len
    if cache is not None:
        past = cache[0].shape[2]
        n_keys += past
        kv_spec = pl.BlockSpec((seq_len, D_MODEL), lambda b, i: (b, 0))
        in_specs = [q_spec, kv_spec, kv_spec] + cache_specs(past)
        args = [q, k, v] + list(cache)
    else:
        in_specs = [q_spec] + cache_specs(seq_len)
        args = [q, k, v.reshape(batch, seq_len * DA_HEADS, DA_V_DIM)]
    in_specs += [_resident((1, DA_HEAD_DIM))] * 4 + [_resident((1, DA_V_DIM))]
    args += list(lam_vecs) + [g_sub]
    return pl.pallas_call(
        functools.partial(_attn_kernel, cached=cache is not None, seq=seq_len, nb=nb, lam_init=lam_init),
        grid=(batch // nb, nq),
        in_specs=in_specs,
        out_specs=o_spec,
        out_shape=jax.ShapeDtypeStruct((rows, D_MODEL), BF16),
        scratch_shapes=[pltpu.VMEM((nb * DA_HEADS, n_keys, DA_V_DIM), BF16),
                        pltpu.VMEM((nb * DA_HEADS, DA_V_DIM, n_keys), BF16),
                        pltpu.VMEM((2, n_keys, 2 * ATT_TQ), F32)],
        compiler_params=_params(2),
        name="diff_attention",
    )(*args)


def _per_chain(fn, a, b):
    return jnp.stack([fn(a[i], b[i]) for i in range(a.shape[0])])


ML_EXT = SUBLANES
ML_SHORT_SEQ_BATCH = 2


def _rows(x):
    return jnp.stack([x[i:i + 1, :] for i in range(x.shape[0])])


def _split3(x):
    hi = x.astype(BF16)
    r = x - hi.astype(F32)
    mid = r.astype(BF16)
    lo = (r - mid.astype(F32)).astype(BF16)
    return jnp.concatenate([hi, mid, lo], axis=1)


def _chunk_scan(x, reverse_rows, tri_prefix, tri_suffix):
    parts = _split3(x)
    return jnp.where(reverse_rows, _dot(parts, tri_suffix), _dot(parts, tri_prefix))


def _paired_value_matmul(v_t, sc):
    B, L, _ = sc.shape
    zero = jnp.zeros((L, L), sc.dtype)
    out = []
    for i in range(0, B, 2):
        lhs = jnp.concatenate([v_t[i], v_t[i + 1]], axis=1)
        rhs = jnp.concatenate([jnp.concatenate([sc[i], zero], axis=1),
                               jnp.concatenate([zero, sc[i + 1]], axis=1)], axis=0)
        both = _dot(lhs, rhs)
        out += [both[:, :L], both[:, L:]]
    return jnp.stack(out)


def _mlstm_step(k, q_t, v_t, ic, fc, CT, m, seen_t, reverse_rows):
    B, L = ic.shape
    d = k.shape[2]
    zero_state = CT is None
    if zero_state:
        m = jnp.zeros((B, 1, 1), F32)
    tri = lambda keep: jnp.concatenate([jnp.where(keep, 1.0, 0.0).astype(BF16)] * 3, axis=0)
    b2 = _chunk_scan(fc, reverse_rows, tri(seen_t[0]), tri(seen_t[B - 1]))
    u2 = ic - b2
    u_t = jnp.concatenate([u2, jnp.zeros((L - B, L), F32)], axis=0).T
    u_col = jnp.stack([jnp.broadcast_to(u_t[:, i:i + 1], (L, L)) for i in range(B)])
    b, i_g, f_g = _rows(b2), _rows(ic), _rows(fc)
    b_last = jnp.sum(f_g, axis=2, keepdims=True)

    log_d = jnp.where(seen_t, b + u_col, -jnp.inf)
    a = b + m
    m_t = jnp.maximum(a, jnp.max(log_d, axis=1, keepdims=True))
    dmat = jnp.exp(log_d - m_t)
    sc = _per_chain(_dot, k, q_t) * dmat
    value_matmul = _paired_value_matmul if 2 * L <= MXU_DIM else functools.partial(_per_chain, _dot)
    num = value_matmul(v_t, sc.astype(BF16))
    den = jnp.sum(sc, axis=1, keepdims=True)
    if not zero_state:
        inter = jnp.exp(a - m_t)
        cq = _per_chain(_dot, CT, q_t)
        num = num + inter * cq[:, :d, :]
        den = den + inter * cq[:, d:d + 1, :]
    h_t = num * (1.0 / jnp.maximum(jnp.abs(den), jnp.exp(-m_t)))

    g = b_last - b + i_g
    m_new = jnp.maximum(b_last + m, jnp.max(g, axis=2, keepdims=True))
    w = jnp.exp(g - m_new)
    decay = jnp.exp(b_last + m - m_new)
    vw = jnp.concatenate([v_t.astype(F32), jnp.ones((B, ML_EXT, L), F32)], axis=1) * w
    CT_new = _per_chain(_dot, vw, k)
    if not zero_state:
        CT_new = decay * CT + CT_new
    return h_t, CT_new, m_new


def _mlstm_kernel(*refs, seq_len, nb, has_state, emit_state):
    q_ref, k_ref, v_ref, so_ref, g_ref, gmh_ref = refs[:6]
    refs = refs[6:]
    if has_state:
        (c0_ref, n0_ref, m0_ref), refs = refs[:3], refs[3:]
    hm_ref, refs = refs[0], refs[1:]
    if emit_state:
        (c_out_ref, n_out_ref, m_out_ref), refs = refs[:3], refs[3:]
    ct_sc, m_sc, gr_sc, h_sc = refs

    nc = seq_len // CHUNK
    d_head = ML_HEAD_DIM
    chains = [(bb, d, h) for bb in range(nb) for d in range(2) for h in range(ML_HEADS)]
    n_chain = len(chains)
    for i, (bb, d, h) in enumerate(chains):
        if has_state:
            ct_sc[i, 0:d_head, :] = c0_ref[bb, d, h].T
            ct_sc[i, d_head:d_head + ML_EXT, :] = jnp.broadcast_to(n0_ref[bb, d, h:h + 1, :], (ML_EXT, d_head))
            m_sc[i] = jnp.full((1, 1), m0_ref[pl.program_id(0) * nb + bb, 0, d, h], F32)
        elif nc > 2:
            ct_sc[i] = jnp.zeros((d_head + ML_EXT, d_head), F32)
            m_sc[i] = jnp.zeros((1, 1), F32)
    for c in range(nb * nc):
        gr_sc[c] = g_ref[c * CHUNK:(c + 1) * CHUNK, :].T

    s_idx = lax.broadcasted_iota(jnp.int32, (CHUNK, CHUNK), 0)
    t_idx = lax.broadcasted_iota(jnp.int32, (CHUNK, CHUNK), 1)
    seen_t = jnp.stack([(s_idx >= t_idx) if d else (s_idx <= t_idx) for _, d, _ in chains])
    reverse_rows = (lax.broadcasted_iota(jnp.int32, (n_chain, CHUNK), 0) // ML_HEADS) % 2 == 1

    def step(c_fwd, c_bwd, rows_of, zero_state=False):
        chunk_of = (c_fwd, c_bwd)
        lo = 2 * ML_HEADS
        ic, fc = [], []
        for bb in range(nb):
            g_fwd, g_bwd = gr_sc[bb * nc + c_fwd], gr_sc[bb * nc + c_bwd]
            ic += [g_fwd[0:ML_HEADS], g_bwd[lo:lo + ML_HEADS]]
            fc += [g_fwd[ML_HEADS:lo], g_bwd[lo + ML_HEADS:2 * lo]]
        head = [slice(h * d_head, (h + 1) * d_head) for _, _, h in chains]
        stack = lambda pick: jnp.stack([pick(i, bb * nc + chunk_of[d]) for i, (bb, d, _) in enumerate(chains)])
        h_t, CT_new, m_new = _mlstm_step(
            stack(lambda i, c: k_ref[rows_of(c), head[i]]),
            stack(lambda i, c: q_ref[c, head[i], :]),
            stack(lambda i, c: v_ref[c, head[i], :]),
            jnp.concatenate(ic, axis=0), jnp.concatenate(fc, axis=0),
            None if zero_state else ct_sc[...], None if zero_state else m_sc[...], seen_t, reverse_rows)
        ct_sc[...] = CT_new
        m_sc[...] = m_new
        for i, (bb, d, _) in enumerate(chains):
            h_sc[d, bb * nc + chunk_of[d], head[i], :] = h_t[i]

    if nc <= 2:
        for c in range(nc):
            step(c, nc - 1 - c, lambda cc: slice(cc * CHUNK, (cc + 1) * CHUNK),
                 zero_state=(c == 0 and not has_state))
    else:
        def body(c, carry):
            step(c, nc - 1 - c, lambda cc: pl.ds(pl.multiple_of(cc * CHUNK, CHUNK), CHUNK))
            return carry
        lax.fori_loop(0, nc, body, 0, unroll=2)

    for h in range(ML_HEADS):
        hcols = slice(h * d_head, (h + 1) * d_head)
        gain = jnp.broadcast_to(gmh_ref[h:h + 1, :], (CHUNK, d_head)).T
        for c in range(nb * nc):
            rows = slice(c * CHUNK, (c + 1) * CHUNK)
            hsum = h_sc[0, c, hcols, :] + h_sc[1, c, hcols, :]
            hn = hsum * lax.rsqrt(jnp.mean(hsum * hsum, axis=0, keepdims=True) + EPS) * gain
            hm_ref[rows, hcols] = (hn.T * so_ref[rows, hcols]).astype(hm_ref.dtype)
    if emit_state:
        for i, (bb, d, h) in enumerate(chains):
            c_out_ref[bb, d, h] = ct_sc[i, 0:d_head, :].T
            n_out_ref[bb, d, h:h + 1, :] = ct_sc[i, d_head:d_head + 1, :]
            m_out_ref[bb, d:d + 1, h:h + 1] = m_sc[i]


def _mlstm(q_t, k, v_t, so, gates, row0, g_mh, state, batch, seq_len, emit_state):
    rows = batch * seq_len
    d = ML_HEAD_DIM
    nc = seq_len // CHUNK
    nb = ML_SHORT_SEQ_BATCH if nc <= 2 else 1
    assert batch % nb == 0
    assert row0 % (nb * seq_len) == 0
    first = row0 // (nb * seq_len)
    tile = pl.BlockSpec((nb * seq_len, D_MODEL), lambda b: (first + b, 0))
    tile_t = pl.BlockSpec((nb * nc, D_MODEL, CHUNK), lambda b: (first + b, 0, 0))
    in_specs = [tile_t, tile, tile_t, tile,
                pl.BlockSpec((nb * seq_len, GATE_PAD), lambda b: (first + b, 0)),
                _resident((ML_HEADS, d))]
    args = [q_t, k, v_t, so, gates, g_mh]
    c_spec = pl.BlockSpec((nb, None, 2, ML_HEADS, d, d), lambda b: (b, 0, 0, 0, 0, 0))
    n_spec = pl.BlockSpec((nb, None, 2, ML_HEADS, d), lambda b: (b, 0, 0, 0, 0))
    if state is not None:
        in_specs += [c_spec, n_spec, pl.BlockSpec(memory_space=pltpu.SMEM)]
        args += list(state)
    out_specs = [pl.BlockSpec((nb * seq_len, D_MODEL), lambda b: (b, 0))]
    out_shape = [jax.ShapeDtypeStruct((rows, D_MODEL), BF16)]
    if emit_state:
        out_specs += [c_spec, n_spec, pl.BlockSpec((nb, None, 2, ML_HEADS), lambda b: (b, 0, 0, 0))]
        out_shape += [jax.ShapeDtypeStruct((batch, 1, 2, ML_HEADS, d, d), F32),
                      jax.ShapeDtypeStruct((batch, 1, 2, ML_HEADS, d), F32),
                      jax.ShapeDtypeStruct((batch, 1, 2, ML_HEADS), F32)]
    n_state = 2 * ML_HEADS * nb
    return pl.pallas_call(
        functools.partial(_mlstm_kernel, seq_len=seq_len, nb=nb, has_state=state is not None,
                          emit_state=emit_state),
        grid=(batch // nb,),
        in_specs=in_specs,
        out_specs=out_specs,
        out_shape=out_shape,
        scratch_shapes=[pltpu.VMEM((n_state, d + ML_EXT, d), F32), pltpu.VMEM((n_state, 1, 1), F32),
                        pltpu.VMEM((nb * nc, GATE_PAD, CHUNK), F32),
                        pltpu.VMEM((2, nb * nc, D_MODEL, CHUNK), F32)],
        compiler_params=_params(1),
        name="mlstm",
    )(*args)


def _merge_kernel(*refs, tiles, mod_index):
    n = len(tiles)
    (x_ref, m_ref, sgm_ref, sgd_ref), refs = refs[:4], refs[4:]
    parts = [refs[j * n:(j + 1) * n] for j in range(2)]
    wm_ref, wd_ref, wo_ref, o_ref, w_sc = refs[2 * n:]
    i = pl.program_id(0)

    @pl.when(i == 0)
    def _():
        for j, w_ref in enumerate((wm_ref, wd_ref, wo_ref)):
            w_sc[j] = w_ref[...].astype(BF16)

    def tile(hm_ref, att_ref):
        y = sgm_ref[...] * _dot(hm_ref[...], w_sc[0]) + sgd_ref[...] * _dot(att_ref[...], w_sc[1])
        o_ref[...] = x_ref[...] + _mod_reader(m_ref, mod_index(i * MERGE_TM))(5) * _dot(y, w_sc[2])

    starts = np.cumsum((0,) + tiles).tolist()
    for k in range(n):
        pl.when((i >= starts[k]) & (i < starts[k + 1]))(functools.partial(tile, *[p[k] for p in parts]))


def _merge(x, mods, mod_index, hm, att, sgm, sgd, wm, wd, wo):
    rows = x.shape[0]
    tiles = tuple(h.shape[0] // MERGE_TM for h in hm)
    assert sum(tiles) * MERGE_TM == rows
    tile = pl.BlockSpec((MERGE_TM, D_MODEL), lambda i: (i, 0))

    def part_spec(k):
        start = sum(tiles[:k])
        return pl.BlockSpec((MERGE_TM, D_MODEL), lambda i: (jnp.clip(i - start, 0, tiles[k] - 1), 0))

    part_specs = [part_spec(k) for k in range(len(tiles))]
    w_spec = _resident((D_MODEL, D_MODEL))
    return pl.pallas_call(
        functools.partial(_merge_kernel, tiles=tiles, mod_index=mod_index),
        grid=(rows // MERGE_TM,),
        in_specs=[tile, _resident(mods.shape), tile, tile]
                 + part_specs * 2 + [w_spec, w_spec, w_spec],
        out_specs=tile,
        out_shape=jax.ShapeDtypeStruct((rows, D_MODEL), F32),
        scratch_shapes=[pltpu.VMEM((3, D_MODEL, D_MODEL), BF16)],
        compiler_params=_params(1),
        name="branch_merge",
    )(x, mods, sgm, sgd, *hm, *att, wm, wd, wo)


def _rope_tables(seq_len):
    lane = np.arange(LANES)
    r = lane % ROPE_AXIS_DIM
    freqs = np.power(np.float32(ROPE_BASE), -(r % ROPE_HALF).astype(np.float32) / np.float32(ROPE_HALF))
    tok = np.arange(seq_len)
    by_row = (lane % DA_HEAD_DIM < ROPE_AXIS_DIM)[None, :]
    pos = np.where(by_row, (tok // GRID_W)[:, None], (tok % GRID_W)[:, None]).astype(np.float32)
    ang = pos * freqs[None, :]
    sign = np.where(r < ROPE_HALF, -1.0, 1.0).astype(np.float32)
    return jnp.asarray(np.cos(ang), F32), jnp.asarray(np.sin(ang) * sign[None, :], F32)


def kernel(x_prompt, x_sample, c, cache_k, cache_v, state_C, state_n, state_m, c_ctx, w_ada, b_ada, g_norm, ffn1_w1, ffn1_w3, ffn1_w2, ffn2_w1, ffn2_w3, ffn2_w2, w_in, b_gate, g_qn, g_kn, lam_q1, lam_k1, lam_q2, lam_k2, g_sub, g_mh, w_br_m, w_br_d, w_out):
    depth = w_ada.shape[0]
    assert depth == 1
    l = 0
    bp, tp, _ = x_prompt.shape
    bs, ts, _ = x_sample.shape
    past = cache_k.shape[2]
    lam_init = 0.8 - 0.6 * math.exp(-0.3 * l)

    cvecs = jnp.concatenate([c_ctx[None, :], c, jnp.zeros((MOD_ROWS - 1 - bs, D_MODEL), F32)], axis=0)
    mods = _mods(cvecs, w_ada[l], b_ada[l])

    group = np.arange(MXU_DIM) // DA_HEAD_DIM
    w = dict(
        g_norm=g_norm[l],
        ffn1_w1=ffn1_w1[l], ffn1_w3=ffn1_w3[l], ffn1_w2=ffn1_w2[l],
        ffn2_w1=ffn2_w1[l], ffn2_w3=ffn2_w3[l], ffn2_w2=ffn2_w2[l],
        w_in_t=w_in[l].T,
        b_gate=b_gate[l:l + 1],
        bd=jnp.asarray(group[:, None] == group[None, :], BF16),
        g_qn=g_qn[l:l + 1], g_kn=g_kn[l:l + 1],
        lam=(lam_q1[l:l + 1], lam_k1[l:l + 1], lam_q2[l:l + 1], lam_k2[l:l + 1]),
        g_sub=g_sub[l:l + 1], g_mh=g_mh[l],
        w_br_m=w_br_m[l], w_br_d=w_br_d[l], w_out=w_out[l],
    )

    n_ctx, n_lat = bp * tp, bs * ts
    mod_index = lambda r: jnp.where(r < n_ctx, 0, 1 + (r - n_ctx) // ts)
    seg_start = lambda j: pl.multiple_of(jnp.where(j < N_SEG, _seg_start(j), GATE_LO), N_GATE_COLS)
    x1, w["w_seg"] = _ffn((x_prompt.reshape(n_ctx, D_MODEL), x_sample.reshape(n_lat, D_MODEL)), (n_ctx + n_lat,),
                          mods, mod_index, w["g_norm"], w["ffn1_w1"], w["ffn1_w3"], w["ffn1_w2"], base=0,
                          side=(w["w_in_t"], N_SEG + 1, seg_start))

    mq, mk, mv, so, dq, sgm, sgd, gates, new_k_t, new_v, dk_lat, dv_lat = _proj(
        x1, n_ctx, tp, ts, mods, mod_index, w["g_norm"], w["w_seg"], w["b_gate"], w["bd"], w["g_qn"], w["g_kn"],
        _rope_tables(ts))
    att_p = _attn(dq, 0, new_k_t, new_v, None, w["lam"], w["g_sub"], bp, tp, lam_init)
    hm_p, new_c, new_n, new_m = _mlstm(mq, mk, mv, so, gates, 0, w["g_mh"], None, bp, tp, emit_state=True)

    cache = (cache_k[:, l].transpose(0, 2, 3, 4, 1).reshape(bs, D_MODEL, past),
             cache_v[:, l].reshape(bs, past * DA_HEADS, DA_V_DIM))
    att_s = _attn(dq, n_ctx, dk_lat, dv_lat, cache, w["lam"], w["g_sub"], bs, ts, lam_init)
    (hm_s,) = _mlstm(mq, mk, mv, so, gates, n_ctx, w["g_mh"], (state_C, state_n, state_m), bs, ts,
                     emit_state=False)

    x2 = _merge(x1, mods, mod_index, (hm_p, hm_s), (att_p, att_s), sgm, sgd, w["w_br_m"], w["w_br_d"], w["w_out"])
    xp, xs = _ffn((x2,), (n_ctx, n_lat), mods, mod_index, w["g_norm"], w["ffn2_w1"], w["ffn2_w3"], w["ffn2_w2"],
                  base=6)

    return (xp.reshape(bp, tp, D_MODEL), xs.reshape(bs, ts, D_MODEL),
            new_k_t.reshape(bp, DA_HEADS, 2, DA_HEAD_DIM, tp).transpose(0, 4, 1, 2, 3)[:, None],
            new_v.reshape(bp, 1, tp, DA_HEADS, DA_V_DIM),
            new_c, new_n, new_m)
```

```python
import functools
import math

import jax
import jax.numpy as jnp
import numpy as np
from jax import lax
from jax.experimental import pallas as pl
from jax.experimental.pallas import tpu as pltpu

F32 = jnp.float32
BF16 = jnp.bfloat16

D_MODEL = 1024
D_FF = 2816
N_MOD = 9
GRID_W = 64
ML_HEADS = 4
ML_HEAD_DIM = 256
DA_HEADS = 8
DA_HEAD_DIM = 64
DA_V_DIM = 128
N_GATE_COLS = 16
CHUNK = 256
ROPE_BASE = 10000.0
QK_LOG2_SCALE = DA_HEAD_DIM ** -0.5 * math.log2(math.e)
EPS = 1e-6

LANES = 128
SUBLANES = 8
GATE_PAD = LANES
MXU_DIM = 256
VMEM_LIMIT = 56 * 1024 * 1024

SIDE_PIECES = 8
MOD_ROWS = SUBLANES
FFN_TM = 512
FFN_TF = 256
PROJ_TM = 256
MERGE_TM = 512
ATT_TQ = 256
ATT_TK_MAX = 768
ATT_SHORT_SEQ_BATCH = 2


def _params(n_axes):
    return pltpu.CompilerParams(dimension_semantics=("arbitrary",) * n_axes,
                                vmem_limit_bytes=VMEM_LIMIT)


def _dot(a, b):
    return jnp.dot(a.astype(BF16), b.astype(BF16), preferred_element_type=F32)


def _dot_nt(a, b):
    return lax.dot_general(a.astype(BF16), b.astype(BF16), (((1,), (1,)), ((), ())),
                           preferred_element_type=F32)


def _sigmoid(x):
    return 1.0 / (1.0 + jnp.exp(-x))


def _log_sigmoid(x):
    return jnp.minimum(x, 0.0) - jnp.log1p(jnp.exp(-jnp.abs(x)))


def _modulated_norm(x, g, shift, scale):
    y = x * lax.rsqrt(jnp.mean(x * x, axis=-1, keepdims=True) + EPS) * g
    return y * (1.0 + scale) + shift


def _resident(shape):
    return pl.BlockSpec(shape, lambda *_: (0,) * len(shape), pipeline_mode=pl.Buffered(1))


MODS_TN = 512
MODS_IN_FLIGHT = 8


def _mods_kernel(c_ref, w_hbm, b_ref, o_ref, w_sc, sem):
    n_chunks = o_ref.shape[1] // MODS_TN

    def copy(j):
        slot = j % MODS_IN_FLIGHT
        return pltpu.make_async_copy(w_hbm.at[:, j * MODS_TN:(j + 1) * MODS_TN], w_sc.at[slot], sem.at[slot])

    for j in range(min(MODS_IN_FLIGHT, n_chunks)):
        copy(j).start()
    c = c_ref[...]
    a = (c * _sigmoid(c)).astype(BF16)
    for j in range(n_chunks):
        cols = slice(j * MODS_TN, (j + 1) * MODS_TN)
        copy(j).wait()
        o_ref[:, cols] = _dot(a, w_sc[j % MODS_IN_FLIGHT]) + b_ref[:, cols]
        if j + MODS_IN_FLIGHT < n_chunks:
            copy(j + MODS_IN_FLIGHT).start()


def _mods(cvecs, w_ada, b_ada):
    n = N_MOD * D_MODEL
    assert n % MODS_TN == 0
    return pl.pallas_call(
        _mods_kernel,
        in_specs=[pl.BlockSpec(memory_space=pltpu.VMEM),
                  pl.BlockSpec(memory_space=pl.ANY),
                  pl.BlockSpec(memory_space=pltpu.VMEM)],
        out_specs=pl.BlockSpec(memory_space=pltpu.VMEM),
        out_shape=jax.ShapeDtypeStruct((MOD_ROWS, n), F32),
        scratch_shapes=[pltpu.VMEM((MODS_IN_FLIGHT, D_MODEL, MODS_TN), F32),
                        pltpu.SemaphoreType.DMA((MODS_IN_FLIGHT,))],
        compiler_params=pltpu.CompilerParams(vmem_limit_bytes=VMEM_LIMIT),
        name="adaln_mods",
    )(cvecs, w_ada, b_ada.reshape(1, n))


def _mod_reader(m_ref, row):
    return lambda j: m_ref[pl.ds(row, 1), j * D_MODEL:(j + 1) * D_MODEL]


def _ffn_kernel(*refs, base, mod_index, in_tiles, out_tiles, side_blocks):
    n_in, n_out = len(in_tiles), len(out_tiles)
    x_refs, refs = refs[:n_in], refs[n_in:]
    (m_ref, g_ref, w1_ref, w3_ref, w2_ref), refs = refs[:5], refs[5:]
    if side_blocks:
        side_in_ref, refs = refs[0], refs[1:]
        side_out_ref, refs = refs[n_out], refs[:n_out] + refs[n_out + 1:]
    o_refs, (w1_sc, w3_sc, w2_sc, hh_sc, acc_sc) = refs[:n_out], refs[n_out:]
    g = pl.program_id(0)
    nf = D_FF // FFN_TF
    row_tile = g - (nf - 1)
    mod = _mod_reader(m_ref, mod_index(jnp.maximum(row_tile, 0) * FFN_TM))
    gain = g_ref[base // 3:base // 3 + 1, :]
    norm = lambda x: _modulated_norm(x, gain, mod(base), mod(base + 1)).astype(BF16)
    finish = lambda x, acc: x + 0.5 * mod(base + 2) * acc

    def tile(hh, f):
        a = _dot(hh, w1_sc[f])
        b = _dot(hh, w3_sc[f])
        return _dot(a * _sigmoid(a) * b, w2_sc[f])

    @pl.when(g < nf)
    def _():
        w1_sc[g] = w1_ref[...].astype(BF16)
        w3_sc[g] = w3_ref[...].astype(BF16)
        w2_sc[g] = w2_ref[...].astype(BF16)

        @pl.when(g == 0)
        def _():
            hh_sc[...] = norm(x_refs[0][...])
            acc_sc[...] = jnp.zeros(acc_sc.shape, F32)

        acc_sc[...] += tile(hh_sc[...], g)

        @pl.when(g == nf - 1)
        def _():
            o_refs[0][...] = finish(x_refs[0][...], acc_sc[...])

    def full_tile(x_ref, o_ref):
        x = x_ref[...]
        hh = norm(x)
        acc = jnp.zeros(x.shape, F32)
        for f in range(nf):
            acc = acc + tile(hh, f)
            if side_blocks and f < SIDE_PIECES:
                rows = slice(f * (D_MODEL // SIDE_PIECES), (f + 1) * (D_MODEL // SIDE_PIECES))
                side_out_ref[rows, :] = side_in_ref[rows, :].astype(BF16)
        o_ref[...] = finish(x, acc)

    bounds = sorted(set(np.cumsum((0,) + in_tiles).tolist()) | set(np.cumsum((0,) + out_tiles).tolist()))
    for lo, hi in zip(bounds[:-1], bounds[1:]):
        k_in = int(np.searchsorted(np.cumsum(in_tiles), lo, side="right"))
        k_out = int(np.searchsorted(np.cumsum(out_tiles), lo, side="right"))
        pl.when((g >= nf) & (row_tile >= lo) & (row_tile < hi))(
            functools.partial(full_tile, x_refs[k_in], o_refs[k_out]))


def _ffn(xs, out_rows, mods, mod_index, g, w1, w3, w2, base, side=None):
    nf = D_FF // FFN_TF
    in_tiles = tuple(x.shape[0] // FFN_TM for x in xs)
    out_tiles = tuple(r // FFN_TM for r in out_rows)
    assert sum(in_tiles) == sum(out_tiles)
    side_in_specs, side_out_specs, side_out_shape, side_args, side_blocks = [], [], [], [], 0
    if side is not None:
        side_w, side_blocks, side_start = side
        assert side_blocks <= sum(in_tiles) - 1
        blk = lambda s: jnp.clip(s - nf, 0, side_blocks - 1)
        side_in_specs = [pl.BlockSpec((pl.Element(D_MODEL), pl.Element(D_MODEL)), lambda s: (side_start(blk(s)), 0))]
        side_out_specs = [pl.BlockSpec((None, D_MODEL, D_MODEL), lambda s: (blk(s), 0, 0))]
        side_out_shape = [jax.ShapeDtypeStruct((side_blocks, D_MODEL, D_MODEL), BF16)]
        side_args = [side_w]
    row_tile = lambda s: jnp.maximum(s - (nf - 1), 0)
    f_tile = lambda s: jnp.minimum(s, nf - 1)

    def part_spec(tiles, k):
        start = sum(tiles[:k])
        return pl.BlockSpec((FFN_TM, D_MODEL), lambda s: (jnp.clip(row_tile(s) - start, 0, tiles[k] - 1), 0))

    return pl.pallas_call(
        functools.partial(_ffn_kernel, base=base, mod_index=mod_index, in_tiles=in_tiles, out_tiles=out_tiles,
                          side_blocks=side_blocks),
        grid=(nf - 1 + sum(in_tiles),),
        in_specs=[part_spec(in_tiles, k) for k in range(len(xs))] + [
            _resident(mods.shape),
            _resident(g.shape),
            pl.BlockSpec((D_MODEL, FFN_TF), lambda s: (0, f_tile(s))),
            pl.BlockSpec((D_MODEL, FFN_TF), lambda s: (0, f_tile(s))),
            pl.BlockSpec((FFN_TF, D_MODEL), lambda s: (f_tile(s), 0))] + side_in_specs,
        out_specs=[part_spec(out_tiles, k) for k in range(len(out_rows))] + side_out_specs,
        out_shape=[jax.ShapeDtypeStruct((r, D_MODEL), F32) for r in out_rows] + side_out_shape,
        scratch_shapes=[pltpu.VMEM((nf, D_MODEL, FFN_TF), BF16), pltpu.VMEM((nf, D_MODEL, FFN_TF), BF16),
                        pltpu.VMEM((nf, FFN_TF, D_MODEL), BF16),
                        pltpu.VMEM((FFN_TM, D_MODEL), BF16), pltpu.VMEM((FFN_TM, D_MODEL), F32)],
        compiler_params=_params(1),
        name="ffn",
    )(*xs, mods, g, w1, w3, w2, *side_args)


def _group_norm64(x, bd, g):
    ss = _dot(x * x, bd)
    return x * lax.rsqrt(ss * (1.0 / DA_HEAD_DIM) + EPS) * g


ROPE_AXIS_DIM = DA_HEAD_DIM // 2
ROPE_HALF = ROPE_AXIS_DIM // 2


def _rope(x, cos, sin_signed):
    first = (lax.broadcasted_iota(jnp.int32, x.shape, 1) % ROPE_AXIS_DIM) < ROPE_HALF
    partner = jnp.where(first, pltpu.roll(x, LANES - ROPE_HALF, 1), pltpu.roll(x, ROPE_HALF, 1))
    return x * cos + partner * sin_signed


N_SEG = 9
PROJ_WARM = (N_SEG + 1) // 2
GATE_LO = 4 * D_MODEL


def _seg_start(s):
    return s * D_MODEL + N_GATE_COLS * (s >= 4)


def _proj_kernel(*refs, ctx_tiles, row_of_step):
    (x_ref, m_ref, g_ref, wt_ref, bg_ref, bd_ref, gq_ref, gk_ref, cos_ref, sin_ref), refs = refs[:10], refs[10:]
    (mq_ref, mk_ref, mv_ref, so_ref, dq_ref, sgm_ref, sgd_ref, gates_ref,
     dk_ctx_ref, dv_ctx_ref, dk_lat_ref, dv_lat_ref, w_sc, wg_sc, hh_sc) = refs
    step = pl.program_id(0)
    mod = _mod_reader(m_ref, row_of_step(step))
    norm = lambda: _modulated_norm(x_ref[...], g_ref[1:2, :], mod(3), mod(4)).astype(BF16)
    groups = MXU_DIM // DA_HEAD_DIM
    tiled = lambda ref, scale: jnp.concatenate([ref[...] * scale] * groups, axis=1)
    gq_row, gk_row = tiled(gq_ref, QK_LOG2_SCALE), tiled(gk_ref, 1.0)
    bias = jnp.concatenate([bg_ref[...], jnp.zeros((1, GATE_PAD - N_GATE_COLS), F32)], axis=1)

    def gates(hh):
        y = _dot_nt(hh, wg_sc[...]) + bias
        col = lax.broadcasted_iota(jnp.int32, y.shape, 1)
        gates_ref[...] = jnp.where((col // ML_HEADS) % 2 == 1, _log_sigmoid(y), y)

    def qk_segment(y, gain_row, o_ref, rope):
        bd = bd_ref[...]
        for c in range(D_MODEL // MXU_DIM):
            cols = slice(c * MXU_DIM, (c + 1) * MXU_DIM)
            z = _group_norm64(y[:, cols], bd, gain_row)
            if rope:
                z = jnp.concatenate(
                    [_rope(z[:, k * LANES:(k + 1) * LANES], cos_ref[...], sin_ref[...])
                     for k in range(MXU_DIM // LANES)], axis=1)
            o_ref[:, cols] = z.astype(o_ref.dtype)

    def cache_key_segment(hh):
        y_t = _dot_nt(w_sc[5], hh)
        z = y_t.reshape(D_MODEL // DA_HEAD_DIM, DA_HEAD_DIM, y_t.shape[1])
        ms = jnp.mean(z * z, axis=1, keepdims=True)
        gain = jnp.broadcast_to(gk_row[:, 0:LANES], (LANES, LANES)).T[0:DA_HEAD_DIM, :]
        gain = jnp.concatenate([gain] * (y_t.shape[1] // LANES), axis=1)
        dk_ctx_ref[...] = (z * lax.rsqrt(ms + EPS) * gain).reshape(y_t.shape)

    def segment(s, hh, ctx):
        if s in (0, 2):
            o_ref = {0: mq_ref, 2: mv_ref}[s]
            y_t = _dot_nt(w_sc[s], hh) * ((ML_HEAD_DIM ** -0.5) if s == 0 else 1.0)
            for j in range(PROJ_TM // CHUNK):
                o_ref[j] = y_t[:, j * CHUNK:(j + 1) * CHUNK].astype(o_ref.dtype)
            return
        if s == 5 and ctx:
            cache_key_segment(hh)
            return
        y = _dot_nt(hh, w_sc[s])
        if s == 4:
            qk_segment(y, gq_row, dq_ref, rope=not ctx)
        elif s == 5:
            qk_segment(y, gk_row, dk_lat_ref, rope=True)
        elif s in (3, 7, 8):
            {3: so_ref, 7: sgm_ref, 8: sgd_ref}[s][...] = _sigmoid(y)
        elif s == 6 and ctx:
            dv_ctx_ref[...] = y.reshape(y.shape[0], DA_HEADS, DA_V_DIM)
        elif s == 6:
            dv_lat_ref[...] = y.astype(dv_lat_ref.dtype)
        else:
            mk_ref[...] = y.astype(mk_ref.dtype)

    for w in range(PROJ_WARM):
        @pl.when(step == w)
        def _(w=w):
            if w == 0:
                hh_sc[...] = norm()
            for j in range(2):
                s = 2 * w + j
                if s < N_SEG:
                    w_sc[s] = wt_ref[j]
                    segment(s, hh_sc[...], ctx=True)
                else:
                    wg_sc[...] = jnp.zeros(wg_sc.shape, BF16)
                    wg_sc[0:N_GATE_COLS, :] = wt_ref[j, 0:N_GATE_COLS, :]
                    gates(hh_sc[...])

    def full_tile(ctx):
        hh = norm()
        gates(hh)
        for s in range(N_SEG):
            segment(s, hh, ctx)

    is_ctx = step - (PROJ_WARM - 1) < ctx_tiles
    pl.when((step >= PROJ_WARM) & is_ctx)(functools.partial(full_tile, True))
    pl.when((step >= PROJ_WARM) & jnp.logical_not(is_ctx))(functools.partial(full_tile, False))


def _proj(x, n_ctx, seq_ctx, seq_lat, mods, mod_index, g, w_seg, b_gate, bd, gq, gk, rope_tabs):
    rows = x.shape[0]
    n_lat = rows - n_ctx
    assert seq_ctx == PROJ_TM
    ctx_tiles, lat_tiles = n_ctx // PROJ_TM, n_lat // PROJ_TM
    row_tile = lambda s: jnp.maximum(s - (PROJ_WARM - 1), 0)
    ctx_tile = lambda s: jnp.minimum(row_tile(s), ctx_tiles - 1)
    lat_tile = lambda s: jnp.clip(row_tile(s) - ctx_tiles, 0, lat_tiles - 1)
    row = lambda s: (row_tile(s), 0)
    tile = pl.BlockSpec((PROJ_TM, D_MODEL), row)
    w_block = (2, D_MODEL, D_MODEL)
    tab = pl.BlockSpec((PROJ_TM, LANES), lambda s: (lat_tile(s) % (seq_lat // PROJ_TM), 0))
    in_specs = [tile,
                _resident(mods.shape),
                _resident(g.shape),
                pl.BlockSpec(w_block, lambda s: (jnp.minimum(s, PROJ_WARM - 1), 0, 0)),
                _resident(b_gate.shape),
                _resident((MXU_DIM, MXU_DIM)),
                _resident(gq.shape),
                _resident(gk.shape),
                tab, tab]
    slab = pl.BlockSpec((PROJ_TM // CHUNK, D_MODEL, CHUNK), lambda s: (row_tile(s), 0, 0))
    slab_shape = jax.ShapeDtypeStruct((rows // CHUNK, D_MODEL, CHUNK), BF16)
    full = lambda dt: jax.ShapeDtypeStruct((rows, D_MODEL), dt)
    out_specs = [slab, tile, slab, tile, tile, tile, tile, pl.BlockSpec((PROJ_TM, GATE_PAD), row),
                 pl.BlockSpec((None, D_MODEL, seq_ctx), lambda s: (ctx_tile(s), 0, 0)),
                 pl.BlockSpec((PROJ_TM, DA_HEADS, DA_V_DIM), lambda s: (ctx_tile(s), 0, 0)),
                 pl.BlockSpec((PROJ_TM, D_MODEL), lambda s: (lat_tile(s), 0)),
                 pl.BlockSpec((PROJ_TM, D_MODEL), lambda s: (lat_tile(s), 0))]
    out_shape = [slab_shape, full(BF16), slab_shape, full(F32), full(BF16), full(F32), full(F32),
                 jax.ShapeDtypeStruct((rows, GATE_PAD), F32),
                 jax.ShapeDtypeStruct((n_ctx // seq_ctx, D_MODEL, seq_ctx), F32),
                 jax.ShapeDtypeStruct((n_ctx, DA_HEADS, DA_V_DIM), F32),
                 jax.ShapeDtypeStruct((n_lat, D_MODEL), BF16),
                 jax.ShapeDtypeStruct((n_lat, D_MODEL), BF16)]
    return pl.pallas_call(
        functools.partial(_proj_kernel, ctx_tiles=ctx_tiles,
                          row_of_step=lambda step: mod_index(row_tile(step) * PROJ_TM)),
        grid=(PROJ_WARM - 1 + rows // PROJ_TM,),
        in_specs=in_specs,
        out_specs=out_specs,
        out_shape=out_shape,
        scratch_shapes=[pltpu.VMEM((N_SEG, D_MODEL, D_MODEL), BF16), pltpu.VMEM((GATE_PAD, D_MODEL), BF16),
                        pltpu.VMEM((PROJ_TM, D_MODEL), BF16)],
        compiler_params=_params(1),
        name="mixer_in_proj",
    )(x, mods, g, w_seg, b_gate, bd, gq, gk, *rope_tabs)


def _lambda(lam_refs, lam_init):
    q1, k1, q2, k2 = (r[...] for r in lam_refs)
    s1 = jnp.sum(q1 * k1, axis=1, keepdims=True)
    s2 = jnp.sum(q2 * k2, axis=1, keepdims=True)
    return jnp.exp(s1) - jnp.exp(s2) + lam_init


def _attn_kernel(*refs, cached, seq, nb, lam_init):
    if cached:
        (q_ref, k_ref, v_ref, ck_ref, cv_ref), refs = refs[:5], refs[5:]
    else:
        (q_ref, k_ref, v_ref), refs = refs[:3], refs[3:]
    lam_refs, (gs_ref, o_ref, kall_sc, vt_sc, s_sc) = refs[:4], refs[4:]
    units = [(bb, h) for bb in range(nb) for h in range(DA_HEADS)]

    def cache_order_kv(kt_ref, vr_ref, u, h, lo, n):
        kall_sc[u, lo:lo + n, :] = kt_ref[h * DA_V_DIM:(h + 1) * DA_V_DIM, :].T.astype(BF16)
        vt_sc[u, :, lo:lo + n] = vr_ref[pl.ds(h, n, stride=DA_HEADS), :].T.astype(BF16)

    @pl.when(pl.program_id(1) == 0)
    def _():
        for u, (bb, h) in enumerate(units):
            if cached:
                cols = slice(h * DA_V_DIM, (h + 1) * DA_V_DIM)
                kall_sc[u, 0:seq, :] = k_ref[:, cols].astype(BF16)
                vt_sc[u, :, 0:seq] = v_ref[:, cols].astype(F32).T.astype(BF16)
                cache_order_kv(ck_ref.at[bb], cv_ref.at[bb], u, h, seq, ck_ref.shape[2])
            else:
                cache_order_kv(k_ref.at[bb], v_ref.at[bb], u, h, 0, seq)

    lam = _lambda(lam_refs, lam_init)
    sub_gain = gs_ref[...] * (1.0 - lam_init)
    lane = lax.broadcasted_iota(jnp.int32, (1, DA_V_DIM), 1)
    comp_masks = [lane < DA_HEAD_DIM, lane >= DA_HEAD_DIM]
    tq = q_ref.shape[0] // nb
    n_keys = kall_sc.shape[1]
    n_tiles = pl.cdiv(n_keys, ATT_TK_MAX)
    tk = n_keys // n_tiles

    def stacked_q(u):
        bb, h = units[u]
        q = q_ref[bb * tq:(bb + 1) * tq, h * DA_V_DIM:(h + 1) * DA_V_DIM].astype(BF16)
        return jnp.concatenate([jnp.where(m, q, jnp.zeros_like(q)) for m in comp_masks], axis=0)

    def score_tile(h, j, qq, m8):
        rows = slice(j * tk, (j + 1) * tk)
        st = _dot_nt(kall_sc[h, rows, :], qq)
        s_sc[h % 2, rows, :] = st
        t8 = jnp.max(st.reshape(tk // SUBLANES, SUBLANES, 2 * tq), axis=0)
        return t8 if m8 is None else jnp.maximum(m8, t8)

    def prob_tile(h, j, mx, d8, pv):
        rows = slice(j * tk, (j + 1) * tk)
        e = jnp.exp2(s_sc[h % 2, rows, :] - mx)
        s8 = jnp.sum(e.reshape(tk // SUBLANES, SUBLANES, 2 * tq), axis=0)
        p = _dot(vt_sc[h, :, rows], e)
        return (s8 if d8 is None else d8 + s8), (p if pv is None else pv + p)

    qq = stacked_q(0)
    m8 = None
    for j in range(n_tiles):
        m8 = score_tile(0, j, qq, m8)
    for h in range(len(units)):
        mx = jnp.max(m8, axis=0, keepdims=True)
        if h + 1 < len(units):
            qq = stacked_q(h + 1)
        m8, d8, pv = None, None, None
        for j in range(n_tiles):
            if h + 1 < len(units):
                m8 = score_tile(h + 1, j, qq, m8)
            d8, pv = prob_tile(h, j, mx, d8, pv)
        inv = 1.0 / jnp.sum(d8, axis=0, keepdims=True)
        out_t = pv[:, :tq] * inv[:, :tq] - pv[:, tq:] * (lam * inv[:, tq:])
        out_t = out_t * lax.rsqrt(jnp.mean(out_t * out_t, axis=0, keepdims=True) + EPS)
        bb, head = units[h]
        o_ref[bb * tq:(bb + 1) * tq, head * DA_V_DIM:(head + 1) * DA_V_DIM] = (out_t.T * sub_gain).astype(o_ref.dtype)


def _attn(q, q_row0, k, v, cache, lam_vecs, g_sub, batch, seq_len, lam_init):
    rows = batch * seq_len
    nq = seq_len // ATT_TQ
    nb = ATT_SHORT_SEQ_BATCH if (cache is None and nq == 1) else 1
    assert batch % nb == 0
    q_spec = pl.BlockSpec((nb * ATT_TQ, D_MODEL), lambda b, i: (q_row0 // (nb * ATT_TQ) + b * nq + i, 0))
    o_spec = pl.BlockSpec((nb * ATT_TQ, D_MODEL), lambda b, i: (b * nq + i, 0))
    cache_specs = lambda n: [pl.BlockSpec((nb, D_MODEL, n), lambda b, i: (b, 0, 0)),
                             pl.BlockSpec((nb, n * DA_HEADS, DA_V_DIM), lambda b, i: (b, 0, 0))]
    n_keys = seq_len
    if cache is not None:
        past = cache[0].shape[2]
        n_keys += past
        kv_spec = pl.BlockSpec((seq_len, D_MODEL), lambda b, i: (b, 0))
        in_specs = [q_spec, kv_spec, kv_spec] + cache_specs(past)
        args = [q, k, v] + list(cache)
    else:
        in_specs = [q_spec] + cache_specs(seq_len)
        args = [q, k, v.reshape(batch, seq_len * DA_HEADS, DA_V_DIM)]
    in_specs += [_resident((1, DA_HEAD_DIM))] * 4 + [_resident((1, DA_V_DIM))]
    args += list(lam_vecs) + [g_sub]
    return pl.pallas_call(
        functools.partial(_attn_kernel, cached=cache is not None, seq=seq_len, nb=nb, lam_init=lam_init),
        grid=(batch // nb, nq),
        in_specs=in_specs,
        out_specs=o_spec,
        out_shape=jax.ShapeDtypeStruct((rows, D_MODEL), BF16),
        scratch_shapes=[pltpu.VMEM((nb * DA_HEADS, n_keys, DA_V_DIM), BF16),
                        pltpu.VMEM((nb * DA_HEADS, DA_V_DIM, n_keys), BF16),
                        pltpu.VMEM((2, n_keys, 2 * ATT_TQ), F32)],
        compiler_params=_params(2),
        name="diff_attention",
    )(*args)


def _per_chain(fn, a, b):
    return jnp.stack([fn(a[i], b[i]) for i in range(a.shape[0])])


ML_EXT = SUBLANES
ML_SHORT_SEQ_BATCH = 2


def _rows(x):
    return jnp.stack([x[i:i + 1, :] for i in range(x.shape[0])])


def _split3(x):
    hi = x.astype(BF16)
    r = x - hi.astype(F32)
    mid = r.astype(BF16)
    lo = (r - mid.astype(F32)).astype(BF16)
    return jnp.concatenate([hi, mid, lo], axis=1)


def _chunk_scan(x, reverse_rows, tri_prefix, tri_suffix):
    parts = _split3(x)
    return jnp.where(reverse_rows, _dot(parts, tri_suffix), _dot(parts, tri_prefix))


def _paired_value_matmul(v_t, sc):
    B, L, _ = sc.shape
    zero = jnp.zeros((L, L), sc.dtype)
    out = []
    for i in range(0, B, 2):
        lhs = jnp.concatenate([v_t[i], v_t[i + 1]], axis=1)
        rhs = jnp.concatenate([jnp.concatenate([sc[i], zero], axis=1),
                               jnp.concatenate([zero, sc[i + 1]], axis=1)], axis=0)
        both = _dot(lhs, rhs)
        out += [both[:, :L], both[:, L:]]
    return jnp.stack(out)


def _mlstm_step(k, q_t, v_t, ic, fc, CT, m, seen_t, reverse_rows):
    B, L = ic.shape
    d = k.shape[2]
    zero_state = CT is None
    if zero_state:
        m = jnp.zeros((B, 1, 1), F32)
    tri = lambda keep: jnp.concatenate([jnp.where(keep, 1.0, 0.0).astype(BF16)] * 3, axis=0)
    b2 = _chunk_scan(fc, reverse_rows, tri(seen_t[0]), tri(seen_t[B - 1]))
    u2 = ic - b2
    u_t = jnp.concatenate([u2, jnp.zeros((L - B, L), F32)], axis=0).T
    u_col = jnp.stack([jnp.broadcast_to(u_t[:, i:i + 1], (L, L)) for i in range(B)])
    b, i_g, f_g = _rows(b2), _rows(ic), _rows(fc)
    b_last = jnp.sum(f_g, axis=2, keepdims=True)

    log_d = jnp.where(seen_t, b + u_col, -jnp.inf)
    a = b + m
    m_t = jnp.maximum(a, jnp.max(log_d, axis=1, keepdims=True))
    dmat = jnp.exp(log_d - m_t)
    sc = _per_chain(_dot, k, q_t) * dmat
    value_matmul = _paired_value_matmul if 2 * L <= MXU_DIM else functools.partial(_per_chain, _dot)
    num = value_matmul(v_t, sc.astype(BF16))
    den = jnp.sum(sc, axis=1, keepdims=True)
    if not zero_state:
        inter = jnp.exp(a - m_t)
        cq = _per_chain(_dot, CT, q_t)
        num = num + inter * cq[:, :d, :]
        den = den + inter * cq[:, d:d + 1, :]
    h_t = num * (1.0 / jnp.maximum(jnp.abs(den), jnp.exp(-m_t)))

    g = b_last - b + i_g
    m_new = jnp.maximum(b_last + m, jnp.max(g, axis=2, keepdims=True))
    w = jnp.exp(g - m_new)
    decay = jnp.exp(b_last + m - m_new)
    vw = jnp.concatenate([v_t.astype(F32), jnp.ones((B, ML_EXT, L), F32)], axis=1) * w
    CT_new = _per_chain(_dot, vw, k)
    if not zero_state:
        CT_new = decay * CT + CT_new
    return h_t, CT_new, m_new


def _mlstm_kernel(*refs, seq_len, nb, has_state, emit_state):
    q_ref, k_ref, v_ref, so_ref, g_ref, gmh_ref = refs[:6]
    refs = refs[6:]
    if has_state:
        (c0_ref, n0_ref, m0_ref), refs = refs[:3], refs[3:]
    hm_ref, refs = refs[0], refs[1:]
    if emit_state:
        (c_out_ref, n_out_ref, m_out_ref), refs = refs[:3], refs[3:]
    ct_sc, m_sc, gr_sc, h_sc = refs

    nc = seq_len // CHUNK
    d_head = ML_HEAD_DIM
    chains = [(bb, d, h) for bb in range(nb) for d in range(2) for h in range(ML_HEADS)]
    n_chain = len(chains)
    for i, (bb, d, h) in enumerate(chains):
        if has_state:
            ct_sc[i, 0:d_head, :] = c0_ref[bb, d, h].T
            ct_sc[i, d_head:d_head + ML_EXT, :] = jnp.broadcast_to(n0_ref[bb, d, h:h + 1, :], (ML_EXT, d_head))
            m_sc[i] = jnp.full((1, 1), m0_ref[pl.program_id(0) * nb + bb, 0, d, h], F32)
        elif nc > 2:
            ct_sc[i] = jnp.zeros((d_head + ML_EXT, d_head), F32)
            m_sc[i] = jnp.zeros((1, 1), F32)
    for c in range(nb * nc):
        gr_sc[c] = g_ref[c * CHUNK:(c + 1) * CHUNK, :].T

    s_idx = lax.broadcasted_iota(jnp.int32, (CHUNK, CHUNK), 0)
    t_idx = lax.broadcasted_iota(jnp.int32, (CHUNK, CHUNK), 1)
    seen_t = jnp.stack([(s_idx >= t_idx) if d else (s_idx <= t_idx) for _, d, _ in chains])
    reverse_rows = (lax.broadcasted_iota(jnp.int32, (n_chain, CHUNK), 0) // ML_HEADS) % 2 == 1

    def step(c_fwd, c_bwd, rows_of, zero_state=False):
        chunk_of = (c_fwd, c_bwd)
        lo = 2 * ML_HEADS
        ic, fc = [], []
        for bb in range(nb):
            g_fwd, g_bwd = gr_sc[bb * nc + c_fwd], gr_sc[bb * nc + c_bwd]
            ic += [g_fwd[0:ML_HEADS], g_bwd[lo:lo + ML_HEADS]]
            fc += [g_fwd[ML_HEADS:lo], g_bwd[lo + ML_HEADS:2 * lo]]
        head = [slice(h * d_head, (h + 1) * d_head) for _, _, h in chains]
        stack = lambda pick: jnp.stack([pick(i, bb * nc + chunk_of[d]) for i, (bb, d, _) in enumerate(chains)])
        h_t, CT_new, m_new = _mlstm_step(
            stack(lambda i, c: k_ref[rows_of(c), head[i]]),
            stack(lambda i, c: q_ref[c, head[i], :]),
            stack(lambda i, c: v_ref[c, head[i], :]),
            jnp.concatenate(ic, axis=0), jnp.concatenate(fc, axis=0),
            None if zero_state else ct_sc[...], None if zero_state else m_sc[...], seen_t, reverse_rows)
        ct_sc[...] = CT_new
        m_sc[...] = m_new
        for i, (bb, d, _) in enumerate(chains):
            h_sc[d, bb * nc + chunk_of[d], head[i], :] = h_t[i]

    if nc <= 2:
        for c in range(nc):
            step(c, nc - 1 - c, lambda cc: slice(cc * CHUNK, (cc + 1) * CHUNK),
                 zero_state=(c == 0 and not has_state))
    else:
        def body(c, carry):
            step(c, nc - 1 - c, lambda cc: pl.ds(pl.multiple_of(cc * CHUNK, CHUNK), CHUNK))
            return carry
        lax.fori_loop(0, nc, body, 0, unroll=2)

    for h in range(ML_HEADS):
        hcols = slice(h * d_head, (h + 1) * d_head)
        gain = jnp.broadcast_to(gmh_ref[h:h + 1, :], (CHUNK, d_head)).T
        for c in range(nb * nc):
            rows = slice(c * CHUNK, (c + 1) * CHUNK)
            hsum = h_sc[0, c, hcols, :] + h_sc[1, c, hcols, :]
            hn = hsum * lax.rsqrt(jnp.mean(hsum * hsum, axis=0, keepdims=True) + EPS) * gain
            hm_ref[rows, hcols] = (hn.T * so_ref[rows, hcols]).astype(hm_ref.dtype)
    if emit_state:
        for i, (bb, d, h) in enumerate(chains):
            c_out_ref[bb, d, h] = ct_sc[i, 0:d_head, :].T
            n_out_ref[bb, d, h:h + 1, :] = ct_sc[i, d_head:d_head + 1, :]
            m_out_ref[bb, d:d + 1, h:h + 1] = m_sc[i]


def _mlstm(q_t, k, v_t, so, gates, row0, g_mh, state, batch, seq_len, emit_state):
    rows = batch * seq_len
    d = ML_HEAD_DIM
    nc = seq_len // CHUNK
    nb = ML_SHORT_SEQ_BATCH if nc <= 2 else 1
    assert batch % nb == 0
    assert row0 % (nb * seq_len) == 0
    first = row0 // (nb * seq_len)
    tile = pl.BlockSpec((nb * seq_len, D_MODEL), lambda b: (first + b, 0))
    tile_t = pl.BlockSpec((nb * nc, D_MODEL, CHUNK), lambda b: (first + b, 0, 0))
    in_specs = [tile_t, tile, tile_t, tile,
                pl.BlockSpec((nb * seq_len, GATE_PAD), lambda b: (first + b, 0)),
                _resident((ML_HEADS, d))]
    args = [q_t, k, v_t, so, gates, g_mh]
    c_spec = pl.BlockSpec((nb, None, 2, ML_HEADS, d, d), lambda b: (b, 0, 0, 0, 0, 0))
    n_spec = pl.BlockSpec((nb, None, 2, ML_HEADS, d), lambda b: (b, 0, 0, 0, 0))
    if state is not None:
        in_specs += [c_spec, n_spec, pl.BlockSpec(memory_space=pltpu.SMEM)]
        args += list(state)
    out_specs = [pl.BlockSpec((nb * seq_len, D_MODEL), lambda b: (b, 0))]
    out_shape = [jax.ShapeDtypeStruct((rows, D_MODEL), BF16)]
    if emit_state:
        out_specs += [c_spec, n_spec, pl.BlockSpec((nb, None, 2, ML_HEADS), lambda b: (b, 0, 0, 0))]
        out_shape += [jax.ShapeDtypeStruct((batch, 1, 2, ML_HEADS, d, d), F32),
                      jax.ShapeDtypeStruct((batch, 1, 2, ML_HEADS, d), F32),
                      jax.ShapeDtypeStruct((batch, 1, 2, ML_HEADS), F32)]
    n_state = 2 * ML_HEADS * nb
    return pl.pallas_call(
        functools.partial(_mlstm_kernel, seq_len=seq_len, nb=nb, has_state=state is not None,
                          emit_state=emit_state),
        grid=(batch // nb,),
        in_specs=in_specs,
        out_specs=out_specs,
        out_shape=out_shape,
        scratch_shapes=[pltpu.VMEM((n_state, d + ML_EXT, d), F32), pltpu.VMEM((n_state, 1, 1), F32),
                        pltpu.VMEM((nb * nc, GATE_PAD, CHUNK), F32),
                        pltpu.VMEM((2, nb * nc, D_MODEL, CHUNK), F32)],
        compiler_params=_params(1),
        name="mlstm",
    )(*args)


def _merge_kernel(*refs, tiles, mod_index):
    n = len(tiles)
    (x_ref, m_ref, sgm_ref, sgd_ref), refs = refs[:4], refs[4:]
    parts = [refs[j * n:(j + 1) * n] for j in range(2)]
    wm_ref, wd_ref, wo_ref, o_ref, w_sc = refs[2 * n:]
    i = pl.program_id(0)

    @pl.when(i == 0)
    def _():
        for j, w_ref in enumerate((wm_ref, wd_ref, wo_ref)):
            w_sc[j] = w_ref[...].astype(BF16)

    def tile(hm_ref, att_ref):
        y = sgm_ref[...] * _dot(hm_ref[...], w_sc[0]) + sgd_ref[...] * _dot(att_ref[...], w_sc[1])
        o_ref[...] = x_ref[...] + _mod_reader(m_ref, mod_index(i * MERGE_TM))(5) * _dot(y, w_sc[2])

    starts = np.cumsum((0,) + tiles).tolist()
    for k in range(n):
        pl.when((i >= starts[k]) & (i < starts[k + 1]))(functools.partial(tile, *[p[k] for p in parts]))


def _merge(x, mods, mod_index, hm, att, sgm, sgd, wm, wd, wo):
    rows = x.shape[0]
    tiles = tuple(h.shape[0] // MERGE_TM for h in hm)
    assert sum(tiles) * MERGE_TM == rows
    tile = pl.BlockSpec((MERGE_TM, D_MODEL), lambda i: (i, 0))

    def part_spec(k):
        start = sum(tiles[:k])
        return pl.BlockSpec((MERGE_TM, D_MODEL), lambda i: (jnp.clip(i - start, 0, tiles[k] - 1), 0))

    part_specs = [part_spec(k) for k in range(len(tiles))]
    w_spec = _resident((D_MODEL, D_MODEL))
    return pl.pallas_call(
        functools.partial(_merge_kernel, tiles=tiles, mod_index=mod_index),
        grid=(rows // MERGE_TM,),
        in_specs=[tile, _resident(mods.shape), tile, tile]
                 + part_specs * 2 + [w_spec, w_spec, w_spec],
        out_specs=tile,
        out_shape=jax.ShapeDtypeStruct((rows, D_MODEL), F32),
        scratch_shapes=[pltpu.VMEM((3, D_MODEL, D_MODEL), BF16)],
        compiler_params=_params(1),
        name="branch_merge",
    )(x, mods, sgm, sgd, *hm, *att, wm, wd, wo)


def _rope_tables(seq_len):
    lane = np.arange(LANES)
    r = lane % ROPE_AXIS_DIM
    freqs = np.power(np.float32(ROPE_BASE), -(r % ROPE_HALF).astype(np.float32) / np.float32(ROPE_HALF))
    tok = np.arange(seq_len)
    by_row = (lane % DA_HEAD_DIM < ROPE_AXIS_DIM)[None, :]
    pos = np.where(by_row, (tok // GRID_W)[:, None], (tok % GRID_W)[:, None]).astype(np.float32)
    ang = pos * freqs[None, :]
    sign = np.where(r < ROPE_HALF, -1.0, 1.0).astype(np.float32)
    return jnp.asarray(np.cos(ang), F32), jnp.asarray(np.sin(ang) * sign[None, :], F32)


def kernel(x_prompt, x_sample, c, cache_k, cache_v, state_C, state_n, state_m, c_ctx, w_ada, b_ada, g_norm, ffn1_w1, ffn1_w3, ffn1_w2, ffn2_w1, ffn2_w3, ffn2_w2, w_in, b_gate, g_qn, g_kn, lam_q1, lam_k1, lam_q2, lam_k2, g_sub, g_mh, w_br_m, w_br_d, w_out):
    depth = w_ada.shape[0]
    assert depth == 1
    l = 0
    bp, tp, _ = x_prompt.shape
    bs, ts, _ = x_sample.shape
    past = cache_k.shape[2]
    lam_init = 0.8 - 0.6 * math.exp(-0.3 * l)

    cvecs = jnp.concatenate([c_ctx[None, :], c, jnp.zeros((MOD_ROWS - 1 - bs, D_MODEL), F32)], axis=0)
    mods = _mods(cvecs, w_ada[l], b_ada[l])

    group = np.arange(MXU_DIM) // DA_HEAD_DIM
    w = dict(
        g_norm=g_norm[l],
        ffn1_w1=ffn1_w1[l], ffn1_w3=ffn1_w3[l], ffn1_w2=ffn1_w2[l],
        ffn2_w1=ffn2_w1[l], ffn2_w3=ffn2_w3[l], ffn2_w2=ffn2_w2[l],
        w_in_t=w_in[l].T,
        b_gate=b_gate[l:l + 1],
        bd=jnp.asarray(group[:, None] == group[None, :], BF16),
        g_qn=g_qn[l:l + 1], g_kn=g_kn[l:l + 1],
        lam=(lam_q1[l:l + 1], lam_k1[l:l + 1], lam_q2[l:l + 1], lam_k2[l:l + 1]),
        g_sub=g_sub[l:l + 1], g_mh=g_mh[l],
        w_br_m=w_br_m[l], w_br_d=w_br_d[l], w_out=w_out[l],
    )

    n_ctx, n_lat = bp * tp, bs * ts
    mod_index = lambda r: jnp.where(r < n_ctx, 0, 1 + (r - n_ctx) // ts)
    seg_start = lambda j: pl.multiple_of(jnp.where(j < N_SEG, _seg_start(j), GATE_LO), N_GATE_COLS)
    x1, w["w_seg"] = _ffn((x_prompt.reshape(n_ctx, D_MODEL), x_sample.reshape(n_lat, D_MODEL)), (n_ctx + n_lat,),
                          mods, mod_index, w["g_norm"], w["ffn1_w1"], w["ffn1_w3"], w["ffn1_w2"], base=0,
                          side=(w["w_in_t"], N_SEG + 1, seg_start))

    mq, mk, mv, so, dq, sgm, sgd, gates, new_k_t, new_v, dk_lat, dv_lat = _proj(
        x1, n_ctx, tp, ts, mods, mod_index, w["g_norm"], w["w_seg"], w["b_gate"], w["bd"], w["g_qn"], w["g_kn"],
        _rope_tables(ts))
    att_p = _attn(dq, 0, new_k_t, new_v, None, w["lam"], w["g_sub"], bp, tp, lam_init)
    hm_p, new_c, new_n, new_m = _mlstm(mq, mk, mv, so, gates, 0, w["g_mh"], None, bp, tp, emit_state=True)

    cache = (cache_k[:, l].transpose(0, 2, 3, 4, 1).reshape(bs, D_MODEL, past),
             cache_v[:, l].reshape(bs, past * DA_HEADS, DA_V_DIM))
    att_s = _attn(dq, n_ctx, dk_lat, dv_lat, cache, w["lam"], w["g_sub"], bs, ts, lam_init)
    (hm_s,) = _mlstm(mq, mk, mv, so, gates, n_ctx, w["g_mh"], (state_C, state_n, state_m), bs, ts,
                     emit_state=False)

    x2 = _merge(x1, mods, mod_index, (hm_p, hm_s), (att_p, att_s), sgm, sgd, w["w_br_m"], w["w_br_d"], w["w_out"])
    xp, xs = _ffn((x2,), (n_ctx, n_lat), mods, mod_index, w["g_norm"], w["ffn2_w1"], w["ffn2_w3"], w["ffn2_w2"],
                  base=6)

    return (xp.reshape(bp, tp, D_MODEL), xs.reshape(bs, ts, D_MODEL),
            new_k_t.reshape(bp, DA_HEADS, 2, DA_HEAD_DIM, tp).transpose(0, 4, 1, 2, 3)[:, None],
            new_v.reshape(bp, 1, tp, DA_HEADS, DA_V_DIM),
            new_c, new_n, new_m)
```

```python
import functools
import math

import jax
import jax.numpy as jnp
import numpy as np
from jax import lax
from jax.experimental import pallas as pl
from jax.experimental.pallas import tpu as pltpu

F32 = jnp.float32
BF16 = jnp.bfloat16

D_MODEL = 1024
D_FF = 2816
N_MOD = 9
GRID_W = 64
ML_HEADS = 4
ML_HEAD_DIM = 256
DA_HEADS = 8
DA_HEAD_DIM = 64
DA_V_DIM = 128
N_GATE_COLS = 16
CHUNK = 256
ROPE_BASE = 10000.0
QK_LOG2_SCALE = DA_HEAD_DIM ** -0.5 * math.log2(math.e)
EPS = 1e-6

LANES = 128
SUBLANES = 8
GATE_PAD = LANES
MXU_DIM = 256
VMEM_LIMIT = 56 * 1024 * 1024

SIDE_PIECES = 8
MOD_ROWS = SUBLANES
FFN_TM = 512
FFN_TF = 256
PROJ_TM = 256
MERGE_TM = 512
ATT_TQ = 256
ATT_TK_MAX = 768
ATT_SHORT_SEQ_BATCH = 2


def _params(n_axes):
    return pltpu.CompilerParams(dimension_semantics=("arbitrary",) * n_axes,
                                vmem_limit_bytes=VMEM_LIMIT)


def _dot(a, b):
    return jnp.dot(a.astype(BF16), b.astype(BF16), preferred_element_type=F32)


def _dot_nt(a, b):
    return lax.dot_general(a.astype(BF16), b.astype(BF16), (((1,), (1,)), ((), ())),
                           preferred_element_type=F32)


def _sigmoid(x):
    return 1.0 / (1.0 + jnp.exp(-x))


def _log_sigmoid(x):
    return jnp.minimum(x, 0.0) - jnp.log1p(jnp.exp(-jnp.abs(x)))


def _modulated_norm(x, g, shift, scale):
    y = x * lax.rsqrt(jnp.mean(x * x, axis=-1, keepdims=True) + EPS) * g
    return y * (1.0 + scale) + shift


def _resident(shape):
    return pl.BlockSpec(shape, lambda *_: (0,) * len(shape), pipeline_mode=pl.Buffered(1))


MODS_TN = 512
MODS_IN_FLIGHT = 4


def _mods_kernel(c_ref, w_hbm, b_ref, o_ref, w_sc, sem):
    n_chunks = o_ref.shape[1] // MODS_TN

    def copy(j):
        slot = j % MODS_IN_FLIGHT
        return pltpu.make_async_copy(w_hbm.at[:, j * MODS_TN:(j + 1) * MODS_TN], w_sc.at[slot], sem.at[slot])

    for j in range(min(MODS_IN_FLIGHT, n_chunks)):
        copy(j).start(priority=j % 2)
    c = c_ref[...]
    a = (c * _sigmoid(c)).astype(BF16)
    for j in range(n_chunks):
        cols = slice(j * MODS_TN, (j + 1) * MODS_TN)
        copy(j).wait()
        o_ref[:, cols] = _dot(a, w_sc[j % MODS_IN_FLIGHT]) + b_ref[:, cols]
        if j + MODS_IN_FLIGHT < n_chunks:
            copy(j + MODS_IN_FLIGHT).start(priority=(j + MODS_IN_FLIGHT) % 2)


def _mods(cvecs, w_ada, b_ada):
    n = N_MOD * D_MODEL
    assert n % MODS_TN == 0
    return pl.pallas_call(
        _mods_kernel,
        in_specs=[pl.BlockSpec(memory_space=pltpu.VMEM),
                  pl.BlockSpec(memory_space=pl.ANY),
                  pl.BlockSpec(memory_space=pltpu.VMEM)],
        out_specs=pl.BlockSpec(memory_space=pltpu.VMEM),
        out_shape=jax.ShapeDtypeStruct((MOD_ROWS, n), F32),
        scratch_shapes=[pltpu.VMEM((MODS_IN_FLIGHT, D_MODEL, MODS_TN), F32),
                        pltpu.SemaphoreType.DMA((MODS_IN_FLIGHT,))],
        compiler_params=pltpu.CompilerParams(vmem_limit_bytes=VMEM_LIMIT),
        name="adaln_mods",
    )(cvecs, w_ada, b_ada.reshape(1, n))


def _mod_reader(m_ref, row):
    return lambda j: m_ref[pl.ds(row, 1), j * D_MODEL:(j + 1) * D_MODEL]


def _ffn_kernel(*refs, base, mod_index, in_tiles, out_tiles, side_blocks):
    n_in, n_out = len(in_tiles), len(out_tiles)
    x_refs, refs = refs[:n_in], refs[n_in:]
    (m_ref, g_ref, w1_ref, w3_ref, w2_ref), refs = refs[:5], refs[5:]
    if side_blocks:
        side_in_ref, refs = refs[0], refs[1:]
        side_out_ref, refs = refs[n_out], refs[:n_out] + refs[n_out + 1:]
    o_refs, (w1_sc, w3_sc, w2_sc, hh_sc, acc_sc) = refs[:n_out], refs[n_out:]
    g = pl.program_id(0)
    nf = D_FF // FFN_TF
    row_tile = g - (nf - 1)
    mod = _mod_reader(m_ref, mod_index(jnp.maximum(row_tile, 0) * FFN_TM))
    gain = g_ref[base // 3:base // 3 + 1, :]
    norm = lambda x: _modulated_norm(x, gain, mod(base), mod(base + 1)).astype(BF16)
    finish = lambda x, acc: x + 0.5 * mod(base + 2) * acc

    def tile(hh, f):
        a = _dot(hh, w1_sc[f])
        b = _dot(hh, w3_sc[f])
        return _dot(a * _sigmoid(a) * b, w2_sc[f])

    @pl.when(g < nf)
    def _():
        w1_sc[g] = w1_ref[...].astype(BF16)
        w3_sc[g] = w3_ref[...].astype(BF16)
        w2_sc[g] = w2_ref[...].astype(BF16)

        @pl.when(g == 0)
        def _():
            hh_sc[...] = norm(x_refs[0][...])
            acc_sc[...] = jnp.zeros(acc_sc.shape, F32)

        acc_sc[...] += tile(hh_sc[...], g)

        @pl.when(g == nf - 1)
        def _():
            o_refs[0][...] = finish(x_refs[0][...], acc_sc[...])

    def full_tile(x_ref, o_ref):
        x = x_ref[...]
        hh = norm(x)
        acc = jnp.zeros(x.shape, F32)
        for f in range(nf):
            acc = acc + tile(hh, f)
            if side_blocks and f < SIDE_PIECES:
                rows = slice(f * (D_MODEL // SIDE_PIECES), (f + 1) * (D_MODEL // SIDE_PIECES))
                side_out_ref[rows, :] = side_in_ref[rows, :].astype(BF16)
        o_ref[...] = finish(x, acc)

    bounds = sorted(set(np.cumsum((0,) + in_tiles).tolist()) | set(np.cumsum((0,) + out_tiles).tolist()))
    for lo, hi in zip(bounds[:-1], bounds[1:]):
        k_in = int(np.searchsorted(np.cumsum(in_tiles), lo, side="right"))
        k_out = int(np.searchsorted(np.cumsum(out_tiles), lo, side="right"))
        pl.when((g >= nf) & (row_tile >= lo) & (row_tile < hi))(
            functools.partial(full_tile, x_refs[k_in], o_refs[k_out]))


def _ffn(xs, out_rows, mods, mod_index, g, w1, w3, w2, base, side=None):
    nf = D_FF // FFN_TF
    in_tiles = tuple(x.shape[0] // FFN_TM for x in xs)
    out_tiles = tuple(r // FFN_TM for r in out_rows)
    assert sum(in_tiles) == sum(out_tiles)
    side_in_specs, side_out_specs, side_out_shape, side_args, side_blocks = [], [], [], [], 0
    if side is not None:
        side_w, side_blocks, side_start = side
        assert side_blocks <= sum(in_tiles) - 1
        blk = lambda s: jnp.clip(s - nf, 0, side_blocks - 1)
        side_in_specs = [pl.BlockSpec((pl.Element(D_MODEL), pl.Element(D_MODEL)), lambda s: (side_start(blk(s)), 0))]
        side_out_specs = [pl.BlockSpec((None, D_MODEL, D_MODEL), lambda s: (blk(s), 0, 0))]
        side_out_shape = [jax.ShapeDtypeStruct((side_blocks, D_MODEL, D_MODEL), BF16)]
        side_args = [side_w]
    row_tile = lambda s: jnp.maximum(s - (nf - 1), 0)
    f_tile = lambda s: jnp.minimum(s, nf - 1)

    def part_spec(tiles, k):
        start = sum(tiles[:k])
        return pl.BlockSpec((FFN_TM, D_MODEL), lambda s: (jnp.clip(row_tile(s) - start, 0, tiles[k] - 1), 0))

    return pl.pallas_call(
        functools.partial(_ffn_kernel, base=base, mod_index=mod_index, in_tiles=in_tiles, out_tiles=out_tiles,
                          side_blocks=side_blocks),
        grid=(nf - 1 + sum(in_tiles),),
        in_specs=[part_spec(in_tiles, k) for k in range(len(xs))] + [
            _resident(mods.shape),
            _resident(g.shape),
            pl.BlockSpec((D_MODEL, FFN_TF), lambda s: (0, f_tile(s))),
            pl.BlockSpec((D_MODEL, FFN_TF), lambda s: (0, f_tile(s))),
            pl.BlockSpec((FFN_TF, D_MODEL), lambda s: (f_tile(s), 0))] + side_in_specs,
        out_specs=[part_spec(out_tiles, k) for k in range(len(out_rows))] + side_out_specs,
        out_shape=[jax.ShapeDtypeStruct((r, D_MODEL), F32) for r in out_rows] + side_out_shape,
        scratch_shapes=[pltpu.VMEM((nf, D_MODEL, FFN_TF), BF16), pltpu.VMEM((nf, D_MODEL, FFN_TF), BF16),
                        pltpu.VMEM((nf, FFN_TF, D_MODEL), BF16),
                        pltpu.VMEM((FFN_TM, D_MODEL), BF16), pltpu.VMEM((FFN_TM, D_MODEL), F32)],
        compiler_params=_params(1),
        name="ffn",
    )(*xs, mods, g, w1, w3, w2, *side_args)


def _group_norm64(x, bd, g):
    ss = _dot(x * x, bd)
    return x * lax.rsqrt(ss * (1.0 / DA_HEAD_DIM) + EPS) * g


ROPE_AXIS_DIM = DA_HEAD_DIM // 2
ROPE_HALF = ROPE_AXIS_DIM // 2


def _rope(x, cos, sin_signed):
    first = (lax.broadcasted_iota(jnp.int32, x.shape, 1) % ROPE_AXIS_DIM) < ROPE_HALF
    partner = jnp.where(first, pltpu.roll(x, LANES - ROPE_HALF, 1), pltpu.roll(x, ROPE_HALF, 1))
    return x * cos + partner * sin_signed


N_SEG = 9
PROJ_WARM = (N_SEG + 1) // 2
GATE_LO = 4 * D_MODEL


def _seg_start(s):
    return s * D_MODEL + N_GATE_COLS * (s >= 4)


def _proj_kernel(*refs, ctx_tiles, row_of_step):
    (x_ref, m_ref, g_ref, wt_ref, bg_ref, bd_ref, gq_ref, gk_ref, cos_ref, sin_ref), refs = refs[:10], refs[10:]
    (mq_ref, mk_ref, mv_ref, so_ref, dq_ref, sgm_ref, sgd_ref, gates_ref,
     dk_ctx_ref, dv_ctx_ref, dk_lat_ref, dv_lat_ref, w_sc, wg_sc, hh_sc) = refs
    step = pl.program_id(0)
    mod = _mod_reader(m_ref, row_of_step(step))
    norm = lambda: _modulated_norm(x_ref[...], g_ref[1:2, :], mod(3), mod(4)).astype(BF16)
    groups = MXU_DIM // DA_HEAD_DIM
    tiled = lambda ref, scale: jnp.concatenate([ref[...] * scale] * groups, axis=1)
    gq_row, gk_row = tiled(gq_ref, QK_LOG2_SCALE), tiled(gk_ref, 1.0)
    bias = jnp.concatenate([bg_ref[...], jnp.zeros((1, GATE_PAD - N_GATE_COLS), F32)], axis=1)

    def gates(hh):
        y = _dot_nt(hh, wg_sc[...]) + bias
        col = lax.broadcasted_iota(jnp.int32, y.shape, 1)
        gates_ref[...] = jnp.where((col // ML_HEADS) % 2 == 1, _log_sigmoid(y), y)

    def qk_segment(y, gain_row, o_ref, rope):
        bd = bd_ref[...]
        for c in range(D_MODEL // MXU_DIM):
            cols = slice(c * MXU_DIM, (c + 1) * MXU_DIM)
            z = _group_norm64(y[:, cols], bd, gain_row)
            if rope:
                z = jnp.concatenate(
                    [_rope(z[:, k * LANES:(k + 1) * LANES], cos_ref[...], sin_ref[...])
                     for k in range(MXU_DIM // LANES)], axis=1)
            o_ref[:, cols] = z.astype(o_ref.dtype)

    def cache_key_segment(hh):
        y_t = _dot_nt(w_sc[5], hh)
        z = y_t.reshape(D_MODEL // DA_HEAD_DIM, DA_HEAD_DIM, y_t.shape[1])
        ms = jnp.mean(z * z, axis=1, keepdims=True)
        gain = jnp.broadcast_to(gk_row[:, 0:LANES], (LANES, LANES)).T[0:DA_HEAD_DIM, :]
        gain = jnp.concatenate([gain] * (y_t.shape[1] // LANES), axis=1)
        dk_ctx_ref[...] = (z * lax.rsqrt(ms + EPS) * gain).reshape(y_t.shape)

    def segment(s, hh, ctx):
        if s in (0, 2):
            o_ref = {0: mq_ref, 2: mv_ref}[s]
            y_t = _dot_nt(w_sc[s], hh) * ((ML_HEAD_DIM ** -0.5) if s == 0 else 1.0)
            for j in range(PROJ_TM // CHUNK):
                o_ref[j] = y_t[:, j * CHUNK:(j + 1) * CHUNK].astype(o_ref.dtype)
            return
        if s == 5 and ctx:
            cache_key_segment(hh)
            return
        y = _dot_nt(hh, w_sc[s])
        if s == 4:
            qk_segment(y, gq_row, dq_ref, rope=not ctx)
        elif s == 5:
            qk_segment(y, gk_row, dk_lat_ref, rope=True)
        elif s in (3, 7, 8):
            {3: so_ref, 7: sgm_ref, 8: sgd_ref}[s][...] = _sigmoid(y)
        elif s == 6 and ctx:
            dv_ctx_ref[...] = y.reshape(y.shape[0], DA_HEADS, DA_V_DIM)
        elif s == 6:
            dv_lat_ref[...] = y.astype(dv_lat_ref.dtype)
        else:
            mk_ref[...] = y.astype(mk_ref.dtype)

    for w in range(PROJ_WARM):
        @pl.when(step == w)
        def _(w=w):
            if w == 0:
                hh_sc[...] = norm()
            for j in range(2):
                s = 2 * w + j
                if s < N_SEG:
                    w_sc[s] = wt_ref[j]
                    segment(s, hh_sc[...], ctx=True)
                else:
                    wg_sc[...] = jnp.zeros(wg_sc.shape, BF16)
                    wg_sc[0:N_GATE_COLS, :] = wt_ref[j, 0:N_GATE_COLS, :]
                    gates(hh_sc[...])

    def full_tile(ctx):
        hh = norm()
        gates(hh)
        for s in range(N_SEG):
            segment(s, hh, ctx)

    is_ctx = step - (PROJ_WARM - 1) < ctx_tiles
    pl.when((step >= PROJ_WARM) & is_ctx)(functools.partial(full_tile, True))
    pl.when((step >= PROJ_WARM) & jnp.logical_not(is_ctx))(functools.partial(full_tile, False))


def _proj(x, n_ctx, seq_ctx, seq_lat, mods, mod_index, g, w_seg, b_gate, bd, gq, gk, rope_tabs):
    rows = x.shape[0]
    n_lat = rows - n_ctx
    assert seq_ctx == PROJ_TM
    ctx_tiles, lat_tiles = n_ctx // PROJ_TM, n_lat // PROJ_TM
    row_tile = lambda s: jnp.maximum(s - (PROJ_WARM - 1), 0)
    ctx_tile = lambda s: jnp.minimum(row_tile(s), ctx_tiles - 1)
    lat_tile = lambda s: jnp.clip(row_tile(s) - ctx_tiles, 0, lat_tiles - 1)
    row = lambda s: (row_tile(s), 0)
    tile = pl.BlockSpec((PROJ_TM, D_MODEL), row)
    w_block = (2, D_MODEL, D_MODEL)
    tab = pl.BlockSpec((PROJ_TM, LANES), lambda s: (lat_tile(s) % (seq_lat // PROJ_TM), 0))
    in_specs = [tile,
                _resident(mods.shape),
                _resident(g.shape),
                pl.BlockSpec(w_block, lambda s: (jnp.minimum(s, PROJ_WARM - 1), 0, 0)),
                _resident(b_gate.shape),
                _resident((MXU_DIM, MXU_DIM)),
                _resident(gq.shape),
                _resident(gk.shape),
                tab, tab]
    slab = pl.BlockSpec((PROJ_TM // CHUNK, D_MODEL, CHUNK), lambda s: (row_tile(s), 0, 0))
    slab_shape = jax.ShapeDtypeStruct((rows // CHUNK, D_MODEL, CHUNK), BF16)
    full = lambda dt: jax.ShapeDtypeStruct((rows, D_MODEL), dt)
    out_specs = [slab, tile, slab, tile, tile, tile, tile, pl.BlockSpec((PROJ_TM, GATE_PAD), row),
                 pl.BlockSpec((None, D_MODEL, seq_ctx), lambda s: (ctx_tile(s), 0, 0)),
                 pl.BlockSpec((PROJ_TM, DA_HEADS, DA_V_DIM), lambda s: (ctx_tile(s), 0, 0)),
                 pl.BlockSpec((PROJ_TM, D_MODEL), lambda s: (lat_tile(s), 0)),
                 pl.BlockSpec((PROJ_TM, D_MODEL), lambda s: (lat_tile(s), 0))]
    out_shape = [slab_shape, full(BF16), slab_shape, full(F32), full(BF16), full(F32), full(F32),
                 jax.ShapeDtypeStruct((rows, GATE_PAD), F32),
                 jax.ShapeDtypeStruct((n_ctx // seq_ctx, D_MODEL, seq_ctx), F32),
                 jax.ShapeDtypeStruct((n_ctx, DA_HEADS, DA_V_DIM), F32),
                 jax.ShapeDtypeStruct((n_lat, D_MODEL), BF16),
                 jax.ShapeDtypeStruct((n_lat, D_MODEL), BF16)]
    return pl.pallas_call(
        functools.partial(_proj_kernel, ctx_tiles=ctx_tiles,
                          row_of_step=lambda step: mod_index(row_tile(step) * PROJ_TM)),
        grid=(PROJ_WARM - 1 + rows // PROJ_TM,),
        in_specs=in_specs,
        out_specs=out_specs,
        out_shape=out_shape,
        scratch_shapes=[pltpu.VMEM((N_SEG, D_MODEL, D_MODEL), BF16), pltpu.VMEM((GATE_PAD, D_MODEL), BF16),
                        pltpu.VMEM((PROJ_TM, D_MODEL), BF16)],
        compiler_params=_params(1),
        name="mixer_in_proj",
    )(x, mods, g, w_seg, b_gate, bd, gq, gk, *rope_tabs)


def _lambda(lam_refs, lam_init):
    q1, k1, q2, k2 = (r[...] for r in lam_refs)
    s1 = jnp.sum(q1 * k1, axis=1, keepdims=True)
    s2 = jnp.sum(q2 * k2, axis=1, keepdims=True)
    return jnp.exp(s1) - jnp.exp(s2) + lam_init


def _attn_kernel(*refs, cached, seq, nb, lam_init):
    if cached:
        (q_ref, k_ref, v_ref, ck_ref, cv_ref), refs = refs[:5], refs[5:]
    else:
        (q_ref, k_ref, v_ref), refs = refs[:3], refs[3:]
    lam_refs, (gs_ref, o_ref, kall_sc, vt_sc, s_sc) = refs[:4], refs[4:]
    units = [(bb, h) for bb in range(nb) for h in range(DA_HEADS)]

    def cache_order_kv(kt_ref, vr_ref, u, h, lo, n):
        kall_sc[u, lo:lo + n, :] = kt_ref[h * DA_V_DIM:(h + 1) * DA_V_DIM, :].T.astype(BF16)
        vt_sc[u, :, lo:lo + n] = vr_ref[pl.ds(h, n, stride=DA_HEADS), :].T.astype(BF16)

    @pl.when(pl.program_id(1) == 0)
    def _():
        for u, (bb, h) in enumerate(units):
            if cached:
                cols = slice(h * DA_V_DIM, (h + 1) * DA_V_DIM)
                kall_sc[u, 0:seq, :] = k_ref[:, cols].astype(BF16)
                vt_sc[u, :, 0:seq] = v_ref[:, cols].astype(F32).T.astype(BF16)
                cache_order_kv(ck_ref.at[bb], cv_ref.at[bb], u, h, seq, ck_ref.shape[2])
            else:
                cache_order_kv(k_ref.at[bb], v_ref.at[bb], u, h, 0, seq)

    lam = _lambda(lam_refs, lam_init)
    sub_gain = gs_ref[...] * (1.0 - lam_init)
    lane = lax.broadcasted_iota(jnp.int32, (1, DA_V_DIM), 1)
    comp_masks = [lane < DA_HEAD_DIM, lane >= DA_HEAD_DIM]
    tq = q_ref.shape[0] // nb
    n_keys = kall_sc.shape[1]
    n_tiles = pl.cdiv(n_keys, ATT_TK_MAX)
    tk = n_keys // n_tiles

    def stacked_q(u):
        bb, h = units[u]
        q = q_ref[bb * tq:(bb + 1) * tq, h * DA_V_DIM:(h + 1) * DA_V_DIM].astype(BF16)
        return jnp.concatenate([jnp.where(m, q, jnp.zeros_like(q)) for m in comp_masks], axis=0)

    def score_tile(h, j, qq, m8):
        rows = slice(j * tk, (j + 1) * tk)
        st = _dot_nt(kall_sc[h, rows, :], qq)
        s_sc[h % 2, rows, :] = st
        t8 = jnp.max(st.reshape(tk // SUBLANES, SUBLANES, 2 * tq), axis=0)
        return t8 if m8 is None else jnp.maximum(m8, t8)

    def prob_tile(h, j, mx, d8, pv):
        rows = slice(j * tk, (j + 1) * tk)
        e = jnp.exp2(s_sc[h % 2, rows, :] - mx)
        s8 = jnp.sum(e.reshape(tk // SUBLANES, SUBLANES, 2 * tq), axis=0)
        p = _dot(vt_sc[h, :, rows], e)
        return (s8 if d8 is None else d8 + s8), (p if pv is None else pv + p)

    qq = stacked_q(0)
    m8 = None
    for j in range(n_tiles):
        m8 = score_tile(0, j, qq, m8)
    for h in range(len(units)):
        mx = jnp.max(m8, axis=0, keepdims=True)
        if h + 1 < len(units):
            qq = stacked_q(h + 1)
        m8, d8, pv = None, None, None
        for j in range(n_tiles):
            if h + 1 < len(units):
                m8 = score_tile(h + 1, j, qq, m8)
            d8, pv = prob_tile(h, j, mx, d8, pv)
        inv = 1.0 / jnp.sum(d8, axis=0, keepdims=True)
        out_t = pv[:, :tq] * inv[:, :tq] - pv[:, tq:] * (lam * inv[:, tq:])
        out_t = out_t * lax.rsqrt(jnp.mean(out_t * out_t, axis=0, keepdims=True) + EPS)
        bb, head = units[h]
        o_ref[bb * tq:(bb + 1) * tq, head * DA_V_DIM:(head + 1) * DA_V_DIM] = (out_t.T * sub_gain).astype(o_ref.dtype)


def _attn(q, q_row0, k, v, cache, lam_vecs, g_sub, batch, seq_len, lam_init):
    rows = batch * seq_len
    nq = seq_len // ATT_TQ
    nb = ATT_SHORT_SEQ_BATCH if (cache is None and nq == 1) else 1
    assert batch % nb == 0
    q_spec = pl.BlockSpec((nb * ATT_TQ, D_MODEL), lambda b, i: (q_row0 // (nb * ATT_TQ) + b * nq + i, 0))
    o_spec = pl.BlockSpec((nb * ATT_TQ, D_MODEL), lambda b, i: (b * nq + i, 0))
    cache_specs = lambda n: [pl.BlockSpec((nb, D_MODEL, n), lambda b, i: (b, 0, 0)),
                             pl.BlockSpec((nb, n * DA_HEADS, DA_V_DIM), lambda b, i: (b, 0, 0))]
    n_keys = seq_len
    if cache is not None:
        past = cache[0].shape[2]
        n_keys += past
        kv_spec = pl.BlockSpec((seq_len, D_MODEL), lambda b, i: (b, 0))
        in_specs = [q_spec, kv_spec, kv_spec] + cache_specs(past)
        args = [q, k, v] + list(cache)
    else:
        in_specs = [q_spec] + cache_specs(seq_len)
        args = [q, k, v.reshape(batch, seq_len * DA_HEADS, DA_V_DIM)]
    in_specs += [_resident((1, DA_HEAD_DIM))] * 4 + [_resident((1, DA_V_DIM))]
    args += list(lam_vecs) + [g_sub]
    return pl.pallas_call(
        functools.partial(_attn_kernel, cached=cache is not None, seq=seq_len, nb=nb, lam_init=lam_init),
        grid=(batch // nb, nq),
        in_specs=in_specs,
        out_specs=o_spec,
        out_shape=jax.ShapeDtypeStruct((rows, D_MODEL), BF16),
        scratch_shapes=[pltpu.VMEM((nb * DA_HEADS, n_keys, DA_V_DIM), BF16),
                        pltpu.VMEM((nb * DA_HEADS, DA_V_DIM, n_keys), BF16),
                        pltpu.VMEM((2, n_keys, 2 * ATT_TQ), F32)],
        compiler_params=_params(2),
        name="diff_attention",
    )(*args)


def _per_chain(fn, a, b):
    return jnp.stack([fn(a[i], b[i]) for i in range(a.shape[0])])


ML_EXT = SUBLANES
ML_SHORT_SEQ_BATCH = 2


def _rows(x):
    return jnp.stack([x[i:i + 1, :] for i in range(x.shape[0])])


def _split3(x):
    hi = x.astype(BF16)
    r = x - hi.astype(F32)
    mid = r.astype(BF16)
    lo = (r - mid.astype(F32)).astype(BF16)
    return jnp.concatenate([hi, mid, lo], axis=1)


def _chunk_scan(x, reverse_rows, tri_prefix, tri_suffix):
    parts = _split3(x)
    return jnp.where(reverse_rows, _dot(parts, tri_suffix), _dot(parts, tri_prefix))


def _paired_value_matmul(v_t, sc):
    B, L, _ = sc.shape
    zero = jnp.zeros((L, L), sc.dtype)
    out = []
    for i in range(0, B, 2):
        lhs = jnp.concatenate([v_t[i], v_t[i + 1]], axis=1)
        rhs = jnp.concatenate([jnp.concatenate([sc[i], zero], axis=1),
                               jnp.concatenate([zero, sc[i + 1]], axis=1)], axis=0)
        both = _dot(lhs, rhs)
        out += [both[:, :L], both[:, L:]]
    return jnp.stack(out)


def _mlstm_step(k, q_t, v_t, ic, fc, CT, m, seen_t, reverse_rows):
    B, L = ic.shape
    d = k.shape[2]
    zero_state = CT is None
    if zero_state:
        m = jnp.zeros((B, 1, 1), F32)
    tri = lambda keep: jnp.concatenate([jnp.where(keep, 1.0, 0.0).astype(BF16)] * 3, axis=0)
    b2 = _chunk_scan(fc, reverse_rows, tri(seen_t[0]), tri(seen_t[B - 1]))
    u2 = ic - b2
    u_t = jnp.concatenate([u2, jnp.zeros((L - B, L), F32)], axis=0).T
    u_col = jnp.stack([jnp.broadcast_to(u_t[:, i:i + 1], (L, L)) for i in range(B)])
    b, i_g, f_g = _rows(b2), _rows(ic), _rows(fc)
    b_last = jnp.sum(f_g, axis=2, keepdims=True)

    log_d = jnp.where(seen_t, b + u_col, -jnp.inf)
    a = b + m
    m_t = jnp.maximum(a, jnp.max(log_d, axis=1, keepdims=True))
    dmat = jnp.exp(log_d - m_t)
    sc = _per_chain(_dot, k, q_t) * dmat
    value_matmul = _paired_value_matmul if 2 * L <= MXU_DIM else functools.partial(_per_chain, _dot)
    num = value_matmul(v_t, sc.astype(BF16))
    den = jnp.sum(sc, axis=1, keepdims=True)
    if not zero_state:
        inter = jnp.exp(a - m_t)
        cq = _per_chain(_dot, CT, q_t)
        num = num + inter * cq[:, :d, :]
        den = den + inter * cq[:, d:d + 1, :]
    h_t = num * (1.0 / jnp.maximum(jnp.abs(den), jnp.exp(-m_t)))

    g = b_last - b + i_g
    m_new = jnp.maximum(b_last + m, jnp.max(g, axis=2, keepdims=True))
    w = jnp.exp(g - m_new)
    decay = jnp.exp(b_last + m - m_new)
    vw = jnp.concatenate([v_t.astype(F32), jnp.ones((B, ML_EXT, L), F32)], axis=1) * w
    CT_new = _per_chain(_dot, vw, k)
    if not zero_state:
        CT_new = decay * CT + CT_new
    return h_t, CT_new, m_new


def _mlstm_kernel(*refs, seq_len, nb, has_state, emit_state):
    q_ref, k_ref, v_ref, so_ref, g_ref, gmh_ref = refs[:6]
    refs = refs[6:]
    if has_state:
        (c0_ref, n0_ref, m0_ref), refs = refs[:3], refs[3:]
    hm_ref, refs = refs[0], refs[1:]
    if emit_state:
        (c_out_ref, n_out_ref, m_out_ref), refs = refs[:3], refs[3:]
    ct_sc, m_sc, gr_sc, h_sc = refs

    nc = seq_len // CHUNK
    d_head = ML_HEAD_DIM
    chains = [(bb, d, h) for bb in range(nb) for d in range(2) for h in range(ML_HEADS)]
    n_chain = len(chains)
    for i, (bb, d, h) in enumerate(chains):
        if has_state:
            ct_sc[i, 0:d_head, :] = c0_ref[bb, d, h].T
            ct_sc[i, d_head:d_head + ML_EXT, :] = jnp.broadcast_to(n0_ref[bb, d, h:h + 1, :], (ML_EXT, d_head))
            m_sc[i] = jnp.full((1, 1), m0_ref[pl.program_id(0) * nb + bb, 0, d, h], F32)
        elif nc > 2:
            ct_sc[i] = jnp.zeros((d_head + ML_EXT, d_head), F32)
            m_sc[i] = jnp.zeros((1, 1), F32)
    for c in range(nb * nc):
        gr_sc[c] = g_ref[c * CHUNK:(c + 1) * CHUNK, :].T

    s_idx = lax.broadcasted_iota(jnp.int32, (CHUNK, CHUNK), 0)
    t_idx = lax.broadcasted_iota(jnp.int32, (CHUNK, CHUNK), 1)
    seen_t = jnp.stack([(s_idx >= t_idx) if d else (s_idx <= t_idx) for _, d, _ in chains])
    reverse_rows = (lax.broadcasted_iota(jnp.int32, (n_chain, CHUNK), 0) // ML_HEADS) % 2 == 1

    def step(c_fwd, c_bwd, rows_of, zero_state=False):
        chunk_of = (c_fwd, c_bwd)
        lo = 2 * ML_HEADS
        ic, fc = [], []
        for bb in range(nb):
            g_fwd, g_bwd = gr_sc[bb * nc + c_fwd], gr_sc[bb * nc + c_bwd]
            ic += [g_fwd[0:ML_HEADS], g_bwd[lo:lo + ML_HEADS]]
            fc += [g_fwd[ML_HEADS:lo], g_bwd[lo + ML_HEADS:2 * lo]]
        head = [slice(h * d_head, (h + 1) * d_head) for _, _, h in chains]
        stack = lambda pick: jnp.stack([pick(i, bb * nc + chunk_of[d]) for i, (bb, d, _) in enumerate(chains)])
        h_t, CT_new, m_new = _mlstm_step(
            stack(lambda i, c: k_ref[rows_of(c), head[i]]),
            stack(lambda i, c: q_ref[c, head[i], :]),
            stack(lambda i, c: v_ref[c, head[i], :]),
            jnp.concatenate(ic, axis=0), jnp.concatenate(fc, axis=0),
            None if zero_state else ct_sc[...], None if zero_state else m_sc[...], seen_t, reverse_rows)
        ct_sc[...] = CT_new
        m_sc[...] = m_new
        for i, (bb, d, _) in enumerate(chains):
            h_sc[d, bb * nc + chunk_of[d], head[i], :] = h_t[i]

    if nc <= 2:
        for c in range(nc):
            step(c, nc - 1 - c, lambda cc: slice(cc * CHUNK, (cc + 1) * CHUNK),
                 zero_state=(c == 0 and not has_state))
    else:
        def body(c, carry):
            step(c, nc - 1 - c, lambda cc: pl.ds(pl.multiple_of(cc * CHUNK, CHUNK), CHUNK))
            return carry
        lax.fori_loop(0, nc, body, 0, unroll=2)

    for h in range(ML_HEADS):
        hcols = slice(h * d_head, (h + 1) * d_head)
        gain = jnp.broadcast_to(gmh_ref[h:h + 1, :], (CHUNK, d_head)).T
        for c in range(nb * nc):
            rows = slice(c * CHUNK, (c + 1) * CHUNK)
            hsum = h_sc[0, c, hcols, :] + h_sc[1, c, hcols, :]
            hn = hsum * lax.rsqrt(jnp.mean(hsum * hsum, axis=0, keepdims=True) + EPS) * gain
            hm_ref[rows, hcols] = (hn.T * so_ref[rows, hcols]).astype(hm_ref.dtype)
    if emit_state:
        for i, (bb, d, h) in enumerate(chains):
            c_out_ref[bb, d, h] = ct_sc[i, 0:d_head, :].T
            n_out_ref[bb, d, h:h + 1, :] = ct_sc[i, d_head:d_head + 1, :]
            m_out_ref[bb, d:d + 1, h:h + 1] = m_sc[i]


def _mlstm(q_t, k, v_t, so, gates, row0, g_mh, state, batch, seq_len, emit_state):
    rows = batch * seq_len
    d = ML_HEAD_DIM
    nc = seq_len // CHUNK
    nb = ML_SHORT_SEQ_BATCH if nc <= 2 else 1
    assert batch % nb == 0
    assert row0 % (nb * seq_len) == 0
    first = row0 // (nb * seq_len)
    tile = pl.BlockSpec((nb * seq_len, D_MODEL), lambda b: (first + b, 0))
    tile_t = pl.BlockSpec((nb * nc, D_MODEL, CHUNK), lambda b: (first + b, 0, 0))
    in_specs = [tile_t, tile, tile_t, tile,
                pl.BlockSpec((nb * seq_len, GATE_PAD), lambda b: (first + b, 0)),
                _resident((ML_HEADS, d))]
    args = [q_t, k, v_t, so, gates, g_mh]
    c_spec = pl.BlockSpec((nb, None, 2, ML_HEADS, d, d), lambda b: (b, 0, 0, 0, 0, 0))
    n_spec = pl.BlockSpec((nb, None, 2, ML_HEADS, d), lambda b: (b, 0, 0, 0, 0))
    if state is not None:
        in_specs += [c_spec, n_spec, pl.BlockSpec(memory_space=pltpu.SMEM)]
        args += list(state)
    out_specs = [pl.BlockSpec((nb * seq_len, D_MODEL), lambda b: (b, 0))]
    out_shape = [jax.ShapeDtypeStruct((rows, D_MODEL), BF16)]
    if emit_state:
        out_specs += [c_spec, n_spec, pl.BlockSpec((nb, None, 2, ML_HEADS), lambda b: (b, 0, 0, 0))]
        out_shape += [jax.ShapeDtypeStruct((batch, 1, 2, ML_HEADS, d, d), F32),
                      jax.ShapeDtypeStruct((batch, 1, 2, ML_HEADS, d), F32),
                      jax.ShapeDtypeStruct((batch, 1, 2, ML_HEADS), F32)]
    n_state = 2 * ML_HEADS * nb
    return pl.pallas_call(
        functools.partial(_mlstm_kernel, seq_len=seq_len, nb=nb, has_state=state is not None,
                          emit_state=emit_state),
        grid=(batch // nb,),
        in_specs=in_specs,
        out_specs=out_specs,
        out_shape=out_shape,
        scratch_shapes=[pltpu.VMEM((n_state, d + ML_EXT, d), F32), pltpu.VMEM((n_state, 1, 1), F32),
                        pltpu.VMEM((nb * nc, GATE_PAD, CHUNK), F32),
                        pltpu.VMEM((2, nb * nc, D_MODEL, CHUNK), F32)],
        compiler_params=_params(1),
        name="mlstm",
    )(*args)


def _merge_kernel(*refs, tiles, mod_index):
    n = len(tiles)
    (x_ref, m_ref, sgm_ref, sgd_ref), refs = refs[:4], refs[4:]
    parts = [refs[j * n:(j + 1) * n] for j in range(2)]
    wm_ref, wd_ref, wo_ref, o_ref, w_sc = refs[2 * n:]
    i = pl.program_id(0)

    @pl.when(i == 0)
    def _():
        for j, w_ref in enumerate((wm_ref, wd_ref, wo_ref)):
            w_sc[j] = w_ref[...].astype(BF16)

    def tile(hm_ref, att_ref):
        y = sgm_ref[...] * _dot(hm_ref[...], w_sc[0]) + sgd_ref[...] * _dot(att_ref[...], w_sc[1])
        o_ref[...] = x_ref[...] + _mod_reader(m_ref, mod_index(i * MERGE_TM))(5) * _dot(y, w_sc[2])

    starts = np.cumsum((0,) + tiles).tolist()
    for k in range(n):
        pl.when((i >= starts[k]) & (i < starts[k + 1]))(functools.partial(tile, *[p[k] for p in parts]))


def _merge(x, mods, mod_index, hm, att, sgm, sgd, wm, wd, wo):
    rows = x.shape[0]
    tiles = tuple(h.shape[0] // MERGE_TM for h in hm)
    assert sum(tiles) * MERGE_TM == rows
    tile = pl.BlockSpec((MERGE_TM, D_MODEL), lambda i: (i, 0))

    def part_spec(k):
        start = sum(tiles[:k])
        return pl.BlockSpec((MERGE_TM, D_MODEL), lambda i: (jnp.clip(i - start, 0, tiles[k] - 1), 0))

    part_specs = [part_spec(k) for k in range(len(tiles))]
    w_spec = _resident((D_MODEL, D_MODEL))
    return pl.pallas_call(
        functools.partial(_merge_kernel, tiles=tiles, mod_index=mod_index),
        grid=(rows // MERGE_TM,),
        in_specs=[tile, _resident(mods.shape), tile, tile]
                 + part_specs * 2 + [w_spec, w_spec, w_spec],
        out_specs=tile,
        out_shape=jax.ShapeDtypeStruct((rows, D_MODEL), F32),
        scratch_shapes=[pltpu.VMEM((3, D_MODEL, D_MODEL), BF16)],
        compiler_params=_params(1),
        name="branch_merge",
    )(x, mods, sgm, sgd, *hm, *att, wm, wd, wo)


def _rope_tables(seq_len):
    lane = np.arange(LANES)
    r = lane % ROPE_AXIS_DIM
    freqs = np.power(np.float32(ROPE_BASE), -(r % ROPE_HALF).astype(np.float32) / np.float32(ROPE_HALF))
    tok = np.arange(seq_len)
    by_row = (lane % DA_HEAD_DIM < ROPE_AXIS_DIM)[None, :]
    pos = np.where(by_row, (tok // GRID_W)[:, None], (tok % GRID_W)[:, None]).astype(np.float32)
    ang = pos * freqs[None, :]
    sign = np.where(r < ROPE_HALF, -1.0, 1.0).astype(np.float32)
    return jnp.asarray(np.cos(ang), F32), jnp.asarray(np.sin(ang) * sign[None, :], F32)


def kernel(x_prompt, x_sample, c, cache_k, cache_v, state_C, state_n, state_m, c_ctx, w_ada, b_ada, g_norm, ffn1_w1, ffn1_w3, ffn1_w2, ffn2_w1, ffn2_w3, ffn2_w2, w_in, b_gate, g_qn, g_kn, lam_q1, lam_k1, lam_q2, lam_k2, g_sub, g_mh, w_br_m, w_br_d, w_out):
    depth = w_ada.shape[0]
    assert depth == 1
    l = 0
    bp, tp, _ = x_prompt.shape
    bs, ts, _ = x_sample.shape
    past = cache_k.shape[2]
    lam_init = 0.8 - 0.6 * math.exp(-0.3 * l)

    cvecs = jnp.concatenate([c_ctx[None, :], c, jnp.zeros((MOD_ROWS - 1 - bs, D_MODEL), F32)], axis=0)
    mods = _mods(cvecs, w_ada[l], b_ada[l])

    group = np.arange(MXU_DIM) // DA_HEAD_DIM
    w = dict(
        g_norm=g_norm[l],
        ffn1_w1=ffn1_w1[l], ffn1_w3=ffn1_w3[l], ffn1_w2=ffn1_w2[l],
        ffn2_w1=ffn2_w1[l], ffn2_w3=ffn2_w3[l], ffn2_w2=ffn2_w2[l],
        w_in_t=w_in[l].T,
        b_gate=b_gate[l:l + 1],
        bd=jnp.asarray(group[:, None] == group[None, :], BF16),
        g_qn=g_qn[l:l + 1], g_kn=g_kn[l:l + 1],
        lam=(lam_q1[l:l + 1], lam_k1[l:l + 1], lam_q2[l:l + 1], lam_k2[l:l + 1]),
        g_sub=g_sub[l:l + 1], g_mh=g_mh[l],
        w_br_m=w_br_m[l], w_br_d=w_br_d[l], w_out=w_out[l],
    )

    n_ctx, n_lat = bp * tp, bs * ts
    mod_index = lambda r: jnp.where(r < n_ctx, 0, 1 + (r - n_ctx) // ts)
    seg_start = lambda j: pl.multiple_of(jnp.where(j < N_SEG, _seg_start(j), GATE_LO), N_GATE_COLS)
    x1, w["w_seg"] = _ffn((x_prompt.reshape(n_ctx, D_MODEL), x_sample.reshape(n_lat, D_MODEL)), (n_ctx + n_lat,),
                          mods, mod_index, w["g_norm"], w["ffn1_w1"], w["ffn1_w3"], w["ffn1_w2"], base=0,
                          side=(w["w_in_t"], N_SEG + 1, seg_start))

    mq, mk, mv, so, dq, sgm, sgd, gates, new_k_t, new_v, dk_lat, dv_lat = _proj(
        x1, n_ctx, tp, ts, mods, mod_index, w["g_norm"], w["w_seg"], w["b_gate"], w["bd"], w["g_qn"], w["g_kn"],
        _rope_tables(ts))
    att_p = _attn(dq, 0, new_k_t, new_v, None, w["lam"], w["g_sub"], bp, tp, lam_init)
    hm_p, new_c, new_n, new_m = _mlstm(mq, mk, mv, so, gates, 0, w["g_mh"], None, bp, tp, emit_state=True)

    cache = (cache_k[:, l].transpose(0, 2, 3, 4, 1).reshape(bs, D_MODEL, past),
             cache_v[:, l].reshape(bs, past * DA_HEADS, DA_V_DIM))
    att_s = _attn(dq, n_ctx, dk_lat, dv_lat, cache, w["lam"], w["g_sub"], bs, ts, lam_init)
    (hm_s,) = _mlstm(mq, mk, mv, so, gates, n_ctx, w["g_mh"], (state_C, state_n, state_m), bs, ts,
                     emit_state=False)

    x2 = _merge(x1, mods, mod_index, (hm_p, hm_s), (att_p, att_s), sgm, sgd, w["w_br_m"], w["w_br_d"], w["w_out"])
    xp, xs = _ffn((x2,), (n_ctx, n_lat), mods, mod_index, w["g_norm"], w["ffn2_w1"], w["ffn2_w3"], w["ffn2_w2"],
                  base=6)

    return (xp.reshape(bp, tp, D_MODEL), xs.reshape(bs, ts, D_MODEL),
            new_k_t.reshape(bp, DA_HEADS, 2, DA_HEAD_DIM, tp).transpose(0, 4, 1, 2, 3)[:, None],
            new_v.reshape(bp, 1, tp, DA_HEADS, DA_V_DIM),
            new_c, new_n, new_m)
```

```python
import functools
import math

import jax
import jax.numpy as jnp
import numpy as np
from jax import lax
from jax.experimental import pallas as pl
from jax.experimental.pallas import tpu as pltpu

F32 = jnp.float32
BF16 = jnp.bfloat16

D_MODEL = 1024
D_FF = 2816
N_MOD = 9
GRID_W = 64
ML_HEADS = 4
ML_HEAD_DIM = 256
DA_HEADS = 8
DA_HEAD_DIM = 64
DA_V_DIM = 128
N_GATE_COLS = 16
CHUNK = 256
ROPE_BASE = 10000.0
QK_LOG2_SCALE = DA_HEAD_DIM ** -0.5 * math.log2(math.e)
EPS = 1e-6

LANES = 128
SUBLANES = 8
GATE_PAD = LANES
MXU_DIM = 256
VMEM_LIMIT = 56 * 1024 * 1024

SIDE_PIECES = 8
MOD_ROWS = SUBLANES
FFN_TM = 512
FFN_TF = 256
PROJ_TM = 256
MERGE_TM = 512
ATT_TQ = 256
ATT_TK_MAX = 768
ATT_SHORT_SEQ_BATCH = 2


def _params(n_axes):
    return pltpu.CompilerParams(dimension_semantics=("arbitrary",) * n_axes,
                                vmem_limit_bytes=VMEM_LIMIT)


def _dot(a, b):
    return jnp.dot(a.astype(BF16), b.astype(BF16), preferred_element_type=F32)


def _dot_nt(a, b):
    return lax.dot_general(a.astype(BF16), b.astype(BF16), (((1,), (1,)), ((), ())),
                           preferred_element_type=F32)


def _sigmoid(x):
    return 1.0 / (1.0 + jnp.exp(-x))


def _log_sigmoid(x):
    return jnp.minimum(x, 0.0) - jnp.log1p(jnp.exp(-jnp.abs(x)))


def _modulated_norm(x, g, shift, scale):
    y = x * lax.rsqrt(jnp.mean(x * x, axis=-1, keepdims=True) + EPS) * g
    return y * (1.0 + scale) + shift


def _resident(shape):
    return pl.BlockSpec(shape, lambda *_: (0,) * len(shape), pipeline_mode=pl.Buffered(1))


MODS_TN = 1024
MODS_IN_FLIGHT = 4


def _mods_kernel(c_ref, w_hbm, b_ref, o_ref, w_sc, sem):
    n_chunks = o_ref.shape[1] // MODS_TN

    def copy(j):
        slot = j % MODS_IN_FLIGHT
        return pltpu.make_async_copy(w_hbm.at[:, j * MODS_TN:(j + 1) * MODS_TN], w_sc.at[slot], sem.at[slot])

    for j in range(min(MODS_IN_FLIGHT, n_chunks)):
        copy(j).start()
    c = c_ref[...]
    a = (c * _sigmoid(c)).astype(BF16)
    for j in range(n_chunks):
        cols = slice(j * MODS_TN, (j + 1) * MODS_TN)
        copy(j).wait()
        o_ref[:, cols] = _dot(a, w_sc[j % MODS_IN_FLIGHT]) + b_ref[:, cols]
        if j + MODS_IN_FLIGHT < n_chunks:
            copy(j + MODS_IN_FLIGHT).start()


def _mods(cvecs, w_ada, b_ada):
    n = N_MOD * D_MODEL
    assert n % MODS_TN == 0
    return pl.pallas_call(
        _mods_kernel,
        in_specs=[pl.BlockSpec(memory_space=pltpu.VMEM),
                  pl.BlockSpec(memory_space=pl.ANY),
                  pl.BlockSpec(memory_space=pltpu.VMEM)],
        out_specs=pl.BlockSpec(memory_space=pltpu.VMEM),
        out_shape=jax.ShapeDtypeStruct((MOD_ROWS, n), F32),
        scratch_shapes=[pltpu.VMEM((MODS_IN_FLIGHT, D_MODEL, MODS_TN), F32),
                        pltpu.SemaphoreType.DMA((MODS_IN_FLIGHT,))],
        compiler_params=pltpu.CompilerParams(vmem_limit_bytes=VMEM_LIMIT),
        name="adaln_mods",
    )(cvecs, w_ada, b_ada.reshape(1, n))


def _mod_reader(m_ref, row):
    return lambda j: m_ref[pl.ds(row, 1), j * D_MODEL:(j + 1) * D_MODEL]


def _ffn_kernel(*refs, base, mod_index, in_tiles, out_tiles, side_blocks):
    n_in, n_out = len(in_tiles), len(out_tiles)
    x_refs, refs = refs[:n_in], refs[n_in:]
    (m_ref, g_ref, w1_ref, w3_ref, w2_ref), refs = refs[:5], refs[5:]
    if side_blocks:
        side_in_ref, refs = refs[0], refs[1:]
        side_out_ref, refs = refs[n_out], refs[:n_out] + refs[n_out + 1:]
    o_refs, (w1_sc, w3_sc, w2_sc, hh_sc, acc_sc) = refs[:n_out], refs[n_out:]
    g = pl.program_id(0)
    nf = D_FF // FFN_TF
    row_tile = g - (nf - 1)
    mod = _mod_reader(m_ref, mod_index(jnp.maximum(row_tile, 0) * FFN_TM))
    gain = g_ref[base // 3:base // 3 + 1, :]
    norm = lambda x: _modulated_norm(x, gain, mod(base), mod(base + 1)).astype(BF16)
    finish = lambda x, acc: x + 0.5 * mod(base + 2) * acc

    def tile(hh, f):
        a = _dot(hh, w1_sc[f])
        b = _dot(hh, w3_sc[f])
        return _dot(a * _sigmoid(a) * b, w2_sc[f])

    @pl.when(g < nf)
    def _():
        w1_sc[g] = w1_ref[...].astype(BF16)
        w3_sc[g] = w3_ref[...].astype(BF16)
        w2_sc[g] = w2_ref[...].astype(BF16)

        @pl.when(g == 0)
        def _():
            hh_sc[...] = norm(x_refs[0][...])
            acc_sc[...] = jnp.zeros(acc_sc.shape, F32)

        acc_sc[...] += tile(hh_sc[...], g)

        @pl.when(g == nf - 1)
        def _():
            o_refs[0][...] = finish(x_refs[0][...], acc_sc[...])

    def full_tile(x_ref, o_ref):
        x = x_ref[...]
        hh = norm(x)
        acc = jnp.zeros(x.shape, F32)
        for f in range(nf):
            acc = acc + tile(hh, f)
            if side_blocks and f < SIDE_PIECES:
                rows = slice(f * (D_MODEL // SIDE_PIECES), (f + 1) * (D_MODEL // SIDE_PIECES))
                side_out_ref[rows, :] = side_in_ref[rows, :].astype(BF16)
        o_ref[...] = finish(x, acc)

    bounds = sorted(set(np.cumsum((0,) + in_tiles).tolist()) | set(np.cumsum((0,) + out_tiles).tolist()))
    for lo, hi in zip(bounds[:-1], bounds[1:]):
        k_in = int(np.searchsorted(np.cumsum(in_tiles), lo, side="right"))
        k_out = int(np.searchsorted(np.cumsum(out_tiles), lo, side="right"))
        pl.when((g >= nf) & (row_tile >= lo) & (row_tile < hi))(
            functools.partial(full_tile, x_refs[k_in], o_refs[k_out]))


def _ffn(xs, out_rows, mods, mod_index, g, w1, w3, w2, base, side=None):
    nf = D_FF // FFN_TF
    in_tiles = tuple(x.shape[0] // FFN_TM for x in xs)
    out_tiles = tuple(r // FFN_TM for r in out_rows)
    assert sum(in_tiles) == sum(out_tiles)
    side_in_specs, side_out_specs, side_out_shape, side_args, side_blocks = [], [], [], [], 0
    if side is not None:
        side_w, side_blocks, side_start = side
        assert side_blocks <= sum(in_tiles) - 1
        blk = lambda s: jnp.clip(s - nf, 0, side_blocks - 1)
        side_in_specs = [pl.BlockSpec((pl.Element(D_MODEL), pl.Element(D_MODEL)), lambda s: (side_start(blk(s)), 0))]
        side_out_specs = [pl.BlockSpec((None, D_MODEL, D_MODEL), lambda s: (blk(s), 0, 0))]
        side_out_shape = [jax.ShapeDtypeStruct((side_blocks, D_MODEL, D_MODEL), BF16)]
        side_args = [side_w]
    row_tile = lambda s: jnp.maximum(s - (nf - 1), 0)
    f_tile = lambda s: jnp.minimum(s, nf - 1)

    def part_spec(tiles, k):
        start = sum(tiles[:k])
        return pl.BlockSpec((FFN_TM, D_MODEL), lambda s: (jnp.clip(row_tile(s) - start, 0, tiles[k] - 1), 0))

    return pl.pallas_call(
        functools.partial(_ffn_kernel, base=base, mod_index=mod_index, in_tiles=in_tiles, out_tiles=out_tiles,
                          side_blocks=side_blocks),
        grid=(nf - 1 + sum(in_tiles),),
        in_specs=[part_spec(in_tiles, k) for k in range(len(xs))] + [
            _resident(mods.shape),
            _resident(g.shape),
            pl.BlockSpec((D_MODEL, FFN_TF), lambda s: (0, f_tile(s))),
            pl.BlockSpec((D_MODEL, FFN_TF), lambda s: (0, f_tile(s))),
            pl.BlockSpec((FFN_TF, D_MODEL), lambda s: (f_tile(s), 0))] + side_in_specs,
        out_specs=[part_spec(out_tiles, k) for k in range(len(out_rows))] + side_out_specs,
        out_shape=[jax.ShapeDtypeStruct((r, D_MODEL), F32) for r in out_rows] + side_out_shape,
        scratch_shapes=[pltpu.VMEM((nf, D_MODEL, FFN_TF), BF16), pltpu.VMEM((nf, D_MODEL, FFN_TF), BF16),
                        pltpu.VMEM((nf, FFN_TF, D_MODEL), BF16),
                        pltpu.VMEM((FFN_TM, D_MODEL), BF16), pltpu.VMEM((FFN_TM, D_MODEL), F32)],
        compiler_params=_params(1),
        name="ffn",
    )(*xs, mods, g, w1, w3, w2, *side_args)


def _group_norm64(x, bd, g):
    ss = _dot(x * x, bd)
    return x * lax.rsqrt(ss * (1.0 / DA_HEAD_DIM) + EPS) * g


ROPE_AXIS_DIM = DA_HEAD_DIM // 2
ROPE_HALF = ROPE_AXIS_DIM // 2


def _rope(x, cos, sin_signed):
    first = (lax.broadcasted_iota(jnp.int32, x.shape, 1) % ROPE_AXIS_DIM) < ROPE_HALF
    partner = jnp.where(first, pltpu.roll(x, LANES - ROPE_HALF, 1), pltpu.roll(x, ROPE_HALF, 1))
    return x * cos + partner * sin_signed


N_SEG = 9
PROJ_WARM = (N_SEG + 1) // 2
GATE_LO = 4 * D_MODEL


def _seg_start(s):
    return s * D_MODEL + N_GATE_COLS * (s >= 4)


def _proj_kernel(*refs, ctx_tiles, row_of_step):
    (x_ref, m_ref, g_ref, wt_ref, bg_ref, bd_ref, gq_ref, gk_ref, cos_ref, sin_ref), refs = refs[:10], refs[10:]
    (mq_ref, mk_ref, mv_ref, so_ref, dq_ref, sgm_ref, sgd_ref, gates_ref,
     dk_ctx_ref, dv_ctx_ref, dk_lat_ref, dv_lat_ref, w_sc, wg_sc, hh_sc) = refs
    step = pl.program_id(0)
    mod = _mod_reader(m_ref, row_of_step(step))
    norm = lambda: _modulated_norm(x_ref[...], g_ref[1:2, :], mod(3), mod(4)).astype(BF16)
    groups = MXU_DIM // DA_HEAD_DIM
    tiled = lambda ref, scale: jnp.concatenate([ref[...] * scale] * groups, axis=1)
    gq_row, gk_row = tiled(gq_ref, QK_LOG2_SCALE), tiled(gk_ref, 1.0)
    bias = jnp.concatenate([bg_ref[...], jnp.zeros((1, GATE_PAD - N_GATE_COLS), F32)], axis=1)

    def gates(hh):
        y = _dot_nt(hh, wg_sc[...]) + bias
        col = lax.broadcasted_iota(jnp.int32, y.shape, 1)
        gates_ref[...] = jnp.where((col // ML_HEADS) % 2 == 1, _log_sigmoid(y), y)

    def qk_segment(y, gain_row, o_ref, rope):
        bd = bd_ref[...]
        for c in range(D_MODEL // MXU_DIM):
            cols = slice(c * MXU_DIM, (c + 1) * MXU_DIM)
            z = _group_norm64(y[:, cols], bd, gain_row)
            if rope:
                z = jnp.concatenate(
                    [_rope(z[:, k * LANES:(k + 1) * LANES], cos_ref[...], sin_ref[...])
                     for k in range(MXU_DIM // LANES)], axis=1)
            o_ref[:, cols] = z.astype(o_ref.dtype)

    def cache_key_segment(hh):
        y_t = _dot_nt(w_sc[5], hh)
        z = y_t.reshape(D_MODEL // DA_HEAD_DIM, DA_HEAD_DIM, y_t.shape[1])
        ms = jnp.mean(z * z, axis=1, keepdims=True)
        gain = jnp.broadcast_to(gk_row[:, 0:LANES], (LANES, LANES)).T[0:DA_HEAD_DIM, :]
        gain = jnp.concatenate([gain] * (y_t.shape[1] // LANES), axis=1)
        dk_ctx_ref[...] = (z * lax.rsqrt(ms + EPS) * gain).reshape(y_t.shape)

    def segment(s, hh, ctx):
        if s in (0, 2):
            o_ref = {0: mq_ref, 2: mv_ref}[s]
            y_t = _dot_nt(w_sc[s], hh) * ((ML_HEAD_DIM ** -0.5) if s == 0 else 1.0)
            for j in range(PROJ_TM // CHUNK):
                o_ref[j] = y_t[:, j * CHUNK:(j + 1) * CHUNK].astype(o_ref.dtype)
            return
        if s == 5 and ctx:
            cache_key_segment(hh)
            return
        y = _dot_nt(hh, w_sc[s])
        if s == 4:
            qk_segment(y, gq_row, dq_ref, rope=not ctx)
        elif s == 5:
            qk_segment(y, gk_row, dk_lat_ref, rope=True)
        elif s in (3, 7, 8):
            {3: so_ref, 7: sgm_ref, 8: sgd_ref}[s][...] = _sigmoid(y)
        elif s == 6 and ctx:
            dv_ctx_ref[...] = y.reshape(y.shape[0], DA_HEADS, DA_V_DIM)
        elif s == 6:
            dv_lat_ref[...] = y.astype(dv_lat_ref.dtype)
        else:
            mk_ref[...] = y.astype(mk_ref.dtype)

    for w in range(PROJ_WARM):
        @pl.when(step == w)
        def _(w=w):
            if w == 0:
                hh_sc[...] = norm()
            for j in range(2):
                s = 2 * w + j
                if s < N_SEG:
                    w_sc[s] = wt_ref[j]
                    segment(s, hh_sc[...], ctx=True)
                else:
                    wg_sc[...] = jnp.zeros(wg_sc.shape, BF16)
                    wg_sc[0:N_GATE_COLS, :] = wt_ref[j, 0:N_GATE_COLS, :]
                    gates(hh_sc[...])

    def full_tile(ctx):
        hh = norm()
        gates(hh)
        for s in range(N_SEG):
            segment(s, hh, ctx)

    is_ctx = step - (PROJ_WARM - 1) < ctx_tiles
    pl.when((step >= PROJ_WARM) & is_ctx)(functools.partial(full_tile, True))
    pl.when((step >= PROJ_WARM) & jnp.logical_not(is_ctx))(functools.partial(full_tile, False))


def _proj(x, n_ctx, seq_ctx, seq_lat, mods, mod_index, g, w_seg, b_gate, bd, gq, gk, rope_tabs):
    rows = x.shape[0]
    n_lat = rows - n_ctx
    assert seq_ctx == PROJ_TM
    ctx_tiles, lat_tiles = n_ctx // PROJ_TM, n_lat // PROJ_TM
    row_tile = lambda s: jnp.maximum(s - (PROJ_WARM - 1), 0)
    ctx_tile = lambda s: jnp.minimum(row_tile(s), ctx_tiles - 1)
    lat_tile = lambda s: jnp.clip(row_tile(s) - ctx_tiles, 0, lat_tiles - 1)
    row = lambda s: (row_tile(s), 0)
    tile = pl.BlockSpec((PROJ_TM, D_MODEL), row)
    w_block = (2, D_MODEL, D_MODEL)
    tab = pl.BlockSpec((PROJ_TM, LANES), lambda s: (lat_tile(s) % (seq_lat // PROJ_TM), 0))
    in_specs = [tile,
                _resident(mods.shape),
                _resident(g.shape),
                pl.BlockSpec(w_block, lambda s: (jnp.minimum(s, PROJ_WARM - 1), 0, 0)),
                _resident(b_gate.shape),
                _resident((MXU_DIM, MXU_DIM)),
                _resident(gq.shape),
                _resident(gk.shape),
                tab, tab]
    slab = pl.BlockSpec((PROJ_TM // CHUNK, D_MODEL, CHUNK), lambda s: (row_tile(s), 0, 0))
    slab_shape = jax.ShapeDtypeStruct((rows // CHUNK, D_MODEL, CHUNK), BF16)
    full = lambda dt: jax.ShapeDtypeStruct((rows, D_MODEL), dt)
    out_specs = [slab, tile, slab, tile, tile, tile, tile, pl.BlockSpec((PROJ_TM, GATE_PAD), row),
                 pl.BlockSpec((None, D_MODEL, seq_ctx), lambda s: (ctx_tile(s), 0, 0)),
                 pl.BlockSpec((PROJ_TM, DA_HEADS, DA_V_DIM), lambda s: (ctx_tile(s), 0, 0)),
                 pl.BlockSpec((PROJ_TM, D_MODEL), lambda s: (lat_tile(s), 0)),
                 pl.BlockSpec((PROJ_TM, D_MODEL), lambda s: (lat_tile(s), 0))]
    out_shape = [slab_shape, full(BF16), slab_shape, full(F32), full(BF16), full(F32), full(F32),
                 jax.ShapeDtypeStruct((rows, GATE_PAD), F32),
                 jax.ShapeDtypeStruct((n_ctx // seq_ctx, D_MODEL, seq_ctx), F32),
                 jax.ShapeDtypeStruct((n_ctx, DA_HEADS, DA_V_DIM), F32),
                 jax.ShapeDtypeStruct((n_lat, D_MODEL), BF16),
                 jax.ShapeDtypeStruct((n_lat, D_MODEL), BF16)]
    return pl.pallas_call(
        functools.partial(_proj_kernel, ctx_tiles=ctx_tiles,
                          row_of_step=lambda step: mod_index(row_tile(step) * PROJ_TM)),
        grid=(PROJ_WARM - 1 + rows // PROJ_TM,),
        in_specs=in_specs,
        out_specs=out_specs,
        out_shape=out_shape,
        scratch_shapes=[pltpu.VMEM((N_SEG, D_MODEL, D_MODEL), BF16), pltpu.VMEM((GATE_PAD, D_MODEL), BF16),
                        pltpu.VMEM((PROJ_TM, D_MODEL), BF16)],
        compiler_params=_params(1),
        name="mixer_in_proj",
    )(x, mods, g, w_seg, b_gate, bd, gq, gk, *rope_tabs)


def _lambda(lam_refs, lam_init):
    q1, k1, q2, k2 = (r[...] for r in lam_refs)
    s1 = jnp.sum(q1 * k1, axis=1, keepdims=True)
    s2 = jnp.sum(q2 * k2, axis=1, keepdims=True)
    return jnp.exp(s1) - jnp.exp(s2) + lam_init


def _attn_kernel(*refs, cached, seq, nb, lam_init):
    if cached:
        (q_ref, k_ref, v_ref, ck_ref, cv_ref), refs = refs[:5], refs[5:]
    else:
        (q_ref, k_ref, v_ref), refs = refs[:3], refs[3:]
    lam_refs, (gs_ref, o_ref, kall_sc, vt_sc, s_sc) = refs[:4], refs[4:]
    units = [(bb, h) for bb in range(nb) for h in range(DA_HEADS)]

    def cache_order_kv(kt_ref, vr_ref, u, h, lo, n):
        kall_sc[u, lo:lo + n, :] = kt_ref[h * DA_V_DIM:(h + 1) * DA_V_DIM, :].T.astype(BF16)
        vt_sc[u, :, lo:lo + n] = vr_ref[pl.ds(h, n, stride=DA_HEADS), :].T.astype(BF16)

    @pl.when(pl.program_id(1) == 0)
    def _():
        for u, (bb, h) in enumerate(units):
            if cached:
                cols = slice(h * DA_V_DIM, (h + 1) * DA_V_DIM)
                kall_sc[u, 0:seq, :] = k_ref[:, cols].astype(BF16)
                vt_sc[u, :, 0:seq] = v_ref[:, cols].astype(F32).T.astype(BF16)
                cache_order_kv(ck_ref.at[bb], cv_ref.at[bb], u, h, seq, ck_ref.shape[2])
            else:
                cache_order_kv(k_ref.at[bb], v_ref.at[bb], u, h, 0, seq)

    lam = _lambda(lam_refs, lam_init)
    sub_gain = gs_ref[...] * (1.0 - lam_init)
    lane = lax.broadcasted_iota(jnp.int32, (1, DA_V_DIM), 1)
    comp_masks = [lane < DA_HEAD_DIM, lane >= DA_HEAD_DIM]
    tq = q_ref.shape[0] // nb
    n_keys = kall_sc.shape[1]
    n_tiles = pl.cdiv(n_keys, ATT_TK_MAX)
    tk = n_keys // n_tiles

    def stacked_q(u):
        bb, h = units[u]
        q = q_ref[bb * tq:(bb + 1) * tq, h * DA_V_DIM:(h + 1) * DA_V_DIM].astype(BF16)
        return jnp.concatenate([jnp.where(m, q, jnp.zeros_like(q)) for m in comp_masks], axis=0)

    def score_tile(h, j, qq, m8):
        rows = slice(j * tk, (j + 1) * tk)
        st = _dot_nt(kall_sc[h, rows, :], qq)
        s_sc[h % 2, rows, :] = st
        t8 = jnp.max(st.reshape(tk // SUBLANES, SUBLANES, 2 * tq), axis=0)
        return t8 if m8 is None else jnp.maximum(m8, t8)

    def prob_tile(h, j, mx, d8, pv):
        rows = slice(j * tk, (j + 1) * tk)
        e = jnp.exp2(s_sc[h % 2, rows, :] - mx)
        s8 = jnp.sum(e.reshape(tk // SUBLANES, SUBLANES, 2 * tq), axis=0)
        p = _dot(vt_sc[h, :, rows], e)
        return (s8 if d8 is None else d8 + s8), (p if pv is None else pv + p)

    qq = stacked_q(0)
    m8 = None
    for j in range(n_tiles):
        m8 = score_tile(0, j, qq, m8)
    for h in range(len(units)):
        mx = jnp.max(m8, axis=0, keepdims=True)
        if h + 1 < len(units):
            qq = stacked_q(h + 1)
        m8, d8, pv = None, None, None
        for j in range(n_tiles):
            if h + 1 < len(units):
                m8 = score_tile(h + 1, j, qq, m8)
            d8, pv = prob_tile(h, j, mx, d8, pv)
        inv = 1.0 / jnp.sum(d8, axis=0, keepdims=True)
        out_t = pv[:, :tq] * inv[:, :tq] - pv[:, tq:] * (lam * inv[:, tq:])
        out_t = out_t * lax.rsqrt(jnp.mean(out_t * out_t, axis=0, keepdims=True) + EPS)
        bb, head = units[h]
        o_ref[bb * tq:(bb + 1) * tq, head * DA_V_DIM:(head + 1) * DA_V_DIM] = (out_t.T * sub_gain).astype(o_ref.dtype)


def _attn(q, q_row0, k, v, cache, lam_vecs, g_sub, batch, seq_len, lam_init):
    rows = batch * seq_len
    nq = seq_len // ATT_TQ
    nb = ATT_SHORT_SEQ_BATCH if (cache is None and nq == 1) else 1
    assert batch % nb == 0
    q_spec = pl.BlockSpec((nb * ATT_TQ, D_MODEL), lambda b, i: (q_row0 // (nb * ATT_TQ) + b * nq + i, 0))
    o_spec = pl.BlockSpec((nb * ATT_TQ, D_MODEL), lambda b, i: (b * nq + i, 0))
    cache_specs = lambda n: [pl.BlockSpec((nb, D_MODEL, n), lambda b, i: (b, 0, 0)),
                             pl.BlockSpec((nb, n * DA_HEADS, DA_V_DIM), lambda b, i: (b, 0, 0))]
    n_keys = seq_len
    if cache is not None:
        past = cache[0].shape[2]
        n_keys += past
        kv_spec = pl.BlockSpec((seq_len, D_MODEL), lambda b, i: (b, 0))
        in_specs = [q_spec, kv_spec, kv_spec] + cache_specs(past)
        args = [q, k, v] + list(cache)
    else:
        in_specs = [q_spec] + cache_specs(seq_len)
        args = [q, k, v.reshape(batch, seq_len * DA_HEADS, DA_V_DIM)]
    in_specs += [_resident((1, DA_HEAD_DIM))] * 4 + [_resident((1, DA_V_DIM))]
    args += list(lam_vecs) + [g_sub]
    return pl.pallas_call(
        functools.partial(_attn_kernel, cached=cache is not None, seq=seq_len, nb=nb, lam_init=lam_init),
        grid=(batch // nb, nq),
        in_specs=in_specs,
        out_specs=o_spec,
        out_shape=jax.ShapeDtypeStruct((rows, D_MODEL), BF16),
        scratch_shapes=[pltpu.VMEM((nb * DA_HEADS, n_keys, DA_V_DIM), BF16),
                        pltpu.VMEM((nb * DA_HEADS, DA_V_DIM, n_keys), BF16),
                        pltpu.VMEM((2, n_keys, 2 * ATT_TQ), F32)],
        compiler_params=_params(2),
        name="diff_attention",
    )(*args)


def _per_chain(fn, a, b):
    return jnp.stack([fn(a[i], b[i]) for i in range(a.shape[0])])


ML_EXT = SUBLANES
ML_SHORT_SEQ_BATCH = 2


def _rows(x):
    return jnp.stack([x[i:i + 1, :] for i in range(x.shape[0])])


def _split3(x):
    hi = x.astype(BF16)
    r = x - hi.astype(F32)
    mid = r.astype(BF16)
    lo = (r - mid.astype(F32)).astype(BF16)
    return jnp.concatenate([hi, mid, lo], axis=1)


def _chunk_scan(x, reverse_rows, tri_prefix, tri_suffix):
    parts = _split3(x)
    return jnp.where(reverse_rows, _dot(parts, tri_suffix), _dot(parts, tri_prefix))


def _paired_value_matmul(v_t, sc):
    B, L, _ = sc.shape
    zero = jnp.zeros((L, L), sc.dtype)
    out = []
    for i in range(0, B, 2):
        lhs = jnp.concatenate([v_t[i], v_t[i + 1]], axis=1)
        rhs = jnp.concatenate([jnp.concatenate([sc[i], zero], axis=1),
                               jnp.concatenate([zero, sc[i + 1]], axis=1)], axis=0)
        both = _dot(lhs, rhs)
        out += [both[:, :L], both[:, L:]]
    return jnp.stack(out)


def _mlstm_step(k, q_t, v_t, ic, fc, CT, m, seen_t, reverse_rows):
    B, L = ic.shape
    d = k.shape[2]
    zero_state = CT is None
    if zero_state:
        m = jnp.zeros((B, 1, 1), F32)
    tri = lambda keep: jnp.concatenate([jnp.where(keep, 1.0, 0.0).astype(BF16)] * 3, axis=0)
    b2 = _chunk_scan(fc, reverse_rows, tri(seen_t[0]), tri(seen_t[B - 1]))
    u2 = ic - b2
    u_t = jnp.concatenate([u2, jnp.zeros((L - B, L), F32)], axis=0).T
    u_col = jnp.stack([jnp.broadcast_to(u_t[:, i:i + 1], (L, L)) for i in range(B)])
    b, i_g, f_g = _rows(b2), _rows(ic), _rows(fc)
    b_last = jnp.sum(f_g, axis=2, keepdims=True)

    log_d = jnp.where(seen_t, b + u_col, -jnp.inf)
    a = b + m
    m_t = jnp.maximum(a, jnp.max(log_d, axis=1, keepdims=True))
    dmat = jnp.exp(log_d - m_t)
    sc = _per_chain(_dot, k, q_t) * dmat
    value_matmul = _paired_value_matmul if 2 * L <= MXU_DIM else functools.partial(_per_chain, _dot)
    num = value_matmul(v_t, sc.astype(BF16))
    den = jnp.sum(sc, axis=1, keepdims=True)
    if not zero_state:
        inter = jnp.exp(a - m_t)
        cq = _per_chain(_dot, CT, q_t)
        num = num + inter * cq[:, :d, :]
        den = den + inter * cq[:, d:d + 1, :]
    h_t = num * (1.0 / jnp.maximum(jnp.abs(den), jnp.exp(-m_t)))

    g = b_last - b + i_g
    m_new = jnp.maximum(b_last + m, jnp.max(g, axis=2, keepdims=True))
    w = jnp.exp(g - m_new)
    decay = jnp.exp(b_last + m - m_new)
    vw = jnp.concatenate([v_t.astype(F32), jnp.ones((B, ML_EXT, L), F32)], axis=1) * w
    CT_new = _per_chain(_dot, vw, k)
    if not zero_state:
        CT_new = decay * CT + CT_new
    return h_t, CT_new, m_new


def _mlstm_kernel(*refs, seq_len, nb, has_state, emit_state):
    q_ref, k_ref, v_ref, so_ref, g_ref, gmh_ref = refs[:6]
    refs = refs[6:]
    if has_state:
        (c0_ref, n0_ref, m0_ref), refs = refs[:3], refs[3:]
    hm_ref, refs = refs[0], refs[1:]
    if emit_state:
        (c_out_ref, n_out_ref, m_out_ref), refs = refs[:3], refs[3:]
    ct_sc, m_sc, gr_sc, h_sc = refs

    nc = seq_len // CHUNK
    d_head = ML_HEAD_DIM
    chains = [(bb, d, h) for bb in range(nb) for d in range(2) for h in range(ML_HEADS)]
    n_chain = len(chains)
    for i, (bb, d, h) in enumerate(chains):
        if has_state:
            ct_sc[i, 0:d_head, :] = c0_ref[bb, d, h].T
            ct_sc[i, d_head:d_head + ML_EXT, :] = jnp.broadcast_to(n0_ref[bb, d, h:h + 1, :], (ML_EXT, d_head))
            m_sc[i] = jnp.full((1, 1), m0_ref[pl.program_id(0) * nb + bb, 0, d, h], F32)
        elif nc > 2:
            ct_sc[i] = jnp.zeros((d_head + ML_EXT, d_head), F32)
            m_sc[i] = jnp.zeros((1, 1), F32)
    for c in range(nb * nc):
        gr_sc[c] = g_ref[c * CHUNK:(c + 1) * CHUNK, :].T

    s_idx = lax.broadcasted_iota(jnp.int32, (CHUNK, CHUNK), 0)
    t_idx = lax.broadcasted_iota(jnp.int32, (CHUNK, CHUNK), 1)
    seen_t = jnp.stack([(s_idx >= t_idx) if d else (s_idx <= t_idx) for _, d, _ in chains])
    reverse_rows = (lax.broadcasted_iota(jnp.int32, (n_chain, CHUNK), 0) // ML_HEADS) % 2 == 1

    def step(c_fwd, c_bwd, rows_of, zero_state=False):
        chunk_of = (c_fwd, c_bwd)
        lo = 2 * ML_HEADS
        ic, fc = [], []
        for bb in range(nb):
            g_fwd, g_bwd = gr_sc[bb * nc + c_fwd], gr_sc[bb * nc + c_bwd]
            ic += [g_fwd[0:ML_HEADS], g_bwd[lo:lo + ML_HEADS]]
            fc += [g_fwd[ML_HEADS:lo], g_bwd[lo + ML_HEADS:2 * lo]]
        head = [slice(h * d_head, (h + 1) * d_head) for _, _, h in chains]
        stack = lambda pick: jnp.stack([pick(i, bb * nc + chunk_of[d]) for i, (bb, d, _) in enumerate(chains)])
        h_t, CT_new, m_new = _mlstm_step(
            stack(lambda i, c: k_ref[rows_of(c), head[i]]),
            stack(lambda i, c: q_ref[c, head[i], :]),
            stack(lambda i, c: v_ref[c, head[i], :]),
            jnp.concatenate(ic, axis=0), jnp.concatenate(fc, axis=0),
            None if zero_state else ct_sc[...], None if zero_state else m_sc[...], seen_t, reverse_rows)
        ct_sc[...] = CT_new
        m_sc[...] = m_new
        for i, (bb, d, _) in enumerate(chains):
            h_sc[d, bb * nc + chunk_of[d], head[i], :] = h_t[i]

    if nc <= 2:
        for c in range(nc):
            step(c, nc - 1 - c, lambda cc: slice(cc * CHUNK, (cc + 1) * CHUNK),
                 zero_state=(c == 0 and not has_state))
    else:
        def body(c, carry):
            step(c, nc - 1 - c, lambda cc: pl.ds(pl.multiple_of(cc * CHUNK, CHUNK), CHUNK))
            return carry
        lax.fori_loop(0, nc, body, 0, unroll=2)

    for h in range(ML_HEADS):
        hcols = slice(h * d_head, (h + 1) * d_head)
        gain = jnp.broadcast_to(gmh_ref[h:h + 1, :], (CHUNK, d_head)).T
        for c in range(nb * nc):
            rows = slice(c * CHUNK, (c + 1) * CHUNK)
            hsum = h_sc[0, c, hcols, :] + h_sc[1, c, hcols, :]
            hn = hsum * lax.rsqrt(jnp.mean(hsum * hsum, axis=0, keepdims=True) + EPS) * gain
            hm_ref[rows, hcols] = (hn.T * so_ref[rows, hcols]).astype(hm_ref.dtype)
    if emit_state:
        for i, (bb, d, h) in enumerate(chains):
            c_out_ref[bb, d, h] = ct_sc[i, 0:d_head, :].T
            n_out_ref[bb, d, h:h + 1, :] = ct_sc[i, d_head:d_head + 1, :]
            m_out_ref[bb, d:d + 1, h:h + 1] = m_sc[i]


def _mlstm(q_t, k, v_t, so, gates, row0, g_mh, state, batch, seq_len, emit_state):
    rows = batch * seq_len
    d = ML_HEAD_DIM
    nc = seq_len // CHUNK
    nb = ML_SHORT_SEQ_BATCH if nc <= 2 else 1
    assert batch % nb == 0
    assert row0 % (nb * seq_len) == 0
    first = row0 // (nb * seq_len)
    tile = pl.BlockSpec((nb * seq_len, D_MODEL), lambda b: (first + b, 0))
    tile_t = pl.BlockSpec((nb * nc, D_MODEL, CHUNK), lambda b: (first + b, 0, 0))
    in_specs = [tile_t, tile, tile_t, tile,
                pl.BlockSpec((nb * seq_len, GATE_PAD), lambda b: (first + b, 0)),
                _resident((ML_HEADS, d))]
    args = [q_t, k, v_t, so, gates, g_mh]
    c_spec = pl.BlockSpec((nb, None, 2, ML_HEADS, d, d), lambda b: (b, 0, 0, 0, 0, 0))
    n_spec = pl.BlockSpec((nb, None, 2, ML_HEADS, d), lambda b: (b, 0, 0, 0, 0))
    if state is not None:
        in_specs += [c_spec, n_spec, pl.BlockSpec(memory_space=pltpu.SMEM)]
        args += list(state)
    out_specs = [pl.BlockSpec((nb * seq_len, D_MODEL), lambda b: (b, 0))]
    out_shape = [jax.ShapeDtypeStruct((rows, D_MODEL), BF16)]
    if emit_state:
        out_specs += [c_spec, n_spec, pl.BlockSpec((nb, None, 2, ML_HEADS), lambda b: (b, 0, 0, 0))]
        out_shape += [jax.ShapeDtypeStruct((batch, 1, 2, ML_HEADS, d, d), F32),
                      jax.ShapeDtypeStruct((batch, 1, 2, ML_HEADS, d), F32),
                      jax.ShapeDtypeStruct((batch, 1, 2, ML_HEADS), F32)]
    n_state = 2 * ML_HEADS * nb
    return pl.pallas_call(
        functools.partial(_mlstm_kernel, seq_len=seq_len, nb=nb, has_state=state is not None,
                          emit_state=emit_state),
        grid=(batch // nb,),
        in_specs=in_specs,
        out_specs=out_specs,
        out_shape=out_shape,
        scratch_shapes=[pltpu.VMEM((n_state, d + ML_EXT, d), F32), pltpu.VMEM((n_state, 1, 1), F32),
                        pltpu.VMEM((nb * nc, GATE_PAD, CHUNK), F32),
                        pltpu.VMEM((2, nb * nc, D_MODEL, CHUNK), F32)],
        compiler_params=_params(1),
        name="mlstm",
    )(*args)


def _merge_kernel(*refs, tiles, mod_index):
    n = len(tiles)
    (x_ref, m_ref, sgm_ref, sgd_ref), refs = refs[:4], refs[4:]
    parts = [refs[j * n:(j + 1) * n] for j in range(2)]
    wm_ref, wd_ref, wo_ref, o_ref, w_sc = refs[2 * n:]
    i = pl.program_id(0)

    @pl.when(i == 0)
    def _():
        for j, w_ref in enumerate((wm_ref, wd_ref, wo_ref)):
            w_sc[j] = w_ref[...].astype(BF16)

    def tile(hm_ref, att_ref):
        y = sgm_ref[...] * _dot(hm_ref[...], w_sc[0]) + sgd_ref[...] * _dot(att_ref[...], w_sc[1])
        o_ref[...] = x_ref[...] + _mod_reader(m_ref, mod_index(i * MERGE_TM))(5) * _dot(y, w_sc[2])

    starts = np.cumsum((0,) + tiles).tolist()
    for k in range(n):
        pl.when((i >= starts[k]) & (i < starts[k + 1]))(functools.partial(tile, *[p[k] for p in parts]))


def _merge(x, mods, mod_index, hm, att, sgm, sgd, wm, wd, wo):
    rows = x.shape[0]
    tiles = tuple(h.shape[0] // MERGE_TM for h in hm)
    assert sum(tiles) * MERGE_TM == rows
    tile = pl.BlockSpec((MERGE_TM, D_MODEL), lambda i: (i, 0))

    def part_spec(k):
        start = sum(tiles[:k])
        return pl.BlockSpec((MERGE_TM, D_MODEL), lambda i: (jnp.clip(i - start, 0, tiles[k] - 1), 0))

    part_specs = [part_spec(k) for k in range(len(tiles))]
    w_spec = _resident((D_MODEL, D_MODEL))
    return pl.pallas_call(
        functools.partial(_merge_kernel, tiles=tiles, mod_index=mod_index),
        grid=(rows // MERGE_TM,),
        in_specs=[tile, _resident(mods.shape), tile, tile]
                 + part_specs * 2 + [w_spec, w_spec, w_spec],
        out_specs=tile,
        out_shape=jax.ShapeDtypeStruct((rows, D_MODEL), F32),
        scratch_shapes=[pltpu.VMEM((3, D_MODEL, D_MODEL), BF16)],
        compiler_params=_params(1),
        name="branch_merge",
    )(x, mods, sgm, sgd, *hm, *att, wm, wd, wo)


def _rope_tables(seq_len):
    lane = np.arange(LANES)
    r = lane % ROPE_AXIS_DIM
    freqs = np.power(np.float32(ROPE_BASE), -(r % ROPE_HALF).astype(np.float32) / np.float32(ROPE_HALF))
    tok = np.arange(seq_len)
    by_row = (lane % DA_HEAD_DIM < ROPE_AXIS_DIM)[None, :]
    pos = np.where(by_row, (tok // GRID_W)[:, None], (tok % GRID_W)[:, None]).astype(np.float32)
    ang = pos * freqs[None, :]
    sign = np.where(r < ROPE_HALF, -1.0, 1.0).astype(np.float32)
    return jnp.asarray(np.cos(ang), F32), jnp.asarray(np.sin(ang) * sign[None, :], F32)


def kernel(x_prompt, x_sample, c, cache_k, cache_v, state_C, state_n, state_m, c_ctx, w_ada, b_ada, g_norm, ffn1_w1, ffn1_w3, ffn1_w2, ffn2_w1, ffn2_w3, ffn2_w2, w_in, b_gate, g_qn, g_kn, lam_q1, lam_k1, lam_q2, lam_k2, g_sub, g_mh, w_br_m, w_br_d, w_out):
    depth = w_ada.shape[0]
    assert depth == 1
    l = 0
    bp, tp, _ = x_prompt.shape
    bs, ts, _ = x_sample.shape
    past = cache_k.shape[2]
    lam_init = 0.8 - 0.6 * math.exp(-0.3 * l)

    cvecs = jnp.concatenate([c_ctx[None, :], c, jnp.zeros((MOD_ROWS - 1 - bs, D_MODEL), F32)], axis=0)
    mods = _mods(cvecs, w_ada[l], b_ada[l])

    group = np.arange(MXU_DIM) // DA_HEAD_DIM
    w = dict(
        g_norm=g_norm[l],
        ffn1_w1=ffn1_w1[l], ffn1_w3=ffn1_w3[l], ffn1_w2=ffn1_w2[l],
        ffn2_w1=ffn2_w1[l], ffn2_w3=ffn2_w3[l], ffn2_w2=ffn2_w2[l],
        w_in_t=w_in[l].T,
        b_gate=b_gate[l:l + 1],
        bd=jnp.asarray(group[:, None] == group[None, :], BF16),
        g_qn=g_qn[l:l + 1], g_kn=g_kn[l:l + 1],
        lam=(lam_q1[l:l + 1], lam_k1[l:l + 1], lam_q2[l:l + 1], lam_k2[l:l + 1]),
        g_sub=g_sub[l:l + 1], g_mh=g_mh[l],
        w_br_m=w_br_m[l], w_br_d=w_br_d[l], w_out=w_out[l],
    )

    n_ctx, n_lat = bp * tp, bs * ts
    mod_index = lambda r: jnp.where(r < n_ctx, 0, 1 + (r - n_ctx) // ts)
    seg_start = lambda j: pl.multiple_of(jnp.where(j < N_SEG, _seg_start(j), GATE_LO), N_GATE_COLS)
    x1, w["w_seg"] = _ffn((x_prompt.reshape(n_ctx, D_MODEL), x_sample.reshape(n_lat, D_MODEL)), (n_ctx + n_lat,),
                          mods, mod_index, w["g_norm"], w["ffn1_w1"], w["ffn1_w3"], w["ffn1_w2"], base=0,
                          side=(w["w_in_t"], N_SEG + 1, seg_start))

    mq, mk, mv, so, dq, sgm, sgd, gates, new_k_t, new_v, dk_lat, dv_lat = _proj(
        x1, n_ctx, tp, ts, mods, mod_index, w["g_norm"], w["w_seg"], w["b_gate"], w["bd"], w["g_qn"], w["g_kn"],
        _rope_tables(ts))
    att_p = _attn(dq, 0, new_k_t, new_v, None, w["lam"], w["g_sub"], bp, tp, lam_init)
    hm_p, new_c, new_n, new_m = _mlstm(mq, mk, mv, so, gates, 0, w["g_mh"], None, bp, tp, emit_state=True)

    cache = (cache_k[:, l].transpose(0, 2, 3, 4, 1).reshape(bs, D_MODEL, past),
             cache_v[:, l].reshape(bs, past * DA_HEADS, DA_V_DIM))
    att_s = _attn(dq, n_ctx, dk_lat, dv_lat, cache, w["lam"], w["g_sub"], bs, ts, lam_init)
    (hm_s,) = _mlstm(mq, mk, mv, so, gates, n_ctx, w["g_mh"], (state_C, state_n, state_m), bs, ts,
                     emit_state=False)

    x2 = _merge(x1, mods, mod_index, (hm_p, hm_s), (att_p, att_s), sgm, sgd, w["w_br_m"], w["w_br_d"], w["w_out"])
    xp, xs = _ffn((x2,), (n_ctx, n_lat), mods, mod_index, w["g_norm"], w["ffn2_w1"], w["ffn2_w3"], w["ffn2_w2"],
                  base=6)

    return (xp.reshape(bp, tp, D_MODEL), xs.reshape(bs, ts, D_MODEL),
            new_k_t.reshape(bp, DA_HEADS, 2, DA_HEAD_DIM, tp).transpose(0, 4, 1, 2, 3)[:, None],
            new_v.reshape(bp, 1, tp, DA_HEADS, DA_V_DIM),
            new_c, new_n, new_m)
```
